```python
import math
import jax, jax.numpy as jnp
from jax import lax
import numpy as np

D_MODEL = 1024
BATCH = 1
SEQ = 16384
DEPTH = 2
DEC_BATCH = 4
DEC_SEQ = 8192
PAST_LEN = 128

HEAD_DIM = 64
GRID_W = 64
A_Q_HEADS = 8
A_KV_HEADS = 2
A_GROUP = A_Q_HEADS // A_KV_HEADS
A_HALF = 128
A_BLOCK = 128
B_PAIRS = ((128, 1), (512, 4), (2048, 16))
B_GROUPS = 3
B_HEADS = 4
B_BLOCK = 64
A_QW = A_Q_HEADS * HEAD_DIM
A_KVW = A_KV_HEADS * HEAD_DIM
A_IN = A_QW + 2 * A_KVW
B_IN = 3 * B_GROUPS * B_HEADS * HEAD_DIM
AB_IN = A_IN + B_IN
B_OUT = B_HEADS * HEAD_DIM
AB_OUT = A_QW + B_OUT
C_HEADS = 16
C_W = C_HEADS * HEAD_DIM
NA_KH = 8
NA_KW = 16
NA_CB = 16
N_EXPERTS = 32
TOP_K = 4
D_FF = 1024
SWIGLU_LIMIT = 7.0
SWIGLU_ALPHA = 1.702
MOE_BLOCK = 128
RMS_EPS = 1e-6
NEG_INF = -1e30

kernel_name = 'hybrid_bidir_alibi_natten_moe_encoder'


def rms_norm(x, g):
    x32 = x.astype(jnp.float32)
    y = x32 * lax.rsqrt(jnp.mean(x32 * x32, axis=-1, keepdims=True) + RMS_EPS)
    return (y * g.astype(jnp.float32)).astype(x.dtype)


def alibi_slopes(n):
    return jnp.asarray(2.0 ** (-8.0 * np.arange(1, n + 1) / n), dtype=jnp.float32)


def banded_attn(q, k, v, half, blk, slopes, step, sink):
    N, L, Hk, G, hd = q.shape
    nb = L // blk
    W = blk + 2 * half
    kp = jnp.pad(k, ((0, 0), (half, half), (0, 0), (0, 0)))
    vp = jnp.pad(v, ((0, 0), (half, half), (0, 0), (0, 0)))
    starts = np.arange(nb) * blk
    kidx = starts[:, None] + np.arange(W)[None, :]
    kb = kp[:, kidx]
    vb = vp[:, kidx]
    qb = q.reshape(N, nb, blk, Hk, G, hd)
    s = jnp.einsum('nbqhgd,nbkhd->nbhgqk', qb.astype(jnp.float32), kb.astype(jnp.float32)) * (hd ** -0.5)
    kpos = kidx - half
    qpos = starts[:, None] + np.arange(blk)[None, :]
    rel = kpos[:, None, :] - qpos[:, :, None]
    valid = (np.abs(rel) <= half) & (kpos >= 0)[:, None, :] & (kpos < L)[:, None, :]
    dist = jnp.asarray(step * np.abs(rel), dtype=jnp.float32)
    s = s - slopes.astype(jnp.float32)[None, None, :, :, None, None] * dist[None, :, None, None, :, :]
    s = jnp.where(jnp.asarray(valid)[None, :, None, None], s, NEG_INF)
    m = jnp.max(s, axis=-1, keepdims=True)
    if sink is not None:
        sink_b = sink.astype(jnp.float32)[None, None, :, :, None, None]
        m = jnp.maximum(m, sink_b)
    e = jnp.exp(s - m)
    den = jnp.sum(e, axis=-1, keepdims=True)
    if sink is not None:
        den = den + jnp.exp(sink_b - m)
    out = jnp.einsum('nbhgqk,nbkhd->nbqhgd', e / den, vb.astype(jnp.float32))
    out = out.reshape(N, L, Hk, G, hd).astype(q.dtype)
    lse = (m + jnp.log(den))[..., 0]
    lse = lse.transpose(0, 1, 4, 2, 3).reshape(N, L, Hk, G)
    return out, lse


def dilated_attn(q, k, v, dil, half, slopes):
    N, L, H, hd = q.shape
    Ld = L // dil

    def to_strided(t):
        return t.reshape(N, Ld, dil, H, hd).transpose(0, 2, 1, 3, 4).reshape(N * dil, Ld, H, hd)

    blk = math.gcd(Ld, B_BLOCK)
    o, lse = banded_attn(to_strided(q)[:, :, :, None], to_strided(k), to_strided(v),
                         half, blk, slopes[:, None], dil, None)
    o = o.reshape(N, dil, Ld, H, hd).transpose(0, 2, 1, 3, 4).reshape(N, L, H, hd)
    lse = lse.reshape(N, dil, Ld, H).transpose(0, 2, 1, 3).reshape(N, L, H)
    return o, lse


def mixer_ab(h, w_in, q_norm_a, k_norm_a, sink_a, q_norm_b, k_norm_b, w_out):
    N, L, _ = h.shape
    proj = h @ w_in
    qa = rms_norm(proj[..., :A_QW].reshape(N, L, A_KV_HEADS, A_GROUP, HEAD_DIM), q_norm_a)
    ka = rms_norm(proj[..., A_QW:A_QW + A_KVW].reshape(N, L, A_KV_HEADS, HEAD_DIM), k_norm_a)
    va = proj[..., A_QW + A_KVW:A_IN].reshape(N, L, A_KV_HEADS, HEAD_DIM)
    oa, _ = banded_attn(qa, ka, va, A_HALF, A_BLOCK,
                        alibi_slopes(A_Q_HEADS).reshape(A_KV_HEADS, A_GROUP), 1,
                        sink_a.reshape(A_KV_HEADS, A_GROUP))
    qkv_b = proj[..., A_IN:].reshape(N, L, 3, B_GROUPS, B_HEADS, HEAD_DIM)
    slopes_b = alibi_slopes(B_GROUPS * B_HEADS).reshape(B_GROUPS, B_HEADS)
    outs, lses = [], []
    for g, (window, dil) in enumerate(B_PAIRS):
        qg = rms_norm(qkv_b[:, :, 0, g], q_norm_b)
        kg = rms_norm(qkv_b[:, :, 1, g], k_norm_b)
        vg = qkv_b[:, :, 2, g]
        og, lg = dilated_attn(qg, kg, vg, dil, window // (2 * dil), slopes_b[g])
        outs.append(og.astype(jnp.float32))
        lses.append(lg)
    wts = jax.nn.softmax(jnp.stack(lses, axis=0), axis=0)
    ob = jnp.sum(wts[..., None] * jnp.stack(outs, axis=0), axis=0).astype(h.dtype)
    o = jnp.concatenate([oa.reshape(N, L, A_QW), ob.reshape(N, L, B_OUT)], axis=-1)
    return o @ w_out


def neighborhood_attn(q, k, v, rpb):
    N, L, H, hd = q.shape
    rows = L // GRID_W
    kh = min(NA_KH, rows)
    ncb = GRID_W // NA_CB
    kcw = NA_CB + NA_KW
    qc = np.arange(GRID_W).reshape(ncb, NA_CB)
    cs_q = np.clip(qc - NA_KW // 2, 0, GRID_W - NA_KW)
    cs_b = np.clip(np.arange(ncb) * NA_CB - NA_KW // 2, 0, GRID_W - kcw)
    kc = cs_b[:, None] + np.arange(kcw)[None, :]
    col_valid = jnp.asarray((kc[:, None, :] >= cs_q[:, :, None]) & (kc[:, None, :] < cs_q[:, :, None] + NA_KW))
    col_idx = np.clip(kc[:, None, :] - qc[:, :, None] + NA_KW - 1, 0, 2 * NA_KW - 2)
    rpb32 = rpb.astype(jnp.float32)
    q_rows = q.reshape(N, rows, ncb, NA_CB, H, hd).transpose(1, 0, 2, 3, 4, 5)

    def row_step(args):
        r, q_row = args
        rs = jnp.clip(r - kh // 2, 0, rows - kh)
        k_win = lax.dynamic_slice_in_dim(k, rs * GRID_W, kh * GRID_W, axis=1).reshape(N, kh, GRID_W, H, hd)[:, :, kc]
        v_win = lax.dynamic_slice_in_dim(v, rs * GRID_W, kh * GRID_W, axis=1).reshape(N, kh, GRID_W, H, hd)[:, :, kc]
        s = jnp.einsum('njqhd,nrjkhd->njhqrk', q_row.astype(jnp.float32), k_win.astype(jnp.float32)) * (hd ** -0.5)
        row_idx = rs + jnp.arange(kh) - r + NA_KH - 1
        bias = rpb32[:, row_idx][:, :, col_idx]
        s = s + bias.transpose(2, 0, 3, 1, 4)[None]
        s = jnp.where(col_valid[None, :, None, :, None, :], s, NEG_INF)
        p = jax.nn.softmax(s.reshape(N, ncb, H, NA_CB, kh * kcw), axis=-1).reshape(s.shape)
        o = jnp.einsum('njhqrk,nrjkhd->njqhd', p, v_win.astype(jnp.float32))
        return o.astype(q.dtype)

    o = lax.map(row_step, (jnp.arange(rows), q_rows))
    return o.transpose(1, 0, 2, 3, 4, 5).reshape(N, L, H, hd)


def mixer_c(h, w_in, q_norm_c, k_norm_c, rpb_c, w_out):
    N, L, _ = h.shape
    qkv = (h @ w_in).reshape(N, L, 3, C_HEADS, HEAD_DIM)
    q = rms_norm(qkv[:, :, 0], q_norm_c)
    k = rms_norm(qkv[:, :, 1], k_norm_c)
    o = neighborhood_attn(q, k, qkv[:, :, 2], rpb_c)
    return o.reshape(N, L, C_W) @ w_out


def moe_ffn(h, w_router, b_router, w_gate_up, b_gate_up, w_down, b_down):
    N, L, D = h.shape
    T = N * L
    xt = h.reshape(T, D)
    logits = (xt @ w_router + b_router).astype(jnp.float32)
    top_val, top_idx = lax.top_k(logits, TOP_K)
    gates = jax.nn.softmax(top_val, axis=-1)
    n_assign = T * TOP_K
    expert = top_idx.reshape(n_assign).astype(jnp.int32)
    token = jnp.arange(n_assign, dtype=jnp.int32) // TOP_K
    weight = gates.reshape(n_assign)
    order = jnp.argsort(expert)
    e_s, tok_s, w_s = expert[order], token[order], weight[order]
    counts = jnp.bincount(expert, length=N_EXPERTS)
    padded = (counts + MOE_BLOCK - 1) // MOE_BLOCK * MOE_BLOCK
    pad_end = jnp.cumsum(padded)
    pad_start = pad_end - padded
    srt_start = jnp.cumsum(counts) - counts
    slot = pad_start[e_s] + jnp.arange(n_assign, dtype=jnp.int32) - srt_start[e_s]
    n_blocks = -(-n_assign // MOE_BLOCK) + N_EXPERTS
    n_slots = n_blocks * MOE_BLOCK
    slot_tok = jnp.full((n_slots,), T, jnp.int32).at[slot].set(tok_s)
    slot_w = jnp.zeros((n_slots,), jnp.float32).at[slot].set(w_s)
    blk_expert = jnp.minimum(jnp.searchsorted(pad_end, jnp.arange(n_blocks, dtype=jnp.int32) * MOE_BLOCK, side='right'), N_EXPERTS - 1)
    x_pad = jnp.concatenate([xt, jnp.zeros((1, D), xt.dtype)], axis=0)

    def expert_block(args):
        tok_b, e_b, w_b = args
        xb = x_pad[tok_b]
        gu = xb @ w_gate_up[e_b] + b_gate_up[e_b]
        gate = jnp.minimum(gu[:, :D_FF], SWIGLU_LIMIT)
        up = jnp.clip(gu[:, D_FF:], -SWIGLU_LIMIT, SWIGLU_LIMIT)
        act = (up + 1) * gate * jax.nn.sigmoid(SWIGLU_ALPHA * gate)
        y = act @ w_down[e_b] + b_down[e_b]
        return y * w_b[:, None].astype(y.dtype)

    ys = lax.map(expert_block, (slot_tok.reshape(n_blocks, MOE_BLOCK), blk_expert,
                                slot_w.reshape(n_blocks, MOE_BLOCK)))
    out = jnp.zeros((T + 1, D), h.dtype).at[slot_tok].add(ys.reshape(n_slots, D).astype(h.dtype))
    return out[:T].reshape(N, L, D)


def trunk(x, c, layers):
    for i in range(DEPTH):
        ada_w, ada_b, norm1, mixer_fn, mixer_params, norm2, moe_params = layers[i]
        mod = jax.nn.silu(c) @ ada_w + ada_b
        shift1, scale1, gate1, shift2, scale2, gate2 = jnp.split(mod[:, None, :], 6, axis=-1)
        h = rms_norm(x, norm1) * (1 + scale1) + shift1
        x = x + gate1 * mixer_fn(h, *mixer_params)
        h = rms_norm(x, norm2) * (1 + scale2) + shift2
        x = x + gate2 * moe_ffn(h, *moe_params)
    return x


def setup_inputs(seed: int = 0) -> dict:
    key = jax.random.key(seed)
    ks = iter(jax.random.split(key, 64))

    def nrm(shape, scale):
        return scale * jax.random.normal(next(ks), shape, jnp.float32)

    def gain(n):
        return 1.0 + 0.02 * jax.random.normal(next(ks), (n,), jnp.float32)

    D = D_MODEL
    inp = {}
    inp['x_prompt'] = nrm((BATCH, SEQ, D), 1.0)
    inp['x_sample'] = nrm((DEC_BATCH, DEC_SEQ, D), 1.0)
    inp['c_prompt'] = nrm((BATCH, D), 1.0)
    inp['c_sample'] = nrm((DEC_BATCH, D), 1.0)

    def add_moe(p):
        inp[p + 'norm2'] = gain(D)
        inp[p + 'w_router'] = nrm((D, N_EXPERTS), D ** -0.5)
        inp[p + 'b_router'] = nrm((N_EXPERTS,), 0.01)
        inp[p + 'w_gate_up'] = nrm((N_EXPERTS, D, 2 * D_FF), D ** -0.5)
        inp[p + 'b_gate_up'] = nrm((N_EXPERTS, 2 * D_FF), 0.01)
        inp[p + 'w_down'] = nrm((N_EXPERTS, D_FF, D), D_FF ** -0.5)
        inp[p + 'b_down'] = nrm((N_EXPERTS, D), 0.01)

    inp['l0_ada_w'] = nrm((D, 6 * D), 0.5 * D ** -0.5)
    inp['l0_ada_b'] = nrm((6 * D,), 0.01)
    inp['l0_norm1'] = gain(D)
    inp['l0_w_in'] = nrm((D, AB_IN), D ** -0.5)
    inp['l0_q_norm_a'] = gain(HEAD_DIM)
    inp['l0_k_norm_a'] = gain(HEAD_DIM)
    inp['l0_sink_a'] = nrm((A_Q_HEADS,), 0.5)
    inp['l0_q_norm_b'] = gain(HEAD_DIM)
    inp['l0_k_norm_b'] = gain(HEAD_DIM)
    inp['l0_w_out'] = nrm((AB_OUT, D), AB_OUT ** -0.5)
    add_moe('l0_')
    inp['l1_ada_w'] = nrm((D, 6 * D), 0.5 * D ** -0.5)
    inp['l1_ada_b'] = nrm((6 * D,), 0.01)
    inp['l1_norm1'] = gain(D)
    inp['l1_w_in'] = nrm((D, 3 * C_W), D ** -0.5)
    inp['l1_q_norm_c'] = gain(HEAD_DIM)
    inp['l1_k_norm_c'] = gain(HEAD_DIM)
    inp['l1_rpb_c'] = nrm((C_HEADS, 2 * NA_KH - 1, 2 * NA_KW - 1), 0.1)
    inp['l1_w_out'] = nrm((C_W, D), C_W ** -0.5)
    add_moe('l1_')
    return inp


def reference(x_prompt, x_sample, c_prompt, c_sample,
              l0_ada_w, l0_ada_b, l0_norm1, l0_w_in, l0_q_norm_a, l0_k_norm_a, l0_sink_a,
              l0_q_norm_b, l0_k_norm_b, l0_w_out, l0_norm2, l0_w_router, l0_b_router,
              l0_w_gate_up, l0_b_gate_up, l0_w_down, l0_b_down,
              l1_ada_w, l1_ada_b, l1_norm1, l1_w_in, l1_q_norm_c, l1_k_norm_c, l1_rpb_c,
              l1_w_out, l1_norm2, l1_w_router, l1_b_router,
              l1_w_gate_up, l1_b_gate_up, l1_w_down, l1_b_down):
    layers = (
        (l0_ada_w, l0_ada_b, l0_norm1, mixer_ab,
         (l0_w_in, l0_q_norm_a, l0_k_norm_a, l0_sink_a, l0_q_norm_b, l0_k_norm_b, l0_w_out),
         l0_norm2,
         (l0_w_router, l0_b_router, l0_w_gate_up, l0_b_gate_up, l0_w_down, l0_b_down)),
        (l1_ada_w, l1_ada_b, l1_norm1, mixer_c,
         (l1_w_in, l1_q_norm_c, l1_k_norm_c, l1_rpb_c, l1_w_out),
         l1_norm2,
         (l1_w_router, l1_b_router, l1_w_gate_up, l1_b_gate_up, l1_w_down, l1_b_down)),
    )
    y_prompt = trunk(x_prompt, c_prompt, layers)
    y_sample = trunk(x_sample, c_sample, layers)
    return (y_prompt, y_sample)
```

```python
import functools
import math

import numpy as np
import jax
import jax.numpy as jnp
from jax import lax
from jax.experimental import pallas as pl
from jax.experimental.pallas import tpu as pltpu

F32 = jnp.float32
BF16 = jnp.bfloat16
HIGHEST = lax.Precision.HIGHEST

D_MODEL = 1024
HEAD_DIM = 64
LANES = 128
GRID_W = 64
A_Q_HEADS = 8
A_KV_HEADS = 2
A_GROUP = A_Q_HEADS // A_KV_HEADS
A_HALF = 128
B_PAIRS = ((128, 1), (512, 4), (2048, 16))
B_GROUPS = 3
B_HEADS = 4
A_QW = A_Q_HEADS * HEAD_DIM
A_KVW = A_KV_HEADS * HEAD_DIM
A_IN = A_QW + 2 * A_KVW
B_GW = B_HEADS * HEAD_DIM
C_HEADS = 16
C_W = C_HEADS * HEAD_DIM
NA_KH = 8
NA_KW = 16
N_EXPERTS = 32
TOP_K = 4
D_FF = 1024
SWIGLU_LIMIT = 7.0
SWIGLU_ALPHA = 1.702
RMS_EPS = 1e-6
NEG_INF = -1e30
QK_SCALE = HEAD_DIM ** -0.5

TOKEN_TILE = 512
ATTN_TQ = 128
NA_TROWS = 8
MOE_BM = 256
ROUTE_TILE = 512
MOVE_TILE = 128
VMEM_LIMIT = 56 * 1024 * 1024


def _alibi_slopes(n):
    return [float(2.0 ** (-8.0 * (j + 1) / n)) for j in range(n)]


def _cparams(sem):
    return pltpu.CompilerParams(dimension_semantics=sem, vmem_limit_bytes=VMEM_LIMIT)


def _mod_kernel(c_ref, w_ref, b_ref, o_ref):
    c = c_ref[...]
    s = c * jax.nn.sigmoid(c)
    o_ref[...] = jnp.dot(s, w_ref[...], precision=HIGHEST, preferred_element_type=F32) + b_ref[...]


def _ada_mod(c_pad, w, b):
    nrow = c_pad.shape[0]
    ncol = w.shape[1]
    tn = D_MODEL
    return pl.pallas_call(
        _mod_kernel,
        grid=(ncol // tn,),
        in_specs=[pl.BlockSpec((nrow, D_MODEL), lambda j: (0, 0)),
                  pl.BlockSpec((D_MODEL, tn), lambda j: (0, j)),
                  pl.BlockSpec((1, tn), lambda j: (0, j))],
        out_specs=pl.BlockSpec((nrow, tn), lambda j: (0, j)),
        out_shape=jax.ShapeDtypeStruct((nrow, ncol), F32),
        compiler_params=_cparams(("arbitrary",)),
        name="ada_mod",
    )(c_pad, w, b.reshape(1, ncol))


def _head_mean_sq(y, bd_ref):
    w = y.shape[1]
    outs = []
    for c0 in range(0, w, 2 * LANES):
        cw = min(2 * LANES, w - c0)
        sq = y[:, c0:c0 + cw] * y[:, c0:c0 + cw]
        hi = sq.astype(BF16)
        lo = (sq - hi.astype(F32)).astype(BF16)
        bd = bd_ref[0:cw, 0:cw]
        outs.append(jnp.dot(hi, bd, preferred_element_type=F32)
                    + jnp.dot(lo, bd, preferred_element_type=F32))
    return outs[0] if len(outs) == 1 else jnp.concatenate(outs, axis=1)


def _rms_mod(x, g_ref, mod_ref, shift_row, scale_row):
    ms = jnp.mean(x * x, axis=-1, keepdims=True)
    xn = x * lax.rsqrt(ms + RMS_EPS) * g_ref[...]
    return xn * (1.0 + mod_ref[scale_row:scale_row + 1, :]) + mod_ref[shift_row:shift_row + 1, :]


def _in_kernel(sid_ref, x_ref, mod_ref, n1_ref, w_ref, gain_ref, bd_ref, *rest, slabs, tm):
    del sid_ref
    nout = len(slabs)
    out_refs = rest[:nout]
    scr_ref = rest[nout] if len(rest) > nout else None
    h = _rms_mod(x_ref[...], n1_ref, mod_ref, 0, 1).astype(BF16)
    for (c0, w, normed, dil), o_ref in zip(slabs, out_refs):
        y = jnp.dot(h, w_ref[:, c0:c0 + w], preferred_element_type=F32)
        if normed:
            y = y * lax.rsqrt(_head_mean_sq(y, bd_ref) + RMS_EPS) * gain_ref[:, c0:c0 + w]
        if dil == 1:
            o_ref[...] = y.astype(BF16)
        else:
            for c in range(w // LANES):
                scr_ref[c] = y[:, c * LANES:(c + 1) * LANES]
            for r in range(dil):
                for c in range(w // LANES):
                    o_ref[:, r * w + c * LANES:r * w + (c + 1) * LANES] = (
                        scr_ref[c, pl.ds(r, tm // dil, stride=dil), :].astype(BF16))


def _in_proj(x, mod, tile_sid, n1, w_bf, gain, bd, slabs):
    t = x.shape[0]
    tm = TOKEN_TILE
    ncols = w_bf.shape[1]
    out_shapes, out_specs = [], []
    for (c0, w, normed, dil) in slabs:
        out_shapes.append(jax.ShapeDtypeStruct((t // dil, dil * w), BF16))
        out_specs.append(pl.BlockSpec((tm // dil, dil * w), lambda i, sid: (i, 0)))
    need_scr = any(s[3] > 1 for s in slabs)
    scratch = [pltpu.VMEM((max(s[1] for s in slabs if s[3] > 1) // LANES, tm, LANES), F32)] if need_scr else []
    gs = pltpu.PrefetchScalarGridSpec(
        num_scalar_prefetch=1,
        grid=(t // tm,),
        in_specs=[pl.BlockSpec((tm, D_MODEL), lambda i, sid: (i, 0)),
                  pl.BlockSpec((None, 6, D_MODEL), lambda i, sid: (sid[i], 0, 0)),
                  pl.BlockSpec((1, D_MODEL), lambda i, sid: (0, 0)),
                  pl.BlockSpec((D_MODEL, ncols), lambda i, sid: (0, 0)),
                  pl.BlockSpec((1, ncols), lambda i, sid: (0, 0)),
                  pl.BlockSpec((2 * LANES, 2 * LANES), lambda i, sid: (0, 0))],
        out_specs=out_specs,
        scratch_shapes=scratch)
    return pl.pallas_call(
        functools.partial(_in_kernel, slabs=slabs, tm=tm),
        grid_spec=gs,
        out_shape=out_shapes,
        compiler_params=_cparams(("arbitrary",)),
        name="in_proj",
    )(tile_sid, x, mod, n1.reshape(1, D_MODEL), w_bf, gain, bd)


def _band_kernel(s0_ref, s1_ref, *refs, tq, hb, half, heads, kv_group, has_sink, want_lse):
    pos = 0
    sink_ref = None
    if has_sink:
        sink_ref = refs[0]
        pos = 1
    q_ref, kp_ref, kc_ref, kn_ref, vp_ref, vc_ref, vn_ref = refs[pos:pos + 7]
    o_ref = refs[pos + 7]
    lse_ref = refs[pos + 8] if want_lse else None
    i = pl.program_id(1)
    wlen = tq + 2 * hb
    kwin = jnp.concatenate([kp_ref[...], kc_ref[...], kn_ref[...]], axis=0)
    vwin = jnp.concatenate([vp_ref[...], vc_ref[...], vn_ref[...]], axis=0)
    q0 = i * tq
    kpos = q0 - hb + lax.broadcasted_iota(jnp.int32, (tq, wlen), 1)
    qpos = q0 + lax.broadcasted_iota(jnp.int32, (tq, wlen), 0)
    rel = kpos - qpos
    valid = (jnp.abs(rel) <= half) & (kpos >= s0_ref[i]) & (kpos < s1_ref[i])
    distm = jnp.where(valid, jnp.abs(rel).astype(F32), 1e32)
    lane = lax.broadcasted_iota(jnp.int32, (tq, LANES), 1)
    lo_lanes = lane < HEAD_DIM
    for g, pair in enumerate(heads):
        qg = q_ref[:, g * LANES:(g + 1) * LANES]
        kg = kwin[:, kv_group[g] * LANES:(kv_group[g] + 1) * LANES]
        vg = vwin[:, kv_group[g] * LANES:(kv_group[g] + 1) * LANES]
        outs, lses = [], []
        for s, (slope, sink_idx) in enumerate(pair):
            keep = lo_lanes if s == 0 else jnp.logical_not(lo_lanes)
            qm = jnp.where(keep, qg, jnp.zeros_like(qg))
            sc = lax.dot_general(qm, kg, (((1,), (1,)), ((), ())), preferred_element_type=F32)
            sc = sc - slope * distm
            m = jnp.max(sc, axis=-1, keepdims=True)
            if has_sink:
                m = jnp.maximum(m, sink_ref[sink_idx])
            e = jnp.exp(sc - m)
            den = jnp.sum(e, axis=-1, keepdims=True)
            if has_sink:
                den = den + jnp.exp(sink_ref[sink_idx] - m)
            pv = jnp.dot(e.astype(BF16), vg, preferred_element_type=F32)
            outs.append(pv / den)
            if want_lse:
                lses.append(jnp.broadcast_to(m + jnp.log(den), (tq, LANES)))
        o_ref[:, g * LANES:(g + 1) * LANES] = jnp.where(lo_lanes, outs[0], outs[1]).astype(o_ref.dtype)
        if want_lse:
            lse_ref[:, g * LANES:(g + 1) * LANES] = jnp.where(lo_lanes, lses[0], lses[1])


def _band_attn(q, k, v, s0, s1, *, dil, qw, kvw, half, heads, kv_group, sink, out_dtype, want_lse):
    rows = q.shape[0]
    tq, hb = ATTN_TQ, half
    per = tq // hb
    nt = rows // tq
    nhb = rows // hb
    has_sink = sink is not None
    in_specs = []
    args = []
    if has_sink:
        in_specs.append(pl.BlockSpec(memory_space=pltpu.SMEM))
        args.append(sink)
    qmap = lambda r, i, a, b: (i, r)
    pmap = lambda r, i, a, b: (jnp.maximum(i * per - 1, 0), r)
    nmap = lambda r, i, a, b: (jnp.minimum((i + 1) * per, nhb - 1), r)
    in_specs += [pl.BlockSpec((tq, qw), qmap),
                 pl.BlockSpec((hb, kvw), pmap), pl.BlockSpec((tq, kvw), qmap), pl.BlockSpec((hb, kvw), nmap),
                 pl.BlockSpec((hb, kvw), pmap), pl.BlockSpec((tq, kvw), qmap), pl.BlockSpec((hb, kvw), nmap)]
    args += [q, k, k, k, v, v, v]
    out_shape = [jax.ShapeDtypeStruct((rows, dil * qw), out_dtype)]
    out_specs = [pl.BlockSpec((tq, qw), qmap)]
    if want_lse:
        out_shape.append(jax.ShapeDtypeStruct((rows, dil * qw), F32))
        out_specs.append(pl.BlockSpec((tq, qw), qmap))
    gs = pltpu.PrefetchScalarGridSpec(num_scalar_prefetch=2, grid=(dil, nt),
                                      in_specs=in_specs, out_specs=out_specs)
    res = pl.pallas_call(
        functools.partial(_band_kernel, tq=tq, hb=hb, half=half, heads=heads, kv_group=kv_group,
                          has_sink=has_sink, want_lse=want_lse),
        grid_spec=gs,
        out_shape=out_shape,
        compiler_params=_cparams(("arbitrary", "arbitrary")),
        name="band_attn_d%d" % dil,
    )(s0, s1, *args)
    return res


def _na_kernel(r0_ref, nr_ref, q_ref, kp_ref, kc_ref, kn_ref, vp_ref, vc_ref, vn_ref, bias_ref,
               o_ref, kcat, vcat, *, halo):
    j = pl.program_id(0)
    tq = NA_TROWS * GRID_W
    hrows = halo * GRID_W
    kcat[0:hrows, :] = kp_ref[...]
    kcat[hrows:hrows + tq, :] = kc_ref[...]
    kcat[hrows + tq:, :] = kn_ref[...]
    vcat[0:hrows, :] = vp_ref[...]
    vcat[hrows:hrows + tq, :] = vc_ref[...]
    vcat[hrows + tq:, :] = vn_ref[...]
    r0 = r0_ref[j]
    nrows = nr_ref[j]
    kwin_len = NA_KH * GRID_W
    lane = lax.broadcasted_iota(jnp.int32, (GRID_W, LANES), 1)
    lo_lanes = lane < HEAD_DIM
    nvar = 2 * NA_KH - 2

    def row_body(u, carry):
        r = r0 + u
        rs = jnp.clip(r - NA_KH // 2, 0, nrows - NA_KH)
        off = pl.multiple_of((rs - r0 + halo) * GRID_W, GRID_W)
        bvar = rs - r + NA_KH - 1
        qrow = pl.multiple_of(u * GRID_W, GRID_W)
        for g in range(C_HEADS // 2):
            qg = q_ref[pl.ds(qrow, GRID_W), g * LANES:(g + 1) * LANES]
            qs = jnp.concatenate([jnp.where(lo_lanes, qg, jnp.zeros_like(qg)),
                                  jnp.where(lo_lanes, jnp.zeros_like(qg), qg)], axis=0)
            kg = kcat[pl.ds(off, kwin_len), g * LANES:(g + 1) * LANES]
            vg = vcat[pl.ds(off, kwin_len), g * LANES:(g + 1) * LANES]
            sc = lax.dot_general(qs, kg, (((1,), (1,)), ((), ())), preferred_element_type=F32)
            bias = []
            for s in range(2):
                hbase = (2 * g + s) * nvar + bvar
                bias.append(jnp.concatenate([bias_ref[hbase + 2 * jj] for jj in range(NA_KH // 2)], axis=1))
            sc = sc + jnp.concatenate(bias, axis=0)
            m = jnp.max(sc, axis=-1, keepdims=True)
            e = jnp.exp(sc - m)
            den = jnp.sum(e, axis=-1, keepdims=True)
            pv = jnp.dot(e.astype(BF16), vg, preferred_element_type=F32) / den
            o = jnp.where(lo_lanes, pv[0:GRID_W], pv[GRID_W:])
            o_ref[pl.ds(qrow, GRID_W), g * LANES:(g + 1) * LANES] = o.astype(o_ref.dtype)
        return carry

    lax.fori_loop(0, NA_TROWS, row_body, 0)


def _na_attn(q, k, v, bias_tab, tile_r0, tile_nr):
    t = q.shape[0]
    halo = NA_KH // 2
    tq = NA_TROWS * GRID_W
    hrows = halo * GRID_W
    per = tq // hrows
    nhb = t // hrows
    nt = t // tq
    qmap = lambda j, a, b: (j, 0)
    pmap = lambda j, a, b: (jnp.maximum(j * per - 1, 0), 0)
    nmap = lambda j, a, b: (jnp.minimum((j + 1) * per, nhb - 1), 0)
    gs = pltpu.PrefetchScalarGridSpec(
        num_scalar_prefetch=2, grid=(nt,),
        in_specs=[pl.BlockSpec((tq, C_W), qmap),
                  pl.BlockSpec((hrows, C_W), pmap), pl.BlockSpec((tq, C_W), qmap), pl.BlockSpec((hrows, C_W), nmap),
                  pl.BlockSpec((hrows, C_W), pmap), pl.BlockSpec((tq, C_W), qmap), pl.BlockSpec((hrows, C_W), nmap),
                  pl.BlockSpec(bias_tab.shape, lambda j, a, b: (0, 0, 0))],
        out_specs=pl.BlockSpec((tq, C_W), qmap),
        scratch_shapes=[pltpu.VMEM((tq + 2 * hrows, C_W), BF16), pltpu.VMEM((tq + 2 * hrows, C_W), BF16)])
    return pl.pallas_call(
        functools.partial(_na_kernel, halo=halo),
        grid_spec=gs,
        out_shape=jax.ShapeDtypeStruct((t, C_W), BF16),
        compiler_params=_cparams(("arbitrary",)),
        name="na_attn",
    )(tile_r0, tile_nr, q, k, k, k, v, v, v, bias_tab)


def _na_bias_table(rpb):
    c = np.arange(GRID_W)
    cs = np.clip(c - NA_KW // 2, 0, GRID_W - NA_KW)
    kc = np.arange(GRID_W)
    valid = (kc[None, :] >= cs[:, None]) & (kc[None, :] < cs[:, None] + NA_KW)
    cidx = np.clip(kc[None, :] - c[:, None] + NA_KW - 1, 0, 2 * NA_KW - 2)
    tab = rpb.astype(F32)[:, :, cidx]
    tab = jnp.where(jnp.asarray(valid)[None, None], tab, NEG_INF)
    pairs = jnp.concatenate([tab[:, :-1], tab[:, 1:]], axis=-1)
    return pairs.reshape(C_HEADS * (2 * NA_KH - 2), GRID_W, LANES)


def _route(h2, wr_ref, br_ref, idx_ref, gate_ref):
    tm = h2.shape[0]
    logits = lax.dot_general(wr_ref[...], h2, (((1,), (1,)), ((), ())),
                             precision=HIGHEST, preferred_element_type=F32) + br_ref[:, 0:1]
    eid = lax.broadcasted_iota(jnp.int32, (N_EXPERTS, tm), 0)
    vals, idxs = [], []
    for _ in range(TOP_K):
        m = jnp.max(logits, axis=0, keepdims=True)
        ix = jnp.min(jnp.where(logits == m, eid, N_EXPERTS), axis=0, keepdims=True)
        vals.append(m)
        idxs.append(ix)
        logits = jnp.where(eid == ix, -jnp.inf, logits)
    es = [jnp.exp(vk - vals[0]) for vk in vals]
    den = es[0] + es[1] + es[2] + es[3]
    pad_i = jnp.zeros((8 - TOP_K, tm), jnp.int32)
    pad_f = jnp.zeros((8 - TOP_K, tm), F32)
    idx_ref[...] = jnp.concatenate(idxs + [pad_i], axis=0)
    gate_ref[...] = jnp.concatenate([ek / den for ek in es] + [pad_f], axis=0)


def _out0_kernel(sid_ref, x_ref, mod_ref, n2_ref, oa_ref, ob0_ref, ls0_ref, ob1_ref, ls1_ref, ob2_ref, ls2_ref,
                 wout_ref, wr_ref, br_ref, x1_ref, h2_ref, idx_ref, gate_ref, oscr, lscr, *, tm):
    del sid_ref
    os_, ls_ = [], []
    for gi, (o_ref, l_ref) in enumerate(((ob0_ref, ls0_ref), (ob1_ref, ls1_ref), (ob2_ref, ls2_ref))):
        dil = B_PAIRS[gi][1]
        if dil == 1:
            os_.append(o_ref[...])
            ls_.append(l_ref[...])
        else:
            ncg = B_GW // LANES
            for r in range(dil):
                for c in range(ncg):
                    lo = r * B_GW + c * LANES
                    oscr[gi * ncg + c, pl.ds(r, tm // dil, stride=dil), :] = o_ref[:, lo:lo + LANES]
                    lscr[gi * ncg + c, pl.ds(r, tm // dil, stride=dil), :] = l_ref[:, lo:lo + LANES]
            os_.append(jnp.concatenate([oscr[gi * ncg + c] for c in range(ncg)], axis=1))
            ls_.append(jnp.concatenate([lscr[gi * ncg + c] for c in range(ncg)], axis=1))
    lmax = jnp.maximum(jnp.maximum(ls_[0], ls_[1]), ls_[2])
    ws = [jnp.exp(l - lmax) for l in ls_]
    ob = (ws[0] * os_[0] + ws[1] * os_[1] + ws[2] * os_[2]) / (ws[0] + ws[1] + ws[2])
    o = (jnp.dot(oa_ref[...], wout_ref[0:A_QW, :], preferred_element_type=F32)
         + jnp.dot(ob.astype(BF16), wout_ref[A_QW:, :], preferred_element_type=F32))
    x1 = x_ref[...] + mod_ref[2:3, :] * o
    x1_ref[...] = x1
    h2 = _rms_mod(x1, n2_ref, mod_ref, 3, 4)
    h2_ref[...] = h2
    _route(h2, wr_ref, br_ref, idx_ref, gate_ref)


def _out1_kernel(sid_ref, x_ref, mod_ref, n2_ref, oc_ref, wout_ref, wr_ref, br_ref,
                 x1_ref, h2_ref, idx_ref, gate_ref):
    del sid_ref
    o = jnp.dot(oc_ref[...], wout_ref[...], preferred_element_type=F32)
    x1 = x_ref[...] + mod_ref[2:3, :] * o
    x1_ref[...] = x1
    h2 = _rms_mod(x1, n2_ref, mod_ref, 3, 4)
    h2_ref[...] = h2
    _route(h2, wr_ref, br_ref, idx_ref, gate_ref)


def _out_proj(layer, x, mod, tile_sid, n2, attn, wout_bf, wr_t, br):
    t = x.shape[0]
    tm = TOKEN_TILE
    row = lambda i, sid: (i, 0)
    const = lambda i, sid: (0, 0)
    in_specs = [pl.BlockSpec((tm, D_MODEL), row),
                pl.BlockSpec((None, 6, D_MODEL), lambda i, sid: (sid[i], 0, 0)),
                pl.BlockSpec((1, D_MODEL), const)]
    scratch = []
    if layer == 0:
        in_specs.append(pl.BlockSpec((tm, A_QW), row))
        for (_, dil) in B_PAIRS:
            in_specs += [pl.BlockSpec((tm // dil, dil * B_GW), row)] * 2
        body = functools.partial(_out0_kernel, tm=tm)
        nscr = B_GROUPS * B_GW // LANES
        scratch = [pltpu.VMEM((nscr, tm, LANES), F32), pltpu.VMEM((nscr, tm, LANES), F32)]
    else:
        in_specs.append(pl.BlockSpec((tm, C_W), row))
        body = _out1_kernel
    in_specs += [pl.BlockSpec(wout_bf.shape, const),
                 pl.BlockSpec((N_EXPERTS, D_MODEL), const),
                 pl.BlockSpec((N_EXPERTS, LANES), const)]
    gs = pltpu.PrefetchScalarGridSpec(
        num_scalar_prefetch=1, grid=(t // tm,), in_specs=in_specs,
        out_specs=[pl.BlockSpec((tm, D_MODEL), row), pl.BlockSpec((tm, D_MODEL), row),
                   pl.BlockSpec((8, tm), lambda i, sid: (0, i)), pl.BlockSpec((8, tm), lambda i, sid: (0, i))],
        scratch_shapes=scratch)
    return pl.pallas_call(
        body, grid_spec=gs,
        out_shape=[jax.ShapeDtypeStruct((t, D_MODEL), F32), jax.ShapeDtypeStruct((t, D_MODEL), F32),
                   jax.ShapeDtypeStruct((8, t), jnp.int32), jax.ShapeDtypeStruct((8, t), F32)],
        compiler_params=_cparams(("arbitrary",)),
        name="out_proj%d" % layer,
    )(tile_sid, x, mod, n2.reshape(1, D_MODEL), *attn, wout_bf, wr_t, br)


def _rank_kernel(idx_ref, tri_ref, rank_ref, cnt_ref, carry):
    i = pl.program_id(0)

    @pl.when(i == 0)
    def _():
        carry[...] = jnp.zeros_like(carry)

    tk = idx_ref.shape[1]
    eid = lax.broadcasted_iota(jnp.int32, (N_EXPERTS, tk), 0)
    hits = [eid == idx_ref[k:k + 1, :] for k in range(TOP_K)]
    onehot = sum(h.astype(F32) for h in hits)
    incl = jnp.dot(onehot.astype(BF16), tri_ref[...], preferred_element_type=F32)
    before = incl - onehot + carry[:, 0:1]
    rows = [jnp.sum(jnp.where(h, before, 0.0), axis=0, keepdims=True) for h in hits]
    rows.append(jnp.zeros((8 - TOP_K, tk), F32))
    rank_ref[...] = jnp.concatenate(rows, axis=0).astype(jnp.int32)
    carry[...] = carry[...] + incl[:, tk - 1:tk]
    cnt_ref[...] = carry[...].astype(jnp.int32)


def _slot_kernel(idx_ref, rank_ref, start_ref, slot_ref):
    tk = idx_ref.shape[1]
    eid = lax.broadcasted_iota(jnp.int32, (N_EXPERTS, tk), 0)
    start = start_ref[:, 0:1]
    rows = []
    for k in range(TOP_K):
        base = jnp.sum(jnp.where(eid == idx_ref[k:k + 1, :], start, 0), axis=0, keepdims=True)
        rows.append(base + rank_ref[k:k + 1, :])
    rows.append(jnp.zeros((8 - TOP_K, tk), jnp.int32))
    slot_ref[...] = jnp.concatenate(rows, axis=0)


def _route_slots(idx):
    t = idx.shape[1]
    tk = ROUTE_TILE
    tri = jnp.asarray(np.triu(np.ones((tk, tk), np.float32)), BF16)
    tok = lambda i: (0, i)
    rank, cnt = pl.pallas_call(
        _rank_kernel, grid=(t // tk,),
        in_specs=[pl.BlockSpec((8, tk), tok), pl.BlockSpec((tk, tk), lambda i: (0, 0))],
        out_specs=[pl.BlockSpec((8, tk), tok), pl.BlockSpec((N_EXPERTS, LANES), lambda i: (0, 0))],
        out_shape=[jax.ShapeDtypeStruct((8, t), jnp.int32), jax.ShapeDtypeStruct((N_EXPERTS, LANES), jnp.int32)],
        scratch_shapes=[pltpu.VMEM((N_EXPERTS, LANES), F32)],
        compiler_params=_cparams(("arbitrary",)),
        name="route_rank",
    )(idx, tri)
    counts = cnt[:, 0]
    padded = (counts + MOE_BM - 1) // MOE_BM * MOE_BM
    pad_end = jnp.cumsum(padded)
    pad_start = pad_end - padded
    nblk = (t * TOP_K) // MOE_BM + N_EXPERTS
    blk_expert = jnp.minimum(
        jnp.searchsorted(pad_end, jnp.arange(nblk, dtype=jnp.int32) * MOE_BM, side="right"),
        N_EXPERTS - 1).astype(jnp.int32)
    nused = (pad_end[-1] // MOE_BM).astype(jnp.int32).reshape(1)
    start_b = jnp.broadcast_to(pad_start.astype(jnp.int32)[:, None], (N_EXPERTS, LANES))
    slot = pl.pallas_call(
        _slot_kernel, grid=(t // tk,),
        in_specs=[pl.BlockSpec((8, tk), tok), pl.BlockSpec((8, tk), tok),
                  pl.BlockSpec((N_EXPERTS, LANES), lambda i: (0, 0))],
        out_specs=pl.BlockSpec((8, tk), tok),
        out_shape=jax.ShapeDtypeStruct((8, t), jnp.int32),
        compiler_params=_cparams(("arbitrary",)),
        name="route_slot",
    )(idx, rank, start_b)
    return slot, blk_expert, nused, pad_end.astype(jnp.int32), padded.astype(jnp.int32)


def _row_copy(src, dst, sem):
    return pltpu.make_async_copy(src, dst, sem)


def _dispatch_kernel(pend_ref, padded_ref, slot_ref, h_ref, xs_ref, zbuf, sem, zsem, *, tm, nblk):
    i = pl.program_id(0)

    @pl.when(i == 0)
    def _():
        zbuf[...] = jnp.zeros_like(zbuf)

        def zero_block(b):
            cp = _row_copy(zbuf, xs_ref.at[pl.ds(pl.multiple_of(b * MOE_BM, MOE_BM), MOE_BM)], zsem)
            cp.start()
            cp.wait()

        for e in range(N_EXPERTS):
            @pl.when(padded_ref[e] > 0)
            def _():
                zero_block(pend_ref[e] // MOE_BM - 1)

        def tail(b, carry):
            zero_block(b)
            return carry

        lax.fori_loop(pend_ref[N_EXPERTS - 1] // MOE_BM, nblk, tail, 0)

    def issue(tk, carry):
        for k in range(TOP_K):
            s = slot_ref[k, tk]
            _row_copy(h_ref.at[pl.ds(tk, 1)], xs_ref.at[pl.ds(s, 1)], sem).start()
        return carry

    lax.fori_loop(0, tm, issue, 0)

    def drain(tk, carry):
        for k in range(TOP_K):
            _row_copy(h_ref.at[pl.ds(0, 1)], xs_ref.at[pl.ds(0, 1)], sem).wait()
        return carry

    lax.fori_loop(0, tm, drain, 0)


def _dispatch(h2, slot, pad_end, padded, nslots):
    t = h2.shape[0]
    tm = MOVE_TILE
    gs = pltpu.PrefetchScalarGridSpec(
        num_scalar_prefetch=2, grid=(t // tm,),
        in_specs=[pl.BlockSpec((8, tm), lambda i, a, b: (0, i), memory_space=pltpu.SMEM),
                  pl.BlockSpec((tm, D_MODEL), lambda i, a, b: (i, 0))],
        out_specs=pl.BlockSpec(memory_space=pl.ANY),
        scratch_shapes=[pltpu.VMEM((MOE_BM, D_MODEL), F32), pltpu.SemaphoreType.DMA, pltpu.SemaphoreType.DMA])
    return pl.pallas_call(
        functools.partial(_dispatch_kernel, tm=tm, nblk=nslots // MOE_BM), grid_spec=gs,
        out_shape=jax.ShapeDtypeStruct((nslots, D_MODEL), F32),
        compiler_params=_cparams(("arbitrary",)),
        name="moe_dispatch",
    )(pad_end, padded, slot, h2)


def _ffn_kernel(be_ref, nu_ref, xs_ref, wgu_ref, bgu_ref, wd_ref, bd_ref, y_ref, wgu_bf, wd_bf):
    i = pl.program_id(0)
    prev = be_ref[jnp.maximum(i - 1, 0)]

    @pl.when(i < nu_ref[0])
    def _():
        @pl.when((i == 0) | (be_ref[i] != prev))
        def _():
            wgu_bf[...] = wgu_ref[...].astype(BF16)
            wd_bf[...] = wd_ref[...].astype(BF16)

        x = xs_ref[...].astype(BF16)
        gu = jnp.dot(x, wgu_bf[...], preferred_element_type=F32) + bgu_ref[...]
        gate = jnp.minimum(gu[:, :D_FF], SWIGLU_LIMIT)
        up = jnp.clip(gu[:, D_FF:], -SWIGLU_LIMIT, SWIGLU_LIMIT)
        act = (up + 1.0) * gate * jax.nn.sigmoid(SWIGLU_ALPHA * gate)
        y_ref[...] = jnp.dot(act.astype(BF16), wd_bf[...], preferred_element_type=F32) + bd_ref[...]

    @pl.when(i >= nu_ref[0])
    def _():
        y_ref[...] = jnp.zeros_like(y_ref)


def _expert_ffn(xs, blk_expert, nused, wgu, bgu, wd, bd):
    nslots = xs.shape[0]
    nblk = nslots // MOE_BM
    blk = lambda i, be, nu: (jnp.minimum(i, nu[0] - 1), 0)
    exp3 = lambda i, be, nu: (be[jnp.minimum(i, nu[0] - 1)], 0, 0)
    gs = pltpu.PrefetchScalarGridSpec(
        num_scalar_prefetch=2, grid=(nblk,),
        in_specs=[pl.BlockSpec((MOE_BM, D_MODEL), blk),
                  pl.BlockSpec((None, D_MODEL, 2 * D_FF), exp3),
                  pl.BlockSpec((None, 1, 2 * D_FF), exp3),
                  pl.BlockSpec((None, D_FF, D_MODEL), exp3),
                  pl.BlockSpec((None, 1, D_MODEL), exp3)],
        out_specs=pl.BlockSpec((MOE_BM, D_MODEL), lambda i, be, nu: (i, 0)),
        scratch_shapes=[pltpu.VMEM((D_MODEL, 2 * D_FF), BF16), pltpu.VMEM((D_FF, D_MODEL), BF16)])
    return pl.pallas_call(
        _ffn_kernel, grid_spec=gs,
        out_shape=jax.ShapeDtypeStruct((nslots, D_MODEL), F32),
        compiler_params=_cparams(("arbitrary",)),
        name="moe_ffn",
    )(blk_expert, nused, xs, wgu, bgu.reshape(N_EXPERTS, 1, 2 * D_FF), wd, bd.reshape(N_EXPERTS, 1, D_MODEL))


def _combine_kernel(sid_ref, slot_ref, x_ref, mod_ref, gate_ref, ys_ref, o_ref, buf, sem, *, tm):
    del sid_ref

    def issue(tk, carry):
        for k in range(TOP_K):
            s = slot_ref[k, tk]
            _row_copy(ys_ref.at[pl.ds(s, 1)], buf.at[k, pl.ds(tk, 1)], sem).start()
        return carry

    lax.fori_loop(0, tm, issue, 0)
    gpad = jnp.concatenate([gate_ref[...], jnp.zeros((LANES - 8, tm), F32)], axis=0)
    gcol = gpad.T

    def drain(tk, carry):
        for k in range(TOP_K):
            _row_copy(ys_ref.at[pl.ds(0, 1)], buf.at[0, pl.ds(0, 1)], sem).wait()
        return carry

    lax.fori_loop(0, tm, drain, 0)
    acc = gcol[:, 0:1] * buf[0]
    for k in range(1, TOP_K):
        acc = acc + gcol[:, k:k + 1] * buf[k]
    o_ref[...] = x_ref[...] + mod_ref[5:6, :] * acc


def _combine(x1, mod, tile_sid, slot, gates, ys):
    t = x1.shape[0]
    tm = MOVE_TILE
    gs = pltpu.PrefetchScalarGridSpec(
        num_scalar_prefetch=1, grid=(t // tm,),
        in_specs=[pl.BlockSpec((8, tm), lambda i, sid: (0, i), memory_space=pltpu.SMEM),
                  pl.BlockSpec((tm, D_MODEL), lambda i, sid: (i, 0)),
                  pl.BlockSpec((None, 6, D_MODEL), lambda i, sid: (sid[i], 0, 0)),
                  pl.BlockSpec((8, tm), lambda i, sid: (0, i)),
                  pl.BlockSpec(memory_space=pl.ANY)],
        out_specs=pl.BlockSpec((tm, D_MODEL), lambda i, sid: (i, 0)),
        scratch_shapes=[pltpu.VMEM((TOP_K, tm, D_MODEL), F32), pltpu.SemaphoreType.DMA])
    return pl.pallas_call(
        functools.partial(_combine_kernel, tm=tm), grid_spec=gs,
        out_shape=jax.ShapeDtypeStruct((t, D_MODEL), F32),
        compiler_params=_cparams(("arbitrary",)),
        name="moe_combine",
    )(tile_sid, slot, x1, mod, gates, ys)


def _moe(x1, h2, idx, gates, mod, tile_sid_move, wgu, bgu, wd, bd):
    t = x1.shape[0]
    nslots = t * TOP_K + N_EXPERTS * MOE_BM
    slot, blk_expert, nused, pad_end, padded = _route_slots(idx)
    xs = _dispatch(h2, slot, pad_end, padded, nslots)
    ys = _expert_ffn(xs, blk_expert, nused, wgu, bgu, wd, bd)
    return _combine(x1, mod, tile_sid_move, slot, gates, ys)


def _tile_table(seq_lens, tile, fn):
    vals = []
    start = 0
    for sid, n in enumerate(seq_lens):
        assert n % tile == 0
        for j in range(n // tile):
            vals.append(fn(sid, start, n, j))
        start += n
    return jnp.asarray(np.asarray(vals, np.int32))


def _head_block_diag():
    bd = np.kron(np.eye(2 * LANES // HEAD_DIM, dtype=np.float32),
                 np.full((HEAD_DIM, HEAD_DIM), 1.0 / HEAD_DIM, np.float32))
    return jnp.asarray(bd, BF16)


_A_HEAD_ORDER = [kv * A_GROUP + g for g in range(A_GROUP) for kv in range(A_KV_HEADS)]


def _trunk(x, c_pad, seq_lens, p):
    t = x.shape[0]
    sid_tok = _tile_table(seq_lens, TOKEN_TILE, lambda sid, s, n, j: sid)
    sid_move = _tile_table(seq_lens, MOVE_TILE, lambda sid, s, n, j: sid)
    bd = _head_block_diag()

    mod0 = _ada_mod(c_pad, p["l0_ada_w"], p["l0_ada_b"]).reshape(c_pad.shape[0], 6, D_MODEL)
    w_in = p["l0_w_in"]
    a_cols = np.concatenate([np.arange(h * HEAD_DIM, (h + 1) * HEAD_DIM) for h in _A_HEAD_ORDER])
    bq, bk, bv = A_IN, A_IN + B_GROUPS * B_GW, A_IN + 2 * B_GROUPS * B_GW
    col_blocks = [w_in[:, a_cols], w_in[:, A_QW:A_IN]]
    for g in range(B_GROUPS):
        col_blocks += [w_in[:, bq + g * B_GW:bq + (g + 1) * B_GW],
                       w_in[:, bk + g * B_GW:bk + (g + 1) * B_GW],
                       w_in[:, bv + g * B_GW:bv + (g + 1) * B_GW]]
    w0 = jnp.concatenate(col_blocks, axis=1).astype(BF16)
    ones = lambda n: jnp.ones((n,), F32)
    gain_blocks = [jnp.tile(p["l0_q_norm_a"], A_Q_HEADS) * QK_SCALE, jnp.tile(p["l0_k_norm_a"], A_KV_HEADS),
                   ones(A_KVW)]
    for g in range(B_GROUPS):
        gain_blocks += [jnp.tile(p["l0_q_norm_b"], B_HEADS) * QK_SCALE, jnp.tile(p["l0_k_norm_b"], B_HEADS),
                        ones(B_GW)]
    gain0 = jnp.concatenate(gain_blocks).reshape(1, -1)
    slabs0 = [(0, A_QW, True, 1), (A_QW, A_KVW, True, 1), (A_QW + A_KVW, A_KVW, False, 1)]
    c0 = A_IN
    for g, (_, dil) in enumerate(B_PAIRS):
        slabs0 += [(c0, B_GW, True, dil), (c0 + B_GW, B_GW, True, dil), (c0 + 2 * B_GW, B_GW, False, dil)]
        c0 += 3 * B_GW
    proj = _in_proj(x, mod0, sid_tok, p["l0_norm1"], w0, gain0, bd, tuple(slabs0))
    qa, ka, va = proj[0:3]

    slopes_a = _alibi_slopes(A_Q_HEADS)
    heads_a = tuple(tuple((slopes_a[kv * A_GROUP + g], kv * A_GROUP + g) for kv in range(A_KV_HEADS))
                    for g in range(A_GROUP))
    s0 = _tile_table(seq_lens, ATTN_TQ, lambda sid, s, n, j: s)
    s1 = _tile_table(seq_lens, ATTN_TQ, lambda sid, s, n, j: s + n)
    (oa,) = _band_attn(qa, ka, va, s0, s1, dil=1, qw=A_QW, kvw=A_KVW, half=A_HALF, heads=heads_a,
                       kv_group=(0,) * A_GROUP, sink=p["l0_sink_a"].astype(F32), out_dtype=BF16, want_lse=False)

    slopes_b = _alibi_slopes(B_GROUPS * B_HEADS)
    attn0 = [oa]
    for g, (window, dil) in enumerate(B_PAIRS):
        qg, kg, vg = proj[3 + 3 * g:6 + 3 * g]
        heads_b = tuple(tuple((slopes_b[g * B_HEADS + 2 * pr + s] * dil, 0) for s in range(2))
                        for pr in range(B_HEADS // 2))
        sd0 = _tile_table([n // dil for n in seq_lens], ATTN_TQ, lambda sid, s, n, j: s)
        sd1 = _tile_table([n // dil for n in seq_lens], ATTN_TQ, lambda sid, s, n, j: s + n)
        og, lg = _band_attn(qg, kg, vg, sd0, sd1, dil=dil, qw=B_GW, kvw=B_GW, half=window // (2 * dil),
                            heads=heads_b, kv_group=tuple(range(B_HEADS // 2)), sink=None, out_dtype=F32,
                            want_lse=True)
        attn0 += [og, lg]

    w_out = p["l0_w_out"]
    wout0 = jnp.concatenate([w_out[a_cols], w_out[A_QW:]], axis=0).astype(BF16)
    br0 = jnp.broadcast_to(p["l0_b_router"].astype(F32)[:, None], (N_EXPERTS, LANES))
    x1, h2, idx, gates = _out_proj(0, x, mod0, sid_tok, p["l0_norm2"], attn0, wout0, p["l0_w_router"].T, br0)
    x = _moe(x1, h2, idx, gates, mod0, sid_move, p["l0_w_gate_up"], p["l0_b_gate_up"], p["l0_w_down"],
             p["l0_b_down"])

    mod1 = _ada_mod(c_pad, p["l1_ada_w"], p["l1_ada_b"]).reshape(c_pad.shape[0], 6, D_MODEL)
    w1 = p["l1_w_in"].astype(BF16)
    gain1 = jnp.concatenate([jnp.tile(p["l1_q_norm_c"], C_HEADS) * QK_SCALE, jnp.tile(p["l1_k_norm_c"], C_HEADS),
                             ones(C_W)]).reshape(1, -1)
    slabs1 = ((0, C_W, True, 1), (C_W, C_W, True, 1), (2 * C_W, C_W, False, 1))
    qc, kc, vc = _in_proj(x, mod1, sid_tok, p["l1_norm1"], w1, gain1, bd, slabs1)
    na_tile = NA_TROWS * GRID_W
    tile_r0 = _tile_table(seq_lens, na_tile, lambda sid, s, n, j: j * NA_TROWS)
    tile_nr = _tile_table(seq_lens, na_tile, lambda sid, s, n, j: n // GRID_W)
    oc = _na_attn(qc, kc, vc, _na_bias_table(p["l1_rpb_c"]), tile_r0, tile_nr)
    br1 = jnp.broadcast_to(p["l1_b_router"].astype(F32)[:, None], (N_EXPERTS, LANES))
    x1, h2, idx, gates = _out_proj(1, x, mod1, sid_tok, p["l1_norm2"], [oc], p["l1_w_out"].astype(BF16),
                                   p["l1_w_router"].T, br1)
    return _moe(x1, h2, idx, gates, mod1, sid_move, p["l1_w_gate_up"], p["l1_b_gate_up"], p["l1_w_down"],
                p["l1_b_down"])


def kernel(x_prompt, x_sample, c_prompt, c_sample, l0_ada_w, l0_ada_b, l0_norm1, l0_w_in, l0_q_norm_a, l0_k_norm_a, l0_sink_a, l0_q_norm_b, l0_k_norm_b, l0_w_out, l0_norm2, l0_w_router, l0_b_router, l0_w_gate_up, l0_b_gate_up, l0_w_down, l0_b_down, l1_ada_w, l1_ada_b, l1_norm1, l1_w_in, l1_q_norm_c, l1_k_norm_c, l1_rpb_c, l1_w_out, l1_norm2, l1_w_router, l1_b_router, l1_w_gate_up, l1_b_gate_up, l1_w_down, l1_b_down):
    p = dict(l0_ada_w=l0_ada_w, l0_ada_b=l0_ada_b, l0_norm1=l0_norm1, l0_w_in=l0_w_in, l0_q_norm_a=l0_q_norm_a,
             l0_k_norm_a=l0_k_norm_a, l0_sink_a=l0_sink_a, l0_q_norm_b=l0_q_norm_b, l0_k_norm_b=l0_k_norm_b,
             l0_w_out=l0_w_out, l0_norm2=l0_norm2, l0_w_router=l0_w_router, l0_b_router=l0_b_router,
             l0_w_gate_up=l0_w_gate_up, l0_b_gate_up=l0_b_gate_up, l0_w_down=l0_w_down, l0_b_down=l0_b_down,
             l1_ada_w=l1_ada_w, l1_ada_b=l1_ada_b, l1_norm1=l1_norm1, l1_w_in=l1_w_in, l1_q_norm_c=l1_q_norm_c,
             l1_k_norm_c=l1_k_norm_c, l1_rpb_c=l1_rpb_c, l1_w_out=l1_w_out, l1_norm2=l1_norm2,
             l1_w_router=l1_w_router, l1_b_router=l1_b_router, l1_w_gate_up=l1_w_gate_up,
             l1_b_gate_up=l1_b_gate_up, l1_w_down=l1_w_down, l1_b_down=l1_b_down)
    nb_p, len_p, d = x_prompt.shape
    nb_s, len_s, _ = x_sample.shape
    seq_lens = (len_p,) * nb_p + (len_s,) * nb_s
    x = jnp.concatenate([x_prompt.reshape(nb_p * len_p, d), x_sample.reshape(nb_s * len_s, d)], axis=0)
    nseq = nb_p + nb_s
    c_pad = jnp.concatenate([c_prompt, c_sample, jnp.zeros((-nseq % 8, d), F32)], axis=0)
    y = _trunk(x, c_pad, seq_lens, p)
    y_prompt = y[:nb_p * len_p].reshape(nb_p, len_p, d)
    y_sample = y[nb_p * len_p:].reshape(nb_s, len_s, d)
    return (y_prompt, y_sample)
```

```python
import functools
import math

import numpy as np
import jax
import jax.numpy as jnp
from jax import lax
from jax.experimental import pallas as pl
from jax.experimental.pallas import tpu as pltpu

F32 = jnp.float32
BF16 = jnp.bfloat16
HIGHEST = lax.Precision.HIGHEST

D_MODEL = 1024
HEAD_DIM = 64
LANES = 128
GRID_W = 64
A_Q_HEADS = 8
A_KV_HEADS = 2
A_GROUP = A_Q_HEADS // A_KV_HEADS
A_HALF = 128
B_PAIRS = ((128, 1), (512, 4), (2048, 16))
B_GROUPS = 3
B_HEADS = 4
A_QW = A_Q_HEADS * HEAD_DIM
A_KVW = A_KV_HEADS * HEAD_DIM
A_IN = A_QW + 2 * A_KVW
B_GW = B_HEADS * HEAD_DIM
C_HEADS = 16
C_W = C_HEADS * HEAD_DIM
NA_KH = 8
NA_KW = 16
N_EXPERTS = 32
TOP_K = 4
D_FF = 1024
SWIGLU_LIMIT = 7.0
SWIGLU_ALPHA = 1.702
RMS_EPS = 1e-6
NEG_INF = -1e30
QK_SCALE = HEAD_DIM ** -0.5

TOKEN_TILE = 512
ATTN_TQ = 128
NA_TROWS = 8
MOE_BM = 256
ROUTE_TILE = 512
MOVE_TILE = 256
VMEM_LIMIT = 56 * 1024 * 1024


def _alibi_slopes(n):
    return [float(2.0 ** (-8.0 * (j + 1) / n)) for j in range(n)]


def _cparams(sem):
    return pltpu.CompilerParams(dimension_semantics=sem, vmem_limit_bytes=VMEM_LIMIT)


def _mod_kernel(c_ref, w_ref, b_ref, o_ref):
    c = c_ref[...]
    s = c * jax.nn.sigmoid(c)
    o_ref[...] = jnp.dot(s, w_ref[...], precision=HIGHEST, preferred_element_type=F32) + b_ref[...]


def _ada_mod(c_pad, w, b):
    nrow = c_pad.shape[0]
    ncol = w.shape[1]
    tn = D_MODEL
    return pl.pallas_call(
        _mod_kernel,
        grid=(ncol // tn,),
        in_specs=[pl.BlockSpec((nrow, D_MODEL), lambda j: (0, 0)),
                  pl.BlockSpec((D_MODEL, tn), lambda j: (0, j)),
                  pl.BlockSpec((1, tn), lambda j: (0, j))],
        out_specs=pl.BlockSpec((nrow, tn), lambda j: (0, j)),
        out_shape=jax.ShapeDtypeStruct((nrow, ncol), F32),
        compiler_params=_cparams(("arbitrary",)),
        name="ada_mod",
    )(c_pad, w, b.reshape(1, ncol))


def _head_mean_sq(y, bd_ref):
    w = y.shape[1]
    outs = []
    for c0 in range(0, w, 2 * LANES):
        cw = min(2 * LANES, w - c0)
        sq = y[:, c0:c0 + cw] * y[:, c0:c0 + cw]
        hi = sq.astype(BF16)
        lo = (sq - hi.astype(F32)).astype(BF16)
        bd = bd_ref[0:cw, 0:cw]
        outs.append(jnp.dot(hi, bd, preferred_element_type=F32)
                    + jnp.dot(lo, bd, preferred_element_type=F32))
    return outs[0] if len(outs) == 1 else jnp.concatenate(outs, axis=1)


def _rms_mod(x, g_ref, mod_ref, shift_row, scale_row):
    ms = jnp.mean(x * x, axis=-1, keepdims=True)
    xn = x * lax.rsqrt(ms + RMS_EPS) * g_ref[...]
    return xn * (1.0 + mod_ref[scale_row:scale_row + 1, :]) + mod_ref[shift_row:shift_row + 1, :]


def _in_kernel(sid_ref, x_ref, mod_ref, n1_ref, w_ref, gain_ref, bd_ref, *rest, slabs, tm):
    del sid_ref
    nout = len(slabs)
    out_refs = rest[:nout]
    scr_ref = rest[nout] if len(rest) > nout else None
    h = _rms_mod(x_ref[...], n1_ref, mod_ref, 0, 1).astype(BF16)
    for (c0, w, normed, dil), o_ref in zip(slabs, out_refs):
        y = jnp.dot(h, w_ref[:, c0:c0 + w], preferred_element_type=F32)
        if normed:
            y = y * lax.rsqrt(_head_mean_sq(y, bd_ref) + RMS_EPS) * gain_ref[:, c0:c0 + w]
        if dil == 1:
            o_ref[...] = y.astype(BF16)
        else:
            for c in range(w // LANES):
                scr_ref[c] = y[:, c * LANES:(c + 1) * LANES]
            for r in range(dil):
                for c in range(w // LANES):
                    o_ref[:, r * w + c * LANES:r * w + (c + 1) * LANES] = (
                        scr_ref[c, pl.ds(r, tm // dil, stride=dil), :].astype(BF16))


def _in_proj(x, mod, tile_sid, n1, w_bf, gain, bd, slabs):
    t = x.shape[0]
    tm = TOKEN_TILE
    ncols = w_bf.shape[1]
    out_shapes, out_specs = [], []
    for (c0, w, normed, dil) in slabs:
        out_shapes.append(jax.ShapeDtypeStruct((t // dil, dil * w), BF16))
        out_specs.append(pl.BlockSpec((tm // dil, dil * w), lambda i, sid: (i, 0)))
    need_scr = any(s[3] > 1 for s in slabs)
    scratch = [pltpu.VMEM((max(s[1] for s in slabs if s[3] > 1) // LANES, tm, LANES), F32)] if need_scr else []
    gs = pltpu.PrefetchScalarGridSpec(
        num_scalar_prefetch=1,
        grid=(t // tm,),
        in_specs=[pl.BlockSpec((tm, D_MODEL), lambda i, sid: (i, 0)),
                  pl.BlockSpec((None, 6, D_MODEL), lambda i, sid: (sid[i], 0, 0)),
                  pl.BlockSpec((1, D_MODEL), lambda i, sid: (0, 0)),
                  pl.BlockSpec((D_MODEL, ncols), lambda i, sid: (0, 0)),
                  pl.BlockSpec((1, ncols), lambda i, sid: (0, 0)),
                  pl.BlockSpec((2 * LANES, 2 * LANES), lambda i, sid: (0, 0))],
        out_specs=out_specs,
        scratch_shapes=scratch)
    return pl.pallas_call(
        functools.partial(_in_kernel, slabs=slabs, tm=tm),
        grid_spec=gs,
        out_shape=out_shapes,
        compiler_params=_cparams(("arbitrary",)),
        name="in_proj",
    )(tile_sid, x, mod, n1.reshape(1, D_MODEL), w_bf, gain, bd)


def _band_kernel(s0_ref, s1_ref, *refs, tq, hb, half, heads, kv_group, has_sink, want_lse):
    pos = 0
    sink_ref = None
    if has_sink:
        sink_ref = refs[0]
        pos = 1
    q_ref, kp_ref, kc_ref, kn_ref, vp_ref, vc_ref, vn_ref = refs[pos:pos + 7]
    o_ref = refs[pos + 7]
    lse_ref = refs[pos + 8] if want_lse else None
    i = pl.program_id(1)
    wlen = tq + 2 * hb
    kwin = jnp.concatenate([kp_ref[...], kc_ref[...], kn_ref[...]], axis=0)
    vwin = jnp.concatenate([vp_ref[...], vc_ref[...], vn_ref[...]], axis=0)
    q0 = i * tq
    kpos = q0 - hb + lax.broadcasted_iota(jnp.int32, (tq, wlen), 1)
    qpos = q0 + lax.broadcasted_iota(jnp.int32, (tq, wlen), 0)
    rel = kpos - qpos
    valid = (jnp.abs(rel) <= half) & (kpos >= s0_ref[i]) & (kpos < s1_ref[i])
    distm = jnp.where(valid, jnp.abs(rel).astype(F32), 1e32)
    lane = lax.broadcasted_iota(jnp.int32, (tq, LANES), 1)
    lo_lanes = lane < HEAD_DIM
    for g, pair in enumerate(heads):
        qg = q_ref[:, g * LANES:(g + 1) * LANES]
        kg = kwin[:, kv_group[g] * LANES:(kv_group[g] + 1) * LANES]
        vg = vwin[:, kv_group[g] * LANES:(kv_group[g] + 1) * LANES]
        outs, lses = [], []
        for s, (slope, sink_idx) in enumerate(pair):
            keep = lo_lanes if s == 0 else jnp.logical_not(lo_lanes)
            qm = jnp.where(keep, qg, jnp.zeros_like(qg))
            sc = lax.dot_general(qm, kg, (((1,), (1,)), ((), ())), preferred_element_type=F32)
            sc = sc - slope * distm
            m = jnp.max(sc, axis=-1, keepdims=True)
            if has_sink:
                m = jnp.maximum(m, sink_ref[sink_idx])
            e = jnp.exp(sc - m)
            den = jnp.sum(e, axis=-1, keepdims=True)
            if has_sink:
                den = den + jnp.exp(sink_ref[sink_idx] - m)
            pv = jnp.dot(e.astype(BF16), vg, preferred_element_type=F32)
            outs.append(pv / den)
            if want_lse:
                lses.append(jnp.broadcast_to(m + jnp.log(den), (tq, LANES)))
        o_ref[:, g * LANES:(g + 1) * LANES] = jnp.where(lo_lanes, outs[0], outs[1]).astype(o_ref.dtype)
        if want_lse:
            lse_ref[:, g * LANES:(g + 1) * LANES] = jnp.where(lo_lanes, lses[0], lses[1])


def _band_attn(q, k, v, s0, s1, *, dil, qw, kvw, half, heads, kv_group, sink, out_dtype, want_lse):
    rows = q.shape[0]
    tq, hb = ATTN_TQ, half
    per = tq // hb
    nt = rows // tq
    nhb = rows // hb
    has_sink = sink is not None
    in_specs = []
    args = []
    if has_sink:
        in_specs.append(pl.BlockSpec(memory_space=pltpu.SMEM))
        args.append(sink)
    qmap = lambda r, i, a, b: (i, r)
    pmap = lambda r, i, a, b: (jnp.maximum(i * per - 1, 0), r)
    nmap = lambda r, i, a, b: (jnp.minimum((i + 1) * per, nhb - 1), r)
    in_specs += [pl.BlockSpec((tq, qw), qmap),
                 pl.BlockSpec((hb, kvw), pmap), pl.BlockSpec((tq, kvw), qmap), pl.BlockSpec((hb, kvw), nmap),
                 pl.BlockSpec((hb, kvw), pmap), pl.BlockSpec((tq, kvw), qmap), pl.BlockSpec((hb, kvw), nmap)]
    args += [q, k, k, k, v, v, v]
    out_shape = [jax.ShapeDtypeStruct((rows, dil * qw), out_dtype)]
    out_specs = [pl.BlockSpec((tq, qw), qmap)]
    if want_lse:
        out_shape.append(jax.ShapeDtypeStruct((rows, dil * qw), F32))
        out_specs.append(pl.BlockSpec((tq, qw), qmap))
    gs = pltpu.PrefetchScalarGridSpec(num_scalar_prefetch=2, grid=(dil, nt),
                                      in_specs=in_specs, out_specs=out_specs)
    res = pl.pallas_call(
        functools.partial(_band_kernel, tq=tq, hb=hb, half=half, heads=heads, kv_group=kv_group,
                          has_sink=has_sink, want_lse=want_lse),
        grid_spec=gs,
        out_shape=out_shape,
        compiler_params=_cparams(("arbitrary", "arbitrary")),
        name="band_attn_d%d" % dil,
    )(s0, s1, *args)
    return res


def _na_kernel(r0_ref, nr_ref, q_ref, kp_ref, kc_ref, kn_ref, vp_ref, vc_ref, vn_ref, bias_ref,
               o_ref, kcat, vcat, *, halo):
    j = pl.program_id(0)
    tq = NA_TROWS * GRID_W
    hrows = halo * GRID_W
    kcat[0:hrows, :] = kp_ref[...]
    kcat[hrows:hrows + tq, :] = kc_ref[...]
    kcat[hrows + tq:, :] = kn_ref[...]
    vcat[0:hrows, :] = vp_ref[...]
    vcat[hrows:hrows + tq, :] = vc_ref[...]
    vcat[hrows + tq:, :] = vn_ref[...]
    r0 = r0_ref[j]
    nrows = nr_ref[j]
    kwin_len = NA_KH * GRID_W
    lane = lax.broadcasted_iota(jnp.int32, (GRID_W, LANES), 1)
    lo_lanes = lane < HEAD_DIM
    nvar = 2 * NA_KH - 2

    def row_body(u, carry):
        r = r0 + u
        rs = jnp.clip(r - NA_KH // 2, 0, nrows - NA_KH)
        off = pl.multiple_of((rs - r0 + halo) * GRID_W, GRID_W)
        bvar = rs - r + NA_KH - 1
        qrow = pl.multiple_of(u * GRID_W, GRID_W)
        for g in range(C_HEADS // 2):
            qg = q_ref[pl.ds(qrow, GRID_W), g * LANES:(g + 1) * LANES]
            qs = jnp.concatenate([jnp.where(lo_lanes, qg, jnp.zeros_like(qg)),
                                  jnp.where(lo_lanes, jnp.zeros_like(qg), qg)], axis=0)
            kg = kcat[pl.ds(off, kwin_len), g * LANES:(g + 1) * LANES]
            vg = vcat[pl.ds(off, kwin_len), g * LANES:(g + 1) * LANES]
            sc = lax.dot_general(qs, kg, (((1,), (1,)), ((), ())), preferred_element_type=F32)
            bias = []
            for s in range(2):
                hbase = (2 * g + s) * nvar + bvar
                bias.append(jnp.concatenate([bias_ref[hbase + 2 * jj] for jj in range(NA_KH // 2)], axis=1))
            sc = sc + jnp.concatenate(bias, axis=0)
            m = jnp.max(sc, axis=-1, keepdims=True)
            e = jnp.exp(sc - m)
            den = jnp.sum(e, axis=-1, keepdims=True)
            pv = jnp.dot(e.astype(BF16), vg, preferred_element_type=F32) / den
            o = jnp.where(lo_lanes, pv[0:GRID_W], pv[GRID_W:])
            o_ref[pl.ds(qrow, GRID_W), g * LANES:(g + 1) * LANES] = o.astype(o_ref.dtype)
        return carry

    lax.fori_loop(0, NA_TROWS, row_body, 0)


def _na_attn(q, k, v, bias_tab, tile_r0, tile_nr):
    t = q.shape[0]
    halo = NA_KH // 2
    tq = NA_TROWS * GRID_W
    hrows = halo * GRID_W
    per = tq // hrows
    nhb = t // hrows
    nt = t // tq
    qmap = lambda j, a, b: (j, 0)
    pmap = lambda j, a, b: (jnp.maximum(j * per - 1, 0), 0)
    nmap = lambda j, a, b: (jnp.minimum((j + 1) * per, nhb - 1), 0)
    gs = pltpu.PrefetchScalarGridSpec(
        num_scalar_prefetch=2, grid=(nt,),
        in_specs=[pl.BlockSpec((tq, C_W), qmap),
                  pl.BlockSpec((hrows, C_W), pmap), pl.BlockSpec((tq, C_W), qmap), pl.BlockSpec((hrows, C_W), nmap),
                  pl.BlockSpec((hrows, C_W), pmap), pl.BlockSpec((tq, C_W), qmap), pl.BlockSpec((hrows, C_W), nmap),
                  pl.BlockSpec(bias_tab.shape, lambda j, a, b: (0, 0, 0))],
        out_specs=pl.BlockSpec((tq, C_W), qmap),
        scratch_shapes=[pltpu.VMEM((tq + 2 * hrows, C_W), BF16), pltpu.VMEM((tq + 2 * hrows, C_W), BF16)])
    return pl.pallas_call(
        functools.partial(_na_kernel, halo=halo),
        grid_spec=gs,
        out_shape=jax.ShapeDtypeStruct((t, C_W), BF16),
        compiler_params=_cparams(("arbitrary",)),
        name="na_attn",
    )(tile_r0, tile_nr, q, k, k, k, v, v, v, bias_tab)


def _na_bias_table(rpb):
    c = np.arange(GRID_W)
    cs = np.clip(c - NA_KW // 2, 0, GRID_W - NA_KW)
    kc = np.arange(GRID_W)
    valid = (kc[None, :] >= cs[:, None]) & (kc[None, :] < cs[:, None] + NA_KW)
    cidx = np.clip(kc[None, :] - c[:, None] + NA_KW - 1, 0, 2 * NA_KW - 2)
    tab = rpb.astype(F32)[:, :, cidx]
    tab = jnp.where(jnp.asarray(valid)[None, None], tab, NEG_INF)
    pairs = jnp.concatenate([tab[:, :-1], tab[:, 1:]], axis=-1)
    return pairs.reshape(C_HEADS * (2 * NA_KH - 2), GRID_W, LANES)


def _route(h2, wr_ref, br_ref, idx_ref, gate_ref):
    tm = h2.shape[0]
    logits = lax.dot_general(wr_ref[...], h2, (((1,), (1,)), ((), ())),
                             precision=HIGHEST, preferred_element_type=F32) + br_ref[:, 0:1]
    eid = lax.broadcasted_iota(jnp.int32, (N_EXPERTS, tm), 0)
    vals, idxs = [], []
    for _ in range(TOP_K):
        m = jnp.max(logits, axis=0, keepdims=True)
        ix = jnp.min(jnp.where(logits == m, eid, N_EXPERTS), axis=0, keepdims=True)
        vals.append(m)
        idxs.append(ix)
        logits = jnp.where(eid == ix, -jnp.inf, logits)
    es = [jnp.exp(vk - vals[0]) for vk in vals]
    den = es[0] + es[1] + es[2] + es[3]
    pad_i = jnp.zeros((8 - TOP_K, tm), jnp.int32)
    pad_f = jnp.zeros((8 - TOP_K, tm), F32)
    idx_ref[...] = jnp.concatenate(idxs + [pad_i], axis=0)
    gate_ref[...] = jnp.concatenate([ek / den for ek in es] + [pad_f], axis=0)


def _out0_kernel(sid_ref, x_ref, mod_ref, n2_ref, oa_ref, ob0_ref, ls0_ref, ob1_ref, ls1_ref, ob2_ref, ls2_ref,
                 wout_ref, wr_ref, br_ref, x1_ref, h2_ref, idx_ref, gate_ref, oscr, lscr, *, tm):
    del sid_ref
    os_, ls_ = [], []
    for gi, (o_ref, l_ref) in enumerate(((ob0_ref, ls0_ref), (ob1_ref, ls1_ref), (ob2_ref, ls2_ref))):
        dil = B_PAIRS[gi][1]
        if dil == 1:
            os_.append(o_ref[...])
            ls_.append(l_ref[...])
        else:
            ncg = B_GW // LANES
            for r in range(dil):
                for c in range(ncg):
                    lo = r * B_GW + c * LANES
                    oscr[gi * ncg + c, pl.ds(r, tm // dil, stride=dil), :] = o_ref[:, lo:lo + LANES]
                    lscr[gi * ncg + c, pl.ds(r, tm // dil, stride=dil), :] = l_ref[:, lo:lo + LANES]
            os_.append(jnp.concatenate([oscr[gi * ncg + c] for c in range(ncg)], axis=1))
            ls_.append(jnp.concatenate([lscr[gi * ncg + c] for c in range(ncg)], axis=1))
    lmax = jnp.maximum(jnp.maximum(ls_[0], ls_[1]), ls_[2])
    ws = [jnp.exp(l - lmax) for l in ls_]
    ob = (ws[0] * os_[0] + ws[1] * os_[1] + ws[2] * os_[2]) / (ws[0] + ws[1] + ws[2])
    o = (jnp.dot(oa_ref[...], wout_ref[0:A_QW, :], preferred_element_type=F32)
         + jnp.dot(ob.astype(BF16), wout_ref[A_QW:, :], preferred_element_type=F32))
    x1 = x_ref[...] + mod_ref[2:3, :] * o
    x1_ref[...] = x1
    h2 = _rms_mod(x1, n2_ref, mod_ref, 3, 4)
    h2_ref[...] = h2
    _route(h2, wr_ref, br_ref, idx_ref, gate_ref)


def _out1_kernel(sid_ref, x_ref, mod_ref, n2_ref, oc_ref, wout_ref, wr_ref, br_ref,
                 x1_ref, h2_ref, idx_ref, gate_ref):
    del sid_ref
    o = jnp.dot(oc_ref[...], wout_ref[...], preferred_element_type=F32)
    x1 = x_ref[...] + mod_ref[2:3, :] * o
    x1_ref[...] = x1
    h2 = _rms_mod(x1, n2_ref, mod_ref, 3, 4)
    h2_ref[...] = h2
    _route(h2, wr_ref, br_ref, idx_ref, gate_ref)


def _out_proj(layer, x, mod, tile_sid, n2, attn, wout_bf, wr_t, br):
    t = x.shape[0]
    tm = TOKEN_TILE
    row = lambda i, sid: (i, 0)
    const = lambda i, sid: (0, 0)
    in_specs = [pl.BlockSpec((tm, D_MODEL), row),
                pl.BlockSpec((None, 6, D_MODEL), lambda i, sid: (sid[i], 0, 0)),
                pl.BlockSpec((1, D_MODEL), const)]
    scratch = []
    if layer == 0:
        in_specs.append(pl.BlockSpec((tm, A_QW), row))
        for (_, dil) in B_PAIRS:
            in_specs += [pl.BlockSpec((tm // dil, dil * B_GW), row)] * 2
        body = functools.partial(_out0_kernel, tm=tm)
        nscr = B_GROUPS * B_GW // LANES
        scratch = [pltpu.VMEM((nscr, tm, LANES), F32), pltpu.VMEM((nscr, tm, LANES), F32)]
    else:
        in_specs.append(pl.BlockSpec((tm, C_W), row))
        body = _out1_kernel
    in_specs += [pl.BlockSpec(wout_bf.shape, const),
                 pl.BlockSpec((N_EXPERTS, D_MODEL), const),
                 pl.BlockSpec((N_EXPERTS, LANES), const)]
    gs = pltpu.PrefetchScalarGridSpec(
        num_scalar_prefetch=1, grid=(t // tm,), in_specs=in_specs,
        out_specs=[pl.BlockSpec((tm, D_MODEL), row), pl.BlockSpec((tm, D_MODEL), row),
                   pl.BlockSpec((8, tm), lambda i, sid: (0, i)), pl.BlockSpec((8, tm), lambda i, sid: (0, i))],
        scratch_shapes=scratch)
    return pl.pallas_call(
        body, grid_spec=gs,
        out_shape=[jax.ShapeDtypeStruct((t, D_MODEL), F32), jax.ShapeDtypeStruct((t, D_MODEL), F32),
                   jax.ShapeDtypeStruct((8, t), jnp.int32), jax.ShapeDtypeStruct((8, t), F32)],
        compiler_params=_cparams(("arbitrary",)),
        name="out_proj%d" % layer,
    )(tile_sid, x, mod, n2.reshape(1, D_MODEL), *attn, wout_bf, wr_t, br)


def _rank_kernel(idx_ref, tri_ref, rank_ref, cnt_ref, carry):
    i = pl.program_id(0)

    @pl.when(i == 0)
    def _():
        carry[...] = jnp.zeros_like(carry)

    tk = idx_ref.shape[1]
    eid = lax.broadcasted_iota(jnp.int32, (N_EXPERTS, tk), 0)
    hits = [eid == idx_ref[k:k + 1, :] for k in range(TOP_K)]
    onehot = sum(h.astype(F32) for h in hits)
    incl = jnp.dot(onehot.astype(BF16), tri_ref[...], preferred_element_type=F32)
    before = incl - onehot + carry[:, 0:1]
    rows = [jnp.sum(jnp.where(h, before, 0.0), axis=0, keepdims=True) for h in hits]
    rows.append(jnp.zeros((8 - TOP_K, tk), F32))
    rank_ref[...] = jnp.concatenate(rows, axis=0).astype(jnp.int32)
    carry[...] = carry[...] + incl[:, tk - 1:tk]
    cnt_ref[...] = carry[...].astype(jnp.int32)


def _slot_kernel(idx_ref, rank_ref, start_ref, slot_ref):
    tk = idx_ref.shape[1]
    eid = lax.broadcasted_iota(jnp.int32, (N_EXPERTS, tk), 0)
    start = start_ref[:, 0:1]
    rows = []
    for k in range(TOP_K):
        base = jnp.sum(jnp.where(eid == idx_ref[k:k + 1, :], start, 0), axis=0, keepdims=True)
        rows.append(base + rank_ref[k:k + 1, :])
    rows.append(jnp.zeros((8 - TOP_K, tk), jnp.int32))
    slot_ref[...] = jnp.concatenate(rows, axis=0)


def _route_slots(idx):
    t = idx.shape[1]
    tk = ROUTE_TILE
    tri = jnp.asarray(np.triu(np.ones((tk, tk), np.float32)), BF16)
    tok = lambda i: (0, i)
    rank, cnt = pl.pallas_call(
        _rank_kernel, grid=(t // tk,),
        in_specs=[pl.BlockSpec((8, tk), tok), pl.BlockSpec((tk, tk), lambda i: (0, 0))],
        out_specs=[pl.BlockSpec((8, tk), tok), pl.BlockSpec((N_EXPERTS, LANES), lambda i: (0, 0))],
        out_shape=[jax.ShapeDtypeStruct((8, t), jnp.int32), jax.ShapeDtypeStruct((N_EXPERTS, LANES), jnp.int32)],
        scratch_shapes=[pltpu.VMEM((N_EXPERTS, LANES), F32)],
        compiler_params=_cparams(("arbitrary",)),
        name="route_rank",
    )(idx, tri)
    counts = cnt[:, 0]
    padded = (counts + MOE_BM - 1) // MOE_BM * MOE_BM
    pad_end = jnp.cumsum(padded)
    pad_start = pad_end - padded
    nblk = (t * TOP_K) // MOE_BM + N_EXPERTS
    blk_row0 = jnp.arange(nblk, dtype=jnp.int32) * MOE_BM
    blk_expert = jnp.minimum(jnp.sum((pad_end[None, :] <= blk_row0[:, None]).astype(jnp.int32), axis=1),
                             N_EXPERTS - 1)
    nused = (pad_end[-1] // MOE_BM).astype(jnp.int32).reshape(1)
    start_b = jnp.broadcast_to(pad_start.astype(jnp.int32)[:, None], (N_EXPERTS, LANES))
    slot = pl.pallas_call(
        _slot_kernel, grid=(t // tk,),
        in_specs=[pl.BlockSpec((8, tk), tok), pl.BlockSpec((8, tk), tok),
                  pl.BlockSpec((N_EXPERTS, LANES), lambda i: (0, 0))],
        out_specs=pl.BlockSpec((8, tk), tok),
        out_shape=jax.ShapeDtypeStruct((8, t), jnp.int32),
        compiler_params=_cparams(("arbitrary",)),
        name="route_slot",
    )(idx, rank, start_b)
    return slot, blk_expert, nused, pad_end.astype(jnp.int32), padded.astype(jnp.int32)


def _row_copy(src, dst, sem):
    return pltpu.make_async_copy(src, dst, sem)


ROW_GROUP = 8


def _for_each_row_copy(n_tok, fn):
    def group(j, carry):
        base = pl.multiple_of(j * ROW_GROUP, ROW_GROUP)
        for u in range(ROW_GROUP):
            for k in range(TOP_K):
                fn(base + u, k, (u * TOP_K + k) % 2)
        return carry

    lax.fori_loop(0, n_tok // ROW_GROUP, group, 0)


def _dispatch_kernel(pend_ref, padded_ref, slot_ref, h_ref, xs_ref, zbuf, sem, zsem, *, tm, nblk):
    i = pl.program_id(0)

    @pl.when(i == 0)
    def _():
        zbuf[...] = jnp.zeros_like(zbuf)

        def zero_block(b):
            cp = _row_copy(zbuf, xs_ref.at[pl.ds(pl.multiple_of(b * MOE_BM, MOE_BM), MOE_BM)], zsem)
            cp.start()
            cp.wait()

        for e in range(N_EXPERTS):
            @pl.when(padded_ref[e] > 0)
            def _():
                zero_block(pend_ref[e] // MOE_BM - 1)

        def tail(b, carry):
            zero_block(b)
            return carry

        lax.fori_loop(pend_ref[N_EXPERTS - 1] // MOE_BM, nblk, tail, 0)

    def issue(tk, k, priority):
        _row_copy(h_ref.at[pl.ds(tk, 1)], xs_ref.at[pl.ds(slot_ref[k, tk], 1)], sem).start(priority=priority)

    _for_each_row_copy(tm, issue)
    for k in range(TOP_K):
        _row_copy(h_ref, xs_ref.at[pl.ds(0, tm)], sem).wait()


def _dispatch(h2, slot, pad_end, padded, nslots):
    t = h2.shape[0]
    tm = MOVE_TILE
    gs = pltpu.PrefetchScalarGridSpec(
        num_scalar_prefetch=2, grid=(t // tm,),
        in_specs=[pl.BlockSpec((8, tm), lambda i, a, b: (0, i), memory_space=pltpu.SMEM),
                  pl.BlockSpec((tm, D_MODEL), lambda i, a, b: (i, 0))],
        out_specs=pl.BlockSpec(memory_space=pl.ANY),
        scratch_shapes=[pltpu.VMEM((MOE_BM, D_MODEL), F32), pltpu.SemaphoreType.DMA, pltpu.SemaphoreType.DMA])
    return pl.pallas_call(
        functools.partial(_dispatch_kernel, tm=tm, nblk=nslots // MOE_BM), grid_spec=gs,
        out_shape=jax.ShapeDtypeStruct((nslots, D_MODEL), F32),
        compiler_params=_cparams(("arbitrary",)),
        name="moe_dispatch",
    )(pad_end, padded, slot, h2)


def _ffn_kernel(be_ref, nu_ref, xs_ref, wgu_ref, bgu_ref, wd_ref, bd_ref, y_ref, wgu_bf, wd_bf):
    i = pl.program_id(0)
    prev = be_ref[jnp.maximum(i - 1, 0)]

    @pl.when(i < nu_ref[0])
    def _():
        @pl.when((i == 0) | (be_ref[i] != prev))
        def _():
            wgu_bf[...] = wgu_ref[...].astype(BF16)
            wd_bf[...] = wd_ref[...].astype(BF16)

        x = xs_ref[...].astype(BF16)
        gu = jnp.dot(x, wgu_bf[...], preferred_element_type=F32) + bgu_ref[...]
        gate = jnp.minimum(gu[:, :D_FF], SWIGLU_LIMIT)
        up = jnp.clip(gu[:, D_FF:], -SWIGLU_LIMIT, SWIGLU_LIMIT)
        act = (up + 1.0) * gate * jax.nn.sigmoid(SWIGLU_ALPHA * gate)
        y_ref[...] = jnp.dot(act.astype(BF16), wd_bf[...], preferred_element_type=F32) + bd_ref[...]

    @pl.when(i >= nu_ref[0])
    def _():
        y_ref[...] = jnp.zeros_like(y_ref)


def _expert_ffn(xs, blk_expert, nused, wgu, bgu, wd, bd):
    nslots = xs.shape[0]
    nblk = nslots // MOE_BM
    blk = lambda i, be, nu: (jnp.minimum(i, nu[0] - 1), 0)
    exp3 = lambda i, be, nu: (be[jnp.minimum(i, nu[0] - 1)], 0, 0)
    gs = pltpu.PrefetchScalarGridSpec(
        num_scalar_prefetch=2, grid=(nblk,),
        in_specs=[pl.BlockSpec((MOE_BM, D_MODEL), blk),
                  pl.BlockSpec((None, D_MODEL, 2 * D_FF), exp3),
                  pl.BlockSpec((None, 1, 2 * D_FF), exp3),
                  pl.BlockSpec((None, D_FF, D_MODEL), exp3),
                  pl.BlockSpec((None, 1, D_MODEL), exp3)],
        out_specs=pl.BlockSpec((MOE_BM, D_MODEL), lambda i, be, nu: (i, 0)),
        scratch_shapes=[pltpu.VMEM((D_MODEL, 2 * D_FF), BF16), pltpu.VMEM((D_FF, D_MODEL), BF16)])
    return pl.pallas_call(
        _ffn_kernel, grid_spec=gs,
        out_shape=jax.ShapeDtypeStruct((nslots, D_MODEL), F32),
        compiler_params=_cparams(("arbitrary",)),
        name="moe_ffn",
    )(blk_expert, nused, xs, wgu, bgu.reshape(N_EXPERTS, 1, 2 * D_FF), wd, bd.reshape(N_EXPERTS, 1, D_MODEL))


def _combine_kernel(sid_ref, slot_ref, x_ref, mod_ref, gate_ref, ys_ref, o_ref, buf, sem, *, tm):
    del sid_ref

    def issue(tk, k, priority):
        _row_copy(ys_ref.at[pl.ds(slot_ref[k, tk], 1)], buf.at[k, pl.ds(tk, 1)], sem).start(priority=priority)

    _for_each_row_copy(tm, issue)
    gpad = jnp.concatenate([gate_ref[...], jnp.zeros((LANES - 8, tm), F32)], axis=0)
    gcol = gpad.T
    for k in range(TOP_K):
        _row_copy(ys_ref.at[pl.ds(0, tm)], buf.at[k], sem).wait()
    acc = gcol[:, 0:1] * buf[0]
    for k in range(1, TOP_K):
        acc = acc + gcol[:, k:k + 1] * buf[k]
    o_ref[...] = x_ref[...] + mod_ref[5:6, :] * acc


def _combine(x1, mod, tile_sid, slot, gates, ys):
    t = x1.shape[0]
    tm = MOVE_TILE
    gs = pltpu.PrefetchScalarGridSpec(
        num_scalar_prefetch=1, grid=(t // tm,),
        in_specs=[pl.BlockSpec((8, tm), lambda i, sid: (0, i), memory_space=pltpu.SMEM),
                  pl.BlockSpec((tm, D_MODEL), lambda i, sid: (i, 0)),
                  pl.BlockSpec((None, 6, D_MODEL), lambda i, sid: (sid[i], 0, 0)),
                  pl.BlockSpec((8, tm), lambda i, sid: (0, i)),
                  pl.BlockSpec(memory_space=pl.ANY)],
        out_specs=pl.BlockSpec((tm, D_MODEL), lambda i, sid: (i, 0)),
        scratch_shapes=[pltpu.VMEM((TOP_K, tm, D_MODEL), F32), pltpu.SemaphoreType.DMA])
    return pl.pallas_call(
        functools.partial(_combine_kernel, tm=tm), grid_spec=gs,
        out_shape=jax.ShapeDtypeStruct((t, D_MODEL), F32),
        compiler_params=_cparams(("arbitrary",)),
        name="moe_combine",
    )(tile_sid, slot, x1, mod, gates, ys)


def _moe(x1, h2, idx, gates, mod, tile_sid_move, wgu, bgu, wd, bd):
    t = x1.shape[0]
    nslots = t * TOP_K + N_EXPERTS * MOE_BM
    slot, blk_expert, nused, pad_end, padded = _route_slots(idx)
    xs = _dispatch(h2, slot, pad_end, padded, nslots)
    ys = _expert_ffn(xs, blk_expert, nused, wgu, bgu, wd, bd)
    return _combine(x1, mod, tile_sid_move, slot, gates, ys)


def _tile_table(seq_lens, tile, fn):
    vals = []
    start = 0
    for sid, n in enumerate(seq_lens):
        assert n % tile == 0
        for j in range(n // tile):
            vals.append(fn(sid, start, n, j))
        start += n
    return jnp.asarray(np.asarray(vals, np.int32))


def _head_block_diag():
    bd = np.kron(np.eye(2 * LANES // HEAD_DIM, dtype=np.float32),
                 np.full((HEAD_DIM, HEAD_DIM), 1.0 / HEAD_DIM, np.float32))
    return jnp.asarray(bd, BF16)


_A_HEAD_ORDER = [kv * A_GROUP + g for g in range(A_GROUP) for kv in range(A_KV_HEADS)]


def _trunk(x, c_pad, seq_lens, p):
    t = x.shape[0]
    sid_tok = _tile_table(seq_lens, TOKEN_TILE, lambda sid, s, n, j: sid)
    sid_move = _tile_table(seq_lens, MOVE_TILE, lambda sid, s, n, j: sid)
    bd = _head_block_diag()

    mod0 = _ada_mod(c_pad, p["l0_ada_w"], p["l0_ada_b"]).reshape(c_pad.shape[0], 6, D_MODEL)
    w_in = p["l0_w_in"]
    a_cols = np.concatenate([np.arange(h * HEAD_DIM, (h + 1) * HEAD_DIM) for h in _A_HEAD_ORDER])
    bq, bk, bv = A_IN, A_IN + B_GROUPS * B_GW, A_IN + 2 * B_GROUPS * B_GW
    col_blocks = [w_in[:, a_cols], w_in[:, A_QW:A_IN]]
    for g in range(B_GROUPS):
        col_blocks += [w_in[:, bq + g * B_GW:bq + (g + 1) * B_GW],
                       w_in[:, bk + g * B_GW:bk + (g + 1) * B_GW],
                       w_in[:, bv + g * B_GW:bv + (g + 1) * B_GW]]
    w0 = jnp.concatenate(col_blocks, axis=1).astype(BF16)
    ones = lambda n: jnp.ones((n,), F32)
    gain_blocks = [jnp.tile(p["l0_q_norm_a"], A_Q_HEADS) * QK_SCALE, jnp.tile(p["l0_k_norm_a"], A_KV_HEADS),
                   ones(A_KVW)]
    for g in range(B_GROUPS):
        gain_blocks += [jnp.tile(p["l0_q_norm_b"], B_HEADS) * QK_SCALE, jnp.tile(p["l0_k_norm_b"], B_HEADS),
                        ones(B_GW)]
    gain0 = jnp.concatenate(gain_blocks).reshape(1, -1)
    slabs0 = [(0, A_QW, True, 1), (A_QW, A_KVW, True, 1), (A_QW + A_KVW, A_KVW, False, 1)]
    c0 = A_IN
    for g, (_, dil) in enumerate(B_PAIRS):
        slabs0 += [(c0, B_GW, True, dil), (c0 + B_GW, B_GW, True, dil), (c0 + 2 * B_GW, B_GW, False, dil)]
        c0 += 3 * B_GW
    proj = _in_proj(x, mod0, sid_tok, p["l0_norm1"], w0, gain0, bd, tuple(slabs0))
    qa, ka, va = proj[0:3]

    slopes_a = _alibi_slopes(A_Q_HEADS)
    heads_a = tuple(tuple((slopes_a[kv * A_GROUP + g], kv * A_GROUP + g) for kv in range(A_KV_HEADS))
                    for g in range(A_GROUP))
    s0 = _tile_table(seq_lens, ATTN_TQ, lambda sid, s, n, j: s)
    s1 = _tile_table(seq_lens, ATTN_TQ, lambda sid, s, n, j: s + n)
    (oa,) = _band_attn(qa, ka, va, s0, s1, dil=1, qw=A_QW, kvw=A_KVW, half=A_HALF, heads=heads_a,
                       kv_group=(0,) * A_GROUP, sink=p["l0_sink_a"].astype(F32), out_dtype=BF16, want_lse=False)

    slopes_b = _alibi_slopes(B_GROUPS * B_HEADS)
    attn0 = [oa]
    for g, (window, dil) in enumerate(B_PAIRS):
        qg, kg, vg = proj[3 + 3 * g:6 + 3 * g]
        heads_b = tuple(tuple((slopes_b[g * B_HEADS + 2 * pr + s] * dil, 0) for s in range(2))
                        for pr in range(B_HEADS // 2))
        sd0 = _tile_table([n // dil for n in seq_lens], ATTN_TQ, lambda sid, s, n, j: s)
        sd1 = _tile_table([n // dil for n in seq_lens], ATTN_TQ, lambda sid, s, n, j: s + n)
        og, lg = _band_attn(qg, kg, vg, sd0, sd1, dil=dil, qw=B_GW, kvw=B_GW, half=window // (2 * dil),
                            heads=heads_b, kv_group=tuple(range(B_HEADS // 2)), sink=None, out_dtype=F32,
                            want_lse=True)
        attn0 += [og, lg]

    w_out = p["l0_w_out"]
    wout0 = jnp.concatenate([w_out[a_cols], w_out[A_QW:]], axis=0).astype(BF16)
    br0 = jnp.broadcast_to(p["l0_b_router"].astype(F32)[:, None], (N_EXPERTS, LANES))
    x1, h2, idx, gates = _out_proj(0, x, mod0, sid_tok, p["l0_norm2"], attn0, wout0, p["l0_w_router"].T, br0)
    x = _moe(x1, h2, idx, gates, mod0, sid_move, p["l0_w_gate_up"], p["l0_b_gate_up"], p["l0_w_down"],
             p["l0_b_down"])

    mod1 = _ada_mod(c_pad, p["l1_ada_w"], p["l1_ada_b"]).reshape(c_pad.shape[0], 6, D_MODEL)
    w1 = p["l1_w_in"].astype(BF16)
    gain1 = jnp.concatenate([jnp.tile(p["l1_q_norm_c"], C_HEADS) * QK_SCALE, jnp.tile(p["l1_k_norm_c"], C_HEADS),
                             ones(C_W)]).reshape(1, -1)
    slabs1 = ((0, C_W, True, 1), (C_W, C_W, True, 1), (2 * C_W, C_W, False, 1))
    qc, kc, vc = _in_proj(x, mod1, sid_tok, p["l1_norm1"], w1, gain1, bd, slabs1)
    na_tile = NA_TROWS * GRID_W
    tile_r0 = _tile_table(seq_lens, na_tile, lambda sid, s, n, j: j * NA_TROWS)
    tile_nr = _tile_table(seq_lens, na_tile, lambda sid, s, n, j: n // GRID_W)
    oc = _na_attn(qc, kc, vc, _na_bias_table(p["l1_rpb_c"]), tile_r0, tile_nr)
    br1 = jnp.broadcast_to(p["l1_b_router"].astype(F32)[:, None], (N_EXPERTS, LANES))
    x1, h2, idx, gates = _out_proj(1, x, mod1, sid_tok, p["l1_norm2"], [oc], p["l1_w_out"].astype(BF16),
                                   p["l1_w_router"].T, br1)
    return _moe(x1, h2, idx, gates, mod1, sid_move, p["l1_w_gate_up"], p["l1_b_gate_up"], p["l1_w_down"],
                p["l1_b_down"])


def kernel(x_prompt, x_sample, c_prompt, c_sample, l0_ada_w, l0_ada_b, l0_norm1, l0_w_in, l0_q_norm_a, l0_k_norm_a, l0_sink_a, l0_q_norm_b, l0_k_norm_b, l0_w_out, l0_norm2, l0_w_router, l0_b_router, l0_w_gate_up, l0_b_gate_up, l0_w_down, l0_b_down, l1_ada_w, l1_ada_b, l1_norm1, l1_w_in, l1_q_norm_c, l1_k_norm_c, l1_rpb_c, l1_w_out, l1_norm2, l1_w_router, l1_b_router, l1_w_gate_up, l1_b_gate_up, l1_w_down, l1_b_down):
    p = dict(l0_ada_w=l0_ada_w, l0_ada_b=l0_ada_b, l0_norm1=l0_norm1, l0_w_in=l0_w_in, l0_q_norm_a=l0_q_norm_a,
             l0_k_norm_a=l0_k_norm_a, l0_sink_a=l0_sink_a, l0_q_norm_b=l0_q_norm_b, l0_k_norm_b=l0_k_norm_b,
             l0_w_out=l0_w_out, l0_norm2=l0_norm2, l0_w_router=l0_w_router, l0_b_router=l0_b_router,
             l0_w_gate_up=l0_w_gate_up, l0_b_gate_up=l0_b_gate_up, l0_w_down=l0_w_down, l0_b_down=l0_b_down,
             l1_ada_w=l1_ada_w, l1_ada_b=l1_ada_b, l1_norm1=l1_norm1, l1_w_in=l1_w_in, l1_q_norm_c=l1_q_norm_c,
             l1_k_norm_c=l1_k_norm_c, l1_rpb_c=l1_rpb_c, l1_w_out=l1_w_out, l1_norm2=l1_norm2,
             l1_w_router=l1_w_router, l1_b_router=l1_b_router, l1_w_gate_up=l1_w_gate_up,
             l1_b_gate_up=l1_b_gate_up, l1_w_down=l1_w_down, l1_b_down=l1_b_down)
    nb_p, len_p, d = x_prompt.shape
    nb_s, len_s, _ = x_sample.shape
    seq_lens = (len_p,) * nb_p + (len_s,) * nb_s
    x = jnp.concatenate([x_prompt.reshape(nb_p * len_p, d), x_sample.reshape(nb_s * len_s, d)], axis=0)
    nseq = nb_p + nb_s
    c_pad = jnp.concatenate([c_prompt, c_sample, jnp.zeros((-nseq % 8, d), F32)], axis=0)
    y = _trunk(x, c_pad, seq_lens, p)
    y_prompt = y[:nb_p * len_p].reshape(nb_p, len_p, d)
    y_sample = y[nb_p * len_p:].reshape(nb_s, len_s, d)
    return (y_prompt, y_sample)
```

```python
import functools
import math

import numpy as np
import jax
import jax.numpy as jnp
from jax import lax
from jax.experimental import pallas as pl
from jax.experimental.pallas import tpu as pltpu

F32 = jnp.float32
BF16 = jnp.bfloat16
HIGHEST = lax.Precision.HIGHEST

D_MODEL = 1024
HEAD_DIM = 64
LANES = 128
GRID_W = 64
A_Q_HEADS = 8
A_KV_HEADS = 2
A_GROUP = A_Q_HEADS // A_KV_HEADS
A_HALF = 128
B_PAIRS = ((128, 1), (512, 4), (2048, 16))
B_GROUPS = 3
B_HEADS = 4
A_QW = A_Q_HEADS * HEAD_DIM
A_KVW = A_KV_HEADS * HEAD_DIM
A_IN = A_QW + 2 * A_KVW
B_GW = B_HEADS * HEAD_DIM
C_HEADS = 16
C_W = C_HEADS * HEAD_DIM
NA_KH = 8
NA_KW = 16
N_EXPERTS = 32
TOP_K = 4
D_FF = 1024
SWIGLU_LIMIT = 7.0
SWIGLU_ALPHA = 1.702
RMS_EPS = 1e-6
NEG_INF = -1e30
QK_SCALE = HEAD_DIM ** -0.5

TOKEN_TILE = 512
ATTN_TQ_A = 256
ATTN_TQ_B = 512
ATTN_UQ = 128
NA_TROWS = 8
MOE_BM = 256
ROUTE_TILE = 512
MOVE_TILE = 256
VMEM_LIMIT = 56 * 1024 * 1024


def _alibi_slopes(n):
    return [float(2.0 ** (-8.0 * (j + 1) / n)) for j in range(n)]


def _cparams(sem, flags=None):
    return pltpu.CompilerParams(dimension_semantics=sem, vmem_limit_bytes=VMEM_LIMIT, flags=flags)


def _mod_kernel(c_ref, w_ref, b_ref, o_ref):
    c = c_ref[...]
    s = c * jax.nn.sigmoid(c)
    o_ref[...] = jnp.dot(s, w_ref[...], precision=HIGHEST, preferred_element_type=F32) + b_ref[...]


def _ada_mod(c_pad, w, b):
    nrow = c_pad.shape[0]
    ncol = w.shape[1]
    tn = D_MODEL
    return pl.pallas_call(
        _mod_kernel,
        grid=(ncol // tn,),
        in_specs=[pl.BlockSpec((nrow, D_MODEL), lambda j: (0, 0)),
                  pl.BlockSpec((D_MODEL, tn), lambda j: (0, j)),
                  pl.BlockSpec((1, tn), lambda j: (0, j))],
        out_specs=pl.BlockSpec((nrow, tn), lambda j: (0, j)),
        out_shape=jax.ShapeDtypeStruct((nrow, ncol), F32),
        compiler_params=_cparams(("arbitrary",)),
        name="ada_mod",
    )(c_pad, w, b.reshape(1, ncol))


def _head_mean_sq(y, bd_ref):
    w = y.shape[1]
    outs = []
    for c0 in range(0, w, 2 * LANES):
        cw = min(2 * LANES, w - c0)
        sq = y[:, c0:c0 + cw] * y[:, c0:c0 + cw]
        hi = sq.astype(BF16)
        lo = (sq - hi.astype(F32)).astype(BF16)
        bd = bd_ref[0:cw, 0:cw]
        outs.append(jnp.dot(hi, bd, preferred_element_type=F32)
                    + jnp.dot(lo, bd, preferred_element_type=F32))
    return outs[0] if len(outs) == 1 else jnp.concatenate(outs, axis=1)


def _rms_mod(x, g_ref, mod_ref, shift_row, scale_row):
    ms = jnp.mean(x * x, axis=-1, keepdims=True)
    xn = x * lax.rsqrt(ms + RMS_EPS) * g_ref[...]
    return xn * (1.0 + mod_ref[scale_row:scale_row + 1, :]) + mod_ref[shift_row:shift_row + 1, :]


def _in_kernel(sid_ref, x_ref, mod_ref, n1_ref, w_ref, gain_ref, bd_ref, *rest, slabs, tm):
    del sid_ref
    nout = len(slabs)
    out_refs = rest[:nout]
    scr_ref = rest[nout] if len(rest) > nout else None
    h = _rms_mod(x_ref[...], n1_ref, mod_ref, 0, 1).astype(BF16)
    for (c0, w, normed, dil), o_ref in zip(slabs, out_refs):
        y = jnp.dot(h, w_ref[:, c0:c0 + w], preferred_element_type=F32)
        if normed:
            y = y * lax.rsqrt(_head_mean_sq(y, bd_ref) + RMS_EPS) * gain_ref[:, c0:c0 + w]
        if dil == 1:
            o_ref[...] = y.astype(BF16)
        else:
            for c in range(w // LANES):
                scr_ref[c] = y[:, c * LANES:(c + 1) * LANES]
            for r in range(dil):
                for c in range(w // LANES):
                    o_ref[:, r * w + c * LANES:r * w + (c + 1) * LANES] = (
                        scr_ref[c, pl.ds(r, tm // dil, stride=dil), :].astype(BF16))


def _in_proj(x, mod, tile_sid, n1, w_bf, gain, bd, slabs):
    t = x.shape[0]
    tm = TOKEN_TILE
    ncols = w_bf.shape[1]
    out_shapes, out_specs = [], []
    for (c0, w, normed, dil) in slabs:
        out_shapes.append(jax.ShapeDtypeStruct((t // dil, dil * w), BF16))
        out_specs.append(pl.BlockSpec((tm // dil, dil * w), lambda i, sid: (i, 0)))
    need_scr = any(s[3] > 1 for s in slabs)
    scratch = [pltpu.VMEM((max(s[1] for s in slabs if s[3] > 1) // LANES, tm, LANES), F32)] if need_scr else []
    gs = pltpu.PrefetchScalarGridSpec(
        num_scalar_prefetch=1,
        grid=(t // tm,),
        in_specs=[pl.BlockSpec((tm, D_MODEL), lambda i, sid: (i, 0)),
                  pl.BlockSpec((None, 6, D_MODEL), lambda i, sid: (sid[i], 0, 0)),
                  pl.BlockSpec((1, D_MODEL), lambda i, sid: (0, 0)),
                  pl.BlockSpec((D_MODEL, ncols), lambda i, sid: (0, 0)),
                  pl.BlockSpec((1, ncols), lambda i, sid: (0, 0)),
                  pl.BlockSpec((2 * LANES, 2 * LANES), lambda i, sid: (0, 0))],
        out_specs=out_specs,
        scratch_shapes=scratch)
    return pl.pallas_call(
        functools.partial(_in_kernel, slabs=slabs, tm=tm),
        grid_spec=gs,
        out_shape=out_shapes,
        compiler_params=_cparams(("arbitrary",)),
        name="in_proj",
    )(tile_sid, x, mod, n1.reshape(1, D_MODEL), w_bf, gain, bd)


def _band_kernel(s0_ref, s1_ref, *refs, tq, hb, half, heads, kv_group, has_sink, want_lse):
    pos = 0
    sink_ref = None
    if has_sink:
        sink_ref = refs[0]
        pos = 1
    q_ref, kp_ref, kc_ref, kn_ref, vp_ref, vc_ref, vn_ref = refs[pos:pos + 7]
    o_ref = refs[pos + 7]
    lse_ref = refs[pos + 8] if want_lse else None
    i = pl.program_id(1)
    uq = ATTN_UQ
    wlen = uq + 2 * hb
    kall = jnp.concatenate([kp_ref[...], kc_ref[...], kn_ref[...]], axis=0)
    vall = jnp.concatenate([vp_ref[...], vc_ref[...], vn_ref[...]], axis=0)
    lane = lax.broadcasted_iota(jnp.int32, (uq, LANES), 1)
    lo_lanes = lane < HEAD_DIM
    for sub in range(tq // uq):
        rows = slice(sub * uq, (sub + 1) * uq)
        kwin = kall[sub * uq:sub * uq + wlen]
        vwin = vall[sub * uq:sub * uq + wlen]
        q0 = i * tq + sub * uq
        kpos = q0 - hb + lax.broadcasted_iota(jnp.int32, (wlen, uq), 0)
        qpos = q0 + lax.broadcasted_iota(jnp.int32, (wlen, uq), 1)
        rel = kpos - qpos
        valid = (jnp.abs(rel) <= half) & (kpos >= s0_ref[i]) & (kpos < s1_ref[i])
        distm = jnp.where(valid, jnp.abs(rel).astype(F32), 1e32)
        lse_rows = []
        for g, pair in enumerate(heads):
            qg = q_ref[rows, g * LANES:(g + 1) * LANES]
            kg = kwin[:, kv_group[g] * LANES:(kv_group[g] + 1) * LANES]
            vg = vwin[:, kv_group[g] * LANES:(kv_group[g] + 1) * LANES]
            outs = []
            for s, (slope, sink_idx) in enumerate(pair):
                keep = lo_lanes if s == 0 else jnp.logical_not(lo_lanes)
                qm = jnp.where(keep, qg, jnp.zeros_like(qg))
                sc = lax.dot_general(kg, qm, (((1,), (1,)), ((), ())), preferred_element_type=F32)
                sc = sc - slope * distm
                m = jnp.max(sc, axis=0, keepdims=True)
                if has_sink:
                    m = jnp.maximum(m, sink_ref[sink_idx])
                e = jnp.exp(sc - m)
                den = jnp.sum(e, axis=0, keepdims=True)
                if has_sink:
                    den = den + jnp.exp(sink_ref[sink_idx] - m)
                p = (e * (1.0 / den)).astype(BF16)
                outs.append(lax.dot_general(p, vg, (((0,), (0,)), ((), ())), preferred_element_type=F32))
                if want_lse:
                    lse_rows.append(m + jnp.log(den))
            o_ref[rows, g * LANES:(g + 1) * LANES] = jnp.where(lo_lanes, outs[0], outs[1]).astype(o_ref.dtype)
        if want_lse:
            lse_t = jnp.concatenate(lse_rows + [jnp.zeros((LANES - len(lse_rows), uq), F32)], axis=0)
            lse_ref[rows, :] = lse_t.T


def _band_attn(q, k, v, seq_rows, *, tq, dil, qw, kvw, half, heads, kv_group, sink, out_dtype, want_lse):
    rows = q.shape[0]
    hb = half
    s0 = _tile_table(seq_rows, tq, lambda sid, s, n, j: s)
    s1 = _tile_table(seq_rows, tq, lambda sid, s, n, j: s + n)
    per = tq // hb
    nt = rows // tq
    nhb = rows // hb
    has_sink = sink is not None
    in_specs = []
    args = []
    if has_sink:
        in_specs.append(pl.BlockSpec(memory_space=pltpu.SMEM))
        args.append(sink)
    qmap = lambda r, i, a, b: (i, r)
    pmap = lambda r, i, a, b: (jnp.maximum(i * per - 1, 0), r)
    nmap = lambda r, i, a, b: (jnp.minimum((i + 1) * per, nhb - 1), r)
    in_specs += [pl.BlockSpec((tq, qw), qmap),
                 pl.BlockSpec((hb, kvw), pmap), pl.BlockSpec((tq, kvw), qmap), pl.BlockSpec((hb, kvw), nmap),
                 pl.BlockSpec((hb, kvw), pmap), pl.BlockSpec((tq, kvw), qmap), pl.BlockSpec((hb, kvw), nmap)]
    args += [q, k, k, k, v, v, v]
    out_shape = [jax.ShapeDtypeStruct((rows, dil * qw), out_dtype)]
    out_specs = [pl.BlockSpec((tq, qw), qmap)]
    if want_lse:
        out_shape.append(jax.ShapeDtypeStruct((rows, dil * LANES), F32))
        out_specs.append(pl.BlockSpec((tq, LANES), qmap))
    gs = pltpu.PrefetchScalarGridSpec(num_scalar_prefetch=2, grid=(dil, nt),
                                      in_specs=in_specs, out_specs=out_specs)
    res = pl.pallas_call(
        functools.partial(_band_kernel, tq=tq, hb=hb, half=half, heads=heads, kv_group=kv_group,
                          has_sink=has_sink, want_lse=want_lse),
        grid_spec=gs,
        out_shape=out_shape,
        compiler_params=_cparams(("arbitrary", "arbitrary")),
        name="band_attn_d%d" % dil,
    )(s0, s1, *args)
    return res


def _na_kernel(r0_ref, nr_ref, q_ref, kp_ref, kc_ref, kn_ref, vp_ref, vc_ref, vn_ref, bias_ref,
               o_ref, kcat, vcat, *, halo):
    j = pl.program_id(0)
    tq = NA_TROWS * GRID_W
    hrows = halo * GRID_W
    kcat[0:hrows, :] = kp_ref[...]
    kcat[hrows:hrows + tq, :] = kc_ref[...]
    kcat[hrows + tq:, :] = kn_ref[...]
    vcat[0:hrows, :] = vp_ref[...]
    vcat[hrows:hrows + tq, :] = vc_ref[...]
    vcat[hrows + tq:, :] = vn_ref[...]
    r0 = r0_ref[j]
    nrows = nr_ref[j]
    kwin_len = NA_KH * GRID_W
    lane = lax.broadcasted_iota(jnp.int32, (GRID_W, LANES), 1)
    lo_lanes = lane < HEAD_DIM
    nrel = 2 * NA_KH - 1

    def row_body(u, carry):
        r = r0 + u
        rs = jnp.clip(r - NA_KH // 2, 0, nrows - NA_KH)
        off = pl.multiple_of((rs - r0 + halo) * GRID_W, GRID_W)
        bvar = rs - r + NA_KH - 1
        qrow = pl.multiple_of(u * GRID_W, GRID_W)
        for g in range(C_HEADS // 2):
            qg = q_ref[pl.ds(qrow, GRID_W), g * LANES:(g + 1) * LANES]
            qs = jnp.concatenate([jnp.where(lo_lanes, qg, jnp.zeros_like(qg)),
                                  jnp.where(lo_lanes, jnp.zeros_like(qg), qg)], axis=0)
            kg = kcat[pl.ds(off, kwin_len), g * LANES:(g + 1) * LANES]
            vg = vcat[pl.ds(off, kwin_len), g * LANES:(g + 1) * LANES]
            sc = lax.dot_general(kg, qs, (((1,), (1,)), ((), ())), preferred_element_type=F32)
            sc = sc + jnp.concatenate([bias_ref[g * nrel + bvar + kr] for kr in range(NA_KH)], axis=0)
            m = jnp.max(sc, axis=0, keepdims=True)
            e = jnp.exp(sc - m)
            den = jnp.sum(e, axis=0, keepdims=True)
            p = (e * (1.0 / den)).astype(BF16)
            pv = lax.dot_general(p, vg, (((0,), (0,)), ((), ())), preferred_element_type=F32)
            o = jnp.where(lo_lanes, pv[0:GRID_W], pv[GRID_W:])
            o_ref[pl.ds(qrow, GRID_W), g * LANES:(g + 1) * LANES] = o.astype(o_ref.dtype)
        return carry

    lax.fori_loop(0, NA_TROWS, row_body, 0)


def _na_attn(q, k, v, bias_tab, tile_r0, tile_nr):
    t = q.shape[0]
    halo = NA_KH // 2
    tq = NA_TROWS * GRID_W
    hrows = halo * GRID_W
    per = tq // hrows
    nhb = t // hrows
    nt = t // tq
    qmap = lambda j, a, b: (j, 0)
    pmap = lambda j, a, b: (jnp.maximum(j * per - 1, 0), 0)
    nmap = lambda j, a, b: (jnp.minimum((j + 1) * per, nhb - 1), 0)
    gs = pltpu.PrefetchScalarGridSpec(
        num_scalar_prefetch=2, grid=(nt,),
        in_specs=[pl.BlockSpec((tq, C_W), qmap),
                  pl.BlockSpec((hrows, C_W), pmap), pl.BlockSpec((tq, C_W), qmap), pl.BlockSpec((hrows, C_W), nmap),
                  pl.BlockSpec((hrows, C_W), pmap), pl.BlockSpec((tq, C_W), qmap), pl.BlockSpec((hrows, C_W), nmap),
                  pl.BlockSpec(bias_tab.shape, lambda j, a, b: (0, 0, 0))],
        out_specs=pl.BlockSpec((tq, C_W), qmap),
        scratch_shapes=[pltpu.VMEM((tq + 2 * hrows, C_W), BF16), pltpu.VMEM((tq + 2 * hrows, C_W), BF16)])
    return pl.pallas_call(
        functools.partial(_na_kernel, halo=halo),
        grid_spec=gs,
        out_shape=jax.ShapeDtypeStruct((t, C_W), BF16),
        compiler_params=_cparams(("arbitrary",)),
        name="na_attn",
    )(tile_r0, tile_nr, q, k, k, k, v, v, v, bias_tab)


def _na_bias_table(rpb):
    c = np.arange(GRID_W)
    cs = np.clip(c - NA_KW // 2, 0, GRID_W - NA_KW)
    kc = np.arange(GRID_W)
    valid = (kc[:, None] >= cs[None, :]) & (kc[:, None] < cs[None, :] + NA_KW)
    cidx = np.clip(kc[:, None] - c[None, :] + NA_KW - 1, 0, 2 * NA_KW - 2)
    tab = rpb.astype(F32)[:, :, cidx]
    tab = jnp.where(jnp.asarray(valid)[None, None], tab, NEG_INF)
    pairs = jnp.concatenate([tab[0::2], tab[1::2]], axis=-1)
    return pairs.reshape((C_HEADS // 2) * (2 * NA_KH - 1), GRID_W, LANES)


def _route(h2, wr_ref, br_ref, idx_ref, gate_ref):
    tm = h2.shape[0]
    logits = lax.dot_general(wr_ref[...], h2, (((1,), (1,)), ((), ())),
                             precision=HIGHEST, preferred_element_type=F32) + br_ref[:, 0:1]
    eid = lax.broadcasted_iota(jnp.int32, (N_EXPERTS, tm), 0)
    vals, idxs = [], []
    for _ in range(TOP_K):
        m = jnp.max(logits, axis=0, keepdims=True)
        ix = jnp.min(jnp.where(logits == m, eid, N_EXPERTS), axis=0, keepdims=True)
        vals.append(m)
        idxs.append(ix)
        logits = jnp.where(eid == ix, -jnp.inf, logits)
    es = [jnp.exp(vk - vals[0]) for vk in vals]
    den = es[0] + es[1] + es[2] + es[3]
    pad_i = jnp.zeros((8 - TOP_K, tm), jnp.int32)
    pad_f = jnp.zeros((8 - TOP_K, tm), F32)
    idx_ref[...] = jnp.concatenate(idxs + [pad_i], axis=0)
    gate_ref[...] = jnp.concatenate([ek / den for ek in es] + [pad_f], axis=0)


def _out0_kernel(sid_ref, x_ref, mod_ref, n2_ref, oa_ref, ob0_ref, ls0_ref, ob1_ref, ls1_ref, ob2_ref, ls2_ref,
                 wout_ref, wr_ref, br_ref, x1_ref, h2_ref, idx_ref, gate_ref, oscr, lscr, *, tm):
    del sid_ref
    os_, ls_ = [], []
    for gi, (o_ref, l_ref) in enumerate(((ob0_ref, ls0_ref), (ob1_ref, ls1_ref), (ob2_ref, ls2_ref))):
        dil = B_PAIRS[gi][1]
        if dil == 1:
            os_.append(o_ref[...])
            ls_.append(l_ref[...])
        else:
            ncg = B_GW // LANES
            for r in range(dil):
                for c in range(ncg):
                    lo = r * B_GW + c * LANES
                    oscr[gi * ncg + c, pl.ds(r, tm // dil, stride=dil), :] = o_ref[:, lo:lo + LANES]
                lscr[gi, pl.ds(r, tm // dil, stride=dil), :] = l_ref[:, r * LANES:(r + 1) * LANES]
            os_.append(jnp.concatenate([oscr[gi * ncg + c] for c in range(ncg)], axis=1))
            ls_.append(lscr[gi])
    lmax = jnp.maximum(jnp.maximum(ls_[0], ls_[1]), ls_[2])
    ws = [jnp.exp(l - lmax) for l in ls_]
    winv = 1.0 / (ws[0] + ws[1] + ws[2])
    lane = lax.broadcasted_iota(jnp.int32, (tm, LANES), 1)
    lo_lanes = lane < HEAD_DIM

    def head_weights(w):
        return jnp.concatenate(
            [jnp.where(lo_lanes, jnp.broadcast_to(w[:, 2 * pr:2 * pr + 1], (tm, LANES)),
                       jnp.broadcast_to(w[:, 2 * pr + 1:2 * pr + 2], (tm, LANES)))
             for pr in range(B_HEADS // 2)], axis=1)

    ob = (head_weights(ws[0] * winv) * os_[0] + head_weights(ws[1] * winv) * os_[1]
          + head_weights(ws[2] * winv) * os_[2])
    o = (jnp.dot(oa_ref[...], wout_ref[0:A_QW, :], preferred_element_type=F32)
         + jnp.dot(ob.astype(BF16), wout_ref[A_QW:, :], preferred_element_type=F32))
    x1 = x_ref[...] + mod_ref[2:3, :] * o
    x1_ref[...] = x1
    h2 = _rms_mod(x1, n2_ref, mod_ref, 3, 4)
    h2_ref[...] = h2
    _route(h2, wr_ref, br_ref, idx_ref, gate_ref)


def _out1_kernel(sid_ref, x_ref, mod_ref, n2_ref, oc_ref, wout_ref, wr_ref, br_ref,
                 x1_ref, h2_ref, idx_ref, gate_ref):
    del sid_ref
    o = jnp.dot(oc_ref[...], wout_ref[...], preferred_element_type=F32)
    x1 = x_ref[...] + mod_ref[2:3, :] * o
    x1_ref[...] = x1
    h2 = _rms_mod(x1, n2_ref, mod_ref, 3, 4)
    h2_ref[...] = h2
    _route(h2, wr_ref, br_ref, idx_ref, gate_ref)


def _out_proj(layer, x, mod, tile_sid, n2, attn, wout_bf, wr_t, br):
    t = x.shape[0]
    tm = TOKEN_TILE
    row = lambda i, sid: (i, 0)
    const = lambda i, sid: (0, 0)
    in_specs = [pl.BlockSpec((tm, D_MODEL), row),
                pl.BlockSpec((None, 6, D_MODEL), lambda i, sid: (sid[i], 0, 0)),
                pl.BlockSpec((1, D_MODEL), const)]
    scratch = []
    if layer == 0:
        in_specs.append(pl.BlockSpec((tm, A_QW), row))
        for (_, dil) in B_PAIRS:
            in_specs += [pl.BlockSpec((tm // dil, dil * B_GW), row), pl.BlockSpec((tm // dil, dil * LANES), row)]
        body = functools.partial(_out0_kernel, tm=tm)
        nscr = B_GROUPS * B_GW // LANES
        scratch = [pltpu.VMEM((nscr, tm, LANES), F32), pltpu.VMEM((B_GROUPS, tm, LANES), F32)]
    else:
        in_specs.append(pl.BlockSpec((tm, C_W), row))
        body = _out1_kernel
    in_specs += [pl.BlockSpec(wout_bf.shape, const),
                 pl.BlockSpec((N_EXPERTS, D_MODEL), const),
                 pl.BlockSpec((N_EXPERTS, LANES), const)]
    gs = pltpu.PrefetchScalarGridSpec(
        num_scalar_prefetch=1, grid=(t // tm,), in_specs=in_specs,
        out_specs=[pl.BlockSpec((tm, D_MODEL), row), pl.BlockSpec((tm, D_MODEL), row),
                   pl.BlockSpec((8, tm), lambda i, sid: (0, i)), pl.BlockSpec((8, tm), lambda i, sid: (0, i))],
        scratch_shapes=scratch)
    return pl.pallas_call(
        body, grid_spec=gs,
        out_shape=[jax.ShapeDtypeStruct((t, D_MODEL), F32), jax.ShapeDtypeStruct((t, D_MODEL), F32),
                   jax.ShapeDtypeStruct((8, t), jnp.int32), jax.ShapeDtypeStruct((8, t), F32)],
        compiler_params=_cparams(("arbitrary",)),
        name="out_proj%d" % layer,
    )(tile_sid, x, mod, n2.reshape(1, D_MODEL), *attn, wout_bf, wr_t, br)


def _rank_kernel(idx_ref, tri_ref, rank_ref, cnt_ref, carry):
    i = pl.program_id(0)

    @pl.when(i == 0)
    def _():
        carry[...] = jnp.zeros_like(carry)

    tk = idx_ref.shape[1]
    eid = lax.broadcasted_iota(jnp.int32, (N_EXPERTS, tk), 0)
    hits = [eid == idx_ref[k:k + 1, :] for k in range(TOP_K)]
    onehot = sum(h.astype(F32) for h in hits)
    incl = jnp.dot(onehot.astype(BF16), tri_ref[...], preferred_element_type=F32)
    before = incl - onehot + carry[:, 0:1]
    rows = [jnp.sum(jnp.where(h, before, 0.0), axis=0, keepdims=True) for h in hits]
    rows.append(jnp.zeros((8 - TOP_K, tk), F32))
    rank_ref[...] = jnp.concatenate(rows, axis=0).astype(jnp.int32)
    carry[...] = carry[...] + incl[:, tk - 1:tk]
    cnt_ref[...] = carry[...].astype(jnp.int32)


def _slot_kernel(idx_ref, rank_ref, start_ref, slot_ref):
    tk = idx_ref.shape[1]
    eid = lax.broadcasted_iota(jnp.int32, (N_EXPERTS, tk), 0)
    start = start_ref[:, 0:1]
    rows = []
    for k in range(TOP_K):
        base = jnp.sum(jnp.where(eid == idx_ref[k:k + 1, :], start, 0), axis=0, keepdims=True)
        rows.append(base + rank_ref[k:k + 1, :])
    rows.append(jnp.zeros((8 - TOP_K, tk), jnp.int32))
    slot_ref[...] = jnp.concatenate(rows, axis=0)


def _route_slots(idx):
    t = idx.shape[1]
    tk = ROUTE_TILE
    tri = jnp.asarray(np.triu(np.ones((tk, tk), np.float32)), BF16)
    tok = lambda i: (0, i)
    rank, cnt = pl.pallas_call(
        _rank_kernel, grid=(t // tk,),
        in_specs=[pl.BlockSpec((8, tk), tok), pl.BlockSpec((tk, tk), lambda i: (0, 0))],
        out_specs=[pl.BlockSpec((8, tk), tok), pl.BlockSpec((N_EXPERTS, LANES), lambda i: (0, 0))],
        out_shape=[jax.ShapeDtypeStruct((8, t), jnp.int32), jax.ShapeDtypeStruct((N_EXPERTS, LANES), jnp.int32)],
        scratch_shapes=[pltpu.VMEM((N_EXPERTS, LANES), F32)],
        compiler_params=_cparams(("arbitrary",)),
        name="route_rank",
    )(idx, tri)
    counts = cnt[:, 0]
    padded = (counts + MOE_BM - 1) // MOE_BM * MOE_BM
    pad_end = jnp.cumsum(padded)
    pad_start = pad_end - padded
    nblk = (t * TOP_K) // MOE_BM + N_EXPERTS
    blk_row0 = jnp.arange(nblk, dtype=jnp.int32) * MOE_BM
    blk_expert = jnp.minimum(jnp.sum((pad_end[None, :] <= blk_row0[:, None]).astype(jnp.int32), axis=1),
                             N_EXPERTS - 1)
    nused = (pad_end[-1] // MOE_BM).astype(jnp.int32).reshape(1)
    start_b = jnp.broadcast_to(pad_start.astype(jnp.int32)[:, None], (N_EXPERTS, LANES))
    slot = pl.pallas_call(
        _slot_kernel, grid=(t // tk,),
        in_specs=[pl.BlockSpec((8, tk), tok), pl.BlockSpec((8, tk), tok),
                  pl.BlockSpec((N_EXPERTS, LANES), lambda i: (0, 0))],
        out_specs=pl.BlockSpec((8, tk), tok),
        out_shape=jax.ShapeDtypeStruct((8, t), jnp.int32),
        compiler_params=_cparams(("arbitrary",)),
        name="route_slot",
    )(idx, rank, start_b)
    return slot, blk_expert, nused, pad_end.astype(jnp.int32), padded.astype(jnp.int32)


def _row_copy(src, dst, sem):
    return pltpu.make_async_copy(src, dst, sem)


ROW_GROUP = 8


def _for_each_row_copy(n_tok, fn):
    def group(j, carry):
        base = pl.multiple_of(j * ROW_GROUP, ROW_GROUP)
        for u in range(ROW_GROUP):
            for k in range(TOP_K):
                fn(base + u, k, (u * TOP_K + k) % 2)
        return carry

    lax.fori_loop(0, n_tok // ROW_GROUP, group, 0)


def _dispatch_kernel(pend_ref, padded_ref, slot_ref, h_ref, xs_ref, zbuf, sem, zsem, *, tm, nblk):
    i = pl.program_id(0)

    @pl.when(i == 0)
    def _():
        zbuf[...] = jnp.zeros_like(zbuf)

        def zero_block(b):
            cp = _row_copy(zbuf, xs_ref.at[pl.ds(pl.multiple_of(b * MOE_BM, MOE_BM), MOE_BM)], zsem)
            cp.start()
            cp.wait()

        for e in range(N_EXPERTS):
            @pl.when(padded_ref[e] > 0)
            def _():
                zero_block(pend_ref[e] // MOE_BM - 1)

        def tail(b, carry):
            zero_block(b)
            return carry

        lax.fori_loop(pend_ref[N_EXPERTS - 1] // MOE_BM, nblk, tail, 0)

    def issue(tk, k, priority):
        _row_copy(h_ref.at[pl.ds(tk, 1)], xs_ref.at[pl.ds(slot_ref[k, tk], 1)], sem).start(priority=priority)

    _for_each_row_copy(tm, issue)
    for k in range(TOP_K):
        _row_copy(h_ref, xs_ref.at[pl.ds(0, tm)], sem).wait()


def _dispatch(h2, slot, pad_end, padded, nslots):
    t = h2.shape[0]
    tm = MOVE_TILE
    gs = pltpu.PrefetchScalarGridSpec(
        num_scalar_prefetch=2, grid=(t // tm,),
        in_specs=[pl.BlockSpec((8, tm), lambda i, a, b: (0, i), memory_space=pltpu.SMEM),
                  pl.BlockSpec((tm, D_MODEL), lambda i, a, b: (i, 0))],
        out_specs=pl.BlockSpec(memory_space=pl.ANY),
        scratch_shapes=[pltpu.VMEM((MOE_BM, D_MODEL), F32), pltpu.SemaphoreType.DMA, pltpu.SemaphoreType.DMA])
    return pl.pallas_call(
        functools.partial(_dispatch_kernel, tm=tm, nblk=nslots // MOE_BM), grid_spec=gs,
        out_shape=jax.ShapeDtypeStruct((nslots, D_MODEL), F32),
        compiler_params=_cparams(("arbitrary",)),
        name="moe_dispatch",
    )(pad_end, padded, slot, h2)


def _ffn_kernel(be_ref, nu_ref, xs_ref, wgu_ref, bgu_ref, wd_ref, bd_ref, y_ref, wgu_bf, wd_bf):
    i = pl.program_id(0)
    prev = be_ref[jnp.maximum(i - 1, 0)]

    @pl.when(i < nu_ref[0])
    def _():
        @pl.when((i == 0) | (be_ref[i] != prev))
        def _():
            wgu_bf[...] = wgu_ref[...].astype(BF16)
            wd_bf[...] = wd_ref[...].astype(BF16)

        x = xs_ref[...].astype(BF16)
        gu = jnp.dot(x, wgu_bf[...], preferred_element_type=F32) + bgu_ref[...]
        gate = jnp.minimum(gu[:, :D_FF], SWIGLU_LIMIT)
        up = jnp.clip(gu[:, D_FF:], -SWIGLU_LIMIT, SWIGLU_LIMIT)
        act = (up + 1.0) * gate * jax.nn.sigmoid(SWIGLU_ALPHA * gate)
        y_ref[...] = jnp.dot(act.astype(BF16), wd_bf[...], preferred_element_type=F32) + bd_ref[...]

    @pl.when(i >= nu_ref[0])
    def _():
        y_ref[...] = jnp.zeros_like(y_ref)


def _expert_ffn(xs, blk_expert, nused, wgu, bgu, wd, bd):
    nslots = xs.shape[0]
    nblk = nslots // MOE_BM
    blk = lambda i, be, nu: (jnp.minimum(i, nu[0] - 1), 0)
    exp3 = lambda i, be, nu: (be[jnp.minimum(i, nu[0] - 1)], 0, 0)
    gs = pltpu.PrefetchScalarGridSpec(
        num_scalar_prefetch=2, grid=(nblk,),
        in_specs=[pl.BlockSpec((MOE_BM, D_MODEL), blk),
                  pl.BlockSpec((None, D_MODEL, 2 * D_FF), exp3),
                  pl.BlockSpec((None, 1, 2 * D_FF), exp3),
                  pl.BlockSpec((None, D_FF, D_MODEL), exp3),
                  pl.BlockSpec((None, 1, D_MODEL), exp3)],
        out_specs=pl.BlockSpec((MOE_BM, D_MODEL), lambda i, be, nu: (i, 0)),
        scratch_shapes=[pltpu.VMEM((D_MODEL, 2 * D_FF), BF16), pltpu.VMEM((D_FF, D_MODEL), BF16)])
    return pl.pallas_call(
        _ffn_kernel, grid_spec=gs,
        out_shape=jax.ShapeDtypeStruct((nslots, D_MODEL), F32),
        compiler_params=_cparams(("arbitrary",)),
        name="moe_ffn",
    )(blk_expert, nused, xs, wgu, bgu.reshape(N_EXPERTS, 1, 2 * D_FF), wd, bd.reshape(N_EXPERTS, 1, D_MODEL))


def _combine_kernel(sid_ref, slot_ref, x_ref, mod_ref, gate_ref, ys_ref, o_ref, buf, sem, *, tm):
    del sid_ref

    def issue(tk, k, priority):
        _row_copy(ys_ref.at[pl.ds(slot_ref[k, tk], 1)], buf.at[k, pl.ds(tk, 1)], sem).start(priority=priority)

    _for_each_row_copy(tm, issue)
    gpad = jnp.concatenate([gate_ref[...], jnp.zeros((LANES - 8, tm), F32)], axis=0)
    gcol = gpad.T
    for k in range(TOP_K):
        _row_copy(ys_ref.at[pl.ds(0, tm)], buf.at[k], sem).wait()
    acc = gcol[:, 0:1] * buf[0]
    for k in range(1, TOP_K):
        acc = acc + gcol[:, k:k + 1] * buf[k]
    o_ref[...] = x_ref[...] + mod_ref[5:6, :] * acc


def _combine(x1, mod, tile_sid, slot, gates, ys):
    t = x1.shape[0]
    tm = MOVE_TILE
    gs = pltpu.PrefetchScalarGridSpec(
        num_scalar_prefetch=1, grid=(t // tm,),
        in_specs=[pl.BlockSpec((8, tm), lambda i, sid: (0, i), memory_space=pltpu.SMEM),
                  pl.BlockSpec((tm, D_MODEL), lambda i, sid: (i, 0)),
                  pl.BlockSpec((None, 6, D_MODEL), lambda i, sid: (sid[i], 0, 0)),
                  pl.BlockSpec((8, tm), lambda i, sid: (0, i)),
                  pl.BlockSpec(memory_space=pl.ANY)],
        out_specs=pl.BlockSpec((tm, D_MODEL), lambda i, sid: (i, 0)),
        scratch_shapes=[pltpu.VMEM((TOP_K, tm, D_MODEL), F32), pltpu.SemaphoreType.DMA])
    return pl.pallas_call(
        functools.partial(_combine_kernel, tm=tm), grid_spec=gs,
        out_shape=jax.ShapeDtypeStruct((t, D_MODEL), F32),
        compiler_params=_cparams(("arbitrary",)),
        name="moe_combine",
    )(tile_sid, slot, x1, mod, gates, ys)


def _moe(x1, h2, idx, gates, mod, tile_sid_move, wgu, bgu, wd, bd):
    t = x1.shape[0]
    nslots = t * TOP_K + N_EXPERTS * MOE_BM
    slot, blk_expert, nused, pad_end, padded = _route_slots(idx)
    xs = _dispatch(h2, slot, pad_end, padded, nslots)
    ys = _expert_ffn(xs, blk_expert, nused, wgu, bgu, wd, bd)
    return _combine(x1, mod, tile_sid_move, slot, gates, ys)


def _tile_table(seq_lens, tile, fn):
    vals = []
    start = 0
    for sid, n in enumerate(seq_lens):
        assert n % tile == 0
        for j in range(n // tile):
            vals.append(fn(sid, start, n, j))
        start += n
    return jnp.asarray(np.asarray(vals, np.int32))


def _head_block_diag():
    bd = np.kron(np.eye(2 * LANES // HEAD_DIM, dtype=np.float32),
                 np.full((HEAD_DIM, HEAD_DIM), 1.0 / HEAD_DIM, np.float32))
    return jnp.asarray(bd, BF16)


_A_HEAD_ORDER = [kv * A_GROUP + g for g in range(A_GROUP) for kv in range(A_KV_HEADS)]


def _trunk(x, c_pad, seq_lens, p):
    t = x.shape[0]
    sid_tok = _tile_table(seq_lens, TOKEN_TILE, lambda sid, s, n, j: sid)
    sid_move = _tile_table(seq_lens, MOVE_TILE, lambda sid, s, n, j: sid)
    bd = _head_block_diag()

    mod0 = _ada_mod(c_pad, p["l0_ada_w"], p["l0_ada_b"]).reshape(c_pad.shape[0], 6, D_MODEL)
    w_in = p["l0_w_in"]
    a_cols = np.concatenate([np.arange(h * HEAD_DIM, (h + 1) * HEAD_DIM) for h in _A_HEAD_ORDER])
    bq, bk, bv = A_IN, A_IN + B_GROUPS * B_GW, A_IN + 2 * B_GROUPS * B_GW
    col_blocks = [w_in[:, a_cols], w_in[:, A_QW:A_IN]]
    for g in range(B_GROUPS):
        col_blocks += [w_in[:, bq + g * B_GW:bq + (g + 1) * B_GW],
                       w_in[:, bk + g * B_GW:bk + (g + 1) * B_GW],
                       w_in[:, bv + g * B_GW:bv + (g + 1) * B_GW]]
    w0 = jnp.concatenate(col_blocks, axis=1).astype(BF16)
    ones = lambda n: jnp.ones((n,), F32)
    gain_blocks = [jnp.tile(p["l0_q_norm_a"], A_Q_HEADS) * QK_SCALE, jnp.tile(p["l0_k_norm_a"], A_KV_HEADS),
                   ones(A_KVW)]
    for g in range(B_GROUPS):
        gain_blocks += [jnp.tile(p["l0_q_norm_b"], B_HEADS) * QK_SCALE, jnp.tile(p["l0_k_norm_b"], B_HEADS),
                        ones(B_GW)]
    gain0 = jnp.concatenate(gain_blocks).reshape(1, -1)
    slabs0 = [(0, A_QW, True, 1), (A_QW, A_KVW, True, 1), (A_QW + A_KVW, A_KVW, False, 1)]
    c0 = A_IN
    for g, (_, dil) in enumerate(B_PAIRS):
        slabs0 += [(c0, B_GW, True, dil), (c0 + B_GW, B_GW, True, dil), (c0 + 2 * B_GW, B_GW, False, dil)]
        c0 += 3 * B_GW
    proj = _in_proj(x, mod0, sid_tok, p["l0_norm1"], w0, gain0, bd, tuple(slabs0))
    qa, ka, va = proj[0:3]

    slopes_a = _alibi_slopes(A_Q_HEADS)
    heads_a = tuple(tuple((slopes_a[kv * A_GROUP + g], kv * A_GROUP + g) for kv in range(A_KV_HEADS))
                    for g in range(A_GROUP))
    (oa,) = _band_attn(qa, ka, va, seq_lens, tq=ATTN_TQ_A, dil=1, qw=A_QW, kvw=A_KVW, half=A_HALF, heads=heads_a,
                       kv_group=(0,) * A_GROUP, sink=p["l0_sink_a"].astype(F32), out_dtype=BF16, want_lse=False)

    slopes_b = _alibi_slopes(B_GROUPS * B_HEADS)
    attn0 = [oa]
    for g, (window, dil) in enumerate(B_PAIRS):
        qg, kg, vg = proj[3 + 3 * g:6 + 3 * g]
        heads_b = tuple(tuple((slopes_b[g * B_HEADS + 2 * pr + s] * dil, 0) for s in range(2))
                        for pr in range(B_HEADS // 2))
        og, lg = _band_attn(qg, kg, vg, [n // dil for n in seq_lens], tq=ATTN_TQ_B, dil=dil, qw=B_GW, kvw=B_GW, half=window // (2 * dil),
                            heads=heads_b, kv_group=tuple(range(B_HEADS // 2)), sink=None, out_dtype=F32,
                            want_lse=True)
        attn0 += [og, lg]

    w_out = p["l0_w_out"]
    wout0 = jnp.concatenate([w_out[a_cols], w_out[A_QW:]], axis=0).astype(BF16)
    br0 = jnp.broadcast_to(p["l0_b_router"].astype(F32)[:, None], (N_EXPERTS, LANES))
    x1, h2, idx, gates = _out_proj(0, x, mod0, sid_tok, p["l0_norm2"], attn0, wout0, p["l0_w_router"].T, br0)
    x = _moe(x1, h2, idx, gates, mod0, sid_move, p["l0_w_gate_up"], p["l0_b_gate_up"], p["l0_w_down"],
             p["l0_b_down"])

    mod1 = _ada_mod(c_pad, p["l1_ada_w"], p["l1_ada_b"]).reshape(c_pad.shape[0], 6, D_MODEL)
    w1 = p["l1_w_in"].astype(BF16)
    gain1 = jnp.concatenate([jnp.tile(p["l1_q_norm_c"], C_HEADS) * QK_SCALE, jnp.tile(p["l1_k_norm_c"], C_HEADS),
                             ones(C_W)]).reshape(1, -1)
    slabs1 = ((0, C_W, True, 1), (C_W, C_W, True, 1), (2 * C_W, C_W, False, 1))
    qc, kc, vc = _in_proj(x, mod1, sid_tok, p["l1_norm1"], w1, gain1, bd, slabs1)
    na_tile = NA_TROWS * GRID_W
    tile_r0 = _tile_table(seq_lens, na_tile, lambda sid, s, n, j: j * NA_TROWS)
    tile_nr = _tile_table(seq_lens, na_tile, lambda sid, s, n, j: n // GRID_W)
    oc = _na_attn(qc, kc, vc, _na_bias_table(p["l1_rpb_c"]), tile_r0, tile_nr)
    br1 = jnp.broadcast_to(p["l1_b_router"].astype(F32)[:, None], (N_EXPERTS, LANES))
    x1, h2, idx, gates = _out_proj(1, x, mod1, sid_tok, p["l1_norm2"], [oc], p["l1_w_out"].astype(BF16),
                                   p["l1_w_router"].T, br1)
    return _moe(x1, h2, idx, gates, mod1, sid_move, p["l1_w_gate_up"], p["l1_b_gate_up"], p["l1_w_down"],
                p["l1_b_down"])


def kernel(x_prompt, x_sample, c_prompt, c_sample, l0_ada_w, l0_ada_b, l0_norm1, l0_w_in, l0_q_norm_a, l0_k_norm_a, l0_sink_a, l0_q_norm_b, l0_k_norm_b, l0_w_out, l0_norm2, l0_w_router, l0_b_router, l0_w_gate_up, l0_b_gate_up, l0_w_down, l0_b_down, l1_ada_w, l1_ada_b, l1_norm1, l1_w_in, l1_q_norm_c, l1_k_norm_c, l1_rpb_c, l1_w_out, l1_norm2, l1_w_router, l1_b_router, l1_w_gate_up, l1_b_gate_up, l1_w_down, l1_b_down):
    p = dict(l0_ada_w=l0_ada_w, l0_ada_b=l0_ada_b, l0_norm1=l0_norm1, l0_w_in=l0_w_in, l0_q_norm_a=l0_q_norm_a,
             l0_k_norm_a=l0_k_norm_a, l0_sink_a=l0_sink_a, l0_q_norm_b=l0_q_norm_b, l0_k_norm_b=l0_k_norm_b,
             l0_w_out=l0_w_out, l0_norm2=l0_norm2, l0_w_router=l0_w_router, l0_b_router=l0_b_router,
             l0_w_gate_up=l0_w_gate_up, l0_b_gate_up=l0_b_gate_up, l0_w_down=l0_w_down, l0_b_down=l0_b_down,
             l1_ada_w=l1_ada_w, l1_ada_b=l1_ada_b, l1_norm1=l1_norm1, l1_w_in=l1_w_in, l1_q_norm_c=l1_q_norm_c,
             l1_k_norm_c=l1_k_norm_c, l1_rpb_c=l1_rpb_c, l1_w_out=l1_w_out, l1_norm2=l1_norm2,
             l1_w_router=l1_w_router, l1_b_router=l1_b_router, l1_w_gate_up=l1_w_gate_up,
             l1_b_gate_up=l1_b_gate_up, l1_w_down=l1_w_down, l1_b_down=l1_b_down)
    nb_p, len_p, d = x_prompt.shape
    nb_s, len_s, _ = x_sample.shape
    seq_lens = (len_p,) * nb_p + (len_s,) * nb_s
    x = jnp.concatenate([x_prompt.reshape(nb_p * len_p, d), x_sample.reshape(nb_s * len_s, d)], axis=0)
    nseq = nb_p + nb_s
    c_pad = jnp.concatenate([c_prompt, c_sample, jnp.zeros((-nseq % 8, d), F32)], axis=0)
    y = _trunk(x, c_pad, seq_lens, p)
    y_prompt = y[:nb_p * len_p].reshape(nb_p, len_p, d)
    y_sample = y[nb_p * len_p:].reshape(nb_s, len_s, d)
    return (y_prompt, y_sample)
```

```python
import functools
import math

import numpy as np
import jax
import jax.numpy as jnp
from jax import lax
from jax.experimental import pallas as pl
from jax.experimental.pallas import tpu as pltpu
from jax.experimental.pallas import tpu_sc as plsc

F32 = jnp.float32
BF16 = jnp.bfloat16
HIGHEST = lax.Precision.HIGHEST

D_MODEL = 1024
HEAD_DIM = 64
LANES = 128
GRID_W = 64
A_Q_HEADS = 8
A_KV_HEADS = 2
A_GROUP = A_Q_HEADS // A_KV_HEADS
A_HALF = 128
B_PAIRS = ((128, 1), (512, 4), (2048, 16))
B_GROUPS = 3
B_HEADS = 4
A_QW = A_Q_HEADS * HEAD_DIM
A_KVW = A_KV_HEADS * HEAD_DIM
A_IN = A_QW + 2 * A_KVW
B_GW = B_HEADS * HEAD_DIM
C_HEADS = 16
C_W = C_HEADS * HEAD_DIM
NA_KH = 8
NA_KW = 16
N_EXPERTS = 32
TOP_K = 4
D_FF = 1024
SWIGLU_LIMIT = 7.0
SWIGLU_ALPHA = 1.702
RMS_EPS = 1e-6
NEG_INF = -1e30
QK_SCALE = HEAD_DIM ** -0.5

TOKEN_TILE = 512
ATTN_TQ_A = 256
ATTN_TQ_B = 512
ATTN_UQ = 128
NA_TROWS = 8
MOE_BM = 256
ROUTE_TILE = 512
MOVE_TILE = 256
VMEM_LIMIT = 56 * 1024 * 1024


def _alibi_slopes(n):
    return [float(2.0 ** (-8.0 * (j + 1) / n)) for j in range(n)]


def _cparams(sem, flags=None):
    return pltpu.CompilerParams(dimension_semantics=sem, vmem_limit_bytes=VMEM_LIMIT, flags=flags)


def _mod_kernel(c_ref, w_ref, b_ref, o_ref):
    c = c_ref[...]
    s = c * jax.nn.sigmoid(c)
    o_ref[...] = jnp.dot(s, w_ref[...], precision=HIGHEST, preferred_element_type=F32) + b_ref[...]


def _ada_mod(c_pad, w, b):
    nrow = c_pad.shape[0]
    ncol = w.shape[1]
    tn = D_MODEL
    return pl.pallas_call(
        _mod_kernel,
        grid=(ncol // tn,),
        in_specs=[pl.BlockSpec((nrow, D_MODEL), lambda j: (0, 0)),
                  pl.BlockSpec((D_MODEL, tn), lambda j: (0, j)),
                  pl.BlockSpec((1, tn), lambda j: (0, j))],
        out_specs=pl.BlockSpec((nrow, tn), lambda j: (0, j)),
        out_shape=jax.ShapeDtypeStruct((nrow, ncol), F32),
        compiler_params=_cparams(("arbitrary",)),
        name="ada_mod",
    )(c_pad, w, b.reshape(1, ncol))


def _head_mean_sq(y, bd_ref):
    w = y.shape[1]
    outs = []
    for c0 in range(0, w, 2 * LANES):
        cw = min(2 * LANES, w - c0)
        sq = y[:, c0:c0 + cw] * y[:, c0:c0 + cw]
        hi = sq.astype(BF16)
        lo = (sq - hi.astype(F32)).astype(BF16)
        bd = bd_ref[0:cw, 0:cw]
        outs.append(jnp.dot(hi, bd, preferred_element_type=F32)
                    + jnp.dot(lo, bd, preferred_element_type=F32))
    return outs[0] if len(outs) == 1 else jnp.concatenate(outs, axis=1)


def _rms_mod(x, g_ref, mod_ref, shift_row, scale_row):
    ms = jnp.mean(x * x, axis=-1, keepdims=True)
    xn = x * lax.rsqrt(ms + RMS_EPS) * g_ref[...]
    return xn * (1.0 + mod_ref[scale_row:scale_row + 1, :]) + mod_ref[shift_row:shift_row + 1, :]


def _in_kernel(sid_ref, x_ref, mod_ref, n1_ref, w_ref, gain_ref, bd_ref, *rest, slabs, tm):
    del sid_ref
    nout = len(slabs)
    out_refs = rest[:nout]
    scr_ref = rest[nout] if len(rest) > nout else None
    h = _rms_mod(x_ref[...], n1_ref, mod_ref, 0, 1).astype(BF16)
    for (c0, w, normed, dil), o_ref in zip(slabs, out_refs):
        y = jnp.dot(h, w_ref[:, c0:c0 + w], preferred_element_type=F32)
        if normed:
            y = y * lax.rsqrt(_head_mean_sq(y, bd_ref) + RMS_EPS) * gain_ref[:, c0:c0 + w]
        if dil == 1:
            o_ref[...] = y.astype(BF16)
        else:
            for c in range(w // LANES):
                scr_ref[c] = y[:, c * LANES:(c + 1) * LANES]
            for r in range(dil):
                for c in range(w // LANES):
                    o_ref[:, r * w + c * LANES:r * w + (c + 1) * LANES] = (
                        scr_ref[c, pl.ds(r, tm // dil, stride=dil), :].astype(BF16))


def _in_proj(x, mod, tile_sid, n1, w_bf, gain, bd, slabs):
    t = x.shape[0]
    tm = TOKEN_TILE
    ncols = w_bf.shape[1]
    out_shapes, out_specs = [], []
    for (c0, w, normed, dil) in slabs:
        out_shapes.append(jax.ShapeDtypeStruct((t // dil, dil * w), BF16))
        out_specs.append(pl.BlockSpec((tm // dil, dil * w), lambda i, sid: (i, 0)))
    need_scr = any(s[3] > 1 for s in slabs)
    scratch = [pltpu.VMEM((max(s[1] for s in slabs if s[3] > 1) // LANES, tm, LANES), F32)] if need_scr else []
    gs = pltpu.PrefetchScalarGridSpec(
        num_scalar_prefetch=1,
        grid=(t // tm,),
        in_specs=[pl.BlockSpec((tm, D_MODEL), lambda i, sid: (i, 0)),
                  pl.BlockSpec((None, 6, D_MODEL), lambda i, sid: (sid[i], 0, 0)),
                  pl.BlockSpec((1, D_MODEL), lambda i, sid: (0, 0)),
                  pl.BlockSpec((D_MODEL, ncols), lambda i, sid: (0, 0)),
                  pl.BlockSpec((1, ncols), lambda i, sid: (0, 0)),
                  pl.BlockSpec((2 * LANES, 2 * LANES), lambda i, sid: (0, 0))],
        out_specs=out_specs,
        scratch_shapes=scratch)
    return pl.pallas_call(
        functools.partial(_in_kernel, slabs=slabs, tm=tm),
        grid_spec=gs,
        out_shape=out_shapes,
        compiler_params=_cparams(("arbitrary",)),
        name="in_proj",
    )(tile_sid, x, mod, n1.reshape(1, D_MODEL), w_bf, gain, bd)


def _band_kernel(s0_ref, s1_ref, *refs, tq, hb, half, heads, kv_group, has_sink, want_lse):
    pos = 0
    sink_ref = None
    if has_sink:
        sink_ref = refs[0]
        pos = 1
    q_ref, kp_ref, kc_ref, kn_ref, vp_ref, vc_ref, vn_ref = refs[pos:pos + 7]
    o_ref = refs[pos + 7]
    lse_ref = refs[pos + 8] if want_lse else None
    i = pl.program_id(1)
    uq = ATTN_UQ
    wlen = uq + 2 * hb
    kall = jnp.concatenate([kp_ref[...], kc_ref[...], kn_ref[...]], axis=0)
    vall = jnp.concatenate([vp_ref[...], vc_ref[...], vn_ref[...]], axis=0)
    lane = lax.broadcasted_iota(jnp.int32, (uq, LANES), 1)
    lo_lanes = lane < HEAD_DIM
    for sub in range(tq // uq):
        rows = slice(sub * uq, (sub + 1) * uq)
        kwin = kall[sub * uq:sub * uq + wlen]
        vwin = vall[sub * uq:sub * uq + wlen]
        q0 = i * tq + sub * uq
        kpos = q0 - hb + lax.broadcasted_iota(jnp.int32, (wlen, uq), 0)
        qpos = q0 + lax.broadcasted_iota(jnp.int32, (wlen, uq), 1)
        rel = kpos - qpos
        valid = (jnp.abs(rel) <= half) & (kpos >= s0_ref[i]) & (kpos < s1_ref[i])
        distm = jnp.where(valid, jnp.abs(rel).astype(F32), 1e32)
        lse_rows = []
        for g, pair in enumerate(heads):
            qg = q_ref[rows, g * LANES:(g + 1) * LANES]
            kg = kwin[:, kv_group[g] * LANES:(kv_group[g] + 1) * LANES]
            vg = vwin[:, kv_group[g] * LANES:(kv_group[g] + 1) * LANES]
            outs = []
            for s, (slope, sink_idx) in enumerate(pair):
                keep = lo_lanes if s == 0 else jnp.logical_not(lo_lanes)
                qm = jnp.where(keep, qg, jnp.zeros_like(qg))
                sc = lax.dot_general(kg, qm, (((1,), (1,)), ((), ())), preferred_element_type=F32)
                sc = sc - slope * distm
                m = jnp.max(sc, axis=0, keepdims=True)
                if has_sink:
                    m = jnp.maximum(m, sink_ref[sink_idx])
                e = jnp.exp(sc - m)
                den = jnp.sum(e, axis=0, keepdims=True)
                if has_sink:
                    den = den + jnp.exp(sink_ref[sink_idx] - m)
                p = (e * (1.0 / den)).astype(BF16)
                outs.append(lax.dot_general(p, vg, (((0,), (0,)), ((), ())), preferred_element_type=F32))
                if want_lse:
                    lse_rows.append(m + jnp.log(den))
            o_ref[rows, g * LANES:(g + 1) * LANES] = jnp.where(lo_lanes, outs[0], outs[1]).astype(o_ref.dtype)
        if want_lse:
            lse_t = jnp.concatenate(lse_rows + [jnp.zeros((LANES - len(lse_rows), uq), F32)], axis=0)
            lse_ref[rows, :] = lse_t.T


def _band_attn(q, k, v, seq_rows, *, tq, dil, qw, kvw, half, heads, kv_group, sink, out_dtype, want_lse):
    rows = q.shape[0]
    hb = half
    s0 = _tile_table(seq_rows, tq, lambda sid, s, n, j: s)
    s1 = _tile_table(seq_rows, tq, lambda sid, s, n, j: s + n)
    per = tq // hb
    nt = rows // tq
    nhb = rows // hb
    has_sink = sink is not None
    in_specs = []
    args = []
    if has_sink:
        in_specs.append(pl.BlockSpec(memory_space=pltpu.SMEM))
        args.append(sink)
    qmap = lambda r, i, a, b: (i, r)
    pmap = lambda r, i, a, b: (jnp.maximum(i * per - 1, 0), r)
    nmap = lambda r, i, a, b: (jnp.minimum((i + 1) * per, nhb - 1), r)
    in_specs += [pl.BlockSpec((tq, qw), qmap),
                 pl.BlockSpec((hb, kvw), pmap), pl.BlockSpec((tq, kvw), qmap), pl.BlockSpec((hb, kvw), nmap),
                 pl.BlockSpec((hb, kvw), pmap), pl.BlockSpec((tq, kvw), qmap), pl.BlockSpec((hb, kvw), nmap)]
    args += [q, k, k, k, v, v, v]
    out_shape = [jax.ShapeDtypeStruct((rows, dil * qw), out_dtype)]
    out_specs = [pl.BlockSpec((tq, qw), qmap)]
    if want_lse:
        out_shape.append(jax.ShapeDtypeStruct((rows, dil * LANES), F32))
        out_specs.append(pl.BlockSpec((tq, LANES), qmap))
    gs = pltpu.PrefetchScalarGridSpec(num_scalar_prefetch=2, grid=(dil, nt),
                                      in_specs=in_specs, out_specs=out_specs)
    res = pl.pallas_call(
        functools.partial(_band_kernel, tq=tq, hb=hb, half=half, heads=heads, kv_group=kv_group,
                          has_sink=has_sink, want_lse=want_lse),
        grid_spec=gs,
        out_shape=out_shape,
        compiler_params=_cparams(("arbitrary", "arbitrary")),
        name="band_attn_d%d" % dil,
    )(s0, s1, *args)
    return res


def _na_kernel(r0_ref, nr_ref, q_ref, kp_ref, kc_ref, kn_ref, vp_ref, vc_ref, vn_ref, bias_ref,
               o_ref, kcat, vcat, *, halo):
    j = pl.program_id(0)
    tq = NA_TROWS * GRID_W
    hrows = halo * GRID_W
    kcat[0:hrows, :] = kp_ref[...]
    kcat[hrows:hrows + tq, :] = kc_ref[...]
    kcat[hrows + tq:, :] = kn_ref[...]
    vcat[0:hrows, :] = vp_ref[...]
    vcat[hrows:hrows + tq, :] = vc_ref[...]
    vcat[hrows + tq:, :] = vn_ref[...]
    r0 = r0_ref[j]
    nrows = nr_ref[j]
    kwin_len = NA_KH * GRID_W
    lane = lax.broadcasted_iota(jnp.int32, (GRID_W, LANES), 1)
    lo_lanes = lane < HEAD_DIM
    nrel = 2 * NA_KH - 1

    def row_body(u, carry):
        r = r0 + u
        rs = jnp.clip(r - NA_KH // 2, 0, nrows - NA_KH)
        off = pl.multiple_of((rs - r0 + halo) * GRID_W, GRID_W)
        bvar = rs - r + NA_KH - 1
        qrow = pl.multiple_of(u * GRID_W, GRID_W)
        for g in range(C_HEADS // 2):
            qg = q_ref[pl.ds(qrow, GRID_W), g * LANES:(g + 1) * LANES]
            qs = jnp.concatenate([jnp.where(lo_lanes, qg, jnp.zeros_like(qg)),
                                  jnp.where(lo_lanes, jnp.zeros_like(qg), qg)], axis=0)
            kg = kcat[pl.ds(off, kwin_len), g * LANES:(g + 1) * LANES]
            vg = vcat[pl.ds(off, kwin_len), g * LANES:(g + 1) * LANES]
            sc = lax.dot_general(kg, qs, (((1,), (1,)), ((), ())), preferred_element_type=F32)
            sc = sc + jnp.concatenate([bias_ref[g * nrel + bvar + kr] for kr in range(NA_KH)], axis=0)
            m = jnp.max(sc, axis=0, keepdims=True)
            e = jnp.exp(sc - m)
            den = jnp.sum(e, axis=0, keepdims=True)
            p = (e * (1.0 / den)).astype(BF16)
            pv = lax.dot_general(p, vg, (((0,), (0,)), ((), ())), preferred_element_type=F32)
            o = jnp.where(lo_lanes, pv[0:GRID_W], pv[GRID_W:])
            o_ref[pl.ds(qrow, GRID_W), g * LANES:(g + 1) * LANES] = o.astype(o_ref.dtype)
        return carry

    lax.fori_loop(0, NA_TROWS, row_body, 0)


def _na_attn(q, k, v, bias_tab, tile_r0, tile_nr):
    t = q.shape[0]
    halo = NA_KH // 2
    tq = NA_TROWS * GRID_W
    hrows = halo * GRID_W
    per = tq // hrows
    nhb = t // hrows
    nt = t // tq
    qmap = lambda j, a, b: (j, 0)
    pmap = lambda j, a, b: (jnp.maximum(j * per - 1, 0), 0)
    nmap = lambda j, a, b: (jnp.minimum((j + 1) * per, nhb - 1), 0)
    gs = pltpu.PrefetchScalarGridSpec(
        num_scalar_prefetch=2, grid=(nt,),
        in_specs=[pl.BlockSpec((tq, C_W), qmap),
                  pl.BlockSpec((hrows, C_W), pmap), pl.BlockSpec((tq, C_W), qmap), pl.BlockSpec((hrows, C_W), nmap),
                  pl.BlockSpec((hrows, C_W), pmap), pl.BlockSpec((tq, C_W), qmap), pl.BlockSpec((hrows, C_W), nmap),
                  pl.BlockSpec(bias_tab.shape, lambda j, a, b: (0, 0, 0))],
        out_specs=pl.BlockSpec((tq, C_W), qmap),
        scratch_shapes=[pltpu.VMEM((tq + 2 * hrows, C_W), BF16), pltpu.VMEM((tq + 2 * hrows, C_W), BF16)])
    return pl.pallas_call(
        functools.partial(_na_kernel, halo=halo),
        grid_spec=gs,
        out_shape=jax.ShapeDtypeStruct((t, C_W), BF16),
        compiler_params=_cparams(("arbitrary",)),
        name="na_attn",
    )(tile_r0, tile_nr, q, k, k, k, v, v, v, bias_tab)


def _na_bias_table(rpb):
    c = np.arange(GRID_W)
    cs = np.clip(c - NA_KW // 2, 0, GRID_W - NA_KW)
    kc = np.arange(GRID_W)
    valid = (kc[:, None] >= cs[None, :]) & (kc[:, None] < cs[None, :] + NA_KW)
    cidx = np.clip(kc[:, None] - c[None, :] + NA_KW - 1, 0, 2 * NA_KW - 2)
    tab = rpb.astype(F32)[:, :, cidx]
    tab = jnp.where(jnp.asarray(valid)[None, None], tab, NEG_INF)
    pairs = jnp.concatenate([tab[0::2], tab[1::2]], axis=-1)
    return pairs.reshape((C_HEADS // 2) * (2 * NA_KH - 1), GRID_W, LANES)


def _route(h2, wr_ref, br_ref, idx_ref, gate_ref):
    tm = h2.shape[0]
    logits = lax.dot_general(wr_ref[...], h2, (((1,), (1,)), ((), ())),
                             precision=HIGHEST, preferred_element_type=F32) + br_ref[:, 0:1]
    eid = lax.broadcasted_iota(jnp.int32, (N_EXPERTS, tm), 0)
    vals, idxs = [], []
    for _ in range(TOP_K):
        m = jnp.max(logits, axis=0, keepdims=True)
        ix = jnp.min(jnp.where(logits == m, eid, N_EXPERTS), axis=0, keepdims=True)
        vals.append(m)
        idxs.append(ix)
        logits = jnp.where(eid == ix, -jnp.inf, logits)
    es = [jnp.exp(vk - vals[0]) for vk in vals]
    den = es[0] + es[1] + es[2] + es[3]
    pad_i = jnp.zeros((8 - TOP_K, tm), jnp.int32)
    pad_f = jnp.zeros((8 - TOP_K, tm), F32)
    idx_ref[...] = jnp.concatenate(idxs + [pad_i], axis=0)
    gate_ref[...] = jnp.concatenate([ek / den for ek in es] + [pad_f], axis=0)


PACK_W = D_MODEL // 2


def _pack_bf16_rows(h):
    bits = pltpu.bitcast(h.astype(BF16).astype(F32), jnp.int32)
    return (bits[:, :PACK_W] & jnp.int32(-65536)) | lax.shift_right_logical(bits[:, PACK_W:], 16)


def _unpack_bf16_rows(w):
    hi = pltpu.bitcast(w & jnp.int32(-65536), F32).astype(BF16)
    lo = pltpu.bitcast(lax.shift_left(w, 16), F32).astype(BF16)
    return jnp.concatenate([hi, lo], axis=1)


def _out0_kernel(sid_ref, x_ref, mod_ref, n2_ref, oa_ref, ob0_ref, ls0_ref, ob1_ref, ls1_ref, ob2_ref, ls2_ref,
                 wout_ref, wr_ref, br_ref, x1_ref, h2_ref, idx_ref, gate_ref, oscr, lscr, *, tm):
    del sid_ref
    os_, ls_ = [], []
    for gi, (o_ref, l_ref) in enumerate(((ob0_ref, ls0_ref), (ob1_ref, ls1_ref), (ob2_ref, ls2_ref))):
        dil = B_PAIRS[gi][1]
        if dil == 1:
            os_.append(o_ref[...])
            ls_.append(l_ref[...])
        else:
            ncg = B_GW // LANES
            for r in range(dil):
                for c in range(ncg):
                    lo = r * B_GW + c * LANES
                    oscr[gi * ncg + c, pl.ds(r, tm // dil, stride=dil), :] = o_ref[:, lo:lo + LANES]
                lscr[gi, pl.ds(r, tm // dil, stride=dil), :] = l_ref[:, r * LANES:(r + 1) * LANES]
            os_.append(jnp.concatenate([oscr[gi * ncg + c] for c in range(ncg)], axis=1))
            ls_.append(lscr[gi])
    lmax = jnp.maximum(jnp.maximum(ls_[0], ls_[1]), ls_[2])
    ws = [jnp.exp(l - lmax) for l in ls_]
    winv = 1.0 / (ws[0] + ws[1] + ws[2])
    lane = lax.broadcasted_iota(jnp.int32, (tm, LANES), 1)
    lo_lanes = lane < HEAD_DIM

    def head_weights(w):
        return jnp.concatenate(
            [jnp.where(lo_lanes, jnp.broadcast_to(w[:, 2 * pr:2 * pr + 1], (tm, LANES)),
                       jnp.broadcast_to(w[:, 2 * pr + 1:2 * pr + 2], (tm, LANES)))
             for pr in range(B_HEADS // 2)], axis=1)

    ob = (head_weights(ws[0] * winv) * os_[0] + head_weights(ws[1] * winv) * os_[1]
          + head_weights(ws[2] * winv) * os_[2])
    o = (jnp.dot(oa_ref[...], wout_ref[0:A_QW, :], preferred_element_type=F32)
         + jnp.dot(ob.astype(BF16), wout_ref[A_QW:, :], preferred_element_type=F32))
    x1 = x_ref[...] + mod_ref[2:3, :] * o
    x1_ref[...] = x1
    h2 = _rms_mod(x1, n2_ref, mod_ref, 3, 4)
    h2_ref[...] = _pack_bf16_rows(h2)
    _route(h2, wr_ref, br_ref, idx_ref, gate_ref)


def _out1_kernel(sid_ref, x_ref, mod_ref, n2_ref, oc_ref, wout_ref, wr_ref, br_ref,
                 x1_ref, h2_ref, idx_ref, gate_ref):
    del sid_ref
    o = jnp.dot(oc_ref[...], wout_ref[...], preferred_element_type=F32)
    x1 = x_ref[...] + mod_ref[2:3, :] * o
    x1_ref[...] = x1
    h2 = _rms_mod(x1, n2_ref, mod_ref, 3, 4)
    h2_ref[...] = _pack_bf16_rows(h2)
    _route(h2, wr_ref, br_ref, idx_ref, gate_ref)


def _out_proj(layer, x, mod, tile_sid, n2, attn, wout_bf, wr_t, br):
    t = x.shape[0]
    tm = TOKEN_TILE
    row = lambda i, sid: (i, 0)
    const = lambda i, sid: (0, 0)
    in_specs = [pl.BlockSpec((tm, D_MODEL), row),
                pl.BlockSpec((None, 6, D_MODEL), lambda i, sid: (sid[i], 0, 0)),
                pl.BlockSpec((1, D_MODEL), const)]
    scratch = []
    if layer == 0:
        in_specs.append(pl.BlockSpec((tm, A_QW), row))
        for (_, dil) in B_PAIRS:
            in_specs += [pl.BlockSpec((tm // dil, dil * B_GW), row), pl.BlockSpec((tm // dil, dil * LANES), row)]
        body = functools.partial(_out0_kernel, tm=tm)
        nscr = B_GROUPS * B_GW // LANES
        scratch = [pltpu.VMEM((nscr, tm, LANES), F32), pltpu.VMEM((B_GROUPS, tm, LANES), F32)]
    else:
        in_specs.append(pl.BlockSpec((tm, C_W), row))
        body = _out1_kernel
    in_specs += [pl.BlockSpec(wout_bf.shape, const),
                 pl.BlockSpec((N_EXPERTS, D_MODEL), const),
                 pl.BlockSpec((N_EXPERTS, LANES), const)]
    gs = pltpu.PrefetchScalarGridSpec(
        num_scalar_prefetch=1, grid=(t // tm,), in_specs=in_specs,
        out_specs=[pl.BlockSpec((tm, D_MODEL), row), pl.BlockSpec((tm, PACK_W), row),
                   pl.BlockSpec((8, tm), lambda i, sid: (0, i)), pl.BlockSpec((8, tm), lambda i, sid: (0, i))],
        scratch_shapes=scratch)
    return pl.pallas_call(
        body, grid_spec=gs,
        out_shape=[jax.ShapeDtypeStruct((t, D_MODEL), F32), jax.ShapeDtypeStruct((t, PACK_W), jnp.int32),
                   jax.ShapeDtypeStruct((8, t), jnp.int32), jax.ShapeDtypeStruct((8, t), F32)],
        compiler_params=_cparams(("arbitrary",)),
        name="out_proj%d" % layer,
    )(tile_sid, x, mod, n2.reshape(1, D_MODEL), *attn, wout_bf, wr_t, br)


def _rank_kernel(idx_ref, tri_ref, rank_ref, cnt_ref, carry):
    i = pl.program_id(0)

    @pl.when(i == 0)
    def _():
        carry[...] = jnp.zeros_like(carry)

    tk = idx_ref.shape[1]
    eid = lax.broadcasted_iota(jnp.int32, (N_EXPERTS, tk), 0)
    hits = [eid == idx_ref[k:k + 1, :] for k in range(TOP_K)]
    onehot = sum(h.astype(F32) for h in hits)
    incl = jnp.dot(onehot.astype(BF16), tri_ref[...], preferred_element_type=F32)
    before = incl - onehot + carry[:, 0:1]
    rows = [jnp.sum(jnp.where(h, before, 0.0), axis=0, keepdims=True) for h in hits]
    rows.append(jnp.zeros((8 - TOP_K, tk), F32))
    rank_ref[...] = jnp.concatenate(rows, axis=0).astype(jnp.int32)
    carry[...] = carry[...] + incl[:, tk - 1:tk]
    cnt_ref[...] = carry[...].astype(jnp.int32)


def _slot_kernel(idx_ref, rank_ref, start_ref, slot_ref):
    tk = idx_ref.shape[1]
    eid = lax.broadcasted_iota(jnp.int32, (N_EXPERTS, tk), 0)
    start = start_ref[:, 0:1]
    rows = []
    for k in range(TOP_K):
        base = jnp.sum(jnp.where(eid == idx_ref[k:k + 1, :], start, 0), axis=0, keepdims=True)
        rows.append(base + rank_ref[k:k + 1, :])
    rows.append(jnp.zeros((8 - TOP_K, tk), jnp.int32))
    slot_ref[...] = jnp.concatenate(rows, axis=0)


def _route_slots(idx):
    t = idx.shape[1]
    tk = ROUTE_TILE
    tri = jnp.asarray(np.triu(np.ones((tk, tk), np.float32)), BF16)
    tok = lambda i: (0, i)
    rank, cnt = pl.pallas_call(
        _rank_kernel, grid=(t // tk,),
        in_specs=[pl.BlockSpec((8, tk), tok), pl.BlockSpec((tk, tk), lambda i: (0, 0))],
        out_specs=[pl.BlockSpec((8, tk), tok), pl.BlockSpec((N_EXPERTS, LANES), lambda i: (0, 0))],
        out_shape=[jax.ShapeDtypeStruct((8, t), jnp.int32), jax.ShapeDtypeStruct((N_EXPERTS, LANES), jnp.int32)],
        scratch_shapes=[pltpu.VMEM((N_EXPERTS, LANES), F32)],
        compiler_params=_cparams(("arbitrary",)),
        name="route_rank",
    )(idx, tri)
    counts = cnt[:, 0]
    padded = (counts + MOE_BM - 1) // MOE_BM * MOE_BM
    pad_end = jnp.cumsum(padded)
    pad_start = pad_end - padded
    nblk = (t * TOP_K) // MOE_BM + N_EXPERTS
    blk_row0 = jnp.arange(nblk, dtype=jnp.int32) * MOE_BM
    blk_expert = jnp.minimum(jnp.sum((pad_end[None, :] <= blk_row0[:, None]).astype(jnp.int32), axis=1),
                             N_EXPERTS - 1)
    nused = (pad_end[-1] // MOE_BM).astype(jnp.int32).reshape(1)
    start_b = jnp.broadcast_to(pad_start.astype(jnp.int32)[:, None], (N_EXPERTS, LANES))
    slot = pl.pallas_call(
        _slot_kernel, grid=(t // tk,),
        in_specs=[pl.BlockSpec((8, tk), tok), pl.BlockSpec((8, tk), tok),
                  pl.BlockSpec((N_EXPERTS, LANES), lambda i: (0, 0))],
        out_specs=pl.BlockSpec((8, tk), tok),
        out_shape=jax.ShapeDtypeStruct((8, t), jnp.int32),
        compiler_params=_cparams(("arbitrary",)),
        name="route_slot",
    )(idx, rank, start_b)
    return slot, blk_expert, nused, pad_end.astype(jnp.int32), padded.astype(jnp.int32)


def _row_copy(src, dst, sem):
    return pltpu.make_async_copy(src, dst, sem)


ROW_GROUP = 8


def _for_each_row_copy(n_tok, fn):
    def group(j, carry):
        base = pl.multiple_of(j * ROW_GROUP, ROW_GROUP)
        for u in range(ROW_GROUP):
            for k in range(TOP_K):
                fn(base + u, k, (u * TOP_K + k) % 2)
        return carry

    lax.fori_loop(0, n_tok // ROW_GROUP, group, 0)


def _dispatch_kernel(pend_ref, padded_ref, slot_ref, h_ref, xs_ref, zbuf, sem, zsem, *, tm, nblk):
    i = pl.program_id(0)

    @pl.when(i == 0)
    def _():
        zbuf[...] = jnp.zeros_like(zbuf)

        def zero_block(b):
            cp = _row_copy(zbuf, xs_ref.at[pl.ds(pl.multiple_of(b * MOE_BM, MOE_BM), MOE_BM)], zsem)
            cp.start()
            cp.wait()

        for e in range(N_EXPERTS):
            @pl.when(padded_ref[e] > 0)
            def _():
                zero_block(pend_ref[e] // MOE_BM - 1)

        def tail(b, carry):
            zero_block(b)
            return carry

        lax.fori_loop(pend_ref[N_EXPERTS - 1] // MOE_BM, nblk, tail, 0)

    def issue(tk, k, priority):
        _row_copy(h_ref.at[pl.ds(tk, 1)], xs_ref.at[pl.ds(slot_ref[k, tk], 1)], sem).start(priority=priority)

    _for_each_row_copy(tm, issue)
    for k in range(TOP_K):
        _row_copy(h_ref, xs_ref.at[pl.ds(0, tm)], sem).wait()


def _dispatch(h2, slot, pad_end, padded, nslots):
    t, w = h2.shape
    tm = MOVE_TILE
    gs = pltpu.PrefetchScalarGridSpec(
        num_scalar_prefetch=2, grid=(t // tm,),
        in_specs=[pl.BlockSpec((8, tm), lambda i, a, b: (0, i), memory_space=pltpu.SMEM),
                  pl.BlockSpec((tm, w), lambda i, a, b: (i, 0))],
        out_specs=pl.BlockSpec(memory_space=pl.ANY),
        scratch_shapes=[pltpu.VMEM((MOE_BM, w), h2.dtype), pltpu.SemaphoreType.DMA, pltpu.SemaphoreType.DMA])
    return pl.pallas_call(
        functools.partial(_dispatch_kernel, tm=tm, nblk=nslots // MOE_BM), grid_spec=gs,
        out_shape=jax.ShapeDtypeStruct((nslots, w), h2.dtype),
        compiler_params=_cparams(("arbitrary",)),
        name="moe_dispatch",
    )(pad_end, padded, slot, h2)


SC_CORES = 2
SC_SUBCORES = 16
SC_CHUNK = 128


def _sc_dispatch(h2, slot, nslots):
    t, w = h2.shape
    nchunk = t // SC_CHUNK
    per = nchunk // (SC_CORES * SC_SUBCORES)
    assert per * SC_CORES * SC_SUBCORES * SC_CHUNK == t
    slot3 = slot[:TOP_K].reshape(TOP_K, nchunk, SC_CHUNK).transpose(1, 0, 2)
    mesh = plsc.VectorSubcoreMesh(core_axis_name="c", subcore_axis_name="s")

    @functools.partial(
        pl.kernel, mesh=mesh, out_type=jax.ShapeDtypeStruct((nslots, w), h2.dtype),
        scratch_types=[pltpu.VMEM((TOP_K, SC_CHUNK), jnp.int32), pltpu.VMEM((SC_CHUNK, w), h2.dtype)],
        name="moe_dispatch_sc")
    def body(h_hbm, slot_hbm, xs_hbm, idx_v, rows_v):
        wid = lax.axis_index("s") * SC_CORES + lax.axis_index("c")

        @pl.loop(0, per)
        def _(j):
            c = wid * per + j
            pltpu.sync_copy(slot_hbm.at[c], idx_v)
            pltpu.sync_copy(h_hbm.at[pl.ds(c * SC_CHUNK, SC_CHUNK)], rows_v)
            for k in range(TOP_K):
                pltpu.sync_copy(rows_v, xs_hbm.at[idx_v.at[k]])

    return body(h2, slot3)


def _ffn_kernel(be_ref, nu_ref, xs_ref, wgu_ref, bgu_ref, wd_ref, bd_ref, y_ref, wgu_bf, wd_bf):
    i = pl.program_id(0)
    prev = be_ref[jnp.maximum(i - 1, 0)]

    @pl.when(i < nu_ref[0])
    def _():
        @pl.when((i == 0) | (be_ref[i] != prev))
        def _():
            wgu_bf[...] = wgu_ref[...].astype(BF16)
            wd_bf[...] = wd_ref[...].astype(BF16)

        x = _unpack_bf16_rows(xs_ref[...])
        gu = jnp.dot(x, wgu_bf[...], preferred_element_type=F32) + bgu_ref[...]
        gate = jnp.minimum(gu[:, :D_FF], SWIGLU_LIMIT)
        up = jnp.clip(gu[:, D_FF:], -SWIGLU_LIMIT, SWIGLU_LIMIT)
        act = (up + 1.0) * gate * jax.nn.sigmoid(SWIGLU_ALPHA * gate)
        y_ref[...] = jnp.dot(act.astype(BF16), wd_bf[...], preferred_element_type=F32) + bd_ref[...]

    @pl.when(i >= nu_ref[0])
    def _():
        y_ref[...] = jnp.zeros_like(y_ref)


def _expert_ffn(xs, blk_expert, nused, wgu, bgu, wd, bd):
    nslots = xs.shape[0]
    nblk = nslots // MOE_BM
    blk = lambda i, be, nu: (jnp.minimum(i, nu[0] - 1), 0)
    exp3 = lambda i, be, nu: (be[jnp.minimum(i, nu[0] - 1)], 0, 0)
    gs = pltpu.PrefetchScalarGridSpec(
        num_scalar_prefetch=2, grid=(nblk,),
        in_specs=[pl.BlockSpec((MOE_BM, PACK_W), blk),
                  pl.BlockSpec((None, D_MODEL, 2 * D_FF), exp3),
                  pl.BlockSpec((None, 1, 2 * D_FF), exp3),
                  pl.BlockSpec((None, D_FF, D_MODEL), exp3),
                  pl.BlockSpec((None, 1, D_MODEL), exp3)],
        out_specs=pl.BlockSpec((MOE_BM, D_MODEL), lambda i, be, nu: (i, 0)),
        scratch_shapes=[pltpu.VMEM((D_MODEL, 2 * D_FF), BF16), pltpu.VMEM((D_FF, D_MODEL), BF16)])
    return pl.pallas_call(
        _ffn_kernel, grid_spec=gs,
        out_shape=jax.ShapeDtypeStruct((nslots, D_MODEL), F32),
        compiler_params=_cparams(("arbitrary",)),
        name="moe_ffn",
    )(blk_expert, nused, xs, wgu, bgu.reshape(N_EXPERTS, 1, 2 * D_FF), wd, bd.reshape(N_EXPERTS, 1, D_MODEL))


def _combine_kernel(sid_ref, slot_ref, x_ref, mod_ref, gate_ref, ys_ref, o_ref, buf, sem, *, tm):
    del sid_ref

    def issue(tk, k, priority):
        _row_copy(ys_ref.at[pl.ds(slot_ref[k, tk], 1)], buf.at[k, pl.ds(tk, 1)], sem).start(priority=priority)

    _for_each_row_copy(tm, issue)
    gpad = jnp.concatenate([gate_ref[...], jnp.zeros((LANES - 8, tm), F32)], axis=0)
    gcol = gpad.T
    for k in range(TOP_K):
        _row_copy(ys_ref.at[pl.ds(0, tm)], buf.at[k], sem).wait()
    acc = gcol[:, 0:1] * buf[0]
    for k in range(1, TOP_K):
        acc = acc + gcol[:, k:k + 1] * buf[k]
    o_ref[...] = x_ref[...] + mod_ref[5:6, :] * acc


def _combine(x1, mod, tile_sid, slot, gates, ys):
    t = x1.shape[0]
    tm = MOVE_TILE
    gs = pltpu.PrefetchScalarGridSpec(
        num_scalar_prefetch=1, grid=(t // tm,),
        in_specs=[pl.BlockSpec((8, tm), lambda i, sid: (0, i), memory_space=pltpu.SMEM),
                  pl.BlockSpec((tm, D_MODEL), lambda i, sid: (i, 0)),
                  pl.BlockSpec((None, 6, D_MODEL), lambda i, sid: (sid[i], 0, 0)),
                  pl.BlockSpec((8, tm), lambda i, sid: (0, i)),
                  pl.BlockSpec(memory_space=pl.ANY)],
        out_specs=pl.BlockSpec((tm, D_MODEL), lambda i, sid: (i, 0)),
        scratch_shapes=[pltpu.VMEM((TOP_K, tm, D_MODEL), F32), pltpu.SemaphoreType.DMA])
    return pl.pallas_call(
        functools.partial(_combine_kernel, tm=tm), grid_spec=gs,
        out_shape=jax.ShapeDtypeStruct((t, D_MODEL), F32),
        compiler_params=_cparams(("arbitrary",)),
        name="moe_combine",
    )(tile_sid, slot, x1, mod, gates, ys)


def _moe(x1, h2, idx, gates, mod, tile_sid_move, wgu, bgu, wd, bd):
    t = x1.shape[0]
    nslots = t * TOP_K + N_EXPERTS * MOE_BM
    slot, blk_expert, nused, pad_end, padded = _route_slots(idx)
    xs = _sc_dispatch(h2, slot, nslots)
    ys = _expert_ffn(xs, blk_expert, nused, wgu, bgu, wd, bd)
    return _combine(x1, mod, tile_sid_move, slot, gates, ys)


def _tile_table(seq_lens, tile, fn):
    vals = []
    start = 0
    for sid, n in enumerate(seq_lens):
        assert n % tile == 0
        for j in range(n // tile):
            vals.append(fn(sid, start, n, j))
        start += n
    return jnp.asarray(np.asarray(vals, np.int32))


def _head_block_diag():
    bd = np.kron(np.eye(2 * LANES // HEAD_DIM, dtype=np.float32),
                 np.full((HEAD_DIM, HEAD_DIM), 1.0 / HEAD_DIM, np.float32))
    return jnp.asarray(bd, BF16)


_A_HEAD_ORDER = [kv * A_GROUP + g for g in range(A_GROUP) for kv in range(A_KV_HEADS)]


def _trunk(x, c_pad, seq_lens, p):
    t = x.shape[0]
    sid_tok = _tile_table(seq_lens, TOKEN_TILE, lambda sid, s, n, j: sid)
    sid_move = _tile_table(seq_lens, MOVE_TILE, lambda sid, s, n, j: sid)
    bd = _head_block_diag()

    mod0 = _ada_mod(c_pad, p["l0_ada_w"], p["l0_ada_b"]).reshape(c_pad.shape[0], 6, D_MODEL)
    w_in = p["l0_w_in"]
    a_cols = np.concatenate([np.arange(h * HEAD_DIM, (h + 1) * HEAD_DIM) for h in _A_HEAD_ORDER])
    bq, bk, bv = A_IN, A_IN + B_GROUPS * B_GW, A_IN + 2 * B_GROUPS * B_GW
    col_blocks = [w_in[:, a_cols], w_in[:, A_QW:A_IN]]
    for g in range(B_GROUPS):
        col_blocks += [w_in[:, bq + g * B_GW:bq + (g + 1) * B_GW],
                       w_in[:, bk + g * B_GW:bk + (g + 1) * B_GW],
                       w_in[:, bv + g * B_GW:bv + (g + 1) * B_GW]]
    w0 = jnp.concatenate(col_blocks, axis=1).astype(BF16)
    ones = lambda n: jnp.ones((n,), F32)
    gain_blocks = [jnp.tile(p["l0_q_norm_a"], A_Q_HEADS) * QK_SCALE, jnp.tile(p["l0_k_norm_a"], A_KV_HEADS),
                   ones(A_KVW)]
    for g in range(B_GROUPS):
        gain_blocks += [jnp.tile(p["l0_q_norm_b"], B_HEADS) * QK_SCALE, jnp.tile(p["l0_k_norm_b"], B_HEADS),
                        ones(B_GW)]
    gain0 = jnp.concatenate(gain_blocks).reshape(1, -1)
    slabs0 = [(0, A_QW, True, 1), (A_QW, A_KVW, True, 1), (A_QW + A_KVW, A_KVW, False, 1)]
    c0 = A_IN
    for g, (_, dil) in enumerate(B_PAIRS):
        slabs0 += [(c0, B_GW, True, dil), (c0 + B_GW, B_GW, True, dil), (c0 + 2 * B_GW, B_GW, False, dil)]
        c0 += 3 * B_GW
    proj = _in_proj(x, mod0, sid_tok, p["l0_norm1"], w0, gain0, bd, tuple(slabs0))
    qa, ka, va = proj[0:3]

    slopes_a = _alibi_slopes(A_Q_HEADS)
    heads_a = tuple(tuple((slopes_a[kv * A_GROUP + g], kv * A_GROUP + g) for kv in range(A_KV_HEADS))
                    for g in range(A_GROUP))
    (oa,) = _band_attn(qa, ka, va, seq_lens, tq=ATTN_TQ_A, dil=1, qw=A_QW, kvw=A_KVW, half=A_HALF, heads=heads_a,
                       kv_group=(0,) * A_GROUP, sink=p["l0_sink_a"].astype(F32), out_dtype=BF16, want_lse=False)

    slopes_b = _alibi_slopes(B_GROUPS * B_HEADS)
    attn0 = [oa]
    for g, (window, dil) in enumerate(B_PAIRS):
        qg, kg, vg = proj[3 + 3 * g:6 + 3 * g]
        heads_b = tuple(tuple((slopes_b[g * B_HEADS + 2 * pr + s] * dil, 0) for s in range(2))
                        for pr in range(B_HEADS // 2))
        og, lg = _band_attn(qg, kg, vg, [n // dil for n in seq_lens], tq=ATTN_TQ_B, dil=dil, qw=B_GW, kvw=B_GW, half=window // (2 * dil),
                            heads=heads_b, kv_group=tuple(range(B_HEADS // 2)), sink=None, out_dtype=F32,
                            want_lse=True)
        attn0 += [og, lg]

    w_out = p["l0_w_out"]
    wout0 = jnp.concatenate([w_out[a_cols], w_out[A_QW:]], axis=0).astype(BF16)
    br0 = jnp.broadcast_to(p["l0_b_router"].astype(F32)[:, None], (N_EXPERTS, LANES))
    x1, h2, idx, gates = _out_proj(0, x, mod0, sid_tok, p["l0_norm2"], attn0, wout0, p["l0_w_router"].T, br0)
    x = _moe(x1, h2, idx, gates, mod0, sid_move, p["l0_w_gate_up"], p["l0_b_gate_up"], p["l0_w_down"],
             p["l0_b_down"])

    mod1 = _ada_mod(c_pad, p["l1_ada_w"], p["l1_ada_b"]).reshape(c_pad.shape[0], 6, D_MODEL)
    w1 = p["l1_w_in"].astype(BF16)
    gain1 = jnp.concatenate([jnp.tile(p["l1_q_norm_c"], C_HEADS) * QK_SCALE, jnp.tile(p["l1_k_norm_c"], C_HEADS),
                             ones(C_W)]).reshape(1, -1)
    slabs1 = ((0, C_W, True, 1), (C_W, C_W, True, 1), (2 * C_W, C_W, False, 1))
    qc, kc, vc = _in_proj(x, mod1, sid_tok, p["l1_norm1"], w1, gain1, bd, slabs1)
    na_tile = NA_TROWS * GRID_W
    tile_r0 = _tile_table(seq_lens, na_tile, lambda sid, s, n, j: j * NA_TROWS)
    tile_nr = _tile_table(seq_lens, na_tile, lambda sid, s, n, j: n // GRID_W)
    oc = _na_attn(qc, kc, vc, _na_bias_table(p["l1_rpb_c"]), tile_r0, tile_nr)
    br1 = jnp.broadcast_to(p["l1_b_router"].astype(F32)[:, None], (N_EXPERTS, LANES))
    x1, h2, idx, gates = _out_proj(1, x, mod1, sid_tok, p["l1_norm2"], [oc], p["l1_w_out"].astype(BF16),
                                   p["l1_w_router"].T, br1)
    return _moe(x1, h2, idx, gates, mod1, sid_move, p["l1_w_gate_up"], p["l1_b_gate_up"], p["l1_w_down"],
                p["l1_b_down"])


def kernel(x_prompt, x_sample, c_prompt, c_sample, l0_ada_w, l0_ada_b, l0_norm1, l0_w_in, l0_q_norm_a, l0_k_norm_a, l0_sink_a, l0_q_norm_b, l0_k_norm_b, l0_w_out, l0_norm2, l0_w_router, l0_b_router, l0_w_gate_up, l0_b_gate_up, l0_w_down, l0_b_down, l1_ada_w, l1_ada_b, l1_norm1, l1_w_in, l1_q_norm_c, l1_k_norm_c, l1_rpb_c, l1_w_out, l1_norm2, l1_w_router, l1_b_router, l1_w_gate_up, l1_b_gate_up, l1_w_down, l1_b_down):
    p = dict(l0_ada_w=l0_ada_w, l0_ada_b=l0_ada_b, l0_norm1=l0_norm1, l0_w_in=l0_w_in, l0_q_norm_a=l0_q_norm_a,
             l0_k_norm_a=l0_k_norm_a, l0_sink_a=l0_sink_a, l0_q_norm_b=l0_q_norm_b, l0_k_norm_b=l0_k_norm_b,
             l0_w_out=l0_w_out, l0_norm2=l0_norm2, l0_w_router=l0_w_router, l0_b_router=l0_b_router,
             l0_w_gate_up=l0_w_gate_up, l0_b_gate_up=l0_b_gate_up, l0_w_down=l0_w_down, l0_b_down=l0_b_down,
             l1_ada_w=l1_ada_w, l1_ada_b=l1_ada_b, l1_norm1=l1_norm1, l1_w_in=l1_w_in, l1_q_norm_c=l1_q_norm_c,
             l1_k_norm_c=l1_k_norm_c, l1_rpb_c=l1_rpb_c, l1_w_out=l1_w_out, l1_norm2=l1_norm2,
             l1_w_router=l1_w_router, l1_b_router=l1_b_router, l1_w_gate_up=l1_w_gate_up,
             l1_b_gate_up=l1_b_gate_up, l1_w_down=l1_w_down, l1_b_down=l1_b_down)
    nb_p, len_p, d = x_prompt.shape
    nb_s, len_s, _ = x_sample.shape
    seq_lens = (len_p,) * nb_p + (len_s,) * nb_s
    x = jnp.concatenate([x_prompt.reshape(nb_p * len_p, d), x_sample.reshape(nb_s * len_s, d)], axis=0)
    nseq = nb_p + nb_s
    c_pad = jnp.concatenate([c_prompt, c_sample, jnp.zeros((-nseq % 8, d), F32)], axis=0)
    y = _trunk(x, c_pad, seq_lens, p)
    y_prompt = y[:nb_p * len_p].reshape(nb_p, len_p, d)
    y_sample = y[nb_p * len_p:].reshape(nb_s, len_s, d)
    return (y_prompt, y_sample)
```

```python
import functools
import math

import numpy as np
import jax
import jax.numpy as jnp
from jax import lax
from jax.experimental import pallas as pl
from jax.experimental.pallas import tpu as pltpu
from jax.experimental.pallas import tpu_sc as plsc

F32 = jnp.float32
BF16 = jnp.bfloat16
HIGHEST = lax.Precision.HIGHEST

D_MODEL = 1024
HEAD_DIM = 64
LANES = 128
GRID_W = 64
A_Q_HEADS = 8
A_KV_HEADS = 2
A_GROUP = A_Q_HEADS // A_KV_HEADS
A_HALF = 128
B_PAIRS = ((128, 1), (512, 4), (2048, 16))
B_GROUPS = 3
B_HEADS = 4
A_QW = A_Q_HEADS * HEAD_DIM
A_KVW = A_KV_HEADS * HEAD_DIM
A_IN = A_QW + 2 * A_KVW
B_GW = B_HEADS * HEAD_DIM
C_HEADS = 16
C_W = C_HEADS * HEAD_DIM
NA_KH = 8
NA_KW = 16
N_EXPERTS = 32
TOP_K = 4
D_FF = 1024
SWIGLU_LIMIT = 7.0
SWIGLU_ALPHA = 1.702
RMS_EPS = 1e-6
NEG_INF = -1e30
QK_SCALE = HEAD_DIM ** -0.5

TOKEN_TILE = 512
ATTN_TQ_A = 256
ATTN_TQ_B = 512
ATTN_UQ = 128
NA_TROWS = 8
MOE_BM = 256
ROUTE_TILE = 512
MOVE_TILE = 256
VMEM_LIMIT = 56 * 1024 * 1024


def _alibi_slopes(n):
    return [float(2.0 ** (-8.0 * (j + 1) / n)) for j in range(n)]


def _cparams(sem, flags=None):
    return pltpu.CompilerParams(dimension_semantics=sem, vmem_limit_bytes=VMEM_LIMIT, flags=flags)


def _mod_kernel(c_ref, w_ref, b_ref, o_ref):
    c = c_ref[...]
    s = c * jax.nn.sigmoid(c)
    o_ref[...] = jnp.dot(s, w_ref[...], precision=HIGHEST, preferred_element_type=F32) + b_ref[...]


def _ada_mod(c_pad, w, b):
    nrow = c_pad.shape[0]
    ncol = w.shape[1]
    tn = D_MODEL
    return pl.pallas_call(
        _mod_kernel,
        grid=(ncol // tn,),
        in_specs=[pl.BlockSpec((nrow, D_MODEL), lambda j: (0, 0)),
                  pl.BlockSpec((D_MODEL, tn), lambda j: (0, j)),
                  pl.BlockSpec((1, tn), lambda j: (0, j))],
        out_specs=pl.BlockSpec((nrow, tn), lambda j: (0, j)),
        out_shape=jax.ShapeDtypeStruct((nrow, ncol), F32),
        compiler_params=_cparams(("arbitrary",)),
        name="ada_mod",
    )(c_pad, w, b.reshape(1, ncol))


def _head_mean_sq(y, bd_ref):
    w = y.shape[1]
    outs = []
    for c0 in range(0, w, 2 * LANES):
        cw = min(2 * LANES, w - c0)
        sq = y[:, c0:c0 + cw] * y[:, c0:c0 + cw]
        hi = sq.astype(BF16)
        lo = (sq - hi.astype(F32)).astype(BF16)
        bd = bd_ref[0:cw, 0:cw]
        outs.append(jnp.dot(hi, bd, preferred_element_type=F32)
                    + jnp.dot(lo, bd, preferred_element_type=F32))
    return outs[0] if len(outs) == 1 else jnp.concatenate(outs, axis=1)


def _rms_mod(x, g_ref, mod_ref, shift_row, scale_row):
    ms = jnp.mean(x * x, axis=-1, keepdims=True)
    xn = x * lax.rsqrt(ms + RMS_EPS) * g_ref[...]
    return xn * (1.0 + mod_ref[scale_row:scale_row + 1, :]) + mod_ref[shift_row:shift_row + 1, :]


def _in_kernel(sid_ref, x_ref, mod_ref, n1_ref, w_ref, gain_ref, bd_ref, *rest, slabs, tm):
    del sid_ref
    nout = len(slabs)
    out_refs = rest[:nout]
    scr_ref = rest[nout] if len(rest) > nout else None
    h = _rms_mod(x_ref[...], n1_ref, mod_ref, 0, 1).astype(BF16)
    for (c0, w, normed, dil), o_ref in zip(slabs, out_refs):
        y = jnp.dot(h, w_ref[:, c0:c0 + w], preferred_element_type=F32)
        if normed:
            y = y * lax.rsqrt(_head_mean_sq(y, bd_ref) + RMS_EPS) * gain_ref[:, c0:c0 + w]
        if dil == 1:
            o_ref[...] = y.astype(BF16)
        else:
            for c in range(w // LANES):
                scr_ref[c] = y[:, c * LANES:(c + 1) * LANES]
            for r in range(dil):
                for c in range(w // LANES):
                    o_ref[:, r * w + c * LANES:r * w + (c + 1) * LANES] = (
                        scr_ref[c, pl.ds(r, tm // dil, stride=dil), :].astype(BF16))


def _in_proj(x, mod, tile_sid, n1, w_bf, gain, bd, slabs):
    t = x.shape[0]
    tm = TOKEN_TILE
    ncols = w_bf.shape[1]
    out_shapes, out_specs = [], []
    for (c0, w, normed, dil) in slabs:
        out_shapes.append(jax.ShapeDtypeStruct((t // dil, dil * w), BF16))
        out_specs.append(pl.BlockSpec((tm // dil, dil * w), lambda i, sid: (i, 0)))
    need_scr = any(s[3] > 1 for s in slabs)
    scratch = [pltpu.VMEM((max(s[1] for s in slabs if s[3] > 1) // LANES, tm, LANES), F32)] if need_scr else []
    gs = pltpu.PrefetchScalarGridSpec(
        num_scalar_prefetch=1,
        grid=(t // tm,),
        in_specs=[pl.BlockSpec((tm, D_MODEL), lambda i, sid: (i, 0)),
                  pl.BlockSpec((None, 6, D_MODEL), lambda i, sid: (sid[i], 0, 0)),
                  pl.BlockSpec((1, D_MODEL), lambda i, sid: (0, 0)),
                  pl.BlockSpec((D_MODEL, ncols), lambda i, sid: (0, 0)),
                  pl.BlockSpec((1, ncols), lambda i, sid: (0, 0)),
                  pl.BlockSpec((2 * LANES, 2 * LANES), lambda i, sid: (0, 0))],
        out_specs=out_specs,
        scratch_shapes=scratch)
    return pl.pallas_call(
        functools.partial(_in_kernel, slabs=slabs, tm=tm),
        grid_spec=gs,
        out_shape=out_shapes,
        compiler_params=_cparams(("arbitrary",)),
        name="in_proj",
    )(tile_sid, x, mod, n1.reshape(1, D_MODEL), w_bf, gain, bd)


def _band_kernel(s0_ref, s1_ref, *refs, tq, hb, half, heads, kv_group, has_sink, want_lse):
    pos = 0
    sink_ref = None
    if has_sink:
        sink_ref = refs[0]
        pos = 1
    q_ref, kp_ref, kc_ref, kn_ref, vp_ref, vc_ref, vn_ref = refs[pos:pos + 7]
    o_ref = refs[pos + 7]
    lse_ref = refs[pos + 8] if want_lse else None
    i = pl.program_id(1)
    uq = ATTN_UQ
    wlen = uq + 2 * hb
    kall = jnp.concatenate([kp_ref[...], kc_ref[...], kn_ref[...]], axis=0)
    vall = jnp.concatenate([vp_ref[...], vc_ref[...], vn_ref[...]], axis=0)
    lane = lax.broadcasted_iota(jnp.int32, (uq, LANES), 1)
    lo_lanes = lane < HEAD_DIM
    for sub in range(tq // uq):
        rows = slice(sub * uq, (sub + 1) * uq)
        kwin = kall[sub * uq:sub * uq + wlen]
        vwin = vall[sub * uq:sub * uq + wlen]
        q0 = i * tq + sub * uq
        kpos = q0 - hb + lax.broadcasted_iota(jnp.int32, (wlen, uq), 0)
        qpos = q0 + lax.broadcasted_iota(jnp.int32, (wlen, uq), 1)
        rel = kpos - qpos
        valid = (jnp.abs(rel) <= half) & (kpos >= s0_ref[i]) & (kpos < s1_ref[i])
        distm = jnp.where(valid, jnp.abs(rel).astype(F32), 1e32)
        lse_rows = []
        for g, pair in enumerate(heads):
            qg = q_ref[rows, g * LANES:(g + 1) * LANES]
            kg = kwin[:, kv_group[g] * LANES:(kv_group[g] + 1) * LANES]
            vg = vwin[:, kv_group[g] * LANES:(kv_group[g] + 1) * LANES]
            outs = []
            for s, (slope, sink_idx) in enumerate(pair):
                keep = lo_lanes if s == 0 else jnp.logical_not(lo_lanes)
                qm = jnp.where(keep, qg, jnp.zeros_like(qg))
                sc = lax.dot_general(kg, qm, (((1,), (1,)), ((), ())), preferred_element_type=F32)
                sc = sc - slope * distm
                m = jnp.max(sc, axis=0, keepdims=True)
                if has_sink:
                    m = jnp.maximum(m, sink_ref[sink_idx])
                e = jnp.exp(sc - m)
                den = jnp.sum(e, axis=0, keepdims=True)
                if has_sink:
                    den = den + jnp.exp(sink_ref[sink_idx] - m)
                p = (e * (1.0 / den)).astype(BF16)
                outs.append(lax.dot_general(p, vg, (((0,), (0,)), ((), ())), preferred_element_type=F32))
                if want_lse:
                    lse_rows.append(m + jnp.log(den))
            o_ref[rows, g * LANES:(g + 1) * LANES] = jnp.where(lo_lanes, outs[0], outs[1]).astype(o_ref.dtype)
        if want_lse:
            lse_t = jnp.concatenate(lse_rows + [jnp.zeros((LANES - len(lse_rows), uq), F32)], axis=0)
            lse_ref[rows, :] = lse_t.T


def _band_attn(q, k, v, seq_rows, *, tq, dil, qw, kvw, half, heads, kv_group, sink, out_dtype, want_lse):
    rows = q.shape[0]
    hb = half
    s0 = _tile_table(seq_rows, tq, lambda sid, s, n, j: s)
    s1 = _tile_table(seq_rows, tq, lambda sid, s, n, j: s + n)
    per = tq // hb
    nt = rows // tq
    nhb = rows // hb
    has_sink = sink is not None
    in_specs = []
    args = []
    if has_sink:
        in_specs.append(pl.BlockSpec(memory_space=pltpu.SMEM))
        args.append(sink)
    qmap = lambda r, i, a, b: (i, r)
    pmap = lambda r, i, a, b: (jnp.maximum(i * per - 1, 0), r)
    nmap = lambda r, i, a, b: (jnp.minimum((i + 1) * per, nhb - 1), r)
    in_specs += [pl.BlockSpec((tq, qw), qmap),
                 pl.BlockSpec((hb, kvw), pmap), pl.BlockSpec((tq, kvw), qmap), pl.BlockSpec((hb, kvw), nmap),
                 pl.BlockSpec((hb, kvw), pmap), pl.BlockSpec((tq, kvw), qmap), pl.BlockSpec((hb, kvw), nmap)]
    args += [q, k, k, k, v, v, v]
    out_shape = [jax.ShapeDtypeStruct((rows, dil * qw), out_dtype)]
    out_specs = [pl.BlockSpec((tq, qw), qmap)]
    if want_lse:
        out_shape.append(jax.ShapeDtypeStruct((rows, dil * LANES), F32))
        out_specs.append(pl.BlockSpec((tq, LANES), qmap))
    gs = pltpu.PrefetchScalarGridSpec(num_scalar_prefetch=2, grid=(dil, nt),
                                      in_specs=in_specs, out_specs=out_specs)
    res = pl.pallas_call(
        functools.partial(_band_kernel, tq=tq, hb=hb, half=half, heads=heads, kv_group=kv_group,
                          has_sink=has_sink, want_lse=want_lse),
        grid_spec=gs,
        out_shape=out_shape,
        compiler_params=_cparams(("arbitrary", "arbitrary")),
        name="band_attn_d%d" % dil,
    )(s0, s1, *args)
    return res


def _na_kernel(r0_ref, nr_ref, q_ref, kp_ref, kc_ref, kn_ref, vp_ref, vc_ref, vn_ref, bias_ref,
               o_ref, kcat, vcat, *, halo):
    j = pl.program_id(0)
    tq = NA_TROWS * GRID_W
    hrows = halo * GRID_W
    kcat[0:hrows, :] = kp_ref[...]
    kcat[hrows:hrows + tq, :] = kc_ref[...]
    kcat[hrows + tq:, :] = kn_ref[...]
    vcat[0:hrows, :] = vp_ref[...]
    vcat[hrows:hrows + tq, :] = vc_ref[...]
    vcat[hrows + tq:, :] = vn_ref[...]
    r0 = r0_ref[j]
    nrows = nr_ref[j]
    kwin_len = NA_KH * GRID_W
    lane = lax.broadcasted_iota(jnp.int32, (GRID_W, LANES), 1)
    lo_lanes = lane < HEAD_DIM
    nrel = 2 * NA_KH - 1

    def row_body(u, carry):
        r = r0 + u
        rs = jnp.clip(r - NA_KH // 2, 0, nrows - NA_KH)
        off = pl.multiple_of((rs - r0 + halo) * GRID_W, GRID_W)
        bvar = rs - r + NA_KH - 1
        qrow = pl.multiple_of(u * GRID_W, GRID_W)
        for g in range(C_HEADS // 2):
            qg = q_ref[pl.ds(qrow, GRID_W), g * LANES:(g + 1) * LANES]
            qs = jnp.concatenate([jnp.where(lo_lanes, qg, jnp.zeros_like(qg)),
                                  jnp.where(lo_lanes, jnp.zeros_like(qg), qg)], axis=0)
            kg = kcat[pl.ds(off, kwin_len), g * LANES:(g + 1) * LANES]
            vg = vcat[pl.ds(off, kwin_len), g * LANES:(g + 1) * LANES]
            sc = lax.dot_general(kg, qs, (((1,), (1,)), ((), ())), preferred_element_type=F32)
            sc = sc + jnp.concatenate([bias_ref[g * nrel + bvar + kr] for kr in range(NA_KH)], axis=0)
            m = jnp.max(sc, axis=0, keepdims=True)
            e = jnp.exp(sc - m)
            den = jnp.sum(e, axis=0, keepdims=True)
            p = (e * (1.0 / den)).astype(BF16)
            pv = lax.dot_general(p, vg, (((0,), (0,)), ((), ())), preferred_element_type=F32)
            o = jnp.where(lo_lanes, pv[0:GRID_W], pv[GRID_W:])
            o_ref[pl.ds(qrow, GRID_W), g * LANES:(g + 1) * LANES] = o.astype(o_ref.dtype)
        return carry

    lax.fori_loop(0, NA_TROWS, row_body, 0)


def _na_attn(q, k, v, bias_tab, tile_r0, tile_nr):
    t = q.shape[0]
    halo = NA_KH // 2
    tq = NA_TROWS * GRID_W
    hrows = halo * GRID_W
    per = tq // hrows
    nhb = t // hrows
    nt = t // tq
    qmap = lambda j, a, b: (j, 0)
    pmap = lambda j, a, b: (jnp.maximum(j * per - 1, 0), 0)
    nmap = lambda j, a, b: (jnp.minimum((j + 1) * per, nhb - 1), 0)
    gs = pltpu.PrefetchScalarGridSpec(
        num_scalar_prefetch=2, grid=(nt,),
        in_specs=[pl.BlockSpec((tq, C_W), qmap),
                  pl.BlockSpec((hrows, C_W), pmap), pl.BlockSpec((tq, C_W), qmap), pl.BlockSpec((hrows, C_W), nmap),
                  pl.BlockSpec((hrows, C_W), pmap), pl.BlockSpec((tq, C_W), qmap), pl.BlockSpec((hrows, C_W), nmap),
                  pl.BlockSpec(bias_tab.shape, lambda j, a, b: (0, 0, 0))],
        out_specs=pl.BlockSpec((tq, C_W), qmap),
        scratch_shapes=[pltpu.VMEM((tq + 2 * hrows, C_W), BF16), pltpu.VMEM((tq + 2 * hrows, C_W), BF16)])
    return pl.pallas_call(
        functools.partial(_na_kernel, halo=halo),
        grid_spec=gs,
        out_shape=jax.ShapeDtypeStruct((t, C_W), BF16),
        compiler_params=_cparams(("arbitrary",)),
        name="na_attn",
    )(tile_r0, tile_nr, q, k, k, k, v, v, v, bias_tab)


def _na_bias_table(rpb):
    c = np.arange(GRID_W)
    cs = np.clip(c - NA_KW // 2, 0, GRID_W - NA_KW)
    kc = np.arange(GRID_W)
    valid = (kc[:, None] >= cs[None, :]) & (kc[:, None] < cs[None, :] + NA_KW)
    cidx = np.clip(kc[:, None] - c[None, :] + NA_KW - 1, 0, 2 * NA_KW - 2)
    tab = rpb.astype(F32)[:, :, cidx]
    tab = jnp.where(jnp.asarray(valid)[None, None], tab, NEG_INF)
    pairs = jnp.concatenate([tab[0::2], tab[1::2]], axis=-1)
    return pairs.reshape((C_HEADS // 2) * (2 * NA_KH - 1), GRID_W, LANES)


def _route(h2, wr_ref, br_ref, idx_ref, gate_ref):
    tm = h2.shape[0]
    logits = lax.dot_general(wr_ref[...], h2, (((1,), (1,)), ((), ())),
                             precision=HIGHEST, preferred_element_type=F32) + br_ref[:, 0:1]
    eid = lax.broadcasted_iota(jnp.int32, (N_EXPERTS, tm), 0)
    vals, idxs = [], []
    for _ in range(TOP_K):
        m = jnp.max(logits, axis=0, keepdims=True)
        ix = jnp.min(jnp.where(logits == m, eid, N_EXPERTS), axis=0, keepdims=True)
        vals.append(m)
        idxs.append(ix)
        logits = jnp.where(eid == ix, -jnp.inf, logits)
    es = [jnp.exp(vk - vals[0]) for vk in vals]
    den = es[0] + es[1] + es[2] + es[3]
    pad_i = jnp.zeros((8 - TOP_K, tm), jnp.int32)
    pad_f = jnp.zeros((8 - TOP_K, tm), F32)
    idx_ref[...] = jnp.concatenate(idxs + [pad_i], axis=0)
    gate_ref[...] = jnp.concatenate([ek / den for ek in es] + [pad_f], axis=0)


PACK_W = D_MODEL // 2


def _pack_bf16_rows(h):
    bits = pltpu.bitcast(h.astype(BF16).astype(F32), jnp.int32)
    return (bits[:, :PACK_W] & jnp.int32(-65536)) | lax.shift_right_logical(bits[:, PACK_W:], 16)


def _unpack_bf16_rows(w):
    hi = pltpu.bitcast(w & jnp.int32(-65536), F32).astype(BF16)
    lo = pltpu.bitcast(lax.shift_left(w, 16), F32).astype(BF16)
    return jnp.concatenate([hi, lo], axis=1)


def _out0_kernel(sid_ref, x_ref, mod_ref, n2_ref, oa_ref, ob0_ref, ls0_ref, ob1_ref, ls1_ref, ob2_ref, ls2_ref,
                 wout_ref, wr_ref, br_ref, x1_ref, h2_ref, idx_ref, gate_ref, oscr, lscr, *, tm):
    del sid_ref
    os_, ls_ = [], []
    for gi, (o_ref, l_ref) in enumerate(((ob0_ref, ls0_ref), (ob1_ref, ls1_ref), (ob2_ref, ls2_ref))):
        dil = B_PAIRS[gi][1]
        if dil == 1:
            os_.append(o_ref[...])
            ls_.append(l_ref[...])
        else:
            ncg = B_GW // LANES
            for r in range(dil):
                for c in range(ncg):
                    lo = r * B_GW + c * LANES
                    oscr[gi * ncg + c, pl.ds(r, tm // dil, stride=dil), :] = o_ref[:, lo:lo + LANES]
                lscr[gi, pl.ds(r, tm // dil, stride=dil), :] = l_ref[:, r * LANES:(r + 1) * LANES]
            os_.append(jnp.concatenate([oscr[gi * ncg + c] for c in range(ncg)], axis=1))
            ls_.append(lscr[gi])
    lmax = jnp.maximum(jnp.maximum(ls_[0], ls_[1]), ls_[2])
    ws = [jnp.exp(l - lmax) for l in ls_]
    winv = 1.0 / (ws[0] + ws[1] + ws[2])
    lane = lax.broadcasted_iota(jnp.int32, (tm, LANES), 1)
    lo_lanes = lane < HEAD_DIM

    def head_weights(w):
        return jnp.concatenate(
            [jnp.where(lo_lanes, jnp.broadcast_to(w[:, 2 * pr:2 * pr + 1], (tm, LANES)),
                       jnp.broadcast_to(w[:, 2 * pr + 1:2 * pr + 2], (tm, LANES)))
             for pr in range(B_HEADS // 2)], axis=1)

    ob = (head_weights(ws[0] * winv) * os_[0] + head_weights(ws[1] * winv) * os_[1]
          + head_weights(ws[2] * winv) * os_[2])
    o = (jnp.dot(oa_ref[...], wout_ref[0:A_QW, :], preferred_element_type=F32)
         + jnp.dot(ob.astype(BF16), wout_ref[A_QW:, :], preferred_element_type=F32))
    x1 = x_ref[...] + mod_ref[2:3, :] * o
    x1_ref[...] = x1
    h2 = _rms_mod(x1, n2_ref, mod_ref, 3, 4)
    h2_ref[...] = _pack_bf16_rows(h2)
    _route(h2, wr_ref, br_ref, idx_ref, gate_ref)


def _out1_kernel(sid_ref, x_ref, mod_ref, n2_ref, oc_ref, wout_ref, wr_ref, br_ref,
                 x1_ref, h2_ref, idx_ref, gate_ref):
    del sid_ref
    o = jnp.dot(oc_ref[...], wout_ref[...], preferred_element_type=F32)
    x1 = x_ref[...] + mod_ref[2:3, :] * o
    x1_ref[...] = x1
    h2 = _rms_mod(x1, n2_ref, mod_ref, 3, 4)
    h2_ref[...] = _pack_bf16_rows(h2)
    _route(h2, wr_ref, br_ref, idx_ref, gate_ref)


def _out_proj(layer, x, mod, tile_sid, n2, attn, wout_bf, wr_t, br):
    t = x.shape[0]
    tm = TOKEN_TILE
    row = lambda i, sid: (i, 0)
    const = lambda i, sid: (0, 0)
    in_specs = [pl.BlockSpec((tm, D_MODEL), row),
                pl.BlockSpec((None, 6, D_MODEL), lambda i, sid: (sid[i], 0, 0)),
                pl.BlockSpec((1, D_MODEL), const)]
    scratch = []
    if layer == 0:
        in_specs.append(pl.BlockSpec((tm, A_QW), row))
        for (_, dil) in B_PAIRS:
            in_specs += [pl.BlockSpec((tm // dil, dil * B_GW), row), pl.BlockSpec((tm // dil, dil * LANES), row)]
        body = functools.partial(_out0_kernel, tm=tm)
        nscr = B_GROUPS * B_GW // LANES
        scratch = [pltpu.VMEM((nscr, tm, LANES), F32), pltpu.VMEM((B_GROUPS, tm, LANES), F32)]
    else:
        in_specs.append(pl.BlockSpec((tm, C_W), row))
        body = _out1_kernel
    in_specs += [pl.BlockSpec(wout_bf.shape, const),
                 pl.BlockSpec((N_EXPERTS, D_MODEL), const),
                 pl.BlockSpec((N_EXPERTS, LANES), const)]
    gs = pltpu.PrefetchScalarGridSpec(
        num_scalar_prefetch=1, grid=(t // tm,), in_specs=in_specs,
        out_specs=[pl.BlockSpec((tm, D_MODEL), row), pl.BlockSpec((tm, PACK_W), row),
                   pl.BlockSpec((8, tm), lambda i, sid: (0, i)), pl.BlockSpec((8, tm), lambda i, sid: (0, i))],
        scratch_shapes=scratch)
    return pl.pallas_call(
        body, grid_spec=gs,
        out_shape=[jax.ShapeDtypeStruct((t, D_MODEL), F32), jax.ShapeDtypeStruct((t, PACK_W), jnp.int32),
                   jax.ShapeDtypeStruct((8, t), jnp.int32), jax.ShapeDtypeStruct((8, t), F32)],
        compiler_params=_cparams(("arbitrary",)),
        name="out_proj%d" % layer,
    )(tile_sid, x, mod, n2.reshape(1, D_MODEL), *attn, wout_bf, wr_t, br)


def _rank_kernel(idx_ref, tri_ref, rank_ref, cnt_ref, carry):
    i = pl.program_id(0)

    @pl.when(i == 0)
    def _():
        carry[...] = jnp.zeros_like(carry)

    tk = idx_ref.shape[1]
    eid = lax.broadcasted_iota(jnp.int32, (N_EXPERTS, tk), 0)
    hits = [eid == idx_ref[k:k + 1, :] for k in range(TOP_K)]
    onehot = sum(h.astype(F32) for h in hits)
    incl = jnp.dot(onehot.astype(BF16), tri_ref[...], preferred_element_type=F32)
    before = incl - onehot + carry[:, 0:1]
    rows = [jnp.sum(jnp.where(h, before, 0.0), axis=0, keepdims=True) for h in hits]
    rows.append(jnp.zeros((8 - TOP_K, tk), F32))
    rank_ref[...] = jnp.concatenate(rows, axis=0).astype(jnp.int32)
    carry[...] = carry[...] + incl[:, tk - 1:tk]
    cnt_ref[...] = carry[...].astype(jnp.int32)


def _slot_kernel(idx_ref, rank_ref, start_ref, slot_ref):
    tk = idx_ref.shape[1]
    eid = lax.broadcasted_iota(jnp.int32, (N_EXPERTS, tk), 0)
    start = start_ref[:, 0:1]
    rows = []
    for k in range(TOP_K):
        base = jnp.sum(jnp.where(eid == idx_ref[k:k + 1, :], start, 0), axis=0, keepdims=True)
        rows.append(base + rank_ref[k:k + 1, :])
    rows.append(jnp.zeros((8 - TOP_K, tk), jnp.int32))
    slot_ref[...] = jnp.concatenate(rows, axis=0)


def _route_slots(idx):
    t = idx.shape[1]
    tk = ROUTE_TILE
    tri = jnp.asarray(np.triu(np.ones((tk, tk), np.float32)), BF16)
    tok = lambda i: (0, i)
    rank, cnt = pl.pallas_call(
        _rank_kernel, grid=(t // tk,),
        in_specs=[pl.BlockSpec((8, tk), tok), pl.BlockSpec((tk, tk), lambda i: (0, 0))],
        out_specs=[pl.BlockSpec((8, tk), tok), pl.BlockSpec((N_EXPERTS, LANES), lambda i: (0, 0))],
        out_shape=[jax.ShapeDtypeStruct((8, t), jnp.int32), jax.ShapeDtypeStruct((N_EXPERTS, LANES), jnp.int32)],
        scratch_shapes=[pltpu.VMEM((N_EXPERTS, LANES), F32)],
        compiler_params=_cparams(("arbitrary",)),
        name="route_rank",
    )(idx, tri)
    counts = cnt[:, 0]
    padded = (counts + MOE_BM - 1) // MOE_BM * MOE_BM
    pad_end = jnp.cumsum(padded)
    pad_start = pad_end - padded
    nblk = (t * TOP_K) // MOE_BM + N_EXPERTS
    blk_row0 = jnp.arange(nblk, dtype=jnp.int32) * MOE_BM
    blk_expert = jnp.minimum(jnp.sum((pad_end[None, :] <= blk_row0[:, None]).astype(jnp.int32), axis=1),
                             N_EXPERTS - 1)
    nused = (pad_end[-1] // MOE_BM).astype(jnp.int32).reshape(1)
    start_b = jnp.broadcast_to(pad_start.astype(jnp.int32)[:, None], (N_EXPERTS, LANES))
    slot = pl.pallas_call(
        _slot_kernel, grid=(t // tk,),
        in_specs=[pl.BlockSpec((8, tk), tok), pl.BlockSpec((8, tk), tok),
                  pl.BlockSpec((N_EXPERTS, LANES), lambda i: (0, 0))],
        out_specs=pl.BlockSpec((8, tk), tok),
        out_shape=jax.ShapeDtypeStruct((8, t), jnp.int32),
        compiler_params=_cparams(("arbitrary",)),
        name="route_slot",
    )(idx, rank, start_b)
    return slot, blk_expert, nused


SC_CORES = 2
SC_SUBCORES = 16
SC_ROW_BUFFER_BYTES = 256 * 1024
SC_MAX_INDEX_LIST = 128


def _sc_chunks(t, row_bytes):
    chunk = min(SC_MAX_INDEX_LIST, SC_ROW_BUFFER_BYTES // row_bytes)
    nchunk = t // chunk
    per = nchunk // (SC_CORES * SC_SUBCORES)
    assert per * SC_CORES * SC_SUBCORES * chunk == t
    return chunk, nchunk, per


def _sc_slot_lists(slot, nchunk, chunk):
    return slot[:TOP_K].reshape(TOP_K, nchunk, chunk).transpose(1, 0, 2)


def _sc_dispatch(h2, slot, nslots):
    t, w = h2.shape
    chunk, nchunk, per = _sc_chunks(t, w * h2.dtype.itemsize)
    mesh = plsc.VectorSubcoreMesh(core_axis_name="c", subcore_axis_name="s")

    @functools.partial(
        pl.kernel, mesh=mesh, out_type=jax.ShapeDtypeStruct((nslots, w), h2.dtype),
        scratch_types=[pltpu.VMEM((TOP_K, chunk), jnp.int32), pltpu.VMEM((chunk, w), h2.dtype)],
        name="moe_dispatch_sc")
    def body(h_hbm, slot_hbm, xs_hbm, idx_v, rows_v):
        wid = lax.axis_index("s") * SC_CORES + lax.axis_index("c")

        @pl.loop(0, per)
        def _(j):
            c = wid * per + j
            pltpu.sync_copy(slot_hbm.at[c], idx_v)
            pltpu.sync_copy(h_hbm.at[pl.ds(c * chunk, chunk)], rows_v)
            for k in range(TOP_K):
                pltpu.sync_copy(rows_v, xs_hbm.at[idx_v.at[k]])

    return body(h2, _sc_slot_lists(slot, nchunk, chunk))


def _sc_gather(ys, slot, t):
    w = ys.shape[1]
    chunk, nchunk, per = _sc_chunks(t, w * ys.dtype.itemsize)
    mesh = plsc.VectorSubcoreMesh(core_axis_name="c", subcore_axis_name="s")

    @functools.partial(
        pl.kernel, mesh=mesh, out_type=jax.ShapeDtypeStruct((TOP_K, t, w), ys.dtype),
        scratch_types=[pltpu.VMEM((TOP_K, chunk), jnp.int32), pltpu.VMEM((chunk, w), ys.dtype)],
        name="moe_gather_sc")
    def body(ys_hbm, slot_hbm, out_hbm, idx_v, rows_v):
        wid = lax.axis_index("s") * SC_CORES + lax.axis_index("c")

        @pl.loop(0, per)
        def _(j):
            c = wid * per + j
            pltpu.sync_copy(slot_hbm.at[c], idx_v)
            for k in range(TOP_K):
                pltpu.sync_copy(ys_hbm.at[idx_v.at[k]], rows_v)
                pltpu.sync_copy(rows_v, out_hbm.at[k, pl.ds(c * chunk, chunk)])

    return body(ys, _sc_slot_lists(slot, nchunk, chunk))


def _ffn_kernel(be_ref, nu_ref, xs_ref, wgu_ref, bgu_ref, wd_ref, bd_ref, y_ref, wgu_bf, wd_bf):
    i = pl.program_id(0)
    prev = be_ref[jnp.maximum(i - 1, 0)]

    @pl.when(i < nu_ref[0])
    def _():
        @pl.when((i == 0) | (be_ref[i] != prev))
        def _():
            wgu_bf[...] = wgu_ref[...].astype(BF16)
            wd_bf[...] = wd_ref[...].astype(BF16)

        x = _unpack_bf16_rows(xs_ref[...])
        gu = jnp.dot(x, wgu_bf[...], preferred_element_type=F32) + bgu_ref[...]
        gate = jnp.minimum(gu[:, :D_FF], SWIGLU_LIMIT)
        up = jnp.clip(gu[:, D_FF:], -SWIGLU_LIMIT, SWIGLU_LIMIT)
        act = (up + 1.0) * gate * jax.nn.sigmoid(SWIGLU_ALPHA * gate)
        y_ref[...] = jnp.dot(act.astype(BF16), wd_bf[...], preferred_element_type=F32) + bd_ref[...]

    @pl.when(i >= nu_ref[0])
    def _():
        y_ref[...] = jnp.zeros_like(y_ref)


def _expert_ffn(xs, blk_expert, nused, wgu, bgu, wd, bd):
    nslots = xs.shape[0]
    nblk = nslots // MOE_BM
    blk = lambda i, be, nu: (jnp.minimum(i, nu[0] - 1), 0)
    exp3 = lambda i, be, nu: (be[jnp.minimum(i, nu[0] - 1)], 0, 0)
    gs = pltpu.PrefetchScalarGridSpec(
        num_scalar_prefetch=2, grid=(nblk,),
        in_specs=[pl.BlockSpec((MOE_BM, PACK_W), blk),
                  pl.BlockSpec((None, D_MODEL, 2 * D_FF), exp3),
                  pl.BlockSpec((None, 1, 2 * D_FF), exp3),
                  pl.BlockSpec((None, D_FF, D_MODEL), exp3),
                  pl.BlockSpec((None, 1, D_MODEL), exp3)],
        out_specs=pl.BlockSpec((MOE_BM, D_MODEL), lambda i, be, nu: (i, 0)),
        scratch_shapes=[pltpu.VMEM((D_MODEL, 2 * D_FF), BF16), pltpu.VMEM((D_FF, D_MODEL), BF16)])
    return pl.pallas_call(
        _ffn_kernel, grid_spec=gs,
        out_shape=jax.ShapeDtypeStruct((nslots, D_MODEL), F32),
        compiler_params=_cparams(("arbitrary",)),
        name="moe_ffn",
    )(blk_expert, nused, xs, wgu, bgu.reshape(N_EXPERTS, 1, 2 * D_FF), wd, bd.reshape(N_EXPERTS, 1, D_MODEL))


def _combine_kernel(sid_ref, x_ref, mod_ref, gate_ref, yk_ref, o_ref, *, tm):
    del sid_ref
    gpad = jnp.concatenate([gate_ref[...], jnp.zeros((LANES - 8, tm), F32)], axis=0)
    gcol = gpad.T
    acc = gcol[:, 0:1] * yk_ref[0]
    for k in range(1, TOP_K):
        acc = acc + gcol[:, k:k + 1] * yk_ref[k]
    o_ref[...] = x_ref[...] + mod_ref[5:6, :] * acc


def _combine(x1, mod, tile_sid, gates, yk):
    t = x1.shape[0]
    tm = MOVE_TILE
    gs = pltpu.PrefetchScalarGridSpec(
        num_scalar_prefetch=1, grid=(t // tm,),
        in_specs=[pl.BlockSpec((tm, D_MODEL), lambda i, sid: (i, 0)),
                  pl.BlockSpec((None, 6, D_MODEL), lambda i, sid: (sid[i], 0, 0)),
                  pl.BlockSpec((8, tm), lambda i, sid: (0, i)),
                  pl.BlockSpec((TOP_K, tm, D_MODEL), lambda i, sid: (0, i, 0))],
        out_specs=pl.BlockSpec((tm, D_MODEL), lambda i, sid: (i, 0)))
    return pl.pallas_call(
        functools.partial(_combine_kernel, tm=tm), grid_spec=gs,
        out_shape=jax.ShapeDtypeStruct((t, D_MODEL), F32),
        compiler_params=_cparams(("arbitrary",)),
        name="moe_combine",
    )(tile_sid, x1, mod, gates, yk)


def _moe(x1, h2, idx, gates, mod, tile_sid_move, wgu, bgu, wd, bd):
    t = x1.shape[0]
    nslots = t * TOP_K + N_EXPERTS * MOE_BM
    slot, blk_expert, nused = _route_slots(idx)
    xs = _sc_dispatch(h2, slot, nslots)
    ys = _expert_ffn(xs, blk_expert, nused, wgu, bgu, wd, bd)
    return _combine(x1, mod, tile_sid_move, gates, _sc_gather(ys, slot, t))


def _tile_table(seq_lens, tile, fn):
    vals = []
    start = 0
    for sid, n in enumerate(seq_lens):
        assert n % tile == 0
        for j in range(n // tile):
            vals.append(fn(sid, start, n, j))
        start += n
    return jnp.asarray(np.asarray(vals, np.int32))


def _head_block_diag():
    bd = np.kron(np.eye(2 * LANES // HEAD_DIM, dtype=np.float32),
                 np.full((HEAD_DIM, HEAD_DIM), 1.0 / HEAD_DIM, np.float32))
    return jnp.asarray(bd, BF16)


_A_HEAD_ORDER = [kv * A_GROUP + g for g in range(A_GROUP) for kv in range(A_KV_HEADS)]


def _trunk(x, c_pad, seq_lens, p):
    t = x.shape[0]
    sid_tok = _tile_table(seq_lens, TOKEN_TILE, lambda sid, s, n, j: sid)
    sid_move = _tile_table(seq_lens, MOVE_TILE, lambda sid, s, n, j: sid)
    bd = _head_block_diag()

    mod0 = _ada_mod(c_pad, p["l0_ada_w"], p["l0_ada_b"]).reshape(c_pad.shape[0], 6, D_MODEL)
    w_in = p["l0_w_in"]
    a_cols = np.concatenate([np.arange(h * HEAD_DIM, (h + 1) * HEAD_DIM) for h in _A_HEAD_ORDER])
    bq, bk, bv = A_IN, A_IN + B_GROUPS * B_GW, A_IN + 2 * B_GROUPS * B_GW
    col_blocks = [w_in[:, a_cols], w_in[:, A_QW:A_IN]]
    for g in range(B_GROUPS):
        col_blocks += [w_in[:, bq + g * B_GW:bq + (g + 1) * B_GW],
                       w_in[:, bk + g * B_GW:bk + (g + 1) * B_GW],
                       w_in[:, bv + g * B_GW:bv + (g + 1) * B_GW]]
    w0 = jnp.concatenate(col_blocks, axis=1).astype(BF16)
    ones = lambda n: jnp.ones((n,), F32)
    gain_blocks = [jnp.tile(p["l0_q_norm_a"], A_Q_HEADS) * QK_SCALE, jnp.tile(p["l0_k_norm_a"], A_KV_HEADS),
                   ones(A_KVW)]
    for g in range(B_GROUPS):
        gain_blocks += [jnp.tile(p["l0_q_norm_b"], B_HEADS) * QK_SCALE, jnp.tile(p["l0_k_norm_b"], B_HEADS),
                        ones(B_GW)]
    gain0 = jnp.concatenate(gain_blocks).reshape(1, -1)
    slabs0 = [(0, A_QW, True, 1), (A_QW, A_KVW, True, 1), (A_QW + A_KVW, A_KVW, False, 1)]
    c0 = A_IN
    for g, (_, dil) in enumerate(B_PAIRS):
        slabs0 += [(c0, B_GW, True, dil), (c0 + B_GW, B_GW, True, dil), (c0 + 2 * B_GW, B_GW, False, dil)]
        c0 += 3 * B_GW
    proj = _in_proj(x, mod0, sid_tok, p["l0_norm1"], w0, gain0, bd, tuple(slabs0))
    qa, ka, va = proj[0:3]

    slopes_a = _alibi_slopes(A_Q_HEADS)
    heads_a = tuple(tuple((slopes_a[kv * A_GROUP + g], kv * A_GROUP + g) for kv in range(A_KV_HEADS))
                    for g in range(A_GROUP))
    (oa,) = _band_attn(qa, ka, va, seq_lens, tq=ATTN_TQ_A, dil=1, qw=A_QW, kvw=A_KVW, half=A_HALF, heads=heads_a,
                       kv_group=(0,) * A_GROUP, sink=p["l0_sink_a"].astype(F32), out_dtype=BF16, want_lse=False)

    slopes_b = _alibi_slopes(B_GROUPS * B_HEADS)
    attn0 = [oa]
    for g, (window, dil) in enumerate(B_PAIRS):
        qg, kg, vg = proj[3 + 3 * g:6 + 3 * g]
        heads_b = tuple(tuple((slopes_b[g * B_HEADS + 2 * pr + s] * dil, 0) for s in range(2))
                        for pr in range(B_HEADS // 2))
        og, lg = _band_attn(qg, kg, vg, [n // dil for n in seq_lens], tq=ATTN_TQ_B, dil=dil, qw=B_GW, kvw=B_GW, half=window // (2 * dil),
                            heads=heads_b, kv_group=tuple(range(B_HEADS // 2)), sink=None, out_dtype=F32,
                            want_lse=True)
        attn0 += [og, lg]

    w_out = p["l0_w_out"]
    wout0 = jnp.concatenate([w_out[a_cols], w_out[A_QW:]], axis=0).astype(BF16)
    br0 = jnp.broadcast_to(p["l0_b_router"].astype(F32)[:, None], (N_EXPERTS, LANES))
    x1, h2, idx, gates = _out_proj(0, x, mod0, sid_tok, p["l0_norm2"], attn0, wout0, p["l0_w_router"].T, br0)
    x = _moe(x1, h2, idx, gates, mod0, sid_move, p["l0_w_gate_up"], p["l0_b_gate_up"], p["l0_w_down"],
             p["l0_b_down"])

    mod1 = _ada_mod(c_pad, p["l1_ada_w"], p["l1_ada_b"]).reshape(c_pad.shape[0], 6, D_MODEL)
    w1 = p["l1_w_in"].astype(BF16)
    gain1 = jnp.concatenate([jnp.tile(p["l1_q_norm_c"], C_HEADS) * QK_SCALE, jnp.tile(p["l1_k_norm_c"], C_HEADS),
                             ones(C_W)]).reshape(1, -1)
    slabs1 = ((0, C_W, True, 1), (C_W, C_W, True, 1), (2 * C_W, C_W, False, 1))
    qc, kc, vc = _in_proj(x, mod1, sid_tok, p["l1_norm1"], w1, gain1, bd, slabs1)
    na_tile = NA_TROWS * GRID_W
    tile_r0 = _tile_table(seq_lens, na_tile, lambda sid, s, n, j: j * NA_TROWS)
    tile_nr = _tile_table(seq_lens, na_tile, lambda sid, s, n, j: n // GRID_W)
    oc = _na_attn(qc, kc, vc, _na_bias_table(p["l1_rpb_c"]), tile_r0, tile_nr)
    br1 = jnp.broadcast_to(p["l1_b_router"].astype(F32)[:, None], (N_EXPERTS, LANES))
    x1, h2, idx, gates = _out_proj(1, x, mod1, sid_tok, p["l1_norm2"], [oc], p["l1_w_out"].astype(BF16),
                                   p["l1_w_router"].T, br1)
    return _moe(x1, h2, idx, gates, mod1, sid_move, p["l1_w_gate_up"], p["l1_b_gate_up"], p["l1_w_down"],
                p["l1_b_down"])


def kernel(x_prompt, x_sample, c_prompt, c_sample, l0_ada_w, l0_ada_b, l0_norm1, l0_w_in, l0_q_norm_a, l0_k_norm_a, l0_sink_a, l0_q_norm_b, l0_k_norm_b, l0_w_out, l0_norm2, l0_w_router, l0_b_router, l0_w_gate_up, l0_b_gate_up, l0_w_down, l0_b_down, l1_ada_w, l1_ada_b, l1_norm1, l1_w_in, l1_q_norm_c, l1_k_norm_c, l1_rpb_c, l1_w_out, l1_norm2, l1_w_router, l1_b_router, l1_w_gate_up, l1_b_gate_up, l1_w_down, l1_b_down):
    p = dict(l0_ada_w=l0_ada_w, l0_ada_b=l0_ada_b, l0_norm1=l0_norm1, l0_w_in=l0_w_in, l0_q_norm_a=l0_q_norm_a,
             l0_k_norm_a=l0_k_norm_a, l0_sink_a=l0_sink_a, l0_q_norm_b=l0_q_norm_b, l0_k_norm_b=l0_k_norm_b,
             l0_w_out=l0_w_out, l0_norm2=l0_norm2, l0_w_router=l0_w_router, l0_b_router=l0_b_router,
             l0_w_gate_up=l0_w_gate_up, l0_b_gate_up=l0_b_gate_up, l0_w_down=l0_w_down, l0_b_down=l0_b_down,
             l1_ada_w=l1_ada_w, l1_ada_b=l1_ada_b, l1_norm1=l1_norm1, l1_w_in=l1_w_in, l1_q_norm_c=l1_q_norm_c,
             l1_k_norm_c=l1_k_norm_c, l1_rpb_c=l1_rpb_c, l1_w_out=l1_w_out, l1_norm2=l1_norm2,
             l1_w_router=l1_w_router, l1_b_router=l1_b_router, l1_w_gate_up=l1_w_gate_up,
             l1_b_gate_up=l1_b_gate_up, l1_w_down=l1_w_down, l1_b_down=l1_b_down)
    nb_p, len_p, d = x_prompt.shape
    nb_s, len_s, _ = x_sample.shape
    seq_lens = (len_p,) * nb_p + (len_s,) * nb_s
    x = jnp.concatenate([x_prompt.reshape(nb_p * len_p, d), x_sample.reshape(nb_s * len_s, d)], axis=0)
    nseq = nb_p + nb_s
    c_pad = jnp.concatenate([c_prompt, c_sample, jnp.zeros((-nseq % 8, d), F32)], axis=0)
    y = _trunk(x, c_pad, seq_lens, p)
    y_prompt = y[:nb_p * len_p].reshape(nb_p, len_p, d)
    y_sample = y[nb_p * len_p:].reshape(nb_s, len_s, d)
    return (y_prompt, y_sample)
```

```python
import functools
import math

import numpy as np
import jax
import jax.numpy as jnp
from jax import lax
from jax.experimental import pallas as pl
from jax.experimental.pallas import tpu as pltpu
from jax.experimental.pallas import tpu_sc as plsc

F32 = jnp.float32
BF16 = jnp.bfloat16
HIGHEST = lax.Precision.HIGHEST

D_MODEL = 1024
HEAD_DIM = 64
LANES = 128
GRID_W = 64
A_Q_HEADS = 8
A_KV_HEADS = 2
A_GROUP = A_Q_HEADS // A_KV_HEADS
A_HALF = 128
B_PAIRS = ((128, 1), (512, 4), (2048, 16))
B_GROUPS = 3
B_HEADS = 4
A_QW = A_Q_HEADS * HEAD_DIM
A_KVW = A_KV_HEADS * HEAD_DIM
A_IN = A_QW + 2 * A_KVW
B_GW = B_HEADS * HEAD_DIM
C_HEADS = 16
C_W = C_HEADS * HEAD_DIM
NA_KH = 8
NA_KW = 16
N_EXPERTS = 32
TOP_K = 4
D_FF = 1024
SWIGLU_LIMIT = 7.0
SWIGLU_ALPHA = 1.702
RMS_EPS = 1e-6
NEG_INF = -1e30
QK_SCALE = HEAD_DIM ** -0.5

TOKEN_TILE = 512
ATTN_TQ_A = 256
ATTN_TQ_B = 512
ATTN_UQ = 128
NA_TROWS = 8
MOE_BM = 256
ROUTE_TILE = 512
MOVE_TILE = 256
VMEM_LIMIT = 56 * 1024 * 1024


def _alibi_slopes(n):
    return [float(2.0 ** (-8.0 * (j + 1) / n)) for j in range(n)]


def _cparams(sem, flags=None):
    return pltpu.CompilerParams(dimension_semantics=sem, vmem_limit_bytes=VMEM_LIMIT, flags=flags)


def _mod_kernel(c_ref, w_ref, b_ref, o_ref):
    c = c_ref[...]
    s = c * jax.nn.sigmoid(c)
    o_ref[...] = jnp.dot(s, w_ref[...], precision=HIGHEST, preferred_element_type=F32) + b_ref[...]


def _ada_mod(c_pad, w, b):
    nrow = c_pad.shape[0]
    ncol = w.shape[1]
    tn = D_MODEL
    return pl.pallas_call(
        _mod_kernel,
        grid=(ncol // tn,),
        in_specs=[pl.BlockSpec((nrow, D_MODEL), lambda j: (0, 0)),
                  pl.BlockSpec((D_MODEL, tn), lambda j: (0, j)),
                  pl.BlockSpec((1, tn), lambda j: (0, j))],
        out_specs=pl.BlockSpec((nrow, tn), lambda j: (0, j)),
        out_shape=jax.ShapeDtypeStruct((nrow, ncol), F32),
        compiler_params=_cparams(("arbitrary",)),
        name="ada_mod",
    )(c_pad, w, b.reshape(1, ncol))


def _head_mean_sq(y, bd_ref):
    w = y.shape[1]
    outs = []
    for c0 in range(0, w, 2 * LANES):
        cw = min(2 * LANES, w - c0)
        sq = y[:, c0:c0 + cw] * y[:, c0:c0 + cw]
        hi = sq.astype(BF16)
        lo = (sq - hi.astype(F32)).astype(BF16)
        bd = bd_ref[0:cw, 0:cw]
        outs.append(jnp.dot(hi, bd, preferred_element_type=F32)
                    + jnp.dot(lo, bd, preferred_element_type=F32))
    return outs[0] if len(outs) == 1 else jnp.concatenate(outs, axis=1)


def _rms_mod(x, g_ref, mod_ref, shift_row, scale_row):
    ms = jnp.mean(x * x, axis=-1, keepdims=True)
    xn = x * lax.rsqrt(ms + RMS_EPS) * g_ref[...]
    return xn * (1.0 + mod_ref[scale_row:scale_row + 1, :]) + mod_ref[shift_row:shift_row + 1, :]


def _part_tiles(parts, tile):
    starts, s = [], 0
    for a in parts:
        assert a.shape[0] % tile == 0
        starts.append(s)
        s += a.shape[0] // tile
    return tuple(starts), s


def _part_specs(parts, starts, tile, ncol):
    def spec(a, s0):
        n = a.shape[0] // tile
        return pl.BlockSpec((tile, ncol), lambda i, *_: (jnp.clip(i - s0, 0, n - 1), 0))
    return [spec(a, s0) for a, s0 in zip(parts, starts)]


def _read_part(refs, starts):
    i = pl.program_id(0)
    x = refs[0][...]
    for ref, s0 in zip(refs[1:], starts[1:]):
        x = jnp.where(i >= s0, ref[...], x)
    return x


def _in_kernel(sid_ref, *refs, slabs, tm, part_starts):
    del sid_ref
    npart = len(part_starts)
    x_refs = refs[:npart]
    mod_ref, n1_ref, w_ref, gain_ref, bd_ref = refs[npart:npart + 5]
    rest = refs[npart + 5:]
    nout = len(slabs)
    out_refs = rest[:nout]
    scr_ref = rest[nout] if len(rest) > nout else None
    h = _rms_mod(_read_part(x_refs, part_starts), n1_ref, mod_ref, 0, 1).astype(BF16)
    for (c0, w, normed, dil), o_ref in zip(slabs, out_refs):
        y = jnp.dot(h, w_ref[:, c0:c0 + w], preferred_element_type=F32)
        if normed:
            y = y * lax.rsqrt(_head_mean_sq(y, bd_ref) + RMS_EPS) * gain_ref[:, c0:c0 + w]
        if dil == 1:
            o_ref[...] = y.astype(BF16)
        else:
            for c in range(w // LANES):
                scr_ref[c] = y[:, c * LANES:(c + 1) * LANES]
            for r in range(dil):
                for c in range(w // LANES):
                    o_ref[:, r * w + c * LANES:r * w + (c + 1) * LANES] = (
                        scr_ref[c, pl.ds(r, tm // dil, stride=dil), :].astype(BF16))


def _in_proj(x_parts, mod, tile_sid, n1, w_bf, gain, bd, slabs):
    tm = TOKEN_TILE
    part_starts, ntiles = _part_tiles(x_parts, tm)
    t = ntiles * tm
    ncols = w_bf.shape[1]
    out_shapes, out_specs = [], []
    for (c0, w, normed, dil) in slabs:
        out_shapes.append(jax.ShapeDtypeStruct((t // dil, dil * w), BF16))
        out_specs.append(pl.BlockSpec((tm // dil, dil * w), lambda i, sid: (i, 0)))
    need_scr = any(s[3] > 1 for s in slabs)
    scratch = [pltpu.VMEM((max(s[1] for s in slabs if s[3] > 1) // LANES, tm, LANES), F32)] if need_scr else []
    gs = pltpu.PrefetchScalarGridSpec(
        num_scalar_prefetch=1,
        grid=(t // tm,),
        in_specs=_part_specs(x_parts, part_starts, tm, D_MODEL) + [
            pl.BlockSpec((None, 6, D_MODEL), lambda i, sid: (sid[i], 0, 0)),
            pl.BlockSpec((1, D_MODEL), lambda i, sid: (0, 0)),
            pl.BlockSpec((D_MODEL, ncols), lambda i, sid: (0, 0)),
            pl.BlockSpec((1, ncols), lambda i, sid: (0, 0)),
            pl.BlockSpec((2 * LANES, 2 * LANES), lambda i, sid: (0, 0))],
        out_specs=out_specs,
        scratch_shapes=scratch)
    return pl.pallas_call(
        functools.partial(_in_kernel, slabs=slabs, tm=tm, part_starts=part_starts),
        grid_spec=gs,
        out_shape=out_shapes,
        compiler_params=_cparams(("arbitrary",)),
        name="in_proj",
    )(tile_sid, *x_parts, mod, n1.reshape(1, D_MODEL), w_bf, gain, bd)


def _band_kernel(s0_ref, s1_ref, *refs, tq, hb, half, heads, kv_group, has_sink, want_lse):
    pos = 0
    sink_ref = None
    if has_sink:
        sink_ref = refs[0]
        pos = 1
    q_ref, kp_ref, kc_ref, kn_ref, vp_ref, vc_ref, vn_ref = refs[pos:pos + 7]
    o_ref = refs[pos + 7]
    lse_ref = refs[pos + 8] if want_lse else None
    i = pl.program_id(1)
    uq = ATTN_UQ
    wlen = uq + 2 * hb
    kall = jnp.concatenate([kp_ref[...], kc_ref[...], kn_ref[...]], axis=0)
    vall = jnp.concatenate([vp_ref[...], vc_ref[...], vn_ref[...]], axis=0)
    lane = lax.broadcasted_iota(jnp.int32, (uq, LANES), 1)
    lo_lanes = lane < HEAD_DIM
    for sub in range(tq // uq):
        rows = slice(sub * uq, (sub + 1) * uq)
        kwin = kall[sub * uq:sub * uq + wlen]
        vwin = vall[sub * uq:sub * uq + wlen]
        q0 = i * tq + sub * uq
        kpos = q0 - hb + lax.broadcasted_iota(jnp.int32, (wlen, uq), 0)
        qpos = q0 + lax.broadcasted_iota(jnp.int32, (wlen, uq), 1)
        rel = kpos - qpos
        valid = (jnp.abs(rel) <= half) & (kpos >= s0_ref[i]) & (kpos < s1_ref[i])
        distm = jnp.where(valid, jnp.abs(rel).astype(F32), 1e32)
        lse_rows = []
        for g, pair in enumerate(heads):
            qg = q_ref[rows, g * LANES:(g + 1) * LANES]
            kg = kwin[:, kv_group[g] * LANES:(kv_group[g] + 1) * LANES]
            vg = vwin[:, kv_group[g] * LANES:(kv_group[g] + 1) * LANES]
            outs = []
            for s, (slope, sink_idx) in enumerate(pair):
                keep = lo_lanes if s == 0 else jnp.logical_not(lo_lanes)
                qm = jnp.where(keep, qg, jnp.zeros_like(qg))
                sc = lax.dot_general(kg, qm, (((1,), (1,)), ((), ())), preferred_element_type=F32)
                sc = sc - slope * distm
                m = jnp.max(sc, axis=0, keepdims=True)
                if has_sink:
                    m = jnp.maximum(m, sink_ref[sink_idx])
                e = jnp.exp(sc - m)
                den = jnp.sum(e, axis=0, keepdims=True)
                if has_sink:
                    den = den + jnp.exp(sink_ref[sink_idx] - m)
                p = (e * (1.0 / den)).astype(BF16)
                outs.append(lax.dot_general(p, vg, (((0,), (0,)), ((), ())), preferred_element_type=F32))
                if want_lse:
                    lse_rows.append(m + jnp.log(den))
            o_ref[rows, g * LANES:(g + 1) * LANES] = jnp.where(lo_lanes, outs[0], outs[1]).astype(o_ref.dtype)
        if want_lse:
            lse_t = jnp.concatenate(lse_rows + [jnp.zeros((LANES - len(lse_rows), uq), F32)], axis=0)
            lse_ref[rows, :] = lse_t.T


def _band_attn(q, k, v, seq_rows, *, tq, dil, qw, kvw, half, heads, kv_group, sink, out_dtype, want_lse):
    rows = q.shape[0]
    hb = half
    s0 = _tile_table(seq_rows, tq, lambda sid, s, n, j: s)
    s1 = _tile_table(seq_rows, tq, lambda sid, s, n, j: s + n)
    per = tq // hb
    nt = rows // tq
    nhb = rows // hb
    has_sink = sink is not None
    in_specs = []
    args = []
    if has_sink:
        in_specs.append(pl.BlockSpec(memory_space=pltpu.SMEM))
        args.append(sink)
    qmap = lambda r, i, a, b: (i, r)
    pmap = lambda r, i, a, b: (jnp.maximum(i * per - 1, 0), r)
    nmap = lambda r, i, a, b: (jnp.minimum((i + 1) * per, nhb - 1), r)
    in_specs += [pl.BlockSpec((tq, qw), qmap),
                 pl.BlockSpec((hb, kvw), pmap), pl.BlockSpec((tq, kvw), qmap), pl.BlockSpec((hb, kvw), nmap),
                 pl.BlockSpec((hb, kvw), pmap), pl.BlockSpec((tq, kvw), qmap), pl.BlockSpec((hb, kvw), nmap)]
    args += [q, k, k, k, v, v, v]
    out_shape = [jax.ShapeDtypeStruct((rows, dil * qw), out_dtype)]
    out_specs = [pl.BlockSpec((tq, qw), qmap)]
    if want_lse:
        out_shape.append(jax.ShapeDtypeStruct((rows, dil * LANES), F32))
        out_specs.append(pl.BlockSpec((tq, LANES), qmap))
    gs = pltpu.PrefetchScalarGridSpec(num_scalar_prefetch=2, grid=(dil, nt),
                                      in_specs=in_specs, out_specs=out_specs)
    res = pl.pallas_call(
        functools.partial(_band_kernel, tq=tq, hb=hb, half=half, heads=heads, kv_group=kv_group,
                          has_sink=has_sink, want_lse=want_lse),
        grid_spec=gs,
        out_shape=out_shape,
        compiler_params=_cparams(("arbitrary", "arbitrary")),
        name="band_attn_d%d" % dil,
    )(s0, s1, *args)
    return res


def _na_kernel(r0_ref, nr_ref, q_ref, kp_ref, kc_ref, kn_ref, vp_ref, vc_ref, vn_ref, bias_ref,
               o_ref, kcat, vcat, *, halo):
    j = pl.program_id(0)
    tq = NA_TROWS * GRID_W
    hrows = halo * GRID_W
    kcat[0:hrows, :] = kp_ref[...]
    kcat[hrows:hrows + tq, :] = kc_ref[...]
    kcat[hrows + tq:, :] = kn_ref[...]
    vcat[0:hrows, :] = vp_ref[...]
    vcat[hrows:hrows + tq, :] = vc_ref[...]
    vcat[hrows + tq:, :] = vn_ref[...]
    r0 = r0_ref[j]
    nrows = nr_ref[j]
    kwin_len = NA_KH * GRID_W
    lane = lax.broadcasted_iota(jnp.int32, (GRID_W, LANES), 1)
    lo_lanes = lane < HEAD_DIM
    nrel = 2 * NA_KH - 1

    def row_body(u, carry):
        r = r0 + u
        rs = jnp.clip(r - NA_KH // 2, 0, nrows - NA_KH)
        off = pl.multiple_of((rs - r0 + halo) * GRID_W, GRID_W)
        bvar = rs - r + NA_KH - 1
        qrow = pl.multiple_of(u * GRID_W, GRID_W)
        for g in range(C_HEADS // 2):
            qg = q_ref[pl.ds(qrow, GRID_W), g * LANES:(g + 1) * LANES]
            qs = jnp.concatenate([jnp.where(lo_lanes, qg, jnp.zeros_like(qg)),
                                  jnp.where(lo_lanes, jnp.zeros_like(qg), qg)], axis=0)
            kg = kcat[pl.ds(off, kwin_len), g * LANES:(g + 1) * LANES]
            vg = vcat[pl.ds(off, kwin_len), g * LANES:(g + 1) * LANES]
            sc = lax.dot_general(kg, qs, (((1,), (1,)), ((), ())), preferred_element_type=F32)
            sc = sc + jnp.concatenate([bias_ref[g * nrel + bvar + kr] for kr in range(NA_KH)], axis=0)
            m = jnp.max(sc, axis=0, keepdims=True)
            e = jnp.exp(sc - m)
            den = jnp.sum(e, axis=0, keepdims=True)
            p = (e * (1.0 / den)).astype(BF16)
            pv = lax.dot_general(p, vg, (((0,), (0,)), ((), ())), preferred_element_type=F32)
            o = jnp.where(lo_lanes, pv[0:GRID_W], pv[GRID_W:])
            o_ref[pl.ds(qrow, GRID_W), g * LANES:(g + 1) * LANES] = o.astype(o_ref.dtype)
        return carry

    lax.fori_loop(0, NA_TROWS, row_body, 0)


def _na_attn(q, k, v, bias_tab, tile_r0, tile_nr):
    t = q.shape[0]
    halo = NA_KH // 2
    tq = NA_TROWS * GRID_W
    hrows = halo * GRID_W
    per = tq // hrows
    nhb = t // hrows
    nt = t // tq
    qmap = lambda j, a, b: (j, 0)
    pmap = lambda j, a, b: (jnp.maximum(j * per - 1, 0), 0)
    nmap = lambda j, a, b: (jnp.minimum((j + 1) * per, nhb - 1), 0)
    gs = pltpu.PrefetchScalarGridSpec(
        num_scalar_prefetch=2, grid=(nt,),
        in_specs=[pl.BlockSpec((tq, C_W), qmap),
                  pl.BlockSpec((hrows, C_W), pmap), pl.BlockSpec((tq, C_W), qmap), pl.BlockSpec((hrows, C_W), nmap),
                  pl.BlockSpec((hrows, C_W), pmap), pl.BlockSpec((tq, C_W), qmap), pl.BlockSpec((hrows, C_W), nmap),
                  pl.BlockSpec(bias_tab.shape, lambda j, a, b: (0, 0, 0))],
        out_specs=pl.BlockSpec((tq, C_W), qmap),
        scratch_shapes=[pltpu.VMEM((tq + 2 * hrows, C_W), BF16), pltpu.VMEM((tq + 2 * hrows, C_W), BF16)])
    return pl.pallas_call(
        functools.partial(_na_kernel, halo=halo),
        grid_spec=gs,
        out_shape=jax.ShapeDtypeStruct((t, C_W), BF16),
        compiler_params=_cparams(("arbitrary",)),
        name="na_attn",
    )(tile_r0, tile_nr, q, k, k, k, v, v, v, bias_tab)


def _na_bias_table(rpb):
    c = np.arange(GRID_W)
    cs = np.clip(c - NA_KW // 2, 0, GRID_W - NA_KW)
    kc = np.arange(GRID_W)
    valid = (kc[:, None] >= cs[None, :]) & (kc[:, None] < cs[None, :] + NA_KW)
    cidx = np.clip(kc[:, None] - c[None, :] + NA_KW - 1, 0, 2 * NA_KW - 2)
    tab = rpb.astype(F32)[:, :, cidx]
    tab = jnp.where(jnp.asarray(valid)[None, None], tab, NEG_INF)
    pairs = jnp.concatenate([tab[0::2], tab[1::2]], axis=-1)
    return pairs.reshape((C_HEADS // 2) * (2 * NA_KH - 1), GRID_W, LANES)


def _route(h2, wr_ref, br_ref, idx_ref, gate_ref):
    tm = h2.shape[0]
    logits = lax.dot_general(wr_ref[...], h2, (((1,), (1,)), ((), ())),
                             precision=HIGHEST, preferred_element_type=F32) + br_ref[:, 0:1]
    eid = lax.broadcasted_iota(jnp.int32, (N_EXPERTS, tm), 0)
    vals, idxs = [], []
    for _ in range(TOP_K):
        m = jnp.max(logits, axis=0, keepdims=True)
        ix = jnp.min(jnp.where(logits == m, eid, N_EXPERTS), axis=0, keepdims=True)
        vals.append(m)
        idxs.append(ix)
        logits = jnp.where(eid == ix, -jnp.inf, logits)
    es = [jnp.exp(vk - vals[0]) for vk in vals]
    den = es[0] + es[1] + es[2] + es[3]
    pad_i = jnp.zeros((8 - TOP_K, tm), jnp.int32)
    pad_f = jnp.zeros((8 - TOP_K, tm), F32)
    idx_ref[...] = jnp.concatenate(idxs + [pad_i], axis=0)
    gate_ref[...] = jnp.concatenate([ek / den for ek in es] + [pad_f], axis=0)


PACK_W = D_MODEL // 2


def _pack_bf16_rows(h):
    bits = pltpu.bitcast(h.astype(BF16).astype(F32), jnp.int32)
    return (bits[:, :PACK_W] & jnp.int32(-65536)) | lax.shift_right_logical(bits[:, PACK_W:], 16)


def _unpack_bf16_rows(w):
    hi = pltpu.bitcast(w & jnp.int32(-65536), F32).astype(BF16)
    lo = pltpu.bitcast(lax.shift_left(w, 16), F32).astype(BF16)
    return jnp.concatenate([hi, lo], axis=1)


def _out0_kernel(sid_ref, *refs, tm, part_starts):
    del sid_ref
    npart = len(part_starts)
    x_refs = refs[:npart]
    (mod_ref, n2_ref, oa_ref, ob0_ref, ls0_ref, ob1_ref, ls1_ref, ob2_ref, ls2_ref,
     wout_ref, wr_ref, br_ref, x1_ref, h2_ref, idx_ref, gate_ref, oscr, lscr) = refs[npart:]
    os_, ls_ = [], []
    for gi, (o_ref, l_ref) in enumerate(((ob0_ref, ls0_ref), (ob1_ref, ls1_ref), (ob2_ref, ls2_ref))):
        dil = B_PAIRS[gi][1]
        if dil == 1:
            os_.append(o_ref[...])
            ls_.append(l_ref[...])
        else:
            ncg = B_GW // LANES
            for r in range(dil):
                for c in range(ncg):
                    lo = r * B_GW + c * LANES
                    oscr[gi * ncg + c, pl.ds(r, tm // dil, stride=dil), :] = o_ref[:, lo:lo + LANES]
                lscr[gi, pl.ds(r, tm // dil, stride=dil), :] = l_ref[:, r * LANES:(r + 1) * LANES]
            os_.append(jnp.concatenate([oscr[gi * ncg + c] for c in range(ncg)], axis=1))
            ls_.append(lscr[gi])
    lmax = jnp.maximum(jnp.maximum(ls_[0], ls_[1]), ls_[2])
    ws = [jnp.exp(l - lmax) for l in ls_]
    winv = 1.0 / (ws[0] + ws[1] + ws[2])
    lane = lax.broadcasted_iota(jnp.int32, (tm, LANES), 1)
    lo_lanes = lane < HEAD_DIM

    def head_weights(w):
        return jnp.concatenate(
            [jnp.where(lo_lanes, jnp.broadcast_to(w[:, 2 * pr:2 * pr + 1], (tm, LANES)),
                       jnp.broadcast_to(w[:, 2 * pr + 1:2 * pr + 2], (tm, LANES)))
             for pr in range(B_HEADS // 2)], axis=1)

    ob = (head_weights(ws[0] * winv) * os_[0] + head_weights(ws[1] * winv) * os_[1]
          + head_weights(ws[2] * winv) * os_[2])
    o = (jnp.dot(oa_ref[...], wout_ref[0:A_QW, :], preferred_element_type=F32)
         + jnp.dot(ob.astype(BF16), wout_ref[A_QW:, :], preferred_element_type=F32))
    x1 = _read_part(x_refs, part_starts) + mod_ref[2:3, :] * o
    x1_ref[...] = x1
    h2 = _rms_mod(x1, n2_ref, mod_ref, 3, 4)
    h2_ref[...] = _pack_bf16_rows(h2)
    _route(h2, wr_ref, br_ref, idx_ref, gate_ref)


def _out1_kernel(sid_ref, *refs, part_starts):
    del sid_ref
    npart = len(part_starts)
    x_refs = refs[:npart]
    mod_ref, n2_ref, oc_ref, wout_ref, wr_ref, br_ref, x1_ref, h2_ref, idx_ref, gate_ref = refs[npart:]
    o = jnp.dot(oc_ref[...], wout_ref[...], preferred_element_type=F32)
    x1 = _read_part(x_refs, part_starts) + mod_ref[2:3, :] * o
    x1_ref[...] = x1
    h2 = _rms_mod(x1, n2_ref, mod_ref, 3, 4)
    h2_ref[...] = _pack_bf16_rows(h2)
    _route(h2, wr_ref, br_ref, idx_ref, gate_ref)


def _out_proj(layer, x_parts, mod, tile_sid, n2, attn, wout_bf, wr_t, br):
    tm = TOKEN_TILE
    part_starts, ntiles = _part_tiles(x_parts, tm)
    t = ntiles * tm
    row = lambda i, sid: (i, 0)
    const = lambda i, sid: (0, 0)
    in_specs = _part_specs(x_parts, part_starts, tm, D_MODEL) + [
        pl.BlockSpec((None, 6, D_MODEL), lambda i, sid: (sid[i], 0, 0)),
        pl.BlockSpec((1, D_MODEL), const)]
    scratch = []
    if layer == 0:
        in_specs.append(pl.BlockSpec((tm, A_QW), row))
        for (_, dil) in B_PAIRS:
            in_specs += [pl.BlockSpec((tm // dil, dil * B_GW), row), pl.BlockSpec((tm // dil, dil * LANES), row)]
        body = functools.partial(_out0_kernel, tm=tm, part_starts=part_starts)
        nscr = B_GROUPS * B_GW // LANES
        scratch = [pltpu.VMEM((nscr, tm, LANES), F32), pltpu.VMEM((B_GROUPS, tm, LANES), F32)]
    else:
        in_specs.append(pl.BlockSpec((tm, C_W), row))
        body = functools.partial(_out1_kernel, part_starts=part_starts)
    in_specs += [pl.BlockSpec(wout_bf.shape, const),
                 pl.BlockSpec((N_EXPERTS, D_MODEL), const),
                 pl.BlockSpec((N_EXPERTS, LANES), const)]
    gs = pltpu.PrefetchScalarGridSpec(
        num_scalar_prefetch=1, grid=(t // tm,), in_specs=in_specs,
        out_specs=[pl.BlockSpec((tm, D_MODEL), row), pl.BlockSpec((tm, PACK_W), row),
                   pl.BlockSpec((8, tm), lambda i, sid: (0, i)), pl.BlockSpec((8, tm), lambda i, sid: (0, i))],
        scratch_shapes=scratch)
    return pl.pallas_call(
        body, grid_spec=gs,
        out_shape=[jax.ShapeDtypeStruct((t, D_MODEL), F32), jax.ShapeDtypeStruct((t, PACK_W), jnp.int32),
                   jax.ShapeDtypeStruct((8, t), jnp.int32), jax.ShapeDtypeStruct((8, t), F32)],
        compiler_params=_cparams(("arbitrary",)),
        name="out_proj%d" % layer,
    )(tile_sid, *x_parts, mod, n2.reshape(1, D_MODEL), *attn, wout_bf, wr_t, br)


def _rank_kernel(idx_ref, tri_ref, rank_ref, cnt_ref, carry):
    i = pl.program_id(0)

    @pl.when(i == 0)
    def _():
        carry[...] = jnp.zeros_like(carry)

    tk = idx_ref.shape[1]
    eid = lax.broadcasted_iota(jnp.int32, (N_EXPERTS, tk), 0)
    hits = [eid == idx_ref[k:k + 1, :] for k in range(TOP_K)]
    onehot = sum(h.astype(F32) for h in hits)
    incl = jnp.dot(onehot.astype(BF16), tri_ref[...], preferred_element_type=F32)
    before = incl - onehot + carry[:, 0:1]
    rows = [jnp.sum(jnp.where(h, before, 0.0), axis=0, keepdims=True) for h in hits]
    rows.append(jnp.zeros((8 - TOP_K, tk), F32))
    rank_ref[...] = jnp.concatenate(rows, axis=0).astype(jnp.int32)
    carry[...] = carry[...] + incl[:, tk - 1:tk]
    cnt_ref[...] = carry[...].astype(jnp.int32)


def _slot_kernel(idx_ref, rank_ref, start_ref, slot_ref):
    tk = idx_ref.shape[1]
    eid = lax.broadcasted_iota(jnp.int32, (N_EXPERTS, tk), 0)
    start = start_ref[:, 0:1]
    rows = []
    for k in range(TOP_K):
        base = jnp.sum(jnp.where(eid == idx_ref[k:k + 1, :], start, 0), axis=0, keepdims=True)
        rows.append(base + rank_ref[k:k + 1, :])
    rows.append(jnp.zeros((8 - TOP_K, tk), jnp.int32))
    slot_ref[...] = jnp.concatenate(rows, axis=0)


def _route_slots(idx):
    t = idx.shape[1]
    tk = ROUTE_TILE
    tri = jnp.asarray(np.triu(np.ones((tk, tk), np.float32)), BF16)
    tok = lambda i: (0, i)
    rank, cnt = pl.pallas_call(
        _rank_kernel, grid=(t // tk,),
        in_specs=[pl.BlockSpec((8, tk), tok), pl.BlockSpec((tk, tk), lambda i: (0, 0))],
        out_specs=[pl.BlockSpec((8, tk), tok), pl.BlockSpec((N_EXPERTS, LANES), lambda i: (0, 0))],
        out_shape=[jax.ShapeDtypeStruct((8, t), jnp.int32), jax.ShapeDtypeStruct((N_EXPERTS, LANES), jnp.int32)],
        scratch_shapes=[pltpu.VMEM((N_EXPERTS, LANES), F32)],
        compiler_params=_cparams(("arbitrary",)),
        name="route_rank",
    )(idx, tri)
    counts = cnt[:, 0]
    padded = (counts + MOE_BM - 1) // MOE_BM * MOE_BM
    pad_end = jnp.cumsum(padded)
    pad_start = pad_end - padded
    nblk = (t * TOP_K) // MOE_BM + N_EXPERTS
    blk_row0 = jnp.arange(nblk, dtype=jnp.int32) * MOE_BM
    blk_expert = jnp.minimum(jnp.sum((pad_end[None, :] <= blk_row0[:, None]).astype(jnp.int32), axis=1),
                             N_EXPERTS - 1)
    nused = (pad_end[-1] // MOE_BM).astype(jnp.int32).reshape(1)
    start_b = jnp.broadcast_to(pad_start.astype(jnp.int32)[:, None], (N_EXPERTS, LANES))
    slot = pl.pallas_call(
        _slot_kernel, grid=(t // tk,),
        in_specs=[pl.BlockSpec((8, tk), tok), pl.BlockSpec((8, tk), tok),
                  pl.BlockSpec((N_EXPERTS, LANES), lambda i: (0, 0))],
        out_specs=pl.BlockSpec((8, tk), tok),
        out_shape=jax.ShapeDtypeStruct((8, t), jnp.int32),
        compiler_params=_cparams(("arbitrary",)),
        name="route_slot",
    )(idx, rank, start_b)
    return slot, blk_expert, nused


SC_CORES = 2
SC_SUBCORES = 16
SC_ROW_BUFFER_BYTES = 256 * 1024
SC_MAX_INDEX_LIST = 128


def _sc_chunks(t, row_bytes):
    chunk = min(SC_MAX_INDEX_LIST, SC_ROW_BUFFER_BYTES // row_bytes)
    nchunk = t // chunk
    per = nchunk // (SC_CORES * SC_SUBCORES)
    assert per * SC_CORES * SC_SUBCORES * chunk == t
    return chunk, nchunk, per


def _sc_slot_lists(slot, nchunk, chunk):
    return slot[:TOP_K].reshape(TOP_K, nchunk, chunk).transpose(1, 0, 2)


def _sc_dispatch(h2, slot, nslots):
    t, w = h2.shape
    chunk, nchunk, per = _sc_chunks(t, w * h2.dtype.itemsize)
    mesh = plsc.VectorSubcoreMesh(core_axis_name="c", subcore_axis_name="s")

    @functools.partial(
        pl.kernel, mesh=mesh, out_type=jax.ShapeDtypeStruct((nslots, w), h2.dtype),
        scratch_types=[pltpu.VMEM((TOP_K, chunk), jnp.int32), pltpu.VMEM((chunk, w), h2.dtype)],
        name="moe_dispatch_sc")
    def body(h_hbm, slot_hbm, xs_hbm, idx_v, rows_v):
        wid = lax.axis_index("s") * SC_CORES + lax.axis_index("c")

        @pl.loop(0, per)
        def _(j):
            c = wid * per + j
            pltpu.sync_copy(slot_hbm.at[c], idx_v)
            pltpu.sync_copy(h_hbm.at[pl.ds(c * chunk, chunk)], rows_v)
            for k in range(TOP_K):
                pltpu.sync_copy(rows_v, xs_hbm.at[idx_v.at[k]])

    return body(h2, _sc_slot_lists(slot, nchunk, chunk))


def _sc_gather(ys, slot, t):
    w = ys.shape[1]
    nbuf = 2
    chunk, nchunk, per = _sc_chunks(t, nbuf * w * ys.dtype.itemsize)
    mesh = plsc.VectorSubcoreMesh(core_axis_name="c", subcore_axis_name="s")

    @functools.partial(
        pl.kernel, mesh=mesh, out_type=jax.ShapeDtypeStruct((TOP_K, t, w), ys.dtype),
        scratch_types=[pltpu.VMEM((TOP_K, chunk), jnp.int32), pltpu.VMEM((nbuf, chunk, w), ys.dtype),
                       pltpu.SemaphoreType.DMA((nbuf,))],
        name="moe_gather_sc")
    def body(ys_hbm, slot_hbm, out_hbm, idx_v, rows_v, wsem):
        wid = lax.axis_index("s") * SC_CORES + lax.axis_index("c")

        @pl.loop(0, per)
        def _(j):
            c = wid * per + j
            pltpu.sync_copy(slot_hbm.at[c], idx_v)
            writes = []
            for k in range(TOP_K):
                b = k % nbuf
                if k >= nbuf:
                    writes[k - nbuf].wait()
                pltpu.sync_copy(ys_hbm.at[idx_v.at[k]], rows_v.at[b])
                writes.append(pltpu.async_copy(rows_v.at[b], out_hbm.at[k, pl.ds(c * chunk, chunk)], wsem.at[b]))
            for wr in writes[-nbuf:]:
                wr.wait()

    return body(ys, _sc_slot_lists(slot, nchunk, chunk))


def _ffn_kernel(be_ref, nu_ref, xs_ref, wgu_ref, bgu_ref, wd_ref, bd_ref, y_ref, wgu_bf, wd_bf):
    i = pl.program_id(0)
    prev = be_ref[jnp.maximum(i - 1, 0)]

    @pl.when(i < nu_ref[0])
    def _():
        @pl.when((i == 0) | (be_ref[i] != prev))
        def _():
            wgu_bf[...] = wgu_ref[...].astype(BF16)
            wd_bf[...] = wd_ref[...].astype(BF16)

        x = _unpack_bf16_rows(xs_ref[...])
        gu = jnp.dot(x, wgu_bf[...], preferred_element_type=F32) + bgu_ref[...]
        gate = jnp.minimum(gu[:, :D_FF], SWIGLU_LIMIT)
        up = jnp.clip(gu[:, D_FF:], -SWIGLU_LIMIT, SWIGLU_LIMIT)
        act = (up + 1.0) * gate * jax.nn.sigmoid(SWIGLU_ALPHA * gate)
        y_ref[...] = jnp.dot(act.astype(BF16), wd_bf[...], preferred_element_type=F32) + bd_ref[...]

    @pl.when(i >= nu_ref[0])
    def _():
        y_ref[...] = jnp.zeros_like(y_ref)


def _expert_ffn(xs, blk_expert, nused, wgu, bgu, wd, bd):
    nslots = xs.shape[0]
    nblk = nslots // MOE_BM
    blk = lambda i, be, nu: (jnp.minimum(i, nu[0] - 1), 0)
    exp3 = lambda i, be, nu: (be[jnp.minimum(i, nu[0] - 1)], 0, 0)
    gs = pltpu.PrefetchScalarGridSpec(
        num_scalar_prefetch=2, grid=(nblk,),
        in_specs=[pl.BlockSpec((MOE_BM, PACK_W), blk),
                  pl.BlockSpec((None, D_MODEL, 2 * D_FF), exp3),
                  pl.BlockSpec((None, 1, 2 * D_FF), exp3),
                  pl.BlockSpec((None, D_FF, D_MODEL), exp3),
                  pl.BlockSpec((None, 1, D_MODEL), exp3)],
        out_specs=pl.BlockSpec((MOE_BM, D_MODEL), lambda i, be, nu: (i, 0)),
        scratch_shapes=[pltpu.VMEM((D_MODEL, 2 * D_FF), BF16), pltpu.VMEM((D_FF, D_MODEL), BF16)])
    return pl.pallas_call(
        _ffn_kernel, grid_spec=gs,
        out_shape=jax.ShapeDtypeStruct((nslots, D_MODEL), F32),
        compiler_params=_cparams(("arbitrary",)),
        name="moe_ffn",
    )(blk_expert, nused, xs, wgu, bgu.reshape(N_EXPERTS, 1, 2 * D_FF), wd, bd.reshape(N_EXPERTS, 1, D_MODEL))


def _combine_kernel(sid_ref, x_ref, mod_ref, gate_ref, yk_ref, *o_refs, tm, out_starts):
    del sid_ref
    gpad = jnp.concatenate([gate_ref[...], jnp.zeros((LANES - 8, tm), F32)], axis=0)
    gcol = gpad.T
    acc = gcol[:, 0:1] * yk_ref[0]
    for k in range(1, TOP_K):
        acc = acc + gcol[:, k:k + 1] * yk_ref[k]
    out = x_ref[...] + mod_ref[5:6, :] * acc
    i = pl.program_id(0)
    ends = out_starts[1:] + (None,)
    for o_ref, s0, s1 in zip(o_refs, out_starts, ends):
        mine = (i >= s0) if s1 is None else ((i >= s0) & (i < s1))

        @pl.when(mine)
        def _():
            o_ref[...] = out


def _combine(x1, mod, tile_sid, gates, yk, out_rows):
    t = x1.shape[0]
    tm = MOVE_TILE
    out_shape = [jax.ShapeDtypeStruct((n, D_MODEL), F32) for n in out_rows]
    out_starts, ntiles = _part_tiles(out_shape, tm)
    assert ntiles * tm == t
    gs = pltpu.PrefetchScalarGridSpec(
        num_scalar_prefetch=1, grid=(t // tm,),
        in_specs=[pl.BlockSpec((tm, D_MODEL), lambda i, sid: (i, 0)),
                  pl.BlockSpec((None, 6, D_MODEL), lambda i, sid: (sid[i], 0, 0)),
                  pl.BlockSpec((8, tm), lambda i, sid: (0, i)),
                  pl.BlockSpec((TOP_K, tm, D_MODEL), lambda i, sid: (0, i, 0))],
        out_specs=_part_specs(out_shape, out_starts, tm, D_MODEL))
    return pl.pallas_call(
        functools.partial(_combine_kernel, tm=tm, out_starts=out_starts), grid_spec=gs,
        out_shape=out_shape,
        compiler_params=_cparams(("arbitrary",)),
        name="moe_combine",
    )(tile_sid, x1, mod, gates, yk)


def _moe(x1, h2, idx, gates, mod, tile_sid_move, wgu, bgu, wd, bd, out_rows):
    t = x1.shape[0]
    nslots = t * TOP_K + N_EXPERTS * MOE_BM
    slot, blk_expert, nused = _route_slots(idx)
    xs = _sc_dispatch(h2, slot, nslots)
    ys = _expert_ffn(xs, blk_expert, nused, wgu, bgu, wd, bd)
    return _combine(x1, mod, tile_sid_move, gates, _sc_gather(ys, slot, t), out_rows)


def _tile_table(seq_lens, tile, fn):
    vals = []
    start = 0
    for sid, n in enumerate(seq_lens):
        assert n % tile == 0
        for j in range(n // tile):
            vals.append(fn(sid, start, n, j))
        start += n
    return jnp.asarray(np.asarray(vals, np.int32))


def _head_block_diag():
    bd = np.kron(np.eye(2 * LANES // HEAD_DIM, dtype=np.float32),
                 np.full((HEAD_DIM, HEAD_DIM), 1.0 / HEAD_DIM, np.float32))
    return jnp.asarray(bd, BF16)


_A_HEAD_ORDER = [kv * A_GROUP + g for g in range(A_GROUP) for kv in range(A_KV_HEADS)]


def _trunk(x_parts, c_pad, seq_lens, p):
    t = sum(a.shape[0] for a in x_parts)
    sid_tok = _tile_table(seq_lens, TOKEN_TILE, lambda sid, s, n, j: sid)
    sid_move = _tile_table(seq_lens, MOVE_TILE, lambda sid, s, n, j: sid)
    bd = _head_block_diag()

    mod0 = _ada_mod(c_pad, p["l0_ada_w"], p["l0_ada_b"]).reshape(c_pad.shape[0], 6, D_MODEL)
    w_in = p["l0_w_in"]
    a_cols = np.concatenate([np.arange(h * HEAD_DIM, (h + 1) * HEAD_DIM) for h in _A_HEAD_ORDER])
    bq, bk, bv = A_IN, A_IN + B_GROUPS * B_GW, A_IN + 2 * B_GROUPS * B_GW
    col_blocks = [w_in[:, a_cols], w_in[:, A_QW:A_IN]]
    for g in range(B_GROUPS):
        col_blocks += [w_in[:, bq + g * B_GW:bq + (g + 1) * B_GW],
                       w_in[:, bk + g * B_GW:bk + (g + 1) * B_GW],
                       w_in[:, bv + g * B_GW:bv + (g + 1) * B_GW]]
    w0 = jnp.concatenate(col_blocks, axis=1).astype(BF16)
    ones = lambda n: jnp.ones((n,), F32)
    gain_blocks = [jnp.tile(p["l0_q_norm_a"], A_Q_HEADS) * QK_SCALE, jnp.tile(p["l0_k_norm_a"], A_KV_HEADS),
                   ones(A_KVW)]
    for g in range(B_GROUPS):
        gain_blocks += [jnp.tile(p["l0_q_norm_b"], B_HEADS) * QK_SCALE, jnp.tile(p["l0_k_norm_b"], B_HEADS),
                        ones(B_GW)]
    gain0 = jnp.concatenate(gain_blocks).reshape(1, -1)
    slabs0 = [(0, A_QW, True, 1), (A_QW, A_KVW, True, 1), (A_QW + A_KVW, A_KVW, False, 1)]
    c0 = A_IN
    for g, (_, dil) in enumerate(B_PAIRS):
        slabs0 += [(c0, B_GW, True, dil), (c0 + B_GW, B_GW, True, dil), (c0 + 2 * B_GW, B_GW, False, dil)]
        c0 += 3 * B_GW
    proj = _in_proj(x_parts, mod0, sid_tok, p["l0_norm1"], w0, gain0, bd, tuple(slabs0))
    qa, ka, va = proj[0:3]

    slopes_a = _alibi_slopes(A_Q_HEADS)
    heads_a = tuple(tuple((slopes_a[kv * A_GROUP + g], kv * A_GROUP + g) for kv in range(A_KV_HEADS))
                    for g in range(A_GROUP))
    (oa,) = _band_attn(qa, ka, va, seq_lens, tq=ATTN_TQ_A, dil=1, qw=A_QW, kvw=A_KVW, half=A_HALF, heads=heads_a,
                       kv_group=(0,) * A_GROUP, sink=p["l0_sink_a"].astype(F32), out_dtype=BF16, want_lse=False)

    slopes_b = _alibi_slopes(B_GROUPS * B_HEADS)
    attn0 = [oa]
    for g, (window, dil) in enumerate(B_PAIRS):
        qg, kg, vg = proj[3 + 3 * g:6 + 3 * g]
        heads_b = tuple(tuple((slopes_b[g * B_HEADS + 2 * pr + s] * dil, 0) for s in range(2))
                        for pr in range(B_HEADS // 2))
        og, lg = _band_attn(qg, kg, vg, [n // dil for n in seq_lens], tq=ATTN_TQ_B, dil=dil, qw=B_GW, kvw=B_GW, half=window // (2 * dil),
                            heads=heads_b, kv_group=tuple(range(B_HEADS // 2)), sink=None, out_dtype=F32,
                            want_lse=True)
        attn0 += [og, lg]

    w_out = p["l0_w_out"]
    wout0 = jnp.concatenate([w_out[a_cols], w_out[A_QW:]], axis=0).astype(BF16)
    br0 = jnp.broadcast_to(p["l0_b_router"].astype(F32)[:, None], (N_EXPERTS, LANES))
    x1, h2, idx, gates = _out_proj(0, x_parts, mod0, sid_tok, p["l0_norm2"], attn0, wout0, p["l0_w_router"].T,
                                   br0)
    x = _moe(x1, h2, idx, gates, mod0, sid_move, p["l0_w_gate_up"], p["l0_b_gate_up"], p["l0_w_down"],
             p["l0_b_down"], [t])

    mod1 = _ada_mod(c_pad, p["l1_ada_w"], p["l1_ada_b"]).reshape(c_pad.shape[0], 6, D_MODEL)
    w1 = p["l1_w_in"].astype(BF16)
    gain1 = jnp.concatenate([jnp.tile(p["l1_q_norm_c"], C_HEADS) * QK_SCALE, jnp.tile(p["l1_k_norm_c"], C_HEADS),
                             ones(C_W)]).reshape(1, -1)
    slabs1 = ((0, C_W, True, 1), (C_W, C_W, True, 1), (2 * C_W, C_W, False, 1))
    qc, kc, vc = _in_proj(x, mod1, sid_tok, p["l1_norm1"], w1, gain1, bd, slabs1)
    na_tile = NA_TROWS * GRID_W
    tile_r0 = _tile_table(seq_lens, na_tile, lambda sid, s, n, j: j * NA_TROWS)
    tile_nr = _tile_table(seq_lens, na_tile, lambda sid, s, n, j: n // GRID_W)
    oc = _na_attn(qc, kc, vc, _na_bias_table(p["l1_rpb_c"]), tile_r0, tile_nr)
    br1 = jnp.broadcast_to(p["l1_b_router"].astype(F32)[:, None], (N_EXPERTS, LANES))
    x1, h2, idx, gates = _out_proj(1, x, mod1, sid_tok, p["l1_norm2"], [oc], p["l1_w_out"].astype(BF16),
                                   p["l1_w_router"].T, br1)
    return _moe(x1, h2, idx, gates, mod1, sid_move, p["l1_w_gate_up"], p["l1_b_gate_up"], p["l1_w_down"],
                p["l1_b_down"], [a.shape[0] for a in x_parts])


def kernel(x_prompt, x_sample, c_prompt, c_sample, l0_ada_w, l0_ada_b, l0_norm1, l0_w_in, l0_q_norm_a, l0_k_norm_a, l0_sink_a, l0_q_norm_b, l0_k_norm_b, l0_w_out, l0_norm2, l0_w_router, l0_b_router, l0_w_gate_up, l0_b_gate_up, l0_w_down, l0_b_down, l1_ada_w, l1_ada_b, l1_norm1, l1_w_in, l1_q_norm_c, l1_k_norm_c, l1_rpb_c, l1_w_out, l1_norm2, l1_w_router, l1_b_router, l1_w_gate_up, l1_b_gate_up, l1_w_down, l1_b_down):
    p = dict(l0_ada_w=l0_ada_w, l0_ada_b=l0_ada_b, l0_norm1=l0_norm1, l0_w_in=l0_w_in, l0_q_norm_a=l0_q_norm_a,
             l0_k_norm_a=l0_k_norm_a, l0_sink_a=l0_sink_a, l0_q_norm_b=l0_q_norm_b, l0_k_norm_b=l0_k_norm_b,
             l0_w_out=l0_w_out, l0_norm2=l0_norm2, l0_w_router=l0_w_router, l0_b_router=l0_b_router,
             l0_w_gate_up=l0_w_gate_up, l0_b_gate_up=l0_b_gate_up, l0_w_down=l0_w_down, l0_b_down=l0_b_down,
             l1_ada_w=l1_ada_w, l1_ada_b=l1_ada_b, l1_norm1=l1_norm1, l1_w_in=l1_w_in, l1_q_norm_c=l1_q_norm_c,
             l1_k_norm_c=l1_k_norm_c, l1_rpb_c=l1_rpb_c, l1_w_out=l1_w_out, l1_norm2=l1_norm2,
             l1_w_router=l1_w_router, l1_b_router=l1_b_router, l1_w_gate_up=l1_w_gate_up,
             l1_b_gate_up=l1_b_gate_up, l1_w_down=l1_w_down, l1_b_down=l1_b_down)
    nb_p, len_p, d = x_prompt.shape
    nb_s, len_s, _ = x_sample.shape
    seq_lens = (len_p,) * nb_p + (len_s,) * nb_s
    x_parts = [x_prompt.reshape(nb_p * len_p, d), x_sample.reshape(nb_s * len_s, d)]
    nseq = nb_p + nb_s
    c_pad = jnp.concatenate([c_prompt, c_sample, jnp.zeros((-nseq % 8, d), F32)], axis=0)
    y_p, y_s = _trunk(x_parts, c_pad, seq_lens, p)
    return (y_p.reshape(nb_p, len_p, d), y_s.reshape(nb_s, len_s, d))
```

```python
import functools
import math

import numpy as np
import jax
import jax.numpy as jnp
from jax import lax
from jax.experimental import pallas as pl
from jax.experimental.pallas import tpu as pltpu
from jax.experimental.pallas import tpu_sc as plsc

F32 = jnp.float32
BF16 = jnp.bfloat16
HIGHEST = lax.Precision.HIGHEST

D_MODEL = 1024
HEAD_DIM = 64
LANES = 128
GRID_W = 64
A_Q_HEADS = 8
A_KV_HEADS = 2
A_GROUP = A_Q_HEADS // A_KV_HEADS
A_HALF = 128
B_PAIRS = ((128, 1), (512, 4), (2048, 16))
B_GROUPS = 3
B_HEADS = 4
A_QW = A_Q_HEADS * HEAD_DIM
A_KVW = A_KV_HEADS * HEAD_DIM
A_IN = A_QW + 2 * A_KVW
B_GW = B_HEADS * HEAD_DIM
C_HEADS = 16
C_W = C_HEADS * HEAD_DIM
NA_KH = 8
NA_KW = 16
N_EXPERTS = 32
TOP_K = 4
D_FF = 1024
SWIGLU_LIMIT = 7.0
SWIGLU_ALPHA = 1.702
RMS_EPS = 1e-6
NEG_INF = -1e30
QK_SCALE = HEAD_DIM ** -0.5

TOKEN_TILE = 512
ATTN_TQ_A = 256
ATTN_TQ_B = 512
ATTN_UQ = 128
NA_TROWS = 8
MOE_BM = 512
ROUTE_TILE = 512
MOVE_TILE = 256
VMEM_LIMIT = 56 * 1024 * 1024


def _alibi_slopes(n):
    return [float(2.0 ** (-8.0 * (j + 1) / n)) for j in range(n)]


def _cparams(sem, flags=None):
    return pltpu.CompilerParams(dimension_semantics=sem, vmem_limit_bytes=VMEM_LIMIT, flags=flags)


def _mod_kernel(c_ref, w_ref, b_ref, o_ref):
    c = c_ref[...]
    s = c * jax.nn.sigmoid(c)
    o_ref[...] = jnp.dot(s, w_ref[...], precision=HIGHEST, preferred_element_type=F32) + b_ref[...]


def _ada_mod(c_pad, w, b):
    nrow = c_pad.shape[0]
    ncol = w.shape[1]
    tn = D_MODEL
    return pl.pallas_call(
        _mod_kernel,
        grid=(ncol // tn,),
        in_specs=[pl.BlockSpec((nrow, D_MODEL), lambda j: (0, 0)),
                  pl.BlockSpec((D_MODEL, tn), lambda j: (0, j)),
                  pl.BlockSpec((1, tn), lambda j: (0, j))],
        out_specs=pl.BlockSpec((nrow, tn), lambda j: (0, j)),
        out_shape=jax.ShapeDtypeStruct((nrow, ncol), F32),
        compiler_params=_cparams(("arbitrary",)),
        name="ada_mod",
    )(c_pad, w, b.reshape(1, ncol))


def _head_mean_sq(y, bd_ref):
    w = y.shape[1]
    outs = []
    for c0 in range(0, w, 2 * LANES):
        cw = min(2 * LANES, w - c0)
        sq = y[:, c0:c0 + cw] * y[:, c0:c0 + cw]
        outs.append(jnp.dot(sq.astype(BF16), bd_ref[0:cw, 0:cw], preferred_element_type=F32))
    return outs[0] if len(outs) == 1 else jnp.concatenate(outs, axis=1)


def _rms_mod(x, g_ref, mod_ref, shift_row, scale_row):
    ms = jnp.mean(x * x, axis=-1, keepdims=True)
    xn = x * lax.rsqrt(ms + RMS_EPS) * g_ref[...]
    return xn * (1.0 + mod_ref[scale_row:scale_row + 1, :]) + mod_ref[shift_row:shift_row + 1, :]


def _part_tiles(parts, tile):
    starts, s = [], 0
    for a in parts:
        assert a.shape[0] % tile == 0
        starts.append(s)
        s += a.shape[0] // tile
    return tuple(starts), s


def _part_specs(parts, starts, tile, ncol):
    def spec(a, s0):
        n = a.shape[0] // tile
        return pl.BlockSpec((tile, ncol), lambda i, *_: (jnp.clip(i - s0, 0, n - 1), 0))
    return [spec(a, s0) for a, s0 in zip(parts, starts)]


def _read_part(refs, starts):
    i = pl.program_id(0)
    x = refs[0][...]
    for ref, s0 in zip(refs[1:], starts[1:]):
        x = jnp.where(i >= s0, ref[...], x)
    return x


def _in_kernel(sid_ref, *refs, slabs, tm, part_starts):
    del sid_ref
    npart = len(part_starts)
    x_refs = refs[:npart]
    mod_ref, n1_ref, w_ref, gain_ref, bd_ref = refs[npart:npart + 5]
    rest = refs[npart + 5:]
    nout = len(slabs)
    out_refs = rest[:nout]
    scr_ref = rest[nout] if len(rest) > nout else None
    h = _rms_mod(_read_part(x_refs, part_starts), n1_ref, mod_ref, 0, 1).astype(BF16)
    for (c0, w, normed, dil), o_ref in zip(slabs, out_refs):
        y = jnp.dot(h, w_ref[:, c0:c0 + w], preferred_element_type=F32)
        if normed:
            y = y * lax.rsqrt(_head_mean_sq(y, bd_ref) + RMS_EPS) * gain_ref[:, c0:c0 + w]
        if dil == 1:
            o_ref[...] = y.astype(BF16)
        else:
            for c in range(w // LANES):
                scr_ref[c] = y[:, c * LANES:(c + 1) * LANES]
            for r in range(dil):
                for c in range(w // LANES):
                    o_ref[:, r * w + c * LANES:r * w + (c + 1) * LANES] = (
                        scr_ref[c, pl.ds(r, tm // dil, stride=dil), :].astype(BF16))


def _in_proj(x_parts, mod, tile_sid, n1, w_bf, gain, bd, slabs):
    tm = TOKEN_TILE
    part_starts, ntiles = _part_tiles(x_parts, tm)
    t = ntiles * tm
    ncols = w_bf.shape[1]
    out_shapes, out_specs = [], []
    for (c0, w, normed, dil) in slabs:
        out_shapes.append(jax.ShapeDtypeStruct((t // dil, dil * w), BF16))
        out_specs.append(pl.BlockSpec((tm // dil, dil * w), lambda i, sid: (i, 0)))
    need_scr = any(s[3] > 1 for s in slabs)
    scratch = [pltpu.VMEM((max(s[1] for s in slabs if s[3] > 1) // LANES, tm, LANES), F32)] if need_scr else []
    gs = pltpu.PrefetchScalarGridSpec(
        num_scalar_prefetch=1,
        grid=(t // tm,),
        in_specs=_part_specs(x_parts, part_starts, tm, D_MODEL) + [
            pl.BlockSpec((None, 6, D_MODEL), lambda i, sid: (sid[i], 0, 0)),
            pl.BlockSpec((1, D_MODEL), lambda i, sid: (0, 0)),
            pl.BlockSpec((D_MODEL, ncols), lambda i, sid: (0, 0)),
            pl.BlockSpec((1, ncols), lambda i, sid: (0, 0)),
            pl.BlockSpec((2 * LANES, 2 * LANES), lambda i, sid: (0, 0))],
        out_specs=out_specs,
        scratch_shapes=scratch)
    return pl.pallas_call(
        functools.partial(_in_kernel, slabs=slabs, tm=tm, part_starts=part_starts),
        grid_spec=gs,
        out_shape=out_shapes,
        compiler_params=_cparams(("arbitrary",)),
        name="in_proj",
    )(tile_sid, *x_parts, mod, n1.reshape(1, D_MODEL), w_bf, gain, bd)


def _band_kernel(s0_ref, s1_ref, *refs, tq, hb, half, heads, kv_group, has_sink, want_lse):
    pos = 0
    sink_ref = None
    if has_sink:
        sink_ref = refs[0]
        pos = 1
    q_ref, kp_ref, kc_ref, kn_ref, vp_ref, vc_ref, vn_ref = refs[pos:pos + 7]
    o_ref = refs[pos + 7]
    lse_ref = refs[pos + 8] if want_lse else None
    i = pl.program_id(1)
    uq = ATTN_UQ
    wlen = uq + 2 * hb
    kall = jnp.concatenate([kp_ref[...], kc_ref[...], kn_ref[...]], axis=0)
    vall = jnp.concatenate([vp_ref[...], vc_ref[...], vn_ref[...]], axis=0)
    lane = lax.broadcasted_iota(jnp.int32, (uq, LANES), 1)
    lo_lanes = lane < HEAD_DIM
    for sub in range(tq // uq):
        rows = slice(sub * uq, (sub + 1) * uq)
        kwin = kall[sub * uq:sub * uq + wlen]
        vwin = vall[sub * uq:sub * uq + wlen]
        q0 = i * tq + sub * uq
        kpos = q0 - hb + lax.broadcasted_iota(jnp.int32, (wlen, uq), 0)
        qpos = q0 + lax.broadcasted_iota(jnp.int32, (wlen, uq), 1)
        rel = kpos - qpos
        valid = (jnp.abs(rel) <= half) & (kpos >= s0_ref[i]) & (kpos < s1_ref[i])
        distm = jnp.where(valid, jnp.abs(rel).astype(F32), 1e32)
        lse_rows = []
        for g, pair in enumerate(heads):
            qg = q_ref[rows, g * LANES:(g + 1) * LANES]
            kg = kwin[:, kv_group[g] * LANES:(kv_group[g] + 1) * LANES]
            vg = vwin[:, kv_group[g] * LANES:(kv_group[g] + 1) * LANES]
            outs = []
            for s, (slope, sink_idx) in enumerate(pair):
                keep = lo_lanes if s == 0 else jnp.logical_not(lo_lanes)
                qm = jnp.where(keep, qg, jnp.zeros_like(qg))
                sc = lax.dot_general(kg, qm, (((1,), (1,)), ((), ())), preferred_element_type=F32)
                sc = sc - slope * distm
                m = jnp.max(sc, axis=0, keepdims=True)
                if has_sink:
                    m = jnp.maximum(m, sink_ref[sink_idx])
                e = jnp.exp(sc - m)
                den = jnp.sum(e, axis=0, keepdims=True)
                if has_sink:
                    den = den + jnp.exp(sink_ref[sink_idx] - m)
                p = (e * (1.0 / den)).astype(BF16)
                outs.append(lax.dot_general(p, vg, (((0,), (0,)), ((), ())), preferred_element_type=F32))
                if want_lse:
                    lse_rows.append(m + jnp.log(den))
            o_ref[rows, g * LANES:(g + 1) * LANES] = jnp.where(lo_lanes, outs[0], outs[1]).astype(o_ref.dtype)
        if want_lse:
            lse_t = jnp.concatenate(lse_rows + [jnp.zeros((LANES - len(lse_rows), uq), F32)], axis=0)
            lse_ref[rows, :] = lse_t.T


def _band_attn(q, k, v, seq_rows, *, tq, dil, qw, kvw, half, heads, kv_group, sink, out_dtype, want_lse):
    rows = q.shape[0]
    hb = half
    s0 = _tile_table(seq_rows, tq, lambda sid, s, n, j: s)
    s1 = _tile_table(seq_rows, tq, lambda sid, s, n, j: s + n)
    per = tq // hb
    nt = rows // tq
    nhb = rows // hb
    has_sink = sink is not None
    in_specs = []
    args = []
    if has_sink:
        in_specs.append(pl.BlockSpec(memory_space=pltpu.SMEM))
        args.append(sink)
    qmap = lambda r, i, a, b: (i, r)
    pmap = lambda r, i, a, b: (jnp.maximum(i * per - 1, 0), r)
    nmap = lambda r, i, a, b: (jnp.minimum((i + 1) * per, nhb - 1), r)
    in_specs += [pl.BlockSpec((tq, qw), qmap),
                 pl.BlockSpec((hb, kvw), pmap), pl.BlockSpec((tq, kvw), qmap), pl.BlockSpec((hb, kvw), nmap),
                 pl.BlockSpec((hb, kvw), pmap), pl.BlockSpec((tq, kvw), qmap), pl.BlockSpec((hb, kvw), nmap)]
    args += [q, k, k, k, v, v, v]
    out_shape = [jax.ShapeDtypeStruct((rows, dil * qw), out_dtype)]
    out_specs = [pl.BlockSpec((tq, qw), qmap)]
    if want_lse:
        out_shape.append(jax.ShapeDtypeStruct((rows, dil * LANES), F32))
        out_specs.append(pl.BlockSpec((tq, LANES), qmap))
    gs = pltpu.PrefetchScalarGridSpec(num_scalar_prefetch=2, grid=(dil, nt),
                                      in_specs=in_specs, out_specs=out_specs)
    res = pl.pallas_call(
        functools.partial(_band_kernel, tq=tq, hb=hb, half=half, heads=heads, kv_group=kv_group,
                          has_sink=has_sink, want_lse=want_lse),
        grid_spec=gs,
        out_shape=out_shape,
        compiler_params=_cparams(("arbitrary", "arbitrary")),
        name="band_attn_d%d" % dil,
    )(s0, s1, *args)
    return res


def _na_kernel(r0_ref, nr_ref, q_ref, kp_ref, kc_ref, kn_ref, vp_ref, vc_ref, vn_ref, bias_ref,
               o_ref, kcat, vcat, *, halo):
    j = pl.program_id(0)
    tq = NA_TROWS * GRID_W
    hrows = halo * GRID_W
    kcat[0:hrows, :] = kp_ref[...]
    kcat[hrows:hrows + tq, :] = kc_ref[...]
    kcat[hrows + tq:, :] = kn_ref[...]
    vcat[0:hrows, :] = vp_ref[...]
    vcat[hrows:hrows + tq, :] = vc_ref[...]
    vcat[hrows + tq:, :] = vn_ref[...]
    r0 = r0_ref[j]
    nrows = nr_ref[j]
    kwin_len = NA_KH * GRID_W
    lane = lax.broadcasted_iota(jnp.int32, (GRID_W, LANES), 1)
    lo_lanes = lane < HEAD_DIM
    nrel = 2 * NA_KH - 1

    def row_body(u, carry):
        r = r0 + u
        rs = jnp.clip(r - NA_KH // 2, 0, nrows - NA_KH)
        off = pl.multiple_of((rs - r0 + halo) * GRID_W, GRID_W)
        bvar = rs - r + NA_KH - 1
        qrow = pl.multiple_of(u * GRID_W, GRID_W)
        for g in range(C_HEADS // 2):
            qg = q_ref[pl.ds(qrow, GRID_W), g * LANES:(g + 1) * LANES]
            qs = jnp.concatenate([jnp.where(lo_lanes, qg, jnp.zeros_like(qg)),
                                  jnp.where(lo_lanes, jnp.zeros_like(qg), qg)], axis=0)
            kg = kcat[pl.ds(off, kwin_len), g * LANES:(g + 1) * LANES]
            vg = vcat[pl.ds(off, kwin_len), g * LANES:(g + 1) * LANES]
            sc = lax.dot_general(kg, qs, (((1,), (1,)), ((), ())), preferred_element_type=F32)
            sc = sc + jnp.concatenate([bias_ref[g * nrel + bvar + kr] for kr in range(NA_KH)], axis=0)
            m = jnp.max(sc, axis=0, keepdims=True)
            e = jnp.exp(sc - m)
            den = jnp.sum(e, axis=0, keepdims=True)
            p = (e * (1.0 / den)).astype(BF16)
            pv = lax.dot_general(p, vg, (((0,), (0,)), ((), ())), preferred_element_type=F32)
            o = jnp.where(lo_lanes, pv[0:GRID_W], pv[GRID_W:])
            o_ref[pl.ds(qrow, GRID_W), g * LANES:(g + 1) * LANES] = o.astype(o_ref.dtype)
        return carry

    lax.fori_loop(0, NA_TROWS, row_body, 0)


def _na_attn(q, k, v, bias_tab, tile_r0, tile_nr):
    t = q.shape[0]
    halo = NA_KH // 2
    tq = NA_TROWS * GRID_W
    hrows = halo * GRID_W
    per = tq // hrows
    nhb = t // hrows
    nt = t // tq
    qmap = lambda j, a, b: (j, 0)
    pmap = lambda j, a, b: (jnp.maximum(j * per - 1, 0), 0)
    nmap = lambda j, a, b: (jnp.minimum((j + 1) * per, nhb - 1), 0)
    gs = pltpu.PrefetchScalarGridSpec(
        num_scalar_prefetch=2, grid=(nt,),
        in_specs=[pl.BlockSpec((tq, C_W), qmap),
                  pl.BlockSpec((hrows, C_W), pmap), pl.BlockSpec((tq, C_W), qmap), pl.BlockSpec((hrows, C_W), nmap),
                  pl.BlockSpec((hrows, C_W), pmap), pl.BlockSpec((tq, C_W), qmap), pl.BlockSpec((hrows, C_W), nmap),
                  pl.BlockSpec(bias_tab.shape, lambda j, a, b: (0, 0, 0))],
        out_specs=pl.BlockSpec((tq, C_W), qmap),
        scratch_shapes=[pltpu.VMEM((tq + 2 * hrows, C_W), BF16), pltpu.VMEM((tq + 2 * hrows, C_W), BF16)])
    return pl.pallas_call(
        functools.partial(_na_kernel, halo=halo),
        grid_spec=gs,
        out_shape=jax.ShapeDtypeStruct((t, C_W), BF16),
        compiler_params=_cparams(("arbitrary",)),
        name="na_attn",
    )(tile_r0, tile_nr, q, k, k, k, v, v, v, bias_tab)


def _na_bias_table(rpb):
    c = np.arange(GRID_W)
    cs = np.clip(c - NA_KW // 2, 0, GRID_W - NA_KW)
    kc = np.arange(GRID_W)
    valid = (kc[:, None] >= cs[None, :]) & (kc[:, None] < cs[None, :] + NA_KW)
    cidx = np.clip(kc[:, None] - c[None, :] + NA_KW - 1, 0, 2 * NA_KW - 2)
    tab = rpb.astype(F32)[:, :, cidx]
    tab = jnp.where(jnp.asarray(valid)[None, None], tab, NEG_INF)
    pairs = jnp.concatenate([tab[0::2], tab[1::2]], axis=-1)
    return pairs.reshape((C_HEADS // 2) * (2 * NA_KH - 1), GRID_W, LANES)


def _route(h2, wr_ref, br_ref, idx_ref, gate_ref):
    tm = h2.shape[0]
    logits = lax.dot_general(wr_ref[...], h2, (((1,), (1,)), ((), ())),
                             precision=HIGHEST, preferred_element_type=F32) + br_ref[:, 0:1]
    eid = lax.broadcasted_iota(jnp.int32, (N_EXPERTS, tm), 0)
    vals, idxs = [], []
    for _ in range(TOP_K):
        m = jnp.max(logits, axis=0, keepdims=True)
        ix = jnp.min(jnp.where(logits == m, eid, N_EXPERTS), axis=0, keepdims=True)
        vals.append(m)
        idxs.append(ix)
        logits = jnp.where(eid == ix, -jnp.inf, logits)
    es = [jnp.exp(vk - vals[0]) for vk in vals]
    den = es[0] + es[1] + es[2] + es[3]
    pad_i = jnp.zeros((8 - TOP_K, tm), jnp.int32)
    pad_f = jnp.zeros((8 - TOP_K, tm), F32)
    idx_ref[...] = jnp.concatenate(idxs + [pad_i], axis=0)
    gate_ref[...] = jnp.concatenate([ek / den for ek in es] + [pad_f], axis=0)


PACK_W = D_MODEL // 2


def _pack_bf16_rows(h):
    bits = pltpu.bitcast(h.astype(BF16).astype(F32), jnp.int32)
    return (bits[:, :PACK_W] & jnp.int32(-65536)) | lax.shift_right_logical(bits[:, PACK_W:], 16)


def _unpack_bf16_rows(w):
    hi = pltpu.bitcast(w & jnp.int32(-65536), F32).astype(BF16)
    lo = pltpu.bitcast(lax.shift_left(w, 16), F32).astype(BF16)
    return jnp.concatenate([hi, lo], axis=1)


def _out0_kernel(sid_ref, *refs, tm, part_starts):
    del sid_ref
    npart = len(part_starts)
    x_refs = refs[:npart]
    (mod_ref, n2_ref, oa_ref, ob0_ref, ls0_ref, ob1_ref, ls1_ref, ob2_ref, ls2_ref,
     wout_ref, wr_ref, br_ref, x1_ref, h2_ref, idx_ref, gate_ref, oscr, lscr) = refs[npart:]
    os_, ls_ = [], []
    for gi, (o_ref, l_ref) in enumerate(((ob0_ref, ls0_ref), (ob1_ref, ls1_ref), (ob2_ref, ls2_ref))):
        dil = B_PAIRS[gi][1]
        if dil == 1:
            os_.append(o_ref[...])
            ls_.append(l_ref[...])
        else:
            ncg = B_GW // LANES
            for r in range(dil):
                for c in range(ncg):
                    lo = r * B_GW + c * LANES
                    oscr[gi * ncg + c, pl.ds(r, tm // dil, stride=dil), :] = o_ref[:, lo:lo + LANES]
                lscr[gi, pl.ds(r, tm // dil, stride=dil), :] = l_ref[:, r * LANES:(r + 1) * LANES]
            os_.append(jnp.concatenate([oscr[gi * ncg + c] for c in range(ncg)], axis=1))
            ls_.append(lscr[gi])
    lmax = jnp.maximum(jnp.maximum(ls_[0], ls_[1]), ls_[2])
    ws = [jnp.exp(l - lmax) for l in ls_]
    winv = 1.0 / (ws[0] + ws[1] + ws[2])
    lane = lax.broadcasted_iota(jnp.int32, (tm, LANES), 1)
    lo_lanes = lane < HEAD_DIM

    def head_weights(w):
        return jnp.concatenate(
            [jnp.where(lo_lanes, jnp.broadcast_to(w[:, 2 * pr:2 * pr + 1], (tm, LANES)),
                       jnp.broadcast_to(w[:, 2 * pr + 1:2 * pr + 2], (tm, LANES)))
             for pr in range(B_HEADS // 2)], axis=1)

    ob = (head_weights(ws[0] * winv) * os_[0] + head_weights(ws[1] * winv) * os_[1]
          + head_weights(ws[2] * winv) * os_[2])
    o = (jnp.dot(oa_ref[...], wout_ref[0:A_QW, :], preferred_element_type=F32)
         + jnp.dot(ob.astype(BF16), wout_ref[A_QW:, :], preferred_element_type=F32))
    x1 = _read_part(x_refs, part_starts) + mod_ref[2:3, :] * o
    x1_ref[...] = x1
    h2 = _rms_mod(x1, n2_ref, mod_ref, 3, 4)
    h2_ref[...] = _pack_bf16_rows(h2)
    _route(h2, wr_ref, br_ref, idx_ref, gate_ref)


def _out1_kernel(sid_ref, *refs, part_starts):
    del sid_ref
    npart = len(part_starts)
    x_refs = refs[:npart]
    mod_ref, n2_ref, oc_ref, wout_ref, wr_ref, br_ref, x1_ref, h2_ref, idx_ref, gate_ref = refs[npart:]
    o = jnp.dot(oc_ref[...], wout_ref[...], preferred_element_type=F32)
    x1 = _read_part(x_refs, part_starts) + mod_ref[2:3, :] * o
    x1_ref[...] = x1
    h2 = _rms_mod(x1, n2_ref, mod_ref, 3, 4)
    h2_ref[...] = _pack_bf16_rows(h2)
    _route(h2, wr_ref, br_ref, idx_ref, gate_ref)


def _out_proj(layer, x_parts, mod, tile_sid, n2, attn, wout_bf, wr_t, br):
    tm = TOKEN_TILE
    part_starts, ntiles = _part_tiles(x_parts, tm)
    t = ntiles * tm
    row = lambda i, sid: (i, 0)
    const = lambda i, sid: (0, 0)
    in_specs = _part_specs(x_parts, part_starts, tm, D_MODEL) + [
        pl.BlockSpec((None, 6, D_MODEL), lambda i, sid: (sid[i], 0, 0)),
        pl.BlockSpec((1, D_MODEL), const)]
    scratch = []
    if layer == 0:
        in_specs.append(pl.BlockSpec((tm, A_QW), row))
        for (_, dil) in B_PAIRS:
            in_specs += [pl.BlockSpec((tm // dil, dil * B_GW), row), pl.BlockSpec((tm // dil, dil * LANES), row)]
        body = functools.partial(_out0_kernel, tm=tm, part_starts=part_starts)
        nscr = B_GROUPS * B_GW // LANES
        scratch = [pltpu.VMEM((nscr, tm, LANES), F32), pltpu.VMEM((B_GROUPS, tm, LANES), F32)]
    else:
        in_specs.append(pl.BlockSpec((tm, C_W), row))
        body = functools.partial(_out1_kernel, part_starts=part_starts)
    in_specs += [pl.BlockSpec(wout_bf.shape, const),
                 pl.BlockSpec((N_EXPERTS, D_MODEL), const),
                 pl.BlockSpec((N_EXPERTS, LANES), const)]
    gs = pltpu.PrefetchScalarGridSpec(
        num_scalar_prefetch=1, grid=(t // tm,), in_specs=in_specs,
        out_specs=[pl.BlockSpec((tm, D_MODEL), row), pl.BlockSpec((tm, PACK_W), row),
                   pl.BlockSpec((8, tm), lambda i, sid: (0, i)), pl.BlockSpec((8, tm), lambda i, sid: (0, i))],
        scratch_shapes=scratch)
    return pl.pallas_call(
        body, grid_spec=gs,
        out_shape=[jax.ShapeDtypeStruct((t, D_MODEL), F32), jax.ShapeDtypeStruct((t, PACK_W), jnp.int32),
                   jax.ShapeDtypeStruct((8, t), jnp.int32), jax.ShapeDtypeStruct((8, t), F32)],
        compiler_params=_cparams(("arbitrary",)),
        name="out_proj%d" % layer,
    )(tile_sid, *x_parts, mod, n2.reshape(1, D_MODEL), *attn, wout_bf, wr_t, br)


def _rank_kernel(idx_ref, tri_ref, rank_ref, cnt_ref, carry):
    i = pl.program_id(0)

    @pl.when(i == 0)
    def _():
        carry[...] = jnp.zeros_like(carry)

    tk = idx_ref.shape[1]
    eid = lax.broadcasted_iota(jnp.int32, (N_EXPERTS, tk), 0)
    hits = [eid == idx_ref[k:k + 1, :] for k in range(TOP_K)]
    onehot = sum(h.astype(F32) for h in hits)
    incl = jnp.dot(onehot.astype(BF16), tri_ref[...], preferred_element_type=F32)
    before = incl - onehot + carry[:, 0:1]
    rows = [jnp.sum(jnp.where(h, before, 0.0), axis=0, keepdims=True) for h in hits]
    rows.append(jnp.zeros((8 - TOP_K, tk), F32))
    rank_ref[...] = jnp.concatenate(rows, axis=0).astype(jnp.int32)
    carry[...] = carry[...] + incl[:, tk - 1:tk]
    cnt_ref[...] = carry[...].astype(jnp.int32)


def _slot_kernel(idx_ref, rank_ref, start_ref, slot_ref):
    tk = idx_ref.shape[1]
    eid = lax.broadcasted_iota(jnp.int32, (N_EXPERTS, tk), 0)
    start = start_ref[:, 0:1]
    rows = []
    for k in range(TOP_K):
        base = jnp.sum(jnp.where(eid == idx_ref[k:k + 1, :], start, 0), axis=0, keepdims=True)
        rows.append(base + rank_ref[k:k + 1, :])
    rows.append(jnp.zeros((8 - TOP_K, tk), jnp.int32))
    slot_ref[...] = jnp.concatenate(rows, axis=0)


def _route_slots(idx):
    t = idx.shape[1]
    tk = ROUTE_TILE
    tri = jnp.asarray(np.triu(np.ones((tk, tk), np.float32)), BF16)
    tok = lambda i: (0, i)
    rank, cnt = pl.pallas_call(
        _rank_kernel, grid=(t // tk,),
        in_specs=[pl.BlockSpec((8, tk), tok), pl.BlockSpec((tk, tk), lambda i: (0, 0))],
        out_specs=[pl.BlockSpec((8, tk), tok), pl.BlockSpec((N_EXPERTS, LANES), lambda i: (0, 0))],
        out_shape=[jax.ShapeDtypeStruct((8, t), jnp.int32), jax.ShapeDtypeStruct((N_EXPERTS, LANES), jnp.int32)],
        scratch_shapes=[pltpu.VMEM((N_EXPERTS, LANES), F32)],
        compiler_params=_cparams(("arbitrary",)),
        name="route_rank",
    )(idx, tri)
    counts = cnt[:, 0]
    padded = (counts + MOE_BM - 1) // MOE_BM * MOE_BM
    pad_end = jnp.cumsum(padded)
    pad_start = pad_end - padded
    nblk = (t * TOP_K) // MOE_BM + N_EXPERTS
    blk_row0 = jnp.arange(nblk, dtype=jnp.int32) * MOE_BM
    blk_expert = jnp.minimum(jnp.sum((pad_end[None, :] <= blk_row0[:, None]).astype(jnp.int32), axis=1),
                             N_EXPERTS - 1)
    nused = (pad_end[-1] // MOE_BM).astype(jnp.int32).reshape(1)
    start_b = jnp.broadcast_to(pad_start.astype(jnp.int32)[:, None], (N_EXPERTS, LANES))
    slot = pl.pallas_call(
        _slot_kernel, grid=(t // tk,),
        in_specs=[pl.BlockSpec((8, tk), tok), pl.BlockSpec((8, tk), tok),
                  pl.BlockSpec((N_EXPERTS, LANES), lambda i: (0, 0))],
        out_specs=pl.BlockSpec((8, tk), tok),
        out_shape=jax.ShapeDtypeStruct((8, t), jnp.int32),
        compiler_params=_cparams(("arbitrary",)),
        name="route_slot",
    )(idx, rank, start_b)
    return slot, blk_expert, nused


SC_CORES = 2
SC_SUBCORES = 16
SC_ROW_BUFFER_BYTES = 256 * 1024
SC_MAX_INDEX_LIST = 128


def _sc_chunks(t, row_bytes):
    chunk = min(SC_MAX_INDEX_LIST, SC_ROW_BUFFER_BYTES // row_bytes)
    nchunk = t // chunk
    per = nchunk // (SC_CORES * SC_SUBCORES)
    assert per * SC_CORES * SC_SUBCORES * chunk == t
    return chunk, nchunk, per


def _sc_slot_lists(slot, nchunk, chunk):
    return slot[:TOP_K].reshape(TOP_K, nchunk, chunk).transpose(1, 0, 2)


def _sc_dispatch(h2, slot, nslots):
    t, w = h2.shape
    chunk, nchunk, per = _sc_chunks(t, w * h2.dtype.itemsize)
    mesh = plsc.VectorSubcoreMesh(core_axis_name="c", subcore_axis_name="s")

    @functools.partial(
        pl.kernel, mesh=mesh, out_type=jax.ShapeDtypeStruct((nslots, w), h2.dtype),
        scratch_types=[pltpu.VMEM((TOP_K, chunk), jnp.int32), pltpu.VMEM((chunk, w), h2.dtype)],
        name="moe_dispatch_sc")
    def body(h_hbm, slot_hbm, xs_hbm, idx_v, rows_v):
        wid = lax.axis_index("s") * SC_CORES + lax.axis_index("c")

        @pl.loop(0, per)
        def _(j):
            c = wid * per + j
            pltpu.sync_copy(slot_hbm.at[c], idx_v)
            pltpu.sync_copy(h_hbm.at[pl.ds(c * chunk, chunk)], rows_v)
            for k in range(TOP_K):
                pltpu.sync_copy(rows_v, xs_hbm.at[idx_v.at[k]])

    return body(h2, _sc_slot_lists(slot, nchunk, chunk))


def _sc_gather(ys, slot, t):
    w = ys.shape[1]
    nbuf = 2
    chunk, nchunk, per = _sc_chunks(t, nbuf * w * ys.dtype.itemsize)
    mesh = plsc.VectorSubcoreMesh(core_axis_name="c", subcore_axis_name="s")

    @functools.partial(
        pl.kernel, mesh=mesh, out_type=jax.ShapeDtypeStruct((TOP_K, t, w), ys.dtype),
        scratch_types=[pltpu.VMEM((TOP_K, chunk), jnp.int32), pltpu.VMEM((nbuf, chunk, w), ys.dtype),
                       pltpu.SemaphoreType.DMA((nbuf,))],
        name="moe_gather_sc")
    def body(ys_hbm, slot_hbm, out_hbm, idx_v, rows_v, wsem):
        wid = lax.axis_index("s") * SC_CORES + lax.axis_index("c")

        @pl.loop(0, per)
        def _(j):
            c = wid * per + j
            pltpu.sync_copy(slot_hbm.at[c], idx_v)
            writes = []
            for k in range(TOP_K):
                b = k % nbuf
                if k >= nbuf:
                    writes[k - nbuf].wait()
                pltpu.sync_copy(ys_hbm.at[idx_v.at[k]], rows_v.at[b])
                writes.append(pltpu.async_copy(rows_v.at[b], out_hbm.at[k, pl.ds(c * chunk, chunk)], wsem.at[b]))
            for wr in writes[-nbuf:]:
                wr.wait()

    return body(ys, _sc_slot_lists(slot, nchunk, chunk))


def _ffn_kernel(be_ref, nu_ref, xs_ref, wgu_ref, bgu_ref, wd_ref, bd_ref, y_ref, wgu_bf, wd_bf):
    i = pl.program_id(0)
    prev = be_ref[jnp.maximum(i - 1, 0)]

    @pl.when(i < nu_ref[0])
    def _():
        @pl.when((i == 0) | (be_ref[i] != prev))
        def _():
            wgu_bf[...] = wgu_ref[...].astype(BF16)
            wd_bf[...] = wd_ref[...].astype(BF16)

        x = _unpack_bf16_rows(xs_ref[...])
        gu = jnp.dot(x, wgu_bf[...], preferred_element_type=F32) + bgu_ref[...]
        gate = jnp.minimum(gu[:, :D_FF], SWIGLU_LIMIT)
        up = jnp.clip(gu[:, D_FF:], -SWIGLU_LIMIT, SWIGLU_LIMIT)
        act = (up + 1.0) * gate * jax.nn.sigmoid(SWIGLU_ALPHA * gate)
        y_ref[...] = jnp.dot(act.astype(BF16), wd_bf[...], preferred_element_type=F32) + bd_ref[...]

    @pl.when(i >= nu_ref[0])
    def _():
        y_ref[...] = jnp.zeros_like(y_ref)


def _expert_ffn(xs, blk_expert, nused, wgu, bgu, wd, bd):
    nslots = xs.shape[0]
    nblk = nslots // MOE_BM
    blk = lambda i, be, nu: (jnp.minimum(i, nu[0] - 1), 0)
    exp3 = lambda i, be, nu: (be[jnp.minimum(i, nu[0] - 1)], 0, 0)
    gs = pltpu.PrefetchScalarGridSpec(
        num_scalar_prefetch=2, grid=(nblk,),
        in_specs=[pl.BlockSpec((MOE_BM, PACK_W), blk),
                  pl.BlockSpec((None, D_MODEL, 2 * D_FF), exp3),
                  pl.BlockSpec((None, 1, 2 * D_FF), exp3),
                  pl.BlockSpec((None, D_FF, D_MODEL), exp3),
                  pl.BlockSpec((None, 1, D_MODEL), exp3)],
        out_specs=pl.BlockSpec((MOE_BM, D_MODEL), lambda i, be, nu: (i, 0)),
        scratch_shapes=[pltpu.VMEM((D_MODEL, 2 * D_FF), BF16), pltpu.VMEM((D_FF, D_MODEL), BF16)])
    return pl.pallas_call(
        _ffn_kernel, grid_spec=gs,
        out_shape=jax.ShapeDtypeStruct((nslots, D_MODEL), F32),
        compiler_params=_cparams(("arbitrary",)),
        name="moe_ffn",
    )(blk_expert, nused, xs, wgu, bgu.reshape(N_EXPERTS, 1, 2 * D_FF), wd, bd.reshape(N_EXPERTS, 1, D_MODEL))


def _combine_kernel(sid_ref, x_ref, mod_ref, gate_ref, yk_ref, *o_refs, tm, out_starts):
    del sid_ref
    gpad = jnp.concatenate([gate_ref[...], jnp.zeros((LANES - 8, tm), F32)], axis=0)
    gcol = gpad.T
    acc = gcol[:, 0:1] * yk_ref[0]
    for k in range(1, TOP_K):
        acc = acc + gcol[:, k:k + 1] * yk_ref[k]
    out = x_ref[...] + mod_ref[5:6, :] * acc
    i = pl.program_id(0)
    ends = out_starts[1:] + (None,)
    for o_ref, s0, s1 in zip(o_refs, out_starts, ends):
        mine = (i >= s0) if s1 is None else ((i >= s0) & (i < s1))

        @pl.when(mine)
        def _():
            o_ref[...] = out


def _combine(x1, mod, tile_sid, gates, yk, out_rows):
    t = x1.shape[0]
    tm = MOVE_TILE
    out_shape = [jax.ShapeDtypeStruct((n, D_MODEL), F32) for n in out_rows]
    out_starts, ntiles = _part_tiles(out_shape, tm)
    assert ntiles * tm == t
    gs = pltpu.PrefetchScalarGridSpec(
        num_scalar_prefetch=1, grid=(t // tm,),
        in_specs=[pl.BlockSpec((tm, D_MODEL), lambda i, sid: (i, 0)),
                  pl.BlockSpec((None, 6, D_MODEL), lambda i, sid: (sid[i], 0, 0)),
                  pl.BlockSpec((8, tm), lambda i, sid: (0, i)),
                  pl.BlockSpec((TOP_K, tm, D_MODEL), lambda i, sid: (0, i, 0))],
        out_specs=_part_specs(out_shape, out_starts, tm, D_MODEL))
    return pl.pallas_call(
        functools.partial(_combine_kernel, tm=tm, out_starts=out_starts), grid_spec=gs,
        out_shape=out_shape,
        compiler_params=_cparams(("arbitrary",)),
        name="moe_combine",
    )(tile_sid, x1, mod, gates, yk)


def _moe(x1, h2, idx, gates, mod, tile_sid_move, wgu, bgu, wd, bd, out_rows):
    t = x1.shape[0]
    nslots = t * TOP_K + N_EXPERTS * MOE_BM
    slot, blk_expert, nused = _route_slots(idx)
    xs = _sc_dispatch(h2, slot, nslots)
    ys = _expert_ffn(xs, blk_expert, nused, wgu, bgu, wd, bd)
    return _combine(x1, mod, tile_sid_move, gates, _sc_gather(ys, slot, t), out_rows)


def _tile_table(seq_lens, tile, fn):
    vals = []
    start = 0
    for sid, n in enumerate(seq_lens):
        assert n % tile == 0
        for j in range(n // tile):
            vals.append(fn(sid, start, n, j))
        start += n
    return jnp.asarray(np.asarray(vals, np.int32))


def _head_block_diag():
    bd = np.kron(np.eye(2 * LANES // HEAD_DIM, dtype=np.float32),
                 np.full((HEAD_DIM, HEAD_DIM), 1.0 / HEAD_DIM, np.float32))
    return jnp.asarray(bd, BF16)


_A_HEAD_ORDER = [kv * A_GROUP + g for g in range(A_GROUP) for kv in range(A_KV_HEADS)]


def _trunk(x_parts, c_pad, seq_lens, p):
    t = sum(a.shape[0] for a in x_parts)
    sid_tok = _tile_table(seq_lens, TOKEN_TILE, lambda sid, s, n, j: sid)
    sid_move = _tile_table(seq_lens, MOVE_TILE, lambda sid, s, n, j: sid)
    bd = _head_block_diag()

    mod0 = _ada_mod(c_pad, p["l0_ada_w"], p["l0_ada_b"]).reshape(c_pad.shape[0], 6, D_MODEL)
    w_in = p["l0_w_in"]
    a_cols = np.concatenate([np.arange(h * HEAD_DIM, (h + 1) * HEAD_DIM) for h in _A_HEAD_ORDER])
    bq, bk, bv = A_IN, A_IN + B_GROUPS * B_GW, A_IN + 2 * B_GROUPS * B_GW
    col_blocks = [w_in[:, a_cols], w_in[:, A_QW:A_IN]]
    for g in range(B_GROUPS):
        col_blocks += [w_in[:, bq + g * B_GW:bq + (g + 1) * B_GW],
                       w_in[:, bk + g * B_GW:bk + (g + 1) * B_GW],
                       w_in[:, bv + g * B_GW:bv + (g + 1) * B_GW]]
    w0 = jnp.concatenate(col_blocks, axis=1).astype(BF16)
    ones = lambda n: jnp.ones((n,), F32)
    gain_blocks = [jnp.tile(p["l0_q_norm_a"], A_Q_HEADS) * QK_SCALE, jnp.tile(p["l0_k_norm_a"], A_KV_HEADS),
                   ones(A_KVW)]
    for g in range(B_GROUPS):
        gain_blocks += [jnp.tile(p["l0_q_norm_b"], B_HEADS) * QK_SCALE, jnp.tile(p["l0_k_norm_b"], B_HEADS),
                        ones(B_GW)]
    gain0 = jnp.concatenate(gain_blocks).reshape(1, -1)
    slabs0 = [(0, A_QW, True, 1), (A_QW, A_KVW, True, 1), (A_QW + A_KVW, A_KVW, False, 1)]
    c0 = A_IN
    for g, (_, dil) in enumerate(B_PAIRS):
        slabs0 += [(c0, B_GW, True, dil), (c0 + B_GW, B_GW, True, dil), (c0 + 2 * B_GW, B_GW, False, dil)]
        c0 += 3 * B_GW
    proj = _in_proj(x_parts, mod0, sid_tok, p["l0_norm1"], w0, gain0, bd, tuple(slabs0))
    qa, ka, va = proj[0:3]

    slopes_a = _alibi_slopes(A_Q_HEADS)
    heads_a = tuple(tuple((slopes_a[kv * A_GROUP + g], kv * A_GROUP + g) for kv in range(A_KV_HEADS))
                    for g in range(A_GROUP))
    (oa,) = _band_attn(qa, ka, va, seq_lens, tq=ATTN_TQ_A, dil=1, qw=A_QW, kvw=A_KVW, half=A_HALF, heads=heads_a,
                       kv_group=(0,) * A_GROUP, sink=p["l0_sink_a"].astype(F32), out_dtype=BF16, want_lse=False)

    slopes_b = _alibi_slopes(B_GROUPS * B_HEADS)
    attn0 = [oa]
    for g, (window, dil) in enumerate(B_PAIRS):
        qg, kg, vg = proj[3 + 3 * g:6 + 3 * g]
        heads_b = tuple(tuple((slopes_b[g * B_HEADS + 2 * pr + s] * dil, 0) for s in range(2))
                        for pr in range(B_HEADS // 2))
        og, lg = _band_attn(qg, kg, vg, [n // dil for n in seq_lens], tq=ATTN_TQ_B, dil=dil, qw=B_GW, kvw=B_GW, half=window // (2 * dil),
                            heads=heads_b, kv_group=tuple(range(B_HEADS // 2)), sink=None, out_dtype=F32,
                            want_lse=True)
        attn0 += [og, lg]

    w_out = p["l0_w_out"]
    wout0 = jnp.concatenate([w_out[a_cols], w_out[A_QW:]], axis=0).astype(BF16)
    br0 = jnp.broadcast_to(p["l0_b_router"].astype(F32)[:, None], (N_EXPERTS, LANES))
    x1, h2, idx, gates = _out_proj(0, x_parts, mod0, sid_tok, p["l0_norm2"], attn0, wout0, p["l0_w_router"].T,
                                   br0)
    x = _moe(x1, h2, idx, gates, mod0, sid_move, p["l0_w_gate_up"], p["l0_b_gate_up"], p["l0_w_down"],
             p["l0_b_down"], [t])

    mod1 = _ada_mod(c_pad, p["l1_ada_w"], p["l1_ada_b"]).reshape(c_pad.shape[0], 6, D_MODEL)
    w1 = p["l1_w_in"].astype(BF16)
    gain1 = jnp.concatenate([jnp.tile(p["l1_q_norm_c"], C_HEADS) * QK_SCALE, jnp.tile(p["l1_k_norm_c"], C_HEADS),
                             ones(C_W)]).reshape(1, -1)
    slabs1 = ((0, C_W, True, 1), (C_W, C_W, True, 1), (2 * C_W, C_W, False, 1))
    qc, kc, vc = _in_proj(x, mod1, sid_tok, p["l1_norm1"], w1, gain1, bd, slabs1)
    na_tile = NA_TROWS * GRID_W
    tile_r0 = _tile_table(seq_lens, na_tile, lambda sid, s, n, j: j * NA_TROWS)
    tile_nr = _tile_table(seq_lens, na_tile, lambda sid, s, n, j: n // GRID_W)
    oc = _na_attn(qc, kc, vc, _na_bias_table(p["l1_rpb_c"]), tile_r0, tile_nr)
    br1 = jnp.broadcast_to(p["l1_b_router"].astype(F32)[:, None], (N_EXPERTS, LANES))
    x1, h2, idx, gates = _out_proj(1, x, mod1, sid_tok, p["l1_norm2"], [oc], p["l1_w_out"].astype(BF16),
                                   p["l1_w_router"].T, br1)
    return _moe(x1, h2, idx, gates, mod1, sid_move, p["l1_w_gate_up"], p["l1_b_gate_up"], p["l1_w_down"],
                p["l1_b_down"], [a.shape[0] for a in x_parts])


def kernel(x_prompt, x_sample, c_prompt, c_sample, l0_ada_w, l0_ada_b, l0_norm1, l0_w_in, l0_q_norm_a, l0_k_norm_a, l0_sink_a, l0_q_norm_b, l0_k_norm_b, l0_w_out, l0_norm2, l0_w_router, l0_b_router, l0_w_gate_up, l0_b_gate_up, l0_w_down, l0_b_down, l1_ada_w, l1_ada_b, l1_norm1, l1_w_in, l1_q_norm_c, l1_k_norm_c, l1_rpb_c, l1_w_out, l1_norm2, l1_w_router, l1_b_router, l1_w_gate_up, l1_b_gate_up, l1_w_down, l1_b_down):
    p = dict(l0_ada_w=l0_ada_w, l0_ada_b=l0_ada_b, l0_norm1=l0_norm1, l0_w_in=l0_w_in, l0_q_norm_a=l0_q_norm_a,
             l0_k_norm_a=l0_k_norm_a, l0_sink_a=l0_sink_a, l0_q_norm_b=l0_q_norm_b, l0_k_norm_b=l0_k_norm_b,
             l0_w_out=l0_w_out, l0_norm2=l0_norm2, l0_w_router=l0_w_router, l0_b_router=l0_b_router,
             l0_w_gate_up=l0_w_gate_up, l0_b_gate_up=l0_b_gate_up, l0_w_down=l0_w_down, l0_b_down=l0_b_down,
             l1_ada_w=l1_ada_w, l1_ada_b=l1_ada_b, l1_norm1=l1_norm1, l1_w_in=l1_w_in, l1_q_norm_c=l1_q_norm_c,
             l1_k_norm_c=l1_k_norm_c, l1_rpb_c=l1_rpb_c, l1_w_out=l1_w_out, l1_norm2=l1_norm2,
             l1_w_router=l1_w_router, l1_b_router=l1_b_router, l1_w_gate_up=l1_w_gate_up,
             l1_b_gate_up=l1_b_gate_up, l1_w_down=l1_w_down, l1_b_down=l1_b_down)
    nb_p, len_p, d = x_prompt.shape
    nb_s, len_s, _ = x_sample.shape
    seq_lens = (len_p,) * nb_p + (len_s,) * nb_s
    x_parts = [x_prompt.reshape(nb_p * len_p, d), x_sample.reshape(nb_s * len_s, d)]
    nseq = nb_p + nb_s
    c_pad = jnp.concatenate([c_prompt, c_sample, jnp.zeros((-nseq % 8, d), F32)], axis=0)
    y_p, y_s = _trunk(x_parts, c_pad, seq_lens, p)
    return (y_p.reshape(nb_p, len_p, d), y_s.reshape(nb_s, len_s, d))
```

```python
import functools
import math

import numpy as np
import jax
import jax.numpy as jnp
from jax import lax
from jax.experimental import pallas as pl
from jax.experimental.pallas import tpu as pltpu
from jax.experimental.pallas import tpu_sc as plsc

F32 = jnp.float32
BF16 = jnp.bfloat16
HIGHEST = lax.Precision.HIGHEST

D_MODEL = 1024
HEAD_DIM = 64
LANES = 128
GRID_W = 64
A_Q_HEADS = 8
A_KV_HEADS = 2
A_GROUP = A_Q_HEADS // A_KV_HEADS
A_HALF = 128
B_PAIRS = ((128, 1), (512, 4), (2048, 16))
B_GROUPS = 3
B_HEADS = 4
A_QW = A_Q_HEADS * HEAD_DIM
A_KVW = A_KV_HEADS * HEAD_DIM
A_IN = A_QW + 2 * A_KVW
B_GW = B_HEADS * HEAD_DIM
C_HEADS = 16
C_W = C_HEADS * HEAD_DIM
NA_KH = 8
NA_KW = 16
N_EXPERTS = 32
TOP_K = 4
D_FF = 1024
SWIGLU_LIMIT = 7.0
SWIGLU_ALPHA = 1.702
RMS_EPS = 1e-6
NEG_INF = -1e30
QK_SCALE = HEAD_DIM ** -0.5

TOKEN_TILE = 1024
ATTN_TQ_A = 256
ATTN_TQ_B = 512
ATTN_UQ = 128
NA_TROWS = 8
MOE_BM = 512
ROUTE_TILE = 512
MOVE_TILE = 256
VMEM_LIMIT = 56 * 1024 * 1024


def _alibi_slopes(n):
    return [float(2.0 ** (-8.0 * (j + 1) / n)) for j in range(n)]


def _cparams(sem, flags=None):
    return pltpu.CompilerParams(dimension_semantics=sem, vmem_limit_bytes=VMEM_LIMIT, flags=flags)


def _mod_kernel(c_ref, w_ref, b_ref, o_ref):
    c = c_ref[...]
    s = c * jax.nn.sigmoid(c)
    o_ref[...] = jnp.dot(s, w_ref[...], precision=HIGHEST, preferred_element_type=F32) + b_ref[...]


def _ada_mod(c_pad, w, b):
    nrow = c_pad.shape[0]
    ncol = w.shape[1]
    tn = D_MODEL
    return pl.pallas_call(
        _mod_kernel,
        grid=(ncol // tn,),
        in_specs=[pl.BlockSpec((nrow, D_MODEL), lambda j: (0, 0)),
                  pl.BlockSpec((D_MODEL, tn), lambda j: (0, j)),
                  pl.BlockSpec((1, tn), lambda j: (0, j))],
        out_specs=pl.BlockSpec((nrow, tn), lambda j: (0, j)),
        out_shape=jax.ShapeDtypeStruct((nrow, ncol), F32),
        compiler_params=_cparams(("arbitrary",)),
        name="ada_mod",
    )(c_pad, w, b.reshape(1, ncol))


def _head_mean_sq(y, bd_ref):
    w = y.shape[1]
    outs = []
    for c0 in range(0, w, 2 * LANES):
        cw = min(2 * LANES, w - c0)
        sq = y[:, c0:c0 + cw] * y[:, c0:c0 + cw]
        outs.append(jnp.dot(sq.astype(BF16), bd_ref[0:cw, 0:cw], preferred_element_type=F32))
    return outs[0] if len(outs) == 1 else jnp.concatenate(outs, axis=1)


def _rms_mod(x, g_ref, mod_ref, shift_row, scale_row):
    ms = jnp.mean(x * x, axis=-1, keepdims=True)
    xn = x * lax.rsqrt(ms + RMS_EPS) * g_ref[...]
    return xn * (1.0 + mod_ref[scale_row:scale_row + 1, :]) + mod_ref[shift_row:shift_row + 1, :]


def _part_tiles(parts, tile):
    starts, s = [], 0
    for a in parts:
        assert a.shape[0] % tile == 0
        starts.append(s)
        s += a.shape[0] // tile
    return tuple(starts), s


def _part_specs(parts, starts, tile, ncol):
    def spec(a, s0):
        n = a.shape[0] // tile
        return pl.BlockSpec((tile, ncol), lambda i, *_: (jnp.clip(i - s0, 0, n - 1), 0))
    return [spec(a, s0) for a, s0 in zip(parts, starts)]


def _read_part(refs, starts):
    i = pl.program_id(0)
    x = refs[0][...]
    for ref, s0 in zip(refs[1:], starts[1:]):
        x = jnp.where(i >= s0, ref[...], x)
    return x


def _in_kernel(sid_ref, *refs, slabs, tm, part_starts):
    del sid_ref
    npart = len(part_starts)
    x_refs = refs[:npart]
    mod_ref, n1_ref, w_ref, gain_ref, bd_ref = refs[npart:npart + 5]
    rest = refs[npart + 5:]
    nout = len(slabs)
    out_refs = rest[:nout]
    scr_ref = rest[nout] if len(rest) > nout else None
    h = _rms_mod(_read_part(x_refs, part_starts), n1_ref, mod_ref, 0, 1).astype(BF16)
    for (c0, w, normed, dil), o_ref in zip(slabs, out_refs):
        y = jnp.dot(h, w_ref[:, c0:c0 + w], preferred_element_type=F32)
        if normed:
            y = y * lax.rsqrt(_head_mean_sq(y, bd_ref) + RMS_EPS) * gain_ref[:, c0:c0 + w]
        if dil == 1:
            o_ref[...] = y.astype(BF16)
        else:
            for c in range(w // LANES):
                scr_ref[c] = y[:, c * LANES:(c + 1) * LANES]
            for r in range(dil):
                for c in range(w // LANES):
                    o_ref[:, r * w + c * LANES:r * w + (c + 1) * LANES] = (
                        scr_ref[c, pl.ds(r, tm // dil, stride=dil), :].astype(BF16))


def _in_proj(x_parts, mod, tile_sid, n1, w_bf, gain, bd, slabs):
    tm = TOKEN_TILE
    part_starts, ntiles = _part_tiles(x_parts, tm)
    t = ntiles * tm
    ncols = w_bf.shape[1]
    out_shapes, out_specs = [], []
    for (c0, w, normed, dil) in slabs:
        out_shapes.append(jax.ShapeDtypeStruct((t // dil, dil * w), BF16))
        out_specs.append(pl.BlockSpec((tm // dil, dil * w), lambda i, sid: (i, 0)))
    need_scr = any(s[3] > 1 for s in slabs)
    scratch = [pltpu.VMEM((max(s[1] for s in slabs if s[3] > 1) // LANES, tm, LANES), F32)] if need_scr else []
    gs = pltpu.PrefetchScalarGridSpec(
        num_scalar_prefetch=1,
        grid=(t // tm,),
        in_specs=_part_specs(x_parts, part_starts, tm, D_MODEL) + [
            pl.BlockSpec((None, 6, D_MODEL), lambda i, sid: (sid[i], 0, 0)),
            pl.BlockSpec((1, D_MODEL), lambda i, sid: (0, 0)),
            pl.BlockSpec((D_MODEL, ncols), lambda i, sid: (0, 0)),
            pl.BlockSpec((1, ncols), lambda i, sid: (0, 0)),
            pl.BlockSpec((2 * LANES, 2 * LANES), lambda i, sid: (0, 0))],
        out_specs=out_specs,
        scratch_shapes=scratch)
    return pl.pallas_call(
        functools.partial(_in_kernel, slabs=slabs, tm=tm, part_starts=part_starts),
        grid_spec=gs,
        out_shape=out_shapes,
        compiler_params=_cparams(("arbitrary",)),
        name="in_proj",
    )(tile_sid, *x_parts, mod, n1.reshape(1, D_MODEL), w_bf, gain, bd)


def _band_kernel(s0_ref, s1_ref, *refs, tq, hb, half, heads, kv_group, has_sink, want_lse):
    pos = 0
    sink_ref = None
    if has_sink:
        sink_ref = refs[0]
        pos = 1
    q_ref, kp_ref, kc_ref, kn_ref, vp_ref, vc_ref, vn_ref = refs[pos:pos + 7]
    o_ref = refs[pos + 7]
    lse_ref = refs[pos + 8] if want_lse else None
    i = pl.program_id(1)
    uq = ATTN_UQ
    wlen = uq + 2 * hb
    kall = jnp.concatenate([kp_ref[...], kc_ref[...], kn_ref[...]], axis=0)
    vall = jnp.concatenate([vp_ref[...], vc_ref[...], vn_ref[...]], axis=0)
    lane = lax.broadcasted_iota(jnp.int32, (uq, LANES), 1)
    lo_lanes = lane < HEAD_DIM
    for sub in range(tq // uq):
        rows = slice(sub * uq, (sub + 1) * uq)
        kwin = kall[sub * uq:sub * uq + wlen]
        vwin = vall[sub * uq:sub * uq + wlen]
        q0 = i * tq + sub * uq
        kpos = q0 - hb + lax.broadcasted_iota(jnp.int32, (wlen, uq), 0)
        qpos = q0 + lax.broadcasted_iota(jnp.int32, (wlen, uq), 1)
        rel = kpos - qpos
        valid = (jnp.abs(rel) <= half) & (kpos >= s0_ref[i]) & (kpos < s1_ref[i])
        distm = jnp.where(valid, jnp.abs(rel).astype(F32), 1e32)
        lse_rows = []
        for g, pair in enumerate(heads):
            qg = q_ref[rows, g * LANES:(g + 1) * LANES]
            kg = kwin[:, kv_group[g] * LANES:(kv_group[g] + 1) * LANES]
            vg = vwin[:, kv_group[g] * LANES:(kv_group[g] + 1) * LANES]
            outs = []
            for s, (slope, sink_idx) in enumerate(pair):
                keep = lo_lanes if s == 0 else jnp.logical_not(lo_lanes)
                qm = jnp.where(keep, qg, jnp.zeros_like(qg))
                sc = lax.dot_general(kg, qm, (((1,), (1,)), ((), ())), preferred_element_type=F32)
                sc = sc - slope * distm
                m = jnp.max(sc, axis=0, keepdims=True)
                if has_sink:
                    m = jnp.maximum(m, sink_ref[sink_idx])
                e = jnp.exp(sc - m)
                den = jnp.sum(e, axis=0, keepdims=True)
                if has_sink:
                    den = den + jnp.exp(sink_ref[sink_idx] - m)
                p = (e * (1.0 / den)).astype(BF16)
                outs.append(lax.dot_general(p, vg, (((0,), (0,)), ((), ())), preferred_element_type=F32))
                if want_lse:
                    lse_rows.append(m + jnp.log(den))
            o_ref[rows, g * LANES:(g + 1) * LANES] = jnp.where(lo_lanes, outs[0], outs[1]).astype(o_ref.dtype)
        if want_lse:
            lse_t = jnp.concatenate(lse_rows + [jnp.zeros((LANES - len(lse_rows), uq), F32)], axis=0)
            lse_ref[rows, :] = lse_t.T


def _band_attn(q, k, v, seq_rows, *, tq, dil, qw, kvw, half, heads, kv_group, sink, out_dtype, want_lse):
    rows = q.shape[0]
    hb = half
    s0 = _tile_table(seq_rows, tq, lambda sid, s, n, j: s)
    s1 = _tile_table(seq_rows, tq, lambda sid, s, n, j: s + n)
    per = tq // hb
    nt = rows // tq
    nhb = rows // hb
    has_sink = sink is not None
    in_specs = []
    args = []
    if has_sink:
        in_specs.append(pl.BlockSpec(memory_space=pltpu.SMEM))
        args.append(sink)
    qmap = lambda r, i, a, b: (i, r)
    pmap = lambda r, i, a, b: (jnp.maximum(i * per - 1, 0), r)
    nmap = lambda r, i, a, b: (jnp.minimum((i + 1) * per, nhb - 1), r)
    in_specs += [pl.BlockSpec((tq, qw), qmap),
                 pl.BlockSpec((hb, kvw), pmap), pl.BlockSpec((tq, kvw), qmap), pl.BlockSpec((hb, kvw), nmap),
                 pl.BlockSpec((hb, kvw), pmap), pl.BlockSpec((tq, kvw), qmap), pl.BlockSpec((hb, kvw), nmap)]
    args += [q, k, k, k, v, v, v]
    out_shape = [jax.ShapeDtypeStruct((rows, dil * qw), out_dtype)]
    out_specs = [pl.BlockSpec((tq, qw), qmap)]
    if want_lse:
        out_shape.append(jax.ShapeDtypeStruct((rows, dil * LANES), F32))
        out_specs.append(pl.BlockSpec((tq, LANES), qmap))
    gs = pltpu.PrefetchScalarGridSpec(num_scalar_prefetch=2, grid=(dil, nt),
                                      in_specs=in_specs, out_specs=out_specs)
    res = pl.pallas_call(
        functools.partial(_band_kernel, tq=tq, hb=hb, half=half, heads=heads, kv_group=kv_group,
                          has_sink=has_sink, want_lse=want_lse),
        grid_spec=gs,
        out_shape=out_shape,
        compiler_params=_cparams(("arbitrary", "arbitrary")),
        name="band_attn_d%d" % dil,
    )(s0, s1, *args)
    return res


def _na_kernel(r0_ref, nr_ref, q_ref, kp_ref, kc_ref, kn_ref, vp_ref, vc_ref, vn_ref, bias_ref,
               o_ref, kcat, vcat, *, halo):
    j = pl.program_id(0)
    tq = NA_TROWS * GRID_W
    hrows = halo * GRID_W
    kcat[0:hrows, :] = kp_ref[...]
    kcat[hrows:hrows + tq, :] = kc_ref[...]
    kcat[hrows + tq:, :] = kn_ref[...]
    vcat[0:hrows, :] = vp_ref[...]
    vcat[hrows:hrows + tq, :] = vc_ref[...]
    vcat[hrows + tq:, :] = vn_ref[...]
    r0 = r0_ref[j]
    nrows = nr_ref[j]
    kwin_len = NA_KH * GRID_W
    lane = lax.broadcasted_iota(jnp.int32, (GRID_W, LANES), 1)
    lo_lanes = lane < HEAD_DIM
    nrel = 2 * NA_KH - 1

    def row_body(u, carry):
        r = r0 + u
        rs = jnp.clip(r - NA_KH // 2, 0, nrows - NA_KH)
        off = pl.multiple_of((rs - r0 + halo) * GRID_W, GRID_W)
        bvar = rs - r + NA_KH - 1
        qrow = pl.multiple_of(u * GRID_W, GRID_W)
        for g in range(C_HEADS // 2):
            qg = q_ref[pl.ds(qrow, GRID_W), g * LANES:(g + 1) * LANES]
            qs = jnp.concatenate([jnp.where(lo_lanes, qg, jnp.zeros_like(qg)),
                                  jnp.where(lo_lanes, jnp.zeros_like(qg), qg)], axis=0)
            kg = kcat[pl.ds(off, kwin_len), g * LANES:(g + 1) * LANES]
            vg = vcat[pl.ds(off, kwin_len), g * LANES:(g + 1) * LANES]
            sc = lax.dot_general(kg, qs, (((1,), (1,)), ((), ())), preferred_element_type=F32)
            sc = sc + jnp.concatenate([bias_ref[g * nrel + bvar + kr] for kr in range(NA_KH)], axis=0)
            m = jnp.max(sc, axis=0, keepdims=True)
            e = jnp.exp(sc - m)
            den = jnp.sum(e, axis=0, keepdims=True)
            p = (e * (1.0 / den)).astype(BF16)
            pv = lax.dot_general(p, vg, (((0,), (0,)), ((), ())), preferred_element_type=F32)
            o = jnp.where(lo_lanes, pv[0:GRID_W], pv[GRID_W:])
            o_ref[pl.ds(qrow, GRID_W), g * LANES:(g + 1) * LANES] = o.astype(o_ref.dtype)
        return carry

    lax.fori_loop(0, NA_TROWS, row_body, 0)


def _na_attn(q, k, v, bias_tab, tile_r0, tile_nr):
    t = q.shape[0]
    halo = NA_KH // 2
    tq = NA_TROWS * GRID_W
    hrows = halo * GRID_W
    per = tq // hrows
    nhb = t // hrows
    nt = t // tq
    qmap = lambda j, a, b: (j, 0)
    pmap = lambda j, a, b: (jnp.maximum(j * per - 1, 0), 0)
    nmap = lambda j, a, b: (jnp.minimum((j + 1) * per, nhb - 1), 0)
    gs = pltpu.PrefetchScalarGridSpec(
        num_scalar_prefetch=2, grid=(nt,),
        in_specs=[pl.BlockSpec((tq, C_W), qmap),
                  pl.BlockSpec((hrows, C_W), pmap), pl.BlockSpec((tq, C_W), qmap), pl.BlockSpec((hrows, C_W), nmap),
                  pl.BlockSpec((hrows, C_W), pmap), pl.BlockSpec((tq, C_W), qmap), pl.BlockSpec((hrows, C_W), nmap),
                  pl.BlockSpec(bias_tab.shape, lambda j, a, b: (0, 0, 0))],
        out_specs=pl.BlockSpec((tq, C_W), qmap),
        scratch_shapes=[pltpu.VMEM((tq + 2 * hrows, C_W), BF16), pltpu.VMEM((tq + 2 * hrows, C_W), BF16)])
    return pl.pallas_call(
        functools.partial(_na_kernel, halo=halo),
        grid_spec=gs,
        out_shape=jax.ShapeDtypeStruct((t, C_W), BF16),
        compiler_params=_cparams(("arbitrary",)),
        name="na_attn",
    )(tile_r0, tile_nr, q, k, k, k, v, v, v, bias_tab)


def _na_bias_table(rpb):
    c = np.arange(GRID_W)
    cs = np.clip(c - NA_KW // 2, 0, GRID_W - NA_KW)
    kc = np.arange(GRID_W)
    valid = (kc[:, None] >= cs[None, :]) & (kc[:, None] < cs[None, :] + NA_KW)
    cidx = np.clip(kc[:, None] - c[None, :] + NA_KW - 1, 0, 2 * NA_KW - 2)
    tab = rpb.astype(F32)[:, :, cidx]
    tab = jnp.where(jnp.asarray(valid)[None, None], tab, NEG_INF)
    pairs = jnp.concatenate([tab[0::2], tab[1::2]], axis=-1)
    return pairs.reshape((C_HEADS // 2) * (2 * NA_KH - 1), GRID_W, LANES)


def _route(h2, wr_ref, br_ref, idx_ref, gate_ref):
    tm = h2.shape[0]
    logits = lax.dot_general(wr_ref[...], h2, (((1,), (1,)), ((), ())),
                             precision=HIGHEST, preferred_element_type=F32) + br_ref[:, 0:1]
    eid = lax.broadcasted_iota(jnp.int32, (N_EXPERTS, tm), 0)
    vals, idxs = [], []
    for _ in range(TOP_K):
        m = jnp.max(logits, axis=0, keepdims=True)
        ix = jnp.min(jnp.where(logits == m, eid, N_EXPERTS), axis=0, keepdims=True)
        vals.append(m)
        idxs.append(ix)
        logits = jnp.where(eid == ix, -jnp.inf, logits)
    es = [jnp.exp(vk - vals[0]) for vk in vals]
    den = es[0] + es[1] + es[2] + es[3]
    pad_i = jnp.zeros((8 - TOP_K, tm), jnp.int32)
    pad_f = jnp.zeros((8 - TOP_K, tm), F32)
    idx_ref[...] = jnp.concatenate(idxs + [pad_i], axis=0)
    gate_ref[...] = jnp.concatenate([ek / den for ek in es] + [pad_f], axis=0)


PACK_W = D_MODEL // 2


def _pack_bf16_rows(h):
    bits = pltpu.bitcast(h.astype(BF16).astype(F32), jnp.int32)
    return (bits[:, :PACK_W] & jnp.int32(-65536)) | lax.shift_right_logical(bits[:, PACK_W:], 16)


def _unpack_bf16_rows(w):
    hi = pltpu.bitcast(w & jnp.int32(-65536), F32).astype(BF16)
    lo = pltpu.bitcast(lax.shift_left(w, 16), F32).astype(BF16)
    return jnp.concatenate([hi, lo], axis=1)


def _out0_kernel(sid_ref, *refs, tm, part_starts):
    del sid_ref
    npart = len(part_starts)
    x_refs = refs[:npart]
    (mod_ref, n2_ref, oa_ref, ob0_ref, ls0_ref, ob1_ref, ls1_ref, ob2_ref, ls2_ref,
     wout_ref, wr_ref, br_ref, x1_ref, h2_ref, idx_ref, gate_ref, oscr, lscr) = refs[npart:]
    os_, ls_ = [], []
    for gi, (o_ref, l_ref) in enumerate(((ob0_ref, ls0_ref), (ob1_ref, ls1_ref), (ob2_ref, ls2_ref))):
        dil = B_PAIRS[gi][1]
        if dil == 1:
            os_.append(o_ref[...])
            ls_.append(l_ref[...])
        else:
            ncg = B_GW // LANES
            for r in range(dil):
                for c in range(ncg):
                    lo = r * B_GW + c * LANES
                    oscr[gi * ncg + c, pl.ds(r, tm // dil, stride=dil), :] = o_ref[:, lo:lo + LANES]
                lscr[gi, pl.ds(r, tm // dil, stride=dil), :] = l_ref[:, r * LANES:(r + 1) * LANES]
            os_.append(jnp.concatenate([oscr[gi * ncg + c] for c in range(ncg)], axis=1))
            ls_.append(lscr[gi])
    lmax = jnp.maximum(jnp.maximum(ls_[0], ls_[1]), ls_[2])
    ws = [jnp.exp(l - lmax) for l in ls_]
    winv = 1.0 / (ws[0] + ws[1] + ws[2])
    lane = lax.broadcasted_iota(jnp.int32, (tm, LANES), 1)
    lo_lanes = lane < HEAD_DIM

    def head_weights(w):
        return jnp.concatenate(
            [jnp.where(lo_lanes, jnp.broadcast_to(w[:, 2 * pr:2 * pr + 1], (tm, LANES)),
                       jnp.broadcast_to(w[:, 2 * pr + 1:2 * pr + 2], (tm, LANES)))
             for pr in range(B_HEADS // 2)], axis=1)

    ob = (head_weights(ws[0] * winv) * os_[0] + head_weights(ws[1] * winv) * os_[1]
          + head_weights(ws[2] * winv) * os_[2])
    o = (jnp.dot(oa_ref[...], wout_ref[0:A_QW, :], preferred_element_type=F32)
         + jnp.dot(ob.astype(BF16), wout_ref[A_QW:, :], preferred_element_type=F32))
    x1 = _read_part(x_refs, part_starts) + mod_ref[2:3, :] * o
    x1_ref[...] = x1
    h2 = _rms_mod(x1, n2_ref, mod_ref, 3, 4)
    h2_ref[...] = _pack_bf16_rows(h2)
    _route(h2, wr_ref, br_ref, idx_ref, gate_ref)


def _out1_kernel(sid_ref, *refs, part_starts):
    del sid_ref
    npart = len(part_starts)
    x_refs = refs[:npart]
    mod_ref, n2_ref, oc_ref, wout_ref, wr_ref, br_ref, x1_ref, h2_ref, idx_ref, gate_ref = refs[npart:]
    o = jnp.dot(oc_ref[...], wout_ref[...], preferred_element_type=F32)
    x1 = _read_part(x_refs, part_starts) + mod_ref[2:3, :] * o
    x1_ref[...] = x1
    h2 = _rms_mod(x1, n2_ref, mod_ref, 3, 4)
    h2_ref[...] = _pack_bf16_rows(h2)
    _route(h2, wr_ref, br_ref, idx_ref, gate_ref)


def _out_proj(layer, x_parts, mod, tile_sid, n2, attn, wout_bf, wr_t, br):
    tm = TOKEN_TILE
    part_starts, ntiles = _part_tiles(x_parts, tm)
    t = ntiles * tm
    row = lambda i, sid: (i, 0)
    const = lambda i, sid: (0, 0)
    in_specs = _part_specs(x_parts, part_starts, tm, D_MODEL) + [
        pl.BlockSpec((None, 6, D_MODEL), lambda i, sid: (sid[i], 0, 0)),
        pl.BlockSpec((1, D_MODEL), const)]
    scratch = []
    if layer == 0:
        in_specs.append(pl.BlockSpec((tm, A_QW), row))
        for (_, dil) in B_PAIRS:
            in_specs += [pl.BlockSpec((tm // dil, dil * B_GW), row), pl.BlockSpec((tm // dil, dil * LANES), row)]
        body = functools.partial(_out0_kernel, tm=tm, part_starts=part_starts)
        nscr = B_GROUPS * B_GW // LANES
        scratch = [pltpu.VMEM((nscr, tm, LANES), F32), pltpu.VMEM((B_GROUPS, tm, LANES), F32)]
    else:
        in_specs.append(pl.BlockSpec((tm, C_W), row))
        body = functools.partial(_out1_kernel, part_starts=part_starts)
    in_specs += [pl.BlockSpec(wout_bf.shape, const),
                 pl.BlockSpec((N_EXPERTS, D_MODEL), const),
                 pl.BlockSpec((N_EXPERTS, LANES), const)]
    gs = pltpu.PrefetchScalarGridSpec(
        num_scalar_prefetch=1, grid=(t // tm,), in_specs=in_specs,
        out_specs=[pl.BlockSpec((tm, D_MODEL), row), pl.BlockSpec((tm, PACK_W), row),
                   pl.BlockSpec((8, tm), lambda i, sid: (0, i)), pl.BlockSpec((8, tm), lambda i, sid: (0, i))],
        scratch_shapes=scratch)
    return pl.pallas_call(
        body, grid_spec=gs,
        out_shape=[jax.ShapeDtypeStruct((t, D_MODEL), F32), jax.ShapeDtypeStruct((t, PACK_W), jnp.int32),
                   jax.ShapeDtypeStruct((8, t), jnp.int32), jax.ShapeDtypeStruct((8, t), F32)],
        compiler_params=_cparams(("arbitrary",)),
        name="out_proj%d" % layer,
    )(tile_sid, *x_parts, mod, n2.reshape(1, D_MODEL), *attn, wout_bf, wr_t, br)


def _rank_kernel(idx_ref, tri_ref, rank_ref, cnt_ref, carry):
    i = pl.program_id(0)

    @pl.when(i == 0)
    def _():
        carry[...] = jnp.zeros_like(carry)

    tk = idx_ref.shape[1]
    eid = lax.broadcasted_iota(jnp.int32, (N_EXPERTS, tk), 0)
    hits = [eid == idx_ref[k:k + 1, :] for k in range(TOP_K)]
    onehot = sum(h.astype(F32) for h in hits)
    incl = jnp.dot(onehot.astype(BF16), tri_ref[...], preferred_element_type=F32)
    before = incl - onehot + carry[:, 0:1]
    rows = [jnp.sum(jnp.where(h, before, 0.0), axis=0, keepdims=True) for h in hits]
    rows.append(jnp.zeros((8 - TOP_K, tk), F32))
    rank_ref[...] = jnp.concatenate(rows, axis=0).astype(jnp.int32)
    carry[...] = carry[...] + incl[:, tk - 1:tk]
    cnt_ref[...] = carry[...].astype(jnp.int32)


def _slot_kernel(idx_ref, rank_ref, start_ref, slot_ref):
    tk = idx_ref.shape[1]
    eid = lax.broadcasted_iota(jnp.int32, (N_EXPERTS, tk), 0)
    start = start_ref[:, 0:1]
    rows = []
    for k in range(TOP_K):
        base = jnp.sum(jnp.where(eid == idx_ref[k:k + 1, :], start, 0), axis=0, keepdims=True)
        rows.append(base + rank_ref[k:k + 1, :])
    rows.append(jnp.zeros((8 - TOP_K, tk), jnp.int32))
    slot_ref[...] = jnp.concatenate(rows, axis=0)


def _route_slots(idx):
    t = idx.shape[1]
    tk = ROUTE_TILE
    tri = jnp.asarray(np.triu(np.ones((tk, tk), np.float32)), BF16)
    tok = lambda i: (0, i)
    rank, cnt = pl.pallas_call(
        _rank_kernel, grid=(t // tk,),
        in_specs=[pl.BlockSpec((8, tk), tok), pl.BlockSpec((tk, tk), lambda i: (0, 0))],
        out_specs=[pl.BlockSpec((8, tk), tok), pl.BlockSpec((N_EXPERTS, LANES), lambda i: (0, 0))],
        out_shape=[jax.ShapeDtypeStruct((8, t), jnp.int32), jax.ShapeDtypeStruct((N_EXPERTS, LANES), jnp.int32)],
        scratch_shapes=[pltpu.VMEM((N_EXPERTS, LANES), F32)],
        compiler_params=_cparams(("arbitrary",)),
        name="route_rank",
    )(idx, tri)
    counts = cnt[:, 0]
    padded = (counts + MOE_BM - 1) // MOE_BM * MOE_BM
    pad_end = jnp.cumsum(padded)
    pad_start = pad_end - padded
    nblk = (t * TOP_K) // MOE_BM + N_EXPERTS
    blk_row0 = jnp.arange(nblk, dtype=jnp.int32) * MOE_BM
    blk_expert = jnp.minimum(jnp.sum((pad_end[None, :] <= blk_row0[:, None]).astype(jnp.int32), axis=1),
                             N_EXPERTS - 1)
    nused = (pad_end[-1] // MOE_BM).astype(jnp.int32).reshape(1)
    start_b = jnp.broadcast_to(pad_start.astype(jnp.int32)[:, None], (N_EXPERTS, LANES))
    slot = pl.pallas_call(
        _slot_kernel, grid=(t // tk,),
        in_specs=[pl.BlockSpec((8, tk), tok), pl.BlockSpec((8, tk), tok),
                  pl.BlockSpec((N_EXPERTS, LANES), lambda i: (0, 0))],
        out_specs=pl.BlockSpec((8, tk), tok),
        out_shape=jax.ShapeDtypeStruct((8, t), jnp.int32),
        compiler_params=_cparams(("arbitrary",)),
        name="route_slot",
    )(idx, rank, start_b)
    return slot, blk_expert, nused


SC_CORES = 2
SC_SUBCORES = 16
SC_ROW_BUFFER_BYTES = 256 * 1024
SC_MAX_INDEX_LIST = 128


def _sc_chunks(t, row_bytes):
    chunk = min(SC_MAX_INDEX_LIST, SC_ROW_BUFFER_BYTES // row_bytes)
    nchunk = t // chunk
    per = nchunk // (SC_CORES * SC_SUBCORES)
    assert per * SC_CORES * SC_SUBCORES * chunk == t
    return chunk, nchunk, per


def _sc_slot_lists(slot, nchunk, chunk):
    return slot[:TOP_K].reshape(TOP_K, nchunk, chunk).transpose(1, 0, 2)


def _sc_dispatch(h2, slot, nslots):
    t, w = h2.shape
    chunk, nchunk, per = _sc_chunks(t, w * h2.dtype.itemsize)
    mesh = plsc.VectorSubcoreMesh(core_axis_name="c", subcore_axis_name="s")

    @functools.partial(
        pl.kernel, mesh=mesh, out_type=jax.ShapeDtypeStruct((nslots, w), h2.dtype),
        scratch_types=[pltpu.VMEM((TOP_K, chunk), jnp.int32), pltpu.VMEM((chunk, w), h2.dtype)],
        name="moe_dispatch_sc")
    def body(h_hbm, slot_hbm, xs_hbm, idx_v, rows_v):
        wid = lax.axis_index("s") * SC_CORES + lax.axis_index("c")

        @pl.loop(0, per)
        def _(j):
            c = wid * per + j
            pltpu.sync_copy(slot_hbm.at[c], idx_v)
            pltpu.sync_copy(h_hbm.at[pl.ds(c * chunk, chunk)], rows_v)
            for k in range(TOP_K):
                pltpu.sync_copy(rows_v, xs_hbm.at[idx_v.at[k]])

    return body(h2, _sc_slot_lists(slot, nchunk, chunk))


def _sc_gather(ys, slot, t):
    w = ys.shape[1]
    nbuf = 2
    chunk, nchunk, per = _sc_chunks(t, nbuf * w * ys.dtype.itemsize)
    mesh = plsc.VectorSubcoreMesh(core_axis_name="c", subcore_axis_name="s")

    @functools.partial(
        pl.kernel, mesh=mesh, out_type=jax.ShapeDtypeStruct((TOP_K, t, w), ys.dtype),
        scratch_types=[pltpu.VMEM((TOP_K, chunk), jnp.int32), pltpu.VMEM((nbuf, chunk, w), ys.dtype),
                       pltpu.SemaphoreType.DMA((nbuf,))],
        name="moe_gather_sc")
    def body(ys_hbm, slot_hbm, out_hbm, idx_v, rows_v, wsem):
        wid = lax.axis_index("s") * SC_CORES + lax.axis_index("c")

        @pl.loop(0, per)
        def _(j):
            c = wid * per + j
            pltpu.sync_copy(slot_hbm.at[c], idx_v)
            writes = []
            for k in range(TOP_K):
                b = k % nbuf
                if k >= nbuf:
                    writes[k - nbuf].wait()
                pltpu.sync_copy(ys_hbm.at[idx_v.at[k]], rows_v.at[b])
                writes.append(pltpu.async_copy(rows_v.at[b], out_hbm.at[k, pl.ds(c * chunk, chunk)], wsem.at[b]))
            for wr in writes[-nbuf:]:
                wr.wait()

    return body(ys, _sc_slot_lists(slot, nchunk, chunk))


def _ffn_kernel(be_ref, nu_ref, xs_ref, wgu_ref, bgu_ref, wd_ref, bd_ref, y_ref, wgu_bf, wd_bf):
    i = pl.program_id(0)
    prev = be_ref[jnp.maximum(i - 1, 0)]

    @pl.when(i < nu_ref[0])
    def _():
        @pl.when((i == 0) | (be_ref[i] != prev))
        def _():
            wgu_bf[...] = wgu_ref[...].astype(BF16)
            wd_bf[...] = wd_ref[...].astype(BF16)

        x = _unpack_bf16_rows(xs_ref[...])
        gu = jnp.dot(x, wgu_bf[...], preferred_element_type=F32) + bgu_ref[...]
        gate = jnp.minimum(gu[:, :D_FF], SWIGLU_LIMIT)
        up = jnp.clip(gu[:, D_FF:], -SWIGLU_LIMIT, SWIGLU_LIMIT)
        act = (up + 1.0) * gate * jax.nn.sigmoid(SWIGLU_ALPHA * gate)
        y = jnp.dot(act.astype(BF16), wd_bf[...], preferred_element_type=F32) + bd_ref[...]
        y_ref[...] = _pack_bf16_rows(y)

    @pl.when(i >= nu_ref[0])
    def _():
        y_ref[...] = jnp.zeros_like(y_ref)


def _expert_ffn(xs, blk_expert, nused, wgu, bgu, wd, bd):
    nslots = xs.shape[0]
    nblk = nslots // MOE_BM
    blk = lambda i, be, nu: (jnp.minimum(i, nu[0] - 1), 0)
    exp3 = lambda i, be, nu: (be[jnp.minimum(i, nu[0] - 1)], 0, 0)
    gs = pltpu.PrefetchScalarGridSpec(
        num_scalar_prefetch=2, grid=(nblk,),
        in_specs=[pl.BlockSpec((MOE_BM, PACK_W), blk),
                  pl.BlockSpec((None, D_MODEL, 2 * D_FF), exp3),
                  pl.BlockSpec((None, 1, 2 * D_FF), exp3),
                  pl.BlockSpec((None, D_FF, D_MODEL), exp3),
                  pl.BlockSpec((None, 1, D_MODEL), exp3)],
        out_specs=pl.BlockSpec((MOE_BM, PACK_W), lambda i, be, nu: (i, 0)),
        scratch_shapes=[pltpu.VMEM((D_MODEL, 2 * D_FF), BF16), pltpu.VMEM((D_FF, D_MODEL), BF16)])
    return pl.pallas_call(
        _ffn_kernel, grid_spec=gs,
        out_shape=jax.ShapeDtypeStruct((nslots, PACK_W), jnp.int32),
        compiler_params=_cparams(("arbitrary",)),
        name="moe_ffn",
    )(blk_expert, nused, xs, wgu, bgu.reshape(N_EXPERTS, 1, 2 * D_FF), wd, bd.reshape(N_EXPERTS, 1, D_MODEL))


def _combine_kernel(sid_ref, x_ref, mod_ref, gate_ref, yk_ref, *o_refs, tm, out_starts):
    del sid_ref
    gpad = jnp.concatenate([gate_ref[...], jnp.zeros((LANES - 8, tm), F32)], axis=0)
    gcol = gpad.T
    acc = gcol[:, 0:1] * _unpack_bf16_rows(yk_ref[0]).astype(F32)
    for k in range(1, TOP_K):
        acc = acc + gcol[:, k:k + 1] * _unpack_bf16_rows(yk_ref[k]).astype(F32)
    out = x_ref[...] + mod_ref[5:6, :] * acc
    i = pl.program_id(0)
    ends = out_starts[1:] + (None,)
    for o_ref, s0, s1 in zip(o_refs, out_starts, ends):
        mine = (i >= s0) if s1 is None else ((i >= s0) & (i < s1))

        @pl.when(mine)
        def _():
            o_ref[...] = out


def _combine(x1, mod, tile_sid, gates, yk, out_rows):
    t = x1.shape[0]
    tm = MOVE_TILE
    out_shape = [jax.ShapeDtypeStruct((n, D_MODEL), F32) for n in out_rows]
    out_starts, ntiles = _part_tiles(out_shape, tm)
    assert ntiles * tm == t
    gs = pltpu.PrefetchScalarGridSpec(
        num_scalar_prefetch=1, grid=(t // tm,),
        in_specs=[pl.BlockSpec((tm, D_MODEL), lambda i, sid: (i, 0)),
                  pl.BlockSpec((None, 6, D_MODEL), lambda i, sid: (sid[i], 0, 0)),
                  pl.BlockSpec((8, tm), lambda i, sid: (0, i)),
                  pl.BlockSpec((TOP_K, tm, PACK_W), lambda i, sid: (0, i, 0))],
        out_specs=_part_specs(out_shape, out_starts, tm, D_MODEL))
    return pl.pallas_call(
        functools.partial(_combine_kernel, tm=tm, out_starts=out_starts), grid_spec=gs,
        out_shape=out_shape,
        compiler_params=_cparams(("arbitrary",)),
        name="moe_combine",
    )(tile_sid, x1, mod, gates, yk)


def _moe(x1, h2, idx, gates, mod, tile_sid_move, wgu, bgu, wd, bd, out_rows):
    t = x1.shape[0]
    nslots = t * TOP_K + N_EXPERTS * MOE_BM
    slot, blk_expert, nused = _route_slots(idx)
    xs = _sc_dispatch(h2, slot, nslots)
    ys = _expert_ffn(xs, blk_expert, nused, wgu, bgu, wd, bd)
    return _combine(x1, mod, tile_sid_move, gates, _sc_gather(ys, slot, t), out_rows)


def _tile_table(seq_lens, tile, fn):
    vals = []
    start = 0
    for sid, n in enumerate(seq_lens):
        assert n % tile == 0
        for j in range(n // tile):
            vals.append(fn(sid, start, n, j))
        start += n
    return jnp.asarray(np.asarray(vals, np.int32))


def _head_block_diag():
    bd = np.kron(np.eye(2 * LANES // HEAD_DIM, dtype=np.float32),
                 np.full((HEAD_DIM, HEAD_DIM), 1.0 / HEAD_DIM, np.float32))
    return jnp.asarray(bd, BF16)


_A_HEAD_ORDER = [kv * A_GROUP + g for g in range(A_GROUP) for kv in range(A_KV_HEADS)]


def _trunk(x_parts, c_pad, seq_lens, p):
    t = sum(a.shape[0] for a in x_parts)
    sid_tok = _tile_table(seq_lens, TOKEN_TILE, lambda sid, s, n, j: sid)
    sid_move = _tile_table(seq_lens, MOVE_TILE, lambda sid, s, n, j: sid)
    bd = _head_block_diag()

    mod0 = _ada_mod(c_pad, p["l0_ada_w"], p["l0_ada_b"]).reshape(c_pad.shape[0], 6, D_MODEL)
    w_in = p["l0_w_in"]
    a_cols = np.concatenate([np.arange(h * HEAD_DIM, (h + 1) * HEAD_DIM) for h in _A_HEAD_ORDER])
    bq, bk, bv = A_IN, A_IN + B_GROUPS * B_GW, A_IN + 2 * B_GROUPS * B_GW
    col_blocks = [w_in[:, a_cols], w_in[:, A_QW:A_IN]]
    for g in range(B_GROUPS):
        col_blocks += [w_in[:, bq + g * B_GW:bq + (g + 1) * B_GW],
                       w_in[:, bk + g * B_GW:bk + (g + 1) * B_GW],
                       w_in[:, bv + g * B_GW:bv + (g + 1) * B_GW]]
    w0 = jnp.concatenate(col_blocks, axis=1).astype(BF16)
    ones = lambda n: jnp.ones((n,), F32)
    gain_blocks = [jnp.tile(p["l0_q_norm_a"], A_Q_HEADS) * QK_SCALE, jnp.tile(p["l0_k_norm_a"], A_KV_HEADS),
                   ones(A_KVW)]
    for g in range(B_GROUPS):
        gain_blocks += [jnp.tile(p["l0_q_norm_b"], B_HEADS) * QK_SCALE, jnp.tile(p["l0_k_norm_b"], B_HEADS),
                        ones(B_GW)]
    gain0 = jnp.concatenate(gain_blocks).reshape(1, -1)
    slabs0 = [(0, A_QW, True, 1), (A_QW, A_KVW, True, 1), (A_QW + A_KVW, A_KVW, False, 1)]
    c0 = A_IN
    for g, (_, dil) in enumerate(B_PAIRS):
        slabs0 += [(c0, B_GW, True, dil), (c0 + B_GW, B_GW, True, dil), (c0 + 2 * B_GW, B_GW, False, dil)]
        c0 += 3 * B_GW
    proj = _in_proj(x_parts, mod0, sid_tok, p["l0_norm1"], w0, gain0, bd, tuple(slabs0))
    qa, ka, va = proj[0:3]

    slopes_a = _alibi_slopes(A_Q_HEADS)
    heads_a = tuple(tuple((slopes_a[kv * A_GROUP + g], kv * A_GROUP + g) for kv in range(A_KV_HEADS))
                    for g in range(A_GROUP))
    (oa,) = _band_attn(qa, ka, va, seq_lens, tq=ATTN_TQ_A, dil=1, qw=A_QW, kvw=A_KVW, half=A_HALF, heads=heads_a,
                       kv_group=(0,) * A_GROUP, sink=p["l0_sink_a"].astype(F32), out_dtype=BF16, want_lse=False)

    slopes_b = _alibi_slopes(B_GROUPS * B_HEADS)
    attn0 = [oa]
    for g, (window, dil) in enumerate(B_PAIRS):
        qg, kg, vg = proj[3 + 3 * g:6 + 3 * g]
        heads_b = tuple(tuple((slopes_b[g * B_HEADS + 2 * pr + s] * dil, 0) for s in range(2))
                        for pr in range(B_HEADS // 2))
        og, lg = _band_attn(qg, kg, vg, [n // dil for n in seq_lens], tq=ATTN_TQ_B, dil=dil, qw=B_GW, kvw=B_GW, half=window // (2 * dil),
                            heads=heads_b, kv_group=tuple(range(B_HEADS // 2)), sink=None, out_dtype=F32,
                            want_lse=True)
        attn0 += [og, lg]

    w_out = p["l0_w_out"]
    wout0 = jnp.concatenate([w_out[a_cols], w_out[A_QW:]], axis=0).astype(BF16)
    br0 = jnp.broadcast_to(p["l0_b_router"].astype(F32)[:, None], (N_EXPERTS, LANES))
    x1, h2, idx, gates = _out_proj(0, x_parts, mod0, sid_tok, p["l0_norm2"], attn0, wout0, p["l0_w_router"].T,
                                   br0)
    x = _moe(x1, h2, idx, gates, mod0, sid_move, p["l0_w_gate_up"], p["l0_b_gate_up"], p["l0_w_down"],
             p["l0_b_down"], [t])

    mod1 = _ada_mod(c_pad, p["l1_ada_w"], p["l1_ada_b"]).reshape(c_pad.shape[0], 6, D_MODEL)
    w1 = p["l1_w_in"].astype(BF16)
    gain1 = jnp.concatenate([jnp.tile(p["l1_q_norm_c"], C_HEADS) * QK_SCALE, jnp.tile(p["l1_k_norm_c"], C_HEADS),
                             ones(C_W)]).reshape(1, -1)
    slabs1 = ((0, C_W, True, 1), (C_W, C_W, True, 1), (2 * C_W, C_W, False, 1))
    qc, kc, vc = _in_proj(x, mod1, sid_tok, p["l1_norm1"], w1, gain1, bd, slabs1)
    na_tile = NA_TROWS * GRID_W
    tile_r0 = _tile_table(seq_lens, na_tile, lambda sid, s, n, j: j * NA_TROWS)
    tile_nr = _tile_table(seq_lens, na_tile, lambda sid, s, n, j: n // GRID_W)
    oc = _na_attn(qc, kc, vc, _na_bias_table(p["l1_rpb_c"]), tile_r0, tile_nr)
    br1 = jnp.broadcast_to(p["l1_b_router"].astype(F32)[:, None], (N_EXPERTS, LANES))
    x1, h2, idx, gates = _out_proj(1, x, mod1, sid_tok, p["l1_norm2"], [oc], p["l1_w_out"].astype(BF16),
                                   p["l1_w_router"].T, br1)
    return _moe(x1, h2, idx, gates, mod1, sid_move, p["l1_w_gate_up"], p["l1_b_gate_up"], p["l1_w_down"],
                p["l1_b_down"], [a.shape[0] for a in x_parts])


def kernel(x_prompt, x_sample, c_prompt, c_sample, l0_ada_w, l0_ada_b, l0_norm1, l0_w_in, l0_q_norm_a, l0_k_norm_a, l0_sink_a, l0_q_norm_b, l0_k_norm_b, l0_w_out, l0_norm2, l0_w_router, l0_b_router, l0_w_gate_up, l0_b_gate_up, l0_w_down, l0_b_down, l1_ada_w, l1_ada_b, l1_norm1, l1_w_in, l1_q_norm_c, l1_k_norm_c, l1_rpb_c, l1_w_out, l1_norm2, l1_w_router, l1_b_router, l1_w_gate_up, l1_b_gate_up, l1_w_down, l1_b_down):
    p = dict(l0_ada_w=l0_ada_w, l0_ada_b=l0_ada_b, l0_norm1=l0_norm1, l0_w_in=l0_w_in, l0_q_norm_a=l0_q_norm_a,
             l0_k_norm_a=l0_k_norm_a, l0_sink_a=l0_sink_a, l0_q_norm_b=l0_q_norm_b, l0_k_norm_b=l0_k_norm_b,
             l0_w_out=l0_w_out, l0_norm2=l0_norm2, l0_w_router=l0_w_router, l0_b_router=l0_b_router,
             l0_w_gate_up=l0_w_gate_up, l0_b_gate_up=l0_b_gate_up, l0_w_down=l0_w_down, l0_b_down=l0_b_down,
             l1_ada_w=l1_ada_w, l1_ada_b=l1_ada_b, l1_norm1=l1_norm1, l1_w_in=l1_w_in, l1_q_norm_c=l1_q_norm_c,
             l1_k_norm_c=l1_k_norm_c, l1_rpb_c=l1_rpb_c, l1_w_out=l1_w_out, l1_norm2=l1_norm2,
             l1_w_router=l1_w_router, l1_b_router=l1_b_router, l1_w_gate_up=l1_w_gate_up,
             l1_b_gate_up=l1_b_gate_up, l1_w_down=l1_w_down, l1_b_down=l1_b_down)
    nb_p, len_p, d = x_prompt.shape
    nb_s, len_s, _ = x_sample.shape
    seq_lens = (len_p,) * nb_p + (len_s,) * nb_s
    x_parts = [x_prompt.reshape(nb_p * len_p, d), x_sample.reshape(nb_s * len_s, d)]
    nseq = nb_p + nb_s
    c_pad = jnp.concatenate([c_prompt, c_sample, jnp.zeros((-nseq % 8, d), F32)], axis=0)
    y_p, y_s = _trunk(x_parts, c_pad, seq_lens, p)
    return (y_p.reshape(nb_p, len_p, d), y_s.reshape(nb_s, len_s, d))
```

```python
import functools
import math

import numpy as np
import jax
import jax.numpy as jnp
from jax import lax
from jax.experimental import pallas as pl
from jax.experimental.pallas import tpu as pltpu
from jax.experimental.pallas import tpu_sc as plsc

F32 = jnp.float32
BF16 = jnp.bfloat16
HIGHEST = lax.Precision.HIGHEST

D_MODEL = 1024
HEAD_DIM = 64
LANES = 128
GRID_W = 64
A_Q_HEADS = 8
A_KV_HEADS = 2
A_GROUP = A_Q_HEADS // A_KV_HEADS
A_HALF = 128
B_PAIRS = ((128, 1), (512, 4), (2048, 16))
B_GROUPS = 3
B_HEADS = 4
A_QW = A_Q_HEADS * HEAD_DIM
A_KVW = A_KV_HEADS * HEAD_DIM
A_IN = A_QW + 2 * A_KVW
B_GW = B_HEADS * HEAD_DIM
C_HEADS = 16
C_W = C_HEADS * HEAD_DIM
NA_KH = 8
NA_KW = 16
N_EXPERTS = 32
TOP_K = 4
D_FF = 1024
SWIGLU_LIMIT = 7.0
SWIGLU_ALPHA = 1.702
RMS_EPS = 1e-6
NEG_INF = -1e30
QK_SCALE = HEAD_DIM ** -0.5

TOKEN_TILE = 1024
ATTN_TQ_A = 256
ATTN_TQ_B = 512
ATTN_UQ = 128
NA_TROWS = 8
MOE_BM = 512
ROUTE_TILE = 1024
MOVE_TILE = 256
VMEM_LIMIT = 56 * 1024 * 1024


def _alibi_slopes(n):
    return [float(2.0 ** (-8.0 * (j + 1) / n)) for j in range(n)]


def _cparams(sem, flags=None):
    return pltpu.CompilerParams(dimension_semantics=sem, vmem_limit_bytes=VMEM_LIMIT, flags=flags)


def _mod_kernel(c_ref, w_ref, b_ref, o_ref):
    c = c_ref[...]
    s = c * jax.nn.sigmoid(c)
    o_ref[...] = jnp.dot(s, w_ref[...], precision=HIGHEST, preferred_element_type=F32) + b_ref[...]


def _ada_mod(c_pad, w, b):
    nrow = c_pad.shape[0]
    ncol = w.shape[1]
    tn = D_MODEL
    return pl.pallas_call(
        _mod_kernel,
        grid=(ncol // tn,),
        in_specs=[pl.BlockSpec((nrow, D_MODEL), lambda j: (0, 0)),
                  pl.BlockSpec((D_MODEL, tn), lambda j: (0, j)),
                  pl.BlockSpec((1, tn), lambda j: (0, j))],
        out_specs=pl.BlockSpec((nrow, tn), lambda j: (0, j)),
        out_shape=jax.ShapeDtypeStruct((nrow, ncol), F32),
        compiler_params=_cparams(("arbitrary",)),
        name="ada_mod",
    )(c_pad, w, b.reshape(1, ncol))


def _head_mean_sq(y, bd_ref):
    w = y.shape[1]
    outs = []
    for c0 in range(0, w, 2 * LANES):
        cw = min(2 * LANES, w - c0)
        sq = y[:, c0:c0 + cw] * y[:, c0:c0 + cw]
        outs.append(jnp.dot(sq.astype(BF16), bd_ref[0:cw, 0:cw], preferred_element_type=F32))
    return outs[0] if len(outs) == 1 else jnp.concatenate(outs, axis=1)


def _rms_mod(x, g_ref, mod_ref, shift_row, scale_row):
    ms = jnp.mean(x * x, axis=-1, keepdims=True)
    xn = x * lax.rsqrt(ms + RMS_EPS) * g_ref[...]
    return xn * (1.0 + mod_ref[scale_row:scale_row + 1, :]) + mod_ref[shift_row:shift_row + 1, :]


def _part_tiles(parts, tile):
    starts, s = [], 0
    for a in parts:
        assert a.shape[0] % tile == 0
        starts.append(s)
        s += a.shape[0] // tile
    return tuple(starts), s


def _part_specs(parts, starts, tile, ncol):
    def spec(a, s0):
        n = a.shape[0] // tile
        return pl.BlockSpec((tile, ncol), lambda i, *_: (jnp.clip(i - s0, 0, n - 1), 0))
    return [spec(a, s0) for a, s0 in zip(parts, starts)]


def _read_part(refs, starts):
    i = pl.program_id(0)
    x = refs[0][...]
    for ref, s0 in zip(refs[1:], starts[1:]):
        x = jnp.where(i >= s0, ref[...], x)
    return x


def _in_kernel(sid_ref, *refs, slabs, tm, part_starts):
    del sid_ref
    npart = len(part_starts)
    x_refs = refs[:npart]
    mod_ref, n1_ref, w_ref, gain_ref, bd_ref = refs[npart:npart + 5]
    rest = refs[npart + 5:]
    nout = len(slabs)
    out_refs = rest[:nout]
    scr_ref = rest[nout] if len(rest) > nout else None
    h = _rms_mod(_read_part(x_refs, part_starts), n1_ref, mod_ref, 0, 1).astype(BF16)
    proj = jnp.dot(h, w_ref[...], preferred_element_type=F32)
    for (c0, w, normed, dil), o_ref in zip(slabs, out_refs):
        y = proj[:, c0:c0 + w]
        if normed:
            y = y * lax.rsqrt(_head_mean_sq(y, bd_ref) + RMS_EPS) * gain_ref[:, c0:c0 + w]
        if dil == 1:
            o_ref[...] = y.astype(BF16)
        else:
            for c in range(w // LANES):
                scr_ref[c] = y[:, c * LANES:(c + 1) * LANES]
            for r in range(dil):
                for c in range(w // LANES):
                    o_ref[:, r * w + c * LANES:r * w + (c + 1) * LANES] = (
                        scr_ref[c, pl.ds(r, tm // dil, stride=dil), :].astype(BF16))


def _in_proj(x_parts, mod, tile_sid, n1, w_bf, gain, bd, slabs):
    tm = TOKEN_TILE
    part_starts, ntiles = _part_tiles(x_parts, tm)
    t = ntiles * tm
    ncols = w_bf.shape[1]
    out_shapes, out_specs = [], []
    for (c0, w, normed, dil) in slabs:
        out_shapes.append(jax.ShapeDtypeStruct((t // dil, dil * w), BF16))
        out_specs.append(pl.BlockSpec((tm // dil, dil * w), lambda i, sid: (i, 0)))
    need_scr = any(s[3] > 1 for s in slabs)
    scratch = [pltpu.VMEM((max(s[1] for s in slabs if s[3] > 1) // LANES, tm, LANES), F32)] if need_scr else []
    gs = pltpu.PrefetchScalarGridSpec(
        num_scalar_prefetch=1,
        grid=(t // tm,),
        in_specs=_part_specs(x_parts, part_starts, tm, D_MODEL) + [
            pl.BlockSpec((None, 6, D_MODEL), lambda i, sid: (sid[i], 0, 0)),
            pl.BlockSpec((1, D_MODEL), lambda i, sid: (0, 0)),
            pl.BlockSpec((D_MODEL, ncols), lambda i, sid: (0, 0)),
            pl.BlockSpec((1, ncols), lambda i, sid: (0, 0)),
            pl.BlockSpec((2 * LANES, 2 * LANES), lambda i, sid: (0, 0))],
        out_specs=out_specs,
        scratch_shapes=scratch)
    return pl.pallas_call(
        functools.partial(_in_kernel, slabs=slabs, tm=tm, part_starts=part_starts),
        grid_spec=gs,
        out_shape=out_shapes,
        compiler_params=_cparams(("arbitrary",)),
        name="in_proj",
    )(tile_sid, *x_parts, mod, n1.reshape(1, D_MODEL), w_bf, gain, bd)


def _band_kernel(s0_ref, s1_ref, *refs, tq, hb, half, heads, kv_group, has_sink, want_lse):
    pos = 0
    sink_ref = None
    if has_sink:
        sink_ref = refs[0]
        pos = 1
    q_ref, kp_ref, kc_ref, kn_ref, vp_ref, vc_ref, vn_ref = refs[pos:pos + 7]
    o_ref = refs[pos + 7]
    lse_ref = refs[pos + 8] if want_lse else None
    i = pl.program_id(1)
    uq = ATTN_UQ
    wlen = uq + 2 * hb
    kall = jnp.concatenate([kp_ref[...], kc_ref[...], kn_ref[...]], axis=0)
    vall = jnp.concatenate([vp_ref[...], vc_ref[...], vn_ref[...]], axis=0)
    lane = lax.broadcasted_iota(jnp.int32, (uq, LANES), 1)
    lo_lanes = lane < HEAD_DIM
    for sub in range(tq // uq):
        rows = slice(sub * uq, (sub + 1) * uq)
        kwin = kall[sub * uq:sub * uq + wlen]
        vwin = vall[sub * uq:sub * uq + wlen]
        q0 = i * tq + sub * uq
        kpos = q0 - hb + lax.broadcasted_iota(jnp.int32, (wlen, uq), 0)
        qpos = q0 + lax.broadcasted_iota(jnp.int32, (wlen, uq), 1)
        rel = kpos - qpos
        valid = (jnp.abs(rel) <= half) & (kpos >= s0_ref[i]) & (kpos < s1_ref[i])
        distm = jnp.where(valid, jnp.abs(rel).astype(F32), 1e32)
        lse_rows = []
        for g, pair in enumerate(heads):
            qg = q_ref[rows, g * LANES:(g + 1) * LANES]
            kg = kwin[:, kv_group[g] * LANES:(kv_group[g] + 1) * LANES]
            vg = vwin[:, kv_group[g] * LANES:(kv_group[g] + 1) * LANES]
            outs = []
            for s, (slope, sink_idx) in enumerate(pair):
                keep = lo_lanes if s == 0 else jnp.logical_not(lo_lanes)
                qm = jnp.where(keep, qg, jnp.zeros_like(qg))
                sc = lax.dot_general(kg, qm, (((1,), (1,)), ((), ())), preferred_element_type=F32)
                sc = sc - slope * distm
                m = jnp.max(sc, axis=0, keepdims=True)
                if has_sink:
                    m = jnp.maximum(m, sink_ref[sink_idx])
                e = jnp.exp(sc - m)
                den = jnp.sum(e, axis=0, keepdims=True)
                if has_sink:
                    den = den + jnp.exp(sink_ref[sink_idx] - m)
                p = (e * (1.0 / den)).astype(BF16)
                outs.append(lax.dot_general(p, vg, (((0,), (0,)), ((), ())), preferred_element_type=F32))
                if want_lse:
                    lse_rows.append(m + jnp.log(den))
            o_ref[rows, g * LANES:(g + 1) * LANES] = jnp.where(lo_lanes, outs[0], outs[1]).astype(o_ref.dtype)
        if want_lse:
            lse_t = jnp.concatenate(lse_rows + [jnp.zeros((LANES - len(lse_rows), uq), F32)], axis=0)
            lse_ref[rows, :] = lse_t.T


def _band_attn(q, k, v, seq_rows, *, tq, dil, qw, kvw, half, heads, kv_group, sink, out_dtype, want_lse):
    rows = q.shape[0]
    hb = half
    s0 = _tile_table(seq_rows, tq, lambda sid, s, n, j: s)
    s1 = _tile_table(seq_rows, tq, lambda sid, s, n, j: s + n)
    per = tq // hb
    nt = rows // tq
    nhb = rows // hb
    has_sink = sink is not None
    in_specs = []
    args = []
    if has_sink:
        in_specs.append(pl.BlockSpec(memory_space=pltpu.SMEM))
        args.append(sink)
    qmap = lambda r, i, a, b: (i, r)
    pmap = lambda r, i, a, b: (jnp.maximum(i * per - 1, 0), r)
    nmap = lambda r, i, a, b: (jnp.minimum((i + 1) * per, nhb - 1), r)
    in_specs += [pl.BlockSpec((tq, qw), qmap),
                 pl.BlockSpec((hb, kvw), pmap), pl.BlockSpec((tq, kvw), qmap), pl.BlockSpec((hb, kvw), nmap),
                 pl.BlockSpec((hb, kvw), pmap), pl.BlockSpec((tq, kvw), qmap), pl.BlockSpec((hb, kvw), nmap)]
    args += [q, k, k, k, v, v, v]
    out_shape = [jax.ShapeDtypeStruct((rows, dil * qw), out_dtype)]
    out_specs = [pl.BlockSpec((tq, qw), qmap)]
    if want_lse:
        out_shape.append(jax.ShapeDtypeStruct((rows, dil * LANES), F32))
        out_specs.append(pl.BlockSpec((tq, LANES), qmap))
    gs = pltpu.PrefetchScalarGridSpec(num_scalar_prefetch=2, grid=(dil, nt),
                                      in_specs=in_specs, out_specs=out_specs)
    res = pl.pallas_call(
        functools.partial(_band_kernel, tq=tq, hb=hb, half=half, heads=heads, kv_group=kv_group,
                          has_sink=has_sink, want_lse=want_lse),
        grid_spec=gs,
        out_shape=out_shape,
        compiler_params=_cparams(("arbitrary", "arbitrary")),
        name="band_attn_d%d" % dil,
    )(s0, s1, *args)
    return res


def _na_kernel(r0_ref, nr_ref, q_ref, kp_ref, kc_ref, kn_ref, vp_ref, vc_ref, vn_ref, bias_ref,
               o_ref, kcat, vcat, *, halo):
    j = pl.program_id(0)
    tq = NA_TROWS * GRID_W
    hrows = halo * GRID_W
    kcat[0:hrows, :] = kp_ref[...]
    kcat[hrows:hrows + tq, :] = kc_ref[...]
    kcat[hrows + tq:, :] = kn_ref[...]
    vcat[0:hrows, :] = vp_ref[...]
    vcat[hrows:hrows + tq, :] = vc_ref[...]
    vcat[hrows + tq:, :] = vn_ref[...]
    r0 = r0_ref[j]
    nrows = nr_ref[j]
    kwin_len = NA_KH * GRID_W
    lane = lax.broadcasted_iota(jnp.int32, (GRID_W, LANES), 1)
    lo_lanes = lane < HEAD_DIM
    nrel = 2 * NA_KH - 1

    def row_body(u, carry):
        r = r0 + u
        rs = jnp.clip(r - NA_KH // 2, 0, nrows - NA_KH)
        off = pl.multiple_of((rs - r0 + halo) * GRID_W, GRID_W)
        bvar = rs - r + NA_KH - 1
        qrow = pl.multiple_of(u * GRID_W, GRID_W)
        for g in range(C_HEADS // 2):
            qg = q_ref[pl.ds(qrow, GRID_W), g * LANES:(g + 1) * LANES]
            qs = jnp.concatenate([jnp.where(lo_lanes, qg, jnp.zeros_like(qg)),
                                  jnp.where(lo_lanes, jnp.zeros_like(qg), qg)], axis=0)
            kg = kcat[pl.ds(off, kwin_len), g * LANES:(g + 1) * LANES]
            vg = vcat[pl.ds(off, kwin_len), g * LANES:(g + 1) * LANES]
            sc = lax.dot_general(kg, qs, (((1,), (1,)), ((), ())), preferred_element_type=F32)
            sc = sc + jnp.concatenate([bias_ref[g * nrel + bvar + kr] for kr in range(NA_KH)], axis=0)
            m = jnp.max(sc, axis=0, keepdims=True)
            e = jnp.exp(sc - m)
            den = jnp.sum(e, axis=0, keepdims=True)
            p = (e * (1.0 / den)).astype(BF16)
            pv = lax.dot_general(p, vg, (((0,), (0,)), ((), ())), preferred_element_type=F32)
            o = jnp.where(lo_lanes, pv[0:GRID_W], pv[GRID_W:])
            o_ref[pl.ds(qrow, GRID_W), g * LANES:(g + 1) * LANES] = o.astype(o_ref.dtype)
        return carry

    lax.fori_loop(0, NA_TROWS, row_body, 0)


def _na_attn(q, k, v, bias_tab, tile_r0, tile_nr):
    t = q.shape[0]
    halo = NA_KH // 2
    tq = NA_TROWS * GRID_W
    hrows = halo * GRID_W
    per = tq // hrows
    nhb = t // hrows
    nt = t // tq
    qmap = lambda j, a, b: (j, 0)
    pmap = lambda j, a, b: (jnp.maximum(j * per - 1, 0), 0)
    nmap = lambda j, a, b: (jnp.minimum((j + 1) * per, nhb - 1), 0)
    gs = pltpu.PrefetchScalarGridSpec(
        num_scalar_prefetch=2, grid=(nt,),
        in_specs=[pl.BlockSpec((tq, C_W), qmap),
                  pl.BlockSpec((hrows, C_W), pmap), pl.BlockSpec((tq, C_W), qmap), pl.BlockSpec((hrows, C_W), nmap),
                  pl.BlockSpec((hrows, C_W), pmap), pl.BlockSpec((tq, C_W), qmap), pl.BlockSpec((hrows, C_W), nmap),
                  pl.BlockSpec(bias_tab.shape, lambda j, a, b: (0, 0, 0))],
        out_specs=pl.BlockSpec((tq, C_W), qmap),
        scratch_shapes=[pltpu.VMEM((tq + 2 * hrows, C_W), BF16), pltpu.VMEM((tq + 2 * hrows, C_W), BF16)])
    return pl.pallas_call(
        functools.partial(_na_kernel, halo=halo),
        grid_spec=gs,
        out_shape=jax.ShapeDtypeStruct((t, C_W), BF16),
        compiler_params=_cparams(("arbitrary",)),
        name="na_attn",
    )(tile_r0, tile_nr, q, k, k, k, v, v, v, bias_tab)


def _na_bias_table(rpb):
    c = np.arange(GRID_W)
    cs = np.clip(c - NA_KW // 2, 0, GRID_W - NA_KW)
    kc = np.arange(GRID_W)
    valid = (kc[:, None] >= cs[None, :]) & (kc[:, None] < cs[None, :] + NA_KW)
    cidx = np.clip(kc[:, None] - c[None, :] + NA_KW - 1, 0, 2 * NA_KW - 2)
    onehot = jnp.asarray((cidx[None] == np.arange(2 * NA_KW - 1)[:, None, None]).astype(np.float32))
    tab = jnp.einsum("hrj,jkc->hrkc", rpb.astype(F32), onehot, precision=HIGHEST)
    tab = jnp.where(jnp.asarray(valid)[None, None], tab, NEG_INF)
    pairs = jnp.concatenate([tab[0::2], tab[1::2]], axis=-1)
    return pairs.reshape((C_HEADS // 2) * (2 * NA_KH - 1), GRID_W, LANES)


def _route(h2, wr_ref, br_ref, idx_ref, gate_ref):
    tm = h2.shape[0]
    logits = lax.dot_general(wr_ref[...], h2, (((1,), (1,)), ((), ())),
                             precision=HIGHEST, preferred_element_type=F32) + br_ref[:, 0:1]
    eid = lax.broadcasted_iota(jnp.int32, (N_EXPERTS, tm), 0)
    vals, idxs = [], []
    for _ in range(TOP_K):
        m = jnp.max(logits, axis=0, keepdims=True)
        ix = jnp.min(jnp.where(logits == m, eid, N_EXPERTS), axis=0, keepdims=True)
        vals.append(m)
        idxs.append(ix)
        logits = jnp.where(eid == ix, -jnp.inf, logits)
    es = [jnp.exp(vk - vals[0]) for vk in vals]
    den = es[0] + es[1] + es[2] + es[3]
    pad_i = jnp.zeros((8 - TOP_K, tm), jnp.int32)
    pad_f = jnp.zeros((8 - TOP_K, tm), F32)
    idx_ref[...] = jnp.concatenate(idxs + [pad_i], axis=0)
    gate_ref[...] = jnp.concatenate([ek / den for ek in es] + [pad_f], axis=0)


PACK_W = D_MODEL // 2


def _pack_bf16_rows(h):
    bits = pltpu.bitcast(h.astype(BF16).astype(F32), jnp.int32)
    return (bits[:, :PACK_W] & jnp.int32(-65536)) | lax.shift_right_logical(bits[:, PACK_W:], 16)


def _unpack_bf16_rows(w):
    hi = pltpu.bitcast(w & jnp.int32(-65536), F32).astype(BF16)
    lo = pltpu.bitcast(lax.shift_left(w, 16), F32).astype(BF16)
    return jnp.concatenate([hi, lo], axis=1)


def _out0_kernel(sid_ref, *refs, tm, part_starts):
    del sid_ref
    npart = len(part_starts)
    x_refs = refs[:npart]
    (mod_ref, n2_ref, oa_ref, ob0_ref, ls0_ref, ob1_ref, ls1_ref, ob2_ref, ls2_ref,
     wout_ref, wr_ref, br_ref, x1_ref, h2_ref, idx_ref, gate_ref, oscr, lscr) = refs[npart:]
    os_, ls_ = [], []
    for gi, (o_ref, l_ref) in enumerate(((ob0_ref, ls0_ref), (ob1_ref, ls1_ref), (ob2_ref, ls2_ref))):
        dil = B_PAIRS[gi][1]
        if dil == 1:
            os_.append(o_ref[...])
            ls_.append(l_ref[...])
        else:
            ncg = B_GW // LANES
            for r in range(dil):
                for c in range(ncg):
                    lo = r * B_GW + c * LANES
                    oscr[gi * ncg + c, pl.ds(r, tm // dil, stride=dil), :] = o_ref[:, lo:lo + LANES]
                lscr[gi, pl.ds(r, tm // dil, stride=dil), :] = l_ref[:, r * LANES:(r + 1) * LANES]
            os_.append(jnp.concatenate([oscr[gi * ncg + c] for c in range(ncg)], axis=1))
            ls_.append(lscr[gi])
    lmax = jnp.maximum(jnp.maximum(ls_[0], ls_[1]), ls_[2])
    ws = [jnp.exp(l - lmax) for l in ls_]
    winv = 1.0 / (ws[0] + ws[1] + ws[2])
    lane = lax.broadcasted_iota(jnp.int32, (tm, LANES), 1)
    lo_lanes = lane < HEAD_DIM

    def head_weights(w):
        return jnp.concatenate(
            [jnp.where(lo_lanes, jnp.broadcast_to(w[:, 2 * pr:2 * pr + 1], (tm, LANES)),
                       jnp.broadcast_to(w[:, 2 * pr + 1:2 * pr + 2], (tm, LANES)))
             for pr in range(B_HEADS // 2)], axis=1)

    ob = (head_weights(ws[0] * winv) * os_[0] + head_weights(ws[1] * winv) * os_[1]
          + head_weights(ws[2] * winv) * os_[2])
    o = (jnp.dot(oa_ref[...], wout_ref[0:A_QW, :], preferred_element_type=F32)
         + jnp.dot(ob.astype(BF16), wout_ref[A_QW:, :], preferred_element_type=F32))
    x1 = _read_part(x_refs, part_starts) + mod_ref[2:3, :] * o
    x1_ref[...] = x1
    h2 = _rms_mod(x1, n2_ref, mod_ref, 3, 4)
    h2_ref[...] = _pack_bf16_rows(h2)
    _route(h2, wr_ref, br_ref, idx_ref, gate_ref)


def _out1_kernel(sid_ref, *refs, part_starts):
    del sid_ref
    npart = len(part_starts)
    x_refs = refs[:npart]
    mod_ref, n2_ref, oc_ref, wout_ref, wr_ref, br_ref, x1_ref, h2_ref, idx_ref, gate_ref = refs[npart:]
    o = jnp.dot(oc_ref[...], wout_ref[...], preferred_element_type=F32)
    x1 = _read_part(x_refs, part_starts) + mod_ref[2:3, :] * o
    x1_ref[...] = x1
    h2 = _rms_mod(x1, n2_ref, mod_ref, 3, 4)
    h2_ref[...] = _pack_bf16_rows(h2)
    _route(h2, wr_ref, br_ref, idx_ref, gate_ref)


def _out_proj(layer, x_parts, mod, tile_sid, n2, attn, wout_bf, wr_t, br):
    tm = TOKEN_TILE
    part_starts, ntiles = _part_tiles(x_parts, tm)
    t = ntiles * tm
    row = lambda i, sid: (i, 0)
    const = lambda i, sid: (0, 0)
    in_specs = _part_specs(x_parts, part_starts, tm, D_MODEL) + [
        pl.BlockSpec((None, 6, D_MODEL), lambda i, sid: (sid[i], 0, 0)),
        pl.BlockSpec((1, D_MODEL), const)]
    scratch = []
    if layer == 0:
        in_specs.append(pl.BlockSpec((tm, A_QW), row))
        for (_, dil) in B_PAIRS:
            in_specs += [pl.BlockSpec((tm // dil, dil * B_GW), row), pl.BlockSpec((tm // dil, dil * LANES), row)]
        body = functools.partial(_out0_kernel, tm=tm, part_starts=part_starts)
        nscr = B_GROUPS * B_GW // LANES
        scratch = [pltpu.VMEM((nscr, tm, LANES), F32), pltpu.VMEM((B_GROUPS, tm, LANES), F32)]
    else:
        in_specs.append(pl.BlockSpec((tm, C_W), row))
        body = functools.partial(_out1_kernel, part_starts=part_starts)
    in_specs += [pl.BlockSpec(wout_bf.shape, const),
                 pl.BlockSpec((N_EXPERTS, D_MODEL), const),
                 pl.BlockSpec((N_EXPERTS, LANES), const)]
    gs = pltpu.PrefetchScalarGridSpec(
        num_scalar_prefetch=1, grid=(t // tm,), in_specs=in_specs,
        out_specs=[pl.BlockSpec((tm, D_MODEL), row), pl.BlockSpec((tm, PACK_W), row),
                   pl.BlockSpec((8, tm), lambda i, sid: (0, i)), pl.BlockSpec((8, tm), lambda i, sid: (0, i))],
        scratch_shapes=scratch)
    return pl.pallas_call(
        body, grid_spec=gs,
        out_shape=[jax.ShapeDtypeStruct((t, D_MODEL), F32), jax.ShapeDtypeStruct((t, PACK_W), jnp.int32),
                   jax.ShapeDtypeStruct((8, t), jnp.int32), jax.ShapeDtypeStruct((8, t), F32)],
        compiler_params=_cparams(("arbitrary",)),
        name="out_proj%d" % layer,
    )(tile_sid, *x_parts, mod, n2.reshape(1, D_MODEL), *attn, wout_bf, wr_t, br)


def _rank_kernel(idx_ref, tri_ref, rank_ref, cnt_ref, carry):
    i = pl.program_id(0)

    @pl.when(i == 0)
    def _():
        carry[...] = jnp.zeros_like(carry)

    tk = idx_ref.shape[1]
    eid = lax.broadcasted_iota(jnp.int32, (N_EXPERTS, tk), 0)
    hits = [eid == idx_ref[k:k + 1, :] for k in range(TOP_K)]
    onehot = sum(h.astype(F32) for h in hits)
    incl = jnp.dot(onehot.astype(BF16), tri_ref[...], preferred_element_type=F32)
    before = incl - onehot + carry[:, 0:1]
    rows = [jnp.sum(jnp.where(h, before, 0.0), axis=0, keepdims=True) for h in hits]
    rows.append(jnp.zeros((8 - TOP_K, tk), F32))
    rank_ref[...] = jnp.concatenate(rows, axis=0).astype(jnp.int32)
    carry[...] = carry[...] + incl[:, tk - 1:tk]
    cnt_ref[...] = carry[...].astype(jnp.int32)


def _slot_kernel(idx_ref, rank_ref, start_ref, slot_ref):
    tk = idx_ref.shape[1]
    eid = lax.broadcasted_iota(jnp.int32, (N_EXPERTS, tk), 0)
    start = start_ref[:, 0:1]
    rows = []
    for k in range(TOP_K):
        base = jnp.sum(jnp.where(eid == idx_ref[k:k + 1, :], start, 0), axis=0, keepdims=True)
        rows.append(base + rank_ref[k:k + 1, :])
    rows.append(jnp.zeros((8 - TOP_K, tk), jnp.int32))
    slot_ref[...] = jnp.concatenate(rows, axis=0)


def _route_slots(idx):
    t = idx.shape[1]
    tk = ROUTE_TILE
    tri = jnp.asarray(np.triu(np.ones((tk, tk), np.float32)), BF16)
    tok = lambda i: (0, i)
    rank, cnt = pl.pallas_call(
        _rank_kernel, grid=(t // tk,),
        in_specs=[pl.BlockSpec((8, tk), tok), pl.BlockSpec((tk, tk), lambda i: (0, 0))],
        out_specs=[pl.BlockSpec((8, tk), tok), pl.BlockSpec((N_EXPERTS, LANES), lambda i: (0, 0))],
        out_shape=[jax.ShapeDtypeStruct((8, t), jnp.int32), jax.ShapeDtypeStruct((N_EXPERTS, LANES), jnp.int32)],
        scratch_shapes=[pltpu.VMEM((N_EXPERTS, LANES), F32)],
        compiler_params=_cparams(("arbitrary",)),
        name="route_rank",
    )(idx, tri)
    counts = cnt[:, 0]
    padded = (counts + MOE_BM - 1) // MOE_BM * MOE_BM
    pad_end = jnp.cumsum(padded)
    pad_start = pad_end - padded
    nblk = (t * TOP_K) // MOE_BM + N_EXPERTS
    blk_row0 = jnp.arange(nblk, dtype=jnp.int32) * MOE_BM
    blk_expert = jnp.minimum(jnp.sum((pad_end[None, :] <= blk_row0[:, None]).astype(jnp.int32), axis=1),
                             N_EXPERTS - 1)
    nused = (pad_end[-1] // MOE_BM).astype(jnp.int32).reshape(1)
    start_b = jnp.broadcast_to(pad_start.astype(jnp.int32)[:, None], (N_EXPERTS, LANES))
    slot = pl.pallas_call(
        _slot_kernel, grid=(t // tk,),
        in_specs=[pl.BlockSpec((8, tk), tok), pl.BlockSpec((8, tk), tok),
                  pl.BlockSpec((N_EXPERTS, LANES), lambda i: (0, 0))],
        out_specs=pl.BlockSpec((8, tk), tok),
        out_shape=jax.ShapeDtypeStruct((8, t), jnp.int32),
        compiler_params=_cparams(("arbitrary",)),
        name="route_slot",
    )(idx, rank, start_b)
    return slot, blk_expert, nused


SC_CORES = 2
SC_SUBCORES = 16
SC_ROW_BUFFER_BYTES = 256 * 1024
SC_MAX_INDEX_LIST = 128


def _sc_chunks(t, row_bytes):
    chunk = min(SC_MAX_INDEX_LIST, SC_ROW_BUFFER_BYTES // row_bytes)
    nchunk = t // chunk
    per = nchunk // (SC_CORES * SC_SUBCORES)
    assert per * SC_CORES * SC_SUBCORES * chunk == t
    return chunk, nchunk, per


def _sc_slot_lists(slot, nchunk, chunk):
    return slot[:TOP_K].reshape(TOP_K, nchunk, chunk).transpose(1, 0, 2)


def _sc_dispatch(h2, slot, nslots):
    t, w = h2.shape
    chunk, nchunk, per = _sc_chunks(t, w * h2.dtype.itemsize)
    mesh = plsc.VectorSubcoreMesh(core_axis_name="c", subcore_axis_name="s")

    @functools.partial(
        pl.kernel, mesh=mesh, out_type=jax.ShapeDtypeStruct((nslots, w), h2.dtype),
        scratch_types=[pltpu.VMEM((TOP_K, chunk), jnp.int32), pltpu.VMEM((chunk, w), h2.dtype)],
        name="moe_dispatch_sc")
    def body(h_hbm, slot_hbm, xs_hbm, idx_v, rows_v):
        wid = lax.axis_index("s") * SC_CORES + lax.axis_index("c")

        @pl.loop(0, per)
        def _(j):
            c = wid * per + j
            pltpu.sync_copy(slot_hbm.at[c], idx_v)
            pltpu.sync_copy(h_hbm.at[pl.ds(c * chunk, chunk)], rows_v)
            for k in range(TOP_K):
                pltpu.sync_copy(rows_v, xs_hbm.at[idx_v.at[k]])

    return body(h2, _sc_slot_lists(slot, nchunk, chunk))


def _sc_gather(ys, slot, t):
    w = ys.shape[1]
    nbuf = 2
    chunk, nchunk, per = _sc_chunks(t, nbuf * w * ys.dtype.itemsize)
    mesh = plsc.VectorSubcoreMesh(core_axis_name="c", subcore_axis_name="s")

    @functools.partial(
        pl.kernel, mesh=mesh, out_type=jax.ShapeDtypeStruct((TOP_K, t, w), ys.dtype),
        scratch_types=[pltpu.VMEM((TOP_K, chunk), jnp.int32), pltpu.VMEM((nbuf, chunk, w), ys.dtype),
                       pltpu.SemaphoreType.DMA((nbuf,))],
        name="moe_gather_sc")
    def body(ys_hbm, slot_hbm, out_hbm, idx_v, rows_v, wsem):
        wid = lax.axis_index("s") * SC_CORES + lax.axis_index("c")

        @pl.loop(0, per)
        def _(j):
            c = wid * per + j
            pltpu.sync_copy(slot_hbm.at[c], idx_v)
            writes = []
            for k in range(TOP_K):
                b = k % nbuf
                if k >= nbuf:
                    writes[k - nbuf].wait()
                pltpu.sync_copy(ys_hbm.at[idx_v.at[k]], rows_v.at[b])
                writes.append(pltpu.async_copy(rows_v.at[b], out_hbm.at[k, pl.ds(c * chunk, chunk)], wsem.at[b]))
            for wr in writes[-nbuf:]:
                wr.wait()

    return body(ys, _sc_slot_lists(slot, nchunk, chunk))


def _ffn_kernel(be_ref, nu_ref, xs_ref, wgu_ref, bgu_ref, wd_ref, bd_ref, y_ref, wgu_bf, wd_bf):
    i = pl.program_id(0)
    prev = be_ref[jnp.maximum(i - 1, 0)]

    @pl.when(i < nu_ref[0])
    def _():
        @pl.when((i == 0) | (be_ref[i] != prev))
        def _():
            wgu_bf[...] = wgu_ref[...].astype(BF16)
            wd_bf[...] = wd_ref[...].astype(BF16)

        x = _unpack_bf16_rows(xs_ref[...])
        gu = jnp.dot(x, wgu_bf[...], preferred_element_type=F32) + bgu_ref[...]
        gate = jnp.minimum(gu[:, :D_FF], SWIGLU_LIMIT)
        up = jnp.clip(gu[:, D_FF:], -SWIGLU_LIMIT, SWIGLU_LIMIT)
        act = (up + 1.0) * gate * jax.nn.sigmoid(SWIGLU_ALPHA * gate)
        y = jnp.dot(act.astype(BF16), wd_bf[...], preferred_element_type=F32) + bd_ref[...]
        y_ref[...] = _pack_bf16_rows(y)

    @pl.when(i >= nu_ref[0])
    def _():
        y_ref[...] = jnp.zeros_like(y_ref)


def _expert_ffn(xs, blk_expert, nused, wgu, bgu, wd, bd):
    nslots = xs.shape[0]
    nblk = nslots // MOE_BM
    blk = lambda i, be, nu: (jnp.minimum(i, nu[0] - 1), 0)
    exp3 = lambda i, be, nu: (be[jnp.minimum(i, nu[0] - 1)], 0, 0)
    gs = pltpu.PrefetchScalarGridSpec(
        num_scalar_prefetch=2, grid=(nblk,),
        in_specs=[pl.BlockSpec((MOE_BM, PACK_W), blk),
                  pl.BlockSpec((None, D_MODEL, 2 * D_FF), exp3),
                  pl.BlockSpec((None, 1, 2 * D_FF), exp3),
                  pl.BlockSpec((None, D_FF, D_MODEL), exp3),
                  pl.BlockSpec((None, 1, D_MODEL), exp3)],
        out_specs=pl.BlockSpec((MOE_BM, PACK_W), lambda i, be, nu: (i, 0)),
        scratch_shapes=[pltpu.VMEM((D_MODEL, 2 * D_FF), BF16), pltpu.VMEM((D_FF, D_MODEL), BF16)])
    return pl.pallas_call(
        _ffn_kernel, grid_spec=gs,
        out_shape=jax.ShapeDtypeStruct((nslots, PACK_W), jnp.int32),
        compiler_params=_cparams(("arbitrary",)),
        name="moe_ffn",
    )(blk_expert, nused, xs, wgu, bgu.reshape(N_EXPERTS, 1, 2 * D_FF), wd, bd.reshape(N_EXPERTS, 1, D_MODEL))


def _combine_kernel(sid_ref, x_ref, mod_ref, gate_ref, yk_ref, *o_refs, tm, out_starts):
    del sid_ref
    gpad = jnp.concatenate([gate_ref[...], jnp.zeros((LANES - 8, tm), F32)], axis=0)
    gcol = gpad.T
    acc = gcol[:, 0:1] * _unpack_bf16_rows(yk_ref[0]).astype(F32)
    for k in range(1, TOP_K):
        acc = acc + gcol[:, k:k + 1] * _unpack_bf16_rows(yk_ref[k]).astype(F32)
    out = x_ref[...] + mod_ref[5:6, :] * acc
    i = pl.program_id(0)
    ends = out_starts[1:] + (None,)
    for o_ref, s0, s1 in zip(o_refs, out_starts, ends):
        mine = (i >= s0) if s1 is None else ((i >= s0) & (i < s1))

        @pl.when(mine)
        def _():
            o_ref[...] = out


def _combine(x1, mod, tile_sid, gates, yk, out_rows):
    t = x1.shape[0]
    tm = MOVE_TILE
    out_shape = [jax.ShapeDtypeStruct((n, D_MODEL), F32) for n in out_rows]
    out_starts, ntiles = _part_tiles(out_shape, tm)
    assert ntiles * tm == t
    gs = pltpu.PrefetchScalarGridSpec(
        num_scalar_prefetch=1, grid=(t // tm,),
        in_specs=[pl.BlockSpec((tm, D_MODEL), lambda i, sid: (i, 0)),
                  pl.BlockSpec((None, 6, D_MODEL), lambda i, sid: (sid[i], 0, 0)),
                  pl.BlockSpec((8, tm), lambda i, sid: (0, i)),
                  pl.BlockSpec((TOP_K, tm, PACK_W), lambda i, sid: (0, i, 0))],
        out_specs=_part_specs(out_shape, out_starts, tm, D_MODEL))
    return pl.pallas_call(
        functools.partial(_combine_kernel, tm=tm, out_starts=out_starts), grid_spec=gs,
        out_shape=out_shape,
        compiler_params=_cparams(("arbitrary",)),
        name="moe_combine",
    )(tile_sid, x1, mod, gates, yk)


def _moe(x1, h2, idx, gates, mod, tile_sid_move, wgu, bgu, wd, bd, out_rows):
    t = x1.shape[0]
    nslots = t * TOP_K + N_EXPERTS * MOE_BM
    slot, blk_expert, nused = _route_slots(idx)
    xs = _sc_dispatch(h2, slot, nslots)
    ys = _expert_ffn(xs, blk_expert, nused, wgu, bgu, wd, bd)
    return _combine(x1, mod, tile_sid_move, gates, _sc_gather(ys, slot, t), out_rows)


def _tile_table(seq_lens, tile, fn):
    vals = []
    start = 0
    for sid, n in enumerate(seq_lens):
        assert n % tile == 0
        for j in range(n // tile):
            vals.append(fn(sid, start, n, j))
        start += n
    return jnp.asarray(np.asarray(vals, np.int32))


def _head_block_diag():
    bd = np.kron(np.eye(2 * LANES // HEAD_DIM, dtype=np.float32),
                 np.full((HEAD_DIM, HEAD_DIM), 1.0 / HEAD_DIM, np.float32))
    return jnp.asarray(bd, BF16)


_A_HEAD_ORDER = [kv * A_GROUP + g for g in range(A_GROUP) for kv in range(A_KV_HEADS)]


def _trunk(x_parts, c_pad, seq_lens, p):
    t = sum(a.shape[0] for a in x_parts)
    sid_tok = _tile_table(seq_lens, TOKEN_TILE, lambda sid, s, n, j: sid)
    sid_move = _tile_table(seq_lens, MOVE_TILE, lambda sid, s, n, j: sid)
    bd = _head_block_diag()

    mod0 = _ada_mod(c_pad, p["l0_ada_w"], p["l0_ada_b"]).reshape(c_pad.shape[0], 6, D_MODEL)
    w_in = p["l0_w_in"]
    a_cols = np.concatenate([np.arange(h * HEAD_DIM, (h + 1) * HEAD_DIM) for h in _A_HEAD_ORDER])
    bq, bk, bv = A_IN, A_IN + B_GROUPS * B_GW, A_IN + 2 * B_GROUPS * B_GW
    col_blocks = [w_in[:, a_cols], w_in[:, A_QW:A_IN]]
    for g in range(B_GROUPS):
        col_blocks += [w_in[:, bq + g * B_GW:bq + (g + 1) * B_GW],
                       w_in[:, bk + g * B_GW:bk + (g + 1) * B_GW],
                       w_in[:, bv + g * B_GW:bv + (g + 1) * B_GW]]
    w0 = jnp.concatenate(col_blocks, axis=1).astype(BF16)
    ones = lambda n: jnp.ones((n,), F32)
    gain_blocks = [jnp.tile(p["l0_q_norm_a"], A_Q_HEADS) * QK_SCALE, jnp.tile(p["l0_k_norm_a"], A_KV_HEADS),
                   ones(A_KVW)]
    for g in range(B_GROUPS):
        gain_blocks += [jnp.tile(p["l0_q_norm_b"], B_HEADS) * QK_SCALE, jnp.tile(p["l0_k_norm_b"], B_HEADS),
                        ones(B_GW)]
    gain0 = jnp.concatenate(gain_blocks).reshape(1, -1)
    slabs0 = [(0, A_QW, True, 1), (A_QW, A_KVW, True, 1), (A_QW + A_KVW, A_KVW, False, 1)]
    c0 = A_IN
    for g, (_, dil) in enumerate(B_PAIRS):
        slabs0 += [(c0, B_GW, True, dil), (c0 + B_GW, B_GW, True, dil), (c0 + 2 * B_GW, B_GW, False, dil)]
        c0 += 3 * B_GW
    proj = _in_proj(x_parts, mod0, sid_tok, p["l0_norm1"], w0, gain0, bd, tuple(slabs0))
    qa, ka, va = proj[0:3]

    slopes_a = _alibi_slopes(A_Q_HEADS)
    heads_a = tuple(tuple((slopes_a[kv * A_GROUP + g], kv * A_GROUP + g) for kv in range(A_KV_HEADS))
                    for g in range(A_GROUP))
    (oa,) = _band_attn(qa, ka, va, seq_lens, tq=ATTN_TQ_A, dil=1, qw=A_QW, kvw=A_KVW, half=A_HALF, heads=heads_a,
                       kv_group=(0,) * A_GROUP, sink=p["l0_sink_a"].astype(F32), out_dtype=BF16, want_lse=False)

    slopes_b = _alibi_slopes(B_GROUPS * B_HEADS)
    attn0 = [oa]
    for g, (window, dil) in enumerate(B_PAIRS):
        qg, kg, vg = proj[3 + 3 * g:6 + 3 * g]
        heads_b = tuple(tuple((slopes_b[g * B_HEADS + 2 * pr + s] * dil, 0) for s in range(2))
                        for pr in range(B_HEADS // 2))
        og, lg = _band_attn(qg, kg, vg, [n // dil for n in seq_lens], tq=ATTN_TQ_B, dil=dil, qw=B_GW, kvw=B_GW, half=window // (2 * dil),
                            heads=heads_b, kv_group=tuple(range(B_HEADS // 2)), sink=None, out_dtype=F32,
                            want_lse=True)
        attn0 += [og, lg]

    w_out = p["l0_w_out"]
    wout0 = jnp.concatenate([w_out[a_cols], w_out[A_QW:]], axis=0).astype(BF16)
    br0 = jnp.broadcast_to(p["l0_b_router"].astype(F32)[:, None], (N_EXPERTS, LANES))
    x1, h2, idx, gates = _out_proj(0, x_parts, mod0, sid_tok, p["l0_norm2"], attn0, wout0, p["l0_w_router"].T,
                                   br0)
    x = _moe(x1, h2, idx, gates, mod0, sid_move, p["l0_w_gate_up"], p["l0_b_gate_up"], p["l0_w_down"],
             p["l0_b_down"], [t])

    mod1 = _ada_mod(c_pad, p["l1_ada_w"], p["l1_ada_b"]).reshape(c_pad.shape[0], 6, D_MODEL)
    w1 = p["l1_w_in"].astype(BF16)
    gain1 = jnp.concatenate([jnp.tile(p["l1_q_norm_c"], C_HEADS) * QK_SCALE, jnp.tile(p["l1_k_norm_c"], C_HEADS),
                             ones(C_W)]).reshape(1, -1)
    slabs1 = ((0, C_W, True, 1), (C_W, C_W, True, 1), (2 * C_W, C_W, False, 1))
    qc, kc, vc = _in_proj(x, mod1, sid_tok, p["l1_norm1"], w1, gain1, bd, slabs1)
    na_tile = NA_TROWS * GRID_W
    tile_r0 = _tile_table(seq_lens, na_tile, lambda sid, s, n, j: j * NA_TROWS)
    tile_nr = _tile_table(seq_lens, na_tile, lambda sid, s, n, j: n // GRID_W)
    oc = _na_attn(qc, kc, vc, _na_bias_table(p["l1_rpb_c"]), tile_r0, tile_nr)
    br1 = jnp.broadcast_to(p["l1_b_router"].astype(F32)[:, None], (N_EXPERTS, LANES))
    x1, h2, idx, gates = _out_proj(1, x, mod1, sid_tok, p["l1_norm2"], [oc], p["l1_w_out"].astype(BF16),
                                   p["l1_w_router"].T, br1)
    return _moe(x1, h2, idx, gates, mod1, sid_move, p["l1_w_gate_up"], p["l1_b_gate_up"], p["l1_w_down"],
                p["l1_b_down"], [a.shape[0] for a in x_parts])


def kernel(x_prompt, x_sample, c_prompt, c_sample, l0_ada_w, l0_ada_b, l0_norm1, l0_w_in, l0_q_norm_a, l0_k_norm_a, l0_sink_a, l0_q_norm_b, l0_k_norm_b, l0_w_out, l0_norm2, l0_w_router, l0_b_router, l0_w_gate_up, l0_b_gate_up, l0_w_down, l0_b_down, l1_ada_w, l1_ada_b, l1_norm1, l1_w_in, l1_q_norm_c, l1_k_norm_c, l1_rpb_c, l1_w_out, l1_norm2, l1_w_router, l1_b_router, l1_w_gate_up, l1_b_gate_up, l1_w_down, l1_b_down):
    p = dict(l0_ada_w=l0_ada_w, l0_ada_b=l0_ada_b, l0_norm1=l0_norm1, l0_w_in=l0_w_in, l0_q_norm_a=l0_q_norm_a,
             l0_k_norm_a=l0_k_norm_a, l0_sink_a=l0_sink_a, l0_q_norm_b=l0_q_norm_b, l0_k_norm_b=l0_k_norm_b,
             l0_w_out=l0_w_out, l0_norm2=l0_norm2, l0_w_router=l0_w_router, l0_b_router=l0_b_router,
             l0_w_gate_up=l0_w_gate_up, l0_b_gate_up=l0_b_gate_up, l0_w_down=l0_w_down, l0_b_down=l0_b_down,
             l1_ada_w=l1_ada_w, l1_ada_b=l1_ada_b, l1_norm1=l1_norm1, l1_w_in=l1_w_in, l1_q_norm_c=l1_q_norm_c,
             l1_k_norm_c=l1_k_norm_c, l1_rpb_c=l1_rpb_c, l1_w_out=l1_w_out, l1_norm2=l1_norm2,
             l1_w_router=l1_w_router, l1_b_router=l1_b_router, l1_w_gate_up=l1_w_gate_up,
             l1_b_gate_up=l1_b_gate_up, l1_w_down=l1_w_down, l1_b_down=l1_b_down)
    nb_p, len_p, d = x_prompt.shape
    nb_s, len_s, _ = x_sample.shape
    seq_lens = (len_p,) * nb_p + (len_s,) * nb_s
    x_parts = [x_prompt.reshape(nb_p * len_p, d), x_sample.reshape(nb_s * len_s, d)]
    nseq = nb_p + nb_s
    c_pad = jnp.concatenate([c_prompt, c_sample, jnp.zeros((-nseq % 8, d), F32)], axis=0)
    y_p, y_s = _trunk(x_parts, c_pad, seq_lens, p)
    return (y_p.reshape(nb_p, len_p, d), y_s.reshape(nb_s, len_s, d))
```

```python
import functools
import math

import numpy as np
import jax
import jax.numpy as jnp
from jax import lax
from jax.experimental import pallas as pl
from jax.experimental.pallas import tpu as pltpu
from jax.experimental.pallas import tpu_sc as plsc

F32 = jnp.float32
BF16 = jnp.bfloat16
HIGHEST = lax.Precision.HIGHEST

D_MODEL = 1024
HEAD_DIM = 64
LANES = 128
GRID_W = 64
A_Q_HEADS = 8
A_KV_HEADS = 2
A_GROUP = A_Q_HEADS // A_KV_HEADS
A_HALF = 128
B_PAIRS = ((128, 1), (512, 4), (2048, 16))
B_GROUPS = 3
B_HEADS = 4
A_QW = A_Q_HEADS * HEAD_DIM
A_KVW = A_KV_HEADS * HEAD_DIM
A_IN = A_QW + 2 * A_KVW
B_GW = B_HEADS * HEAD_DIM
C_HEADS = 16
C_W = C_HEADS * HEAD_DIM
NA_KH = 8
NA_KW = 16
N_EXPERTS = 32
TOP_K = 4
D_FF = 1024
SWIGLU_LIMIT = 7.0
SWIGLU_ALPHA = 1.702
RMS_EPS = 1e-6
NEG_INF = -1e30
QK_SCALE = HEAD_DIM ** -0.5

TOKEN_TILE = 1024
ATTN_TQ_A = 256
ATTN_TQ_B = 512
ATTN_UQ = 128
NA_TROWS = 8
MOE_BM = 512
ROUTE_TILE = 1024
MOVE_TILE = 256
VMEM_LIMIT = 56 * 1024 * 1024


def _alibi_slopes(n):
    return [float(2.0 ** (-8.0 * (j + 1) / n)) for j in range(n)]


def _cparams(sem, flags=None):
    return pltpu.CompilerParams(dimension_semantics=sem, vmem_limit_bytes=VMEM_LIMIT, flags=flags)


def _mod_kernel(c_ref, w_ref, b_ref, o_ref):
    c = c_ref[...]
    s = c * jax.nn.sigmoid(c)
    o_ref[...] = jnp.dot(s, w_ref[...], precision=HIGHEST, preferred_element_type=F32) + b_ref[...]


def _ada_mod(c_pad, w, b):
    nrow = c_pad.shape[0]
    ncol = w.shape[1]
    tn = D_MODEL
    return pl.pallas_call(
        _mod_kernel,
        grid=(ncol // tn,),
        in_specs=[pl.BlockSpec((nrow, D_MODEL), lambda j: (0, 0)),
                  pl.BlockSpec((D_MODEL, tn), lambda j: (0, j)),
                  pl.BlockSpec((1, tn), lambda j: (0, j))],
        out_specs=pl.BlockSpec((nrow, tn), lambda j: (0, j)),
        out_shape=jax.ShapeDtypeStruct((nrow, ncol), F32),
        compiler_params=_cparams(("arbitrary",)),
        name="ada_mod",
    )(c_pad, w, b.reshape(1, ncol))


def _head_mean_sq(y, bd_ref):
    w = y.shape[1]
    outs = []
    for c0 in range(0, w, 2 * LANES):
        cw = min(2 * LANES, w - c0)
        sq = y[:, c0:c0 + cw] * y[:, c0:c0 + cw]
        outs.append(jnp.dot(sq.astype(BF16), bd_ref[0:cw, 0:cw], preferred_element_type=F32))
    return outs[0] if len(outs) == 1 else jnp.concatenate(outs, axis=1)


def _rms_mod(x, g_ref, mod_ref, shift_row, scale_row):
    ms = jnp.mean(x * x, axis=-1, keepdims=True)
    xn = x * lax.rsqrt(ms + RMS_EPS) * g_ref[...]
    return xn * (1.0 + mod_ref[scale_row:scale_row + 1, :]) + mod_ref[shift_row:shift_row + 1, :]


def _part_tiles(parts, tile):
    starts, s = [], 0
    for a in parts:
        assert a.shape[0] % tile == 0
        starts.append(s)
        s += a.shape[0] // tile
    return tuple(starts), s


def _part_specs(parts, starts, tile, ncol):
    def spec(a, s0):
        n = a.shape[0] // tile
        return pl.BlockSpec((tile, ncol), lambda i, *_: (jnp.clip(i - s0, 0, n - 1), 0))
    return [spec(a, s0) for a, s0 in zip(parts, starts)]


def _read_part(refs, starts):
    i = pl.program_id(0)
    x = refs[0][...]
    for ref, s0 in zip(refs[1:], starts[1:]):
        x = jnp.where(i >= s0, ref[...], x)
    return x


def _in_kernel(sid_ref, *refs, slabs, tm, part_starts):
    del sid_ref
    npart = len(part_starts)
    x_refs = refs[:npart]
    mod_ref, n1_ref, w_ref, gain_ref, bd_ref = refs[npart:npart + 5]
    rest = refs[npart + 5:]
    nout = len(slabs)
    out_refs = rest[:nout]
    scr_ref = rest[nout] if len(rest) > nout else None
    h = _rms_mod(_read_part(x_refs, part_starts), n1_ref, mod_ref, 0, 1).astype(BF16)
    proj = jnp.dot(h, w_ref[...], preferred_element_type=F32)
    for (c0, w, normed, dil), o_ref in zip(slabs, out_refs):
        y = proj[:, c0:c0 + w]
        if normed:
            y = y * lax.rsqrt(_head_mean_sq(y, bd_ref) + RMS_EPS) * gain_ref[:, c0:c0 + w]
        if dil == 1:
            o_ref[...] = y.astype(BF16)
        else:
            for c in range(w // LANES):
                scr_ref[c] = y[:, c * LANES:(c + 1) * LANES]
            for r in range(dil):
                for c in range(w // LANES):
                    o_ref[:, r * w + c * LANES:r * w + (c + 1) * LANES] = (
                        scr_ref[c, pl.ds(r, tm // dil, stride=dil), :].astype(BF16))


def _in_proj(x_parts, mod, tile_sid, n1, w_bf, gain, bd, slabs):
    tm = TOKEN_TILE
    part_starts, ntiles = _part_tiles(x_parts, tm)
    t = ntiles * tm
    ncols = w_bf.shape[1]
    out_shapes, out_specs = [], []
    for (c0, w, normed, dil) in slabs:
        out_shapes.append(jax.ShapeDtypeStruct((t // dil, dil * w), BF16))
        out_specs.append(pl.BlockSpec((tm // dil, dil * w), lambda i, sid: (i, 0)))
    need_scr = any(s[3] > 1 for s in slabs)
    scratch = [pltpu.VMEM((max(s[1] for s in slabs if s[3] > 1) // LANES, tm, LANES), F32)] if need_scr else []
    gs = pltpu.PrefetchScalarGridSpec(
        num_scalar_prefetch=1,
        grid=(t // tm,),
        in_specs=_part_specs(x_parts, part_starts, tm, D_MODEL) + [
            pl.BlockSpec((None, 6, D_MODEL), lambda i, sid: (sid[i], 0, 0)),
            pl.BlockSpec((1, D_MODEL), lambda i, sid: (0, 0)),
            pl.BlockSpec((D_MODEL, ncols), lambda i, sid: (0, 0)),
            pl.BlockSpec((1, ncols), lambda i, sid: (0, 0)),
            pl.BlockSpec((2 * LANES, 2 * LANES), lambda i, sid: (0, 0))],
        out_specs=out_specs,
        scratch_shapes=scratch)
    return pl.pallas_call(
        functools.partial(_in_kernel, slabs=slabs, tm=tm, part_starts=part_starts),
        grid_spec=gs,
        out_shape=out_shapes,
        compiler_params=_cparams(("arbitrary",)),
        name="in_proj",
    )(tile_sid, *x_parts, mod, n1.reshape(1, D_MODEL), w_bf, gain, bd)


def _band_kernel(s0_ref, s1_ref, *refs, tq, hb, half, heads, kv_group, has_sink, want_lse):
    pos = 0
    sink_ref = None
    if has_sink:
        sink_ref = refs[0]
        pos = 1
    q_ref, kp_ref, kc_ref, kn_ref, vp_ref, vc_ref, vn_ref = refs[pos:pos + 7]
    o_ref = refs[pos + 7]
    lse_ref = refs[pos + 8] if want_lse else None
    i = pl.program_id(1)
    uq = ATTN_UQ
    wlen = uq + 2 * hb
    kall = jnp.concatenate([kp_ref[...], kc_ref[...], kn_ref[...]], axis=0)
    vall = jnp.concatenate([vp_ref[...], vc_ref[...], vn_ref[...]], axis=0)
    lane = lax.broadcasted_iota(jnp.int32, (uq, LANES), 1)
    lo_lanes = lane < HEAD_DIM
    for sub in range(tq // uq):
        rows = slice(sub * uq, (sub + 1) * uq)
        kwin = kall[sub * uq:sub * uq + wlen]
        vwin = vall[sub * uq:sub * uq + wlen]
        q0 = i * tq + sub * uq
        kpos = q0 - hb + lax.broadcasted_iota(jnp.int32, (wlen, uq), 0)
        qpos = q0 + lax.broadcasted_iota(jnp.int32, (wlen, uq), 1)
        rel = kpos - qpos
        valid = (jnp.abs(rel) <= half) & (kpos >= s0_ref[i]) & (kpos < s1_ref[i])
        distm = jnp.where(valid, jnp.abs(rel).astype(F32), 1e32)
        lse_rows = []
        for g, pair in enumerate(heads):
            qg = q_ref[rows, g * LANES:(g + 1) * LANES]
            kg = kwin[:, kv_group[g] * LANES:(kv_group[g] + 1) * LANES]
            vg = vwin[:, kv_group[g] * LANES:(kv_group[g] + 1) * LANES]
            outs = []
            for s, (slope, sink_idx) in enumerate(pair):
                keep = lo_lanes if s == 0 else jnp.logical_not(lo_lanes)
                qm = jnp.where(keep, qg, jnp.zeros_like(qg))
                sc = lax.dot_general(kg, qm, (((1,), (1,)), ((), ())), preferred_element_type=F32)
                sc = sc - slope * distm
                m = jnp.max(sc, axis=0, keepdims=True)
                if has_sink:
                    m = jnp.maximum(m, sink_ref[sink_idx])
                e = jnp.exp(sc - m)
                den = jnp.sum(e, axis=0, keepdims=True)
                if has_sink:
                    den = den + jnp.exp(sink_ref[sink_idx] - m)
                p = (e * (1.0 / den)).astype(BF16)
                outs.append(lax.dot_general(p, vg, (((0,), (0,)), ((), ())), preferred_element_type=F32))
                if want_lse:
                    lse_rows.append(m + jnp.log(den))
            o_ref[rows, g * LANES:(g + 1) * LANES] = jnp.where(lo_lanes, outs[0], outs[1]).astype(o_ref.dtype)
        if want_lse:
            lse_t = jnp.concatenate(lse_rows + [jnp.zeros((LANES - len(lse_rows), uq), F32)], axis=0)
            lse_ref[rows, :] = lse_t.T


def _band_attn(q, k, v, seq_rows, *, tq, dil, qw, kvw, half, heads, kv_group, sink, out_dtype, want_lse):
    rows = q.shape[0]
    hb = half
    s0 = _tile_table(seq_rows, tq, lambda sid, s, n, j: s)
    s1 = _tile_table(seq_rows, tq, lambda sid, s, n, j: s + n)
    per = tq // hb
    nt = rows // tq
    nhb = rows // hb
    has_sink = sink is not None
    in_specs = []
    args = []
    if has_sink:
        in_specs.append(pl.BlockSpec(memory_space=pltpu.SMEM))
        args.append(sink)
    qmap = lambda r, i, a, b: (i, r)
    pmap = lambda r, i, a, b: (jnp.maximum(i * per - 1, 0), r)
    nmap = lambda r, i, a, b: (jnp.minimum((i + 1) * per, nhb - 1), r)
    in_specs += [pl.BlockSpec((tq, qw), qmap),
                 pl.BlockSpec((hb, kvw), pmap), pl.BlockSpec((tq, kvw), qmap), pl.BlockSpec((hb, kvw), nmap),
                 pl.BlockSpec((hb, kvw), pmap), pl.BlockSpec((tq, kvw), qmap), pl.BlockSpec((hb, kvw), nmap)]
    args += [q, k, k, k, v, v, v]
    out_shape = [jax.ShapeDtypeStruct((rows, dil * qw), out_dtype)]
    out_specs = [pl.BlockSpec((tq, qw), qmap)]
    if want_lse:
        out_shape.append(jax.ShapeDtypeStruct((rows, dil * LANES), F32))
        out_specs.append(pl.BlockSpec((tq, LANES), qmap))
    gs = pltpu.PrefetchScalarGridSpec(num_scalar_prefetch=2, grid=(dil, nt),
                                      in_specs=in_specs, out_specs=out_specs)
    res = pl.pallas_call(
        functools.partial(_band_kernel, tq=tq, hb=hb, half=half, heads=heads, kv_group=kv_group,
                          has_sink=has_sink, want_lse=want_lse),
        grid_spec=gs,
        out_shape=out_shape,
        compiler_params=_cparams(("arbitrary", "arbitrary")),
        name="band_attn_d%d" % dil,
    )(s0, s1, *args)
    return res


def _na_kernel(r0_ref, nr_ref, q_ref, kp_ref, kc_ref, kn_ref, vp_ref, vc_ref, vn_ref, bias_ref,
               o_ref, kcat, vcat, *, halo):
    j = pl.program_id(0)
    tq = NA_TROWS * GRID_W
    hrows = halo * GRID_W
    kcat[0:hrows, :] = kp_ref[...]
    kcat[hrows:hrows + tq, :] = kc_ref[...]
    kcat[hrows + tq:, :] = kn_ref[...]
    vcat[0:hrows, :] = vp_ref[...]
    vcat[hrows:hrows + tq, :] = vc_ref[...]
    vcat[hrows + tq:, :] = vn_ref[...]
    r0 = r0_ref[j]
    nrows = nr_ref[j]
    kwin_len = NA_KH * GRID_W
    lane = lax.broadcasted_iota(jnp.int32, (GRID_W, LANES), 1)
    lo_lanes = lane < HEAD_DIM
    nrel = 2 * NA_KH - 1

    def row_body(u, carry):
        r = r0 + u
        rs = jnp.clip(r - NA_KH // 2, 0, nrows - NA_KH)
        off = pl.multiple_of((rs - r0 + halo) * GRID_W, GRID_W)
        bvar = rs - r + NA_KH - 1
        qrow = pl.multiple_of(u * GRID_W, GRID_W)
        for g in range(C_HEADS // 2):
            qg = q_ref[pl.ds(qrow, GRID_W), g * LANES:(g + 1) * LANES]
            qs = jnp.concatenate([jnp.where(lo_lanes, qg, jnp.zeros_like(qg)),
                                  jnp.where(lo_lanes, jnp.zeros_like(qg), qg)], axis=0)
            kg = kcat[pl.ds(off, kwin_len), g * LANES:(g + 1) * LANES]
            vg = vcat[pl.ds(off, kwin_len), g * LANES:(g + 1) * LANES]
            sc = lax.dot_general(kg, qs, (((1,), (1,)), ((), ())), preferred_element_type=F32)
            sc = sc + jnp.concatenate([bias_ref[g * nrel + bvar + kr] for kr in range(NA_KH)], axis=0)
            m = jnp.max(sc, axis=0, keepdims=True)
            e = jnp.exp(sc - m)
            den = jnp.sum(e, axis=0, keepdims=True)
            p = (e * (1.0 / den)).astype(BF16)
            pv = lax.dot_general(p, vg, (((0,), (0,)), ((), ())), preferred_element_type=F32)
            o = jnp.where(lo_lanes, pv[0:GRID_W], pv[GRID_W:])
            o_ref[pl.ds(qrow, GRID_W), g * LANES:(g + 1) * LANES] = o.astype(o_ref.dtype)
        return carry

    lax.fori_loop(0, NA_TROWS, row_body, 0)


def _na_attn(q, k, v, bias_tab, tile_r0, tile_nr):
    t = q.shape[0]
    halo = NA_KH // 2
    tq = NA_TROWS * GRID_W
    hrows = halo * GRID_W
    per = tq // hrows
    nhb = t // hrows
    nt = t // tq
    qmap = lambda j, a, b: (j, 0)
    pmap = lambda j, a, b: (jnp.maximum(j * per - 1, 0), 0)
    nmap = lambda j, a, b: (jnp.minimum((j + 1) * per, nhb - 1), 0)
    gs = pltpu.PrefetchScalarGridSpec(
        num_scalar_prefetch=2, grid=(nt,),
        in_specs=[pl.BlockSpec((tq, C_W), qmap),
                  pl.BlockSpec((hrows, C_W), pmap), pl.BlockSpec((tq, C_W), qmap), pl.BlockSpec((hrows, C_W), nmap),
                  pl.BlockSpec((hrows, C_W), pmap), pl.BlockSpec((tq, C_W), qmap), pl.BlockSpec((hrows, C_W), nmap),
                  pl.BlockSpec(bias_tab.shape, lambda j, a, b: (0, 0, 0))],
        out_specs=pl.BlockSpec((tq, C_W), qmap),
        scratch_shapes=[pltpu.VMEM((tq + 2 * hrows, C_W), BF16), pltpu.VMEM((tq + 2 * hrows, C_W), BF16)])
    return pl.pallas_call(
        functools.partial(_na_kernel, halo=halo),
        grid_spec=gs,
        out_shape=jax.ShapeDtypeStruct((t, C_W), BF16),
        compiler_params=_cparams(("arbitrary",)),
        name="na_attn",
    )(tile_r0, tile_nr, q, k, k, k, v, v, v, bias_tab)


def _na_bias_table(rpb):
    c = np.arange(GRID_W)
    cs = np.clip(c - NA_KW // 2, 0, GRID_W - NA_KW)
    kc = np.arange(GRID_W)
    valid = (kc[:, None] >= cs[None, :]) & (kc[:, None] < cs[None, :] + NA_KW)
    cidx = np.clip(kc[:, None] - c[None, :] + NA_KW - 1, 0, 2 * NA_KW - 2)
    tab = rpb.astype(F32)[:, :, cidx]
    tab = jnp.where(jnp.asarray(valid)[None, None], tab, NEG_INF)
    pairs = jnp.concatenate([tab[0::2], tab[1::2]], axis=-1)
    return pairs.reshape((C_HEADS // 2) * (2 * NA_KH - 1), GRID_W, LANES)


def _route(h2, wr_ref, br_ref, idx_ref, gate_ref):
    tm = h2.shape[0]
    logits = lax.dot_general(wr_ref[...], h2, (((1,), (1,)), ((), ())),
                             precision=HIGHEST, preferred_element_type=F32) + br_ref[:, 0:1]
    eid = lax.broadcasted_iota(jnp.int32, (N_EXPERTS, tm), 0)
    vals, idxs = [], []
    for _ in range(TOP_K):
        m = jnp.max(logits, axis=0, keepdims=True)
        ix = jnp.min(jnp.where(logits == m, eid, N_EXPERTS), axis=0, keepdims=True)
        vals.append(m)
        idxs.append(ix)
        logits = jnp.where(eid == ix, -jnp.inf, logits)
    es = [jnp.exp(vk - vals[0]) for vk in vals]
    den = es[0] + es[1] + es[2] + es[3]
    pad_i = jnp.zeros((8 - TOP_K, tm), jnp.int32)
    pad_f = jnp.zeros((8 - TOP_K, tm), F32)
    idx_ref[...] = jnp.concatenate(idxs + [pad_i], axis=0)
    gate_ref[...] = jnp.concatenate([ek / den for ek in es] + [pad_f], axis=0)


PACK_W = D_MODEL // 2


def _pack_bf16_rows(h):
    bits = pltpu.bitcast(h.astype(BF16).astype(F32), jnp.int32)
    return (bits[:, :PACK_W] & jnp.int32(-65536)) | lax.shift_right_logical(bits[:, PACK_W:], 16)


def _unpack_bf16_rows(w):
    hi = pltpu.bitcast(w & jnp.int32(-65536), F32).astype(BF16)
    lo = pltpu.bitcast(lax.shift_left(w, 16), F32).astype(BF16)
    return jnp.concatenate([hi, lo], axis=1)


def _out0_kernel(sid_ref, *refs, tm, part_starts):
    del sid_ref
    npart = len(part_starts)
    x_refs = refs[:npart]
    (mod_ref, n2_ref, oa_ref, ob0_ref, ls0_ref, ob1_ref, ls1_ref, ob2_ref, ls2_ref,
     wout_ref, wr_ref, br_ref, x1_ref, h2_ref, idx_ref, gate_ref, oscr, lscr) = refs[npart:]
    os_, ls_ = [], []
    for gi, (o_ref, l_ref) in enumerate(((ob0_ref, ls0_ref), (ob1_ref, ls1_ref), (ob2_ref, ls2_ref))):
        dil = B_PAIRS[gi][1]
        if dil == 1:
            os_.append(o_ref[...])
            ls_.append(l_ref[...])
        else:
            ncg = B_GW // LANES
            for r in range(dil):
                for c in range(ncg):
                    lo = r * B_GW + c * LANES
                    oscr[gi * ncg + c, pl.ds(r, tm // dil, stride=dil), :] = o_ref[:, lo:lo + LANES]
                lscr[gi, pl.ds(r, tm // dil, stride=dil), :] = l_ref[:, r * LANES:(r + 1) * LANES]
            os_.append(jnp.concatenate([oscr[gi * ncg + c] for c in range(ncg)], axis=1))
            ls_.append(lscr[gi])
    lmax = jnp.maximum(jnp.maximum(ls_[0], ls_[1]), ls_[2])
    ws = [jnp.exp(l - lmax) for l in ls_]
    winv = 1.0 / (ws[0] + ws[1] + ws[2])
    lane = lax.broadcasted_iota(jnp.int32, (tm, LANES), 1)
    lo_lanes = lane < HEAD_DIM

    def head_weights(w):
        return jnp.concatenate(
            [jnp.where(lo_lanes, jnp.broadcast_to(w[:, 2 * pr:2 * pr + 1], (tm, LANES)),
                       jnp.broadcast_to(w[:, 2 * pr + 1:2 * pr + 2], (tm, LANES)))
             for pr in range(B_HEADS // 2)], axis=1)

    ob = (head_weights(ws[0] * winv) * os_[0] + head_weights(ws[1] * winv) * os_[1]
          + head_weights(ws[2] * winv) * os_[2])
    o = (jnp.dot(oa_ref[...], wout_ref[0:A_QW, :], preferred_element_type=F32)
         + jnp.dot(ob.astype(BF16), wout_ref[A_QW:, :], preferred_element_type=F32))
    x1 = _read_part(x_refs, part_starts) + mod_ref[2:3, :] * o
    x1_ref[...] = x1
    h2 = _rms_mod(x1, n2_ref, mod_ref, 3, 4)
    h2_ref[...] = _pack_bf16_rows(h2)
    _route(h2, wr_ref, br_ref, idx_ref, gate_ref)


def _out1_kernel(sid_ref, *refs, part_starts):
    del sid_ref
    npart = len(part_starts)
    x_refs = refs[:npart]
    mod_ref, n2_ref, oc_ref, wout_ref, wr_ref, br_ref, x1_ref, h2_ref, idx_ref, gate_ref = refs[npart:]
    o = jnp.dot(oc_ref[...], wout_ref[...], preferred_element_type=F32)
    x1 = _read_part(x_refs, part_starts) + mod_ref[2:3, :] * o
    x1_ref[...] = x1
    h2 = _rms_mod(x1, n2_ref, mod_ref, 3, 4)
    h2_ref[...] = _pack_bf16_rows(h2)
    _route(h2, wr_ref, br_ref, idx_ref, gate_ref)


def _out_proj(layer, x_parts, mod, tile_sid, n2, attn, wout_bf, wr_t, br):
    tm = TOKEN_TILE
    part_starts, ntiles = _part_tiles(x_parts, tm)
    t = ntiles * tm
    row = lambda i, sid: (i, 0)
    const = lambda i, sid: (0, 0)
    in_specs = _part_specs(x_parts, part_starts, tm, D_MODEL) + [
        pl.BlockSpec((None, 6, D_MODEL), lambda i, sid: (sid[i], 0, 0)),
        pl.BlockSpec((1, D_MODEL), const)]
    scratch = []
    if layer == 0:
        in_specs.append(pl.BlockSpec((tm, A_QW), row))
        for (_, dil) in B_PAIRS:
            in_specs += [pl.BlockSpec((tm // dil, dil * B_GW), row), pl.BlockSpec((tm // dil, dil * LANES), row)]
        body = functools.partial(_out0_kernel, tm=tm, part_starts=part_starts)
        nscr = B_GROUPS * B_GW // LANES
        scratch = [pltpu.VMEM((nscr, tm, LANES), F32), pltpu.VMEM((B_GROUPS, tm, LANES), F32)]
    else:
        in_specs.append(pl.BlockSpec((tm, C_W), row))
        body = functools.partial(_out1_kernel, part_starts=part_starts)
    in_specs += [pl.BlockSpec(wout_bf.shape, const),
                 pl.BlockSpec((N_EXPERTS, D_MODEL), const),
                 pl.BlockSpec((N_EXPERTS, LANES), const)]
    gs = pltpu.PrefetchScalarGridSpec(
        num_scalar_prefetch=1, grid=(t // tm,), in_specs=in_specs,
        out_specs=[pl.BlockSpec((tm, D_MODEL), row), pl.BlockSpec((tm, PACK_W), row),
                   pl.BlockSpec((8, tm), lambda i, sid: (0, i)), pl.BlockSpec((8, tm), lambda i, sid: (0, i))],
        scratch_shapes=scratch)
    return pl.pallas_call(
        body, grid_spec=gs,
        out_shape=[jax.ShapeDtypeStruct((t, D_MODEL), F32), jax.ShapeDtypeStruct((t, PACK_W), jnp.int32),
                   jax.ShapeDtypeStruct((8, t), jnp.int32), jax.ShapeDtypeStruct((8, t), F32)],
        compiler_params=_cparams(("arbitrary",)),
        name="out_proj%d" % layer,
    )(tile_sid, *x_parts, mod, n2.reshape(1, D_MODEL), *attn, wout_bf, wr_t, br)


def _rank_kernel(idx_ref, tri_ref, rank_ref, cnt_ref, carry):
    i = pl.program_id(0)

    @pl.when(i == 0)
    def _():
        carry[...] = jnp.zeros_like(carry)

    tk = idx_ref.shape[1]
    eid = lax.broadcasted_iota(jnp.int32, (N_EXPERTS, tk), 0)
    hits = [eid == idx_ref[k:k + 1, :] for k in range(TOP_K)]
    onehot = sum(h.astype(F32) for h in hits)
    incl = jnp.dot(onehot.astype(BF16), tri_ref[...], preferred_element_type=F32)
    before = incl - onehot + carry[:, 0:1]
    rows = [jnp.sum(jnp.where(h, before, 0.0), axis=0, keepdims=True) for h in hits]
    rows.append(jnp.zeros((8 - TOP_K, tk), F32))
    rank_ref[...] = jnp.concatenate(rows, axis=0).astype(jnp.int32)
    carry[...] = carry[...] + incl[:, tk - 1:tk]
    cnt_ref[...] = carry[...].astype(jnp.int32)


def _slot_kernel(idx_ref, rank_ref, start_ref, slot_ref):
    tk = idx_ref.shape[1]
    eid = lax.broadcasted_iota(jnp.int32, (N_EXPERTS, tk), 0)
    start = start_ref[:, 0:1]
    rows = []
    for k in range(TOP_K):
        base = jnp.sum(jnp.where(eid == idx_ref[k:k + 1, :], start, 0), axis=0, keepdims=True)
        rows.append(base + rank_ref[k:k + 1, :])
    rows.append(jnp.zeros((8 - TOP_K, tk), jnp.int32))
    slot_ref[...] = jnp.concatenate(rows, axis=0)


def _route_slots(idx):
    t = idx.shape[1]
    tk = ROUTE_TILE
    tri = jnp.asarray(np.triu(np.ones((tk, tk), np.float32)), BF16)
    tok = lambda i: (0, i)
    rank, cnt = pl.pallas_call(
        _rank_kernel, grid=(t // tk,),
        in_specs=[pl.BlockSpec((8, tk), tok), pl.BlockSpec((tk, tk), lambda i: (0, 0))],
        out_specs=[pl.BlockSpec((8, tk), tok), pl.BlockSpec((N_EXPERTS, LANES), lambda i: (0, 0))],
        out_shape=[jax.ShapeDtypeStruct((8, t), jnp.int32), jax.ShapeDtypeStruct((N_EXPERTS, LANES), jnp.int32)],
        scratch_shapes=[pltpu.VMEM((N_EXPERTS, LANES), F32)],
        compiler_params=_cparams(("arbitrary",)),
        name="route_rank",
    )(idx, tri)
    counts = cnt[:, 0]
    padded = (counts + MOE_BM - 1) // MOE_BM * MOE_BM
    pad_end = jnp.cumsum(padded)
    pad_start = pad_end - padded
    nblk = (t * TOP_K) // MOE_BM + N_EXPERTS
    blk_row0 = jnp.arange(nblk, dtype=jnp.int32) * MOE_BM
    blk_expert = jnp.minimum(jnp.sum((pad_end[None, :] <= blk_row0[:, None]).astype(jnp.int32), axis=1),
                             N_EXPERTS - 1)
    nused = (pad_end[-1] // MOE_BM).astype(jnp.int32).reshape(1)
    start_b = jnp.broadcast_to(pad_start.astype(jnp.int32)[:, None], (N_EXPERTS, LANES))
    slot = pl.pallas_call(
        _slot_kernel, grid=(t // tk,),
        in_specs=[pl.BlockSpec((8, tk), tok), pl.BlockSpec((8, tk), tok),
                  pl.BlockSpec((N_EXPERTS, LANES), lambda i: (0, 0))],
        out_specs=pl.BlockSpec((8, tk), tok),
        out_shape=jax.ShapeDtypeStruct((8, t), jnp.int32),
        compiler_params=_cparams(("arbitrary",)),
        name="route_slot",
    )(idx, rank, start_b)
    return slot, blk_expert, nused


SC_CORES = 2
SC_SUBCORES = 16
SC_ROW_BUFFER_BYTES = 256 * 1024
SC_MAX_INDEX_LIST = 128


def _sc_chunks(t, row_bytes):
    chunk = min(SC_MAX_INDEX_LIST, SC_ROW_BUFFER_BYTES // row_bytes)
    nchunk = t // chunk
    per = nchunk // (SC_CORES * SC_SUBCORES)
    assert per * SC_CORES * SC_SUBCORES * chunk == t
    return chunk, nchunk, per


def _sc_slot_lists(slot, nchunk, chunk):
    return slot[:TOP_K].reshape(TOP_K, nchunk, chunk).transpose(1, 0, 2)


def _sc_dispatch(h2, slot, nslots):
    t, w = h2.shape
    chunk, nchunk, per = _sc_chunks(t, w * h2.dtype.itemsize)
    mesh = plsc.VectorSubcoreMesh(core_axis_name="c", subcore_axis_name="s")

    @functools.partial(
        pl.kernel, mesh=mesh, out_type=jax.ShapeDtypeStruct((nslots, w), h2.dtype),
        scratch_types=[pltpu.VMEM((TOP_K, chunk), jnp.int32), pltpu.VMEM((chunk, w), h2.dtype)],
        name="moe_dispatch_sc")
    def body(h_hbm, slot_hbm, xs_hbm, idx_v, rows_v):
        wid = lax.axis_index("s") * SC_CORES + lax.axis_index("c")

        @pl.loop(0, per)
        def _(j):
            c = wid * per + j
            pltpu.sync_copy(slot_hbm.at[c], idx_v)
            pltpu.sync_copy(h_hbm.at[pl.ds(c * chunk, chunk)], rows_v)
            for k in range(TOP_K):
                pltpu.sync_copy(rows_v, xs_hbm.at[idx_v.at[k]])

    return body(h2, _sc_slot_lists(slot, nchunk, chunk))


def _sc_gather(ys, slot, t):
    w = ys.shape[1]
    nbuf = 2
    chunk, nchunk, per = _sc_chunks(t, nbuf * w * ys.dtype.itemsize)
    mesh = plsc.VectorSubcoreMesh(core_axis_name="c", subcore_axis_name="s")

    @functools.partial(
        pl.kernel, mesh=mesh, out_type=jax.ShapeDtypeStruct((TOP_K, t, w), ys.dtype),
        scratch_types=[pltpu.VMEM((TOP_K, chunk), jnp.int32), pltpu.VMEM((nbuf, chunk, w), ys.dtype),
                       pltpu.SemaphoreType.DMA((nbuf,))],
        name="moe_gather_sc")
    def body(ys_hbm, slot_hbm, out_hbm, idx_v, rows_v, wsem):
        wid = lax.axis_index("s") * SC_CORES + lax.axis_index("c")

        @pl.loop(0, per)
        def _(j):
            c = wid * per + j
            pltpu.sync_copy(slot_hbm.at[c], idx_v)
            writes = []
            for k in range(TOP_K):
                b = k % nbuf
                if k >= nbuf:
                    writes[k - nbuf].wait()
                pltpu.sync_copy(ys_hbm.at[idx_v.at[k]], rows_v.at[b])
                writes.append(pltpu.async_copy(rows_v.at[b], out_hbm.at[k, pl.ds(c * chunk, chunk)], wsem.at[b]))
            for wr in writes[-nbuf:]:
                wr.wait()

    return body(ys, _sc_slot_lists(slot, nchunk, chunk))


def _ffn_kernel(be_ref, nu_ref, xs_ref, wgu_ref, bgu_ref, wd_ref, bd_ref, y_ref, wgu_bf, wd_bf):
    i = pl.program_id(0)
    prev = be_ref[jnp.maximum(i - 1, 0)]

    @pl.when(i < nu_ref[0])
    def _():
        @pl.when((i == 0) | (be_ref[i] != prev))
        def _():
            wgu_bf[...] = wgu_ref[...].astype(BF16)
            wd_bf[...] = wd_ref[...].astype(BF16)

        x = _unpack_bf16_rows(xs_ref[...])
        gu = jnp.dot(x, wgu_bf[...], preferred_element_type=F32) + bgu_ref[...]
        gate = jnp.minimum(gu[:, :D_FF], SWIGLU_LIMIT)
        up = jnp.clip(gu[:, D_FF:], -SWIGLU_LIMIT, SWIGLU_LIMIT)
        act = (up + 1.0) * gate * jax.nn.sigmoid(SWIGLU_ALPHA * gate)
        y = jnp.dot(act.astype(BF16), wd_bf[...], preferred_element_type=F32) + bd_ref[...]
        y_ref[...] = _pack_bf16_rows(y)

    @pl.when(i >= nu_ref[0])
    def _():
        y_ref[...] = jnp.zeros_like(y_ref)


def _expert_ffn(xs, blk_expert, nused, wgu, bgu, wd, bd):
    nslots = xs.shape[0]
    nblk = nslots // MOE_BM
    blk = lambda i, be, nu: (jnp.minimum(i, nu[0] - 1), 0)
    exp3 = lambda i, be, nu: (be[jnp.minimum(i, nu[0] - 1)], 0, 0)
    gs = pltpu.PrefetchScalarGridSpec(
        num_scalar_prefetch=2, grid=(nblk,),
        in_specs=[pl.BlockSpec((MOE_BM, PACK_W), blk),
                  pl.BlockSpec((None, D_MODEL, 2 * D_FF), exp3),
                  pl.BlockSpec((None, 1, 2 * D_FF), exp3),
                  pl.BlockSpec((None, D_FF, D_MODEL), exp3),
                  pl.BlockSpec((None, 1, D_MODEL), exp3)],
        out_specs=pl.BlockSpec((MOE_BM, PACK_W), lambda i, be, nu: (i, 0)),
        scratch_shapes=[pltpu.VMEM((D_MODEL, 2 * D_FF), BF16), pltpu.VMEM((D_FF, D_MODEL), BF16)])
    return pl.pallas_call(
        _ffn_kernel, grid_spec=gs,
        out_shape=jax.ShapeDtypeStruct((nslots, PACK_W), jnp.int32),
        compiler_params=_cparams(("arbitrary",)),
        name="moe_ffn",
    )(blk_expert, nused, xs, wgu, bgu.reshape(N_EXPERTS, 1, 2 * D_FF), wd, bd.reshape(N_EXPERTS, 1, D_MODEL))


def _combine_kernel(sid_ref, x_ref, mod_ref, gate_ref, yk_ref, *o_refs, tm, out_starts):
    del sid_ref
    gpad = jnp.concatenate([gate_ref[...], jnp.zeros((LANES - 8, tm), F32)], axis=0)
    gcol = gpad.T
    acc = gcol[:, 0:1] * _unpack_bf16_rows(yk_ref[0]).astype(F32)
    for k in range(1, TOP_K):
        acc = acc + gcol[:, k:k + 1] * _unpack_bf16_rows(yk_ref[k]).astype(F32)
    out = x_ref[...] + mod_ref[5:6, :] * acc
    i = pl.program_id(0)
    ends = out_starts[1:] + (None,)
    for o_ref, s0, s1 in zip(o_refs, out_starts, ends):
        mine = (i >= s0) if s1 is None else ((i >= s0) & (i < s1))

        @pl.when(mine)
        def _():
            o_ref[...] = out


def _combine(x1, mod, tile_sid, gates, yk, out_rows):
    t = x1.shape[0]
    tm = MOVE_TILE
    out_shape = [jax.ShapeDtypeStruct((n, D_MODEL), F32) for n in out_rows]
    out_starts, ntiles = _part_tiles(out_shape, tm)
    assert ntiles * tm == t
    gs = pltpu.PrefetchScalarGridSpec(
        num_scalar_prefetch=1, grid=(t // tm,),
        in_specs=[pl.BlockSpec((tm, D_MODEL), lambda i, sid: (i, 0)),
                  pl.BlockSpec((None, 6, D_MODEL), lambda i, sid: (sid[i], 0, 0)),
                  pl.BlockSpec((8, tm), lambda i, sid: (0, i)),
                  pl.BlockSpec((TOP_K, tm, PACK_W), lambda i, sid: (0, i, 0))],
        out_specs=_part_specs(out_shape, out_starts, tm, D_MODEL))
    return pl.pallas_call(
        functools.partial(_combine_kernel, tm=tm, out_starts=out_starts), grid_spec=gs,
        out_shape=out_shape,
        compiler_params=_cparams(("arbitrary",)),
        name="moe_combine",
    )(tile_sid, x1, mod, gates, yk)


def _moe(x1, h2, idx, gates, mod, tile_sid_move, wgu, bgu, wd, bd, out_rows):
    t = x1.shape[0]
    nslots = t * TOP_K + N_EXPERTS * MOE_BM
    slot, blk_expert, nused = _route_slots(idx)
    xs = _sc_dispatch(h2, slot, nslots)
    ys = _expert_ffn(xs, blk_expert, nused, wgu, bgu, wd, bd)
    return _combine(x1, mod, tile_sid_move, gates, _sc_gather(ys, slot, t), out_rows)


def _tile_table(seq_lens, tile, fn):
    vals = []
    start = 0
    for sid, n in enumerate(seq_lens):
        assert n % tile == 0
        for j in range(n // tile):
            vals.append(fn(sid, start, n, j))
        start += n
    return jnp.asarray(np.asarray(vals, np.int32))


def _head_block_diag():
    bd = np.kron(np.eye(2 * LANES // HEAD_DIM, dtype=np.float32),
                 np.full((HEAD_DIM, HEAD_DIM), 1.0 / HEAD_DIM, np.float32))
    return jnp.asarray(bd, BF16)


_A_HEAD_ORDER = [kv * A_GROUP + g for g in range(A_GROUP) for kv in range(A_KV_HEADS)]


def _trunk(x_parts, c_pad, seq_lens, p):
    t = sum(a.shape[0] for a in x_parts)
    sid_tok = _tile_table(seq_lens, TOKEN_TILE, lambda sid, s, n, j: sid)
    sid_move = _tile_table(seq_lens, MOVE_TILE, lambda sid, s, n, j: sid)
    bd = _head_block_diag()

    mod0 = _ada_mod(c_pad, p["l0_ada_w"], p["l0_ada_b"]).reshape(c_pad.shape[0], 6, D_MODEL)
    w_in = p["l0_w_in"]
    a_cols = np.concatenate([np.arange(h * HEAD_DIM, (h + 1) * HEAD_DIM) for h in _A_HEAD_ORDER])
    bq, bk, bv = A_IN, A_IN + B_GROUPS * B_GW, A_IN + 2 * B_GROUPS * B_GW
    col_blocks = [w_in[:, a_cols], w_in[:, A_QW:A_IN]]
    for g in range(B_GROUPS):
        col_blocks += [w_in[:, bq + g * B_GW:bq + (g + 1) * B_GW],
                       w_in[:, bk + g * B_GW:bk + (g + 1) * B_GW],
                       w_in[:, bv + g * B_GW:bv + (g + 1) * B_GW]]
    w0 = jnp.concatenate(col_blocks, axis=1).astype(BF16)
    ones = lambda n: jnp.ones((n,), F32)
    gain_blocks = [jnp.tile(p["l0_q_norm_a"], A_Q_HEADS) * QK_SCALE, jnp.tile(p["l0_k_norm_a"], A_KV_HEADS),
                   ones(A_KVW)]
    for g in range(B_GROUPS):
        gain_blocks += [jnp.tile(p["l0_q_norm_b"], B_HEADS) * QK_SCALE, jnp.tile(p["l0_k_norm_b"], B_HEADS),
                        ones(B_GW)]
    gain0 = jnp.concatenate(gain_blocks).reshape(1, -1)
    slabs0 = [(0, A_QW, True, 1), (A_QW, A_KVW, True, 1), (A_QW + A_KVW, A_KVW, False, 1)]
    c0 = A_IN
    for g, (_, dil) in enumerate(B_PAIRS):
        slabs0 += [(c0, B_GW, True, dil), (c0 + B_GW, B_GW, True, dil), (c0 + 2 * B_GW, B_GW, False, dil)]
        c0 += 3 * B_GW
    proj = _in_proj(x_parts, mod0, sid_tok, p["l0_norm1"], w0, gain0, bd, tuple(slabs0))
    qa, ka, va = proj[0:3]

    slopes_a = _alibi_slopes(A_Q_HEADS)
    heads_a = tuple(tuple((slopes_a[kv * A_GROUP + g], kv * A_GROUP + g) for kv in range(A_KV_HEADS))
                    for g in range(A_GROUP))
    (oa,) = _band_attn(qa, ka, va, seq_lens, tq=ATTN_TQ_A, dil=1, qw=A_QW, kvw=A_KVW, half=A_HALF, heads=heads_a,
                       kv_group=(0,) * A_GROUP, sink=p["l0_sink_a"].astype(F32), out_dtype=BF16, want_lse=False)

    slopes_b = _alibi_slopes(B_GROUPS * B_HEADS)
    attn0 = [oa]
    for g, (window, dil) in enumerate(B_PAIRS):
        qg, kg, vg = proj[3 + 3 * g:6 + 3 * g]
        heads_b = tuple(tuple((slopes_b[g * B_HEADS + 2 * pr + s] * dil, 0) for s in range(2))
                        for pr in range(B_HEADS // 2))
        og, lg = _band_attn(qg, kg, vg, [n // dil for n in seq_lens], tq=ATTN_TQ_B, dil=dil, qw=B_GW, kvw=B_GW, half=window // (2 * dil),
                            heads=heads_b, kv_group=tuple(range(B_HEADS // 2)), sink=None, out_dtype=F32,
                            want_lse=True)
        attn0 += [og, lg]

    w_out = p["l0_w_out"]
    wout0 = jnp.concatenate([w_out[a_cols], w_out[A_QW:]], axis=0).astype(BF16)
    br0 = jnp.broadcast_to(p["l0_b_router"].astype(F32)[:, None], (N_EXPERTS, LANES))
    x1, h2, idx, gates = _out_proj(0, x_parts, mod0, sid_tok, p["l0_norm2"], attn0, wout0, p["l0_w_router"].T,
                                   br0)
    x = _moe(x1, h2, idx, gates, mod0, sid_move, p["l0_w_gate_up"], p["l0_b_gate_up"], p["l0_w_down"],
             p["l0_b_down"], [t])

    mod1 = _ada_mod(c_pad, p["l1_ada_w"], p["l1_ada_b"]).reshape(c_pad.shape[0], 6, D_MODEL)
    w1 = p["l1_w_in"].astype(BF16)
    gain1 = jnp.concatenate([jnp.tile(p["l1_q_norm_c"], C_HEADS) * QK_SCALE, jnp.tile(p["l1_k_norm_c"], C_HEADS),
                             ones(C_W)]).reshape(1, -1)
    slabs1 = ((0, C_W, True, 1), (C_W, C_W, True, 1), (2 * C_W, C_W, False, 1))
    qc, kc, vc = _in_proj(x, mod1, sid_tok, p["l1_norm1"], w1, gain1, bd, slabs1)
    na_tile = NA_TROWS * GRID_W
    tile_r0 = _tile_table(seq_lens, na_tile, lambda sid, s, n, j: j * NA_TROWS)
    tile_nr = _tile_table(seq_lens, na_tile, lambda sid, s, n, j: n // GRID_W)
    oc = _na_attn(qc, kc, vc, _na_bias_table(p["l1_rpb_c"]), tile_r0, tile_nr)
    br1 = jnp.broadcast_to(p["l1_b_router"].astype(F32)[:, None], (N_EXPERTS, LANES))
    x1, h2, idx, gates = _out_proj(1, x, mod1, sid_tok, p["l1_norm2"], [oc], p["l1_w_out"].astype(BF16),
                                   p["l1_w_router"].T, br1)
    return _moe(x1, h2, idx, gates, mod1, sid_move, p["l1_w_gate_up"], p["l1_b_gate_up"], p["l1_w_down"],
                p["l1_b_down"], [a.shape[0] for a in x_parts])


def kernel(x_prompt, x_sample, c_prompt, c_sample, l0_ada_w, l0_ada_b, l0_norm1, l0_w_in, l0_q_norm_a, l0_k_norm_a, l0_sink_a, l0_q_norm_b, l0_k_norm_b, l0_w_out, l0_norm2, l0_w_router, l0_b_router, l0_w_gate_up, l0_b_gate_up, l0_w_down, l0_b_down, l1_ada_w, l1_ada_b, l1_norm1, l1_w_in, l1_q_norm_c, l1_k_norm_c, l1_rpb_c, l1_w_out, l1_norm2, l1_w_router, l1_b_router, l1_w_gate_up, l1_b_gate_up, l1_w_down, l1_b_down):
    p = dict(l0_ada_w=l0_ada_w, l0_ada_b=l0_ada_b, l0_norm1=l0_norm1, l0_w_in=l0_w_in, l0_q_norm_a=l0_q_norm_a,
             l0_k_norm_a=l0_k_norm_a, l0_sink_a=l0_sink_a, l0_q_norm_b=l0_q_norm_b, l0_k_norm_b=l0_k_norm_b,
             l0_w_out=l0_w_out, l0_norm2=l0_norm2, l0_w_router=l0_w_router, l0_b_router=l0_b_router,
             l0_w_gate_up=l0_w_gate_up, l0_b_gate_up=l0_b_gate_up, l0_w_down=l0_w_down, l0_b_down=l0_b_down,
             l1_ada_w=l1_ada_w, l1_ada_b=l1_ada_b, l1_norm1=l1_norm1, l1_w_in=l1_w_in, l1_q_norm_c=l1_q_norm_c,
             l1_k_norm_c=l1_k_norm_c, l1_rpb_c=l1_rpb_c, l1_w_out=l1_w_out, l1_norm2=l1_norm2,
             l1_w_router=l1_w_router, l1_b_router=l1_b_router, l1_w_gate_up=l1_w_gate_up,
             l1_b_gate_up=l1_b_gate_up, l1_w_down=l1_w_down, l1_b_down=l1_b_down)
    nb_p, len_p, d = x_prompt.shape
    nb_s, len_s, _ = x_sample.shape
    outs = []
    for xg, cg in ((x_prompt, c_prompt), (x_sample, c_sample)):
        nb, ln, _ = xg.shape
        c_pad = jnp.concatenate([cg, jnp.zeros((-nb % 8, d), F32)], axis=0)
        (y,) = _trunk([xg.reshape(nb * ln, d)], c_pad, (ln,) * nb, p)
        outs.append(y.reshape(nb, ln, d))
    return tuple(outs)
```

```python
import functools
import math

import numpy as np
import jax
import jax.numpy as jnp
from jax import lax
from jax.experimental import pallas as pl
from jax.experimental.pallas import tpu as pltpu
from jax.experimental.pallas import tpu_sc as plsc

F32 = jnp.float32
BF16 = jnp.bfloat16
HIGHEST = lax.Precision.HIGHEST

D_MODEL = 1024
HEAD_DIM = 64
LANES = 128
GRID_W = 64
A_Q_HEADS = 8
A_KV_HEADS = 2
A_GROUP = A_Q_HEADS // A_KV_HEADS
A_HALF = 128
B_PAIRS = ((128, 1), (512, 4), (2048, 16))
B_GROUPS = 3
B_HEADS = 4
A_QW = A_Q_HEADS * HEAD_DIM
A_KVW = A_KV_HEADS * HEAD_DIM
A_IN = A_QW + 2 * A_KVW
B_GW = B_HEADS * HEAD_DIM
C_HEADS = 16
C_W = C_HEADS * HEAD_DIM
NA_KH = 8
NA_KW = 16
N_EXPERTS = 32
TOP_K = 4
D_FF = 1024
SWIGLU_LIMIT = 7.0
SWIGLU_ALPHA = 1.702
RMS_EPS = 1e-6
NEG_INF = -1e30
QK_SCALE = HEAD_DIM ** -0.5

TOKEN_TILE = 1024
ATTN_TQ_A = 256
ATTN_TQ_B = 512
ATTN_UQ = 128
NA_TROWS = 8
MOE_BM = 512
ROUTE_TILE = 1024
MOVE_TILE = 256
VMEM_LIMIT = 56 * 1024 * 1024


def _alibi_slopes(n):
    return [float(2.0 ** (-8.0 * (j + 1) / n)) for j in range(n)]


def _cparams(sem, flags=None):
    return pltpu.CompilerParams(dimension_semantics=sem, vmem_limit_bytes=VMEM_LIMIT, flags=flags)


def _mod_kernel(c_ref, w_ref, b_ref, o_ref):
    c = c_ref[...]
    s = c * jax.nn.sigmoid(c)
    o_ref[...] = jnp.dot(s, w_ref[...], precision=HIGHEST, preferred_element_type=F32) + b_ref[...]


def _ada_mod(c_pad, w, b):
    nrow = c_pad.shape[0]
    ncol = w.shape[1]
    tn = D_MODEL
    return pl.pallas_call(
        _mod_kernel,
        grid=(ncol // tn,),
        in_specs=[pl.BlockSpec((nrow, D_MODEL), lambda j: (0, 0)),
                  pl.BlockSpec((D_MODEL, tn), lambda j: (0, j)),
                  pl.BlockSpec((1, tn), lambda j: (0, j))],
        out_specs=pl.BlockSpec((nrow, tn), lambda j: (0, j)),
        out_shape=jax.ShapeDtypeStruct((nrow, ncol), F32),
        compiler_params=_cparams(("arbitrary",)),
        name="ada_mod",
    )(c_pad, w, b.reshape(1, ncol))


def _head_mean_sq(y, bd_ref):
    w = y.shape[1]
    outs = []
    for c0 in range(0, w, 2 * LANES):
        cw = min(2 * LANES, w - c0)
        sq = y[:, c0:c0 + cw] * y[:, c0:c0 + cw]
        outs.append(jnp.dot(sq.astype(BF16), bd_ref[0:cw, 0:cw], preferred_element_type=F32))
    return outs[0] if len(outs) == 1 else jnp.concatenate(outs, axis=1)


def _rms_mod(x, g_ref, mod_ref, shift_row, scale_row):
    ms = jnp.mean(x * x, axis=-1, keepdims=True)
    xn = x * lax.rsqrt(ms + RMS_EPS) * g_ref[...]
    return xn * (1.0 + mod_ref[scale_row:scale_row + 1, :]) + mod_ref[shift_row:shift_row + 1, :]


def _part_tiles(parts, tile):
    starts, s = [], 0
    for a in parts:
        assert a.shape[0] % tile == 0
        starts.append(s)
        s += a.shape[0] // tile
    return tuple(starts), s


def _part_specs(parts, starts, tile, ncol):
    def spec(a, s0):
        n = a.shape[0] // tile
        return pl.BlockSpec((tile, ncol), lambda i, *_: (jnp.clip(i - s0, 0, n - 1), 0))
    return [spec(a, s0) for a, s0 in zip(parts, starts)]


def _read_part(refs, starts):
    i = pl.program_id(0)
    x = refs[0][...]
    for ref, s0 in zip(refs[1:], starts[1:]):
        x = jnp.where(i >= s0, ref[...], x)
    return x


def _in_kernel(sid_ref, *refs, slabs, tm, part_starts):
    del sid_ref
    npart = len(part_starts)
    x_refs = refs[:npart]
    mod_ref, n1_ref, w_ref, gain_ref, bd_ref = refs[npart:npart + 5]
    rest = refs[npart + 5:]
    nout = len(slabs)
    out_refs = rest[:nout]
    scr_ref = rest[nout] if len(rest) > nout else None
    h = _rms_mod(_read_part(x_refs, part_starts), n1_ref, mod_ref, 0, 1).astype(BF16)
    proj = jnp.dot(h, w_ref[...], preferred_element_type=F32)
    for (c0, w, normed, dil), o_ref in zip(slabs, out_refs):
        y = proj[:, c0:c0 + w]
        if normed:
            y = y * lax.rsqrt(_head_mean_sq(y, bd_ref) + RMS_EPS) * gain_ref[:, c0:c0 + w]
        if dil == 1:
            o_ref[...] = y.astype(BF16)
        else:
            for c in range(w // LANES):
                scr_ref[c] = y[:, c * LANES:(c + 1) * LANES]
            for r in range(dil):
                for c in range(w // LANES):
                    o_ref[:, r * w + c * LANES:r * w + (c + 1) * LANES] = (
                        scr_ref[c, pl.ds(r, tm // dil, stride=dil), :].astype(BF16))


def _in_proj(x_parts, mod, tile_sid, n1, w_bf, gain, bd, slabs):
    tm = TOKEN_TILE
    part_starts, ntiles = _part_tiles(x_parts, tm)
    t = ntiles * tm
    ncols = w_bf.shape[1]
    out_shapes, out_specs = [], []
    for (c0, w, normed, dil) in slabs:
        out_shapes.append(jax.ShapeDtypeStruct((t // dil, dil * w), BF16))
        out_specs.append(pl.BlockSpec((tm // dil, dil * w), lambda i, sid: (i, 0)))
    need_scr = any(s[3] > 1 for s in slabs)
    scratch = [pltpu.VMEM((max(s[1] for s in slabs if s[3] > 1) // LANES, tm, LANES), F32)] if need_scr else []
    gs = pltpu.PrefetchScalarGridSpec(
        num_scalar_prefetch=1,
        grid=(t // tm,),
        in_specs=_part_specs(x_parts, part_starts, tm, D_MODEL) + [
            pl.BlockSpec((None, 6, D_MODEL), lambda i, sid: (sid[i], 0, 0)),
            pl.BlockSpec((1, D_MODEL), lambda i, sid: (0, 0)),
            pl.BlockSpec((D_MODEL, ncols), lambda i, sid: (0, 0)),
            pl.BlockSpec((1, ncols), lambda i, sid: (0, 0)),
            pl.BlockSpec((2 * LANES, 2 * LANES), lambda i, sid: (0, 0))],
        out_specs=out_specs,
        scratch_shapes=scratch)
    return pl.pallas_call(
        functools.partial(_in_kernel, slabs=slabs, tm=tm, part_starts=part_starts),
        grid_spec=gs,
        out_shape=out_shapes,
        compiler_params=_cparams(("arbitrary",)),
        name="in_proj",
    )(tile_sid, *x_parts, mod, n1.reshape(1, D_MODEL), w_bf, gain, bd)


def _band_kernel(s0_ref, s1_ref, *refs, tq, hb, half, heads, kv_group, has_sink, want_lse):
    pos = 0
    sink_ref = None
    if has_sink:
        sink_ref = refs[0]
        pos = 1
    q_ref, kp_ref, kc_ref, kn_ref, vp_ref, vc_ref, vn_ref = refs[pos:pos + 7]
    o_ref = refs[pos + 7]
    lse_ref = refs[pos + 8] if want_lse else None
    i = pl.program_id(1)
    uq = ATTN_UQ
    wlen = uq + 2 * hb
    kall = jnp.concatenate([kp_ref[...], kc_ref[...], kn_ref[...]], axis=0)
    vall = jnp.concatenate([vp_ref[...], vc_ref[...], vn_ref[...]], axis=0)
    lane = lax.broadcasted_iota(jnp.int32, (uq, LANES), 1)
    lo_lanes = lane < HEAD_DIM
    for sub in range(tq // uq):
        rows = slice(sub * uq, (sub + 1) * uq)
        kwin = kall[sub * uq:sub * uq + wlen]
        vwin = vall[sub * uq:sub * uq + wlen]
        q0 = i * tq + sub * uq
        kpos = q0 - hb + lax.broadcasted_iota(jnp.int32, (wlen, uq), 0)
        qpos = q0 + lax.broadcasted_iota(jnp.int32, (wlen, uq), 1)
        rel = kpos - qpos
        valid = (jnp.abs(rel) <= half) & (kpos >= s0_ref[i]) & (kpos < s1_ref[i])
        distm = jnp.where(valid, jnp.abs(rel).astype(F32), 1e32)
        lse_rows = []
        for g, pair in enumerate(heads):
            qg = q_ref[rows, g * LANES:(g + 1) * LANES]
            kg = kwin[:, kv_group[g] * LANES:(kv_group[g] + 1) * LANES]
            vg = vwin[:, kv_group[g] * LANES:(kv_group[g] + 1) * LANES]
            outs = []
            for s, (slope, sink_idx) in enumerate(pair):
                keep = lo_lanes if s == 0 else jnp.logical_not(lo_lanes)
                qm = jnp.where(keep, qg, jnp.zeros_like(qg))
                sc = lax.dot_general(kg, qm, (((1,), (1,)), ((), ())), preferred_element_type=F32)
                sc = sc - slope * distm
                m = jnp.max(sc, axis=0, keepdims=True)
                if has_sink:
                    m = jnp.maximum(m, sink_ref[sink_idx])
                e = jnp.exp(sc - m)
                den = jnp.sum(e, axis=0, keepdims=True)
                if has_sink:
                    den = den + jnp.exp(sink_ref[sink_idx] - m)
                p = (e * (1.0 / den)).astype(BF16)
                outs.append(lax.dot_general(p, vg, (((0,), (0,)), ((), ())), preferred_element_type=F32))
                if want_lse:
                    lse_rows.append(m + jnp.log(den))
            o_ref[rows, g * LANES:(g + 1) * LANES] = jnp.where(lo_lanes, outs[0], outs[1]).astype(o_ref.dtype)
        if want_lse:
            lse_t = jnp.concatenate(lse_rows + [jnp.zeros((LANES - len(lse_rows), uq), F32)], axis=0)
            lse_ref[rows, :] = lse_t.T


def _band_attn(q, k, v, seq_rows, *, tq, dil, qw, kvw, half, heads, kv_group, sink, out_dtype, want_lse):
    rows = q.shape[0]
    hb = half
    s0 = _tile_table(seq_rows, tq, lambda sid, s, n, j: s)
    s1 = _tile_table(seq_rows, tq, lambda sid, s, n, j: s + n)
    per = tq // hb
    nt = rows // tq
    nhb = rows // hb
    has_sink = sink is not None
    in_specs = []
    args = []
    if has_sink:
        in_specs.append(pl.BlockSpec(memory_space=pltpu.SMEM))
        args.append(sink)
    qmap = lambda r, i, a, b: (i, r)
    pmap = lambda r, i, a, b: (jnp.maximum(i * per - 1, 0), r)
    nmap = lambda r, i, a, b: (jnp.minimum((i + 1) * per, nhb - 1), r)
    in_specs += [pl.BlockSpec((tq, qw), qmap),
                 pl.BlockSpec((hb, kvw), pmap), pl.BlockSpec((tq, kvw), qmap), pl.BlockSpec((hb, kvw), nmap),
                 pl.BlockSpec((hb, kvw), pmap), pl.BlockSpec((tq, kvw), qmap), pl.BlockSpec((hb, kvw), nmap)]
    args += [q, k, k, k, v, v, v]
    out_shape = [jax.ShapeDtypeStruct((rows, dil * qw), out_dtype)]
    out_specs = [pl.BlockSpec((tq, qw), qmap)]
    if want_lse:
        out_shape.append(jax.ShapeDtypeStruct((rows, dil * LANES), F32))
        out_specs.append(pl.BlockSpec((tq, LANES), qmap))
    gs = pltpu.PrefetchScalarGridSpec(num_scalar_prefetch=2, grid=(dil, nt),
                                      in_specs=in_specs, out_specs=out_specs)
    res = pl.pallas_call(
        functools.partial(_band_kernel, tq=tq, hb=hb, half=half, heads=heads, kv_group=kv_group,
                          has_sink=has_sink, want_lse=want_lse),
        grid_spec=gs,
        out_shape=out_shape,
        compiler_params=_cparams(("arbitrary", "arbitrary")),
        name="band_attn_d%d" % dil,
    )(s0, s1, *args)
    return res


def _na_kernel(r0_ref, nr_ref, q_ref, kp_ref, kc_ref, kn_ref, vp_ref, vc_ref, vn_ref, bias_ref,
               o_ref, kcat, vcat, *, halo):
    j = pl.program_id(0)
    tq = NA_TROWS * GRID_W
    hrows = halo * GRID_W
    kcat[0:hrows, :] = kp_ref[...]
    kcat[hrows:hrows + tq, :] = kc_ref[...]
    kcat[hrows + tq:, :] = kn_ref[...]
    vcat[0:hrows, :] = vp_ref[...]
    vcat[hrows:hrows + tq, :] = vc_ref[...]
    vcat[hrows + tq:, :] = vn_ref[...]
    r0 = r0_ref[j]
    nrows = nr_ref[j]
    kwin_len = NA_KH * GRID_W
    lane = lax.broadcasted_iota(jnp.int32, (GRID_W, LANES), 1)
    lo_lanes = lane < HEAD_DIM
    nrel = 2 * NA_KH - 1

    def row_body(u, carry):
        r = r0 + u
        rs = jnp.clip(r - NA_KH // 2, 0, nrows - NA_KH)
        off = pl.multiple_of((rs - r0 + halo) * GRID_W, GRID_W)
        bvar = rs - r + NA_KH - 1
        qrow = pl.multiple_of(u * GRID_W, GRID_W)
        for g in range(C_HEADS // 2):
            qg = q_ref[pl.ds(qrow, GRID_W), g * LANES:(g + 1) * LANES]
            qs = jnp.concatenate([jnp.where(lo_lanes, qg, jnp.zeros_like(qg)),
                                  jnp.where(lo_lanes, jnp.zeros_like(qg), qg)], axis=0)
            kg = kcat[pl.ds(off, kwin_len), g * LANES:(g + 1) * LANES]
            vg = vcat[pl.ds(off, kwin_len), g * LANES:(g + 1) * LANES]
            sc = lax.dot_general(kg, qs, (((1,), (1,)), ((), ())), preferred_element_type=F32)
            sc = sc + jnp.concatenate([bias_ref[g * nrel + bvar + kr] for kr in range(NA_KH)], axis=0)
            m = jnp.max(sc, axis=0, keepdims=True)
            e = jnp.exp(sc - m)
            den = jnp.sum(e, axis=0, keepdims=True)
            p = (e * (1.0 / den)).astype(BF16)
            pv = lax.dot_general(p, vg, (((0,), (0,)), ((), ())), preferred_element_type=F32)
            o = jnp.where(lo_lanes, pv[0:GRID_W], pv[GRID_W:])
            o_ref[pl.ds(qrow, GRID_W), g * LANES:(g + 1) * LANES] = o.astype(o_ref.dtype)
        return carry

    lax.fori_loop(0, NA_TROWS, row_body, 0)


def _na_attn(q, k, v, bias_tab, tile_r0, tile_nr):
    t = q.shape[0]
    halo = NA_KH // 2
    tq = NA_TROWS * GRID_W
    hrows = halo * GRID_W
    per = tq // hrows
    nhb = t // hrows
    nt = t // tq
    qmap = lambda j, a, b: (j, 0)
    pmap = lambda j, a, b: (jnp.maximum(j * per - 1, 0), 0)
    nmap = lambda j, a, b: (jnp.minimum((j + 1) * per, nhb - 1), 0)
    gs = pltpu.PrefetchScalarGridSpec(
        num_scalar_prefetch=2, grid=(nt,),
        in_specs=[pl.BlockSpec((tq, C_W), qmap),
                  pl.BlockSpec((hrows, C_W), pmap), pl.BlockSpec((tq, C_W), qmap), pl.BlockSpec((hrows, C_W), nmap),
                  pl.BlockSpec((hrows, C_W), pmap), pl.BlockSpec((tq, C_W), qmap), pl.BlockSpec((hrows, C_W), nmap),
                  pl.BlockSpec(bias_tab.shape, lambda j, a, b: (0, 0, 0))],
        out_specs=pl.BlockSpec((tq, C_W), qmap),
        scratch_shapes=[pltpu.VMEM((tq + 2 * hrows, C_W), BF16), pltpu.VMEM((tq + 2 * hrows, C_W), BF16)])
    return pl.pallas_call(
        functools.partial(_na_kernel, halo=halo),
        grid_spec=gs,
        out_shape=jax.ShapeDtypeStruct((t, C_W), BF16),
        compiler_params=_cparams(("arbitrary",)),
        name="na_attn",
    )(tile_r0, tile_nr, q, k, k, k, v, v, v, bias_tab)


def _na_bias_table(rpb):
    c = np.arange(GRID_W)
    cs = np.clip(c - NA_KW // 2, 0, GRID_W - NA_KW)
    kc = np.arange(GRID_W)
    valid = (kc[:, None] >= cs[None, :]) & (kc[:, None] < cs[None, :] + NA_KW)
    cidx = np.clip(kc[:, None] - c[None, :] + NA_KW - 1, 0, 2 * NA_KW - 2)
    tab = rpb.astype(F32)[:, :, cidx]
    tab = jnp.where(jnp.asarray(valid)[None, None], tab, NEG_INF)
    pairs = jnp.concatenate([tab[0::2], tab[1::2]], axis=-1)
    return pairs.reshape((C_HEADS // 2) * (2 * NA_KH - 1), GRID_W, LANES)


def _route(h2, wr_ref, br_ref, idx_ref, gate_ref):
    tm = h2.shape[0]
    logits = lax.dot_general(wr_ref[...], h2, (((1,), (1,)), ((), ())),
                             precision=HIGHEST, preferred_element_type=F32) + br_ref[:, 0:1]
    eid = lax.broadcasted_iota(jnp.int32, (N_EXPERTS, tm), 0)
    vals, idxs = [], []
    for _ in range(TOP_K):
        m = jnp.max(logits, axis=0, keepdims=True)
        ix = jnp.min(jnp.where(logits == m, eid, N_EXPERTS), axis=0, keepdims=True)
        vals.append(m)
        idxs.append(ix)
        logits = jnp.where(eid == ix, -jnp.inf, logits)
    es = [jnp.exp(vk - vals[0]) for vk in vals]
    den = es[0] + es[1] + es[2] + es[3]
    pad_i = jnp.zeros((8 - TOP_K, tm), jnp.int32)
    pad_f = jnp.zeros((8 - TOP_K, tm), F32)
    idx_ref[...] = jnp.concatenate(idxs + [pad_i], axis=0)
    gate_ref[...] = jnp.concatenate([ek / den for ek in es] + [pad_f], axis=0)


PACK_W = D_MODEL // 2


def _pack_bf16_rows(h):
    bits = pltpu.bitcast(h.astype(BF16).astype(F32), jnp.int32)
    return (bits[:, :PACK_W] & jnp.int32(-65536)) | lax.shift_right_logical(bits[:, PACK_W:], 16)


def _unpack_bf16_rows(w):
    hi = pltpu.bitcast(w & jnp.int32(-65536), F32).astype(BF16)
    lo = pltpu.bitcast(lax.shift_left(w, 16), F32).astype(BF16)
    return jnp.concatenate([hi, lo], axis=1)


def _out0_kernel(sid_ref, *refs, tm, part_starts):
    del sid_ref
    npart = len(part_starts)
    x_refs = refs[:npart]
    (mod_ref, n2_ref, oa_ref, ob0_ref, ls0_ref, ob1_ref, ls1_ref, ob2_ref, ls2_ref,
     wout_ref, wr_ref, br_ref, x1_ref, h2_ref, idx_ref, gate_ref, oscr, lscr) = refs[npart:]
    os_, ls_ = [], []
    for gi, (o_ref, l_ref) in enumerate(((ob0_ref, ls0_ref), (ob1_ref, ls1_ref), (ob2_ref, ls2_ref))):
        dil = B_PAIRS[gi][1]
        if dil == 1:
            os_.append(o_ref[...])
            ls_.append(l_ref[...])
        else:
            ncg = B_GW // LANES
            for r in range(dil):
                for c in range(ncg):
                    lo = r * B_GW + c * LANES
                    oscr[gi * ncg + c, pl.ds(r, tm // dil, stride=dil), :] = o_ref[:, lo:lo + LANES]
                lscr[gi, pl.ds(r, tm // dil, stride=dil), :] = l_ref[:, r * LANES:(r + 1) * LANES]
            os_.append(jnp.concatenate([oscr[gi * ncg + c] for c in range(ncg)], axis=1))
            ls_.append(lscr[gi])
    lmax = jnp.maximum(jnp.maximum(ls_[0], ls_[1]), ls_[2])
    ws = [jnp.exp(l - lmax) for l in ls_]
    winv = 1.0 / (ws[0] + ws[1] + ws[2])
    lane = lax.broadcasted_iota(jnp.int32, (tm, LANES), 1)
    lo_lanes = lane < HEAD_DIM

    def head_weights(w):
        return jnp.concatenate(
            [jnp.where(lo_lanes, jnp.broadcast_to(w[:, 2 * pr:2 * pr + 1], (tm, LANES)),
                       jnp.broadcast_to(w[:, 2 * pr + 1:2 * pr + 2], (tm, LANES)))
             for pr in range(B_HEADS // 2)], axis=1)

    ob = (head_weights(ws[0] * winv) * os_[0] + head_weights(ws[1] * winv) * os_[1]
          + head_weights(ws[2] * winv) * os_[2])
    o = (jnp.dot(oa_ref[...], wout_ref[0:A_QW, :], preferred_element_type=F32)
         + jnp.dot(ob.astype(BF16), wout_ref[A_QW:, :], preferred_element_type=F32))
    x1 = _read_part(x_refs, part_starts) + mod_ref[2:3, :] * o
    x1_ref[...] = x1
    h2 = _rms_mod(x1, n2_ref, mod_ref, 3, 4)
    h2_ref[...] = _pack_bf16_rows(h2)
    _route(h2, wr_ref, br_ref, idx_ref, gate_ref)


def _out1_kernel(sid_ref, *refs, part_starts):
    del sid_ref
    npart = len(part_starts)
    x_refs = refs[:npart]
    mod_ref, n2_ref, oc_ref, wout_ref, wr_ref, br_ref, x1_ref, h2_ref, idx_ref, gate_ref = refs[npart:]
    o = jnp.dot(oc_ref[...], wout_ref[...], preferred_element_type=F32)
    x1 = _read_part(x_refs, part_starts) + mod_ref[2:3, :] * o
    x1_ref[...] = x1
    h2 = _rms_mod(x1, n2_ref, mod_ref, 3, 4)
    h2_ref[...] = _pack_bf16_rows(h2)
    _route(h2, wr_ref, br_ref, idx_ref, gate_ref)


def _out_proj(layer, x_parts, mod, tile_sid, n2, attn, wout_bf, wr_t, br):
    tm = TOKEN_TILE
    part_starts, ntiles = _part_tiles(x_parts, tm)
    t = ntiles * tm
    row = lambda i, sid: (i, 0)
    const = lambda i, sid: (0, 0)
    in_specs = _part_specs(x_parts, part_starts, tm, D_MODEL) + [
        pl.BlockSpec((None, 6, D_MODEL), lambda i, sid: (sid[i], 0, 0)),
        pl.BlockSpec((1, D_MODEL), const)]
    scratch = []
    if layer == 0:
        in_specs.append(pl.BlockSpec((tm, A_QW), row))
        for (_, dil) in B_PAIRS:
            in_specs += [pl.BlockSpec((tm // dil, dil * B_GW), row), pl.BlockSpec((tm // dil, dil * LANES), row)]
        body = functools.partial(_out0_kernel, tm=tm, part_starts=part_starts)
        nscr = B_GROUPS * B_GW // LANES
        scratch = [pltpu.VMEM((nscr, tm, LANES), F32), pltpu.VMEM((B_GROUPS, tm, LANES), F32)]
    else:
        in_specs.append(pl.BlockSpec((tm, C_W), row))
        body = functools.partial(_out1_kernel, part_starts=part_starts)
    in_specs += [pl.BlockSpec(wout_bf.shape, const),
                 pl.BlockSpec((N_EXPERTS, D_MODEL), const),
                 pl.BlockSpec((N_EXPERTS, LANES), const)]
    gs = pltpu.PrefetchScalarGridSpec(
        num_scalar_prefetch=1, grid=(t // tm,), in_specs=in_specs,
        out_specs=[pl.BlockSpec((tm, D_MODEL), row), pl.BlockSpec((tm, PACK_W), row),
                   pl.BlockSpec((8, tm), lambda i, sid: (0, i)), pl.BlockSpec((8, tm), lambda i, sid: (0, i))],
        scratch_shapes=scratch)
    return pl.pallas_call(
        body, grid_spec=gs,
        out_shape=[jax.ShapeDtypeStruct((t, D_MODEL), F32), jax.ShapeDtypeStruct((t, PACK_W), jnp.int32),
                   jax.ShapeDtypeStruct((8, t), jnp.int32), jax.ShapeDtypeStruct((8, t), F32)],
        compiler_params=_cparams(("arbitrary",)),
        name="out_proj%d" % layer,
    )(tile_sid, *x_parts, mod, n2.reshape(1, D_MODEL), *attn, wout_bf, wr_t, br)


def _rank_kernel(idx_ref, tri_ref, rank_ref, cnt_ref, carry):
    i = pl.program_id(0)

    @pl.when(i == 0)
    def _():
        carry[...] = jnp.zeros_like(carry)

    tk = idx_ref.shape[1]
    eid = lax.broadcasted_iota(jnp.int32, (N_EXPERTS, tk), 0)
    hits = [eid == idx_ref[k:k + 1, :] for k in range(TOP_K)]
    onehot = sum(h.astype(F32) for h in hits)
    incl = jnp.dot(onehot.astype(BF16), tri_ref[...], preferred_element_type=F32)
    before = incl - onehot + carry[:, 0:1]
    rows = [jnp.sum(jnp.where(h, before, 0.0), axis=0, keepdims=True) for h in hits]
    rows.append(jnp.zeros((8 - TOP_K, tk), F32))
    rank_ref[...] = jnp.concatenate(rows, axis=0).astype(jnp.int32)
    carry[...] = carry[...] + incl[:, tk - 1:tk]
    cnt_ref[...] = carry[...].astype(jnp.int32)


def _slot_kernel(idx_ref, rank_ref, start_ref, slot_ref):
    tk = idx_ref.shape[1]
    eid = lax.broadcasted_iota(jnp.int32, (N_EXPERTS, tk), 0)
    start = start_ref[:, 0:1]
    rows = []
    for k in range(TOP_K):
        base = jnp.sum(jnp.where(eid == idx_ref[k:k + 1, :], start, 0), axis=0, keepdims=True)
        rows.append(base + rank_ref[k:k + 1, :])
    rows.append(jnp.zeros((8 - TOP_K, tk), jnp.int32))
    slot_ref[...] = jnp.concatenate(rows, axis=0)


def _route_slots(idx):
    t = idx.shape[1]
    tk = ROUTE_TILE
    tri = jnp.asarray(np.triu(np.ones((tk, tk), np.float32)), BF16)
    tok = lambda i: (0, i)
    rank, cnt = pl.pallas_call(
        _rank_kernel, grid=(t // tk,),
        in_specs=[pl.BlockSpec((8, tk), tok), pl.BlockSpec((tk, tk), lambda i: (0, 0))],
        out_specs=[pl.BlockSpec((8, tk), tok), pl.BlockSpec((N_EXPERTS, LANES), lambda i: (0, 0))],
        out_shape=[jax.ShapeDtypeStruct((8, t), jnp.int32), jax.ShapeDtypeStruct((N_EXPERTS, LANES), jnp.int32)],
        scratch_shapes=[pltpu.VMEM((N_EXPERTS, LANES), F32)],
        compiler_params=_cparams(("arbitrary",)),
        name="route_rank",
    )(idx, tri)
    counts = cnt[:, 0]
    padded = (counts + MOE_BM - 1) // MOE_BM * MOE_BM
    pad_end = jnp.cumsum(padded)
    pad_start = pad_end - padded
    nblk = (t * TOP_K) // MOE_BM + N_EXPERTS
    blk_row0 = jnp.arange(nblk, dtype=jnp.int32) * MOE_BM
    blk_expert = jnp.minimum(jnp.sum((pad_end[None, :] <= blk_row0[:, None]).astype(jnp.int32), axis=1),
                             N_EXPERTS - 1)
    nused = (pad_end[-1] // MOE_BM).astype(jnp.int32).reshape(1)
    eids = jnp.arange(N_EXPERTS, dtype=jnp.int32)
    later = (eids[None, :] > eids[:, None]) & (padded[None, :] > 0)
    next_expert = jnp.min(jnp.where(later, eids[None, :], N_EXPERTS), axis=1)
    next_expert = jnp.where(next_expert == N_EXPERTS, -1, next_expert).astype(jnp.int32)
    blk_next = next_expert[blk_expert]
    start_b = jnp.broadcast_to(pad_start.astype(jnp.int32)[:, None], (N_EXPERTS, LANES))
    slot = pl.pallas_call(
        _slot_kernel, grid=(t // tk,),
        in_specs=[pl.BlockSpec((8, tk), tok), pl.BlockSpec((8, tk), tok),
                  pl.BlockSpec((N_EXPERTS, LANES), lambda i: (0, 0))],
        out_specs=pl.BlockSpec((8, tk), tok),
        out_shape=jax.ShapeDtypeStruct((8, t), jnp.int32),
        compiler_params=_cparams(("arbitrary",)),
        name="route_slot",
    )(idx, rank, start_b)
    return slot, blk_expert, nused, blk_next


SC_CORES = 2
SC_SUBCORES = 16
SC_ROW_BUFFER_BYTES = 256 * 1024
SC_MAX_INDEX_LIST = 128


def _sc_chunks(t, row_bytes):
    chunk = min(SC_MAX_INDEX_LIST, SC_ROW_BUFFER_BYTES // row_bytes)
    nchunk = t // chunk
    per = nchunk // (SC_CORES * SC_SUBCORES)
    assert per * SC_CORES * SC_SUBCORES * chunk == t
    return chunk, nchunk, per


def _sc_slot_lists(slot, nchunk, chunk):
    return slot[:TOP_K].reshape(TOP_K, nchunk, chunk).transpose(1, 0, 2)


def _sc_dispatch(h2, slot, nslots):
    t, w = h2.shape
    chunk, nchunk, per = _sc_chunks(t, w * h2.dtype.itemsize)
    mesh = plsc.VectorSubcoreMesh(core_axis_name="c", subcore_axis_name="s")

    @functools.partial(
        pl.kernel, mesh=mesh, out_type=jax.ShapeDtypeStruct((nslots, w), h2.dtype),
        scratch_types=[pltpu.VMEM((TOP_K, chunk), jnp.int32), pltpu.VMEM((chunk, w), h2.dtype)],
        name="moe_dispatch_sc")
    def body(h_hbm, slot_hbm, xs_hbm, idx_v, rows_v):
        wid = lax.axis_index("s") * SC_CORES + lax.axis_index("c")

        @pl.loop(0, per)
        def _(j):
            c = wid * per + j
            pltpu.sync_copy(slot_hbm.at[c], idx_v)
            pltpu.sync_copy(h_hbm.at[pl.ds(c * chunk, chunk)], rows_v)
            for k in range(TOP_K):
                pltpu.sync_copy(rows_v, xs_hbm.at[idx_v.at[k]])

    return body(h2, _sc_slot_lists(slot, nchunk, chunk))


def _sc_gather(ys, slot, t):
    w = ys.shape[1]
    nbuf = 2
    chunk, nchunk, per = _sc_chunks(t, nbuf * w * ys.dtype.itemsize)
    mesh = plsc.VectorSubcoreMesh(core_axis_name="c", subcore_axis_name="s")

    @functools.partial(
        pl.kernel, mesh=mesh, out_type=jax.ShapeDtypeStruct((TOP_K, t, w), ys.dtype),
        scratch_types=[pltpu.VMEM((TOP_K, chunk), jnp.int32), pltpu.VMEM((nbuf, chunk, w), ys.dtype),
                       pltpu.SemaphoreType.DMA((nbuf,))],
        name="moe_gather_sc")
    def body(ys_hbm, slot_hbm, out_hbm, idx_v, rows_v, wsem):
        wid = lax.axis_index("s") * SC_CORES + lax.axis_index("c")

        @pl.loop(0, per)
        def _(j):
            c = wid * per + j
            pltpu.sync_copy(slot_hbm.at[c], idx_v)
            writes = []
            for k in range(TOP_K):
                b = k % nbuf
                if k >= nbuf:
                    writes[k - nbuf].wait()
                pltpu.sync_copy(ys_hbm.at[idx_v.at[k]], rows_v.at[b])
                writes.append(pltpu.async_copy(rows_v.at[b], out_hbm.at[k, pl.ds(c * chunk, chunk)], wsem.at[b]))
            for wr in writes[-nbuf:]:
                wr.wait()

    return body(ys, _sc_slot_lists(slot, nchunk, chunk))


def _ffn_kernel(be_ref, nu_ref, nxt_ref, xs_ref, wgu_hbm, bgu_ref, wd_hbm, bd_ref, y_ref,
                wgu_f, wd_f, wgu_bf, wd_bf, wsem):
    i = pl.program_id(0)
    prev = be_ref[jnp.maximum(i - 1, 0)]

    def weight_copies(e):
        return (pltpu.make_async_copy(wgu_hbm.at[e], wgu_f, wsem.at[0]),
                pltpu.make_async_copy(wd_hbm.at[e], wd_f, wsem.at[1]))

    @pl.when(i == 0)
    def _():
        for cp in weight_copies(be_ref[0]):
            cp.start()

    @pl.when(i < nu_ref[0])
    def _():
        @pl.when((i == 0) | (be_ref[i] != prev))
        def _():
            for cp in weight_copies(be_ref[i]):
                cp.wait()
            wgu_bf[...] = wgu_f[...].astype(BF16)
            wd_bf[...] = wd_f[...].astype(BF16)

            @pl.when(nxt_ref[i] >= 0)
            def _():
                for cp in weight_copies(nxt_ref[i]):
                    cp.start()

        x = _unpack_bf16_rows(xs_ref[...])
        gu = jnp.dot(x, wgu_bf[...], preferred_element_type=F32) + bgu_ref[...]
        gate = jnp.minimum(gu[:, :D_FF], SWIGLU_LIMIT)
        up = jnp.clip(gu[:, D_FF:], -SWIGLU_LIMIT, SWIGLU_LIMIT)
        act = (up + 1.0) * gate * jax.nn.sigmoid(SWIGLU_ALPHA * gate)
        y = jnp.dot(act.astype(BF16), wd_bf[...], preferred_element_type=F32) + bd_ref[...]
        y_ref[...] = _pack_bf16_rows(y)

    @pl.when(i >= nu_ref[0])
    def _():
        y_ref[...] = jnp.zeros_like(y_ref)


def _expert_ffn(xs, blk_expert, nused, blk_next, wgu, bgu, wd, bd):
    nslots = xs.shape[0]
    nblk = nslots // MOE_BM
    blk = lambda i, be, nu, nx: (jnp.minimum(i, nu[0] - 1), 0)
    exp3 = lambda i, be, nu, nx: (be[jnp.minimum(i, nu[0] - 1)], 0, 0)
    gs = pltpu.PrefetchScalarGridSpec(
        num_scalar_prefetch=3, grid=(nblk,),
        in_specs=[pl.BlockSpec((MOE_BM, PACK_W), blk),
                  pl.BlockSpec(memory_space=pl.ANY),
                  pl.BlockSpec((None, 1, 2 * D_FF), exp3),
                  pl.BlockSpec(memory_space=pl.ANY),
                  pl.BlockSpec((None, 1, D_MODEL), exp3)],
        out_specs=pl.BlockSpec((MOE_BM, PACK_W), lambda i, be, nu, nx: (i, 0)),
        scratch_shapes=[pltpu.VMEM((D_MODEL, 2 * D_FF), F32), pltpu.VMEM((D_FF, D_MODEL), F32),
                        pltpu.VMEM((D_MODEL, 2 * D_FF), BF16), pltpu.VMEM((D_FF, D_MODEL), BF16),
                        pltpu.SemaphoreType.DMA((2,))])
    return pl.pallas_call(
        _ffn_kernel, grid_spec=gs,
        out_shape=jax.ShapeDtypeStruct((nslots, PACK_W), jnp.int32),
        compiler_params=_cparams(("arbitrary",)),
        name="moe_ffn",
    )(blk_expert, nused, blk_next, xs, wgu, bgu.reshape(N_EXPERTS, 1, 2 * D_FF), wd,
      bd.reshape(N_EXPERTS, 1, D_MODEL))


def _combine_kernel(sid_ref, x_ref, mod_ref, gate_ref, yk_ref, *o_refs, tm, out_starts):
    del sid_ref
    gpad = jnp.concatenate([gate_ref[...], jnp.zeros((LANES - 8, tm), F32)], axis=0)
    gcol = gpad.T
    acc = gcol[:, 0:1] * _unpack_bf16_rows(yk_ref[0]).astype(F32)
    for k in range(1, TOP_K):
        acc = acc + gcol[:, k:k + 1] * _unpack_bf16_rows(yk_ref[k]).astype(F32)
    out = x_ref[...] + mod_ref[5:6, :] * acc
    i = pl.program_id(0)
    ends = out_starts[1:] + (None,)
    for o_ref, s0, s1 in zip(o_refs, out_starts, ends):
        mine = (i >= s0) if s1 is None else ((i >= s0) & (i < s1))

        @pl.when(mine)
        def _():
            o_ref[...] = out


def _combine(x1, mod, tile_sid, gates, yk, out_rows):
    t = x1.shape[0]
    tm = MOVE_TILE
    out_shape = [jax.ShapeDtypeStruct((n, D_MODEL), F32) for n in out_rows]
    out_starts, ntiles = _part_tiles(out_shape, tm)
    assert ntiles * tm == t
    gs = pltpu.PrefetchScalarGridSpec(
        num_scalar_prefetch=1, grid=(t // tm,),
        in_specs=[pl.BlockSpec((tm, D_MODEL), lambda i, sid: (i, 0)),
                  pl.BlockSpec((None, 6, D_MODEL), lambda i, sid: (sid[i], 0, 0)),
                  pl.BlockSpec((8, tm), lambda i, sid: (0, i)),
                  pl.BlockSpec((TOP_K, tm, PACK_W), lambda i, sid: (0, i, 0))],
        out_specs=_part_specs(out_shape, out_starts, tm, D_MODEL))
    return pl.pallas_call(
        functools.partial(_combine_kernel, tm=tm, out_starts=out_starts), grid_spec=gs,
        out_shape=out_shape,
        compiler_params=_cparams(("arbitrary",)),
        name="moe_combine",
    )(tile_sid, x1, mod, gates, yk)


def _moe(x1, h2, idx, gates, mod, tile_sid_move, wgu, bgu, wd, bd, out_rows):
    t = x1.shape[0]
    nslots = t * TOP_K + N_EXPERTS * MOE_BM
    slot, blk_expert, nused, blk_next = _route_slots(idx)
    xs = _sc_dispatch(h2, slot, nslots)
    ys = _expert_ffn(xs, blk_expert, nused, blk_next, wgu, bgu, wd, bd)
    return _combine(x1, mod, tile_sid_move, gates, _sc_gather(ys, slot, t), out_rows)


def _tile_table(seq_lens, tile, fn):
    vals = []
    start = 0
    for sid, n in enumerate(seq_lens):
        assert n % tile == 0
        for j in range(n // tile):
            vals.append(fn(sid, start, n, j))
        start += n
    return jnp.asarray(np.asarray(vals, np.int32))


def _head_block_diag():
    bd = np.kron(np.eye(2 * LANES // HEAD_DIM, dtype=np.float32),
                 np.full((HEAD_DIM, HEAD_DIM), 1.0 / HEAD_DIM, np.float32))
    return jnp.asarray(bd, BF16)


_A_HEAD_ORDER = [kv * A_GROUP + g for g in range(A_GROUP) for kv in range(A_KV_HEADS)]


def _trunk(x_parts, c_pad, seq_lens, p):
    t = sum(a.shape[0] for a in x_parts)
    sid_tok = _tile_table(seq_lens, TOKEN_TILE, lambda sid, s, n, j: sid)
    sid_move = _tile_table(seq_lens, MOVE_TILE, lambda sid, s, n, j: sid)
    bd = _head_block_diag()

    mod0 = _ada_mod(c_pad, p["l0_ada_w"], p["l0_ada_b"]).reshape(c_pad.shape[0], 6, D_MODEL)
    w_in = p["l0_w_in"]
    a_cols = np.concatenate([np.arange(h * HEAD_DIM, (h + 1) * HEAD_DIM) for h in _A_HEAD_ORDER])
    bq, bk, bv = A_IN, A_IN + B_GROUPS * B_GW, A_IN + 2 * B_GROUPS * B_GW
    col_blocks = [w_in[:, a_cols], w_in[:, A_QW:A_IN]]
    for g in range(B_GROUPS):
        col_blocks += [w_in[:, bq + g * B_GW:bq + (g + 1) * B_GW],
                       w_in[:, bk + g * B_GW:bk + (g + 1) * B_GW],
                       w_in[:, bv + g * B_GW:bv + (g + 1) * B_GW]]
    w0 = jnp.concatenate(col_blocks, axis=1).astype(BF16)
    ones = lambda n: jnp.ones((n,), F32)
    gain_blocks = [jnp.tile(p["l0_q_norm_a"], A_Q_HEADS) * QK_SCALE, jnp.tile(p["l0_k_norm_a"], A_KV_HEADS),
                   ones(A_KVW)]
    for g in range(B_GROUPS):
        gain_blocks += [jnp.tile(p["l0_q_norm_b"], B_HEADS) * QK_SCALE, jnp.tile(p["l0_k_norm_b"], B_HEADS),
                        ones(B_GW)]
    gain0 = jnp.concatenate(gain_blocks).reshape(1, -1)
    slabs0 = [(0, A_QW, True, 1), (A_QW, A_KVW, True, 1), (A_QW + A_KVW, A_KVW, False, 1)]
    c0 = A_IN
    for g, (_, dil) in enumerate(B_PAIRS):
        slabs0 += [(c0, B_GW, True, dil), (c0 + B_GW, B_GW, True, dil), (c0 + 2 * B_GW, B_GW, False, dil)]
        c0 += 3 * B_GW
    proj = _in_proj(x_parts, mod0, sid_tok, p["l0_norm1"], w0, gain0, bd, tuple(slabs0))
    qa, ka, va = proj[0:3]

    slopes_a = _alibi_slopes(A_Q_HEADS)
    heads_a = tuple(tuple((slopes_a[kv * A_GROUP + g], kv * A_GROUP + g) for kv in range(A_KV_HEADS))
                    for g in range(A_GROUP))
    (oa,) = _band_attn(qa, ka, va, seq_lens, tq=ATTN_TQ_A, dil=1, qw=A_QW, kvw=A_KVW, half=A_HALF, heads=heads_a,
                       kv_group=(0,) * A_GROUP, sink=p["l0_sink_a"].astype(F32), out_dtype=BF16, want_lse=False)

    slopes_b = _alibi_slopes(B_GROUPS * B_HEADS)
    attn0 = [oa]
    for g, (window, dil) in enumerate(B_PAIRS):
        qg, kg, vg = proj[3 + 3 * g:6 + 3 * g]
        heads_b = tuple(tuple((slopes_b[g * B_HEADS + 2 * pr + s] * dil, 0) for s in range(2))
                        for pr in range(B_HEADS // 2))
        og, lg = _band_attn(qg, kg, vg, [n // dil for n in seq_lens], tq=ATTN_TQ_B, dil=dil, qw=B_GW, kvw=B_GW, half=window // (2 * dil),
                            heads=heads_b, kv_group=tuple(range(B_HEADS // 2)), sink=None, out_dtype=F32,
                            want_lse=True)
        attn0 += [og, lg]

    w_out = p["l0_w_out"]
    wout0 = jnp.concatenate([w_out[a_cols], w_out[A_QW:]], axis=0).astype(BF16)
    br0 = jnp.broadcast_to(p["l0_b_router"].astype(F32)[:, None], (N_EXPERTS, LANES))
    x1, h2, idx, gates = _out_proj(0, x_parts, mod0, sid_tok, p["l0_norm2"], attn0, wout0, p["l0_w_router"].T,
                                   br0)
    x = _moe(x1, h2, idx, gates, mod0, sid_move, p["l0_w_gate_up"], p["l0_b_gate_up"], p["l0_w_down"],
             p["l0_b_down"], [t])

    mod1 = _ada_mod(c_pad, p["l1_ada_w"], p["l1_ada_b"]).reshape(c_pad.shape[0], 6, D_MODEL)
    w1 = p["l1_w_in"].astype(BF16)
    gain1 = jnp.concatenate([jnp.tile(p["l1_q_norm_c"], C_HEADS) * QK_SCALE, jnp.tile(p["l1_k_norm_c"], C_HEADS),
                             ones(C_W)]).reshape(1, -1)
    slabs1 = ((0, C_W, True, 1), (C_W, C_W, True, 1), (2 * C_W, C_W, False, 1))
    qc, kc, vc = _in_proj(x, mod1, sid_tok, p["l1_norm1"], w1, gain1, bd, slabs1)
    na_tile = NA_TROWS * GRID_W
    tile_r0 = _tile_table(seq_lens, na_tile, lambda sid, s, n, j: j * NA_TROWS)
    tile_nr = _tile_table(seq_lens, na_tile, lambda sid, s, n, j: n // GRID_W)
    oc = _na_attn(qc, kc, vc, _na_bias_table(p["l1_rpb_c"]), tile_r0, tile_nr)
    br1 = jnp.broadcast_to(p["l1_b_router"].astype(F32)[:, None], (N_EXPERTS, LANES))
    x1, h2, idx, gates = _out_proj(1, x, mod1, sid_tok, p["l1_norm2"], [oc], p["l1_w_out"].astype(BF16),
                                   p["l1_w_router"].T, br1)
    return _moe(x1, h2, idx, gates, mod1, sid_move, p["l1_w_gate_up"], p["l1_b_gate_up"], p["l1_w_down"],
                p["l1_b_down"], [a.shape[0] for a in x_parts])


def kernel(x_prompt, x_sample, c_prompt, c_sample, l0_ada_w, l0_ada_b, l0_norm1, l0_w_in, l0_q_norm_a, l0_k_norm_a, l0_sink_a, l0_q_norm_b, l0_k_norm_b, l0_w_out, l0_norm2, l0_w_router, l0_b_router, l0_w_gate_up, l0_b_gate_up, l0_w_down, l0_b_down, l1_ada_w, l1_ada_b, l1_norm1, l1_w_in, l1_q_norm_c, l1_k_norm_c, l1_rpb_c, l1_w_out, l1_norm2, l1_w_router, l1_b_router, l1_w_gate_up, l1_b_gate_up, l1_w_down, l1_b_down):
    p = dict(l0_ada_w=l0_ada_w, l0_ada_b=l0_ada_b, l0_norm1=l0_norm1, l0_w_in=l0_w_in, l0_q_norm_a=l0_q_norm_a,
             l0_k_norm_a=l0_k_norm_a, l0_sink_a=l0_sink_a, l0_q_norm_b=l0_q_norm_b, l0_k_norm_b=l0_k_norm_b,
             l0_w_out=l0_w_out, l0_norm2=l0_norm2, l0_w_router=l0_w_router, l0_b_router=l0_b_router,
             l0_w_gate_up=l0_w_gate_up, l0_b_gate_up=l0_b_gate_up, l0_w_down=l0_w_down, l0_b_down=l0_b_down,
             l1_ada_w=l1_ada_w, l1_ada_b=l1_ada_b, l1_norm1=l1_norm1, l1_w_in=l1_w_in, l1_q_norm_c=l1_q_norm_c,
             l1_k_norm_c=l1_k_norm_c, l1_rpb_c=l1_rpb_c, l1_w_out=l1_w_out, l1_norm2=l1_norm2,
             l1_w_router=l1_w_router, l1_b_router=l1_b_router, l1_w_gate_up=l1_w_gate_up,
             l1_b_gate_up=l1_b_gate_up, l1_w_down=l1_w_down, l1_b_down=l1_b_down)
    nb_p, len_p, d = x_prompt.shape
    nb_s, len_s, _ = x_sample.shape
    outs = []
    for xg, cg in ((x_prompt, c_prompt), (x_sample, c_sample)):
        nb, ln, _ = xg.shape
        c_pad = jnp.concatenate([cg, jnp.zeros((-nb % 8, d), F32)], axis=0)
        (y,) = _trunk([xg.reshape(nb * ln, d)], c_pad, (ln,) * nb, p)
        outs.append(y.reshape(nb, ln, d))
    return tuple(outs)
```

```python
import functools
import math

import numpy as np
import jax
import jax.numpy as jnp
from jax import lax
from jax.experimental import pallas as pl
from jax.experimental.pallas import tpu as pltpu
from jax.experimental.pallas import tpu_sc as plsc

F32 = jnp.float32
BF16 = jnp.bfloat16
HIGHEST = lax.Precision.HIGHEST

D_MODEL = 1024
HEAD_DIM = 64
LANES = 128
GRID_W = 64
A_Q_HEADS = 8
A_KV_HEADS = 2
A_GROUP = A_Q_HEADS // A_KV_HEADS
A_HALF = 128
B_PAIRS = ((128, 1), (512, 4), (2048, 16))
B_GROUPS = 3
B_HEADS = 4
A_QW = A_Q_HEADS * HEAD_DIM
A_KVW = A_KV_HEADS * HEAD_DIM
A_IN = A_QW + 2 * A_KVW
B_GW = B_HEADS * HEAD_DIM
C_HEADS = 16
C_W = C_HEADS * HEAD_DIM
NA_KH = 8
NA_KW = 16
N_EXPERTS = 32
TOP_K = 4
D_FF = 1024
SWIGLU_LIMIT = 7.0
SWIGLU_ALPHA = 1.702
RMS_EPS = 1e-6
NEG_INF = -1e30
LOG2E = math.log2(math.e)
LN2 = math.log(2.0)
QK_SCALE = HEAD_DIM ** -0.5 * LOG2E

TOKEN_TILE = 1024
ATTN_TQ_A = 256
ATTN_TQ_B = 512
ATTN_UQ = 128
NA_TROWS = 8
MOE_BM = 512
ROUTE_TILE = 1024
MOVE_TILE = 256
VMEM_LIMIT = 56 * 1024 * 1024


def _alibi_slopes(n):
    return [float(2.0 ** (-8.0 * (j + 1) / n)) for j in range(n)]


def _cparams(sem, flags=None):
    return pltpu.CompilerParams(dimension_semantics=sem, vmem_limit_bytes=VMEM_LIMIT, flags=flags)


def _mod_kernel(c_ref, w_ref, b_ref, o_ref):
    c = c_ref[...]
    s = c * jax.nn.sigmoid(c)
    o_ref[...] = jnp.dot(s, w_ref[...], precision=HIGHEST, preferred_element_type=F32) + b_ref[...]


def _ada_mod(c_pad, w, b):
    nrow = c_pad.shape[0]
    ncol = w.shape[1]
    tn = D_MODEL
    return pl.pallas_call(
        _mod_kernel,
        grid=(ncol // tn,),
        in_specs=[pl.BlockSpec((nrow, D_MODEL), lambda j: (0, 0)),
                  pl.BlockSpec((D_MODEL, tn), lambda j: (0, j)),
                  pl.BlockSpec((1, tn), lambda j: (0, j))],
        out_specs=pl.BlockSpec((nrow, tn), lambda j: (0, j)),
        out_shape=jax.ShapeDtypeStruct((nrow, ncol), F32),
        compiler_params=_cparams(("arbitrary",)),
        name="ada_mod",
    )(c_pad, w, b.reshape(1, ncol))


def _head_mean_sq(y, bd_ref):
    w = y.shape[1]
    outs = []
    for c0 in range(0, w, 2 * LANES):
        cw = min(2 * LANES, w - c0)
        sq = y[:, c0:c0 + cw] * y[:, c0:c0 + cw]
        outs.append(jnp.dot(sq.astype(BF16), bd_ref[0:cw, 0:cw], preferred_element_type=F32))
    return outs[0] if len(outs) == 1 else jnp.concatenate(outs, axis=1)


def _rms_mod(x, g_ref, mod_ref, shift_row, scale_row):
    ms = jnp.mean(x * x, axis=-1, keepdims=True)
    xn = x * lax.rsqrt(ms + RMS_EPS) * g_ref[...]
    return xn * (1.0 + mod_ref[scale_row:scale_row + 1, :]) + mod_ref[shift_row:shift_row + 1, :]


def _part_tiles(parts, tile):
    starts, s = [], 0
    for a in parts:
        assert a.shape[0] % tile == 0
        starts.append(s)
        s += a.shape[0] // tile
    return tuple(starts), s


def _part_specs(parts, starts, tile, ncol):
    def spec(a, s0):
        n = a.shape[0] // tile
        return pl.BlockSpec((tile, ncol), lambda i, *_: (jnp.clip(i - s0, 0, n - 1), 0))
    return [spec(a, s0) for a, s0 in zip(parts, starts)]


def _read_part(refs, starts):
    i = pl.program_id(0)
    x = refs[0][...]
    for ref, s0 in zip(refs[1:], starts[1:]):
        x = jnp.where(i >= s0, ref[...], x)
    return x


def _in_kernel(sid_ref, *refs, slabs, tm, part_starts):
    del sid_ref
    npart = len(part_starts)
    x_refs = refs[:npart]
    mod_ref, n1_ref, w_ref, gain_ref, bd_ref = refs[npart:npart + 5]
    rest = refs[npart + 5:]
    nout = len(slabs)
    out_refs = rest[:nout]
    scr_ref = rest[nout] if len(rest) > nout else None
    h = _rms_mod(_read_part(x_refs, part_starts), n1_ref, mod_ref, 0, 1).astype(BF16)
    proj = jnp.dot(h, w_ref[...], preferred_element_type=F32)
    for (c0, w, normed, dil), o_ref in zip(slabs, out_refs):
        y = proj[:, c0:c0 + w]
        if normed:
            y = y * lax.rsqrt(_head_mean_sq(y, bd_ref) + RMS_EPS) * gain_ref[:, c0:c0 + w]
        if dil == 1:
            o_ref[...] = y.astype(BF16)
        else:
            for c in range(w // LANES):
                scr_ref[c] = y[:, c * LANES:(c + 1) * LANES]
            for r in range(dil):
                for c in range(w // LANES):
                    o_ref[:, r * w + c * LANES:r * w + (c + 1) * LANES] = (
                        scr_ref[c, pl.ds(r, tm // dil, stride=dil), :].astype(BF16))


def _in_proj(x_parts, mod, tile_sid, n1, w_bf, gain, bd, slabs):
    tm = TOKEN_TILE
    part_starts, ntiles = _part_tiles(x_parts, tm)
    t = ntiles * tm
    ncols = w_bf.shape[1]
    out_shapes, out_specs = [], []
    for (c0, w, normed, dil) in slabs:
        out_shapes.append(jax.ShapeDtypeStruct((t // dil, dil * w), BF16))
        out_specs.append(pl.BlockSpec((tm // dil, dil * w), lambda i, sid: (i, 0)))
    need_scr = any(s[3] > 1 for s in slabs)
    scratch = [pltpu.VMEM((max(s[1] for s in slabs if s[3] > 1) // LANES, tm, LANES), F32)] if need_scr else []
    gs = pltpu.PrefetchScalarGridSpec(
        num_scalar_prefetch=1,
        grid=(t // tm,),
        in_specs=_part_specs(x_parts, part_starts, tm, D_MODEL) + [
            pl.BlockSpec((None, 6, D_MODEL), lambda i, sid: (sid[i], 0, 0)),
            pl.BlockSpec((1, D_MODEL), lambda i, sid: (0, 0)),
            pl.BlockSpec((D_MODEL, ncols), lambda i, sid: (0, 0)),
            pl.BlockSpec((1, ncols), lambda i, sid: (0, 0)),
            pl.BlockSpec((2 * LANES, 2 * LANES), lambda i, sid: (0, 0))],
        out_specs=out_specs,
        scratch_shapes=scratch)
    return pl.pallas_call(
        functools.partial(_in_kernel, slabs=slabs, tm=tm, part_starts=part_starts),
        grid_spec=gs,
        out_shape=out_shapes,
        compiler_params=_cparams(("arbitrary",)),
        name="in_proj",
    )(tile_sid, *x_parts, mod, n1.reshape(1, D_MODEL), w_bf, gain, bd)


def _band_kernel(s0_ref, s1_ref, *refs, tq, hb, half, heads, kv_group, has_sink, want_lse):
    pos = 0
    sink_ref = None
    if has_sink:
        sink_ref = refs[0]
        pos = 1
    q_ref, kp_ref, kc_ref, kn_ref, vp_ref, vc_ref, vn_ref = refs[pos:pos + 7]
    o_ref = refs[pos + 7]
    lse_ref = refs[pos + 8] if want_lse else None
    i = pl.program_id(1)
    uq = ATTN_UQ
    wlen = uq + 2 * hb
    kall = jnp.concatenate([kp_ref[...], kc_ref[...], kn_ref[...]], axis=0)
    vall = jnp.concatenate([vp_ref[...], vc_ref[...], vn_ref[...]], axis=0)
    lane = lax.broadcasted_iota(jnp.int32, (uq, LANES), 1)
    lo_lanes = lane < HEAD_DIM
    for sub in range(tq // uq):
        rows = slice(sub * uq, (sub + 1) * uq)
        kwin = kall[sub * uq:sub * uq + wlen]
        vwin = vall[sub * uq:sub * uq + wlen]
        q0 = i * tq + sub * uq
        kpos = q0 - hb + lax.broadcasted_iota(jnp.int32, (wlen, uq), 0)
        qpos = q0 + lax.broadcasted_iota(jnp.int32, (wlen, uq), 1)
        rel = kpos - qpos
        valid = (jnp.abs(rel) <= half) & (kpos >= s0_ref[i]) & (kpos < s1_ref[i])
        distm = jnp.where(valid, jnp.abs(rel).astype(F32), 1e32)
        lse_rows = []
        for g, pair in enumerate(heads):
            qg = q_ref[rows, g * LANES:(g + 1) * LANES]
            kg = kwin[:, kv_group[g] * LANES:(kv_group[g] + 1) * LANES]
            vg = vwin[:, kv_group[g] * LANES:(kv_group[g] + 1) * LANES]
            outs = []
            for s, (slope, sink_idx) in enumerate(pair):
                keep = lo_lanes if s == 0 else jnp.logical_not(lo_lanes)
                qm = jnp.where(keep, qg, jnp.zeros_like(qg))
                sc = lax.dot_general(kg, qm, (((1,), (1,)), ((), ())), preferred_element_type=F32)
                sc = sc - (slope * LOG2E) * distm
                m = jnp.max(sc, axis=0, keepdims=True)
                if has_sink:
                    sink2 = sink_ref[sink_idx] * LOG2E
                    m = jnp.maximum(m, sink2)
                e = jnp.exp2(sc - m)
                den = jnp.sum(e, axis=0, keepdims=True)
                if has_sink:
                    den = den + jnp.exp2(sink2 - m)
                p = (e * (1.0 / den)).astype(BF16)
                outs.append(lax.dot_general(p, vg, (((0,), (0,)), ((), ())), preferred_element_type=F32))
                if want_lse:
                    lse_rows.append((m + jnp.log2(den)) * LN2)
            o_ref[rows, g * LANES:(g + 1) * LANES] = jnp.where(lo_lanes, outs[0], outs[1]).astype(o_ref.dtype)
        if want_lse:
            lse_t = jnp.concatenate(lse_rows + [jnp.zeros((LANES - len(lse_rows), uq), F32)], axis=0)
            lse_ref[rows, :] = lse_t.T


def _band_attn(q, k, v, seq_rows, *, tq, dil, qw, kvw, half, heads, kv_group, sink, out_dtype, want_lse):
    rows = q.shape[0]
    hb = half
    s0 = _tile_table(seq_rows, tq, lambda sid, s, n, j: s)
    s1 = _tile_table(seq_rows, tq, lambda sid, s, n, j: s + n)
    per = tq // hb
    nt = rows // tq
    nhb = rows // hb
    has_sink = sink is not None
    in_specs = []
    args = []
    if has_sink:
        in_specs.append(pl.BlockSpec(memory_space=pltpu.SMEM))
        args.append(sink)
    qmap = lambda r, i, a, b: (i, r)
    pmap = lambda r, i, a, b: (jnp.maximum(i * per - 1, 0), r)
    nmap = lambda r, i, a, b: (jnp.minimum((i + 1) * per, nhb - 1), r)
    in_specs += [pl.BlockSpec((tq, qw), qmap),
                 pl.BlockSpec((hb, kvw), pmap), pl.BlockSpec((tq, kvw), qmap), pl.BlockSpec((hb, kvw), nmap),
                 pl.BlockSpec((hb, kvw), pmap), pl.BlockSpec((tq, kvw), qmap), pl.BlockSpec((hb, kvw), nmap)]
    args += [q, k, k, k, v, v, v]
    out_shape = [jax.ShapeDtypeStruct((rows, dil * qw), out_dtype)]
    out_specs = [pl.BlockSpec((tq, qw), qmap)]
    if want_lse:
        out_shape.append(jax.ShapeDtypeStruct((rows, dil * LANES), F32))
        out_specs.append(pl.BlockSpec((tq, LANES), qmap))
    gs = pltpu.PrefetchScalarGridSpec(num_scalar_prefetch=2, grid=(dil, nt),
                                      in_specs=in_specs, out_specs=out_specs)
    res = pl.pallas_call(
        functools.partial(_band_kernel, tq=tq, hb=hb, half=half, heads=heads, kv_group=kv_group,
                          has_sink=has_sink, want_lse=want_lse),
        grid_spec=gs,
        out_shape=out_shape,
        compiler_params=_cparams(("arbitrary", "arbitrary")),
        name="band_attn_d%d" % dil,
    )(s0, s1, *args)
    return res


def _na_kernel(r0_ref, nr_ref, q_ref, kp_ref, kc_ref, kn_ref, vp_ref, vc_ref, vn_ref, bias_ref,
               o_ref, kcat, vcat, *, halo):
    j = pl.program_id(0)
    tq = NA_TROWS * GRID_W
    hrows = halo * GRID_W
    kcat[0:hrows, :] = kp_ref[...]
    kcat[hrows:hrows + tq, :] = kc_ref[...]
    kcat[hrows + tq:, :] = kn_ref[...]
    vcat[0:hrows, :] = vp_ref[...]
    vcat[hrows:hrows + tq, :] = vc_ref[...]
    vcat[hrows + tq:, :] = vn_ref[...]
    r0 = r0_ref[j]
    nrows = nr_ref[j]
    kwin_len = NA_KH * GRID_W
    lane = lax.broadcasted_iota(jnp.int32, (GRID_W, LANES), 1)
    lo_lanes = lane < HEAD_DIM
    nrel = 2 * NA_KH - 1

    def row_body(u, carry):
        r = r0 + u
        rs = jnp.clip(r - NA_KH // 2, 0, nrows - NA_KH)
        off = pl.multiple_of((rs - r0 + halo) * GRID_W, GRID_W)
        bvar = rs - r + NA_KH - 1
        qrow = pl.multiple_of(u * GRID_W, GRID_W)
        for g in range(C_HEADS // 2):
            qg = q_ref[pl.ds(qrow, GRID_W), g * LANES:(g + 1) * LANES]
            qs = jnp.concatenate([jnp.where(lo_lanes, qg, jnp.zeros_like(qg)),
                                  jnp.where(lo_lanes, jnp.zeros_like(qg), qg)], axis=0)
            kg = kcat[pl.ds(off, kwin_len), g * LANES:(g + 1) * LANES]
            vg = vcat[pl.ds(off, kwin_len), g * LANES:(g + 1) * LANES]
            sc = lax.dot_general(kg, qs, (((1,), (1,)), ((), ())), preferred_element_type=F32)
            sc = sc + jnp.concatenate([bias_ref[g * nrel + bvar + kr] for kr in range(NA_KH)], axis=0)
            m = jnp.max(sc, axis=0, keepdims=True)
            e = jnp.exp2(sc - m)
            den = jnp.sum(e, axis=0, keepdims=True)
            p = (e * (1.0 / den)).astype(BF16)
            pv = lax.dot_general(p, vg, (((0,), (0,)), ((), ())), preferred_element_type=F32)
            o = jnp.where(lo_lanes, pv[0:GRID_W], pv[GRID_W:])
            o_ref[pl.ds(qrow, GRID_W), g * LANES:(g + 1) * LANES] = o.astype(o_ref.dtype)
        return carry

    lax.fori_loop(0, NA_TROWS, row_body, 0)


def _na_attn(q, k, v, bias_tab, tile_r0, tile_nr):
    t = q.shape[0]
    halo = NA_KH // 2
    tq = NA_TROWS * GRID_W
    hrows = halo * GRID_W
    per = tq // hrows
    nhb = t // hrows
    nt = t // tq
    qmap = lambda j, a, b: (j, 0)
    pmap = lambda j, a, b: (jnp.maximum(j * per - 1, 0), 0)
    nmap = lambda j, a, b: (jnp.minimum((j + 1) * per, nhb - 1), 0)
    gs = pltpu.PrefetchScalarGridSpec(
        num_scalar_prefetch=2, grid=(nt,),
        in_specs=[pl.BlockSpec((tq, C_W), qmap),
                  pl.BlockSpec((hrows, C_W), pmap), pl.BlockSpec((tq, C_W), qmap), pl.BlockSpec((hrows, C_W), nmap),
                  pl.BlockSpec((hrows, C_W), pmap), pl.BlockSpec((tq, C_W), qmap), pl.BlockSpec((hrows, C_W), nmap),
                  pl.BlockSpec(bias_tab.shape, lambda j, a, b: (0, 0, 0))],
        out_specs=pl.BlockSpec((tq, C_W), qmap),
        scratch_shapes=[pltpu.VMEM((tq + 2 * hrows, C_W), BF16), pltpu.VMEM((tq + 2 * hrows, C_W), BF16)])
    return pl.pallas_call(
        functools.partial(_na_kernel, halo=halo),
        grid_spec=gs,
        out_shape=jax.ShapeDtypeStruct((t, C_W), BF16),
        compiler_params=_cparams(("arbitrary",)),
        name="na_attn",
    )(tile_r0, tile_nr, q, k, k, k, v, v, v, bias_tab)


def _na_bias_table(rpb):
    c = np.arange(GRID_W)
    cs = np.clip(c - NA_KW // 2, 0, GRID_W - NA_KW)
    kc = np.arange(GRID_W)
    valid = (kc[:, None] >= cs[None, :]) & (kc[:, None] < cs[None, :] + NA_KW)
    cidx = np.clip(kc[:, None] - c[None, :] + NA_KW - 1, 0, 2 * NA_KW - 2)
    tab = rpb.astype(F32)[:, :, cidx] * LOG2E
    tab = jnp.where(jnp.asarray(valid)[None, None], tab, NEG_INF)
    pairs = jnp.concatenate([tab[0::2], tab[1::2]], axis=-1)
    return pairs.reshape((C_HEADS // 2) * (2 * NA_KH - 1), GRID_W, LANES)


def _route(h2, wr_ref, br_ref, idx_ref, gate_ref):
    tm = h2.shape[0]
    logits = lax.dot_general(wr_ref[...], h2, (((1,), (1,)), ((), ())),
                             precision=HIGHEST, preferred_element_type=F32) + br_ref[:, 0:1]
    eid = lax.broadcasted_iota(jnp.int32, (N_EXPERTS, tm), 0)
    vals, idxs = [], []
    for _ in range(TOP_K):
        m = jnp.max(logits, axis=0, keepdims=True)
        ix = jnp.min(jnp.where(logits == m, eid, N_EXPERTS), axis=0, keepdims=True)
        vals.append(m)
        idxs.append(ix)
        logits = jnp.where(eid == ix, -jnp.inf, logits)
    es = [jnp.exp(vk - vals[0]) for vk in vals]
    den = es[0] + es[1] + es[2] + es[3]
    pad_i = jnp.zeros((8 - TOP_K, tm), jnp.int32)
    pad_f = jnp.zeros((8 - TOP_K, tm), F32)
    idx_ref[...] = jnp.concatenate(idxs + [pad_i], axis=0)
    gate_ref[...] = jnp.concatenate([ek / den for ek in es] + [pad_f], axis=0)


PACK_W = D_MODEL // 2


def _pack_bf16_rows(h):
    bits = pltpu.bitcast(h.astype(BF16).astype(F32), jnp.int32)
    return (bits[:, :PACK_W] & jnp.int32(-65536)) | lax.shift_right_logical(bits[:, PACK_W:], 16)


def _unpack_rows_f32(w):
    hi = pltpu.bitcast(w & jnp.int32(-65536), F32)
    lo = pltpu.bitcast(lax.shift_left(w, 16), F32)
    return jnp.concatenate([hi, lo], axis=1)


def _unpack_bf16_rows(w):
    return _unpack_rows_f32(w).astype(BF16)


def _out0_kernel(sid_ref, *refs, tm, part_starts):
    del sid_ref
    npart = len(part_starts)
    x_refs = refs[:npart]
    (mod_ref, n2_ref, oa_ref, ob0_ref, ls0_ref, ob1_ref, ls1_ref, ob2_ref, ls2_ref,
     wout_ref, wr_ref, br_ref, x1_ref, h2_ref, idx_ref, gate_ref, oscr, lscr) = refs[npart:]
    os_, ls_ = [], []
    for gi, (o_ref, l_ref) in enumerate(((ob0_ref, ls0_ref), (ob1_ref, ls1_ref), (ob2_ref, ls2_ref))):
        dil = B_PAIRS[gi][1]
        if dil == 1:
            os_.append(o_ref[...])
            ls_.append(l_ref[...])
        else:
            ncg = B_GW // LANES
            for r in range(dil):
                for c in range(ncg):
                    lo = r * B_GW + c * LANES
                    oscr[gi * ncg + c, pl.ds(r, tm // dil, stride=dil), :] = o_ref[:, lo:lo + LANES]
                lscr[gi, pl.ds(r, tm // dil, stride=dil), :] = l_ref[:, r * LANES:(r + 1) * LANES]
            os_.append(jnp.concatenate([oscr[gi * ncg + c] for c in range(ncg)], axis=1))
            ls_.append(lscr[gi])
    lmax = jnp.maximum(jnp.maximum(ls_[0], ls_[1]), ls_[2])
    ws = [jnp.exp(l - lmax) for l in ls_]
    winv = 1.0 / (ws[0] + ws[1] + ws[2])
    lane = lax.broadcasted_iota(jnp.int32, (tm, LANES), 1)
    lo_lanes = lane < HEAD_DIM

    def head_weights(w):
        return jnp.concatenate(
            [jnp.where(lo_lanes, jnp.broadcast_to(w[:, 2 * pr:2 * pr + 1], (tm, LANES)),
                       jnp.broadcast_to(w[:, 2 * pr + 1:2 * pr + 2], (tm, LANES)))
             for pr in range(B_HEADS // 2)], axis=1)

    ob = (head_weights(ws[0] * winv) * os_[0] + head_weights(ws[1] * winv) * os_[1]
          + head_weights(ws[2] * winv) * os_[2])
    o = (jnp.dot(oa_ref[...], wout_ref[0:A_QW, :], preferred_element_type=F32)
         + jnp.dot(ob.astype(BF16), wout_ref[A_QW:, :], preferred_element_type=F32))
    x1 = _read_part(x_refs, part_starts) + mod_ref[2:3, :] * o
    x1_ref[...] = x1
    h2 = _rms_mod(x1, n2_ref, mod_ref, 3, 4)
    h2_ref[...] = _pack_bf16_rows(h2)
    _route(h2, wr_ref, br_ref, idx_ref, gate_ref)


def _out1_kernel(sid_ref, *refs, part_starts):
    del sid_ref
    npart = len(part_starts)
    x_refs = refs[:npart]
    mod_ref, n2_ref, oc_ref, wout_ref, wr_ref, br_ref, x1_ref, h2_ref, idx_ref, gate_ref = refs[npart:]
    o = jnp.dot(oc_ref[...], wout_ref[...], preferred_element_type=F32)
    x1 = _read_part(x_refs, part_starts) + mod_ref[2:3, :] * o
    x1_ref[...] = x1
    h2 = _rms_mod(x1, n2_ref, mod_ref, 3, 4)
    h2_ref[...] = _pack_bf16_rows(h2)
    _route(h2, wr_ref, br_ref, idx_ref, gate_ref)


def _out_proj(layer, x_parts, mod, tile_sid, n2, attn, wout_bf, wr_t, br):
    tm = TOKEN_TILE
    part_starts, ntiles = _part_tiles(x_parts, tm)
    t = ntiles * tm
    row = lambda i, sid: (i, 0)
    const = lambda i, sid: (0, 0)
    in_specs = _part_specs(x_parts, part_starts, tm, D_MODEL) + [
        pl.BlockSpec((None, 6, D_MODEL), lambda i, sid: (sid[i], 0, 0)),
        pl.BlockSpec((1, D_MODEL), const)]
    scratch = []
    if layer == 0:
        in_specs.append(pl.BlockSpec((tm, A_QW), row))
        for (_, dil) in B_PAIRS:
            in_specs += [pl.BlockSpec((tm // dil, dil * B_GW), row), pl.BlockSpec((tm // dil, dil * LANES), row)]
        body = functools.partial(_out0_kernel, tm=tm, part_starts=part_starts)
        nscr = B_GROUPS * B_GW // LANES
        scratch = [pltpu.VMEM((nscr, tm, LANES), F32), pltpu.VMEM((B_GROUPS, tm, LANES), F32)]
    else:
        in_specs.append(pl.BlockSpec((tm, C_W), row))
        body = functools.partial(_out1_kernel, part_starts=part_starts)
    in_specs += [pl.BlockSpec(wout_bf.shape, const),
                 pl.BlockSpec((N_EXPERTS, D_MODEL), const),
                 pl.BlockSpec((N_EXPERTS, LANES), const)]
    gs = pltpu.PrefetchScalarGridSpec(
        num_scalar_prefetch=1, grid=(t // tm,), in_specs=in_specs,
        out_specs=[pl.BlockSpec((tm, D_MODEL), row), pl.BlockSpec((tm, PACK_W), row),
                   pl.BlockSpec((8, tm), lambda i, sid: (0, i)), pl.BlockSpec((8, tm), lambda i, sid: (0, i))],
        scratch_shapes=scratch)
    return pl.pallas_call(
        body, grid_spec=gs,
        out_shape=[jax.ShapeDtypeStruct((t, D_MODEL), F32), jax.ShapeDtypeStruct((t, PACK_W), jnp.int32),
                   jax.ShapeDtypeStruct((8, t), jnp.int32), jax.ShapeDtypeStruct((8, t), F32)],
        compiler_params=_cparams(("arbitrary",)),
        name="out_proj%d" % layer,
    )(tile_sid, *x_parts, mod, n2.reshape(1, D_MODEL), *attn, wout_bf, wr_t, br)


def _rank_kernel(idx_ref, tri_ref, rank_ref, cnt_ref, carry):
    i = pl.program_id(0)

    @pl.when(i == 0)
    def _():
        carry[...] = jnp.zeros_like(carry)

    tk = idx_ref.shape[1]
    eid = lax.broadcasted_iota(jnp.int32, (N_EXPERTS, tk), 0)
    hits = [eid == idx_ref[k:k + 1, :] for k in range(TOP_K)]
    onehot = sum(h.astype(F32) for h in hits)
    incl = jnp.dot(onehot.astype(BF16), tri_ref[...], preferred_element_type=F32)
    before = incl - onehot + carry[:, 0:1]
    rows = [jnp.sum(jnp.where(h, before, 0.0), axis=0, keepdims=True) for h in hits]
    rows.append(jnp.zeros((8 - TOP_K, tk), F32))
    rank_ref[...] = jnp.concatenate(rows, axis=0).astype(jnp.int32)
    carry[...] = carry[...] + incl[:, tk - 1:tk]
    cnt_ref[...] = carry[...].astype(jnp.int32)


def _slot_kernel(idx_ref, rank_ref, start_ref, slot_ref):
    tk = idx_ref.shape[1]
    eid = lax.broadcasted_iota(jnp.int32, (N_EXPERTS, tk), 0)
    start = start_ref[:, 0:1]
    rows = []
    for k in range(TOP_K):
        base = jnp.sum(jnp.where(eid == idx_ref[k:k + 1, :], start, 0), axis=0, keepdims=True)
        rows.append(base + rank_ref[k:k + 1, :])
    rows.append(jnp.zeros((8 - TOP_K, tk), jnp.int32))
    slot_ref[...] = jnp.concatenate(rows, axis=0)


def _route_slots(idx):
    t = idx.shape[1]
    tk = ROUTE_TILE
    tri = jnp.asarray(np.triu(np.ones((tk, tk), np.float32)), BF16)
    tok = lambda i: (0, i)
    rank, cnt = pl.pallas_call(
        _rank_kernel, grid=(t // tk,),
        in_specs=[pl.BlockSpec((8, tk), tok), pl.BlockSpec((tk, tk), lambda i: (0, 0))],
        out_specs=[pl.BlockSpec((8, tk), tok), pl.BlockSpec((N_EXPERTS, LANES), lambda i: (0, 0))],
        out_shape=[jax.ShapeDtypeStruct((8, t), jnp.int32), jax.ShapeDtypeStruct((N_EXPERTS, LANES), jnp.int32)],
        scratch_shapes=[pltpu.VMEM((N_EXPERTS, LANES), F32)],
        compiler_params=_cparams(("arbitrary",)),
        name="route_rank",
    )(idx, tri)
    counts = cnt[:, 0]
    padded = (counts + MOE_BM - 1) // MOE_BM * MOE_BM
    pad_end = jnp.cumsum(padded)
    pad_start = pad_end - padded
    nblk = (t * TOP_K) // MOE_BM + N_EXPERTS
    blk_row0 = jnp.arange(nblk, dtype=jnp.int32) * MOE_BM
    blk_expert = jnp.minimum(jnp.sum((pad_end[None, :] <= blk_row0[:, None]).astype(jnp.int32), axis=1),
                             N_EXPERTS - 1)
    nused = (pad_end[-1] // MOE_BM).astype(jnp.int32).reshape(1)
    eids = jnp.arange(N_EXPERTS, dtype=jnp.int32)
    later = (eids[None, :] > eids[:, None]) & (padded[None, :] > 0)
    next_expert = jnp.min(jnp.where(later, eids[None, :], N_EXPERTS), axis=1)
    next_expert = jnp.where(next_expert == N_EXPERTS, -1, next_expert).astype(jnp.int32)
    blk_next = next_expert[blk_expert]
    start_b = jnp.broadcast_to(pad_start.astype(jnp.int32)[:, None], (N_EXPERTS, LANES))
    slot = pl.pallas_call(
        _slot_kernel, grid=(t // tk,),
        in_specs=[pl.BlockSpec((8, tk), tok), pl.BlockSpec((8, tk), tok),
                  pl.BlockSpec((N_EXPERTS, LANES), lambda i: (0, 0))],
        out_specs=pl.BlockSpec((8, tk), tok),
        out_shape=jax.ShapeDtypeStruct((8, t), jnp.int32),
        compiler_params=_cparams(("arbitrary",)),
        name="route_slot",
    )(idx, rank, start_b)
    return slot, blk_expert, nused, blk_next


SC_CORES = 2
SC_SUBCORES = 16
SC_ROW_BUFFER_BYTES = 256 * 1024
SC_MAX_INDEX_LIST = 128


def _sc_chunks(t, row_bytes):
    chunk = min(SC_MAX_INDEX_LIST, SC_ROW_BUFFER_BYTES // row_bytes)
    nchunk = t // chunk
    per = nchunk // (SC_CORES * SC_SUBCORES)
    assert per * SC_CORES * SC_SUBCORES * chunk == t
    return chunk, nchunk, per


def _sc_slot_lists(slot, nchunk, chunk):
    return slot[:TOP_K].reshape(TOP_K, nchunk, chunk).transpose(1, 0, 2)


def _sc_dispatch(h2, slot, nslots):
    t, w = h2.shape
    chunk, nchunk, per = _sc_chunks(t, w * h2.dtype.itemsize)
    mesh = plsc.VectorSubcoreMesh(core_axis_name="c", subcore_axis_name="s")

    @functools.partial(
        pl.kernel, mesh=mesh, out_type=jax.ShapeDtypeStruct((nslots, w), h2.dtype),
        scratch_types=[pltpu.VMEM((TOP_K, chunk), jnp.int32), pltpu.VMEM((chunk, w), h2.dtype)],
        name="moe_dispatch_sc")
    def body(h_hbm, slot_hbm, xs_hbm, idx_v, rows_v):
        wid = lax.axis_index("s") * SC_CORES + lax.axis_index("c")

        @pl.loop(0, per)
        def _(j):
            c = wid * per + j
            pltpu.sync_copy(slot_hbm.at[c], idx_v)
            pltpu.sync_copy(h_hbm.at[pl.ds(c * chunk, chunk)], rows_v)
            for k in range(TOP_K):
                pltpu.sync_copy(rows_v, xs_hbm.at[idx_v.at[k]])

    return body(h2, _sc_slot_lists(slot, nchunk, chunk))


def _sc_gather(ys, slot, t):
    w = ys.shape[1]
    nbuf = 2
    chunk, nchunk, per = _sc_chunks(t, nbuf * w * ys.dtype.itemsize)
    mesh = plsc.VectorSubcoreMesh(core_axis_name="c", subcore_axis_name="s")

    @functools.partial(
        pl.kernel, mesh=mesh, out_type=jax.ShapeDtypeStruct((TOP_K, t, w), ys.dtype),
        scratch_types=[pltpu.VMEM((TOP_K, chunk), jnp.int32), pltpu.VMEM((nbuf, chunk, w), ys.dtype),
                       pltpu.SemaphoreType.DMA((nbuf,))],
        name="moe_gather_sc")
    def body(ys_hbm, slot_hbm, out_hbm, idx_v, rows_v, wsem):
        wid = lax.axis_index("s") * SC_CORES + lax.axis_index("c")

        @pl.loop(0, per)
        def _(j):
            c = wid * per + j
            pltpu.sync_copy(slot_hbm.at[c], idx_v)
            writes = []
            for k in range(TOP_K):
                b = k % nbuf
                if k >= nbuf:
                    writes[k - nbuf].wait()
                pltpu.sync_copy(ys_hbm.at[idx_v.at[k]], rows_v.at[b])
                writes.append(pltpu.async_copy(rows_v.at[b], out_hbm.at[k, pl.ds(c * chunk, chunk)], wsem.at[b]))
            for wr in writes[-nbuf:]:
                wr.wait()

    return body(ys, _sc_slot_lists(slot, nchunk, chunk))


def _ffn_kernel(be_ref, nu_ref, nxt_ref, xs_ref, wgu_hbm, bgu_ref, wd_hbm, bd_ref, y_ref,
                wgu_f, wd_f, wgu_bf, wd_bf, wsem):
    i = pl.program_id(0)
    prev = be_ref[jnp.maximum(i - 1, 0)]

    def weight_copies(e):
        return (pltpu.make_async_copy(wgu_hbm.at[e], wgu_f, wsem.at[0]),
                pltpu.make_async_copy(wd_hbm.at[e], wd_f, wsem.at[1]))

    @pl.when(i == 0)
    def _():
        for cp in weight_copies(be_ref[0]):
            cp.start()

    @pl.when(i < nu_ref[0])
    def _():
        @pl.when((i == 0) | (be_ref[i] != prev))
        def _():
            for cp in weight_copies(be_ref[i]):
                cp.wait()
            wgu_bf[...] = wgu_f[...].astype(BF16)
            wd_bf[...] = wd_f[...].astype(BF16)

            @pl.when(nxt_ref[i] >= 0)
            def _():
                for cp in weight_copies(nxt_ref[i]):
                    cp.start()

        x = _unpack_bf16_rows(xs_ref[...])
        gu = jnp.dot(x, wgu_bf[...], preferred_element_type=F32) + bgu_ref[...]
        gate = jnp.minimum(gu[:, :D_FF], SWIGLU_LIMIT)
        up = jnp.clip(gu[:, D_FF:], -SWIGLU_LIMIT, SWIGLU_LIMIT)
        act = (up + 1.0) * gate * jax.nn.sigmoid(SWIGLU_ALPHA * gate)
        y = jnp.dot(act.astype(BF16), wd_bf[...], preferred_element_type=F32) + bd_ref[...]
        y_ref[...] = _pack_bf16_rows(y)

    @pl.when(i >= nu_ref[0])
    def _():
        y_ref[...] = jnp.zeros_like(y_ref)


def _expert_ffn(xs, blk_expert, nused, blk_next, wgu, bgu, wd, bd):
    nslots = xs.shape[0]
    nblk = nslots // MOE_BM
    blk = lambda i, be, nu, nx: (jnp.minimum(i, nu[0] - 1), 0)
    exp3 = lambda i, be, nu, nx: (be[jnp.minimum(i, nu[0] - 1)], 0, 0)
    gs = pltpu.PrefetchScalarGridSpec(
        num_scalar_prefetch=3, grid=(nblk,),
        in_specs=[pl.BlockSpec((MOE_BM, PACK_W), blk),
                  pl.BlockSpec(memory_space=pl.ANY),
                  pl.BlockSpec((None, 1, 2 * D_FF), exp3),
                  pl.BlockSpec(memory_space=pl.ANY),
                  pl.BlockSpec((None, 1, D_MODEL), exp3)],
        out_specs=pl.BlockSpec((MOE_BM, PACK_W), lambda i, be, nu, nx: (i, 0)),
        scratch_shapes=[pltpu.VMEM((D_MODEL, 2 * D_FF), F32), pltpu.VMEM((D_FF, D_MODEL), F32),
                        pltpu.VMEM((D_MODEL, 2 * D_FF), BF16), pltpu.VMEM((D_FF, D_MODEL), BF16),
                        pltpu.SemaphoreType.DMA((2,))])
    return pl.pallas_call(
        _ffn_kernel, grid_spec=gs,
        out_shape=jax.ShapeDtypeStruct((nslots, PACK_W), jnp.int32),
        compiler_params=_cparams(("arbitrary",)),
        name="moe_ffn",
    )(blk_expert, nused, blk_next, xs, wgu, bgu.reshape(N_EXPERTS, 1, 2 * D_FF), wd,
      bd.reshape(N_EXPERTS, 1, D_MODEL))


def _combine_kernel(sid_ref, x_ref, mod_ref, gate_ref, yk_ref, *o_refs, tm, out_starts):
    del sid_ref
    gpad = jnp.concatenate([gate_ref[...], jnp.zeros((LANES - 8, tm), F32)], axis=0)
    gcol = gpad.T
    acc = gcol[:, 0:1] * _unpack_rows_f32(yk_ref[0])
    for k in range(1, TOP_K):
        acc = acc + gcol[:, k:k + 1] * _unpack_rows_f32(yk_ref[k])
    out = x_ref[...] + mod_ref[5:6, :] * acc
    i = pl.program_id(0)
    ends = out_starts[1:] + (None,)
    for o_ref, s0, s1 in zip(o_refs, out_starts, ends):
        mine = (i >= s0) if s1 is None else ((i >= s0) & (i < s1))

        @pl.when(mine)
        def _():
            o_ref[...] = out


def _combine(x1, mod, tile_sid, gates, yk, out_rows):
    t = x1.shape[0]
    tm = MOVE_TILE
    out_shape = [jax.ShapeDtypeStruct((n, D_MODEL), F32) for n in out_rows]
    out_starts, ntiles = _part_tiles(out_shape, tm)
    assert ntiles * tm == t
    gs = pltpu.PrefetchScalarGridSpec(
        num_scalar_prefetch=1, grid=(t // tm,),
        in_specs=[pl.BlockSpec((tm, D_MODEL), lambda i, sid: (i, 0)),
                  pl.BlockSpec((None, 6, D_MODEL), lambda i, sid: (sid[i], 0, 0)),
                  pl.BlockSpec((8, tm), lambda i, sid: (0, i)),
                  pl.BlockSpec((TOP_K, tm, PACK_W), lambda i, sid: (0, i, 0))],
        out_specs=_part_specs(out_shape, out_starts, tm, D_MODEL))
    return pl.pallas_call(
        functools.partial(_combine_kernel, tm=tm, out_starts=out_starts), grid_spec=gs,
        out_shape=out_shape,
        compiler_params=_cparams(("arbitrary",)),
        name="moe_combine",
    )(tile_sid, x1, mod, gates, yk)


def _moe(x1, h2, idx, gates, mod, tile_sid_move, wgu, bgu, wd, bd, out_rows):
    t = x1.shape[0]
    nslots = t * TOP_K + N_EXPERTS * MOE_BM
    slot, blk_expert, nused, blk_next = _route_slots(idx)
    xs = _sc_dispatch(h2, slot, nslots)
    ys = _expert_ffn(xs, blk_expert, nused, blk_next, wgu, bgu, wd, bd)
    return _combine(x1, mod, tile_sid_move, gates, _sc_gather(ys, slot, t), out_rows)


def _tile_table(seq_lens, tile, fn):
    vals = []
    start = 0
    for sid, n in enumerate(seq_lens):
        assert n % tile == 0
        for j in range(n // tile):
            vals.append(fn(sid, start, n, j))
        start += n
    return jnp.asarray(np.asarray(vals, np.int32))


def _head_block_diag():
    bd = np.kron(np.eye(2 * LANES // HEAD_DIM, dtype=np.float32),
                 np.full((HEAD_DIM, HEAD_DIM), 1.0 / HEAD_DIM, np.float32))
    return jnp.asarray(bd, BF16)


_A_HEAD_ORDER = [kv * A_GROUP + g for g in range(A_GROUP) for kv in range(A_KV_HEADS)]


def _trunk(x_parts, mods, mod_row0, seq_lens, p):
    t = sum(a.shape[0] for a in x_parts)
    sid_tok = _tile_table(seq_lens, TOKEN_TILE, lambda sid, s, n, j: mod_row0 + sid)
    sid_move = _tile_table(seq_lens, MOVE_TILE, lambda sid, s, n, j: mod_row0 + sid)
    bd = _head_block_diag()
    mod0, mod1 = mods

    w_in = p["l0_w_in"]
    a_cols = np.concatenate([np.arange(h * HEAD_DIM, (h + 1) * HEAD_DIM) for h in _A_HEAD_ORDER])
    bq, bk, bv = A_IN, A_IN + B_GROUPS * B_GW, A_IN + 2 * B_GROUPS * B_GW
    col_blocks = [w_in[:, a_cols], w_in[:, A_QW:A_IN]]
    for g in range(B_GROUPS):
        col_blocks += [w_in[:, bq + g * B_GW:bq + (g + 1) * B_GW],
                       w_in[:, bk + g * B_GW:bk + (g + 1) * B_GW],
                       w_in[:, bv + g * B_GW:bv + (g + 1) * B_GW]]
    w0 = jnp.concatenate(col_blocks, axis=1).astype(BF16)
    ones = lambda n: jnp.ones((n,), F32)
    gain_blocks = [jnp.tile(p["l0_q_norm_a"], A_Q_HEADS) * QK_SCALE, jnp.tile(p["l0_k_norm_a"], A_KV_HEADS),
                   ones(A_KVW)]
    for g in range(B_GROUPS):
        gain_blocks += [jnp.tile(p["l0_q_norm_b"], B_HEADS) * QK_SCALE, jnp.tile(p["l0_k_norm_b"], B_HEADS),
                        ones(B_GW)]
    gain0 = jnp.concatenate(gain_blocks).reshape(1, -1)
    slabs0 = [(0, A_QW, True, 1), (A_QW, A_KVW, True, 1), (A_QW + A_KVW, A_KVW, False, 1)]
    c0 = A_IN
    for g, (_, dil) in enumerate(B_PAIRS):
        slabs0 += [(c0, B_GW, True, dil), (c0 + B_GW, B_GW, True, dil), (c0 + 2 * B_GW, B_GW, False, dil)]
        c0 += 3 * B_GW
    proj = _in_proj(x_parts, mod0, sid_tok, p["l0_norm1"], w0, gain0, bd, tuple(slabs0))
    qa, ka, va = proj[0:3]

    slopes_a = _alibi_slopes(A_Q_HEADS)
    heads_a = tuple(tuple((slopes_a[kv * A_GROUP + g], kv * A_GROUP + g) for kv in range(A_KV_HEADS))
                    for g in range(A_GROUP))
    (oa,) = _band_attn(qa, ka, va, seq_lens, tq=ATTN_TQ_A, dil=1, qw=A_QW, kvw=A_KVW, half=A_HALF, heads=heads_a,
                       kv_group=(0,) * A_GROUP, sink=p["l0_sink_a"].astype(F32), out_dtype=BF16, want_lse=False)

    slopes_b = _alibi_slopes(B_GROUPS * B_HEADS)
    attn0 = [oa]
    for g, (window, dil) in enumerate(B_PAIRS):
        qg, kg, vg = proj[3 + 3 * g:6 + 3 * g]
        heads_b = tuple(tuple((slopes_b[g * B_HEADS + 2 * pr + s] * dil, 0) for s in range(2))
                        for pr in range(B_HEADS // 2))
        og, lg = _band_attn(qg, kg, vg, [n // dil for n in seq_lens], tq=ATTN_TQ_B, dil=dil, qw=B_GW, kvw=B_GW, half=window // (2 * dil),
                            heads=heads_b, kv_group=tuple(range(B_HEADS // 2)), sink=None, out_dtype=F32,
                            want_lse=True)
        attn0 += [og, lg]

    w_out = p["l0_w_out"]
    wout0 = jnp.concatenate([w_out[a_cols], w_out[A_QW:]], axis=0).astype(BF16)
    br0 = jnp.broadcast_to(p["l0_b_router"].astype(F32)[:, None], (N_EXPERTS, LANES))
    x1, h2, idx, gates = _out_proj(0, x_parts, mod0, sid_tok, p["l0_norm2"], attn0, wout0, p["l0_w_router"].T,
                                   br0)
    x = _moe(x1, h2, idx, gates, mod0, sid_move, p["l0_w_gate_up"], p["l0_b_gate_up"], p["l0_w_down"],
             p["l0_b_down"], [t])

    w1 = p["l1_w_in"].astype(BF16)
    gain1 = jnp.concatenate([jnp.tile(p["l1_q_norm_c"], C_HEADS) * QK_SCALE, jnp.tile(p["l1_k_norm_c"], C_HEADS),
                             ones(C_W)]).reshape(1, -1)
    slabs1 = ((0, C_W, True, 1), (C_W, C_W, True, 1), (2 * C_W, C_W, False, 1))
    qc, kc, vc = _in_proj(x, mod1, sid_tok, p["l1_norm1"], w1, gain1, bd, slabs1)
    na_tile = NA_TROWS * GRID_W
    tile_r0 = _tile_table(seq_lens, na_tile, lambda sid, s, n, j: j * NA_TROWS)
    tile_nr = _tile_table(seq_lens, na_tile, lambda sid, s, n, j: n // GRID_W)
    oc = _na_attn(qc, kc, vc, _na_bias_table(p["l1_rpb_c"]), tile_r0, tile_nr)
    br1 = jnp.broadcast_to(p["l1_b_router"].astype(F32)[:, None], (N_EXPERTS, LANES))
    x1, h2, idx, gates = _out_proj(1, x, mod1, sid_tok, p["l1_norm2"], [oc], p["l1_w_out"].astype(BF16),
                                   p["l1_w_router"].T, br1)
    return _moe(x1, h2, idx, gates, mod1, sid_move, p["l1_w_gate_up"], p["l1_b_gate_up"], p["l1_w_down"],
                p["l1_b_down"], [a.shape[0] for a in x_parts])


def kernel(x_prompt, x_sample, c_prompt, c_sample, l0_ada_w, l0_ada_b, l0_norm1, l0_w_in, l0_q_norm_a, l0_k_norm_a, l0_sink_a, l0_q_norm_b, l0_k_norm_b, l0_w_out, l0_norm2, l0_w_router, l0_b_router, l0_w_gate_up, l0_b_gate_up, l0_w_down, l0_b_down, l1_ada_w, l1_ada_b, l1_norm1, l1_w_in, l1_q_norm_c, l1_k_norm_c, l1_rpb_c, l1_w_out, l1_norm2, l1_w_router, l1_b_router, l1_w_gate_up, l1_b_gate_up, l1_w_down, l1_b_down):
    p = dict(l0_ada_w=l0_ada_w, l0_ada_b=l0_ada_b, l0_norm1=l0_norm1, l0_w_in=l0_w_in, l0_q_norm_a=l0_q_norm_a,
             l0_k_norm_a=l0_k_norm_a, l0_sink_a=l0_sink_a, l0_q_norm_b=l0_q_norm_b, l0_k_norm_b=l0_k_norm_b,
             l0_w_out=l0_w_out, l0_norm2=l0_norm2, l0_w_router=l0_w_router, l0_b_router=l0_b_router,
             l0_w_gate_up=l0_w_gate_up, l0_b_gate_up=l0_b_gate_up, l0_w_down=l0_w_down, l0_b_down=l0_b_down,
             l1_ada_w=l1_ada_w, l1_ada_b=l1_ada_b, l1_norm1=l1_norm1, l1_w_in=l1_w_in, l1_q_norm_c=l1_q_norm_c,
             l1_k_norm_c=l1_k_norm_c, l1_rpb_c=l1_rpb_c, l1_w_out=l1_w_out, l1_norm2=l1_norm2,
             l1_w_router=l1_w_router, l1_b_router=l1_b_router, l1_w_gate_up=l1_w_gate_up,
             l1_b_gate_up=l1_b_gate_up, l1_w_down=l1_w_down, l1_b_down=l1_b_down)
    nb_p, len_p, d = x_prompt.shape
    nb_s, len_s, _ = x_sample.shape
    nseq = nb_p + nb_s
    c_pad = jnp.concatenate([c_prompt, c_sample, jnp.zeros((-nseq % 8, d), F32)], axis=0)
    mods = tuple(_ada_mod(c_pad, p[l + "_ada_w"], p[l + "_ada_b"]).reshape(c_pad.shape[0], 6, D_MODEL)
                 for l in ("l0", "l1"))
    outs = []
    for xg, row0 in ((x_prompt, 0), (x_sample, nb_p)):
        nb, ln, _ = xg.shape
        (y,) = _trunk([xg.reshape(nb * ln, d)], mods, row0, (ln,) * nb, p)
        outs.append(y.reshape(nb, ln, d))
    return tuple(outs)
```

```python
import functools
import math

import numpy as np
import jax
import jax.numpy as jnp
from jax import lax
from jax.experimental import pallas as pl
from jax.experimental.pallas import tpu as pltpu
from jax.experimental.pallas import tpu_sc as plsc

F32 = jnp.float32
BF16 = jnp.bfloat16
HIGHEST = lax.Precision.HIGHEST

D_MODEL = 1024
HEAD_DIM = 64
LANES = 128
GRID_W = 64
A_Q_HEADS = 8
A_KV_HEADS = 2
A_GROUP = A_Q_HEADS // A_KV_HEADS
A_HALF = 128
B_PAIRS = ((128, 1), (512, 4), (2048, 16))
B_GROUPS = 3
B_HEADS = 4
A_QW = A_Q_HEADS * HEAD_DIM
A_KVW = A_KV_HEADS * HEAD_DIM
A_IN = A_QW + 2 * A_KVW
B_GW = B_HEADS * HEAD_DIM
C_HEADS = 16
C_W = C_HEADS * HEAD_DIM
NA_KH = 8
NA_KW = 16
N_EXPERTS = 32
TOP_K = 4
D_FF = 1024
SWIGLU_LIMIT = 7.0
SWIGLU_ALPHA = 1.702
RMS_EPS = 1e-6
NEG_INF = -1e30
LOG2E = math.log2(math.e)
LN2 = math.log(2.0)
QK_SCALE = HEAD_DIM ** -0.5 * LOG2E

TOKEN_TILE = 1024
ATTN_TQ_A = 512
ATTN_TQ_B = 512
ATTN_UQ = 128
NA_TROWS = 8
MOE_BM = 512
ROUTE_TILE = 1024
MOVE_TILE = 512
VMEM_LIMIT = 56 * 1024 * 1024


def _alibi_slopes(n):
    return [float(2.0 ** (-8.0 * (j + 1) / n)) for j in range(n)]


def _cparams(sem, flags=None):
    return pltpu.CompilerParams(dimension_semantics=sem, vmem_limit_bytes=VMEM_LIMIT, flags=flags)


def _mod_kernel(c_ref, w_ref, b_ref, o_ref):
    c = c_ref[...]
    s = c * jax.nn.sigmoid(c)
    o_ref[...] = jnp.dot(s, w_ref[...], precision=HIGHEST, preferred_element_type=F32) + b_ref[...]


def _ada_mod(c_pad, w, b):
    nrow = c_pad.shape[0]
    ncol = w.shape[1]
    tn = D_MODEL
    return pl.pallas_call(
        _mod_kernel,
        grid=(ncol // tn,),
        in_specs=[pl.BlockSpec((nrow, D_MODEL), lambda j: (0, 0)),
                  pl.BlockSpec((D_MODEL, tn), lambda j: (0, j)),
                  pl.BlockSpec((1, tn), lambda j: (0, j))],
        out_specs=pl.BlockSpec((nrow, tn), lambda j: (0, j)),
        out_shape=jax.ShapeDtypeStruct((nrow, ncol), F32),
        compiler_params=_cparams(("arbitrary",)),
        name="ada_mod",
    )(c_pad, w, b.reshape(1, ncol))


def _head_mean_sq(y, bd_ref):
    w = y.shape[1]
    outs = []
    for c0 in range(0, w, 2 * LANES):
        cw = min(2 * LANES, w - c0)
        sq = y[:, c0:c0 + cw] * y[:, c0:c0 + cw]
        outs.append(jnp.dot(sq.astype(BF16), bd_ref[0:cw, 0:cw], preferred_element_type=F32))
    return outs[0] if len(outs) == 1 else jnp.concatenate(outs, axis=1)


def _rms_mod(x, g_ref, mod_ref, shift_row, scale_row):
    ms = jnp.mean(x * x, axis=-1, keepdims=True)
    xn = x * lax.rsqrt(ms + RMS_EPS) * g_ref[...]
    return xn * (1.0 + mod_ref[scale_row:scale_row + 1, :]) + mod_ref[shift_row:shift_row + 1, :]


def _part_tiles(parts, tile):
    starts, s = [], 0
    for a in parts:
        assert a.shape[0] % tile == 0
        starts.append(s)
        s += a.shape[0] // tile
    return tuple(starts), s


def _part_specs(parts, starts, tile, ncol):
    def spec(a, s0):
        n = a.shape[0] // tile
        return pl.BlockSpec((tile, ncol), lambda i, *_: (jnp.clip(i - s0, 0, n - 1), 0))
    return [spec(a, s0) for a, s0 in zip(parts, starts)]


def _read_part(refs, starts):
    i = pl.program_id(0)
    x = refs[0][...]
    for ref, s0 in zip(refs[1:], starts[1:]):
        x = jnp.where(i >= s0, ref[...], x)
    return x


def _in_kernel(sid_ref, *refs, slabs, tm, part_starts):
    del sid_ref
    npart = len(part_starts)
    x_refs = refs[:npart]
    mod_ref, n1_ref, w_ref, gain_ref, bd_ref = refs[npart:npart + 5]
    rest = refs[npart + 5:]
    nout = len(slabs)
    out_refs = rest[:nout]
    scr_ref = rest[nout] if len(rest) > nout else None
    h = _rms_mod(_read_part(x_refs, part_starts), n1_ref, mod_ref, 0, 1).astype(BF16)
    proj = jnp.dot(h, w_ref[...], preferred_element_type=F32)
    for (c0, w, normed, dil), o_ref in zip(slabs, out_refs):
        y = proj[:, c0:c0 + w]
        if normed:
            y = y * lax.rsqrt(_head_mean_sq(y, bd_ref) + RMS_EPS) * gain_ref[:, c0:c0 + w]
        if dil == 1:
            o_ref[...] = y.astype(BF16)
        else:
            for c in range(w // LANES):
                scr_ref[c] = y[:, c * LANES:(c + 1) * LANES]
            for r in range(dil):
                for c in range(w // LANES):
                    o_ref[:, r * w + c * LANES:r * w + (c + 1) * LANES] = (
                        scr_ref[c, pl.ds(r, tm // dil, stride=dil), :].astype(BF16))


def _in_proj(x_parts, mod, tile_sid, n1, w_bf, gain, bd, slabs):
    tm = TOKEN_TILE
    part_starts, ntiles = _part_tiles(x_parts, tm)
    t = ntiles * tm
    ncols = w_bf.shape[1]
    out_shapes, out_specs = [], []
    for (c0, w, normed, dil) in slabs:
        out_shapes.append(jax.ShapeDtypeStruct((t // dil, dil * w), BF16))
        out_specs.append(pl.BlockSpec((tm // dil, dil * w), lambda i, sid: (i, 0)))
    need_scr = any(s[3] > 1 for s in slabs)
    scratch = [pltpu.VMEM((max(s[1] for s in slabs if s[3] > 1) // LANES, tm, LANES), F32)] if need_scr else []
    gs = pltpu.PrefetchScalarGridSpec(
        num_scalar_prefetch=1,
        grid=(t // tm,),
        in_specs=_part_specs(x_parts, part_starts, tm, D_MODEL) + [
            pl.BlockSpec((None, 6, D_MODEL), lambda i, sid: (sid[i], 0, 0)),
            pl.BlockSpec((1, D_MODEL), lambda i, sid: (0, 0)),
            pl.BlockSpec((D_MODEL, ncols), lambda i, sid: (0, 0)),
            pl.BlockSpec((1, ncols), lambda i, sid: (0, 0)),
            pl.BlockSpec((2 * LANES, 2 * LANES), lambda i, sid: (0, 0))],
        out_specs=out_specs,
        scratch_shapes=scratch)
    return pl.pallas_call(
        functools.partial(_in_kernel, slabs=slabs, tm=tm, part_starts=part_starts),
        grid_spec=gs,
        out_shape=out_shapes,
        compiler_params=_cparams(("arbitrary",)),
        name="in_proj",
    )(tile_sid, *x_parts, mod, n1.reshape(1, D_MODEL), w_bf, gain, bd)


def _band_kernel(s0_ref, s1_ref, *refs, tq, hb, half, heads, kv_group, has_sink, want_lse):
    pos = 0
    sink_ref = None
    if has_sink:
        sink_ref = refs[0]
        pos = 1
    q_ref, kp_ref, kc_ref, kn_ref, vp_ref, vc_ref, vn_ref = refs[pos:pos + 7]
    o_ref = refs[pos + 7]
    lse_ref = refs[pos + 8] if want_lse else None
    i = pl.program_id(1)
    uq = ATTN_UQ
    wlen = uq + 2 * hb
    kall = jnp.concatenate([kp_ref[...], kc_ref[...], kn_ref[...]], axis=0)
    vall = jnp.concatenate([vp_ref[...], vc_ref[...], vn_ref[...]], axis=0)
    lane = lax.broadcasted_iota(jnp.int32, (uq, LANES), 1)
    lo_lanes = lane < HEAD_DIM
    for sub in range(tq // uq):
        rows = slice(sub * uq, (sub + 1) * uq)
        kwin = kall[sub * uq:sub * uq + wlen]
        vwin = vall[sub * uq:sub * uq + wlen]
        q0 = i * tq + sub * uq
        kpos = q0 - hb + lax.broadcasted_iota(jnp.int32, (wlen, uq), 0)
        qpos = q0 + lax.broadcasted_iota(jnp.int32, (wlen, uq), 1)
        rel = kpos - qpos
        valid = (jnp.abs(rel) <= half) & (kpos >= s0_ref[i]) & (kpos < s1_ref[i])
        distm = jnp.where(valid, jnp.abs(rel).astype(F32), 1e32)
        lse_rows = []
        for g, pair in enumerate(heads):
            qg = q_ref[rows, g * LANES:(g + 1) * LANES]
            kg = kwin[:, kv_group[g] * LANES:(kv_group[g] + 1) * LANES]
            vg = vwin[:, kv_group[g] * LANES:(kv_group[g] + 1) * LANES]
            outs = []
            for s, (slope, sink_idx) in enumerate(pair):
                keep = lo_lanes if s == 0 else jnp.logical_not(lo_lanes)
                qm = jnp.where(keep, qg, jnp.zeros_like(qg))
                sc = lax.dot_general(kg, qm, (((1,), (1,)), ((), ())), preferred_element_type=F32)
                sc = sc - (slope * LOG2E) * distm
                m = jnp.max(sc, axis=0, keepdims=True)
                if has_sink:
                    sink2 = sink_ref[sink_idx] * LOG2E
                    m = jnp.maximum(m, sink2)
                e = jnp.exp2(sc - m)
                den = jnp.sum(e, axis=0, keepdims=True)
                if has_sink:
                    den = den + jnp.exp2(sink2 - m)
                p = (e * (1.0 / den)).astype(BF16)
                outs.append(lax.dot_general(p, vg, (((0,), (0,)), ((), ())), preferred_element_type=F32))
                if want_lse:
                    lse_rows.append((m + jnp.log2(den)) * LN2)
            o_ref[rows, g * LANES:(g + 1) * LANES] = jnp.where(lo_lanes, outs[0], outs[1]).astype(o_ref.dtype)
        if want_lse:
            lse_t = jnp.concatenate(lse_rows + [jnp.zeros((LANES - len(lse_rows), uq), F32)], axis=0)
            lse_ref[rows, :] = lse_t.T


def _band_attn(q, k, v, seq_rows, *, tq, dil, qw, kvw, half, heads, kv_group, sink, out_dtype, want_lse):
    rows = q.shape[0]
    hb = half
    s0 = _tile_table(seq_rows, tq, lambda sid, s, n, j: s)
    s1 = _tile_table(seq_rows, tq, lambda sid, s, n, j: s + n)
    per = tq // hb
    nt = rows // tq
    nhb = rows // hb
    has_sink = sink is not None
    in_specs = []
    args = []
    if has_sink:
        in_specs.append(pl.BlockSpec(memory_space=pltpu.SMEM))
        args.append(sink)
    qmap = lambda r, i, a, b: (i, r)
    pmap = lambda r, i, a, b: (jnp.maximum(i * per - 1, 0), r)
    nmap = lambda r, i, a, b: (jnp.minimum((i + 1) * per, nhb - 1), r)
    in_specs += [pl.BlockSpec((tq, qw), qmap),
                 pl.BlockSpec((hb, kvw), pmap), pl.BlockSpec((tq, kvw), qmap), pl.BlockSpec((hb, kvw), nmap),
                 pl.BlockSpec((hb, kvw), pmap), pl.BlockSpec((tq, kvw), qmap), pl.BlockSpec((hb, kvw), nmap)]
    args += [q, k, k, k, v, v, v]
    out_shape = [jax.ShapeDtypeStruct((rows, dil * qw), out_dtype)]
    out_specs = [pl.BlockSpec((tq, qw), qmap)]
    if want_lse:
        out_shape.append(jax.ShapeDtypeStruct((rows, dil * LANES), F32))
        out_specs.append(pl.BlockSpec((tq, LANES), qmap))
    gs = pltpu.PrefetchScalarGridSpec(num_scalar_prefetch=2, grid=(dil, nt),
                                      in_specs=in_specs, out_specs=out_specs)
    res = pl.pallas_call(
        functools.partial(_band_kernel, tq=tq, hb=hb, half=half, heads=heads, kv_group=kv_group,
                          has_sink=has_sink, want_lse=want_lse),
        grid_spec=gs,
        out_shape=out_shape,
        compiler_params=_cparams(("arbitrary", "arbitrary")),
        name="band_attn_d%d" % dil,
    )(s0, s1, *args)
    return res


def _na_kernel(r0_ref, nr_ref, q_ref, kp_ref, kc_ref, kn_ref, vp_ref, vc_ref, vn_ref, bias_ref,
               o_ref, kcat, vcat, *, halo):
    j = pl.program_id(0)
    tq = NA_TROWS * GRID_W
    hrows = halo * GRID_W
    kcat[0:hrows, :] = kp_ref[...]
    kcat[hrows:hrows + tq, :] = kc_ref[...]
    kcat[hrows + tq:, :] = kn_ref[...]
    vcat[0:hrows, :] = vp_ref[...]
    vcat[hrows:hrows + tq, :] = vc_ref[...]
    vcat[hrows + tq:, :] = vn_ref[...]
    r0 = r0_ref[j]
    nrows = nr_ref[j]
    kwin_len = NA_KH * GRID_W
    lane = lax.broadcasted_iota(jnp.int32, (GRID_W, LANES), 1)
    lo_lanes = lane < HEAD_DIM
    nrel = 2 * NA_KH - 1

    def row_body(u, carry):
        r = r0 + u
        rs = jnp.clip(r - NA_KH // 2, 0, nrows - NA_KH)
        off = pl.multiple_of((rs - r0 + halo) * GRID_W, GRID_W)
        bvar = rs - r + NA_KH - 1
        qrow = pl.multiple_of(u * GRID_W, GRID_W)
        for g in range(C_HEADS // 2):
            qg = q_ref[pl.ds(qrow, GRID_W), g * LANES:(g + 1) * LANES]
            qs = jnp.concatenate([jnp.where(lo_lanes, qg, jnp.zeros_like(qg)),
                                  jnp.where(lo_lanes, jnp.zeros_like(qg), qg)], axis=0)
            kg = kcat[pl.ds(off, kwin_len), g * LANES:(g + 1) * LANES]
            vg = vcat[pl.ds(off, kwin_len), g * LANES:(g + 1) * LANES]
            sc = lax.dot_general(kg, qs, (((1,), (1,)), ((), ())), preferred_element_type=F32)
            sc = sc + jnp.concatenate([bias_ref[g * nrel + bvar + kr] for kr in range(NA_KH)], axis=0)
            m = jnp.max(sc, axis=0, keepdims=True)
            e = jnp.exp2(sc - m)
            den = jnp.sum(e, axis=0, keepdims=True)
            p = (e * (1.0 / den)).astype(BF16)
            pv = lax.dot_general(p, vg, (((0,), (0,)), ((), ())), preferred_element_type=F32)
            o = jnp.where(lo_lanes, pv[0:GRID_W], pv[GRID_W:])
            o_ref[pl.ds(qrow, GRID_W), g * LANES:(g + 1) * LANES] = o.astype(o_ref.dtype)
        return carry

    lax.fori_loop(0, NA_TROWS, row_body, 0, unroll=True)


def _na_attn(q, k, v, bias_tab, tile_r0, tile_nr):
    t = q.shape[0]
    halo = NA_KH // 2
    tq = NA_TROWS * GRID_W
    hrows = halo * GRID_W
    per = tq // hrows
    nhb = t // hrows
    nt = t // tq
    qmap = lambda j, a, b: (j, 0)
    pmap = lambda j, a, b: (jnp.maximum(j * per - 1, 0), 0)
    nmap = lambda j, a, b: (jnp.minimum((j + 1) * per, nhb - 1), 0)
    gs = pltpu.PrefetchScalarGridSpec(
        num_scalar_prefetch=2, grid=(nt,),
        in_specs=[pl.BlockSpec((tq, C_W), qmap),
                  pl.BlockSpec((hrows, C_W), pmap), pl.BlockSpec((tq, C_W), qmap), pl.BlockSpec((hrows, C_W), nmap),
                  pl.BlockSpec((hrows, C_W), pmap), pl.BlockSpec((tq, C_W), qmap), pl.BlockSpec((hrows, C_W), nmap),
                  pl.BlockSpec(bias_tab.shape, lambda j, a, b: (0, 0, 0))],
        out_specs=pl.BlockSpec((tq, C_W), qmap),
        scratch_shapes=[pltpu.VMEM((tq + 2 * hrows, C_W), BF16), pltpu.VMEM((tq + 2 * hrows, C_W), BF16)])
    return pl.pallas_call(
        functools.partial(_na_kernel, halo=halo),
        grid_spec=gs,
        out_shape=jax.ShapeDtypeStruct((t, C_W), BF16),
        compiler_params=_cparams(("arbitrary",)),
        name="na_attn",
    )(tile_r0, tile_nr, q, k, k, k, v, v, v, bias_tab)


def _na_bias_table(rpb):
    c = np.arange(GRID_W)
    cs = np.clip(c - NA_KW // 2, 0, GRID_W - NA_KW)
    kc = np.arange(GRID_W)
    valid = (kc[:, None] >= cs[None, :]) & (kc[:, None] < cs[None, :] + NA_KW)
    span = GRID_W - NA_KW
    rev = jnp.flip(jnp.pad(rpb.astype(F32) * LOG2E, ((0, 0), (0, 0), (span, span)), mode="edge"), axis=-1)
    tab = jnp.stack([rev[:, :, GRID_W - 1 - k:2 * GRID_W - 1 - k] for k in range(GRID_W)], axis=2)
    tab = jnp.where(jnp.asarray(valid)[None, None], tab, NEG_INF)
    pairs = jnp.concatenate([tab[0::2], tab[1::2]], axis=-1)
    return pairs.reshape((C_HEADS // 2) * (2 * NA_KH - 1), GRID_W, LANES)


def _route(h2, wr_ref, br_ref, idx_ref, gate_ref):
    tm = h2.shape[0]
    logits = lax.dot_general(wr_ref[...], h2, (((1,), (1,)), ((), ())),
                             precision=HIGHEST, preferred_element_type=F32) + br_ref[:, 0:1]
    eid = lax.broadcasted_iota(jnp.int32, (N_EXPERTS, tm), 0)
    vals, idxs = [], []
    for _ in range(TOP_K):
        m = jnp.max(logits, axis=0, keepdims=True)
        ix = jnp.min(jnp.where(logits == m, eid, N_EXPERTS), axis=0, keepdims=True)
        vals.append(m)
        idxs.append(ix)
        logits = jnp.where(eid == ix, -jnp.inf, logits)
    es = [jnp.exp(vk - vals[0]) for vk in vals]
    den = es[0] + es[1] + es[2] + es[3]
    pad_i = jnp.zeros((8 - TOP_K, tm), jnp.int32)
    pad_f = jnp.zeros((8 - TOP_K, tm), F32)
    idx_ref[...] = jnp.concatenate(idxs + [pad_i], axis=0)
    gate_ref[...] = jnp.concatenate([ek / den for ek in es] + [pad_f], axis=0)


PACK_W = D_MODEL // 2


def _pack_bf16_rows(h):
    bits = pltpu.bitcast(h.astype(BF16).astype(F32), jnp.int32)
    return (bits[:, :PACK_W] & jnp.int32(-65536)) | lax.shift_right_logical(bits[:, PACK_W:], 16)


def _unpack_rows_f32(w):
    hi = pltpu.bitcast(w & jnp.int32(-65536), F32)
    lo = pltpu.bitcast(lax.shift_left(w, 16), F32)
    return jnp.concatenate([hi, lo], axis=1)


def _unpack_bf16_rows(w):
    return _unpack_rows_f32(w).astype(BF16)


def _out0_kernel(sid_ref, *refs, tm, part_starts):
    del sid_ref
    npart = len(part_starts)
    x_refs = refs[:npart]
    (mod_ref, n2_ref, oa_ref, ob0_ref, ls0_ref, ob1_ref, ls1_ref, ob2_ref, ls2_ref,
     wout_ref, wr_ref, br_ref, x1_ref, h2_ref, idx_ref, gate_ref, oscr, lscr) = refs[npart:]
    os_, ls_ = [], []
    for gi, (o_ref, l_ref) in enumerate(((ob0_ref, ls0_ref), (ob1_ref, ls1_ref), (ob2_ref, ls2_ref))):
        dil = B_PAIRS[gi][1]
        if dil == 1:
            os_.append(o_ref[...])
            ls_.append(l_ref[...])
        else:
            ncg = B_GW // LANES
            for r in range(dil):
                for c in range(ncg):
                    lo = r * B_GW + c * LANES
                    oscr[gi * ncg + c, pl.ds(r, tm // dil, stride=dil), :] = o_ref[:, lo:lo + LANES]
                lscr[gi, pl.ds(r, tm // dil, stride=dil), :] = l_ref[:, r * LANES:(r + 1) * LANES]
            os_.append(jnp.concatenate([oscr[gi * ncg + c] for c in range(ncg)], axis=1))
            ls_.append(lscr[gi])
    lmax = jnp.maximum(jnp.maximum(ls_[0], ls_[1]), ls_[2])
    ws = [jnp.exp(l - lmax) for l in ls_]
    winv = 1.0 / (ws[0] + ws[1] + ws[2])
    lane = lax.broadcasted_iota(jnp.int32, (tm, LANES), 1)
    lo_lanes = lane < HEAD_DIM

    def head_weights(w):
        return jnp.concatenate(
            [jnp.where(lo_lanes, jnp.broadcast_to(w[:, 2 * pr:2 * pr + 1], (tm, LANES)),
                       jnp.broadcast_to(w[:, 2 * pr + 1:2 * pr + 2], (tm, LANES)))
             for pr in range(B_HEADS // 2)], axis=1)

    ob = (head_weights(ws[0] * winv) * os_[0] + head_weights(ws[1] * winv) * os_[1]
          + head_weights(ws[2] * winv) * os_[2])
    o = (jnp.dot(oa_ref[...], wout_ref[0:A_QW, :], preferred_element_type=F32)
         + jnp.dot(ob.astype(BF16), wout_ref[A_QW:, :], preferred_element_type=F32))
    x1 = _read_part(x_refs, part_starts) + mod_ref[2:3, :] * o
    x1_ref[...] = x1
    h2 = _rms_mod(x1, n2_ref, mod_ref, 3, 4)
    h2_ref[...] = _pack_bf16_rows(h2)
    _route(h2, wr_ref, br_ref, idx_ref, gate_ref)


def _out1_kernel(sid_ref, *refs, part_starts):
    del sid_ref
    npart = len(part_starts)
    x_refs = refs[:npart]
    mod_ref, n2_ref, oc_ref, wout_ref, wr_ref, br_ref, x1_ref, h2_ref, idx_ref, gate_ref = refs[npart:]
    o = jnp.dot(oc_ref[...], wout_ref[...], preferred_element_type=F32)
    x1 = _read_part(x_refs, part_starts) + mod_ref[2:3, :] * o
    x1_ref[...] = x1
    h2 = _rms_mod(x1, n2_ref, mod_ref, 3, 4)
    h2_ref[...] = _pack_bf16_rows(h2)
    _route(h2, wr_ref, br_ref, idx_ref, gate_ref)


def _out_proj(layer, x_parts, mod, tile_sid, n2, attn, wout_bf, wr_t, br):
    tm = TOKEN_TILE
    part_starts, ntiles = _part_tiles(x_parts, tm)
    t = ntiles * tm
    row = lambda i, sid: (i, 0)
    const = lambda i, sid: (0, 0)
    in_specs = _part_specs(x_parts, part_starts, tm, D_MODEL) + [
        pl.BlockSpec((None, 6, D_MODEL), lambda i, sid: (sid[i], 0, 0)),
        pl.BlockSpec((1, D_MODEL), const)]
    scratch = []
    if layer == 0:
        in_specs.append(pl.BlockSpec((tm, A_QW), row))
        for (_, dil) in B_PAIRS:
            in_specs += [pl.BlockSpec((tm // dil, dil * B_GW), row), pl.BlockSpec((tm // dil, dil * LANES), row)]
        body = functools.partial(_out0_kernel, tm=tm, part_starts=part_starts)
        nscr = B_GROUPS * B_GW // LANES
        scratch = [pltpu.VMEM((nscr, tm, LANES), F32), pltpu.VMEM((B_GROUPS, tm, LANES), F32)]
    else:
        in_specs.append(pl.BlockSpec((tm, C_W), row))
        body = functools.partial(_out1_kernel, part_starts=part_starts)
    in_specs += [pl.BlockSpec(wout_bf.shape, const),
                 pl.BlockSpec((N_EXPERTS, D_MODEL), const),
                 pl.BlockSpec((N_EXPERTS, LANES), const)]
    gs = pltpu.PrefetchScalarGridSpec(
        num_scalar_prefetch=1, grid=(t // tm,), in_specs=in_specs,
        out_specs=[pl.BlockSpec((tm, D_MODEL), row), pl.BlockSpec((tm, PACK_W), row),
                   pl.BlockSpec((8, tm), lambda i, sid: (0, i)), pl.BlockSpec((8, tm), lambda i, sid: (0, i))],
        scratch_shapes=scratch)
    return pl.pallas_call(
        body, grid_spec=gs,
        out_shape=[jax.ShapeDtypeStruct((t, D_MODEL), F32), jax.ShapeDtypeStruct((t, PACK_W), jnp.int32),
                   jax.ShapeDtypeStruct((8, t), jnp.int32), jax.ShapeDtypeStruct((8, t), F32)],
        compiler_params=_cparams(("arbitrary",)),
        name="out_proj%d" % layer,
    )(tile_sid, *x_parts, mod, n2.reshape(1, D_MODEL), *attn, wout_bf, wr_t, br)


def _rank_kernel(idx_ref, tri_ref, rank_ref, cnt_ref, carry):
    i = pl.program_id(0)

    @pl.when(i == 0)
    def _():
        carry[...] = jnp.zeros_like(carry)

    tk = idx_ref.shape[1]
    eid = lax.broadcasted_iota(jnp.int32, (N_EXPERTS, tk), 0)
    hits = [eid == idx_ref[k:k + 1, :] for k in range(TOP_K)]
    onehot = sum(h.astype(F32) for h in hits)
    incl = jnp.dot(onehot.astype(BF16), tri_ref[...], preferred_element_type=F32)
    before = incl - onehot + carry[:, 0:1]
    rows = [jnp.sum(jnp.where(h, before, 0.0), axis=0, keepdims=True) for h in hits]
    rows.append(jnp.zeros((8 - TOP_K, tk), F32))
    rank_ref[...] = jnp.concatenate(rows, axis=0).astype(jnp.int32)
    carry[...] = carry[...] + incl[:, tk - 1:tk]
    cnt_ref[...] = carry[...].astype(jnp.int32)


def _slot_kernel(idx_ref, rank_ref, start_ref, slot_ref):
    tk = idx_ref.shape[1]
    eid = lax.broadcasted_iota(jnp.int32, (N_EXPERTS, tk), 0)
    start = start_ref[:, 0:1]
    rows = []
    for k in range(TOP_K):
        base = jnp.sum(jnp.where(eid == idx_ref[k:k + 1, :], start, 0), axis=0, keepdims=True)
        rows.append(base + rank_ref[k:k + 1, :])
    rows.append(jnp.zeros((8 - TOP_K, tk), jnp.int32))
    slot_ref[...] = jnp.concatenate(rows, axis=0)


def _route_slots(idx):
    t = idx.shape[1]
    tk = ROUTE_TILE
    tri = jnp.asarray(np.triu(np.ones((tk, tk), np.float32)), BF16)
    tok = lambda i: (0, i)
    rank, cnt = pl.pallas_call(
        _rank_kernel, grid=(t // tk,),
        in_specs=[pl.BlockSpec((8, tk), tok), pl.BlockSpec((tk, tk), lambda i: (0, 0))],
        out_specs=[pl.BlockSpec((8, tk), tok), pl.BlockSpec((N_EXPERTS, LANES), lambda i: (0, 0))],
        out_shape=[jax.ShapeDtypeStruct((8, t), jnp.int32), jax.ShapeDtypeStruct((N_EXPERTS, LANES), jnp.int32)],
        scratch_shapes=[pltpu.VMEM((N_EXPERTS, LANES), F32)],
        compiler_params=_cparams(("arbitrary",)),
        name="route_rank",
    )(idx, tri)
    counts = cnt[:, 0]
    padded = (counts + MOE_BM - 1) // MOE_BM * MOE_BM
    pad_end = jnp.cumsum(padded)
    pad_start = pad_end - padded
    nblk = (t * TOP_K) // MOE_BM + N_EXPERTS
    blk_row0 = jnp.arange(nblk, dtype=jnp.int32) * MOE_BM
    blk_expert = jnp.minimum(jnp.sum((pad_end[None, :] <= blk_row0[:, None]).astype(jnp.int32), axis=1),
                             N_EXPERTS - 1)
    nused = (pad_end[-1] // MOE_BM).astype(jnp.int32).reshape(1)
    eids = jnp.arange(N_EXPERTS, dtype=jnp.int32)
    later = (eids[None, :] > eids[:, None]) & (padded[None, :] > 0)
    next_expert = jnp.min(jnp.where(later, eids[None, :], N_EXPERTS), axis=1)
    next_expert = jnp.where(next_expert == N_EXPERTS, -1, next_expert).astype(jnp.int32)
    blk_next = next_expert[blk_expert]
    start_b = jnp.broadcast_to(pad_start.astype(jnp.int32)[:, None], (N_EXPERTS, LANES))
    slot = pl.pallas_call(
        _slot_kernel, grid=(t // tk,),
        in_specs=[pl.BlockSpec((8, tk), tok), pl.BlockSpec((8, tk), tok),
                  pl.BlockSpec((N_EXPERTS, LANES), lambda i: (0, 0))],
        out_specs=pl.BlockSpec((8, tk), tok),
        out_shape=jax.ShapeDtypeStruct((8, t), jnp.int32),
        compiler_params=_cparams(("arbitrary",)),
        name="route_slot",
    )(idx, rank, start_b)
    return slot, blk_expert, nused, blk_next


SC_CORES = 2
SC_SUBCORES = 16
SC_ROW_BUFFER_BYTES = 256 * 1024
SC_MAX_INDEX_LIST = 128


def _sc_chunks(t, row_bytes):
    chunk = min(SC_MAX_INDEX_LIST, SC_ROW_BUFFER_BYTES // row_bytes)
    nchunk = t // chunk
    per = nchunk // (SC_CORES * SC_SUBCORES)
    assert per * SC_CORES * SC_SUBCORES * chunk == t
    return chunk, nchunk, per


def _sc_slot_lists(slot, nchunk, chunk):
    return slot[:TOP_K].reshape(TOP_K, nchunk, chunk).transpose(1, 0, 2)


def _sc_dispatch(h2, slot, nslots):
    t, w = h2.shape
    chunk, nchunk, per = _sc_chunks(t, w * h2.dtype.itemsize)
    mesh = plsc.VectorSubcoreMesh(core_axis_name="c", subcore_axis_name="s")

    @functools.partial(
        pl.kernel, mesh=mesh, out_type=jax.ShapeDtypeStruct((nslots, w), h2.dtype),
        scratch_types=[pltpu.VMEM((TOP_K, chunk), jnp.int32), pltpu.VMEM((chunk, w), h2.dtype)],
        name="moe_dispatch_sc")
    def body(h_hbm, slot_hbm, xs_hbm, idx_v, rows_v):
        wid = lax.axis_index("s") * SC_CORES + lax.axis_index("c")

        @pl.loop(0, per)
        def _(j):
            c = wid * per + j
            pltpu.sync_copy(slot_hbm.at[c], idx_v)
            pltpu.sync_copy(h_hbm.at[pl.ds(c * chunk, chunk)], rows_v)
            for k in range(TOP_K):
                pltpu.sync_copy(rows_v, xs_hbm.at[idx_v.at[k]])

    return body(h2, _sc_slot_lists(slot, nchunk, chunk))


def _sc_gather(ys, slot, t):
    w = ys.shape[1]
    nbuf = 2
    chunk, nchunk, per = _sc_chunks(t, nbuf * w * ys.dtype.itemsize)
    mesh = plsc.VectorSubcoreMesh(core_axis_name="c", subcore_axis_name="s")

    @functools.partial(
        pl.kernel, mesh=mesh, out_type=jax.ShapeDtypeStruct((TOP_K, t, w), ys.dtype),
        scratch_types=[pltpu.VMEM((TOP_K, chunk), jnp.int32), pltpu.VMEM((nbuf, chunk, w), ys.dtype),
                       pltpu.SemaphoreType.DMA((nbuf,))],
        name="moe_gather_sc")
    def body(ys_hbm, slot_hbm, out_hbm, idx_v, rows_v, wsem):
        wid = lax.axis_index("s") * SC_CORES + lax.axis_index("c")

        @pl.loop(0, per)
        def _(j):
            c = wid * per + j
            pltpu.sync_copy(slot_hbm.at[c], idx_v)
            writes = []
            for k in range(TOP_K):
                b = k % nbuf
                if k >= nbuf:
                    writes[k - nbuf].wait()
                pltpu.sync_copy(ys_hbm.at[idx_v.at[k]], rows_v.at[b])
                writes.append(pltpu.async_copy(rows_v.at[b], out_hbm.at[k, pl.ds(c * chunk, chunk)], wsem.at[b]))
            for wr in writes[-nbuf:]:
                wr.wait()

    return body(ys, _sc_slot_lists(slot, nchunk, chunk))


def _ffn_kernel(be_ref, nu_ref, nxt_ref, xs_ref, wgu_hbm, bgu_ref, wd_hbm, bd_ref, y_ref,
                wgu_f, wd_f, wgu_bf, wd_bf, wsem):
    i = pl.program_id(0)
    prev = be_ref[jnp.maximum(i - 1, 0)]

    def weight_copies(e):
        return (pltpu.make_async_copy(wgu_hbm.at[e], wgu_f, wsem.at[0]),
                pltpu.make_async_copy(wd_hbm.at[e], wd_f, wsem.at[1]))

    @pl.when(i == 0)
    def _():
        for cp in weight_copies(be_ref[0]):
            cp.start()

    @pl.when(i < nu_ref[0])
    def _():
        @pl.when((i == 0) | (be_ref[i] != prev))
        def _():
            for cp in weight_copies(be_ref[i]):
                cp.wait()
            wgu_bf[...] = wgu_f[...].astype(BF16)
            wd_bf[...] = wd_f[...].astype(BF16)

            @pl.when(nxt_ref[i] >= 0)
            def _():
                for cp in weight_copies(nxt_ref[i]):
                    cp.start()

        x = _unpack_bf16_rows(xs_ref[...])
        gu = jnp.dot(x, wgu_bf[...], preferred_element_type=F32) + bgu_ref[...]
        gate = jnp.minimum(gu[:, :D_FF], SWIGLU_LIMIT)
        up = jnp.clip(gu[:, D_FF:], -SWIGLU_LIMIT, SWIGLU_LIMIT)
        act = (up + 1.0) * gate * jax.nn.sigmoid(SWIGLU_ALPHA * gate)
        y = jnp.dot(act.astype(BF16), wd_bf[...], preferred_element_type=F32) + bd_ref[...]
        y_ref[...] = _pack_bf16_rows(y)

    @pl.when(i >= nu_ref[0])
    def _():
        y_ref[...] = jnp.zeros_like(y_ref)


def _expert_ffn(xs, blk_expert, nused, blk_next, wgu, bgu, wd, bd):
    nslots = xs.shape[0]
    nblk = nslots // MOE_BM
    blk = lambda i, be, nu, nx: (jnp.minimum(i, nu[0] - 1), 0)
    exp3 = lambda i, be, nu, nx: (be[jnp.minimum(i, nu[0] - 1)], 0, 0)
    gs = pltpu.PrefetchScalarGridSpec(
        num_scalar_prefetch=3, grid=(nblk,),
        in_specs=[pl.BlockSpec((MOE_BM, PACK_W), blk),
                  pl.BlockSpec(memory_space=pl.ANY),
                  pl.BlockSpec((None, 1, 2 * D_FF), exp3),
                  pl.BlockSpec(memory_space=pl.ANY),
                  pl.BlockSpec((None, 1, D_MODEL), exp3)],
        out_specs=pl.BlockSpec((MOE_BM, PACK_W), lambda i, be, nu, nx: (i, 0)),
        scratch_shapes=[pltpu.VMEM((D_MODEL, 2 * D_FF), F32), pltpu.VMEM((D_FF, D_MODEL), F32),
                        pltpu.VMEM((D_MODEL, 2 * D_FF), BF16), pltpu.VMEM((D_FF, D_MODEL), BF16),
                        pltpu.SemaphoreType.DMA((2,))])
    return pl.pallas_call(
        _ffn_kernel, grid_spec=gs,
        out_shape=jax.ShapeDtypeStruct((nslots, PACK_W), jnp.int32),
        compiler_params=_cparams(("arbitrary",)),
        name="moe_ffn",
    )(blk_expert, nused, blk_next, xs, wgu, bgu.reshape(N_EXPERTS, 1, 2 * D_FF), wd,
      bd.reshape(N_EXPERTS, 1, D_MODEL))


def _combine_kernel(sid_ref, x_ref, mod_ref, gate_ref, yk_ref, *o_refs, tm, out_starts):
    del sid_ref
    gpad = jnp.concatenate([gate_ref[...], jnp.zeros((LANES - 8, tm), F32)], axis=0)
    gcol = gpad.T
    acc = gcol[:, 0:1] * _unpack_rows_f32(yk_ref[0])
    for k in range(1, TOP_K):
        acc = acc + gcol[:, k:k + 1] * _unpack_rows_f32(yk_ref[k])
    out = x_ref[...] + mod_ref[5:6, :] * acc
    i = pl.program_id(0)
    ends = out_starts[1:] + (None,)
    for o_ref, s0, s1 in zip(o_refs, out_starts, ends):
        mine = (i >= s0) if s1 is None else ((i >= s0) & (i < s1))

        @pl.when(mine)
        def _():
            o_ref[...] = out


def _combine(x1, mod, tile_sid, gates, yk, out_rows):
    t = x1.shape[0]
    tm = MOVE_TILE
    out_shape = [jax.ShapeDtypeStruct((n, D_MODEL), F32) for n in out_rows]
    out_starts, ntiles = _part_tiles(out_shape, tm)
    assert ntiles * tm == t
    gs = pltpu.PrefetchScalarGridSpec(
        num_scalar_prefetch=1, grid=(t // tm,),
        in_specs=[pl.BlockSpec((tm, D_MODEL), lambda i, sid: (i, 0)),
                  pl.BlockSpec((None, 6, D_MODEL), lambda i, sid: (sid[i], 0, 0)),
                  pl.BlockSpec((8, tm), lambda i, sid: (0, i)),
                  pl.BlockSpec((TOP_K, tm, PACK_W), lambda i, sid: (0, i, 0))],
        out_specs=_part_specs(out_shape, out_starts, tm, D_MODEL))
    return pl.pallas_call(
        functools.partial(_combine_kernel, tm=tm, out_starts=out_starts), grid_spec=gs,
        out_shape=out_shape,
        compiler_params=_cparams(("arbitrary",)),
        name="moe_combine",
    )(tile_sid, x1, mod, gates, yk)


def _moe(x1, h2, idx, gates, mod, tile_sid_move, wgu, bgu, wd, bd, out_rows):
    t = x1.shape[0]
    nslots = t * TOP_K + N_EXPERTS * MOE_BM
    slot, blk_expert, nused, blk_next = _route_slots(idx)
    xs = _sc_dispatch(h2, slot, nslots)
    ys = _expert_ffn(xs, blk_expert, nused, blk_next, wgu, bgu, wd, bd)
    return _combine(x1, mod, tile_sid_move, gates, _sc_gather(ys, slot, t), out_rows)


def _tile_table(seq_lens, tile, fn):
    vals = []
    start = 0
    for sid, n in enumerate(seq_lens):
        assert n % tile == 0
        for j in range(n // tile):
            vals.append(fn(sid, start, n, j))
        start += n
    return jnp.asarray(np.asarray(vals, np.int32))


def _head_block_diag():
    bd = np.kron(np.eye(2 * LANES // HEAD_DIM, dtype=np.float32),
                 np.full((HEAD_DIM, HEAD_DIM), 1.0 / HEAD_DIM, np.float32))
    return jnp.asarray(bd, BF16)


_A_HEAD_ORDER = [kv * A_GROUP + g for g in range(A_GROUP) for kv in range(A_KV_HEADS)]


def _trunk(x_parts, mods, mod_row0, seq_lens, p):
    t = sum(a.shape[0] for a in x_parts)
    sid_tok = _tile_table(seq_lens, TOKEN_TILE, lambda sid, s, n, j: mod_row0 + sid)
    sid_move = _tile_table(seq_lens, MOVE_TILE, lambda sid, s, n, j: mod_row0 + sid)
    bd = _head_block_diag()
    mod0, mod1 = mods

    w_in = p["l0_w_in"]
    a_cols = np.concatenate([np.arange(h * HEAD_DIM, (h + 1) * HEAD_DIM) for h in _A_HEAD_ORDER])
    bq, bk, bv = A_IN, A_IN + B_GROUPS * B_GW, A_IN + 2 * B_GROUPS * B_GW
    col_blocks = [w_in[:, a_cols], w_in[:, A_QW:A_IN]]
    for g in range(B_GROUPS):
        col_blocks += [w_in[:, bq + g * B_GW:bq + (g + 1) * B_GW],
                       w_in[:, bk + g * B_GW:bk + (g + 1) * B_GW],
                       w_in[:, bv + g * B_GW:bv + (g + 1) * B_GW]]
    w0 = jnp.concatenate(col_blocks, axis=1).astype(BF16)
    ones = lambda n: jnp.ones((n,), F32)
    gain_blocks = [jnp.tile(p["l0_q_norm_a"], A_Q_HEADS) * QK_SCALE, jnp.tile(p["l0_k_norm_a"], A_KV_HEADS),
                   ones(A_KVW)]
    for g in range(B_GROUPS):
        gain_blocks += [jnp.tile(p["l0_q_norm_b"], B_HEADS) * QK_SCALE, jnp.tile(p["l0_k_norm_b"], B_HEADS),
                        ones(B_GW)]
    gain0 = jnp.concatenate(gain_blocks).reshape(1, -1)
    slabs0 = [(0, A_QW, True, 1), (A_QW, A_KVW, True, 1), (A_QW + A_KVW, A_KVW, False, 1)]
    c0 = A_IN
    for g, (_, dil) in enumerate(B_PAIRS):
        slabs0 += [(c0, B_GW, True, dil), (c0 + B_GW, B_GW, True, dil), (c0 + 2 * B_GW, B_GW, False, dil)]
        c0 += 3 * B_GW
    proj = _in_proj(x_parts, mod0, sid_tok, p["l0_norm1"], w0, gain0, bd, tuple(slabs0))
    qa, ka, va = proj[0:3]

    slopes_a = _alibi_slopes(A_Q_HEADS)
    heads_a = tuple(tuple((slopes_a[kv * A_GROUP + g], kv * A_GROUP + g) for kv in range(A_KV_HEADS))
                    for g in range(A_GROUP))
    (oa,) = _band_attn(qa, ka, va, seq_lens, tq=ATTN_TQ_A, dil=1, qw=A_QW, kvw=A_KVW, half=A_HALF, heads=heads_a,
                       kv_group=(0,) * A_GROUP, sink=p["l0_sink_a"].astype(F32), out_dtype=BF16, want_lse=False)

    slopes_b = _alibi_slopes(B_GROUPS * B_HEADS)
    attn0 = [oa]
    for g, (window, dil) in enumerate(B_PAIRS):
        qg, kg, vg = proj[3 + 3 * g:6 + 3 * g]
        heads_b = tuple(tuple((slopes_b[g * B_HEADS + 2 * pr + s] * dil, 0) for s in range(2))
                        for pr in range(B_HEADS // 2))
        og, lg = _band_attn(qg, kg, vg, [n // dil for n in seq_lens], tq=ATTN_TQ_B, dil=dil, qw=B_GW, kvw=B_GW, half=window // (2 * dil),
                            heads=heads_b, kv_group=tuple(range(B_HEADS // 2)), sink=None, out_dtype=F32,
                            want_lse=True)
        attn0 += [og, lg]

    w_out = p["l0_w_out"]
    wout0 = jnp.concatenate([w_out[a_cols], w_out[A_QW:]], axis=0).astype(BF16)
    br0 = jnp.broadcast_to(p["l0_b_router"].astype(F32)[:, None], (N_EXPERTS, LANES))
    x1, h2, idx, gates = _out_proj(0, x_parts, mod0, sid_tok, p["l0_norm2"], attn0, wout0, p["l0_w_router"].T,
                                   br0)
    x = _moe(x1, h2, idx, gates, mod0, sid_move, p["l0_w_gate_up"], p["l0_b_gate_up"], p["l0_w_down"],
             p["l0_b_down"], [t])

    w1 = p["l1_w_in"].astype(BF16)
    gain1 = jnp.concatenate([jnp.tile(p["l1_q_norm_c"], C_HEADS) * QK_SCALE, jnp.tile(p["l1_k_norm_c"], C_HEADS),
                             ones(C_W)]).reshape(1, -1)
    slabs1 = ((0, C_W, True, 1), (C_W, C_W, True, 1), (2 * C_W, C_W, False, 1))
    qc, kc, vc = _in_proj(x, mod1, sid_tok, p["l1_norm1"], w1, gain1, bd, slabs1)
    na_tile = NA_TROWS * GRID_W
    tile_r0 = _tile_table(seq_lens, na_tile, lambda sid, s, n, j: j * NA_TROWS)
    tile_nr = _tile_table(seq_lens, na_tile, lambda sid, s, n, j: n // GRID_W)
    oc = _na_attn(qc, kc, vc, _na_bias_table(p["l1_rpb_c"]), tile_r0, tile_nr)
    br1 = jnp.broadcast_to(p["l1_b_router"].astype(F32)[:, None], (N_EXPERTS, LANES))
    x1, h2, idx, gates = _out_proj(1, x, mod1, sid_tok, p["l1_norm2"], [oc], p["l1_w_out"].astype(BF16),
                                   p["l1_w_router"].T, br1)
    return _moe(x1, h2, idx, gates, mod1, sid_move, p["l1_w_gate_up"], p["l1_b_gate_up"], p["l1_w_down"],
                p["l1_b_down"], [a.shape[0] for a in x_parts])


def kernel(x_prompt, x_sample, c_prompt, c_sample, l0_ada_w, l0_ada_b, l0_norm1, l0_w_in, l0_q_norm_a, l0_k_norm_a, l0_sink_a, l0_q_norm_b, l0_k_norm_b, l0_w_out, l0_norm2, l0_w_router, l0_b_router, l0_w_gate_up, l0_b_gate_up, l0_w_down, l0_b_down, l1_ada_w, l1_ada_b, l1_norm1, l1_w_in, l1_q_norm_c, l1_k_norm_c, l1_rpb_c, l1_w_out, l1_norm2, l1_w_router, l1_b_router, l1_w_gate_up, l1_b_gate_up, l1_w_down, l1_b_down):
    p = dict(l0_ada_w=l0_ada_w, l0_ada_b=l0_ada_b, l0_norm1=l0_norm1, l0_w_in=l0_w_in, l0_q_norm_a=l0_q_norm_a,
             l0_k_norm_a=l0_k_norm_a, l0_sink_a=l0_sink_a, l0_q_norm_b=l0_q_norm_b, l0_k_norm_b=l0_k_norm_b,
             l0_w_out=l0_w_out, l0_norm2=l0_norm2, l0_w_router=l0_w_router, l0_b_router=l0_b_router,
             l0_w_gate_up=l0_w_gate_up, l0_b_gate_up=l0_b_gate_up, l0_w_down=l0_w_down, l0_b_down=l0_b_down,
             l1_ada_w=l1_ada_w, l1_ada_b=l1_ada_b, l1_norm1=l1_norm1, l1_w_in=l1_w_in, l1_q_norm_c=l1_q_norm_c,
             l1_k_norm_c=l1_k_norm_c, l1_rpb_c=l1_rpb_c, l1_w_out=l1_w_out, l1_norm2=l1_norm2,
             l1_w_router=l1_w_router, l1_b_router=l1_b_router, l1_w_gate_up=l1_w_gate_up,
             l1_b_gate_up=l1_b_gate_up, l1_w_down=l1_w_down, l1_b_down=l1_b_down)
    nb_p, len_p, d = x_prompt.shape
    nb_s, len_s, _ = x_sample.shape
    nseq = nb_p + nb_s
    c_pad = jnp.concatenate([c_prompt, c_sample, jnp.zeros((-nseq % 8, d), F32)], axis=0)
    mods = tuple(_ada_mod(c_pad, p[l + "_ada_w"], p[l + "_ada_b"]).reshape(c_pad.shape[0], 6, D_MODEL)
                 for l in ("l0", "l1"))
    outs = []
    for xg, row0 in ((x_prompt, 0), (x_sample, nb_p)):
        nb, ln, _ = xg.shape
        (y,) = _trunk([xg.reshape(nb * ln, d)], mods, row0, (ln,) * nb, p)
        outs.append(y.reshape(nb, ln, d))
    return tuple(outs)
```

```python
import functools
import math

import numpy as np
import jax
import jax.numpy as jnp
from jax import lax
from jax.experimental import pallas as pl
from jax.experimental.pallas import tpu as pltpu
from jax.experimental.pallas import tpu_sc as plsc

F32 = jnp.float32
BF16 = jnp.bfloat16
HIGHEST = lax.Precision.HIGHEST

D_MODEL = 1024
HEAD_DIM = 64
LANES = 128
GRID_W = 64
A_Q_HEADS = 8
A_KV_HEADS = 2
A_GROUP = A_Q_HEADS // A_KV_HEADS
A_HALF = 128
B_PAIRS = ((128, 1), (512, 4), (2048, 16))
B_GROUPS = 3
B_HEADS = 4
A_QW = A_Q_HEADS * HEAD_DIM
A_KVW = A_KV_HEADS * HEAD_DIM
A_IN = A_QW + 2 * A_KVW
B_GW = B_HEADS * HEAD_DIM
C_HEADS = 16
C_W = C_HEADS * HEAD_DIM
NA_KH = 8
NA_KW = 16
N_EXPERTS = 32
TOP_K = 4
D_FF = 1024
SWIGLU_LIMIT = 7.0
SWIGLU_ALPHA = 1.702
RMS_EPS = 1e-6
NEG_INF = -1e30
LOG2E = math.log2(math.e)
LN2 = math.log(2.0)
QK_SCALE = HEAD_DIM ** -0.5 * LOG2E

TOKEN_TILE = 1024
ATTN_TQ_A = 512
ATTN_TQ_B = 512
ATTN_UQ = 128
NA_TROWS = 8
MOE_BM = 512
ROUTE_TILE = 1024
MOVE_TILE = 512
VMEM_LIMIT = 56 * 1024 * 1024


def _alibi_slopes(n):
    return [float(2.0 ** (-8.0 * (j + 1) / n)) for j in range(n)]


def _cparams(sem, flags=None):
    return pltpu.CompilerParams(dimension_semantics=sem, vmem_limit_bytes=VMEM_LIMIT, flags=flags)


def _mod_kernel(c_ref, w_ref, b_ref, o_ref):
    c = c_ref[...]
    s = c * jax.nn.sigmoid(c)
    o_ref[...] = jnp.dot(s, w_ref[...], precision=HIGHEST, preferred_element_type=F32) + b_ref[...]


def _ada_mod(c_pad, w, b):
    nrow = c_pad.shape[0]
    ncol = w.shape[1]
    tn = D_MODEL
    return pl.pallas_call(
        _mod_kernel,
        grid=(ncol // tn,),
        in_specs=[pl.BlockSpec((nrow, D_MODEL), lambda j: (0, 0)),
                  pl.BlockSpec((D_MODEL, tn), lambda j: (0, j)),
                  pl.BlockSpec((1, tn), lambda j: (0, j))],
        out_specs=pl.BlockSpec((nrow, tn), lambda j: (0, j)),
        out_shape=jax.ShapeDtypeStruct((nrow, ncol), F32),
        compiler_params=_cparams(("arbitrary",)),
        name="ada_mod",
    )(c_pad, w, b.reshape(1, ncol))


def _head_mean_sq(y, bd_ref):
    w = y.shape[1]
    outs = []
    for c0 in range(0, w, 2 * LANES):
        cw = min(2 * LANES, w - c0)
        sq = y[:, c0:c0 + cw] * y[:, c0:c0 + cw]
        outs.append(jnp.dot(sq.astype(BF16), bd_ref[0:cw, 0:cw], preferred_element_type=F32))
    return outs[0] if len(outs) == 1 else jnp.concatenate(outs, axis=1)


def _rms_mod(x, g_ref, mod_ref, shift_row, scale_row):
    ms = jnp.mean(x * x, axis=-1, keepdims=True)
    xn = x * lax.rsqrt(ms + RMS_EPS) * g_ref[...]
    return xn * (1.0 + mod_ref[scale_row:scale_row + 1, :]) + mod_ref[shift_row:shift_row + 1, :]


def _part_tiles(parts, tile):
    starts, s = [], 0
    for a in parts:
        assert a.shape[0] % tile == 0
        starts.append(s)
        s += a.shape[0] // tile
    return tuple(starts), s


def _part_specs(parts, starts, tile, ncol):
    def spec(a, s0):
        n = a.shape[0] // tile
        return pl.BlockSpec((tile, ncol), lambda i, *_: (jnp.clip(i - s0, 0, n - 1), 0))
    return [spec(a, s0) for a, s0 in zip(parts, starts)]


def _read_part(refs, starts):
    i = pl.program_id(0)
    x = refs[0][...]
    for ref, s0 in zip(refs[1:], starts[1:]):
        x = jnp.where(i >= s0, ref[...], x)
    return x


def _in_kernel(sid_ref, *refs, slabs, tm, part_starts):
    del sid_ref
    npart = len(part_starts)
    x_refs = refs[:npart]
    mod_ref, n1_ref, w_ref, gain_ref, bd_ref = refs[npart:npart + 5]
    rest = refs[npart + 5:]
    nout = len(slabs)
    out_refs = rest[:nout]
    scr_ref = rest[nout] if len(rest) > nout else None
    h = _rms_mod(_read_part(x_refs, part_starts), n1_ref, mod_ref, 0, 1).astype(BF16)
    proj = jnp.dot(h, w_ref[...], preferred_element_type=F32)
    for (c0, w, normed, dil), o_ref in zip(slabs, out_refs):
        y = proj[:, c0:c0 + w]
        if normed:
            y = y * lax.rsqrt(_head_mean_sq(y, bd_ref) + RMS_EPS) * gain_ref[:, c0:c0 + w]
        if dil == 1:
            o_ref[...] = y.astype(BF16)
        else:
            for c in range(w // LANES):
                scr_ref[c] = y[:, c * LANES:(c + 1) * LANES]
            for r in range(dil):
                for c in range(w // LANES):
                    o_ref[:, r * w + c * LANES:r * w + (c + 1) * LANES] = (
                        scr_ref[c, pl.ds(r, tm // dil, stride=dil), :].astype(BF16))


def _in_proj(x_parts, mod, tile_sid, n1, w_bf, gain, bd, slabs):
    tm = TOKEN_TILE
    part_starts, ntiles = _part_tiles(x_parts, tm)
    t = ntiles * tm
    ncols = w_bf.shape[1]
    out_shapes, out_specs = [], []
    for (c0, w, normed, dil) in slabs:
        out_shapes.append(jax.ShapeDtypeStruct((t // dil, dil * w), BF16))
        out_specs.append(pl.BlockSpec((tm // dil, dil * w), lambda i, sid: (i, 0)))
    need_scr = any(s[3] > 1 for s in slabs)
    scratch = [pltpu.VMEM((max(s[1] for s in slabs if s[3] > 1) // LANES, tm, LANES), F32)] if need_scr else []
    gs = pltpu.PrefetchScalarGridSpec(
        num_scalar_prefetch=1,
        grid=(t // tm,),
        in_specs=_part_specs(x_parts, part_starts, tm, D_MODEL) + [
            pl.BlockSpec((None, 6, D_MODEL), lambda i, sid: (sid[i], 0, 0)),
            pl.BlockSpec((1, D_MODEL), lambda i, sid: (0, 0)),
            pl.BlockSpec((D_MODEL, ncols), lambda i, sid: (0, 0)),
            pl.BlockSpec((1, ncols), lambda i, sid: (0, 0)),
            pl.BlockSpec((2 * LANES, 2 * LANES), lambda i, sid: (0, 0))],
        out_specs=out_specs,
        scratch_shapes=scratch)
    return pl.pallas_call(
        functools.partial(_in_kernel, slabs=slabs, tm=tm, part_starts=part_starts),
        grid_spec=gs,
        out_shape=out_shapes,
        compiler_params=_cparams(("arbitrary",)),
        name="in_proj",
    )(tile_sid, *x_parts, mod, n1.reshape(1, D_MODEL), w_bf, gain, bd)


def _band_kernel(s0_ref, s1_ref, *refs, tq, hb, half, heads, kv_group, has_sink, want_lse):
    pos = 0
    sink_ref = None
    if has_sink:
        sink_ref = refs[0]
        pos = 1
    q_ref, kp_ref, kc_ref, kn_ref, vp_ref, vc_ref, vn_ref = refs[pos:pos + 7]
    o_ref = refs[pos + 7]
    lse_ref = refs[pos + 8] if want_lse else None
    i = pl.program_id(1)
    uq = ATTN_UQ
    wlen = uq + 2 * hb
    kall = jnp.concatenate([kp_ref[...], kc_ref[...], kn_ref[...]], axis=0)
    vall = jnp.concatenate([vp_ref[...], vc_ref[...], vn_ref[...]], axis=0)
    lane = lax.broadcasted_iota(jnp.int32, (uq, LANES), 1)
    lo_lanes = lane < HEAD_DIM
    for sub in range(tq // uq):
        rows = slice(sub * uq, (sub + 1) * uq)
        kwin = kall[sub * uq:sub * uq + wlen]
        vwin = vall[sub * uq:sub * uq + wlen]
        q0 = i * tq + sub * uq
        kpos = q0 - hb + lax.broadcasted_iota(jnp.int32, (wlen, uq), 0)
        qpos = q0 + lax.broadcasted_iota(jnp.int32, (wlen, uq), 1)
        rel = kpos - qpos
        valid = (jnp.abs(rel) <= half) & (kpos >= s0_ref[i]) & (kpos < s1_ref[i])
        distm = jnp.where(valid, jnp.abs(rel).astype(F32), 1e32)
        lse_rows = []
        for g, pair in enumerate(heads):
            qg = q_ref[rows, g * LANES:(g + 1) * LANES]
            kg = kwin[:, kv_group[g] * LANES:(kv_group[g] + 1) * LANES]
            vg = vwin[:, kv_group[g] * LANES:(kv_group[g] + 1) * LANES]
            outs = []
            for s, (slope, sink_idx) in enumerate(pair):
                keep = lo_lanes if s == 0 else jnp.logical_not(lo_lanes)
                qm = jnp.where(keep, qg, jnp.zeros_like(qg))
                sc = lax.dot_general(kg, qm, (((1,), (1,)), ((), ())), preferred_element_type=F32)
                sc = sc - (slope * LOG2E) * distm
                m = jnp.max(sc, axis=0, keepdims=True)
                if has_sink:
                    sink2 = sink_ref[sink_idx] * LOG2E
                    m = jnp.maximum(m, sink2)
                e = jnp.exp2(sc - m)
                den = jnp.sum(e, axis=0, keepdims=True)
                if has_sink:
                    den = den + jnp.exp2(sink2 - m)
                p = (e * (1.0 / den)).astype(BF16)
                outs.append(lax.dot_general(p, vg, (((0,), (0,)), ((), ())), preferred_element_type=F32))
                if want_lse:
                    lse_rows.append((m + jnp.log2(den)) * LN2)
            o_ref[rows, g * LANES:(g + 1) * LANES] = jnp.where(lo_lanes, outs[0], outs[1]).astype(o_ref.dtype)
        if want_lse:
            lse_t = jnp.concatenate(lse_rows + [jnp.zeros((LANES - len(lse_rows), uq), F32)], axis=0)
            lse_ref[rows, :] = lse_t.T


def _band_attn(q, k, v, seq_rows, *, tq, dil, qw, kvw, half, heads, kv_group, sink, out_dtype, want_lse):
    rows = q.shape[0]
    hb = half
    s0 = _tile_table(seq_rows, tq, lambda sid, s, n, j: s)
    s1 = _tile_table(seq_rows, tq, lambda sid, s, n, j: s + n)
    per = tq // hb
    nt = rows // tq
    nhb = rows // hb
    has_sink = sink is not None
    in_specs = []
    args = []
    if has_sink:
        in_specs.append(pl.BlockSpec(memory_space=pltpu.SMEM))
        args.append(sink)
    qmap = lambda r, i, a, b: (i, r)
    pmap = lambda r, i, a, b: (jnp.maximum(i * per - 1, 0), r)
    nmap = lambda r, i, a, b: (jnp.minimum((i + 1) * per, nhb - 1), r)
    in_specs += [pl.BlockSpec((tq, qw), qmap),
                 pl.BlockSpec((hb, kvw), pmap), pl.BlockSpec((tq, kvw), qmap), pl.BlockSpec((hb, kvw), nmap),
                 pl.BlockSpec((hb, kvw), pmap), pl.BlockSpec((tq, kvw), qmap), pl.BlockSpec((hb, kvw), nmap)]
    args += [q, k, k, k, v, v, v]
    out_shape = [jax.ShapeDtypeStruct((rows, dil * qw), out_dtype)]
    out_specs = [pl.BlockSpec((tq, qw), qmap)]
    if want_lse:
        out_shape.append(jax.ShapeDtypeStruct((rows, dil * LANES), F32))
        out_specs.append(pl.BlockSpec((tq, LANES), qmap))
    gs = pltpu.PrefetchScalarGridSpec(num_scalar_prefetch=2, grid=(dil, nt),
                                      in_specs=in_specs, out_specs=out_specs)
    res = pl.pallas_call(
        functools.partial(_band_kernel, tq=tq, hb=hb, half=half, heads=heads, kv_group=kv_group,
                          has_sink=has_sink, want_lse=want_lse),
        grid_spec=gs,
        out_shape=out_shape,
        compiler_params=_cparams(("arbitrary", "arbitrary")),
        name="band_attn_d%d" % dil,
    )(s0, s1, *args)
    return res


def _na_kernel(r0_ref, nr_ref, q_ref, kp_ref, kc_ref, kn_ref, vp_ref, vc_ref, vn_ref, bias_ref,
               o_ref, kcat, vcat, *, halo):
    j = pl.program_id(0)
    tq = NA_TROWS * GRID_W
    hrows = halo * GRID_W
    kcat[0:hrows, :] = kp_ref[...]
    kcat[hrows:hrows + tq, :] = kc_ref[...]
    kcat[hrows + tq:, :] = kn_ref[...]
    vcat[0:hrows, :] = vp_ref[...]
    vcat[hrows:hrows + tq, :] = vc_ref[...]
    vcat[hrows + tq:, :] = vn_ref[...]
    r0 = r0_ref[j]
    nrows = nr_ref[j]
    kwin_len = NA_KH * GRID_W
    lane = lax.broadcasted_iota(jnp.int32, (GRID_W, LANES), 1)
    lo_lanes = lane < HEAD_DIM
    nrel = 2 * NA_KH - 1

    def row_body(u, carry):
        r = r0 + u
        rs = jnp.clip(r - NA_KH // 2, 0, nrows - NA_KH)
        off = pl.multiple_of((rs - r0 + halo) * GRID_W, GRID_W)
        bvar = rs - r + NA_KH - 1
        qrow = pl.multiple_of(u * GRID_W, GRID_W)
        for g in range(C_HEADS // 2):
            qg = q_ref[pl.ds(qrow, GRID_W), g * LANES:(g + 1) * LANES]
            qs = jnp.concatenate([jnp.where(lo_lanes, qg, jnp.zeros_like(qg)),
                                  jnp.where(lo_lanes, jnp.zeros_like(qg), qg)], axis=0)
            kg = kcat[pl.ds(off, kwin_len), g * LANES:(g + 1) * LANES]
            vg = vcat[pl.ds(off, kwin_len), g * LANES:(g + 1) * LANES]
            sc = lax.dot_general(kg, qs, (((1,), (1,)), ((), ())), preferred_element_type=F32)
            sc = sc + jnp.concatenate([bias_ref[g * nrel + bvar + kr] for kr in range(NA_KH)], axis=0)
            m = jnp.max(sc, axis=0, keepdims=True)
            e = jnp.exp2(sc - m)
            den = jnp.sum(e, axis=0, keepdims=True)
            p = (e * (1.0 / den)).astype(BF16)
            pv = lax.dot_general(p, vg, (((0,), (0,)), ((), ())), preferred_element_type=F32)
            o = jnp.where(lo_lanes, pv[0:GRID_W], pv[GRID_W:])
            o_ref[pl.ds(qrow, GRID_W), g * LANES:(g + 1) * LANES] = o.astype(o_ref.dtype)
        return carry

    lax.fori_loop(0, NA_TROWS, row_body, 0, unroll=True)


def _na_attn(q, k, v, bias_tab, tile_r0, tile_nr):
    t = q.shape[0]
    halo = NA_KH // 2
    tq = NA_TROWS * GRID_W
    hrows = halo * GRID_W
    per = tq // hrows
    nhb = t // hrows
    nt = t // tq
    qmap = lambda j, a, b: (j, 0)
    pmap = lambda j, a, b: (jnp.maximum(j * per - 1, 0), 0)
    nmap = lambda j, a, b: (jnp.minimum((j + 1) * per, nhb - 1), 0)
    gs = pltpu.PrefetchScalarGridSpec(
        num_scalar_prefetch=2, grid=(nt,),
        in_specs=[pl.BlockSpec((tq, C_W), qmap),
                  pl.BlockSpec((hrows, C_W), pmap), pl.BlockSpec((tq, C_W), qmap), pl.BlockSpec((hrows, C_W), nmap),
                  pl.BlockSpec((hrows, C_W), pmap), pl.BlockSpec((tq, C_W), qmap), pl.BlockSpec((hrows, C_W), nmap),
                  pl.BlockSpec(bias_tab.shape, lambda j, a, b: (0, 0, 0))],
        out_specs=pl.BlockSpec((tq, C_W), qmap),
        scratch_shapes=[pltpu.VMEM((tq + 2 * hrows, C_W), BF16), pltpu.VMEM((tq + 2 * hrows, C_W), BF16)])
    return pl.pallas_call(
        functools.partial(_na_kernel, halo=halo),
        grid_spec=gs,
        out_shape=jax.ShapeDtypeStruct((t, C_W), BF16),
        compiler_params=_cparams(("arbitrary",)),
        name="na_attn",
    )(tile_r0, tile_nr, q, k, k, k, v, v, v, bias_tab)


def _na_bias_table(rpb):
    c = np.arange(GRID_W)
    cs = np.clip(c - NA_KW // 2, 0, GRID_W - NA_KW)
    kc = np.arange(GRID_W)
    valid = (kc[:, None] >= cs[None, :]) & (kc[:, None] < cs[None, :] + NA_KW)
    span = GRID_W - NA_KW
    nrel = 2 * NA_KH - 1
    rpb2 = rpb.astype(F32).reshape(C_HEADS // 2, 2, nrel, 2 * NA_KW - 1) * LOG2E
    rev = jnp.flip(jnp.pad(rpb2, ((0, 0), (0, 0), (0, 0), (span, span)), mode="edge"), axis=-1)
    row = jnp.pad(rev, ((0, 0), (0, 0), (0, 0), (0, 1)))
    skew = jnp.broadcast_to(row[..., None, :], row.shape[:-1] + (GRID_W, 2 * GRID_W))
    skew = skew.reshape(row.shape[:-1] + (2 * GRID_W * GRID_W,))[..., :GRID_W * (2 * GRID_W - 1)]
    tab = skew.reshape(row.shape[:-1] + (GRID_W, 2 * GRID_W - 1))[..., GRID_W - 1:]
    tab = jnp.where(jnp.asarray(valid), tab, NEG_INF)
    pairs = jnp.concatenate([tab[:, 0], tab[:, 1]], axis=-1)
    return pairs.reshape((C_HEADS // 2) * (2 * NA_KH - 1), GRID_W, LANES)


def _route(h2, wr_ref, br_ref, idx_ref, gate_ref):
    tm = h2.shape[0]
    h_hi = h2.astype(BF16)
    h_lo = (h2 - h_hi.astype(F32)).astype(BF16)
    w = wr_ref[...]
    w_hi = w.astype(BF16)
    w_lo = (w - w_hi.astype(F32)).astype(BF16)
    nt = (((1,), (1,)), ((), ()))
    logits = (lax.dot_general(w_hi, h_hi, nt, preferred_element_type=F32)
              + lax.dot_general(w_lo, h_hi, nt, preferred_element_type=F32)
              + lax.dot_general(w_hi, h_lo, nt, preferred_element_type=F32)) + br_ref[:, 0:1]
    eid = lax.broadcasted_iota(jnp.int32, (N_EXPERTS, tm), 0)
    vals, idxs = [], []
    for _ in range(TOP_K):
        m = jnp.max(logits, axis=0, keepdims=True)
        ix = jnp.min(jnp.where(logits == m, eid, N_EXPERTS), axis=0, keepdims=True)
        vals.append(m)
        idxs.append(ix)
        logits = jnp.where(eid == ix, -jnp.inf, logits)
    es = [jnp.exp(vk - vals[0]) for vk in vals]
    den = es[0] + es[1] + es[2] + es[3]
    pad_i = jnp.zeros((8 - TOP_K, tm), jnp.int32)
    pad_f = jnp.zeros((8 - TOP_K, tm), F32)
    idx_ref[...] = jnp.concatenate(idxs + [pad_i], axis=0)
    gate_ref[...] = jnp.concatenate([ek / den for ek in es] + [pad_f], axis=0)


PACK_W = D_MODEL // 2


def _pack_bf16_rows(h):
    bits = pltpu.bitcast(h.astype(BF16).astype(F32), jnp.int32)
    return (bits[:, :PACK_W] & jnp.int32(-65536)) | lax.shift_right_logical(bits[:, PACK_W:], 16)


def _unpack_rows_f32(w):
    hi = pltpu.bitcast(w & jnp.int32(-65536), F32)
    lo = pltpu.bitcast(lax.shift_left(w, 16), F32)
    return jnp.concatenate([hi, lo], axis=1)


def _unpack_bf16_rows(w):
    return _unpack_rows_f32(w).astype(BF16)


def _out0_kernel(sid_ref, *refs, tm, part_starts):
    del sid_ref
    npart = len(part_starts)
    x_refs = refs[:npart]
    (mod_ref, n2_ref, oa_ref, ob0_ref, ls0_ref, ob1_ref, ls1_ref, ob2_ref, ls2_ref,
     wout_ref, wr_ref, br_ref, x1_ref, h2_ref, idx_ref, gate_ref, oscr, lscr) = refs[npart:]
    os_, ls_ = [], []
    for gi, (o_ref, l_ref) in enumerate(((ob0_ref, ls0_ref), (ob1_ref, ls1_ref), (ob2_ref, ls2_ref))):
        dil = B_PAIRS[gi][1]
        if dil == 1:
            os_.append(o_ref[...])
            ls_.append(l_ref[...])
        else:
            ncg = B_GW // LANES
            for r in range(dil):
                for c in range(ncg):
                    lo = r * B_GW + c * LANES
                    oscr[gi * ncg + c, pl.ds(r, tm // dil, stride=dil), :] = o_ref[:, lo:lo + LANES]
                lscr[gi, pl.ds(r, tm // dil, stride=dil), :] = l_ref[:, r * LANES:(r + 1) * LANES]
            os_.append(jnp.concatenate([oscr[gi * ncg + c] for c in range(ncg)], axis=1))
            ls_.append(lscr[gi])
    lmax = jnp.maximum(jnp.maximum(ls_[0], ls_[1]), ls_[2])
    ws = [jnp.exp(l - lmax) for l in ls_]
    winv = 1.0 / (ws[0] + ws[1] + ws[2])
    lane = lax.broadcasted_iota(jnp.int32, (tm, LANES), 1)
    lo_lanes = lane < HEAD_DIM

    def head_weights(w):
        return jnp.concatenate(
            [jnp.where(lo_lanes, jnp.broadcast_to(w[:, 2 * pr:2 * pr + 1], (tm, LANES)),
                       jnp.broadcast_to(w[:, 2 * pr + 1:2 * pr + 2], (tm, LANES)))
             for pr in range(B_HEADS // 2)], axis=1)

    ob = (head_weights(ws[0] * winv) * os_[0] + head_weights(ws[1] * winv) * os_[1]
          + head_weights(ws[2] * winv) * os_[2])
    o = (jnp.dot(oa_ref[...], wout_ref[0:A_QW, :], preferred_element_type=F32)
         + jnp.dot(ob.astype(BF16), wout_ref[A_QW:, :], preferred_element_type=F32))
    x1 = _read_part(x_refs, part_starts) + mod_ref[2:3, :] * o
    x1_ref[...] = x1
    h2 = _rms_mod(x1, n2_ref, mod_ref, 3, 4)
    h2_ref[...] = _pack_bf16_rows(h2)
    _route(h2, wr_ref, br_ref, idx_ref, gate_ref)


def _out1_kernel(sid_ref, *refs, part_starts):
    del sid_ref
    npart = len(part_starts)
    x_refs = refs[:npart]
    mod_ref, n2_ref, oc_ref, wout_ref, wr_ref, br_ref, x1_ref, h2_ref, idx_ref, gate_ref = refs[npart:]
    o = jnp.dot(oc_ref[...], wout_ref[...], preferred_element_type=F32)
    x1 = _read_part(x_refs, part_starts) + mod_ref[2:3, :] * o
    x1_ref[...] = x1
    h2 = _rms_mod(x1, n2_ref, mod_ref, 3, 4)
    h2_ref[...] = _pack_bf16_rows(h2)
    _route(h2, wr_ref, br_ref, idx_ref, gate_ref)


def _out_proj(layer, x_parts, mod, tile_sid, n2, attn, wout_bf, wr_t, br):
    tm = TOKEN_TILE
    part_starts, ntiles = _part_tiles(x_parts, tm)
    t = ntiles * tm
    row = lambda i, sid: (i, 0)
    const = lambda i, sid: (0, 0)
    in_specs = _part_specs(x_parts, part_starts, tm, D_MODEL) + [
        pl.BlockSpec((None, 6, D_MODEL), lambda i, sid: (sid[i], 0, 0)),
        pl.BlockSpec((1, D_MODEL), const)]
    scratch = []
    if layer == 0:
        in_specs.append(pl.BlockSpec((tm, A_QW), row))
        for (_, dil) in B_PAIRS:
            in_specs += [pl.BlockSpec((tm // dil, dil * B_GW), row), pl.BlockSpec((tm // dil, dil * LANES), row)]
        body = functools.partial(_out0_kernel, tm=tm, part_starts=part_starts)
        nscr = B_GROUPS * B_GW // LANES
        scratch = [pltpu.VMEM((nscr, tm, LANES), F32), pltpu.VMEM((B_GROUPS, tm, LANES), F32)]
    else:
        in_specs.append(pl.BlockSpec((tm, C_W), row))
        body = functools.partial(_out1_kernel, part_starts=part_starts)
    in_specs += [pl.BlockSpec(wout_bf.shape, const),
                 pl.BlockSpec((N_EXPERTS, D_MODEL), const),
                 pl.BlockSpec((N_EXPERTS, LANES), const)]
    gs = pltpu.PrefetchScalarGridSpec(
        num_scalar_prefetch=1, grid=(t // tm,), in_specs=in_specs,
        out_specs=[pl.BlockSpec((tm, D_MODEL), row), pl.BlockSpec((tm, PACK_W), row),
                   pl.BlockSpec((8, tm), lambda i, sid: (0, i)), pl.BlockSpec((8, tm), lambda i, sid: (0, i))],
        scratch_shapes=scratch)
    return pl.pallas_call(
        body, grid_spec=gs,
        out_shape=[jax.ShapeDtypeStruct((t, D_MODEL), F32), jax.ShapeDtypeStruct((t, PACK_W), jnp.int32),
                   jax.ShapeDtypeStruct((8, t), jnp.int32), jax.ShapeDtypeStruct((8, t), F32)],
        compiler_params=_cparams(("arbitrary",)),
        name="out_proj%d" % layer,
    )(tile_sid, *x_parts, mod, n2.reshape(1, D_MODEL), *attn, wout_bf, wr_t, br)


def _rank_kernel(idx_ref, tri_ref, rank_ref, cnt_ref, carry):
    i = pl.program_id(0)

    @pl.when(i == 0)
    def _():
        carry[...] = jnp.zeros_like(carry)

    tk = idx_ref.shape[1]
    eid = lax.broadcasted_iota(jnp.int32, (N_EXPERTS, tk), 0)
    hits = [eid == idx_ref[k:k + 1, :] for k in range(TOP_K)]
    onehot = sum(h.astype(F32) for h in hits)
    incl = jnp.dot(onehot.astype(BF16), tri_ref[...], preferred_element_type=F32)
    before = incl - onehot + carry[:, 0:1]
    rows = [jnp.sum(jnp.where(h, before, 0.0), axis=0, keepdims=True) for h in hits]
    rows.append(jnp.zeros((8 - TOP_K, tk), F32))
    rank_ref[...] = jnp.concatenate(rows, axis=0).astype(jnp.int32)
    carry[...] = carry[...] + incl[:, tk - 1:tk]
    cnt_ref[...] = carry[...].astype(jnp.int32)


def _slot_kernel(idx_ref, rank_ref, start_ref, slot_ref):
    tk = idx_ref.shape[1]
    eid = lax.broadcasted_iota(jnp.int32, (N_EXPERTS, tk), 0)
    start = start_ref[:, 0:1]
    rows = []
    for k in range(TOP_K):
        base = jnp.sum(jnp.where(eid == idx_ref[k:k + 1, :], start, 0), axis=0, keepdims=True)
        rows.append(base + rank_ref[k:k + 1, :])
    rows.append(jnp.zeros((8 - TOP_K, tk), jnp.int32))
    slot_ref[...] = jnp.concatenate(rows, axis=0)


def _route_slots(idx):
    t = idx.shape[1]
    tk = ROUTE_TILE
    tri = jnp.asarray(np.triu(np.ones((tk, tk), np.float32)), BF16)
    tok = lambda i: (0, i)
    rank, cnt = pl.pallas_call(
        _rank_kernel, grid=(t // tk,),
        in_specs=[pl.BlockSpec((8, tk), tok), pl.BlockSpec((tk, tk), lambda i: (0, 0))],
        out_specs=[pl.BlockSpec((8, tk), tok), pl.BlockSpec((N_EXPERTS, LANES), lambda i: (0, 0))],
        out_shape=[jax.ShapeDtypeStruct((8, t), jnp.int32), jax.ShapeDtypeStruct((N_EXPERTS, LANES), jnp.int32)],
        scratch_shapes=[pltpu.VMEM((N_EXPERTS, LANES), F32)],
        compiler_params=_cparams(("arbitrary",)),
        name="route_rank",
    )(idx, tri)
    counts = cnt[:, 0]
    padded = (counts + MOE_BM - 1) // MOE_BM * MOE_BM
    pad_end = jnp.cumsum(padded)
    pad_start = pad_end - padded
    nblk = (t * TOP_K) // MOE_BM + N_EXPERTS
    blk_row0 = jnp.arange(nblk, dtype=jnp.int32) * MOE_BM
    blk_expert = jnp.minimum(jnp.sum((pad_end[None, :] <= blk_row0[:, None]).astype(jnp.int32), axis=1),
                             N_EXPERTS - 1)
    nused = (pad_end[-1] // MOE_BM).astype(jnp.int32).reshape(1)
    eids = jnp.arange(N_EXPERTS, dtype=jnp.int32)
    later = (eids[None, :] > eids[:, None]) & (padded[None, :] > 0)
    next_expert = jnp.min(jnp.where(later, eids[None, :], N_EXPERTS), axis=1)
    next_expert = jnp.where(next_expert == N_EXPERTS, -1, next_expert).astype(jnp.int32)
    blk_next = next_expert[blk_expert]
    start_b = jnp.broadcast_to(pad_start.astype(jnp.int32)[:, None], (N_EXPERTS, LANES))
    slot = pl.pallas_call(
        _slot_kernel, grid=(t // tk,),
        in_specs=[pl.BlockSpec((8, tk), tok), pl.BlockSpec((8, tk), tok),
                  pl.BlockSpec((N_EXPERTS, LANES), lambda i: (0, 0))],
        out_specs=pl.BlockSpec((8, tk), tok),
        out_shape=jax.ShapeDtypeStruct((8, t), jnp.int32),
        compiler_params=_cparams(("arbitrary",)),
        name="route_slot",
    )(idx, rank, start_b)
    return slot, blk_expert, nused, blk_next


SC_CORES = 2
SC_SUBCORES = 16
SC_ROW_BUFFER_BYTES = 256 * 1024
SC_MAX_INDEX_LIST = 128


def _sc_chunks(t, row_bytes):
    chunk = min(SC_MAX_INDEX_LIST, SC_ROW_BUFFER_BYTES // row_bytes)
    nchunk = t // chunk
    per = nchunk // (SC_CORES * SC_SUBCORES)
    assert per * SC_CORES * SC_SUBCORES * chunk == t
    return chunk, nchunk, per


def _sc_slot_lists(slot, nchunk, chunk):
    return slot[:TOP_K].reshape(TOP_K, nchunk, chunk).transpose(1, 0, 2)


def _sc_dispatch(h2, slot, nslots):
    t, w = h2.shape
    chunk, nchunk, per = _sc_chunks(t, w * h2.dtype.itemsize)
    mesh = plsc.VectorSubcoreMesh(core_axis_name="c", subcore_axis_name="s")

    @functools.partial(
        pl.kernel, mesh=mesh, out_type=jax.ShapeDtypeStruct((nslots, w), h2.dtype),
        scratch_types=[pltpu.VMEM((TOP_K, chunk), jnp.int32), pltpu.VMEM((chunk, w), h2.dtype)],
        name="moe_dispatch_sc")
    def body(h_hbm, slot_hbm, xs_hbm, idx_v, rows_v):
        wid = lax.axis_index("s") * SC_CORES + lax.axis_index("c")

        @pl.loop(0, per)
        def _(j):
            c = wid * per + j
            pltpu.sync_copy(slot_hbm.at[c], idx_v)
            pltpu.sync_copy(h_hbm.at[pl.ds(c * chunk, chunk)], rows_v)
            for k in range(TOP_K):
                pltpu.sync_copy(rows_v, xs_hbm.at[idx_v.at[k]])

    return body(h2, _sc_slot_lists(slot, nchunk, chunk))


def _sc_gather(ys, slot, t):
    w = ys.shape[1]
    nbuf = 2
    chunk, nchunk, per = _sc_chunks(t, nbuf * w * ys.dtype.itemsize)
    mesh = plsc.VectorSubcoreMesh(core_axis_name="c", subcore_axis_name="s")

    @functools.partial(
        pl.kernel, mesh=mesh, out_type=jax.ShapeDtypeStruct((TOP_K, t, w), ys.dtype),
        scratch_types=[pltpu.VMEM((TOP_K, chunk), jnp.int32), pltpu.VMEM((nbuf, chunk, w), ys.dtype),
                       pltpu.SemaphoreType.DMA((nbuf,))],
        name="moe_gather_sc")
    def body(ys_hbm, slot_hbm, out_hbm, idx_v, rows_v, wsem):
        wid = lax.axis_index("s") * SC_CORES + lax.axis_index("c")

        @pl.loop(0, per)
        def _(j):
            c = wid * per + j
            pltpu.sync_copy(slot_hbm.at[c], idx_v)
            writes = []
            for k in range(TOP_K):
                b = k % nbuf
                if k >= nbuf:
                    writes[k - nbuf].wait()
                pltpu.sync_copy(ys_hbm.at[idx_v.at[k]], rows_v.at[b])
                writes.append(pltpu.async_copy(rows_v.at[b], out_hbm.at[k, pl.ds(c * chunk, chunk)], wsem.at[b]))
            for wr in writes[-nbuf:]:
                wr.wait()

    return body(ys, _sc_slot_lists(slot, nchunk, chunk))


def _ffn_kernel(be_ref, nu_ref, nxt_ref, xs_ref, wgu_hbm, bgu_ref, wd_hbm, bd_ref, y_ref,
                wgu_f, wd_f, wgu_bf, wd_bf, wsem):
    i = pl.program_id(0)
    prev = be_ref[jnp.maximum(i - 1, 0)]

    def weight_copies(e):
        return (pltpu.make_async_copy(wgu_hbm.at[e], wgu_f, wsem.at[0]),
                pltpu.make_async_copy(wd_hbm.at[e], wd_f, wsem.at[1]))

    @pl.when(i == 0)
    def _():
        for cp in weight_copies(be_ref[0]):
            cp.start()

    @pl.when(i < nu_ref[0])
    def _():
        @pl.when((i == 0) | (be_ref[i] != prev))
        def _():
            for cp in weight_copies(be_ref[i]):
                cp.wait()
            wgu_bf[...] = wgu_f[...].astype(BF16)
            wd_bf[...] = wd_f[...].astype(BF16)

            @pl.when(nxt_ref[i] >= 0)
            def _():
                for cp in weight_copies(nxt_ref[i]):
                    cp.start()

        x = _unpack_bf16_rows(xs_ref[...])
        gu = jnp.dot(x, wgu_bf[...], preferred_element_type=F32) + bgu_ref[...]
        gate = jnp.minimum(gu[:, :D_FF], SWIGLU_LIMIT)
        up = jnp.clip(gu[:, D_FF:], -SWIGLU_LIMIT, SWIGLU_LIMIT)
        act = (up + 1.0) * gate * jax.nn.sigmoid(SWIGLU_ALPHA * gate)
        y = jnp.dot(act.astype(BF16), wd_bf[...], preferred_element_type=F32) + bd_ref[...]
        y_ref[...] = _pack_bf16_rows(y)

    @pl.when(i >= nu_ref[0])
    def _():
        y_ref[...] = jnp.zeros_like(y_ref)


def _expert_ffn(xs, blk_expert, nused, blk_next, wgu, bgu, wd, bd):
    nslots = xs.shape[0]
    nblk = nslots // MOE_BM
    blk = lambda i, be, nu, nx: (jnp.minimum(i, nu[0] - 1), 0)
    exp3 = lambda i, be, nu, nx: (be[jnp.minimum(i, nu[0] - 1)], 0, 0)
    gs = pltpu.PrefetchScalarGridSpec(
        num_scalar_prefetch=3, grid=(nblk,),
        in_specs=[pl.BlockSpec((MOE_BM, PACK_W), blk),
                  pl.BlockSpec(memory_space=pl.ANY),
                  pl.BlockSpec((None, 1, 2 * D_FF), exp3),
                  pl.BlockSpec(memory_space=pl.ANY),
                  pl.BlockSpec((None, 1, D_MODEL), exp3)],
        out_specs=pl.BlockSpec((MOE_BM, PACK_W), lambda i, be, nu, nx: (i, 0)),
        scratch_shapes=[pltpu.VMEM((D_MODEL, 2 * D_FF), F32), pltpu.VMEM((D_FF, D_MODEL), F32),
                        pltpu.VMEM((D_MODEL, 2 * D_FF), BF16), pltpu.VMEM((D_FF, D_MODEL), BF16),
                        pltpu.SemaphoreType.DMA((2,))])
    return pl.pallas_call(
        _ffn_kernel, grid_spec=gs,
        out_shape=jax.ShapeDtypeStruct((nslots, PACK_W), jnp.int32),
        compiler_params=_cparams(("arbitrary",)),
        name="moe_ffn",
    )(blk_expert, nused, blk_next, xs, wgu, bgu.reshape(N_EXPERTS, 1, 2 * D_FF), wd,
      bd.reshape(N_EXPERTS, 1, D_MODEL))


def _combine_kernel(sid_ref, x_ref, mod_ref, gate_ref, yk_ref, *o_refs, tm, out_starts):
    del sid_ref
    gpad = jnp.concatenate([gate_ref[...], jnp.zeros((LANES - 8, tm), F32)], axis=0)
    gcol = gpad.T
    acc = gcol[:, 0:1] * _unpack_rows_f32(yk_ref[0])
    for k in range(1, TOP_K):
        acc = acc + gcol[:, k:k + 1] * _unpack_rows_f32(yk_ref[k])
    out = x_ref[...] + mod_ref[5:6, :] * acc
    i = pl.program_id(0)
    ends = out_starts[1:] + (None,)
    for o_ref, s0, s1 in zip(o_refs, out_starts, ends):
        mine = (i >= s0) if s1 is None else ((i >= s0) & (i < s1))

        @pl.when(mine)
        def _():
            o_ref[...] = out


def _combine(x1, mod, tile_sid, gates, yk, out_rows):
    t = x1.shape[0]
    tm = MOVE_TILE
    out_shape = [jax.ShapeDtypeStruct((n, D_MODEL), F32) for n in out_rows]
    out_starts, ntiles = _part_tiles(out_shape, tm)
    assert ntiles * tm == t
    gs = pltpu.PrefetchScalarGridSpec(
        num_scalar_prefetch=1, grid=(t // tm,),
        in_specs=[pl.BlockSpec((tm, D_MODEL), lambda i, sid: (i, 0)),
                  pl.BlockSpec((None, 6, D_MODEL), lambda i, sid: (sid[i], 0, 0)),
                  pl.BlockSpec((8, tm), lambda i, sid: (0, i)),
                  pl.BlockSpec((TOP_K, tm, PACK_W), lambda i, sid: (0, i, 0))],
        out_specs=_part_specs(out_shape, out_starts, tm, D_MODEL))
    return pl.pallas_call(
        functools.partial(_combine_kernel, tm=tm, out_starts=out_starts), grid_spec=gs,
        out_shape=out_shape,
        compiler_params=_cparams(("arbitrary",)),
        name="moe_combine",
    )(tile_sid, x1, mod, gates, yk)


def _moe(x1, h2, idx, gates, mod, tile_sid_move, wgu, bgu, wd, bd, out_rows):
    t = x1.shape[0]
    nslots = t * TOP_K + N_EXPERTS * MOE_BM
    slot, blk_expert, nused, blk_next = _route_slots(idx)
    xs = _sc_dispatch(h2, slot, nslots)
    ys = _expert_ffn(xs, blk_expert, nused, blk_next, wgu, bgu, wd, bd)
    return _combine(x1, mod, tile_sid_move, gates, _sc_gather(ys, slot, t), out_rows)


def _tile_table(seq_lens, tile, fn):
    vals = []
    start = 0
    for sid, n in enumerate(seq_lens):
        assert n % tile == 0
        for j in range(n // tile):
            vals.append(fn(sid, start, n, j))
        start += n
    return jnp.asarray(np.asarray(vals, np.int32))


def _head_block_diag():
    bd = np.kron(np.eye(2 * LANES // HEAD_DIM, dtype=np.float32),
                 np.full((HEAD_DIM, HEAD_DIM), 1.0 / HEAD_DIM, np.float32))
    return jnp.asarray(bd, BF16)


_A_HEAD_ORDER = [kv * A_GROUP + g for g in range(A_GROUP) for kv in range(A_KV_HEADS)]


def _trunk(x_parts, mods, mod_row0, seq_lens, p):
    t = sum(a.shape[0] for a in x_parts)
    sid_tok = _tile_table(seq_lens, TOKEN_TILE, lambda sid, s, n, j: mod_row0 + sid)
    sid_move = _tile_table(seq_lens, MOVE_TILE, lambda sid, s, n, j: mod_row0 + sid)
    bd = _head_block_diag()
    mod0, mod1 = mods

    w_in = p["l0_w_in"]
    a_cols = np.concatenate([np.arange(h * HEAD_DIM, (h + 1) * HEAD_DIM) for h in _A_HEAD_ORDER])
    bq, bk, bv = A_IN, A_IN + B_GROUPS * B_GW, A_IN + 2 * B_GROUPS * B_GW
    col_blocks = [w_in[:, a_cols], w_in[:, A_QW:A_IN]]
    for g in range(B_GROUPS):
        col_blocks += [w_in[:, bq + g * B_GW:bq + (g + 1) * B_GW],
                       w_in[:, bk + g * B_GW:bk + (g + 1) * B_GW],
                       w_in[:, bv + g * B_GW:bv + (g + 1) * B_GW]]
    w0 = jnp.concatenate(col_blocks, axis=1).astype(BF16)
    ones = lambda n: jnp.ones((n,), F32)
    gain_blocks = [jnp.tile(p["l0_q_norm_a"], A_Q_HEADS) * QK_SCALE, jnp.tile(p["l0_k_norm_a"], A_KV_HEADS),
                   ones(A_KVW)]
    for g in range(B_GROUPS):
        gain_blocks += [jnp.tile(p["l0_q_norm_b"], B_HEADS) * QK_SCALE, jnp.tile(p["l0_k_norm_b"], B_HEADS),
                        ones(B_GW)]
    gain0 = jnp.concatenate(gain_blocks).reshape(1, -1)
    slabs0 = [(0, A_QW, True, 1), (A_QW, A_KVW, True, 1), (A_QW + A_KVW, A_KVW, False, 1)]
    c0 = A_IN
    for g, (_, dil) in enumerate(B_PAIRS):
        slabs0 += [(c0, B_GW, True, dil), (c0 + B_GW, B_GW, True, dil), (c0 + 2 * B_GW, B_GW, False, dil)]
        c0 += 3 * B_GW
    proj = _in_proj(x_parts, mod0, sid_tok, p["l0_norm1"], w0, gain0, bd, tuple(slabs0))
    qa, ka, va = proj[0:3]

    slopes_a = _alibi_slopes(A_Q_HEADS)
    heads_a = tuple(tuple((slopes_a[kv * A_GROUP + g], kv * A_GROUP + g) for kv in range(A_KV_HEADS))
                    for g in range(A_GROUP))
    (oa,) = _band_attn(qa, ka, va, seq_lens, tq=ATTN_TQ_A, dil=1, qw=A_QW, kvw=A_KVW, half=A_HALF, heads=heads_a,
                       kv_group=(0,) * A_GROUP, sink=p["l0_sink_a"].astype(F32), out_dtype=BF16, want_lse=False)

    slopes_b = _alibi_slopes(B_GROUPS * B_HEADS)
    attn0 = [oa]
    for g, (window, dil) in enumerate(B_PAIRS):
        qg, kg, vg = proj[3 + 3 * g:6 + 3 * g]
        heads_b = tuple(tuple((slopes_b[g * B_HEADS + 2 * pr + s] * dil, 0) for s in range(2))
                        for pr in range(B_HEADS // 2))
        og, lg = _band_attn(qg, kg, vg, [n // dil for n in seq_lens], tq=ATTN_TQ_B, dil=dil, qw=B_GW, kvw=B_GW, half=window // (2 * dil),
                            heads=heads_b, kv_group=tuple(range(B_HEADS // 2)), sink=None, out_dtype=F32,
                            want_lse=True)
        attn0 += [og, lg]

    w_out = p["l0_w_out"]
    wout0 = jnp.concatenate([w_out[a_cols], w_out[A_QW:]], axis=0).astype(BF16)
    br0 = jnp.broadcast_to(p["l0_b_router"].astype(F32)[:, None], (N_EXPERTS, LANES))
    x1, h2, idx, gates = _out_proj(0, x_parts, mod0, sid_tok, p["l0_norm2"], attn0, wout0, p["l0_w_router"].T,
                                   br0)
    x = _moe(x1, h2, idx, gates, mod0, sid_move, p["l0_w_gate_up"], p["l0_b_gate_up"], p["l0_w_down"],
             p["l0_b_down"], [t])

    w1 = p["l1_w_in"].astype(BF16)
    gain1 = jnp.concatenate([jnp.tile(p["l1_q_norm_c"], C_HEADS) * QK_SCALE, jnp.tile(p["l1_k_norm_c"], C_HEADS),
                             ones(C_W)]).reshape(1, -1)
    slabs1 = ((0, C_W, True, 1), (C_W, C_W, True, 1), (2 * C_W, C_W, False, 1))
    qc, kc, vc = _in_proj(x, mod1, sid_tok, p["l1_norm1"], w1, gain1, bd, slabs1)
    na_tile = NA_TROWS * GRID_W
    tile_r0 = _tile_table(seq_lens, na_tile, lambda sid, s, n, j: j * NA_TROWS)
    tile_nr = _tile_table(seq_lens, na_tile, lambda sid, s, n, j: n // GRID_W)
    oc = _na_attn(qc, kc, vc, _na_bias_table(p["l1_rpb_c"]), tile_r0, tile_nr)
    br1 = jnp.broadcast_to(p["l1_b_router"].astype(F32)[:, None], (N_EXPERTS, LANES))
    x1, h2, idx, gates = _out_proj(1, x, mod1, sid_tok, p["l1_norm2"], [oc], p["l1_w_out"].astype(BF16),
                                   p["l1_w_router"].T, br1)
    return _moe(x1, h2, idx, gates, mod1, sid_move, p["l1_w_gate_up"], p["l1_b_gate_up"], p["l1_w_down"],
                p["l1_b_down"], [a.shape[0] for a in x_parts])


def kernel(x_prompt, x_sample, c_prompt, c_sample, l0_ada_w, l0_ada_b, l0_norm1, l0_w_in, l0_q_norm_a, l0_k_norm_a, l0_sink_a, l0_q_norm_b, l0_k_norm_b, l0_w_out, l0_norm2, l0_w_router, l0_b_router, l0_w_gate_up, l0_b_gate_up, l0_w_down, l0_b_down, l1_ada_w, l1_ada_b, l1_norm1, l1_w_in, l1_q_norm_c, l1_k_norm_c, l1_rpb_c, l1_w_out, l1_norm2, l1_w_router, l1_b_router, l1_w_gate_up, l1_b_gate_up, l1_w_down, l1_b_down):
    p = dict(l0_ada_w=l0_ada_w, l0_ada_b=l0_ada_b, l0_norm1=l0_norm1, l0_w_in=l0_w_in, l0_q_norm_a=l0_q_norm_a,
             l0_k_norm_a=l0_k_norm_a, l0_sink_a=l0_sink_a, l0_q_norm_b=l0_q_norm_b, l0_k_norm_b=l0_k_norm_b,
             l0_w_out=l0_w_out, l0_norm2=l0_norm2, l0_w_router=l0_w_router, l0_b_router=l0_b_router,
             l0_w_gate_up=l0_w_gate_up, l0_b_gate_up=l0_b_gate_up, l0_w_down=l0_w_down, l0_b_down=l0_b_down,
             l1_ada_w=l1_ada_w, l1_ada_b=l1_ada_b, l1_norm1=l1_norm1, l1_w_in=l1_w_in, l1_q_norm_c=l1_q_norm_c,
             l1_k_norm_c=l1_k_norm_c, l1_rpb_c=l1_rpb_c, l1_w_out=l1_w_out, l1_norm2=l1_norm2,
             l1_w_router=l1_w_router, l1_b_router=l1_b_router, l1_w_gate_up=l1_w_gate_up,
             l1_b_gate_up=l1_b_gate_up, l1_w_down=l1_w_down, l1_b_down=l1_b_down)
    nb_p, len_p, d = x_prompt.shape
    nb_s, len_s, _ = x_sample.shape
    nseq = nb_p + nb_s
    c_pad = jnp.concatenate([c_prompt, c_sample, jnp.zeros((-nseq % 8, d), F32)], axis=0)
    mods = tuple(_ada_mod(c_pad, p[l + "_ada_w"], p[l + "_ada_b"]).reshape(c_pad.shape[0], 6, D_MODEL)
                 for l in ("l0", "l1"))
    outs = []
    for xg, row0 in ((x_prompt, 0), (x_sample, nb_p)):
        nb, ln, _ = xg.shape
        (y,) = _trunk([xg.reshape(nb * ln, d)], mods, row0, (ln,) * nb, p)
        outs.append(y.reshape(nb, ln, d))
    return tuple(outs)
```

```python
import functools
import math

import numpy as np
import jax
import jax.numpy as jnp
from jax import lax
from jax.experimental import pallas as pl
from jax.experimental.pallas import tpu as pltpu
from jax.experimental.pallas import tpu_sc as plsc

F32 = jnp.float32
BF16 = jnp.bfloat16
HIGHEST = lax.Precision.HIGHEST

D_MODEL = 1024
HEAD_DIM = 64
LANES = 128
GRID_W = 64
A_Q_HEADS = 8
A_KV_HEADS = 2
A_GROUP = A_Q_HEADS // A_KV_HEADS
A_HALF = 128
B_PAIRS = ((128, 1), (512, 4), (2048, 16))
B_GROUPS = 3
B_HEADS = 4
A_QW = A_Q_HEADS * HEAD_DIM
A_KVW = A_KV_HEADS * HEAD_DIM
A_IN = A_QW + 2 * A_KVW
B_GW = B_HEADS * HEAD_DIM
C_HEADS = 16
C_W = C_HEADS * HEAD_DIM
NA_KH = 8
NA_KW = 16
N_EXPERTS = 32
TOP_K = 4
D_FF = 1024
SWIGLU_LIMIT = 7.0
SWIGLU_ALPHA = 1.702
RMS_EPS = 1e-6
NEG_INF = -1e30
LOG2E = math.log2(math.e)
LN2 = math.log(2.0)
QK_SCALE = HEAD_DIM ** -0.5 * LOG2E

TOKEN_TILE = 1024
ATTN_TQ_A = 512
ATTN_TQ_B = 512
ATTN_UQ = 128
NA_TROWS = 8
MOE_BM = 512
ROUTE_TILE = 1024
MOVE_TILE = 512
VMEM_LIMIT = 56 * 1024 * 1024


def _alibi_slopes(n):
    return [float(2.0 ** (-8.0 * (j + 1) / n)) for j in range(n)]


def _cparams(sem, flags=None):
    return pltpu.CompilerParams(dimension_semantics=sem, vmem_limit_bytes=VMEM_LIMIT, flags=flags)


def _mod_kernel(c_ref, w_ref, b_ref, o_ref):
    c = c_ref[...]
    s = c * jax.nn.sigmoid(c)
    o_ref[...] = jnp.dot(s, w_ref[...], precision=HIGHEST, preferred_element_type=F32) + b_ref[...]


def _ada_mod(c_pad, w, b):
    nrow = c_pad.shape[0]
    ncol = w.shape[1]
    tn = D_MODEL
    return pl.pallas_call(
        _mod_kernel,
        grid=(ncol // tn,),
        in_specs=[pl.BlockSpec((nrow, D_MODEL), lambda j: (0, 0)),
                  pl.BlockSpec((D_MODEL, tn), lambda j: (0, j)),
                  pl.BlockSpec((1, tn), lambda j: (0, j))],
        out_specs=pl.BlockSpec((nrow, tn), lambda j: (0, j)),
        out_shape=jax.ShapeDtypeStruct((nrow, ncol), F32),
        compiler_params=_cparams(("arbitrary",)),
        name="ada_mod",
    )(c_pad, w, b.reshape(1, ncol))


def _head_mean_sq(y, bd_ref):
    w = y.shape[1]
    outs = []
    for c0 in range(0, w, 2 * LANES):
        cw = min(2 * LANES, w - c0)
        sq = y[:, c0:c0 + cw] * y[:, c0:c0 + cw]
        outs.append(jnp.dot(sq.astype(BF16), bd_ref[0:cw, 0:cw], preferred_element_type=F32))
    return outs[0] if len(outs) == 1 else jnp.concatenate(outs, axis=1)


def _rms_mod(x, g_ref, mod_ref, shift_row, scale_row):
    ms = jnp.mean(x * x, axis=-1, keepdims=True)
    xn = x * lax.rsqrt(ms + RMS_EPS) * g_ref[...]
    return xn * (1.0 + mod_ref[scale_row:scale_row + 1, :]) + mod_ref[shift_row:shift_row + 1, :]


def _part_tiles(parts, tile):
    starts, s = [], 0
    for a in parts:
        assert a.shape[0] % tile == 0
        starts.append(s)
        s += a.shape[0] // tile
    return tuple(starts), s


def _part_specs(parts, starts, tile, ncol):
    def spec(a, s0):
        n = a.shape[0] // tile
        return pl.BlockSpec((tile, ncol), lambda i, *_: (jnp.clip(i - s0, 0, n - 1), 0))
    return [spec(a, s0) for a, s0 in zip(parts, starts)]


def _read_part(refs, starts):
    i = pl.program_id(0)
    x = refs[0][...]
    for ref, s0 in zip(refs[1:], starts[1:]):
        x = jnp.where(i >= s0, ref[...], x)
    return x


def _in_kernel(sid_ref, *refs, slabs, tm, part_starts):
    del sid_ref
    npart = len(part_starts)
    x_refs = refs[:npart]
    mod_ref, n1_ref, w_ref, gain_ref, bd_ref = refs[npart:npart + 5]
    rest = refs[npart + 5:]
    nout = len(slabs)
    out_refs = rest[:nout]
    scr_ref = rest[nout] if len(rest) > nout else None
    h = _rms_mod(_read_part(x_refs, part_starts), n1_ref, mod_ref, 0, 1).astype(BF16)
    proj = jnp.dot(h, w_ref[...], preferred_element_type=F32)
    for (c0, w, normed, dil), o_ref in zip(slabs, out_refs):
        y = proj[:, c0:c0 + w]
        if normed:
            y = y * lax.rsqrt(_head_mean_sq(y, bd_ref) + RMS_EPS) * gain_ref[:, c0:c0 + w]
        if dil == 1:
            o_ref[...] = y.astype(BF16)
        else:
            for c in range(w // LANES):
                scr_ref[c] = y[:, c * LANES:(c + 1) * LANES]
            for r in range(dil):
                for c in range(w // LANES):
                    o_ref[:, r * w + c * LANES:r * w + (c + 1) * LANES] = (
                        scr_ref[c, pl.ds(r, tm // dil, stride=dil), :].astype(BF16))


def _in_proj(x_parts, mod, tile_sid, n1, w_bf, gain, bd, slabs):
    tm = TOKEN_TILE
    part_starts, ntiles = _part_tiles(x_parts, tm)
    t = ntiles * tm
    ncols = w_bf.shape[1]
    out_shapes, out_specs = [], []
    for (c0, w, normed, dil) in slabs:
        out_shapes.append(jax.ShapeDtypeStruct((t // dil, dil * w), BF16))
        out_specs.append(pl.BlockSpec((tm // dil, dil * w), lambda i, sid: (i, 0)))
    need_scr = any(s[3] > 1 for s in slabs)
    scratch = [pltpu.VMEM((max(s[1] for s in slabs if s[3] > 1) // LANES, tm, LANES), F32)] if need_scr else []
    gs = pltpu.PrefetchScalarGridSpec(
        num_scalar_prefetch=1,
        grid=(t // tm,),
        in_specs=_part_specs(x_parts, part_starts, tm, D_MODEL) + [
            pl.BlockSpec((None, 6, D_MODEL), lambda i, sid: (sid[i], 0, 0)),
            pl.BlockSpec((1, D_MODEL), lambda i, sid: (0, 0)),
            pl.BlockSpec((D_MODEL, ncols), lambda i, sid: (0, 0)),
            pl.BlockSpec((1, ncols), lambda i, sid: (0, 0)),
            pl.BlockSpec((2 * LANES, 2 * LANES), lambda i, sid: (0, 0))],
        out_specs=out_specs,
        scratch_shapes=scratch)
    return pl.pallas_call(
        functools.partial(_in_kernel, slabs=slabs, tm=tm, part_starts=part_starts),
        grid_spec=gs,
        out_shape=out_shapes,
        compiler_params=_cparams(("arbitrary",)),
        name="in_proj",
    )(tile_sid, *x_parts, mod, n1.reshape(1, D_MODEL), w_bf, gain, bd)


def _band_kernel(s0_ref, s1_ref, *refs, tq, hb, half, heads, kv_group, has_sink, want_lse):
    pos = 0
    sink_ref = None
    if has_sink:
        sink_ref = refs[0]
        pos = 1
    q_ref, kp_ref, kc_ref, kn_ref, vp_ref, vc_ref, vn_ref = refs[pos:pos + 7]
    o_ref = refs[pos + 7]
    lse_ref = refs[pos + 8] if want_lse else None
    i = pl.program_id(1)
    uq = ATTN_UQ
    wlen = uq + 2 * hb
    kall = jnp.concatenate([kp_ref[...], kc_ref[...], kn_ref[...]], axis=0)
    vall = jnp.concatenate([vp_ref[...], vc_ref[...], vn_ref[...]], axis=0)
    lane = lax.broadcasted_iota(jnp.int32, (uq, LANES), 1)
    lo_lanes = lane < HEAD_DIM
    for sub in range(tq // uq):
        rows = slice(sub * uq, (sub + 1) * uq)
        kwin = kall[sub * uq:sub * uq + wlen]
        vwin = vall[sub * uq:sub * uq + wlen]
        q0 = i * tq + sub * uq
        kpos = q0 - hb + lax.broadcasted_iota(jnp.int32, (wlen, uq), 0)
        qpos = q0 + lax.broadcasted_iota(jnp.int32, (wlen, uq), 1)
        rel = kpos - qpos
        valid = (jnp.abs(rel) <= half) & (kpos >= s0_ref[i]) & (kpos < s1_ref[i])
        distm = jnp.where(valid, jnp.abs(rel).astype(F32), 1e32)
        lse_rows = []
        for g, pair in enumerate(heads):
            qg = q_ref[rows, g * LANES:(g + 1) * LANES]
            kg = kwin[:, kv_group[g] * LANES:(kv_group[g] + 1) * LANES]
            vg = vwin[:, kv_group[g] * LANES:(kv_group[g] + 1) * LANES]
            outs = []
            for s, (slope, sink_idx) in enumerate(pair):
                keep = lo_lanes if s == 0 else jnp.logical_not(lo_lanes)
                qm = jnp.where(keep, qg, jnp.zeros_like(qg))
                sc = lax.dot_general(kg, qm, (((1,), (1,)), ((), ())), preferred_element_type=F32)
                sc = sc - (slope * LOG2E) * distm
                m = jnp.max(sc, axis=0, keepdims=True)
                if has_sink:
                    sink2 = sink_ref[sink_idx] * LOG2E
                    m = jnp.maximum(m, sink2)
                e = jnp.exp2(sc - m)
                den = jnp.sum(e, axis=0, keepdims=True)
                if has_sink:
                    den = den + jnp.exp2(sink2 - m)
                p = (e * (1.0 / den)).astype(BF16)
                outs.append(lax.dot_general(p, vg, (((0,), (0,)), ((), ())), preferred_element_type=F32))
                if want_lse:
                    lse_rows.append((m + jnp.log2(den)) * LN2)
            o_ref[rows, g * LANES:(g + 1) * LANES] = jnp.where(lo_lanes, outs[0], outs[1]).astype(o_ref.dtype)
        if want_lse:
            lse_t = jnp.concatenate(lse_rows + [jnp.zeros((LANES - len(lse_rows), uq), F32)], axis=0)
            lse_ref[rows, :] = lse_t.T


def _band_attn(q, k, v, seq_rows, *, tq, dil, qw, kvw, half, heads, kv_group, sink, out_dtype, want_lse):
    rows = q.shape[0]
    hb = half
    s0 = _tile_table(seq_rows, tq, lambda sid, s, n, j: s)
    s1 = _tile_table(seq_rows, tq, lambda sid, s, n, j: s + n)
    per = tq // hb
    nt = rows // tq
    nhb = rows // hb
    has_sink = sink is not None
    in_specs = []
    args = []
    if has_sink:
        in_specs.append(pl.BlockSpec(memory_space=pltpu.SMEM))
        args.append(sink)
    qmap = lambda r, i, a, b: (i, r)
    pmap = lambda r, i, a, b: (jnp.maximum(i * per - 1, 0), r)
    nmap = lambda r, i, a, b: (jnp.minimum((i + 1) * per, nhb - 1), r)
    in_specs += [pl.BlockSpec((tq, qw), qmap),
                 pl.BlockSpec((hb, kvw), pmap), pl.BlockSpec((tq, kvw), qmap), pl.BlockSpec((hb, kvw), nmap),
                 pl.BlockSpec((hb, kvw), pmap), pl.BlockSpec((tq, kvw), qmap), pl.BlockSpec((hb, kvw), nmap)]
    args += [q, k, k, k, v, v, v]
    out_shape = [jax.ShapeDtypeStruct((rows, dil * qw), out_dtype)]
    out_specs = [pl.BlockSpec((tq, qw), qmap)]
    if want_lse:
        out_shape.append(jax.ShapeDtypeStruct((rows, dil * LANES), F32))
        out_specs.append(pl.BlockSpec((tq, LANES), qmap))
    gs = pltpu.PrefetchScalarGridSpec(num_scalar_prefetch=2, grid=(dil, nt),
                                      in_specs=in_specs, out_specs=out_specs)
    res = pl.pallas_call(
        functools.partial(_band_kernel, tq=tq, hb=hb, half=half, heads=heads, kv_group=kv_group,
                          has_sink=has_sink, want_lse=want_lse),
        grid_spec=gs,
        out_shape=out_shape,
        compiler_params=_cparams(("arbitrary", "arbitrary")),
        name="band_attn_d%d" % dil,
    )(s0, s1, *args)
    return res


def _na_kernel(r0_ref, nr_ref, q_ref, kp_ref, kc_ref, kn_ref, vp_ref, vc_ref, vn_ref, bias_ref,
               o_ref, kcat, vcat, *, halo):
    j = pl.program_id(0)
    tq = NA_TROWS * GRID_W
    hrows = halo * GRID_W
    kcat[0:hrows, :] = kp_ref[...]
    kcat[hrows:hrows + tq, :] = kc_ref[...]
    kcat[hrows + tq:, :] = kn_ref[...]
    vcat[0:hrows, :] = vp_ref[...]
    vcat[hrows:hrows + tq, :] = vc_ref[...]
    vcat[hrows + tq:, :] = vn_ref[...]
    r0 = r0_ref[j]
    nrows = nr_ref[j]
    kwin_len = NA_KH * GRID_W
    lane = lax.broadcasted_iota(jnp.int32, (GRID_W, LANES), 1)
    lo_lanes = lane < HEAD_DIM
    nrel = 2 * NA_KH - 1

    def row_body(u, carry):
        r = r0 + u
        rs = jnp.clip(r - NA_KH // 2, 0, nrows - NA_KH)
        off = pl.multiple_of((rs - r0 + halo) * GRID_W, GRID_W)
        bvar = rs - r + NA_KH - 1
        qrow = pl.multiple_of(u * GRID_W, GRID_W)
        for g in range(C_HEADS // 2):
            qg = q_ref[pl.ds(qrow, GRID_W), g * LANES:(g + 1) * LANES]
            qs = jnp.concatenate([jnp.where(lo_lanes, qg, jnp.zeros_like(qg)),
                                  jnp.where(lo_lanes, jnp.zeros_like(qg), qg)], axis=0)
            kg = kcat[pl.ds(off, kwin_len), g * LANES:(g + 1) * LANES]
            vg = vcat[pl.ds(off, kwin_len), g * LANES:(g + 1) * LANES]
            sc = lax.dot_general(kg, qs, (((1,), (1,)), ((), ())), preferred_element_type=F32)
            sc = sc + jnp.concatenate([bias_ref[g * nrel + bvar + kr] for kr in range(NA_KH)], axis=0)
            m = jnp.max(sc, axis=0, keepdims=True)
            e = jnp.exp2(sc - m)
            den = jnp.sum(e, axis=0, keepdims=True)
            p = (e * (1.0 / den)).astype(BF16)
            pv = lax.dot_general(p, vg, (((0,), (0,)), ((), ())), preferred_element_type=F32)
            o = jnp.where(lo_lanes, pv[0:GRID_W], pv[GRID_W:])
            o_ref[pl.ds(qrow, GRID_W), g * LANES:(g + 1) * LANES] = o.astype(o_ref.dtype)
        return carry

    lax.fori_loop(0, NA_TROWS, row_body, 0, unroll=True)


def _na_attn(q, k, v, bias_tab, tile_r0, tile_nr):
    t = q.shape[0]
    halo = NA_KH // 2
    tq = NA_TROWS * GRID_W
    hrows = halo * GRID_W
    per = tq // hrows
    nhb = t // hrows
    nt = t // tq
    qmap = lambda j, a, b: (j, 0)
    pmap = lambda j, a, b: (jnp.maximum(j * per - 1, 0), 0)
    nmap = lambda j, a, b: (jnp.minimum((j + 1) * per, nhb - 1), 0)
    gs = pltpu.PrefetchScalarGridSpec(
        num_scalar_prefetch=2, grid=(nt,),
        in_specs=[pl.BlockSpec((tq, C_W), qmap),
                  pl.BlockSpec((hrows, C_W), pmap), pl.BlockSpec((tq, C_W), qmap), pl.BlockSpec((hrows, C_W), nmap),
                  pl.BlockSpec((hrows, C_W), pmap), pl.BlockSpec((tq, C_W), qmap), pl.BlockSpec((hrows, C_W), nmap),
                  pl.BlockSpec(bias_tab.shape, lambda j, a, b: (0, 0, 0))],
        out_specs=pl.BlockSpec((tq, C_W), qmap),
        scratch_shapes=[pltpu.VMEM((tq + 2 * hrows, C_W), BF16), pltpu.VMEM((tq + 2 * hrows, C_W), BF16)])
    return pl.pallas_call(
        functools.partial(_na_kernel, halo=halo),
        grid_spec=gs,
        out_shape=jax.ShapeDtypeStruct((t, C_W), BF16),
        compiler_params=_cparams(("arbitrary",)),
        name="na_attn",
    )(tile_r0, tile_nr, q, k, k, k, v, v, v, bias_tab)


def _na_bias_table(rpb):
    c = np.arange(GRID_W)
    cs = np.clip(c - NA_KW // 2, 0, GRID_W - NA_KW)
    kc = np.arange(GRID_W)
    valid = (kc[:, None] >= cs[None, :]) & (kc[:, None] < cs[None, :] + NA_KW)
    span = GRID_W - NA_KW
    nrel = 2 * NA_KH - 1
    rpb2 = rpb.astype(F32).reshape(C_HEADS // 2, 2, nrel, 2 * NA_KW - 1) * LOG2E
    rev = jnp.flip(jnp.pad(rpb2, ((0, 0), (0, 0), (0, 0), (span, span)), mode="edge"), axis=-1)
    row = jnp.pad(rev, ((0, 0), (0, 0), (0, 0), (0, 1)))
    skew = jnp.broadcast_to(row[..., None, :], row.shape[:-1] + (GRID_W, 2 * GRID_W))
    skew = skew.reshape(row.shape[:-1] + (2 * GRID_W * GRID_W,))[..., :GRID_W * (2 * GRID_W - 1)]
    tab = skew.reshape(row.shape[:-1] + (GRID_W, 2 * GRID_W - 1))[..., GRID_W - 1:]
    tab = jnp.where(jnp.asarray(valid), tab, NEG_INF)
    pairs = jnp.concatenate([tab[:, 0], tab[:, 1]], axis=-1)
    return pairs.reshape((C_HEADS // 2) * (2 * NA_KH - 1), GRID_W, LANES)


def _route(h2, wr_ref, br_ref, idx_ref, gate_ref):
    tm = h2.shape[0]
    h_hi = h2.astype(BF16)
    h_lo = (h2 - h_hi.astype(F32)).astype(BF16)
    w = wr_ref[...]
    w_hi = w.astype(BF16)
    w_lo = (w - w_hi.astype(F32)).astype(BF16)
    nt = (((1,), (1,)), ((), ()))
    logits = (lax.dot_general(w_hi, h_hi, nt, preferred_element_type=F32)
              + lax.dot_general(w_lo, h_hi, nt, preferred_element_type=F32)
              + lax.dot_general(w_hi, h_lo, nt, preferred_element_type=F32)) + br_ref[:, 0:1]
    eid = lax.broadcasted_iota(jnp.int32, (N_EXPERTS, tm), 0)
    vals, idxs = [], []
    for _ in range(TOP_K):
        m = jnp.max(logits, axis=0, keepdims=True)
        ix = jnp.min(jnp.where(logits == m, eid, N_EXPERTS), axis=0, keepdims=True)
        vals.append(m)
        idxs.append(ix)
        logits = jnp.where(eid == ix, -jnp.inf, logits)
    es = [jnp.exp(vk - vals[0]) for vk in vals]
    den = es[0] + es[1] + es[2] + es[3]
    pad_i = jnp.zeros((8 - TOP_K, tm), jnp.int32)
    pad_f = jnp.zeros((8 - TOP_K, tm), F32)
    idx_ref[...] = jnp.concatenate(idxs + [pad_i], axis=0)
    gate_ref[...] = jnp.concatenate([ek / den for ek in es] + [pad_f], axis=0)


PACK_W = D_MODEL // 2


def _pack_bf16_rows(h):
    bits = pltpu.bitcast(h.astype(BF16).astype(F32), jnp.int32)
    return (bits[:, :PACK_W] & jnp.int32(-65536)) | lax.shift_right_logical(bits[:, PACK_W:], 16)


def _unpack_rows_f32(w):
    hi = pltpu.bitcast(w & jnp.int32(-65536), F32)
    lo = pltpu.bitcast(lax.shift_left(w, 16), F32)
    return jnp.concatenate([hi, lo], axis=1)


def _unpack_bf16_rows(w):
    return _unpack_rows_f32(w).astype(BF16)


def _out0_kernel(sid_ref, *refs, tm, part_starts):
    del sid_ref
    npart = len(part_starts)
    x_refs = refs[:npart]
    (mod_ref, n2_ref, oa_ref, ob0_ref, ls0_ref, ob1_ref, ls1_ref, ob2_ref, ls2_ref,
     wout_ref, wr_ref, br_ref, x1_ref, h2_ref, idx_ref, gate_ref, oscr, lscr) = refs[npart:]
    os_, ls_ = [], []
    for gi, (o_ref, l_ref) in enumerate(((ob0_ref, ls0_ref), (ob1_ref, ls1_ref), (ob2_ref, ls2_ref))):
        dil = B_PAIRS[gi][1]
        if dil == 1:
            os_.append(o_ref[...])
            ls_.append(l_ref[...])
        else:
            ncg = B_GW // LANES
            for r in range(dil):
                for c in range(ncg):
                    lo = r * B_GW + c * LANES
                    oscr[gi * ncg + c, pl.ds(r, tm // dil, stride=dil), :] = o_ref[:, lo:lo + LANES]
                lscr[gi, pl.ds(r, tm // dil, stride=dil), :] = l_ref[:, r * LANES:(r + 1) * LANES]
            os_.append(jnp.concatenate([oscr[gi * ncg + c] for c in range(ncg)], axis=1))
            ls_.append(lscr[gi])
    lmax = jnp.maximum(jnp.maximum(ls_[0], ls_[1]), ls_[2])
    ws = [jnp.exp(l - lmax) for l in ls_]
    winv = 1.0 / (ws[0] + ws[1] + ws[2])
    lane = lax.broadcasted_iota(jnp.int32, (tm, LANES), 1)
    lo_lanes = lane < HEAD_DIM

    def head_weights(w):
        return jnp.concatenate(
            [jnp.where(lo_lanes, jnp.broadcast_to(w[:, 2 * pr:2 * pr + 1], (tm, LANES)),
                       jnp.broadcast_to(w[:, 2 * pr + 1:2 * pr + 2], (tm, LANES)))
             for pr in range(B_HEADS // 2)], axis=1)

    ob = (head_weights(ws[0] * winv) * os_[0] + head_weights(ws[1] * winv) * os_[1]
          + head_weights(ws[2] * winv) * os_[2])
    o = (jnp.dot(oa_ref[...], wout_ref[0:A_QW, :], preferred_element_type=F32)
         + jnp.dot(ob.astype(BF16), wout_ref[A_QW:, :], preferred_element_type=F32))
    x1 = _read_part(x_refs, part_starts) + mod_ref[2:3, :] * o
    x1_ref[...] = x1
    h2 = _rms_mod(x1, n2_ref, mod_ref, 3, 4)
    h2_ref[...] = _pack_bf16_rows(h2)
    _route(h2, wr_ref, br_ref, idx_ref, gate_ref)


def _out1_kernel(sid_ref, *refs, part_starts):
    del sid_ref
    npart = len(part_starts)
    x_refs = refs[:npart]
    mod_ref, n2_ref, oc_ref, wout_ref, wr_ref, br_ref, x1_ref, h2_ref, idx_ref, gate_ref = refs[npart:]
    o = jnp.dot(oc_ref[...], wout_ref[...], preferred_element_type=F32)
    x1 = _read_part(x_refs, part_starts) + mod_ref[2:3, :] * o
    x1_ref[...] = x1
    h2 = _rms_mod(x1, n2_ref, mod_ref, 3, 4)
    h2_ref[...] = _pack_bf16_rows(h2)
    _route(h2, wr_ref, br_ref, idx_ref, gate_ref)


def _out_proj(layer, x_parts, mod, tile_sid, n2, attn, wout_bf, wr_t, br):
    tm = TOKEN_TILE
    part_starts, ntiles = _part_tiles(x_parts, tm)
    t = ntiles * tm
    row = lambda i, sid: (i, 0)
    const = lambda i, sid: (0, 0)
    in_specs = _part_specs(x_parts, part_starts, tm, D_MODEL) + [
        pl.BlockSpec((None, 6, D_MODEL), lambda i, sid: (sid[i], 0, 0)),
        pl.BlockSpec((1, D_MODEL), const)]
    scratch = []
    if layer == 0:
        in_specs.append(pl.BlockSpec((tm, A_QW), row))
        for (_, dil) in B_PAIRS:
            in_specs += [pl.BlockSpec((tm // dil, dil * B_GW), row), pl.BlockSpec((tm // dil, dil * LANES), row)]
        body = functools.partial(_out0_kernel, tm=tm, part_starts=part_starts)
        nscr = B_GROUPS * B_GW // LANES
        scratch = [pltpu.VMEM((nscr, tm, LANES), F32), pltpu.VMEM((B_GROUPS, tm, LANES), F32)]
    else:
        in_specs.append(pl.BlockSpec((tm, C_W), row))
        body = functools.partial(_out1_kernel, part_starts=part_starts)
    in_specs += [pl.BlockSpec(wout_bf.shape, const),
                 pl.BlockSpec((N_EXPERTS, D_MODEL), const),
                 pl.BlockSpec((N_EXPERTS, LANES), const)]
    gs = pltpu.PrefetchScalarGridSpec(
        num_scalar_prefetch=1, grid=(t // tm,), in_specs=in_specs,
        out_specs=[pl.BlockSpec((tm, D_MODEL), row), pl.BlockSpec((tm, PACK_W), row),
                   pl.BlockSpec((8, tm), lambda i, sid: (0, i)), pl.BlockSpec((8, tm), lambda i, sid: (0, i))],
        scratch_shapes=scratch)
    return pl.pallas_call(
        body, grid_spec=gs,
        out_shape=[jax.ShapeDtypeStruct((t, D_MODEL), F32), jax.ShapeDtypeStruct((t, PACK_W), jnp.int32),
                   jax.ShapeDtypeStruct((8, t), jnp.int32), jax.ShapeDtypeStruct((8, t), F32)],
        compiler_params=_cparams(("arbitrary",)),
        name="out_proj%d" % layer,
    )(tile_sid, *x_parts, mod, n2.reshape(1, D_MODEL), *attn, wout_bf, wr_t, br)


def _rank_kernel(idx_ref, tri_ref, rank_ref, cnt_ref, carry):
    i = pl.program_id(0)

    @pl.when(i == 0)
    def _():
        carry[...] = jnp.zeros_like(carry)

    tk = idx_ref.shape[1]
    eid = lax.broadcasted_iota(jnp.int32, (N_EXPERTS, tk), 0)
    hits = [eid == idx_ref[k:k + 1, :] for k in range(TOP_K)]
    onehot = sum(h.astype(F32) for h in hits)
    incl = jnp.dot(onehot.astype(BF16), tri_ref[...], preferred_element_type=F32)
    before = incl - onehot + carry[:, 0:1]
    rows = [jnp.sum(jnp.where(h, before, 0.0), axis=0, keepdims=True) for h in hits]
    rows.append(jnp.zeros((8 - TOP_K, tk), F32))
    rank_ref[...] = jnp.concatenate(rows, axis=0).astype(jnp.int32)
    carry[...] = carry[...] + incl[:, tk - 1:tk]
    cnt_ref[...] = carry[...].astype(jnp.int32)


def _slot_kernel(idx_ref, rank_ref, start_ref, slot_ref):
    tk = idx_ref.shape[1]
    eid = lax.broadcasted_iota(jnp.int32, (N_EXPERTS, tk), 0)
    start = start_ref[:, 0:1]
    rows = []
    for k in range(TOP_K):
        base = jnp.sum(jnp.where(eid == idx_ref[k:k + 1, :], start, 0), axis=0, keepdims=True)
        rows.append(base + rank_ref[k:k + 1, :])
    rows.append(jnp.zeros((8 - TOP_K, tk), jnp.int32))
    slot_ref[...] = jnp.concatenate(rows, axis=0)


def _route_slots(idx):
    t = idx.shape[1]
    tk = ROUTE_TILE
    tri = jnp.asarray(np.triu(np.ones((tk, tk), np.float32)), BF16)
    tok = lambda i: (0, i)
    rank, cnt = pl.pallas_call(
        _rank_kernel, grid=(t // tk,),
        in_specs=[pl.BlockSpec((8, tk), tok), pl.BlockSpec((tk, tk), lambda i: (0, 0))],
        out_specs=[pl.BlockSpec((8, tk), tok), pl.BlockSpec((N_EXPERTS, LANES), lambda i: (0, 0))],
        out_shape=[jax.ShapeDtypeStruct((8, t), jnp.int32), jax.ShapeDtypeStruct((N_EXPERTS, LANES), jnp.int32)],
        scratch_shapes=[pltpu.VMEM((N_EXPERTS, LANES), F32)],
        compiler_params=_cparams(("arbitrary",)),
        name="route_rank",
    )(idx, tri)
    counts = cnt[:, 0]
    padded = (counts + MOE_BM - 1) // MOE_BM * MOE_BM
    pad_end = jnp.cumsum(padded)
    pad_start = pad_end - padded
    nblk = (t * TOP_K) // MOE_BM + N_EXPERTS
    blk_row0 = jnp.arange(nblk, dtype=jnp.int32) * MOE_BM
    blk_expert = jnp.minimum(jnp.sum((pad_end[None, :] <= blk_row0[:, None]).astype(jnp.int32), axis=1),
                             N_EXPERTS - 1)
    nused = (pad_end[-1] // MOE_BM).astype(jnp.int32).reshape(1)
    eids = jnp.arange(N_EXPERTS, dtype=jnp.int32)
    later = (eids[None, :] > eids[:, None]) & (padded[None, :] > 0)
    next_expert = jnp.min(jnp.where(later, eids[None, :], N_EXPERTS), axis=1)
    next_expert = jnp.where(next_expert == N_EXPERTS, -1, next_expert).astype(jnp.int32)
    blk_next = jnp.sum(jnp.where(blk_expert[:, None] == eids[None, :], next_expert[None, :], 0), axis=1)
    start_b = jnp.broadcast_to(pad_start.astype(jnp.int32)[:, None], (N_EXPERTS, LANES))
    slot = pl.pallas_call(
        _slot_kernel, grid=(t // tk,),
        in_specs=[pl.BlockSpec((8, tk), tok), pl.BlockSpec((8, tk), tok),
                  pl.BlockSpec((N_EXPERTS, LANES), lambda i: (0, 0))],
        out_specs=pl.BlockSpec((8, tk), tok),
        out_shape=jax.ShapeDtypeStruct((8, t), jnp.int32),
        compiler_params=_cparams(("arbitrary",)),
        name="route_slot",
    )(idx, rank, start_b)
    return slot, blk_expert, nused, blk_next


SC_CORES = 2
SC_SUBCORES = 16
SC_ROW_BUFFER_BYTES = 256 * 1024
SC_MAX_INDEX_LIST = 128


def _sc_chunks(t, row_bytes):
    chunk = min(SC_MAX_INDEX_LIST, SC_ROW_BUFFER_BYTES // row_bytes)
    nchunk = t // chunk
    per = nchunk // (SC_CORES * SC_SUBCORES)
    assert per * SC_CORES * SC_SUBCORES * chunk == t
    return chunk, nchunk, per


def _sc_slot_lists(slot, nchunk, chunk):
    return slot[:TOP_K].reshape(TOP_K, nchunk, chunk).transpose(1, 0, 2)


def _sc_dispatch(h2, slot, nslots):
    t, w = h2.shape
    chunk, nchunk, per = _sc_chunks(t, w * h2.dtype.itemsize)
    mesh = plsc.VectorSubcoreMesh(core_axis_name="c", subcore_axis_name="s")

    @functools.partial(
        pl.kernel, mesh=mesh, out_type=jax.ShapeDtypeStruct((nslots, w), h2.dtype),
        scratch_types=[pltpu.VMEM((TOP_K, chunk), jnp.int32), pltpu.VMEM((chunk, w), h2.dtype)],
        name="moe_dispatch_sc")
    def body(h_hbm, slot_hbm, xs_hbm, idx_v, rows_v):
        wid = lax.axis_index("s") * SC_CORES + lax.axis_index("c")

        @pl.loop(0, per)
        def _(j):
            c = wid * per + j
            pltpu.sync_copy(slot_hbm.at[c], idx_v)
            pltpu.sync_copy(h_hbm.at[pl.ds(c * chunk, chunk)], rows_v)
            for k in range(TOP_K):
                pltpu.sync_copy(rows_v, xs_hbm.at[idx_v.at[k]])

    return body(h2, _sc_slot_lists(slot, nchunk, chunk))


def _sc_gather(ys, slot, t):
    w = ys.shape[1]
    nbuf = 2
    chunk, nchunk, per = _sc_chunks(t, nbuf * w * ys.dtype.itemsize)
    mesh = plsc.VectorSubcoreMesh(core_axis_name="c", subcore_axis_name="s")

    @functools.partial(
        pl.kernel, mesh=mesh, out_type=jax.ShapeDtypeStruct((TOP_K, t, w), ys.dtype),
        scratch_types=[pltpu.VMEM((TOP_K, chunk), jnp.int32), pltpu.VMEM((nbuf, chunk, w), ys.dtype),
                       pltpu.SemaphoreType.DMA((nbuf,))],
        name="moe_gather_sc")
    def body(ys_hbm, slot_hbm, out_hbm, idx_v, rows_v, wsem):
        wid = lax.axis_index("s") * SC_CORES + lax.axis_index("c")

        @pl.loop(0, per)
        def _(j):
            c = wid * per + j
            pltpu.sync_copy(slot_hbm.at[c], idx_v)
            writes = []
            for k in range(TOP_K):
                b = k % nbuf
                if k >= nbuf:
                    writes[k - nbuf].wait()
                pltpu.sync_copy(ys_hbm.at[idx_v.at[k]], rows_v.at[b])
                writes.append(pltpu.async_copy(rows_v.at[b], out_hbm.at[k, pl.ds(c * chunk, chunk)], wsem.at[b]))
            for wr in writes[-nbuf:]:
                wr.wait()

    return body(ys, _sc_slot_lists(slot, nchunk, chunk))


def _ffn_kernel(be_ref, nu_ref, nxt_ref, xs_ref, wgu_hbm, bgu_ref, wd_hbm, bd_ref, y_ref,
                wgu_f, wd_f, wgu_bf, wd_bf, wsem):
    i = pl.program_id(0)
    prev = be_ref[jnp.maximum(i - 1, 0)]

    def weight_copies(e):
        return (pltpu.make_async_copy(wgu_hbm.at[e], wgu_f, wsem.at[0]),
                pltpu.make_async_copy(wd_hbm.at[e], wd_f, wsem.at[1]))

    @pl.when(i == 0)
    def _():
        for cp in weight_copies(be_ref[0]):
            cp.start()

    @pl.when(i < nu_ref[0])
    def _():
        @pl.when((i == 0) | (be_ref[i] != prev))
        def _():
            for cp in weight_copies(be_ref[i]):
                cp.wait()
            wgu_bf[...] = wgu_f[...].astype(BF16)
            wd_bf[...] = wd_f[...].astype(BF16)

            @pl.when(nxt_ref[i] >= 0)
            def _():
                for cp in weight_copies(nxt_ref[i]):
                    cp.start()

        x = _unpack_bf16_rows(xs_ref[...])
        gu = jnp.dot(x, wgu_bf[...], preferred_element_type=F32) + bgu_ref[...]
        gate = jnp.minimum(gu[:, :D_FF], SWIGLU_LIMIT)
        up = jnp.clip(gu[:, D_FF:], -SWIGLU_LIMIT, SWIGLU_LIMIT)
        act = (up + 1.0) * gate * jax.nn.sigmoid(SWIGLU_ALPHA * gate)
        y = jnp.dot(act.astype(BF16), wd_bf[...], preferred_element_type=F32) + bd_ref[...]
        y_ref[...] = _pack_bf16_rows(y)

    @pl.when(i >= nu_ref[0])
    def _():
        y_ref[...] = jnp.zeros_like(y_ref)


def _expert_ffn(xs, blk_expert, nused, blk_next, wgu, bgu, wd, bd):
    nslots = xs.shape[0]
    nblk = nslots // MOE_BM
    blk = lambda i, be, nu, nx: (jnp.minimum(i, nu[0] - 1), 0)
    exp3 = lambda i, be, nu, nx: (be[jnp.minimum(i, nu[0] - 1)], 0, 0)
    gs = pltpu.PrefetchScalarGridSpec(
        num_scalar_prefetch=3, grid=(nblk,),
        in_specs=[pl.BlockSpec((MOE_BM, PACK_W), blk),
                  pl.BlockSpec(memory_space=pl.ANY),
                  pl.BlockSpec((None, 1, 2 * D_FF), exp3),
                  pl.BlockSpec(memory_space=pl.ANY),
                  pl.BlockSpec((None, 1, D_MODEL), exp3)],
        out_specs=pl.BlockSpec((MOE_BM, PACK_W), lambda i, be, nu, nx: (i, 0)),
        scratch_shapes=[pltpu.VMEM((D_MODEL, 2 * D_FF), F32), pltpu.VMEM((D_FF, D_MODEL), F32),
                        pltpu.VMEM((D_MODEL, 2 * D_FF), BF16), pltpu.VMEM((D_FF, D_MODEL), BF16),
                        pltpu.SemaphoreType.DMA((2,))])
    return pl.pallas_call(
        _ffn_kernel, grid_spec=gs,
        out_shape=jax.ShapeDtypeStruct((nslots, PACK_W), jnp.int32),
        compiler_params=_cparams(("arbitrary",)),
        name="moe_ffn",
    )(blk_expert, nused, blk_next, xs, wgu, bgu.reshape(N_EXPERTS, 1, 2 * D_FF), wd,
      bd.reshape(N_EXPERTS, 1, D_MODEL))


def _combine_kernel(sid_ref, x_ref, mod_ref, gate_ref, yk_ref, *o_refs, tm, out_starts):
    del sid_ref
    gpad = jnp.concatenate([gate_ref[...], jnp.zeros((LANES - 8, tm), F32)], axis=0)
    gcol = gpad.T
    acc = gcol[:, 0:1] * _unpack_rows_f32(yk_ref[0])
    for k in range(1, TOP_K):
        acc = acc + gcol[:, k:k + 1] * _unpack_rows_f32(yk_ref[k])
    out = x_ref[...] + mod_ref[5:6, :] * acc
    i = pl.program_id(0)
    ends = out_starts[1:] + (None,)
    for o_ref, s0, s1 in zip(o_refs, out_starts, ends):
        mine = (i >= s0) if s1 is None else ((i >= s0) & (i < s1))

        @pl.when(mine)
        def _():
            o_ref[...] = out


def _combine(x1, mod, tile_sid, gates, yk, out_rows):
    t = x1.shape[0]
    tm = MOVE_TILE
    out_shape = [jax.ShapeDtypeStruct((n, D_MODEL), F32) for n in out_rows]
    out_starts, ntiles = _part_tiles(out_shape, tm)
    assert ntiles * tm == t
    gs = pltpu.PrefetchScalarGridSpec(
        num_scalar_prefetch=1, grid=(t // tm,),
        in_specs=[pl.BlockSpec((tm, D_MODEL), lambda i, sid: (i, 0)),
                  pl.BlockSpec((None, 6, D_MODEL), lambda i, sid: (sid[i], 0, 0)),
                  pl.BlockSpec((8, tm), lambda i, sid: (0, i)),
                  pl.BlockSpec((TOP_K, tm, PACK_W), lambda i, sid: (0, i, 0))],
        out_specs=_part_specs(out_shape, out_starts, tm, D_MODEL))
    return pl.pallas_call(
        functools.partial(_combine_kernel, tm=tm, out_starts=out_starts), grid_spec=gs,
        out_shape=out_shape,
        compiler_params=_cparams(("arbitrary",)),
        name="moe_combine",
    )(tile_sid, x1, mod, gates, yk)


def _moe(x1, h2, idx, gates, mod, tile_sid_move, wgu, bgu, wd, bd, out_rows):
    t = x1.shape[0]
    nslots = t * TOP_K + N_EXPERTS * MOE_BM
    slot, blk_expert, nused, blk_next = _route_slots(idx)
    xs = _sc_dispatch(h2, slot, nslots)
    ys = _expert_ffn(xs, blk_expert, nused, blk_next, wgu, bgu, wd, bd)
    return _combine(x1, mod, tile_sid_move, gates, _sc_gather(ys, slot, t), out_rows)


def _tile_table(seq_lens, tile, fn):
    vals = []
    start = 0
    for sid, n in enumerate(seq_lens):
        assert n % tile == 0
        for j in range(n // tile):
            vals.append(fn(sid, start, n, j))
        start += n
    return jnp.asarray(np.asarray(vals, np.int32))


def _head_block_diag():
    bd = np.kron(np.eye(2 * LANES // HEAD_DIM, dtype=np.float32),
                 np.full((HEAD_DIM, HEAD_DIM), 1.0 / HEAD_DIM, np.float32))
    return jnp.asarray(bd, BF16)


_A_HEAD_ORDER = [kv * A_GROUP + g for g in range(A_GROUP) for kv in range(A_KV_HEADS)]


def _trunk(x_parts, mods, mod_row0, seq_lens, p):
    t = sum(a.shape[0] for a in x_parts)
    sid_tok = _tile_table(seq_lens, TOKEN_TILE, lambda sid, s, n, j: mod_row0 + sid)
    sid_move = _tile_table(seq_lens, MOVE_TILE, lambda sid, s, n, j: mod_row0 + sid)
    bd = _head_block_diag()
    mod0, mod1 = mods

    w_in = p["l0_w_in"]
    a_cols = np.concatenate([np.arange(h * HEAD_DIM, (h + 1) * HEAD_DIM) for h in _A_HEAD_ORDER])
    bq, bk, bv = A_IN, A_IN + B_GROUPS * B_GW, A_IN + 2 * B_GROUPS * B_GW
    col_blocks = [w_in[:, a_cols], w_in[:, A_QW:A_IN]]
    for g in range(B_GROUPS):
        col_blocks += [w_in[:, bq + g * B_GW:bq + (g + 1) * B_GW],
                       w_in[:, bk + g * B_GW:bk + (g + 1) * B_GW],
                       w_in[:, bv + g * B_GW:bv + (g + 1) * B_GW]]
    w0 = jnp.concatenate(col_blocks, axis=1).astype(BF16)
    ones = lambda n: jnp.ones((n,), F32)
    gain_blocks = [jnp.tile(p["l0_q_norm_a"], A_Q_HEADS) * QK_SCALE, jnp.tile(p["l0_k_norm_a"], A_KV_HEADS),
                   ones(A_KVW)]
    for g in range(B_GROUPS):
        gain_blocks += [jnp.tile(p["l0_q_norm_b"], B_HEADS) * QK_SCALE, jnp.tile(p["l0_k_norm_b"], B_HEADS),
                        ones(B_GW)]
    gain0 = jnp.concatenate(gain_blocks).reshape(1, -1)
    slabs0 = [(0, A_QW, True, 1), (A_QW, A_KVW, True, 1), (A_QW + A_KVW, A_KVW, False, 1)]
    c0 = A_IN
    for g, (_, dil) in enumerate(B_PAIRS):
        slabs0 += [(c0, B_GW, True, dil), (c0 + B_GW, B_GW, True, dil), (c0 + 2 * B_GW, B_GW, False, dil)]
        c0 += 3 * B_GW
    proj = _in_proj(x_parts, mod0, sid_tok, p["l0_norm1"], w0, gain0, bd, tuple(slabs0))
    qa, ka, va = proj[0:3]

    slopes_a = _alibi_slopes(A_Q_HEADS)
    heads_a = tuple(tuple((slopes_a[kv * A_GROUP + g], kv * A_GROUP + g) for kv in range(A_KV_HEADS))
                    for g in range(A_GROUP))
    (oa,) = _band_attn(qa, ka, va, seq_lens, tq=ATTN_TQ_A, dil=1, qw=A_QW, kvw=A_KVW, half=A_HALF, heads=heads_a,
                       kv_group=(0,) * A_GROUP, sink=p["l0_sink_a"].astype(F32), out_dtype=BF16, want_lse=False)

    slopes_b = _alibi_slopes(B_GROUPS * B_HEADS)
    attn0 = [oa]
    for g, (window, dil) in enumerate(B_PAIRS):
        qg, kg, vg = proj[3 + 3 * g:6 + 3 * g]
        heads_b = tuple(tuple((slopes_b[g * B_HEADS + 2 * pr + s] * dil, 0) for s in range(2))
                        for pr in range(B_HEADS // 2))
        og, lg = _band_attn(qg, kg, vg, [n // dil for n in seq_lens], tq=ATTN_TQ_B, dil=dil, qw=B_GW, kvw=B_GW, half=window // (2 * dil),
                            heads=heads_b, kv_group=tuple(range(B_HEADS // 2)), sink=None, out_dtype=F32,
                            want_lse=True)
        attn0 += [og, lg]

    w_out = p["l0_w_out"]
    wout0 = jnp.concatenate([w_out[a_cols], w_out[A_QW:]], axis=0).astype(BF16)
    br0 = jnp.broadcast_to(p["l0_b_router"].astype(F32)[:, None], (N_EXPERTS, LANES))
    x1, h2, idx, gates = _out_proj(0, x_parts, mod0, sid_tok, p["l0_norm2"], attn0, wout0, p["l0_w_router"].T,
                                   br0)
    x = _moe(x1, h2, idx, gates, mod0, sid_move, p["l0_w_gate_up"], p["l0_b_gate_up"], p["l0_w_down"],
             p["l0_b_down"], [t])

    w1 = p["l1_w_in"].astype(BF16)
    gain1 = jnp.concatenate([jnp.tile(p["l1_q_norm_c"], C_HEADS) * QK_SCALE, jnp.tile(p["l1_k_norm_c"], C_HEADS),
                             ones(C_W)]).reshape(1, -1)
    slabs1 = ((0, C_W, True, 1), (C_W, C_W, True, 1), (2 * C_W, C_W, False, 1))
    qc, kc, vc = _in_proj(x, mod1, sid_tok, p["l1_norm1"], w1, gain1, bd, slabs1)
    na_tile = NA_TROWS * GRID_W
    tile_r0 = _tile_table(seq_lens, na_tile, lambda sid, s, n, j: j * NA_TROWS)
    tile_nr = _tile_table(seq_lens, na_tile, lambda sid, s, n, j: n // GRID_W)
    oc = _na_attn(qc, kc, vc, _na_bias_table(p["l1_rpb_c"]), tile_r0, tile_nr)
    br1 = jnp.broadcast_to(p["l1_b_router"].astype(F32)[:, None], (N_EXPERTS, LANES))
    x1, h2, idx, gates = _out_proj(1, x, mod1, sid_tok, p["l1_norm2"], [oc], p["l1_w_out"].astype(BF16),
                                   p["l1_w_router"].T, br1)
    return _moe(x1, h2, idx, gates, mod1, sid_move, p["l1_w_gate_up"], p["l1_b_gate_up"], p["l1_w_down"],
                p["l1_b_down"], [a.shape[0] for a in x_parts])


def kernel(x_prompt, x_sample, c_prompt, c_sample, l0_ada_w, l0_ada_b, l0_norm1, l0_w_in, l0_q_norm_a, l0_k_norm_a, l0_sink_a, l0_q_norm_b, l0_k_norm_b, l0_w_out, l0_norm2, l0_w_router, l0_b_router, l0_w_gate_up, l0_b_gate_up, l0_w_down, l0_b_down, l1_ada_w, l1_ada_b, l1_norm1, l1_w_in, l1_q_norm_c, l1_k_norm_c, l1_rpb_c, l1_w_out, l1_norm2, l1_w_router, l1_b_router, l1_w_gate_up, l1_b_gate_up, l1_w_down, l1_b_down):
    p = dict(l0_ada_w=l0_ada_w, l0_ada_b=l0_ada_b, l0_norm1=l0_norm1, l0_w_in=l0_w_in, l0_q_norm_a=l0_q_norm_a,
             l0_k_norm_a=l0_k_norm_a, l0_sink_a=l0_sink_a, l0_q_norm_b=l0_q_norm_b, l0_k_norm_b=l0_k_norm_b,
             l0_w_out=l0_w_out, l0_norm2=l0_norm2, l0_w_router=l0_w_router, l0_b_router=l0_b_router,
             l0_w_gate_up=l0_w_gate_up, l0_b_gate_up=l0_b_gate_up, l0_w_down=l0_w_down, l0_b_down=l0_b_down,
             l1_ada_w=l1_ada_w, l1_ada_b=l1_ada_b, l1_norm1=l1_norm1, l1_w_in=l1_w_in, l1_q_norm_c=l1_q_norm_c,
             l1_k_norm_c=l1_k_norm_c, l1_rpb_c=l1_rpb_c, l1_w_out=l1_w_out, l1_norm2=l1_norm2,
             l1_w_router=l1_w_router, l1_b_router=l1_b_router, l1_w_gate_up=l1_w_gate_up,
             l1_b_gate_up=l1_b_gate_up, l1_w_down=l1_w_down, l1_b_down=l1_b_down)
    nb_p, len_p, d = x_prompt.shape
    nb_s, len_s, _ = x_sample.shape
    nseq = nb_p + nb_s
    c_pad = jnp.concatenate([c_prompt, c_sample, jnp.zeros((-nseq % 8, d), F32)], axis=0)
    mods = tuple(_ada_mod(c_pad, p[l + "_ada_w"], p[l + "_ada_b"]).reshape(c_pad.shape[0], 6, D_MODEL)
                 for l in ("l0", "l1"))
    outs = []
    for xg, row0 in ((x_prompt, 0), (x_sample, nb_p)):
        nb, ln, _ = xg.shape
        (y,) = _trunk([xg.reshape(nb * ln, d)], mods, row0, (ln,) * nb, p)
        outs.append(y.reshape(nb, ln, d))
    return tuple(outs)
```

```python
import functools
import math

import numpy as np
import jax
import jax.numpy as jnp
from jax import lax
from jax.experimental import pallas as pl
from jax.experimental.pallas import tpu as pltpu
from jax.experimental.pallas import tpu_sc as plsc

F32 = jnp.float32
BF16 = jnp.bfloat16
HIGHEST = lax.Precision.HIGHEST

D_MODEL = 1024
HEAD_DIM = 64
LANES = 128
GRID_W = 64
A_Q_HEADS = 8
A_KV_HEADS = 2
A_GROUP = A_Q_HEADS // A_KV_HEADS
A_HALF = 128
B_PAIRS = ((128, 1), (512, 4), (2048, 16))
B_GROUPS = 3
B_HEADS = 4
A_QW = A_Q_HEADS * HEAD_DIM
A_KVW = A_KV_HEADS * HEAD_DIM
A_IN = A_QW + 2 * A_KVW
B_GW = B_HEADS * HEAD_DIM
C_HEADS = 16
C_W = C_HEADS * HEAD_DIM
NA_KH = 8
NA_KW = 16
N_EXPERTS = 32
TOP_K = 4
D_FF = 1024
SWIGLU_LIMIT = 7.0
SWIGLU_ALPHA = 1.702
RMS_EPS = 1e-6
NEG_INF = -1e30
LOG2E = math.log2(math.e)
LN2 = math.log(2.0)
QK_SCALE = HEAD_DIM ** -0.5 * LOG2E

TOKEN_TILE = 1024
ATTN_TQ_A = 1024
ATTN_TQ_B = 512
ATTN_UQ = 128
BAND_RESIDUES_PER_STEP = 4
NA_TROWS = 8
MOE_BM = 512
ROUTE_TILE = 1024
MOVE_TILE = 512
VMEM_LIMIT = 56 * 1024 * 1024


def _alibi_slopes(n):
    return [float(2.0 ** (-8.0 * (j + 1) / n)) for j in range(n)]


def _cparams(sem):
    return pltpu.CompilerParams(dimension_semantics=sem, vmem_limit_bytes=VMEM_LIMIT)


def _mod_kernel(c_ref, w_ref, b_ref, o_ref):
    c = c_ref[...]
    s = c * jax.nn.sigmoid(c)
    o_ref[...] = jnp.dot(s, w_ref[...], precision=HIGHEST, preferred_element_type=F32) + b_ref[...]


def _ada_mod(c_pad, w, b):
    nrow = c_pad.shape[0]
    ncol = w.shape[1]
    tn = D_MODEL
    return pl.pallas_call(
        _mod_kernel,
        grid=(ncol // tn,),
        in_specs=[pl.BlockSpec((nrow, D_MODEL), lambda j: (0, 0)),
                  pl.BlockSpec((D_MODEL, tn), lambda j: (0, j)),
                  pl.BlockSpec((1, tn), lambda j: (0, j))],
        out_specs=pl.BlockSpec((nrow, tn), lambda j: (0, j)),
        out_shape=jax.ShapeDtypeStruct((nrow, ncol), F32),
        compiler_params=_cparams(("arbitrary",)),
        name="ada_mod",
    )(c_pad, w, b.reshape(1, ncol))


def _head_mean_sq(y, bd_ref):
    w = y.shape[1]
    outs = []
    for c0 in range(0, w, 2 * LANES):
        cw = min(2 * LANES, w - c0)
        sq = y[:, c0:c0 + cw] * y[:, c0:c0 + cw]
        outs.append(jnp.dot(sq.astype(BF16), bd_ref[0:cw, 0:cw], preferred_element_type=F32))
    return outs[0] if len(outs) == 1 else jnp.concatenate(outs, axis=1)


def _rms_mod(x, g_ref, mod_ref, shift_row, scale_row):
    ms = jnp.mean(x * x, axis=-1, keepdims=True)
    xn = x * lax.rsqrt(ms + RMS_EPS) * g_ref[...]
    return xn * (1.0 + mod_ref[scale_row:scale_row + 1, :]) + mod_ref[shift_row:shift_row + 1, :]


def _part_tiles(parts, tile):
    starts, s = [], 0
    for a in parts:
        assert a.shape[0] % tile == 0
        starts.append(s)
        s += a.shape[0] // tile
    return tuple(starts), s


def _part_specs(parts, starts, tile, ncol):
    def spec(a, s0):
        n = a.shape[0] // tile
        return pl.BlockSpec((tile, ncol), lambda i, *_: (jnp.clip(i - s0, 0, n - 1), 0))
    return [spec(a, s0) for a, s0 in zip(parts, starts)]


def _read_part(refs, starts):
    i = pl.program_id(0)
    x = refs[0][...]
    for ref, s0 in zip(refs[1:], starts[1:]):
        x = jnp.where(i >= s0, ref[...], x)
    return x


def _in_kernel(sid_ref, *refs, slabs, tm, part_starts):
    del sid_ref
    npart = len(part_starts)
    x_refs = refs[:npart]
    mod_ref, n1_ref, w_ref, gain_ref, bd_ref = refs[npart:npart + 5]
    rest = refs[npart + 5:]
    nout = len(slabs)
    out_refs = rest[:nout]
    scr_ref = rest[nout] if len(rest) > nout else None
    h = _rms_mod(_read_part(x_refs, part_starts), n1_ref, mod_ref, 0, 1).astype(BF16)
    proj = jnp.dot(h, w_ref[...], preferred_element_type=F32)
    for (c0, w, normed, dil), o_ref in zip(slabs, out_refs):
        y = proj[:, c0:c0 + w]
        if normed:
            y = y * lax.rsqrt(_head_mean_sq(y, bd_ref) + RMS_EPS) * gain_ref[:, c0:c0 + w]
        if dil == 1:
            o_ref[...] = y.astype(BF16)
        else:
            for c in range(w // LANES):
                scr_ref[c] = y[:, c * LANES:(c + 1) * LANES]
            for r in range(dil):
                for c in range(w // LANES):
                    o_ref[:, r * w + c * LANES:r * w + (c + 1) * LANES] = (
                        scr_ref[c, pl.ds(r, tm // dil, stride=dil), :].astype(BF16))


def _in_proj(x_parts, mod, tile_sid, n1, w_bf, gain, bd, slabs):
    tm = TOKEN_TILE
    part_starts, ntiles = _part_tiles(x_parts, tm)
    t = ntiles * tm
    ncols = w_bf.shape[1]
    out_shapes, out_specs = [], []
    for (c0, w, normed, dil) in slabs:
        out_shapes.append(jax.ShapeDtypeStruct((t // dil, dil * w), BF16))
        out_specs.append(pl.BlockSpec((tm // dil, dil * w), lambda i, sid: (i, 0)))
    need_scr = any(s[3] > 1 for s in slabs)
    scratch = [pltpu.VMEM((max(s[1] for s in slabs if s[3] > 1) // LANES, tm, LANES), F32)] if need_scr else []
    gs = pltpu.PrefetchScalarGridSpec(
        num_scalar_prefetch=1,
        grid=(t // tm,),
        in_specs=_part_specs(x_parts, part_starts, tm, D_MODEL) + [
            pl.BlockSpec((None, 6, D_MODEL), lambda i, sid: (sid[i], 0, 0)),
            pl.BlockSpec((1, D_MODEL), lambda i, sid: (0, 0)),
            pl.BlockSpec((D_MODEL, ncols), lambda i, sid: (0, 0)),
            pl.BlockSpec((1, ncols), lambda i, sid: (0, 0)),
            pl.BlockSpec((2 * LANES, 2 * LANES), lambda i, sid: (0, 0))],
        out_specs=out_specs,
        scratch_shapes=scratch)
    return pl.pallas_call(
        functools.partial(_in_kernel, slabs=slabs, tm=tm, part_starts=part_starts),
        grid_spec=gs,
        out_shape=out_shapes,
        compiler_params=_cparams(("arbitrary",)),
        name="in_proj",
    )(tile_sid, *x_parts, mod, n1.reshape(1, D_MODEL), w_bf, gain, bd)


def _band_kernel(s0_ref, s1_ref, *refs, tq, hb, half, heads, kv_group, kvw, rps, has_sink, want_lse):
    pos = 0
    sink_ref = None
    if has_sink:
        sink_ref = refs[0]
        pos = 1
    q_ref, kp_ref, kc_ref, kn_ref, vp_ref, vc_ref, vn_ref = refs[pos:pos + 7]
    o_ref = refs[pos + 7]
    lse_ref = refs[pos + 8] if want_lse else None
    i = pl.program_id(1)
    uq = ATTN_UQ
    wlen = uq + 2 * hb
    kall = jnp.concatenate([kp_ref[...], kc_ref[...], kn_ref[...]], axis=0)
    vall = jnp.concatenate([vp_ref[...], vc_ref[...], vn_ref[...]], axis=0)
    lane = lax.broadcasted_iota(jnp.int32, (uq, LANES), 1)
    lo_lanes = lane < HEAD_DIM
    for sub in range(tq // uq):
        rows = slice(sub * uq, (sub + 1) * uq)
        kwin = kall[sub * uq:sub * uq + wlen]
        vwin = vall[sub * uq:sub * uq + wlen]
        q0 = i * tq + sub * uq
        kpos = q0 - hb + lax.broadcasted_iota(jnp.int32, (wlen, uq), 0)
        qpos = q0 + lax.broadcasted_iota(jnp.int32, (wlen, uq), 1)
        rel = kpos - qpos
        valid = (jnp.abs(rel) <= half) & (kpos >= s0_ref[i]) & (kpos < s1_ref[i])
        distm = jnp.where(valid, jnp.abs(rel).astype(F32), 1e32)
        for rr in range(rps):
            lse_rows = []
            for g, pair in enumerate(heads):
                qc = rr * len(heads) + g
                kc = rr * (kvw // LANES) + kv_group[g]
                qg = q_ref[rows, qc * LANES:(qc + 1) * LANES]
                kg = kwin[:, kc * LANES:(kc + 1) * LANES]
                vg = vwin[:, kc * LANES:(kc + 1) * LANES]
                outs = []
                for s, (slope, sink_idx) in enumerate(pair):
                    keep = lo_lanes if s == 0 else jnp.logical_not(lo_lanes)
                    qm = jnp.where(keep, qg, jnp.zeros_like(qg))
                    sc = lax.dot_general(kg, qm, (((1,), (1,)), ((), ())), preferred_element_type=F32)
                    sc = sc - (slope * LOG2E) * distm
                    m = jnp.max(sc, axis=0, keepdims=True)
                    if has_sink:
                        sink2 = sink_ref[sink_idx] * LOG2E
                        m = jnp.maximum(m, sink2)
                    e = jnp.exp2(sc - m)
                    den = jnp.sum(e, axis=0, keepdims=True)
                    if has_sink:
                        den = den + jnp.exp2(sink2 - m)
                    p = (e * (1.0 / den)).astype(BF16)
                    outs.append(lax.dot_general(p, vg, (((0,), (0,)), ((), ())), preferred_element_type=F32))
                    if want_lse:
                        lse_rows.append((m + jnp.log2(den)) * LN2)
                o_ref[rows, qc * LANES:(qc + 1) * LANES] = (
                    jnp.where(lo_lanes, outs[0], outs[1]).astype(o_ref.dtype))
            if want_lse:
                lse_t = jnp.concatenate(lse_rows + [jnp.zeros((LANES - len(lse_rows), uq), F32)], axis=0)
                lse_ref[rows, rr * LANES:(rr + 1) * LANES] = lse_t.T


def _band_attn(q, k, v, seq_rows, *, tq, dil, qw, kvw, half, heads, kv_group, sink, out_dtype, want_lse):
    rows = q.shape[0]
    hb = half
    s0 = _tile_table(seq_rows, tq, lambda sid, s, n, j: s)
    s1 = _tile_table(seq_rows, tq, lambda sid, s, n, j: s + n)
    per = tq // hb
    nt = rows // tq
    nhb = rows // hb
    has_sink = sink is not None
    in_specs = []
    args = []
    if has_sink:
        in_specs.append(pl.BlockSpec(memory_space=pltpu.SMEM))
        args.append(sink)
    qmap = lambda r, i, a, b: (i, r)
    pmap = lambda r, i, a, b: (jnp.maximum(i * per - 1, 0), r)
    nmap = lambda r, i, a, b: (jnp.minimum((i + 1) * per, nhb - 1), r)
    rps = min(dil, BAND_RESIDUES_PER_STEP)
    in_specs += [pl.BlockSpec((tq, rps * qw), qmap),
                 pl.BlockSpec((hb, rps * kvw), pmap), pl.BlockSpec((tq, rps * kvw), qmap),
                 pl.BlockSpec((hb, rps * kvw), nmap),
                 pl.BlockSpec((hb, rps * kvw), pmap), pl.BlockSpec((tq, rps * kvw), qmap),
                 pl.BlockSpec((hb, rps * kvw), nmap)]
    args += [q, k, k, k, v, v, v]
    out_shape = [jax.ShapeDtypeStruct((rows, dil * qw), out_dtype)]
    out_specs = [pl.BlockSpec((tq, rps * qw), qmap)]
    if want_lse:
        out_shape.append(jax.ShapeDtypeStruct((rows, dil * LANES), F32))
        out_specs.append(pl.BlockSpec((tq, rps * LANES), qmap))
    gs = pltpu.PrefetchScalarGridSpec(num_scalar_prefetch=2, grid=(dil // rps, nt),
                                      in_specs=in_specs, out_specs=out_specs)
    res = pl.pallas_call(
        functools.partial(_band_kernel, tq=tq, hb=hb, half=half, heads=heads, kv_group=kv_group, kvw=kvw,
                          rps=rps, has_sink=has_sink, want_lse=want_lse),
        grid_spec=gs,
        out_shape=out_shape,
        compiler_params=_cparams(("arbitrary", "arbitrary")),
        name="band_attn_d%d" % dil,
    )(s0, s1, *args)
    return res


def _na_kernel(r0_ref, nr_ref, q_ref, kp_ref, kc_ref, kn_ref, vp_ref, vc_ref, vn_ref, bias_ref,
               o_ref, kcat, vcat, *, halo):
    j = pl.program_id(0)
    tq = NA_TROWS * GRID_W
    hrows = halo * GRID_W
    kcat[0:hrows, :] = kp_ref[...]
    kcat[hrows:hrows + tq, :] = kc_ref[...]
    kcat[hrows + tq:, :] = kn_ref[...]
    vcat[0:hrows, :] = vp_ref[...]
    vcat[hrows:hrows + tq, :] = vc_ref[...]
    vcat[hrows + tq:, :] = vn_ref[...]
    r0 = r0_ref[j]
    nrows = nr_ref[j]
    kwin_len = NA_KH * GRID_W
    lane = lax.broadcasted_iota(jnp.int32, (GRID_W, LANES), 1)
    lo_lanes = lane < HEAD_DIM
    nrel = 2 * NA_KH - 1

    def row_body(u, carry):
        r = r0 + u
        rs = jnp.clip(r - NA_KH // 2, 0, nrows - NA_KH)
        off = pl.multiple_of((rs - r0 + halo) * GRID_W, GRID_W)
        bvar = rs - r + NA_KH - 1
        qrow = pl.multiple_of(u * GRID_W, GRID_W)
        for g in range(C_HEADS // 2):
            qg = q_ref[pl.ds(qrow, GRID_W), g * LANES:(g + 1) * LANES]
            qs = jnp.concatenate([jnp.where(lo_lanes, qg, jnp.zeros_like(qg)),
                                  jnp.where(lo_lanes, jnp.zeros_like(qg), qg)], axis=0)
            kg = kcat[pl.ds(off, kwin_len), g * LANES:(g + 1) * LANES]
            vg = vcat[pl.ds(off, kwin_len), g * LANES:(g + 1) * LANES]
            sc = lax.dot_general(kg, qs, (((1,), (1,)), ((), ())), preferred_element_type=F32)
            sc = sc + jnp.concatenate([bias_ref[g * nrel + bvar + kr] for kr in range(NA_KH)], axis=0)
            m = jnp.max(sc, axis=0, keepdims=True)
            e = jnp.exp2(sc - m)
            den = jnp.sum(e, axis=0, keepdims=True)
            p = (e * (1.0 / den)).astype(BF16)
            pv = lax.dot_general(p, vg, (((0,), (0,)), ((), ())), preferred_element_type=F32)
            o = jnp.where(lo_lanes, pv[0:GRID_W], pv[GRID_W:])
            o_ref[pl.ds(qrow, GRID_W), g * LANES:(g + 1) * LANES] = o.astype(o_ref.dtype)
        return carry

    lax.fori_loop(0, NA_TROWS, row_body, 0, unroll=True)


def _na_attn(q, k, v, bias_tab, tile_r0, tile_nr):
    t = q.shape[0]
    halo = NA_KH // 2
    tq = NA_TROWS * GRID_W
    hrows = halo * GRID_W
    per = tq // hrows
    nhb = t // hrows
    nt = t // tq
    qmap = lambda j, a, b: (j, 0)
    pmap = lambda j, a, b: (jnp.maximum(j * per - 1, 0), 0)
    nmap = lambda j, a, b: (jnp.minimum((j + 1) * per, nhb - 1), 0)
    gs = pltpu.PrefetchScalarGridSpec(
        num_scalar_prefetch=2, grid=(nt,),
        in_specs=[pl.BlockSpec((tq, C_W), qmap),
                  pl.BlockSpec((hrows, C_W), pmap), pl.BlockSpec((tq, C_W), qmap), pl.BlockSpec((hrows, C_W), nmap),
                  pl.BlockSpec((hrows, C_W), pmap), pl.BlockSpec((tq, C_W), qmap), pl.BlockSpec((hrows, C_W), nmap),
                  pl.BlockSpec(bias_tab.shape, lambda j, a, b: (0, 0, 0))],
        out_specs=pl.BlockSpec((tq, C_W), qmap),
        scratch_shapes=[pltpu.VMEM((tq + 2 * hrows, C_W), BF16), pltpu.VMEM((tq + 2 * hrows, C_W), BF16)])
    return pl.pallas_call(
        functools.partial(_na_kernel, halo=halo),
        grid_spec=gs,
        out_shape=jax.ShapeDtypeStruct((t, C_W), BF16),
        compiler_params=_cparams(("arbitrary",)),
        name="na_attn",
    )(tile_r0, tile_nr, q, k, k, k, v, v, v, bias_tab)


def _na_bias_table(rpb):
    c = np.arange(GRID_W)
    cs = np.clip(c - NA_KW // 2, 0, GRID_W - NA_KW)
    kc = np.arange(GRID_W)
    valid = (kc[:, None] >= cs[None, :]) & (kc[:, None] < cs[None, :] + NA_KW)
    span = GRID_W - NA_KW
    nrel = 2 * NA_KH - 1
    rpb2 = rpb.astype(F32).reshape(C_HEADS // 2, 2, nrel, 2 * NA_KW - 1) * LOG2E
    rev = jnp.flip(jnp.pad(rpb2, ((0, 0), (0, 0), (0, 0), (span, span)), mode="edge"), axis=-1)
    row = jnp.pad(rev, ((0, 0), (0, 0), (0, 0), (0, 1)))
    skew = jnp.broadcast_to(row[..., None, :], row.shape[:-1] + (GRID_W, 2 * GRID_W))
    skew = skew.reshape(row.shape[:-1] + (2 * GRID_W * GRID_W,))[..., :GRID_W * (2 * GRID_W - 1)]
    tab = skew.reshape(row.shape[:-1] + (GRID_W, 2 * GRID_W - 1))[..., GRID_W - 1:]
    tab = jnp.where(jnp.asarray(valid), tab, NEG_INF)
    pairs = jnp.concatenate([tab[:, 0], tab[:, 1]], axis=-1)
    return pairs.reshape((C_HEADS // 2) * (2 * NA_KH - 1), GRID_W, LANES)


def _route(h2, wr_ref, br_ref, idx_ref, gate_ref):
    tm = h2.shape[0]
    h_hi = h2.astype(BF16)
    h_lo = (h2 - h_hi.astype(F32)).astype(BF16)
    w = wr_ref[...]
    w_hi = w.astype(BF16)
    w_lo = (w - w_hi.astype(F32)).astype(BF16)
    nt = (((1,), (1,)), ((), ()))
    logits = (lax.dot_general(w_hi, h_hi, nt, preferred_element_type=F32)
              + lax.dot_general(w_lo, h_hi, nt, preferred_element_type=F32)
              + lax.dot_general(w_hi, h_lo, nt, preferred_element_type=F32)) + br_ref[:, 0:1]
    eid = lax.broadcasted_iota(jnp.int32, (N_EXPERTS, tm), 0)
    vals, idxs = [], []
    for _ in range(TOP_K):
        m = jnp.max(logits, axis=0, keepdims=True)
        ix = jnp.min(jnp.where(logits == m, eid, N_EXPERTS), axis=0, keepdims=True)
        vals.append(m)
        idxs.append(ix)
        logits = jnp.where(eid == ix, -jnp.inf, logits)
    es = [jnp.exp(vk - vals[0]) for vk in vals]
    den = es[0] + es[1] + es[2] + es[3]
    pad_i = jnp.zeros((8 - TOP_K, tm), jnp.int32)
    pad_f = jnp.zeros((8 - TOP_K, tm), F32)
    idx_ref[...] = jnp.concatenate(idxs + [pad_i], axis=0)
    gate_ref[...] = jnp.concatenate([ek / den for ek in es] + [pad_f], axis=0)


PACK_W = D_MODEL // 2


def _pack_bf16_rows(h):
    bits = pltpu.bitcast(h.astype(BF16).astype(F32), jnp.int32)
    return (bits[:, :PACK_W] & jnp.int32(-65536)) | lax.shift_right_logical(bits[:, PACK_W:], 16)


def _unpack_rows_f32(w):
    hi = pltpu.bitcast(w & jnp.int32(-65536), F32)
    lo = pltpu.bitcast(lax.shift_left(w, 16), F32)
    return jnp.concatenate([hi, lo], axis=1)


def _unpack_bf16_rows(w):
    return _unpack_rows_f32(w).astype(BF16)


def _out0_kernel(sid_ref, *refs, tm, part_starts):
    del sid_ref
    npart = len(part_starts)
    x_refs = refs[:npart]
    (mod_ref, n2_ref, oa_ref, ob0_ref, ls0_ref, ob1_ref, ls1_ref, ob2_ref, ls2_ref,
     wout_ref, wr_ref, br_ref, x1_ref, h2_ref, idx_ref, gate_ref, oscr, lscr) = refs[npart:]
    os_, ls_ = [], []
    for gi, (o_ref, l_ref) in enumerate(((ob0_ref, ls0_ref), (ob1_ref, ls1_ref), (ob2_ref, ls2_ref))):
        dil = B_PAIRS[gi][1]
        if dil == 1:
            os_.append(o_ref[...])
            ls_.append(l_ref[...])
        else:
            ncg = B_GW // LANES
            for r in range(dil):
                for c in range(ncg):
                    lo = r * B_GW + c * LANES
                    oscr[gi * ncg + c, pl.ds(r, tm // dil, stride=dil), :] = o_ref[:, lo:lo + LANES]
                lscr[gi, pl.ds(r, tm // dil, stride=dil), :] = l_ref[:, r * LANES:(r + 1) * LANES]
            os_.append(jnp.concatenate([oscr[gi * ncg + c] for c in range(ncg)], axis=1))
            ls_.append(lscr[gi])
    lmax = jnp.maximum(jnp.maximum(ls_[0], ls_[1]), ls_[2])
    ws = [jnp.exp(l - lmax) for l in ls_]
    winv = 1.0 / (ws[0] + ws[1] + ws[2])
    lane = lax.broadcasted_iota(jnp.int32, (tm, LANES), 1)
    lo_lanes = lane < HEAD_DIM

    def head_weights(w):
        return jnp.concatenate(
            [jnp.where(lo_lanes, jnp.broadcast_to(w[:, 2 * pr:2 * pr + 1], (tm, LANES)),
                       jnp.broadcast_to(w[:, 2 * pr + 1:2 * pr + 2], (tm, LANES)))
             for pr in range(B_HEADS // 2)], axis=1)

    ob = (head_weights(ws[0] * winv) * os_[0] + head_weights(ws[1] * winv) * os_[1]
          + head_weights(ws[2] * winv) * os_[2])
    o = (jnp.dot(oa_ref[...], wout_ref[0:A_QW, :], preferred_element_type=F32)
         + jnp.dot(ob.astype(BF16), wout_ref[A_QW:, :], preferred_element_type=F32))
    x1 = _read_part(x_refs, part_starts) + mod_ref[2:3, :] * o
    x1_ref[...] = x1
    h2 = _rms_mod(x1, n2_ref, mod_ref, 3, 4)
    h2_ref[...] = _pack_bf16_rows(h2)
    _route(h2, wr_ref, br_ref, idx_ref, gate_ref)


def _out1_kernel(sid_ref, *refs, part_starts):
    del sid_ref
    npart = len(part_starts)
    x_refs = refs[:npart]
    mod_ref, n2_ref, oc_ref, wout_ref, wr_ref, br_ref, x1_ref, h2_ref, idx_ref, gate_ref = refs[npart:]
    o = jnp.dot(oc_ref[...], wout_ref[...], preferred_element_type=F32)
    x1 = _read_part(x_refs, part_starts) + mod_ref[2:3, :] * o
    x1_ref[...] = x1
    h2 = _rms_mod(x1, n2_ref, mod_ref, 3, 4)
    h2_ref[...] = _pack_bf16_rows(h2)
    _route(h2, wr_ref, br_ref, idx_ref, gate_ref)


def _out_proj(layer, x_parts, mod, tile_sid, n2, attn, wout_bf, wr_t, br):
    tm = TOKEN_TILE
    part_starts, ntiles = _part_tiles(x_parts, tm)
    t = ntiles * tm
    row = lambda i, sid: (i, 0)
    const = lambda i, sid: (0, 0)
    in_specs = _part_specs(x_parts, part_starts, tm, D_MODEL) + [
        pl.BlockSpec((None, 6, D_MODEL), lambda i, sid: (sid[i], 0, 0)),
        pl.BlockSpec((1, D_MODEL), const)]
    scratch = []
    if layer == 0:
        in_specs.append(pl.BlockSpec((tm, A_QW), row))
        for (_, dil) in B_PAIRS:
            in_specs += [pl.BlockSpec((tm // dil, dil * B_GW), row), pl.BlockSpec((tm // dil, dil * LANES), row)]
        body = functools.partial(_out0_kernel, tm=tm, part_starts=part_starts)
        nscr = B_GROUPS * B_GW // LANES
        scratch = [pltpu.VMEM((nscr, tm, LANES), F32), pltpu.VMEM((B_GROUPS, tm, LANES), F32)]
    else:
        in_specs.append(pl.BlockSpec((tm, C_W), row))
        body = functools.partial(_out1_kernel, part_starts=part_starts)
    in_specs += [pl.BlockSpec(wout_bf.shape, const),
                 pl.BlockSpec((N_EXPERTS, D_MODEL), const),
                 pl.BlockSpec((N_EXPERTS, LANES), const)]
    gs = pltpu.PrefetchScalarGridSpec(
        num_scalar_prefetch=1, grid=(t // tm,), in_specs=in_specs,
        out_specs=[pl.BlockSpec((tm, D_MODEL), row), pl.BlockSpec((tm, PACK_W), row),
                   pl.BlockSpec((8, tm), lambda i, sid: (0, i)), pl.BlockSpec((8, tm), lambda i, sid: (0, i))],
        scratch_shapes=scratch)
    return pl.pallas_call(
        body, grid_spec=gs,
        out_shape=[jax.ShapeDtypeStruct((t, D_MODEL), F32), jax.ShapeDtypeStruct((t, PACK_W), jnp.int32),
                   jax.ShapeDtypeStruct((8, t), jnp.int32), jax.ShapeDtypeStruct((8, t), F32)],
        compiler_params=_cparams(("arbitrary",)),
        name="out_proj%d" % layer,
    )(tile_sid, *x_parts, mod, n2.reshape(1, D_MODEL), *attn, wout_bf, wr_t, br)


def _rank_kernel(idx_ref, tri_ref, rank_ref, cnt_ref, carry):
    i = pl.program_id(0)

    @pl.when(i == 0)
    def _():
        carry[...] = jnp.zeros_like(carry)

    tk = idx_ref.shape[1]
    eid = lax.broadcasted_iota(jnp.int32, (N_EXPERTS, tk), 0)
    hits = [eid == idx_ref[k:k + 1, :] for k in range(TOP_K)]
    onehot = sum(h.astype(F32) for h in hits)
    incl = jnp.dot(onehot.astype(BF16), tri_ref[...], preferred_element_type=F32)
    before = incl - onehot + carry[:, 0:1]
    rows = [jnp.sum(jnp.where(h, before, 0.0), axis=0, keepdims=True) for h in hits]
    rows.append(jnp.zeros((8 - TOP_K, tk), F32))
    rank_ref[...] = jnp.concatenate(rows, axis=0).astype(jnp.int32)
    carry[...] = carry[...] + incl[:, tk - 1:tk]
    cnt_ref[...] = carry[...].astype(jnp.int32)


def _slot_kernel(idx_ref, rank_ref, start_ref, slot_ref):
    tk = idx_ref.shape[1]
    eid = lax.broadcasted_iota(jnp.int32, (N_EXPERTS, tk), 0)
    start = start_ref[:, 0:1]
    rows = []
    for k in range(TOP_K):
        base = jnp.sum(jnp.where(eid == idx_ref[k:k + 1, :], start, 0), axis=0, keepdims=True)
        rows.append(base + rank_ref[k:k + 1, :])
    rows.append(jnp.zeros((8 - TOP_K, tk), jnp.int32))
    slot_ref[...] = jnp.concatenate(rows, axis=0)


def _route_slots(idx):
    t = idx.shape[1]
    tk = ROUTE_TILE
    tri = jnp.asarray(np.triu(np.ones((tk, tk), np.float32)), BF16)
    tok = lambda i: (0, i)
    rank, cnt = pl.pallas_call(
        _rank_kernel, grid=(t // tk,),
        in_specs=[pl.BlockSpec((8, tk), tok), pl.BlockSpec((tk, tk), lambda i: (0, 0))],
        out_specs=[pl.BlockSpec((8, tk), tok), pl.BlockSpec((N_EXPERTS, LANES), lambda i: (0, 0))],
        out_shape=[jax.ShapeDtypeStruct((8, t), jnp.int32), jax.ShapeDtypeStruct((N_EXPERTS, LANES), jnp.int32)],
        scratch_shapes=[pltpu.VMEM((N_EXPERTS, LANES), F32)],
        compiler_params=_cparams(("arbitrary",)),
        name="route_rank",
    )(idx, tri)
    counts = cnt[:, 0]
    padded = (counts + MOE_BM - 1) // MOE_BM * MOE_BM
    pad_end = jnp.cumsum(padded)
    pad_start = pad_end - padded
    nblk = (t * TOP_K) // MOE_BM + N_EXPERTS
    blk_row0 = jnp.arange(nblk, dtype=jnp.int32) * MOE_BM
    blk_expert = jnp.minimum(jnp.sum((pad_end[None, :] <= blk_row0[:, None]).astype(jnp.int32), axis=1),
                             N_EXPERTS - 1)
    nused = (pad_end[-1] // MOE_BM).astype(jnp.int32).reshape(1)
    eids = jnp.arange(N_EXPERTS, dtype=jnp.int32)
    later = (eids[None, :] > eids[:, None]) & (padded[None, :] > 0)
    next_expert = jnp.min(jnp.where(later, eids[None, :], N_EXPERTS), axis=1)
    next_expert = jnp.where(next_expert == N_EXPERTS, -1, next_expert).astype(jnp.int32)
    blk_next = jnp.sum(jnp.where(blk_expert[:, None] == eids[None, :], next_expert[None, :], 0), axis=1)
    start_b = jnp.broadcast_to(pad_start.astype(jnp.int32)[:, None], (N_EXPERTS, LANES))
    slot = pl.pallas_call(
        _slot_kernel, grid=(t // tk,),
        in_specs=[pl.BlockSpec((8, tk), tok), pl.BlockSpec((8, tk), tok),
                  pl.BlockSpec((N_EXPERTS, LANES), lambda i: (0, 0))],
        out_specs=pl.BlockSpec((8, tk), tok),
        out_shape=jax.ShapeDtypeStruct((8, t), jnp.int32),
        compiler_params=_cparams(("arbitrary",)),
        name="route_slot",
    )(idx, rank, start_b)
    return slot, blk_expert, nused, blk_next


SC_CORES = 2
SC_SUBCORES = 16
SC_ROW_BUFFER_BYTES = 256 * 1024
SC_MAX_INDEX_LIST = 128


def _sc_chunks(t, row_bytes):
    chunk = min(SC_MAX_INDEX_LIST, SC_ROW_BUFFER_BYTES // row_bytes)
    nchunk = t // chunk
    per = nchunk // (SC_CORES * SC_SUBCORES)
    assert per * SC_CORES * SC_SUBCORES * chunk == t
    return chunk, nchunk, per


def _sc_slot_lists(slot, nchunk, chunk):
    return slot[:TOP_K].reshape(TOP_K, nchunk, chunk).transpose(1, 0, 2)


def _sc_dispatch(h2, slot, nslots):
    t, w = h2.shape
    chunk, nchunk, per = _sc_chunks(t, w * h2.dtype.itemsize)
    mesh = plsc.VectorSubcoreMesh(core_axis_name="c", subcore_axis_name="s")

    @functools.partial(
        pl.kernel, mesh=mesh, out_type=jax.ShapeDtypeStruct((nslots, w), h2.dtype),
        scratch_types=[pltpu.VMEM((TOP_K, chunk), jnp.int32), pltpu.VMEM((chunk, w), h2.dtype)],
        name="moe_dispatch_sc")
    def body(h_hbm, slot_hbm, xs_hbm, idx_v, rows_v):
        wid = lax.axis_index("s") * SC_CORES + lax.axis_index("c")

        @pl.loop(0, per)
        def _(j):
            c = wid * per + j
            pltpu.sync_copy(slot_hbm.at[c], idx_v)
            pltpu.sync_copy(h_hbm.at[pl.ds(c * chunk, chunk)], rows_v)
            for k in range(TOP_K):
                pltpu.sync_copy(rows_v, xs_hbm.at[idx_v.at[k]])

    return body(h2, _sc_slot_lists(slot, nchunk, chunk))


def _sc_gather(ys, slot, t):
    w = ys.shape[1]
    nbuf = 2
    chunk, nchunk, per = _sc_chunks(t, nbuf * w * ys.dtype.itemsize)
    mesh = plsc.VectorSubcoreMesh(core_axis_name="c", subcore_axis_name="s")

    @functools.partial(
        pl.kernel, mesh=mesh, out_type=jax.ShapeDtypeStruct((TOP_K, t, w), ys.dtype),
        scratch_types=[pltpu.VMEM((TOP_K, chunk), jnp.int32), pltpu.VMEM((nbuf, chunk, w), ys.dtype),
                       pltpu.SemaphoreType.DMA((nbuf,))],
        name="moe_gather_sc")
    def body(ys_hbm, slot_hbm, out_hbm, idx_v, rows_v, wsem):
        wid = lax.axis_index("s") * SC_CORES + lax.axis_index("c")

        @pl.loop(0, per)
        def _(j):
            c = wid * per + j
            pltpu.sync_copy(slot_hbm.at[c], idx_v)
            writes = []
            for k in range(TOP_K):
                b = k % nbuf
                if k >= nbuf:
                    writes[k - nbuf].wait()
                pltpu.sync_copy(ys_hbm.at[idx_v.at[k]], rows_v.at[b])
                writes.append(pltpu.async_copy(rows_v.at[b], out_hbm.at[k, pl.ds(c * chunk, chunk)], wsem.at[b]))
            for wr in writes[-nbuf:]:
                wr.wait()

    return body(ys, _sc_slot_lists(slot, nchunk, chunk))


def _ffn_kernel(be_ref, nu_ref, nxt_ref, xs_ref, wgu_hbm, bgu_ref, wd_hbm, bd_ref, y_ref,
                wgu_f, wd_f, wgu_bf, wd_bf, wsem):
    i = pl.program_id(0)
    prev = be_ref[jnp.maximum(i - 1, 0)]

    def weight_copies(e):
        return (pltpu.make_async_copy(wgu_hbm.at[e], wgu_f, wsem.at[0]),
                pltpu.make_async_copy(wd_hbm.at[e], wd_f, wsem.at[1]))

    @pl.when(i == 0)
    def _():
        for cp in weight_copies(be_ref[0]):
            cp.start()

    @pl.when(i < nu_ref[0])
    def _():
        @pl.when((i == 0) | (be_ref[i] != prev))
        def _():
            for cp in weight_copies(be_ref[i]):
                cp.wait()
            wgu_bf[...] = wgu_f[...].astype(BF16)
            wd_bf[...] = wd_f[...].astype(BF16)

            @pl.when(nxt_ref[i] >= 0)
            def _():
                for cp in weight_copies(nxt_ref[i]):
                    cp.start()

        x = _unpack_bf16_rows(xs_ref[...])
        gu = jnp.dot(x, wgu_bf[...], preferred_element_type=F32) + bgu_ref[...]
        gate = jnp.minimum(gu[:, :D_FF], SWIGLU_LIMIT)
        up = jnp.clip(gu[:, D_FF:], -SWIGLU_LIMIT, SWIGLU_LIMIT)
        act = (up + 1.0) * gate * jax.nn.sigmoid(SWIGLU_ALPHA * gate)
        y = jnp.dot(act.astype(BF16), wd_bf[...], preferred_element_type=F32) + bd_ref[...]
        y_ref[...] = _pack_bf16_rows(y)

    @pl.when(i >= nu_ref[0])
    def _():
        y_ref[...] = jnp.zeros_like(y_ref)


def _expert_ffn(xs, blk_expert, nused, blk_next, wgu, bgu, wd, bd):
    nslots = xs.shape[0]
    nblk = nslots // MOE_BM
    blk = lambda i, be, nu, nx: (jnp.minimum(i, nu[0] - 1), 0)
    exp3 = lambda i, be, nu, nx: (be[jnp.minimum(i, nu[0] - 1)], 0, 0)
    gs = pltpu.PrefetchScalarGridSpec(
        num_scalar_prefetch=3, grid=(nblk,),
        in_specs=[pl.BlockSpec((MOE_BM, PACK_W), blk),
                  pl.BlockSpec(memory_space=pl.ANY),
                  pl.BlockSpec((None, 1, 2 * D_FF), exp3),
                  pl.BlockSpec(memory_space=pl.ANY),
                  pl.BlockSpec((None, 1, D_MODEL), exp3)],
        out_specs=pl.BlockSpec((MOE_BM, PACK_W), lambda i, be, nu, nx: (i, 0)),
        scratch_shapes=[pltpu.VMEM((D_MODEL, 2 * D_FF), F32), pltpu.VMEM((D_FF, D_MODEL), F32),
                        pltpu.VMEM((D_MODEL, 2 * D_FF), BF16), pltpu.VMEM((D_FF, D_MODEL), BF16),
                        pltpu.SemaphoreType.DMA((2,))])
    return pl.pallas_call(
        _ffn_kernel, grid_spec=gs,
        out_shape=jax.ShapeDtypeStruct((nslots, PACK_W), jnp.int32),
        compiler_params=_cparams(("arbitrary",)),
        name="moe_ffn",
    )(blk_expert, nused, blk_next, xs, wgu, bgu.reshape(N_EXPERTS, 1, 2 * D_FF), wd,
      bd.reshape(N_EXPERTS, 1, D_MODEL))


def _combine_kernel(sid_ref, x_ref, mod_ref, gate_ref, yk_ref, *o_refs, tm, out_starts):
    del sid_ref
    gpad = jnp.concatenate([gate_ref[...], jnp.zeros((LANES - 8, tm), F32)], axis=0)
    gcol = gpad.T
    acc = gcol[:, 0:1] * _unpack_rows_f32(yk_ref[0])
    for k in range(1, TOP_K):
        acc = acc + gcol[:, k:k + 1] * _unpack_rows_f32(yk_ref[k])
    out = x_ref[...] + mod_ref[5:6, :] * acc
    i = pl.program_id(0)
    ends = out_starts[1:] + (None,)
    for o_ref, s0, s1 in zip(o_refs, out_starts, ends):
        mine = (i >= s0) if s1 is None else ((i >= s0) & (i < s1))

        @pl.when(mine)
        def _():
            o_ref[...] = out


def _combine(x1, mod, tile_sid, gates, yk, out_rows):
    t = x1.shape[0]
    tm = MOVE_TILE
    out_shape = [jax.ShapeDtypeStruct((n, D_MODEL), F32) for n in out_rows]
    out_starts, ntiles = _part_tiles(out_shape, tm)
    assert ntiles * tm == t
    gs = pltpu.PrefetchScalarGridSpec(
        num_scalar_prefetch=1, grid=(t // tm,),
        in_specs=[pl.BlockSpec((tm, D_MODEL), lambda i, sid: (i, 0)),
                  pl.BlockSpec((None, 6, D_MODEL), lambda i, sid: (sid[i], 0, 0)),
                  pl.BlockSpec((8, tm), lambda i, sid: (0, i)),
                  pl.BlockSpec((TOP_K, tm, PACK_W), lambda i, sid: (0, i, 0))],
        out_specs=_part_specs(out_shape, out_starts, tm, D_MODEL))
    return pl.pallas_call(
        functools.partial(_combine_kernel, tm=tm, out_starts=out_starts), grid_spec=gs,
        out_shape=out_shape,
        compiler_params=_cparams(("arbitrary",)),
        name="moe_combine",
    )(tile_sid, x1, mod, gates, yk)


def _moe(x1, h2, idx, gates, mod, tile_sid_move, wgu, bgu, wd, bd, out_rows):
    t = x1.shape[0]
    nslots = t * TOP_K + N_EXPERTS * MOE_BM
    slot, blk_expert, nused, blk_next = _route_slots(idx)
    xs = _sc_dispatch(h2, slot, nslots)
    ys = _expert_ffn(xs, blk_expert, nused, blk_next, wgu, bgu, wd, bd)
    return _combine(x1, mod, tile_sid_move, gates, _sc_gather(ys, slot, t), out_rows)


def _tile_table(seq_lens, tile, fn):
    vals = []
    start = 0
    for sid, n in enumerate(seq_lens):
        assert n % tile == 0
        for j in range(n // tile):
            vals.append(fn(sid, start, n, j))
        start += n
    return jnp.asarray(np.asarray(vals, np.int32))


def _head_block_diag():
    bd = np.kron(np.eye(2 * LANES // HEAD_DIM, dtype=np.float32),
                 np.full((HEAD_DIM, HEAD_DIM), 1.0 / HEAD_DIM, np.float32))
    return jnp.asarray(bd, BF16)


_A_HEAD_ORDER = [kv * A_GROUP + g for g in range(A_GROUP) for kv in range(A_KV_HEADS)]


def _trunk(x_parts, mods, mod_row0, seq_lens, p):
    t = sum(a.shape[0] for a in x_parts)
    sid_tok = _tile_table(seq_lens, TOKEN_TILE, lambda sid, s, n, j: mod_row0 + sid)
    sid_move = _tile_table(seq_lens, MOVE_TILE, lambda sid, s, n, j: mod_row0 + sid)
    bd = _head_block_diag()
    mod0, mod1 = mods

    w_in = p["l0_w_in"]
    a_cols = np.concatenate([np.arange(h * HEAD_DIM, (h + 1) * HEAD_DIM) for h in _A_HEAD_ORDER])
    bq, bk, bv = A_IN, A_IN + B_GROUPS * B_GW, A_IN + 2 * B_GROUPS * B_GW
    col_blocks = [w_in[:, a_cols], w_in[:, A_QW:A_IN]]
    for g in range(B_GROUPS):
        col_blocks += [w_in[:, bq + g * B_GW:bq + (g + 1) * B_GW],
                       w_in[:, bk + g * B_GW:bk + (g + 1) * B_GW],
                       w_in[:, bv + g * B_GW:bv + (g + 1) * B_GW]]
    w0 = jnp.concatenate(col_blocks, axis=1).astype(BF16)
    ones = lambda n: jnp.ones((n,), F32)
    gain_blocks = [jnp.tile(p["l0_q_norm_a"], A_Q_HEADS) * QK_SCALE, jnp.tile(p["l0_k_norm_a"], A_KV_HEADS),
                   ones(A_KVW)]
    for g in range(B_GROUPS):
        gain_blocks += [jnp.tile(p["l0_q_norm_b"], B_HEADS) * QK_SCALE, jnp.tile(p["l0_k_norm_b"], B_HEADS),
                        ones(B_GW)]
    gain0 = jnp.concatenate(gain_blocks).reshape(1, -1)
    slabs0 = [(0, A_QW, True, 1), (A_QW, A_KVW, True, 1), (A_QW + A_KVW, A_KVW, False, 1)]
    c0 = A_IN
    for g, (_, dil) in enumerate(B_PAIRS):
        slabs0 += [(c0, B_GW, True, dil), (c0 + B_GW, B_GW, True, dil), (c0 + 2 * B_GW, B_GW, False, dil)]
        c0 += 3 * B_GW
    proj = _in_proj(x_parts, mod0, sid_tok, p["l0_norm1"], w0, gain0, bd, tuple(slabs0))
    qa, ka, va = proj[0:3]

    slopes_a = _alibi_slopes(A_Q_HEADS)
    heads_a = tuple(tuple((slopes_a[kv * A_GROUP + g], kv * A_GROUP + g) for kv in range(A_KV_HEADS))
                    for g in range(A_GROUP))
    (oa,) = _band_attn(qa, ka, va, seq_lens, tq=ATTN_TQ_A, dil=1, qw=A_QW, kvw=A_KVW, half=A_HALF, heads=heads_a,
                       kv_group=(0,) * A_GROUP, sink=p["l0_sink_a"].astype(F32), out_dtype=BF16, want_lse=False)

    slopes_b = _alibi_slopes(B_GROUPS * B_HEADS)
    attn0 = [oa]
    for g, (window, dil) in enumerate(B_PAIRS):
        qg, kg, vg = proj[3 + 3 * g:6 + 3 * g]
        heads_b = tuple(tuple((slopes_b[g * B_HEADS + 2 * pr + s] * dil, 0) for s in range(2))
                        for pr in range(B_HEADS // 2))
        tq_b = ATTN_TQ_B * BAND_RESIDUES_PER_STEP // min(dil, BAND_RESIDUES_PER_STEP)
        og, lg = _band_attn(qg, kg, vg, [n // dil for n in seq_lens], tq=tq_b, dil=dil, qw=B_GW, kvw=B_GW, half=window // (2 * dil),
                            heads=heads_b, kv_group=tuple(range(B_HEADS // 2)), sink=None, out_dtype=F32,
                            want_lse=True)
        attn0 += [og, lg]

    w_out = p["l0_w_out"]
    wout0 = jnp.concatenate([w_out[a_cols], w_out[A_QW:]], axis=0).astype(BF16)
    br0 = jnp.broadcast_to(p["l0_b_router"].astype(F32)[:, None], (N_EXPERTS, LANES))
    x1, h2, idx, gates = _out_proj(0, x_parts, mod0, sid_tok, p["l0_norm2"], attn0, wout0, p["l0_w_router"].T,
                                   br0)
    x = _moe(x1, h2, idx, gates, mod0, sid_move, p["l0_w_gate_up"], p["l0_b_gate_up"], p["l0_w_down"],
             p["l0_b_down"], [t])

    w1 = p["l1_w_in"].astype(BF16)
    gain1 = jnp.concatenate([jnp.tile(p["l1_q_norm_c"], C_HEADS) * QK_SCALE, jnp.tile(p["l1_k_norm_c"], C_HEADS),
                             ones(C_W)]).reshape(1, -1)
    slabs1 = ((0, C_W, True, 1), (C_W, C_W, True, 1), (2 * C_W, C_W, False, 1))
    qc, kc, vc = _in_proj(x, mod1, sid_tok, p["l1_norm1"], w1, gain1, bd, slabs1)
    na_tile = NA_TROWS * GRID_W
    tile_r0 = _tile_table(seq_lens, na_tile, lambda sid, s, n, j: j * NA_TROWS)
    tile_nr = _tile_table(seq_lens, na_tile, lambda sid, s, n, j: n // GRID_W)
    oc = _na_attn(qc, kc, vc, _na_bias_table(p["l1_rpb_c"]), tile_r0, tile_nr)
    br1 = jnp.broadcast_to(p["l1_b_router"].astype(F32)[:, None], (N_EXPERTS, LANES))
    x1, h2, idx, gates = _out_proj(1, x, mod1, sid_tok, p["l1_norm2"], [oc], p["l1_w_out"].astype(BF16),
                                   p["l1_w_router"].T, br1)
    return _moe(x1, h2, idx, gates, mod1, sid_move, p["l1_w_gate_up"], p["l1_b_gate_up"], p["l1_w_down"],
                p["l1_b_down"], [a.shape[0] for a in x_parts])


def kernel(x_prompt, x_sample, c_prompt, c_sample, l0_ada_w, l0_ada_b, l0_norm1, l0_w_in, l0_q_norm_a, l0_k_norm_a, l0_sink_a, l0_q_norm_b, l0_k_norm_b, l0_w_out, l0_norm2, l0_w_router, l0_b_router, l0_w_gate_up, l0_b_gate_up, l0_w_down, l0_b_down, l1_ada_w, l1_ada_b, l1_norm1, l1_w_in, l1_q_norm_c, l1_k_norm_c, l1_rpb_c, l1_w_out, l1_norm2, l1_w_router, l1_b_router, l1_w_gate_up, l1_b_gate_up, l1_w_down, l1_b_down):
    p = dict(l0_ada_w=l0_ada_w, l0_ada_b=l0_ada_b, l0_norm1=l0_norm1, l0_w_in=l0_w_in, l0_q_norm_a=l0_q_norm_a,
             l0_k_norm_a=l0_k_norm_a, l0_sink_a=l0_sink_a, l0_q_norm_b=l0_q_norm_b, l0_k_norm_b=l0_k_norm_b,
             l0_w_out=l0_w_out, l0_norm2=l0_norm2, l0_w_router=l0_w_router, l0_b_router=l0_b_router,
             l0_w_gate_up=l0_w_gate_up, l0_b_gate_up=l0_b_gate_up, l0_w_down=l0_w_down, l0_b_down=l0_b_down,
             l1_ada_w=l1_ada_w, l1_ada_b=l1_ada_b, l1_norm1=l1_norm1, l1_w_in=l1_w_in, l1_q_norm_c=l1_q_norm_c,
             l1_k_norm_c=l1_k_norm_c, l1_rpb_c=l1_rpb_c, l1_w_out=l1_w_out, l1_norm2=l1_norm2,
             l1_w_router=l1_w_router, l1_b_router=l1_b_router, l1_w_gate_up=l1_w_gate_up,
             l1_b_gate_up=l1_b_gate_up, l1_w_down=l1_w_down, l1_b_down=l1_b_down)
    nb_p, len_p, d = x_prompt.shape
    nb_s, len_s, _ = x_sample.shape
    nseq = nb_p + nb_s
    c_pad = jnp.concatenate([c_prompt, c_sample, jnp.zeros((-nseq % 8, d), F32)], axis=0)
    mods = tuple(_ada_mod(c_pad, p[l + "_ada_w"], p[l + "_ada_b"]).reshape(c_pad.shape[0], 6, D_MODEL)
                 for l in ("l0", "l1"))
    outs = []
    for xg, row0 in ((x_prompt, 0), (x_sample, nb_p)):
        nb, ln, _ = xg.shape
        (y,) = _trunk([xg.reshape(nb * ln, d)], mods, row0, (ln,) * nb, p)
        outs.append(y.reshape(nb, ln, d))
    return tuple(outs)
```

```python
import functools
import math

import numpy as np
import jax
import jax.numpy as jnp
from jax import lax
from jax.experimental import pallas as pl
from jax.experimental.pallas import tpu as pltpu
from jax.experimental.pallas import tpu_sc as plsc

F32 = jnp.float32
BF16 = jnp.bfloat16
HIGHEST = lax.Precision.HIGHEST

D_MODEL = 1024
HEAD_DIM = 64
LANES = 128
GRID_W = 64
A_Q_HEADS = 8
A_KV_HEADS = 2
A_GROUP = A_Q_HEADS // A_KV_HEADS
A_HALF = 128
B_PAIRS = ((128, 1), (512, 4), (2048, 16))
B_GROUPS = 3
B_HEADS = 4
A_QW = A_Q_HEADS * HEAD_DIM
A_KVW = A_KV_HEADS * HEAD_DIM
A_IN = A_QW + 2 * A_KVW
B_GW = B_HEADS * HEAD_DIM
C_HEADS = 16
C_W = C_HEADS * HEAD_DIM
NA_KH = 8
NA_KW = 16
N_EXPERTS = 32
TOP_K = 4
D_FF = 1024
SWIGLU_LIMIT = 7.0
SWIGLU_ALPHA = 1.702
RMS_EPS = 1e-6
NEG_INF = -1e30
LOG2E = math.log2(math.e)
LN2 = math.log(2.0)
QK_SCALE = HEAD_DIM ** -0.5 * LOG2E

TOKEN_TILE = 1024
ATTN_TQ_A = 1024
ATTN_TQ_B = 512
ATTN_UQ = 128
BAND_RESIDUES_PER_STEP = 4
NA_TROWS = 16
MOE_BM = 512
ROUTE_TILE = 1024
MOVE_TILE = 1024
VMEM_LIMIT = 56 * 1024 * 1024


def _alibi_slopes(n):
    return [float(2.0 ** (-8.0 * (j + 1) / n)) for j in range(n)]


def _cparams(sem):
    return pltpu.CompilerParams(dimension_semantics=sem, vmem_limit_bytes=VMEM_LIMIT)


def _mod_kernel(c_ref, w_ref, b_ref, o_ref):
    c = c_ref[...]
    s = c * jax.nn.sigmoid(c)
    o_ref[...] = jnp.dot(s, w_ref[...], precision=HIGHEST, preferred_element_type=F32) + b_ref[...]


def _ada_mod(c_pad, w, b):
    nrow = c_pad.shape[0]
    ncol = w.shape[1]
    tn = D_MODEL
    return pl.pallas_call(
        _mod_kernel,
        grid=(ncol // tn,),
        in_specs=[pl.BlockSpec((nrow, D_MODEL), lambda j: (0, 0)),
                  pl.BlockSpec((D_MODEL, tn), lambda j: (0, j)),
                  pl.BlockSpec((1, tn), lambda j: (0, j))],
        out_specs=pl.BlockSpec((nrow, tn), lambda j: (0, j)),
        out_shape=jax.ShapeDtypeStruct((nrow, ncol), F32),
        compiler_params=_cparams(("arbitrary",)),
        name="ada_mod",
    )(c_pad, w, b.reshape(1, ncol))


def _head_mean_sq(y, bd_ref):
    w = y.shape[1]
    outs = []
    for c0 in range(0, w, 2 * LANES):
        cw = min(2 * LANES, w - c0)
        sq = y[:, c0:c0 + cw] * y[:, c0:c0 + cw]
        outs.append(jnp.dot(sq.astype(BF16), bd_ref[0:cw, 0:cw], preferred_element_type=F32))
    return outs[0] if len(outs) == 1 else jnp.concatenate(outs, axis=1)


def _rms_mod(x, g_ref, mod_ref, shift_row, scale_row):
    ms = jnp.mean(x * x, axis=-1, keepdims=True)
    xn = x * lax.rsqrt(ms + RMS_EPS) * g_ref[...]
    return xn * (1.0 + mod_ref[scale_row:scale_row + 1, :]) + mod_ref[shift_row:shift_row + 1, :]


def _part_tiles(parts, tile):
    starts, s = [], 0
    for a in parts:
        assert a.shape[0] % tile == 0
        starts.append(s)
        s += a.shape[0] // tile
    return tuple(starts), s


def _part_specs(parts, starts, tile, ncol):
    def spec(a, s0):
        n = a.shape[0] // tile
        return pl.BlockSpec((tile, ncol), lambda i, *_: (jnp.clip(i - s0, 0, n - 1), 0))
    return [spec(a, s0) for a, s0 in zip(parts, starts)]


def _read_part(refs, starts):
    i = pl.program_id(0)
    x = refs[0][...]
    for ref, s0 in zip(refs[1:], starts[1:]):
        x = jnp.where(i >= s0, ref[...], x)
    return x


def _in_kernel(sid_ref, *refs, slabs, tm, part_starts):
    del sid_ref
    npart = len(part_starts)
    x_refs = refs[:npart]
    mod_ref, n1_ref, w_ref, gain_ref, bd_ref = refs[npart:npart + 5]
    rest = refs[npart + 5:]
    nout = len(slabs)
    out_refs = rest[:nout]
    scr_ref = rest[nout] if len(rest) > nout else None
    h = _rms_mod(_read_part(x_refs, part_starts), n1_ref, mod_ref, 0, 1).astype(BF16)
    proj = jnp.dot(h, w_ref[...], preferred_element_type=F32)
    for (c0, w, normed, dil), o_ref in zip(slabs, out_refs):
        y = proj[:, c0:c0 + w]
        if normed:
            y = y * lax.rsqrt(_head_mean_sq(y, bd_ref) + RMS_EPS) * gain_ref[:, c0:c0 + w]
        if dil == 1:
            o_ref[...] = y.astype(BF16)
        else:
            for c in range(w // LANES):
                scr_ref[c] = y[:, c * LANES:(c + 1) * LANES]
            for r in range(dil):
                for c in range(w // LANES):
                    o_ref[:, r * w + c * LANES:r * w + (c + 1) * LANES] = (
                        scr_ref[c, pl.ds(r, tm // dil, stride=dil), :].astype(BF16))


def _in_proj(x_parts, mod, tile_sid, n1, w_bf, gain, bd, slabs):
    tm = TOKEN_TILE
    part_starts, ntiles = _part_tiles(x_parts, tm)
    t = ntiles * tm
    ncols = w_bf.shape[1]
    out_shapes, out_specs = [], []
    for (c0, w, normed, dil) in slabs:
        out_shapes.append(jax.ShapeDtypeStruct((t // dil, dil * w), BF16))
        out_specs.append(pl.BlockSpec((tm // dil, dil * w), lambda i, sid: (i, 0)))
    need_scr = any(s[3] > 1 for s in slabs)
    scratch = [pltpu.VMEM((max(s[1] for s in slabs if s[3] > 1) // LANES, tm, LANES), F32)] if need_scr else []
    gs = pltpu.PrefetchScalarGridSpec(
        num_scalar_prefetch=1,
        grid=(t // tm,),
        in_specs=_part_specs(x_parts, part_starts, tm, D_MODEL) + [
            pl.BlockSpec((None, 6, D_MODEL), lambda i, sid: (sid[i], 0, 0)),
            pl.BlockSpec((1, D_MODEL), lambda i, sid: (0, 0)),
            pl.BlockSpec((D_MODEL, ncols), lambda i, sid: (0, 0)),
            pl.BlockSpec((1, ncols), lambda i, sid: (0, 0)),
            pl.BlockSpec((2 * LANES, 2 * LANES), lambda i, sid: (0, 0))],
        out_specs=out_specs,
        scratch_shapes=scratch)
    return pl.pallas_call(
        functools.partial(_in_kernel, slabs=slabs, tm=tm, part_starts=part_starts),
        grid_spec=gs,
        out_shape=out_shapes,
        compiler_params=_cparams(("arbitrary",)),
        name="in_proj",
    )(tile_sid, *x_parts, mod, n1.reshape(1, D_MODEL), w_bf, gain, bd)


def _band_kernel(s0_ref, s1_ref, *refs, tq, hb, half, heads, kv_group, kvw, rps, has_sink, want_lse):
    pos = 0
    sink_ref = None
    if has_sink:
        sink_ref = refs[0]
        pos = 1
    q_ref, kp_ref, kc_ref, kn_ref, vp_ref, vc_ref, vn_ref = refs[pos:pos + 7]
    o_ref = refs[pos + 7]
    lse_ref = refs[pos + 8] if want_lse else None
    i = pl.program_id(1)
    uq = ATTN_UQ
    wlen = uq + 2 * hb
    kall = jnp.concatenate([kp_ref[...], kc_ref[...], kn_ref[...]], axis=0)
    vall = jnp.concatenate([vp_ref[...], vc_ref[...], vn_ref[...]], axis=0)
    lane = lax.broadcasted_iota(jnp.int32, (uq, LANES), 1)
    lo_lanes = lane < HEAD_DIM
    for sub in range(tq // uq):
        rows = slice(sub * uq, (sub + 1) * uq)
        kwin = kall[sub * uq:sub * uq + wlen]
        vwin = vall[sub * uq:sub * uq + wlen]
        q0 = i * tq + sub * uq
        kpos = q0 - hb + lax.broadcasted_iota(jnp.int32, (wlen, uq), 0)
        qpos = q0 + lax.broadcasted_iota(jnp.int32, (wlen, uq), 1)
        rel = kpos - qpos
        valid = (jnp.abs(rel) <= half) & (kpos >= s0_ref[i]) & (kpos < s1_ref[i])
        distm = jnp.where(valid, jnp.abs(rel).astype(F32), 1e32)
        for rr in range(rps):
            lse_rows = []
            for g, pair in enumerate(heads):
                qc = rr * len(heads) + g
                kc = rr * (kvw // LANES) + kv_group[g]
                qg = q_ref[rows, qc * LANES:(qc + 1) * LANES]
                kg = kwin[:, kc * LANES:(kc + 1) * LANES]
                vg = vwin[:, kc * LANES:(kc + 1) * LANES]
                outs = []
                for s, (slope, sink_idx) in enumerate(pair):
                    keep = lo_lanes if s == 0 else jnp.logical_not(lo_lanes)
                    qm = jnp.where(keep, qg, jnp.zeros_like(qg))
                    sc = lax.dot_general(kg, qm, (((1,), (1,)), ((), ())), preferred_element_type=F32)
                    sc = sc - (slope * LOG2E) * distm
                    m = jnp.max(sc, axis=0, keepdims=True)
                    if has_sink:
                        sink2 = sink_ref[sink_idx] * LOG2E
                        m = jnp.maximum(m, sink2)
                    e = jnp.exp2(sc - m)
                    den = jnp.sum(e, axis=0, keepdims=True)
                    if has_sink:
                        den = den + jnp.exp2(sink2 - m)
                    p = (e * (1.0 / den)).astype(BF16)
                    outs.append(lax.dot_general(p, vg, (((0,), (0,)), ((), ())), preferred_element_type=F32))
                    if want_lse:
                        lse_rows.append((m + jnp.log2(den)) * LN2)
                o_ref[rows, qc * LANES:(qc + 1) * LANES] = (
                    jnp.where(lo_lanes, outs[0], outs[1]).astype(o_ref.dtype))
            if want_lse:
                lse_t = jnp.concatenate(lse_rows + [jnp.zeros((LANES - len(lse_rows), uq), F32)], axis=0)
                lse_ref[rows, rr * LANES:(rr + 1) * LANES] = lse_t.T


def _band_attn(q, k, v, seq_rows, *, tq, dil, qw, kvw, half, heads, kv_group, sink, out_dtype, want_lse):
    rows = q.shape[0]
    hb = half
    s0 = _tile_table(seq_rows, tq, lambda sid, s, n, j: s)
    s1 = _tile_table(seq_rows, tq, lambda sid, s, n, j: s + n)
    per = tq // hb
    nt = rows // tq
    nhb = rows // hb
    has_sink = sink is not None
    in_specs = []
    args = []
    if has_sink:
        in_specs.append(pl.BlockSpec(memory_space=pltpu.SMEM))
        args.append(sink)
    qmap = lambda r, i, a, b: (i, r)
    pmap = lambda r, i, a, b: (jnp.maximum(i * per - 1, 0), r)
    nmap = lambda r, i, a, b: (jnp.minimum((i + 1) * per, nhb - 1), r)
    rps = min(dil, BAND_RESIDUES_PER_STEP)
    in_specs += [pl.BlockSpec((tq, rps * qw), qmap),
                 pl.BlockSpec((hb, rps * kvw), pmap), pl.BlockSpec((tq, rps * kvw), qmap),
                 pl.BlockSpec((hb, rps * kvw), nmap),
                 pl.BlockSpec((hb, rps * kvw), pmap), pl.BlockSpec((tq, rps * kvw), qmap),
                 pl.BlockSpec((hb, rps * kvw), nmap)]
    args += [q, k, k, k, v, v, v]
    out_shape = [jax.ShapeDtypeStruct((rows, dil * qw), out_dtype)]
    out_specs = [pl.BlockSpec((tq, rps * qw), qmap)]
    if want_lse:
        out_shape.append(jax.ShapeDtypeStruct((rows, dil * LANES), F32))
        out_specs.append(pl.BlockSpec((tq, rps * LANES), qmap))
    gs = pltpu.PrefetchScalarGridSpec(num_scalar_prefetch=2, grid=(dil // rps, nt),
                                      in_specs=in_specs, out_specs=out_specs)
    res = pl.pallas_call(
        functools.partial(_band_kernel, tq=tq, hb=hb, half=half, heads=heads, kv_group=kv_group, kvw=kvw,
                          rps=rps, has_sink=has_sink, want_lse=want_lse),
        grid_spec=gs,
        out_shape=out_shape,
        compiler_params=_cparams(("arbitrary", "arbitrary")),
        name="band_attn_d%d" % dil,
    )(s0, s1, *args)
    return res


def _na_kernel(r0_ref, nr_ref, q_ref, kp_ref, kc_ref, kn_ref, vp_ref, vc_ref, vn_ref, bias_ref,
               o_ref, kcat, vcat, *, halo):
    j = pl.program_id(0)
    tq = NA_TROWS * GRID_W
    hrows = halo * GRID_W
    kcat[0:hrows, :] = kp_ref[...]
    kcat[hrows:hrows + tq, :] = kc_ref[...]
    kcat[hrows + tq:, :] = kn_ref[...]
    vcat[0:hrows, :] = vp_ref[...]
    vcat[hrows:hrows + tq, :] = vc_ref[...]
    vcat[hrows + tq:, :] = vn_ref[...]
    r0 = r0_ref[j]
    nrows = nr_ref[j]
    kwin_len = NA_KH * GRID_W
    lane = lax.broadcasted_iota(jnp.int32, (GRID_W, LANES), 1)
    lo_lanes = lane < HEAD_DIM
    nrel = 2 * NA_KH - 1

    def row_body(u, carry):
        r = r0 + u
        rs = jnp.clip(r - NA_KH // 2, 0, nrows - NA_KH)
        off = pl.multiple_of((rs - r0 + halo) * GRID_W, GRID_W)
        bvar = rs - r + NA_KH - 1
        qrow = pl.multiple_of(u * GRID_W, GRID_W)
        for g in range(C_HEADS // 2):
            qg = q_ref[pl.ds(qrow, GRID_W), g * LANES:(g + 1) * LANES]
            qs = jnp.concatenate([jnp.where(lo_lanes, qg, jnp.zeros_like(qg)),
                                  jnp.where(lo_lanes, jnp.zeros_like(qg), qg)], axis=0)
            kg = kcat[pl.ds(off, kwin_len), g * LANES:(g + 1) * LANES]
            vg = vcat[pl.ds(off, kwin_len), g * LANES:(g + 1) * LANES]
            sc = lax.dot_general(kg, qs, (((1,), (1,)), ((), ())), preferred_element_type=F32)
            sc = sc + jnp.concatenate([bias_ref[g * nrel + bvar + kr] for kr in range(NA_KH)], axis=0)
            m = jnp.max(sc, axis=0, keepdims=True)
            e = jnp.exp2(sc - m)
            den = jnp.sum(e, axis=0, keepdims=True)
            p = (e * (1.0 / den)).astype(BF16)
            pv = lax.dot_general(p, vg, (((0,), (0,)), ((), ())), preferred_element_type=F32)
            o = jnp.where(lo_lanes, pv[0:GRID_W], pv[GRID_W:])
            o_ref[pl.ds(qrow, GRID_W), g * LANES:(g + 1) * LANES] = o.astype(o_ref.dtype)
        return carry

    lax.fori_loop(0, NA_TROWS, row_body, 0, unroll=True)


def _na_attn(q, k, v, bias_tab, tile_r0, tile_nr):
    t = q.shape[0]
    halo = NA_KH // 2
    tq = NA_TROWS * GRID_W
    hrows = halo * GRID_W
    per = tq // hrows
    nhb = t // hrows
    nt = t // tq
    qmap = lambda j, a, b: (j, 0)
    pmap = lambda j, a, b: (jnp.maximum(j * per - 1, 0), 0)
    nmap = lambda j, a, b: (jnp.minimum((j + 1) * per, nhb - 1), 0)
    gs = pltpu.PrefetchScalarGridSpec(
        num_scalar_prefetch=2, grid=(nt,),
        in_specs=[pl.BlockSpec((tq, C_W), qmap),
                  pl.BlockSpec((hrows, C_W), pmap), pl.BlockSpec((tq, C_W), qmap), pl.BlockSpec((hrows, C_W), nmap),
                  pl.BlockSpec((hrows, C_W), pmap), pl.BlockSpec((tq, C_W), qmap), pl.BlockSpec((hrows, C_W), nmap),
                  pl.BlockSpec(bias_tab.shape, lambda j, a, b: (0, 0, 0))],
        out_specs=pl.BlockSpec((tq, C_W), qmap),
        scratch_shapes=[pltpu.VMEM((tq + 2 * hrows, C_W), BF16), pltpu.VMEM((tq + 2 * hrows, C_W), BF16)])
    return pl.pallas_call(
        functools.partial(_na_kernel, halo=halo),
        grid_spec=gs,
        out_shape=jax.ShapeDtypeStruct((t, C_W), BF16),
        compiler_params=_cparams(("arbitrary",)),
        name="na_attn",
    )(tile_r0, tile_nr, q, k, k, k, v, v, v, bias_tab)


def _na_bias_table(rpb):
    c = np.arange(GRID_W)
    cs = np.clip(c - NA_KW // 2, 0, GRID_W - NA_KW)
    kc = np.arange(GRID_W)
    valid = (kc[:, None] >= cs[None, :]) & (kc[:, None] < cs[None, :] + NA_KW)
    span = GRID_W - NA_KW
    nrel = 2 * NA_KH - 1
    rpb2 = rpb.astype(F32).reshape(C_HEADS // 2, 2, nrel, 2 * NA_KW - 1) * LOG2E
    rev = jnp.flip(jnp.pad(rpb2, ((0, 0), (0, 0), (0, 0), (span, span)), mode="edge"), axis=-1)
    row = jnp.pad(rev, ((0, 0), (0, 0), (0, 0), (0, 1)))
    skew = jnp.broadcast_to(row[..., None, :], row.shape[:-1] + (GRID_W, 2 * GRID_W))
    skew = skew.reshape(row.shape[:-1] + (2 * GRID_W * GRID_W,))[..., :GRID_W * (2 * GRID_W - 1)]
    tab = skew.reshape(row.shape[:-1] + (GRID_W, 2 * GRID_W - 1))[..., GRID_W - 1:]
    tab = jnp.where(jnp.asarray(valid), tab, NEG_INF)
    pairs = jnp.concatenate([tab[:, 0], tab[:, 1]], axis=-1)
    return pairs.reshape((C_HEADS // 2) * (2 * NA_KH - 1), GRID_W, LANES)


def _route(h2, wr_ref, br_ref, idx_ref, gate_ref):
    tm = h2.shape[0]
    h_hi = h2.astype(BF16)
    h_lo = (h2 - h_hi.astype(F32)).astype(BF16)
    w = wr_ref[...]
    w_hi = w.astype(BF16)
    w_lo = (w - w_hi.astype(F32)).astype(BF16)
    nt = (((1,), (1,)), ((), ()))
    logits = (lax.dot_general(w_hi, h_hi, nt, preferred_element_type=F32)
              + lax.dot_general(w_lo, h_hi, nt, preferred_element_type=F32)
              + lax.dot_general(w_hi, h_lo, nt, preferred_element_type=F32)) + br_ref[:, 0:1]
    eid = lax.broadcasted_iota(jnp.int32, (N_EXPERTS, tm), 0)
    vals, idxs = [], []
    for _ in range(TOP_K):
        m = jnp.max(logits, axis=0, keepdims=True)
        ix = jnp.min(jnp.where(logits == m, eid, N_EXPERTS), axis=0, keepdims=True)
        vals.append(m)
        idxs.append(ix)
        logits = jnp.where(eid == ix, -jnp.inf, logits)
    es = [jnp.exp(vk - vals[0]) for vk in vals]
    den = es[0] + es[1] + es[2] + es[3]
    pad_i = jnp.zeros((8 - TOP_K, tm), jnp.int32)
    pad_f = jnp.zeros((8 - TOP_K, tm), F32)
    idx_ref[...] = jnp.concatenate(idxs + [pad_i], axis=0)
    gate_ref[...] = jnp.concatenate([ek / den for ek in es] + [pad_f], axis=0)


PACK_W = D_MODEL // 2


def _pack_bf16_rows(h):
    bits = pltpu.bitcast(h.astype(BF16).astype(F32), jnp.int32)
    return (bits[:, :PACK_W] & jnp.int32(-65536)) | lax.shift_right_logical(bits[:, PACK_W:], 16)


def _unpack_rows_f32(w):
    hi = pltpu.bitcast(w & jnp.int32(-65536), F32)
    lo = pltpu.bitcast(lax.shift_left(w, 16), F32)
    return jnp.concatenate([hi, lo], axis=1)


def _unpack_bf16_rows(w):
    return _unpack_rows_f32(w).astype(BF16)


def _out0_kernel(sid_ref, *refs, tm, part_starts):
    del sid_ref
    npart = len(part_starts)
    x_refs = refs[:npart]
    (mod_ref, n2_ref, oa_ref, ob0_ref, ls0_ref, ob1_ref, ls1_ref, ob2_ref, ls2_ref,
     wout_ref, wr_ref, br_ref, x1_ref, h2_ref, idx_ref, gate_ref, oscr, lscr) = refs[npart:]
    os_, ls_ = [], []
    for gi, (o_ref, l_ref) in enumerate(((ob0_ref, ls0_ref), (ob1_ref, ls1_ref), (ob2_ref, ls2_ref))):
        dil = B_PAIRS[gi][1]
        if dil == 1:
            os_.append(o_ref[...])
            ls_.append(l_ref[...])
        else:
            ncg = B_GW // LANES
            for r in range(dil):
                for c in range(ncg):
                    lo = r * B_GW + c * LANES
                    oscr[gi * ncg + c, pl.ds(r, tm // dil, stride=dil), :] = o_ref[:, lo:lo + LANES]
                lscr[gi, pl.ds(r, tm // dil, stride=dil), :] = l_ref[:, r * LANES:(r + 1) * LANES]
            os_.append(jnp.concatenate([oscr[gi * ncg + c] for c in range(ncg)], axis=1))
            ls_.append(lscr[gi])
    lmax = jnp.maximum(jnp.maximum(ls_[0], ls_[1]), ls_[2])
    ws = [jnp.exp(l - lmax) for l in ls_]
    winv = 1.0 / (ws[0] + ws[1] + ws[2])
    lane = lax.broadcasted_iota(jnp.int32, (tm, LANES), 1)
    lo_lanes = lane < HEAD_DIM

    def head_weights(w):
        return jnp.concatenate(
            [jnp.where(lo_lanes, jnp.broadcast_to(w[:, 2 * pr:2 * pr + 1], (tm, LANES)),
                       jnp.broadcast_to(w[:, 2 * pr + 1:2 * pr + 2], (tm, LANES)))
             for pr in range(B_HEADS // 2)], axis=1)

    ob = (head_weights(ws[0] * winv) * os_[0] + head_weights(ws[1] * winv) * os_[1]
          + head_weights(ws[2] * winv) * os_[2])
    o = (jnp.dot(oa_ref[...], wout_ref[0:A_QW, :], preferred_element_type=F32)
         + jnp.dot(ob.astype(BF16), wout_ref[A_QW:, :], preferred_element_type=F32))
    x1 = _read_part(x_refs, part_starts) + mod_ref[2:3, :] * o
    x1_ref[...] = x1
    h2 = _rms_mod(x1, n2_ref, mod_ref, 3, 4)
    h2_ref[...] = _pack_bf16_rows(h2)
    _route(h2, wr_ref, br_ref, idx_ref, gate_ref)


def _out1_kernel(sid_ref, *refs, part_starts):
    del sid_ref
    npart = len(part_starts)
    x_refs = refs[:npart]
    mod_ref, n2_ref, oc_ref, wout_ref, wr_ref, br_ref, x1_ref, h2_ref, idx_ref, gate_ref = refs[npart:]
    o = jnp.dot(oc_ref[...], wout_ref[...], preferred_element_type=F32)
    x1 = _read_part(x_refs, part_starts) + mod_ref[2:3, :] * o
    x1_ref[...] = x1
    h2 = _rms_mod(x1, n2_ref, mod_ref, 3, 4)
    h2_ref[...] = _pack_bf16_rows(h2)
    _route(h2, wr_ref, br_ref, idx_ref, gate_ref)


def _out_proj(layer, x_parts, mod, tile_sid, n2, attn, wout_bf, wr_t, br):
    tm = TOKEN_TILE
    part_starts, ntiles = _part_tiles(x_parts, tm)
    t = ntiles * tm
    row = lambda i, sid: (i, 0)
    const = lambda i, sid: (0, 0)
    in_specs = _part_specs(x_parts, part_starts, tm, D_MODEL) + [
        pl.BlockSpec((None, 6, D_MODEL), lambda i, sid: (sid[i], 0, 0)),
        pl.BlockSpec((1, D_MODEL), const)]
    scratch = []
    if layer == 0:
        in_specs.append(pl.BlockSpec((tm, A_QW), row))
        for (_, dil) in B_PAIRS:
            in_specs += [pl.BlockSpec((tm // dil, dil * B_GW), row), pl.BlockSpec((tm // dil, dil * LANES), row)]
        body = functools.partial(_out0_kernel, tm=tm, part_starts=part_starts)
        nscr = B_GROUPS * B_GW // LANES
        scratch = [pltpu.VMEM((nscr, tm, LANES), F32), pltpu.VMEM((B_GROUPS, tm, LANES), F32)]
    else:
        in_specs.append(pl.BlockSpec((tm, C_W), row))
        body = functools.partial(_out1_kernel, part_starts=part_starts)
    in_specs += [pl.BlockSpec(wout_bf.shape, const),
                 pl.BlockSpec((N_EXPERTS, D_MODEL), const),
                 pl.BlockSpec((N_EXPERTS, LANES), const)]
    gs = pltpu.PrefetchScalarGridSpec(
        num_scalar_prefetch=1, grid=(t // tm,), in_specs=in_specs,
        out_specs=[pl.BlockSpec((tm, D_MODEL), row), pl.BlockSpec((tm, PACK_W), row),
                   pl.BlockSpec((8, tm), lambda i, sid: (0, i)), pl.BlockSpec((8, tm), lambda i, sid: (0, i))],
        scratch_shapes=scratch)
    return pl.pallas_call(
        body, grid_spec=gs,
        out_shape=[jax.ShapeDtypeStruct((t, D_MODEL), F32), jax.ShapeDtypeStruct((t, PACK_W), jnp.int32),
                   jax.ShapeDtypeStruct((8, t), jnp.int32), jax.ShapeDtypeStruct((8, t), F32)],
        compiler_params=_cparams(("arbitrary",)),
        name="out_proj%d" % layer,
    )(tile_sid, *x_parts, mod, n2.reshape(1, D_MODEL), *attn, wout_bf, wr_t, br)


def _rank_kernel(idx_ref, tri_ref, rank_ref, cnt_ref, carry):
    i = pl.program_id(0)

    @pl.when(i == 0)
    def _():
        carry[...] = jnp.zeros_like(carry)

    tk = idx_ref.shape[1]
    eid = lax.broadcasted_iota(jnp.int32, (N_EXPERTS, tk), 0)
    hits = [eid == idx_ref[k:k + 1, :] for k in range(TOP_K)]
    onehot = sum(h.astype(F32) for h in hits)
    incl = jnp.dot(onehot.astype(BF16), tri_ref[...], preferred_element_type=F32)
    before = incl - onehot + carry[:, 0:1]
    rows = [jnp.sum(jnp.where(h, before, 0.0), axis=0, keepdims=True) for h in hits]
    rows.append(jnp.zeros((8 - TOP_K, tk), F32))
    rank_ref[...] = jnp.concatenate(rows, axis=0).astype(jnp.int32)
    carry[...] = carry[...] + incl[:, tk - 1:tk]
    cnt_ref[...] = carry[...].astype(jnp.int32)


def _slot_kernel(idx_ref, rank_ref, start_ref, slot_ref):
    tk = idx_ref.shape[1]
    eid = lax.broadcasted_iota(jnp.int32, (N_EXPERTS, tk), 0)
    start = start_ref[:, 0:1]
    rows = []
    for k in range(TOP_K):
        base = jnp.sum(jnp.where(eid == idx_ref[k:k + 1, :], start, 0), axis=0, keepdims=True)
        rows.append(base + rank_ref[k:k + 1, :])
    rows.append(jnp.zeros((8 - TOP_K, tk), jnp.int32))
    slot_ref[...] = jnp.concatenate(rows, axis=0)


def _route_slots(idx):
    t = idx.shape[1]
    tk = ROUTE_TILE
    tri = jnp.asarray(np.triu(np.ones((tk, tk), np.float32)), BF16)
    tok = lambda i: (0, i)
    rank, cnt = pl.pallas_call(
        _rank_kernel, grid=(t // tk,),
        in_specs=[pl.BlockSpec((8, tk), tok), pl.BlockSpec((tk, tk), lambda i: (0, 0))],
        out_specs=[pl.BlockSpec((8, tk), tok), pl.BlockSpec((N_EXPERTS, LANES), lambda i: (0, 0))],
        out_shape=[jax.ShapeDtypeStruct((8, t), jnp.int32), jax.ShapeDtypeStruct((N_EXPERTS, LANES), jnp.int32)],
        scratch_shapes=[pltpu.VMEM((N_EXPERTS, LANES), F32)],
        compiler_params=_cparams(("arbitrary",)),
        name="route_rank",
    )(idx, tri)
    counts = cnt[:, 0]
    padded = (counts + MOE_BM - 1) // MOE_BM * MOE_BM
    pad_end = jnp.cumsum(padded)
    pad_start = pad_end - padded
    nblk = (t * TOP_K) // MOE_BM + N_EXPERTS
    blk_row0 = jnp.arange(nblk, dtype=jnp.int32) * MOE_BM
    blk_expert = jnp.minimum(jnp.sum((pad_end[None, :] <= blk_row0[:, None]).astype(jnp.int32), axis=1),
                             N_EXPERTS - 1)
    nused = (pad_end[-1] // MOE_BM).astype(jnp.int32).reshape(1)
    eids = jnp.arange(N_EXPERTS, dtype=jnp.int32)
    later = (eids[None, :] > eids[:, None]) & (padded[None, :] > 0)
    next_expert = jnp.min(jnp.where(later, eids[None, :], N_EXPERTS), axis=1)
    next_expert = jnp.where(next_expert == N_EXPERTS, -1, next_expert).astype(jnp.int32)
    blk_next = jnp.sum(jnp.where(blk_expert[:, None] == eids[None, :], next_expert[None, :], 0), axis=1)
    start_b = jnp.broadcast_to(pad_start.astype(jnp.int32)[:, None], (N_EXPERTS, LANES))
    slot = pl.pallas_call(
        _slot_kernel, grid=(t // tk,),
        in_specs=[pl.BlockSpec((8, tk), tok), pl.BlockSpec((8, tk), tok),
                  pl.BlockSpec((N_EXPERTS, LANES), lambda i: (0, 0))],
        out_specs=pl.BlockSpec((8, tk), tok),
        out_shape=jax.ShapeDtypeStruct((8, t), jnp.int32),
        compiler_params=_cparams(("arbitrary",)),
        name="route_slot",
    )(idx, rank, start_b)
    return slot, blk_expert, nused, blk_next


SC_CORES = 2
SC_SUBCORES = 16
SC_ROW_BUFFER_BYTES = 256 * 1024
SC_MAX_INDEX_LIST = 128


def _sc_chunks(t, row_bytes):
    chunk = min(SC_MAX_INDEX_LIST, SC_ROW_BUFFER_BYTES // row_bytes)
    nchunk = t // chunk
    per = nchunk // (SC_CORES * SC_SUBCORES)
    assert per * SC_CORES * SC_SUBCORES * chunk == t
    return chunk, nchunk, per


def _sc_slot_lists(slot, nchunk, chunk):
    return slot[:TOP_K].reshape(TOP_K, nchunk, chunk).transpose(1, 0, 2)


def _sc_dispatch(h2, slot, nslots):
    t, w = h2.shape
    chunk, nchunk, per = _sc_chunks(t, w * h2.dtype.itemsize)
    mesh = plsc.VectorSubcoreMesh(core_axis_name="c", subcore_axis_name="s")

    @functools.partial(
        pl.kernel, mesh=mesh, out_type=jax.ShapeDtypeStruct((nslots, w), h2.dtype),
        scratch_types=[pltpu.VMEM((TOP_K, chunk), jnp.int32), pltpu.VMEM((chunk, w), h2.dtype)],
        name="moe_dispatch_sc")
    def body(h_hbm, slot_hbm, xs_hbm, idx_v, rows_v):
        wid = lax.axis_index("s") * SC_CORES + lax.axis_index("c")

        @pl.loop(0, per)
        def _(j):
            c = wid * per + j
            pltpu.sync_copy(slot_hbm.at[c], idx_v)
            pltpu.sync_copy(h_hbm.at[pl.ds(c * chunk, chunk)], rows_v)
            for k in range(TOP_K):
                pltpu.sync_copy(rows_v, xs_hbm.at[idx_v.at[k]])

    return body(h2, _sc_slot_lists(slot, nchunk, chunk))


def _sc_gather(ys, slot, t):
    w = ys.shape[1]
    nbuf = 2
    chunk, nchunk, per = _sc_chunks(t, nbuf * w * ys.dtype.itemsize)
    mesh = plsc.VectorSubcoreMesh(core_axis_name="c", subcore_axis_name="s")

    @functools.partial(
        pl.kernel, mesh=mesh, out_type=jax.ShapeDtypeStruct((TOP_K, t, w), ys.dtype),
        scratch_types=[pltpu.VMEM((TOP_K, chunk), jnp.int32), pltpu.VMEM((nbuf, chunk, w), ys.dtype),
                       pltpu.SemaphoreType.DMA((nbuf,))],
        name="moe_gather_sc")
    def body(ys_hbm, slot_hbm, out_hbm, idx_v, rows_v, wsem):
        wid = lax.axis_index("s") * SC_CORES + lax.axis_index("c")

        @pl.loop(0, per)
        def _(j):
            c = wid * per + j
            pltpu.sync_copy(slot_hbm.at[c], idx_v)
            writes = []
            for k in range(TOP_K):
                b = k % nbuf
                if k >= nbuf:
                    writes[k - nbuf].wait()
                pltpu.sync_copy(ys_hbm.at[idx_v.at[k]], rows_v.at[b])
                writes.append(pltpu.async_copy(rows_v.at[b], out_hbm.at[k, pl.ds(c * chunk, chunk)], wsem.at[b]))
            for wr in writes[-nbuf:]:
                wr.wait()

    return body(ys, _sc_slot_lists(slot, nchunk, chunk))


def _ffn_kernel(be_ref, nu_ref, nxt_ref, xs_ref, wgu_hbm, bgu_ref, wd_hbm, bd_ref, y_ref,
                wgu_f, wd_f, wgu_bf, wd_bf, wsem):
    i = pl.program_id(0)
    prev = be_ref[jnp.maximum(i - 1, 0)]

    def weight_copies(e):
        return (pltpu.make_async_copy(wgu_hbm.at[e], wgu_f, wsem.at[0]),
                pltpu.make_async_copy(wd_hbm.at[e], wd_f, wsem.at[1]))

    @pl.when(i == 0)
    def _():
        for cp in weight_copies(be_ref[0]):
            cp.start()

    @pl.when(i < nu_ref[0])
    def _():
        @pl.when((i == 0) | (be_ref[i] != prev))
        def _():
            for cp in weight_copies(be_ref[i]):
                cp.wait()
            wgu_bf[...] = wgu_f[...].astype(BF16)
            wd_bf[...] = wd_f[...].astype(BF16)

            @pl.when(nxt_ref[i] >= 0)
            def _():
                for cp in weight_copies(nxt_ref[i]):
                    cp.start()

        x = _unpack_bf16_rows(xs_ref[...])
        gu = jnp.dot(x, wgu_bf[...], preferred_element_type=F32) + bgu_ref[...]
        gate = jnp.minimum(gu[:, :D_FF], SWIGLU_LIMIT)
        up = jnp.clip(gu[:, D_FF:], -SWIGLU_LIMIT, SWIGLU_LIMIT)
        act = (up + 1.0) * gate * jax.nn.sigmoid(SWIGLU_ALPHA * gate)
        y = jnp.dot(act.astype(BF16), wd_bf[...], preferred_element_type=F32) + bd_ref[...]
        y_ref[...] = _pack_bf16_rows(y)

    @pl.when(i >= nu_ref[0])
    def _():
        y_ref[...] = jnp.zeros_like(y_ref)


def _expert_ffn(xs, blk_expert, nused, blk_next, wgu, bgu, wd, bd):
    nslots = xs.shape[0]
    nblk = nslots // MOE_BM
    blk = lambda i, be, nu, nx: (jnp.minimum(i, nu[0] - 1), 0)
    exp3 = lambda i, be, nu, nx: (be[jnp.minimum(i, nu[0] - 1)], 0, 0)
    gs = pltpu.PrefetchScalarGridSpec(
        num_scalar_prefetch=3, grid=(nblk,),
        in_specs=[pl.BlockSpec((MOE_BM, PACK_W), blk),
                  pl.BlockSpec(memory_space=pl.ANY),
                  pl.BlockSpec((None, 1, 2 * D_FF), exp3),
                  pl.BlockSpec(memory_space=pl.ANY),
                  pl.BlockSpec((None, 1, D_MODEL), exp3)],
        out_specs=pl.BlockSpec((MOE_BM, PACK_W), lambda i, be, nu, nx: (i, 0)),
        scratch_shapes=[pltpu.VMEM((D_MODEL, 2 * D_FF), F32), pltpu.VMEM((D_FF, D_MODEL), F32),
                        pltpu.VMEM((D_MODEL, 2 * D_FF), BF16), pltpu.VMEM((D_FF, D_MODEL), BF16),
                        pltpu.SemaphoreType.DMA((2,))])
    return pl.pallas_call(
        _ffn_kernel, grid_spec=gs,
        out_shape=jax.ShapeDtypeStruct((nslots, PACK_W), jnp.int32),
        compiler_params=_cparams(("arbitrary",)),
        name="moe_ffn",
    )(blk_expert, nused, blk_next, xs, wgu, bgu.reshape(N_EXPERTS, 1, 2 * D_FF), wd,
      bd.reshape(N_EXPERTS, 1, D_MODEL))


def _combine_kernel(sid_ref, x_ref, mod_ref, gate_ref, yk_ref, *o_refs, tm, out_starts):
    del sid_ref
    gpad = jnp.concatenate([gate_ref[...], jnp.zeros((LANES - 8, tm), F32)], axis=0)
    gcol = gpad.T
    acc = gcol[:, 0:1] * _unpack_rows_f32(yk_ref[0])
    for k in range(1, TOP_K):
        acc = acc + gcol[:, k:k + 1] * _unpack_rows_f32(yk_ref[k])
    out = x_ref[...] + mod_ref[5:6, :] * acc
    i = pl.program_id(0)
    ends = out_starts[1:] + (None,)
    for o_ref, s0, s1 in zip(o_refs, out_starts, ends):
        mine = (i >= s0) if s1 is None else ((i >= s0) & (i < s1))

        @pl.when(mine)
        def _():
            o_ref[...] = out


def _combine(x1, mod, tile_sid, gates, yk, out_rows):
    t = x1.shape[0]
    tm = MOVE_TILE
    out_shape = [jax.ShapeDtypeStruct((n, D_MODEL), F32) for n in out_rows]
    out_starts, ntiles = _part_tiles(out_shape, tm)
    assert ntiles * tm == t
    gs = pltpu.PrefetchScalarGridSpec(
        num_scalar_prefetch=1, grid=(t // tm,),
        in_specs=[pl.BlockSpec((tm, D_MODEL), lambda i, sid: (i, 0)),
                  pl.BlockSpec((None, 6, D_MODEL), lambda i, sid: (sid[i], 0, 0)),
                  pl.BlockSpec((8, tm), lambda i, sid: (0, i)),
                  pl.BlockSpec((TOP_K, tm, PACK_W), lambda i, sid: (0, i, 0))],
        out_specs=_part_specs(out_shape, out_starts, tm, D_MODEL))
    return pl.pallas_call(
        functools.partial(_combine_kernel, tm=tm, out_starts=out_starts), grid_spec=gs,
        out_shape=out_shape,
        compiler_params=_cparams(("arbitrary",)),
        name="moe_combine",
    )(tile_sid, x1, mod, gates, yk)


def _moe(x1, h2, idx, gates, mod, tile_sid_move, wgu, bgu, wd, bd, out_rows):
    t = x1.shape[0]
    nslots = t * TOP_K + N_EXPERTS * MOE_BM
    slot, blk_expert, nused, blk_next = _route_slots(idx)
    xs = _sc_dispatch(h2, slot, nslots)
    ys = _expert_ffn(xs, blk_expert, nused, blk_next, wgu, bgu, wd, bd)
    return _combine(x1, mod, tile_sid_move, gates, _sc_gather(ys, slot, t), out_rows)


def _tile_table(seq_lens, tile, fn):
    vals = []
    start = 0
    for sid, n in enumerate(seq_lens):
        assert n % tile == 0
        for j in range(n // tile):
            vals.append(fn(sid, start, n, j))
        start += n
    return jnp.asarray(np.asarray(vals, np.int32))


def _head_block_diag():
    bd = np.kron(np.eye(2 * LANES // HEAD_DIM, dtype=np.float32),
                 np.full((HEAD_DIM, HEAD_DIM), 1.0 / HEAD_DIM, np.float32))
    return jnp.asarray(bd, BF16)


_A_HEAD_ORDER = [kv * A_GROUP + g for g in range(A_GROUP) for kv in range(A_KV_HEADS)]


def _trunk(x_parts, mods, mod_row0, seq_lens, p):
    t = sum(a.shape[0] for a in x_parts)
    sid_tok = _tile_table(seq_lens, TOKEN_TILE, lambda sid, s, n, j: mod_row0 + sid)
    sid_move = _tile_table(seq_lens, MOVE_TILE, lambda sid, s, n, j: mod_row0 + sid)
    bd = _head_block_diag()
    mod0, mod1 = mods

    w_in = p["l0_w_in"]
    a_cols = np.concatenate([np.arange(h * HEAD_DIM, (h + 1) * HEAD_DIM) for h in _A_HEAD_ORDER])
    bq, bk, bv = A_IN, A_IN + B_GROUPS * B_GW, A_IN + 2 * B_GROUPS * B_GW
    col_blocks = [w_in[:, a_cols], w_in[:, A_QW:A_IN]]
    for g in range(B_GROUPS):
        col_blocks += [w_in[:, bq + g * B_GW:bq + (g + 1) * B_GW],
                       w_in[:, bk + g * B_GW:bk + (g + 1) * B_GW],
                       w_in[:, bv + g * B_GW:bv + (g + 1) * B_GW]]
    w0 = jnp.concatenate(col_blocks, axis=1).astype(BF16)
    ones = lambda n: jnp.ones((n,), F32)
    gain_blocks = [jnp.tile(p["l0_q_norm_a"], A_Q_HEADS) * QK_SCALE, jnp.tile(p["l0_k_norm_a"], A_KV_HEADS),
                   ones(A_KVW)]
    for g in range(B_GROUPS):
        gain_blocks += [jnp.tile(p["l0_q_norm_b"], B_HEADS) * QK_SCALE, jnp.tile(p["l0_k_norm_b"], B_HEADS),
                        ones(B_GW)]
    gain0 = jnp.concatenate(gain_blocks).reshape(1, -1)
    slabs0 = [(0, A_QW, True, 1), (A_QW, A_KVW, True, 1), (A_QW + A_KVW, A_KVW, False, 1)]
    c0 = A_IN
    for g, (_, dil) in enumerate(B_PAIRS):
        slabs0 += [(c0, B_GW, True, dil), (c0 + B_GW, B_GW, True, dil), (c0 + 2 * B_GW, B_GW, False, dil)]
        c0 += 3 * B_GW
    proj = _in_proj(x_parts, mod0, sid_tok, p["l0_norm1"], w0, gain0, bd, tuple(slabs0))
    qa, ka, va = proj[0:3]

    slopes_a = _alibi_slopes(A_Q_HEADS)
    heads_a = tuple(tuple((slopes_a[kv * A_GROUP + g], kv * A_GROUP + g) for kv in range(A_KV_HEADS))
                    for g in range(A_GROUP))
    (oa,) = _band_attn(qa, ka, va, seq_lens, tq=ATTN_TQ_A, dil=1, qw=A_QW, kvw=A_KVW, half=A_HALF, heads=heads_a,
                       kv_group=(0,) * A_GROUP, sink=p["l0_sink_a"].astype(F32), out_dtype=BF16, want_lse=False)

    slopes_b = _alibi_slopes(B_GROUPS * B_HEADS)
    attn0 = [oa]
    for g, (window, dil) in enumerate(B_PAIRS):
        qg, kg, vg = proj[3 + 3 * g:6 + 3 * g]
        heads_b = tuple(tuple((slopes_b[g * B_HEADS + 2 * pr + s] * dil, 0) for s in range(2))
                        for pr in range(B_HEADS // 2))
        tq_b = ATTN_TQ_B * BAND_RESIDUES_PER_STEP // min(dil, BAND_RESIDUES_PER_STEP)
        og, lg = _band_attn(qg, kg, vg, [n // dil for n in seq_lens], tq=tq_b, dil=dil, qw=B_GW, kvw=B_GW, half=window // (2 * dil),
                            heads=heads_b, kv_group=tuple(range(B_HEADS // 2)), sink=None, out_dtype=F32,
                            want_lse=True)
        attn0 += [og, lg]

    w_out = p["l0_w_out"]
    wout0 = jnp.concatenate([w_out[a_cols], w_out[A_QW:]], axis=0).astype(BF16)
    br0 = jnp.broadcast_to(p["l0_b_router"].astype(F32)[:, None], (N_EXPERTS, LANES))
    x1, h2, idx, gates = _out_proj(0, x_parts, mod0, sid_tok, p["l0_norm2"], attn0, wout0, p["l0_w_router"].T,
                                   br0)
    x = _moe(x1, h2, idx, gates, mod0, sid_move, p["l0_w_gate_up"], p["l0_b_gate_up"], p["l0_w_down"],
             p["l0_b_down"], [t])

    w1 = p["l1_w_in"].astype(BF16)
    gain1 = jnp.concatenate([jnp.tile(p["l1_q_norm_c"], C_HEADS) * QK_SCALE, jnp.tile(p["l1_k_norm_c"], C_HEADS),
                             ones(C_W)]).reshape(1, -1)
    slabs1 = ((0, C_W, True, 1), (C_W, C_W, True, 1), (2 * C_W, C_W, False, 1))
    qc, kc, vc = _in_proj(x, mod1, sid_tok, p["l1_norm1"], w1, gain1, bd, slabs1)
    na_tile = NA_TROWS * GRID_W
    tile_r0 = _tile_table(seq_lens, na_tile, lambda sid, s, n, j: j * NA_TROWS)
    tile_nr = _tile_table(seq_lens, na_tile, lambda sid, s, n, j: n // GRID_W)
    oc = _na_attn(qc, kc, vc, _na_bias_table(p["l1_rpb_c"]), tile_r0, tile_nr)
    br1 = jnp.broadcast_to(p["l1_b_router"].astype(F32)[:, None], (N_EXPERTS, LANES))
    x1, h2, idx, gates = _out_proj(1, x, mod1, sid_tok, p["l1_norm2"], [oc], p["l1_w_out"].astype(BF16),
                                   p["l1_w_router"].T, br1)
    return _moe(x1, h2, idx, gates, mod1, sid_move, p["l1_w_gate_up"], p["l1_b_gate_up"], p["l1_w_down"],
                p["l1_b_down"], [a.shape[0] for a in x_parts])


def kernel(x_prompt, x_sample, c_prompt, c_sample, l0_ada_w, l0_ada_b, l0_norm1, l0_w_in, l0_q_norm_a, l0_k_norm_a, l0_sink_a, l0_q_norm_b, l0_k_norm_b, l0_w_out, l0_norm2, l0_w_router, l0_b_router, l0_w_gate_up, l0_b_gate_up, l0_w_down, l0_b_down, l1_ada_w, l1_ada_b, l1_norm1, l1_w_in, l1_q_norm_c, l1_k_norm_c, l1_rpb_c, l1_w_out, l1_norm2, l1_w_router, l1_b_router, l1_w_gate_up, l1_b_gate_up, l1_w_down, l1_b_down):
    p = dict(l0_ada_w=l0_ada_w, l0_ada_b=l0_ada_b, l0_norm1=l0_norm1, l0_w_in=l0_w_in, l0_q_norm_a=l0_q_norm_a,
             l0_k_norm_a=l0_k_norm_a, l0_sink_a=l0_sink_a, l0_q_norm_b=l0_q_norm_b, l0_k_norm_b=l0_k_norm_b,
             l0_w_out=l0_w_out, l0_norm2=l0_norm2, l0_w_router=l0_w_router, l0_b_router=l0_b_router,
             l0_w_gate_up=l0_w_gate_up, l0_b_gate_up=l0_b_gate_up, l0_w_down=l0_w_down, l0_b_down=l0_b_down,
             l1_ada_w=l1_ada_w, l1_ada_b=l1_ada_b, l1_norm1=l1_norm1, l1_w_in=l1_w_in, l1_q_norm_c=l1_q_norm_c,
             l1_k_norm_c=l1_k_norm_c, l1_rpb_c=l1_rpb_c, l1_w_out=l1_w_out, l1_norm2=l1_norm2,
             l1_w_router=l1_w_router, l1_b_router=l1_b_router, l1_w_gate_up=l1_w_gate_up,
             l1_b_gate_up=l1_b_gate_up, l1_w_down=l1_w_down, l1_b_down=l1_b_down)
    nb_p, len_p, d = x_prompt.shape
    nb_s, len_s, _ = x_sample.shape
    nseq = nb_p + nb_s
    c_pad = jnp.concatenate([c_prompt, c_sample, jnp.zeros((-nseq % 8, d), F32)], axis=0)
    mods = tuple(_ada_mod(c_pad, p[l + "_ada_w"], p[l + "_ada_b"]).reshape(c_pad.shape[0], 6, D_MODEL)
                 for l in ("l0", "l1"))
    outs = []
    for xg, row0 in ((x_prompt, 0), (x_sample, nb_p)):
        nb, ln, _ = xg.shape
        (y,) = _trunk([xg.reshape(nb * ln, d)], mods, row0, (ln,) * nb, p)
        outs.append(y.reshape(nb, ln, d))
    return tuple(outs)
```

```python
import functools
import math

import numpy as np
import jax
import jax.numpy as jnp
from jax import lax
from jax.experimental import pallas as pl
from jax.experimental.pallas import tpu as pltpu
from jax.experimental.pallas import tpu_sc as plsc

F32 = jnp.float32
BF16 = jnp.bfloat16
HIGHEST = lax.Precision.HIGHEST

D_MODEL = 1024
HEAD_DIM = 64
LANES = 128
GRID_W = 64
A_Q_HEADS = 8
A_KV_HEADS = 2
A_GROUP = A_Q_HEADS // A_KV_HEADS
A_HALF = 128
B_PAIRS = ((128, 1), (512, 4), (2048, 16))
B_GROUPS = 3
B_HEADS = 4
A_QW = A_Q_HEADS * HEAD_DIM
A_KVW = A_KV_HEADS * HEAD_DIM
A_IN = A_QW + 2 * A_KVW
B_GW = B_HEADS * HEAD_DIM
C_HEADS = 16
C_W = C_HEADS * HEAD_DIM
NA_KH = 8
NA_KW = 16
N_EXPERTS = 32
TOP_K = 4
D_FF = 1024
SWIGLU_LIMIT = 7.0
SWIGLU_ALPHA = 1.702
RMS_EPS = 1e-6
NEG_INF = -1e30
LOG2E = math.log2(math.e)
LN2 = math.log(2.0)
QK_SCALE = HEAD_DIM ** -0.5 * LOG2E

TOKEN_TILE = 1024
ATTN_TQ_A = 1024
ATTN_TQ_B = 512
ATTN_UQ = 128
BAND_RESIDUES_PER_STEP = 4
NA_TROWS = 16
MOE_BM = 512
ROUTE_TILE = 1024
MOVE_TILE = 1024
VMEM_LIMIT = 56 * 1024 * 1024


def _alibi_slopes(n):
    return [float(2.0 ** (-8.0 * (j + 1) / n)) for j in range(n)]


def _cparams(sem):
    return pltpu.CompilerParams(dimension_semantics=sem, vmem_limit_bytes=VMEM_LIMIT)


def _mod_kernel(c_ref, w_ref, b_ref, o_ref):
    c = c_ref[...]
    s = c * jax.nn.sigmoid(c)
    o_ref[...] = jnp.dot(s, w_ref[...], precision=HIGHEST, preferred_element_type=F32) + b_ref[...]


def _ada_mod(c_pad, w, b):
    nrow = c_pad.shape[0]
    ncol = w.shape[1]
    tn = D_MODEL
    return pl.pallas_call(
        _mod_kernel,
        grid=(ncol // tn,),
        in_specs=[pl.BlockSpec((nrow, D_MODEL), lambda j: (0, 0)),
                  pl.BlockSpec((D_MODEL, tn), lambda j: (0, j)),
                  pl.BlockSpec((1, tn), lambda j: (0, j))],
        out_specs=pl.BlockSpec((nrow, tn), lambda j: (0, j)),
        out_shape=jax.ShapeDtypeStruct((nrow, ncol), F32),
        compiler_params=_cparams(("arbitrary",)),
        name="ada_mod",
    )(c_pad, w, b.reshape(1, ncol))


def _head_mean_sq(y, bd_ref):
    w = y.shape[1]
    outs = []
    for c0 in range(0, w, 2 * LANES):
        cw = min(2 * LANES, w - c0)
        sq = y[:, c0:c0 + cw] * y[:, c0:c0 + cw]
        outs.append(jnp.dot(sq.astype(BF16), bd_ref[0:cw, 0:cw], preferred_element_type=F32))
    return outs[0] if len(outs) == 1 else jnp.concatenate(outs, axis=1)


def _rms_mod(x, g_ref, mod_ref, shift_row, scale_row):
    ms = jnp.mean(x * x, axis=-1, keepdims=True)
    xn = x * lax.rsqrt(ms + RMS_EPS) * g_ref[...]
    return xn * (1.0 + mod_ref[scale_row:scale_row + 1, :]) + mod_ref[shift_row:shift_row + 1, :]


def _part_tiles(parts, tile):
    starts, s = [], 0
    for a in parts:
        assert a.shape[0] % tile == 0
        starts.append(s)
        s += a.shape[0] // tile
    return tuple(starts), s


def _part_specs(parts, starts, tile, ncol):
    def spec(a, s0):
        n = a.shape[0] // tile
        return pl.BlockSpec((tile, ncol), lambda i, *_: (jnp.clip(i - s0, 0, n - 1), 0))
    return [spec(a, s0) for a, s0 in zip(parts, starts)]


def _read_part(refs, starts):
    i = pl.program_id(0)
    x = refs[0][...]
    for ref, s0 in zip(refs[1:], starts[1:]):
        x = jnp.where(i >= s0, ref[...], x)
    return x


def _in_kernel(sid_ref, *refs, slabs, tm, part_starts):
    del sid_ref
    npart = len(part_starts)
    x_refs = refs[:npart]
    mod_ref, n1_ref, w_ref, gain_ref, bd_ref = refs[npart:npart + 5]
    rest = refs[npart + 5:]
    nout = len(slabs)
    out_refs = rest[:nout]
    scr_ref = rest[nout] if len(rest) > nout else None
    h = _rms_mod(_read_part(x_refs, part_starts), n1_ref, mod_ref, 0, 1).astype(BF16)
    proj = jnp.dot(h, w_ref[...], preferred_element_type=F32)
    for (c0, w, normed, dil), o_ref in zip(slabs, out_refs):
        y = proj[:, c0:c0 + w]
        if normed:
            y = y * lax.rsqrt(_head_mean_sq(y, bd_ref) + RMS_EPS) * gain_ref[:, c0:c0 + w]
        if dil == 1:
            o_ref[...] = y.astype(BF16)
        else:
            for c in range(w // LANES):
                scr_ref[c] = y[:, c * LANES:(c + 1) * LANES]
            for r in range(dil):
                for c in range(w // LANES):
                    o_ref[:, r * w + c * LANES:r * w + (c + 1) * LANES] = (
                        scr_ref[c, pl.ds(r, tm // dil, stride=dil), :].astype(BF16))


def _in_proj(x_parts, mod, tile_sid, n1, w_bf, gain, bd, slabs):
    tm = TOKEN_TILE
    part_starts, ntiles = _part_tiles(x_parts, tm)
    t = ntiles * tm
    ncols = w_bf.shape[1]
    out_shapes, out_specs = [], []
    for (c0, w, normed, dil) in slabs:
        out_shapes.append(jax.ShapeDtypeStruct((t // dil, dil * w), BF16))
        out_specs.append(pl.BlockSpec((tm // dil, dil * w), lambda i, sid: (i, 0)))
    need_scr = any(s[3] > 1 for s in slabs)
    scratch = [pltpu.VMEM((max(s[1] for s in slabs if s[3] > 1) // LANES, tm, LANES), F32)] if need_scr else []
    gs = pltpu.PrefetchScalarGridSpec(
        num_scalar_prefetch=1,
        grid=(t // tm,),
        in_specs=_part_specs(x_parts, part_starts, tm, D_MODEL) + [
            pl.BlockSpec((None, 6, D_MODEL), lambda i, sid: (sid[i], 0, 0)),
            pl.BlockSpec((1, D_MODEL), lambda i, sid: (0, 0)),
            pl.BlockSpec((D_MODEL, ncols), lambda i, sid: (0, 0)),
            pl.BlockSpec((1, ncols), lambda i, sid: (0, 0)),
            pl.BlockSpec((2 * LANES, 2 * LANES), lambda i, sid: (0, 0))],
        out_specs=out_specs,
        scratch_shapes=scratch)
    return pl.pallas_call(
        functools.partial(_in_kernel, slabs=slabs, tm=tm, part_starts=part_starts),
        grid_spec=gs,
        out_shape=out_shapes,
        compiler_params=_cparams(("arbitrary",)),
        name="in_proj",
    )(tile_sid, *x_parts, mod, n1.reshape(1, D_MODEL), w_bf, gain, bd)


def _band_kernel(s0_ref, s1_ref, *refs, tq, hb, half, heads, kv_group, kvw, rps, has_sink, want_lse):
    pos = 0
    sink_ref = None
    if has_sink:
        sink_ref = refs[0]
        pos = 1
    q_ref, kp_ref, kc_ref, kn_ref, vp_ref, vc_ref, vn_ref = refs[pos:pos + 7]
    o_ref = refs[pos + 7]
    lse_ref = refs[pos + 8] if want_lse else None
    i = pl.program_id(1)
    uq = ATTN_UQ
    wlen = uq + 2 * hb
    kall = jnp.concatenate([kp_ref[...], kc_ref[...], kn_ref[...]], axis=0)
    vall = jnp.concatenate([vp_ref[...], vc_ref[...], vn_ref[...]], axis=0)
    lane = lax.broadcasted_iota(jnp.int32, (uq, LANES), 1)
    lo_lanes = lane < HEAD_DIM
    for sub in range(tq // uq):
        rows = slice(sub * uq, (sub + 1) * uq)
        kwin = kall[sub * uq:sub * uq + wlen]
        vwin = vall[sub * uq:sub * uq + wlen]
        q0 = i * tq + sub * uq
        kpos = q0 - hb + lax.broadcasted_iota(jnp.int32, (wlen, uq), 0)
        qpos = q0 + lax.broadcasted_iota(jnp.int32, (wlen, uq), 1)
        rel = kpos - qpos
        valid = (jnp.abs(rel) <= half) & (kpos >= s0_ref[i]) & (kpos < s1_ref[i])
        distm = jnp.where(valid, jnp.abs(rel).astype(F32), 1e32)
        for rr in range(rps):
            lse_rows = []
            for g, pair in enumerate(heads):
                qc = rr * len(heads) + g
                kc = rr * (kvw // LANES) + kv_group[g]
                qg = q_ref[rows, qc * LANES:(qc + 1) * LANES]
                kg = kwin[:, kc * LANES:(kc + 1) * LANES]
                vg = vwin[:, kc * LANES:(kc + 1) * LANES]
                outs = []
                for s, (slope, sink_idx) in enumerate(pair):
                    keep = lo_lanes if s == 0 else jnp.logical_not(lo_lanes)
                    qm = jnp.where(keep, qg, jnp.zeros_like(qg))
                    sc = lax.dot_general(kg, qm, (((1,), (1,)), ((), ())), preferred_element_type=F32)
                    sc = sc - (slope * LOG2E) * distm
                    m = jnp.max(sc, axis=0, keepdims=True)
                    if has_sink:
                        sink2 = sink_ref[sink_idx] * LOG2E
                        m = jnp.maximum(m, sink2)
                    e = jnp.exp2(sc - m)
                    den = jnp.sum(e, axis=0, keepdims=True)
                    if has_sink:
                        den = den + jnp.exp2(sink2 - m)
                    p = (e * (1.0 / den)).astype(BF16)
                    outs.append(lax.dot_general(p, vg, (((0,), (0,)), ((), ())), preferred_element_type=F32))
                    if want_lse:
                        lse_rows.append((m + jnp.log2(den)) * LN2)
                o_ref[rows, qc * LANES:(qc + 1) * LANES] = (
                    jnp.where(lo_lanes, outs[0], outs[1]).astype(o_ref.dtype))
            if want_lse:
                lse_t = jnp.concatenate(lse_rows + [jnp.zeros((LANES - len(lse_rows), uq), F32)], axis=0)
                lse_ref[rows, rr * LANES:(rr + 1) * LANES] = lse_t.T


def _band_attn(q, k, v, seq_rows, *, tq, dil, qw, kvw, half, heads, kv_group, sink, out_dtype, want_lse):
    rows = q.shape[0]
    hb = half
    s0 = _tile_table(seq_rows, tq, lambda sid, s, n, j: s)
    s1 = _tile_table(seq_rows, tq, lambda sid, s, n, j: s + n)
    per = tq // hb
    nt = rows // tq
    nhb = rows // hb
    has_sink = sink is not None
    in_specs = []
    args = []
    if has_sink:
        in_specs.append(pl.BlockSpec(memory_space=pltpu.SMEM))
        args.append(sink)
    qmap = lambda r, i, a, b: (i, r)
    pmap = lambda r, i, a, b: (jnp.maximum(i * per - 1, 0), r)
    nmap = lambda r, i, a, b: (jnp.minimum((i + 1) * per, nhb - 1), r)
    rps = min(dil, BAND_RESIDUES_PER_STEP)
    in_specs += [pl.BlockSpec((tq, rps * qw), qmap),
                 pl.BlockSpec((hb, rps * kvw), pmap), pl.BlockSpec((tq, rps * kvw), qmap),
                 pl.BlockSpec((hb, rps * kvw), nmap),
                 pl.BlockSpec((hb, rps * kvw), pmap), pl.BlockSpec((tq, rps * kvw), qmap),
                 pl.BlockSpec((hb, rps * kvw), nmap)]
    args += [q, k, k, k, v, v, v]
    out_shape = [jax.ShapeDtypeStruct((rows, dil * qw), out_dtype)]
    out_specs = [pl.BlockSpec((tq, rps * qw), qmap)]
    if want_lse:
        out_shape.append(jax.ShapeDtypeStruct((rows, dil * LANES), F32))
        out_specs.append(pl.BlockSpec((tq, rps * LANES), qmap))
    gs = pltpu.PrefetchScalarGridSpec(num_scalar_prefetch=2, grid=(dil // rps, nt),
                                      in_specs=in_specs, out_specs=out_specs)
    res = pl.pallas_call(
        functools.partial(_band_kernel, tq=tq, hb=hb, half=half, heads=heads, kv_group=kv_group, kvw=kvw,
                          rps=rps, has_sink=has_sink, want_lse=want_lse),
        grid_spec=gs,
        out_shape=out_shape,
        compiler_params=_cparams(("arbitrary", "arbitrary")),
        name="band_attn_d%d" % dil,
    )(s0, s1, *args)
    return res


def _na_kernel(r0_ref, nr_ref, q_ref, kp_ref, kc_ref, kn_ref, vp_ref, vc_ref, vn_ref, bias_ref,
               o_ref, kcat, vcat, *, halo):
    j = pl.program_id(0)
    tq = NA_TROWS * GRID_W
    hrows = halo * GRID_W
    kcat[0:hrows, :] = kp_ref[...]
    kcat[hrows:hrows + tq, :] = kc_ref[...]
    kcat[hrows + tq:, :] = kn_ref[...]
    vcat[0:hrows, :] = vp_ref[...]
    vcat[hrows:hrows + tq, :] = vc_ref[...]
    vcat[hrows + tq:, :] = vn_ref[...]
    r0 = r0_ref[j]
    nrows = nr_ref[j]
    kwin_len = NA_KH * GRID_W
    lane = lax.broadcasted_iota(jnp.int32, (GRID_W, LANES), 1)
    lo_lanes = lane < HEAD_DIM
    nrel = 2 * NA_KH - 1

    def row_body(u, carry):
        r = r0 + u
        rs = jnp.clip(r - NA_KH // 2, 0, nrows - NA_KH)
        off = pl.multiple_of((rs - r0 + halo) * GRID_W, GRID_W)
        bvar = rs - r + NA_KH - 1
        qrow = pl.multiple_of(u * GRID_W, GRID_W)
        for g in range(C_HEADS // 2):
            qg = q_ref[pl.ds(qrow, GRID_W), g * LANES:(g + 1) * LANES]
            qs = jnp.concatenate([jnp.where(lo_lanes, qg, jnp.zeros_like(qg)),
                                  jnp.where(lo_lanes, jnp.zeros_like(qg), qg)], axis=0)
            kg = kcat[pl.ds(off, kwin_len), g * LANES:(g + 1) * LANES]
            vg = vcat[pl.ds(off, kwin_len), g * LANES:(g + 1) * LANES]
            sc = lax.dot_general(kg, qs, (((1,), (1,)), ((), ())), preferred_element_type=F32)
            sc = sc + jnp.concatenate([bias_ref[g * nrel + bvar + kr] for kr in range(NA_KH)], axis=0)
            m = jnp.max(sc, axis=0, keepdims=True)
            e = jnp.exp2(sc - m)
            den = jnp.sum(e, axis=0, keepdims=True)
            p = (e * (1.0 / den)).astype(BF16)
            pv = lax.dot_general(p, vg, (((0,), (0,)), ((), ())), preferred_element_type=F32)
            o = jnp.where(lo_lanes, pv[0:GRID_W], pv[GRID_W:])
            o_ref[pl.ds(qrow, GRID_W), g * LANES:(g + 1) * LANES] = o.astype(o_ref.dtype)
        return carry

    lax.fori_loop(0, NA_TROWS, row_body, 0, unroll=True)


def _na_attn(q, k, v, bias_tab, tile_r0, tile_nr):
    t = q.shape[0]
    halo = NA_KH // 2
    tq = NA_TROWS * GRID_W
    hrows = halo * GRID_W
    per = tq // hrows
    nhb = t // hrows
    nt = t // tq
    qmap = lambda j, a, b: (j, 0)
    pmap = lambda j, a, b: (jnp.maximum(j * per - 1, 0), 0)
    nmap = lambda j, a, b: (jnp.minimum((j + 1) * per, nhb - 1), 0)
    gs = pltpu.PrefetchScalarGridSpec(
        num_scalar_prefetch=2, grid=(nt,),
        in_specs=[pl.BlockSpec((tq, C_W), qmap),
                  pl.BlockSpec((hrows, C_W), pmap), pl.BlockSpec((tq, C_W), qmap), pl.BlockSpec((hrows, C_W), nmap),
                  pl.BlockSpec((hrows, C_W), pmap), pl.BlockSpec((tq, C_W), qmap), pl.BlockSpec((hrows, C_W), nmap),
                  pl.BlockSpec(bias_tab.shape, lambda j, a, b: (0, 0, 0))],
        out_specs=pl.BlockSpec((tq, C_W), qmap),
        scratch_shapes=[pltpu.VMEM((tq + 2 * hrows, C_W), BF16), pltpu.VMEM((tq + 2 * hrows, C_W), BF16)])
    return pl.pallas_call(
        functools.partial(_na_kernel, halo=halo),
        grid_spec=gs,
        out_shape=jax.ShapeDtypeStruct((t, C_W), BF16),
        compiler_params=_cparams(("arbitrary",)),
        name="na_attn",
    )(tile_r0, tile_nr, q, k, k, k, v, v, v, bias_tab)


def _na_bias_table(rpb):
    c = np.arange(GRID_W)
    cs = np.clip(c - NA_KW // 2, 0, GRID_W - NA_KW)
    kc = np.arange(GRID_W)
    valid = (kc[:, None] >= cs[None, :]) & (kc[:, None] < cs[None, :] + NA_KW)
    span = GRID_W - NA_KW
    nrel = 2 * NA_KH - 1
    rpb2 = rpb.astype(F32).reshape(C_HEADS // 2, 2, nrel, 2 * NA_KW - 1) * LOG2E
    rev = jnp.flip(jnp.pad(rpb2, ((0, 0), (0, 0), (0, 0), (span, span)), mode="edge"), axis=-1)
    row = jnp.pad(rev, ((0, 0), (0, 0), (0, 0), (0, 1)))
    skew = jnp.broadcast_to(row[..., None, :], row.shape[:-1] + (GRID_W, 2 * GRID_W))
    skew = skew.reshape(row.shape[:-1] + (2 * GRID_W * GRID_W,))[..., :GRID_W * (2 * GRID_W - 1)]
    tab = skew.reshape(row.shape[:-1] + (GRID_W, 2 * GRID_W - 1))[..., GRID_W - 1:]
    tab = jnp.where(jnp.asarray(valid), tab, NEG_INF)
    pairs = jnp.concatenate([tab[:, 0], tab[:, 1]], axis=-1)
    return pairs.reshape((C_HEADS // 2) * (2 * NA_KH - 1), GRID_W, LANES)


def _route(h2, wr_ref, br_ref, idx_ref, gate_ref):
    tm = h2.shape[0]
    h_hi = h2.astype(BF16)
    h_lo = (h2 - h_hi.astype(F32)).astype(BF16)
    w = wr_ref[...]
    w_hi = w.astype(BF16)
    w_lo = (w - w_hi.astype(F32)).astype(BF16)
    nt = (((1,), (1,)), ((), ()))
    logits = (lax.dot_general(w_hi, h_hi, nt, preferred_element_type=F32)
              + lax.dot_general(w_lo, h_hi, nt, preferred_element_type=F32)
              + lax.dot_general(w_hi, h_lo, nt, preferred_element_type=F32)) + br_ref[:, 0:1]
    eid = lax.broadcasted_iota(jnp.int32, (N_EXPERTS, tm), 0)
    vals, idxs = [], []
    for _ in range(TOP_K):
        m = jnp.max(logits, axis=0, keepdims=True)
        ix = jnp.min(jnp.where(logits == m, eid, N_EXPERTS), axis=0, keepdims=True)
        vals.append(m)
        idxs.append(ix)
        logits = jnp.where(eid == ix, -jnp.inf, logits)
    es = [jnp.exp(vk - vals[0]) for vk in vals]
    den = es[0] + es[1] + es[2] + es[3]
    pad_i = jnp.zeros((8 - TOP_K, tm), jnp.int32)
    pad_f = jnp.zeros((8 - TOP_K, tm), F32)
    idx_ref[...] = jnp.concatenate(idxs + [pad_i], axis=0)
    gate_ref[...] = jnp.concatenate([ek / den for ek in es] + [pad_f], axis=0)


PACK_W = D_MODEL // 2


def _pack_bf16_rows(h):
    bits = pltpu.bitcast(h.astype(BF16).astype(F32), jnp.int32)
    return (bits[:, :PACK_W] & jnp.int32(-65536)) | lax.shift_right_logical(bits[:, PACK_W:], 16)


def _unpack_rows_f32(w):
    hi = pltpu.bitcast(w & jnp.int32(-65536), F32)
    lo = pltpu.bitcast(lax.shift_left(w, 16), F32)
    return jnp.concatenate([hi, lo], axis=1)


def _unpack_bf16_rows(w):
    return _unpack_rows_f32(w).astype(BF16)


def _out0_kernel(sid_ref, *refs, tm, part_starts):
    del sid_ref
    npart = len(part_starts)
    x_refs = refs[:npart]
    (mod_ref, n2_ref, oa_ref, ob0_ref, ls0_ref, ob1_ref, ls1_ref, ob2_ref, ls2_ref,
     wout_ref, wr_ref, br_ref, x1_ref, h2_ref, idx_ref, gate_ref, oscr, lscr) = refs[npart:]
    os_, ls_ = [], []
    for gi, (o_ref, l_ref) in enumerate(((ob0_ref, ls0_ref), (ob1_ref, ls1_ref), (ob2_ref, ls2_ref))):
        dil = B_PAIRS[gi][1]
        if dil == 1:
            os_.append(o_ref[...])
            ls_.append(l_ref[...])
        else:
            ncg = B_GW // LANES
            for r in range(dil):
                for c in range(ncg):
                    lo = r * B_GW + c * LANES
                    oscr[gi * ncg + c, pl.ds(r, tm // dil, stride=dil), :] = o_ref[:, lo:lo + LANES]
                lscr[gi, pl.ds(r, tm // dil, stride=dil), :] = l_ref[:, r * LANES:(r + 1) * LANES]
            os_.append(jnp.concatenate([oscr[gi * ncg + c] for c in range(ncg)], axis=1))
            ls_.append(lscr[gi])
    lmax = jnp.maximum(jnp.maximum(ls_[0], ls_[1]), ls_[2])
    ws = [jnp.exp(l - lmax) for l in ls_]
    winv = 1.0 / (ws[0] + ws[1] + ws[2])
    lane = lax.broadcasted_iota(jnp.int32, (tm, LANES), 1)
    lo_lanes = lane < HEAD_DIM

    def head_weights(w):
        return jnp.concatenate(
            [jnp.where(lo_lanes, jnp.broadcast_to(w[:, 2 * pr:2 * pr + 1], (tm, LANES)),
                       jnp.broadcast_to(w[:, 2 * pr + 1:2 * pr + 2], (tm, LANES)))
             for pr in range(B_HEADS // 2)], axis=1)

    ob = (head_weights(ws[0] * winv) * os_[0] + head_weights(ws[1] * winv) * os_[1]
          + head_weights(ws[2] * winv) * os_[2])
    o = (jnp.dot(oa_ref[...], wout_ref[0:A_QW, :], preferred_element_type=F32)
         + jnp.dot(ob.astype(BF16), wout_ref[A_QW:, :], preferred_element_type=F32))
    x1 = _read_part(x_refs, part_starts) + mod_ref[2:3, :] * o
    x1_ref[...] = x1
    h2 = _rms_mod(x1, n2_ref, mod_ref, 3, 4)
    h2_ref[...] = _pack_bf16_rows(h2)
    _route(h2, wr_ref, br_ref, idx_ref, gate_ref)


def _out1_kernel(sid_ref, *refs, part_starts):
    del sid_ref
    npart = len(part_starts)
    x_refs = refs[:npart]
    mod_ref, n2_ref, oc_ref, wout_ref, wr_ref, br_ref, x1_ref, h2_ref, idx_ref, gate_ref = refs[npart:]
    o = jnp.dot(oc_ref[...], wout_ref[...], preferred_element_type=F32)
    x1 = _read_part(x_refs, part_starts) + mod_ref[2:3, :] * o
    x1_ref[...] = x1
    h2 = _rms_mod(x1, n2_ref, mod_ref, 3, 4)
    h2_ref[...] = _pack_bf16_rows(h2)
    _route(h2, wr_ref, br_ref, idx_ref, gate_ref)


def _out_proj(layer, x_parts, mod, tile_sid, n2, attn, wout_bf, wr_t, br):
    tm = TOKEN_TILE
    part_starts, ntiles = _part_tiles(x_parts, tm)
    t = ntiles * tm
    row = lambda i, sid: (i, 0)
    const = lambda i, sid: (0, 0)
    in_specs = _part_specs(x_parts, part_starts, tm, D_MODEL) + [
        pl.BlockSpec((None, 6, D_MODEL), lambda i, sid: (sid[i], 0, 0)),
        pl.BlockSpec((1, D_MODEL), const)]
    scratch = []
    if layer == 0:
        in_specs.append(pl.BlockSpec((tm, A_QW), row))
        for (_, dil) in B_PAIRS:
            in_specs += [pl.BlockSpec((tm // dil, dil * B_GW), row), pl.BlockSpec((tm // dil, dil * LANES), row)]
        body = functools.partial(_out0_kernel, tm=tm, part_starts=part_starts)
        nscr = B_GROUPS * B_GW // LANES
        scratch = [pltpu.VMEM((nscr, tm, LANES), F32), pltpu.VMEM((B_GROUPS, tm, LANES), F32)]
    else:
        in_specs.append(pl.BlockSpec((tm, C_W), row))
        body = functools.partial(_out1_kernel, part_starts=part_starts)
    in_specs += [pl.BlockSpec(wout_bf.shape, const),
                 pl.BlockSpec((N_EXPERTS, D_MODEL), const),
                 pl.BlockSpec((N_EXPERTS, LANES), const)]
    gs = pltpu.PrefetchScalarGridSpec(
        num_scalar_prefetch=1, grid=(t // tm,), in_specs=in_specs,
        out_specs=[pl.BlockSpec((tm, D_MODEL), row), pl.BlockSpec((tm, PACK_W), row),
                   pl.BlockSpec((8, tm), lambda i, sid: (0, i)), pl.BlockSpec((8, tm), lambda i, sid: (0, i))],
        scratch_shapes=scratch)
    return pl.pallas_call(
        body, grid_spec=gs,
        out_shape=[jax.ShapeDtypeStruct((t, D_MODEL), F32), jax.ShapeDtypeStruct((t, PACK_W), jnp.int32),
                   jax.ShapeDtypeStruct((8, t), jnp.int32), jax.ShapeDtypeStruct((8, t), F32)],
        compiler_params=_cparams(("arbitrary",)),
        name="out_proj%d" % layer,
    )(tile_sid, *x_parts, mod, n2.reshape(1, D_MODEL), *attn, wout_bf, wr_t, br)


def _rank_kernel(idx_ref, tri_ref, rank_ref, cnt_ref, carry):
    i = pl.program_id(0)

    @pl.when(i == 0)
    def _():
        carry[...] = jnp.zeros_like(carry)

    tk = idx_ref.shape[1]
    eid = lax.broadcasted_iota(jnp.int32, (N_EXPERTS, tk), 0)
    hits = [eid == idx_ref[k:k + 1, :] for k in range(TOP_K)]
    onehot = sum(h.astype(F32) for h in hits)
    incl = jnp.dot(onehot.astype(BF16), tri_ref[...], preferred_element_type=F32)
    before = incl - onehot + carry[:, 0:1]
    rows = [jnp.sum(jnp.where(h, before, 0.0), axis=0, keepdims=True) for h in hits]
    rows.append(jnp.zeros((8 - TOP_K, tk), F32))
    rank_ref[...] = jnp.concatenate(rows, axis=0).astype(jnp.int32)
    carry[...] = carry[...] + incl[:, tk - 1:tk]
    cnt_ref[...] = carry[...].astype(jnp.int32)


def _slot_kernel(idx_ref, rank_ref, start_ref, slot_ref):
    tk = idx_ref.shape[1]
    eid = lax.broadcasted_iota(jnp.int32, (N_EXPERTS, tk), 0)
    start = start_ref[:, 0:1]
    rows = []
    for k in range(TOP_K):
        base = jnp.sum(jnp.where(eid == idx_ref[k:k + 1, :], start, 0), axis=0, keepdims=True)
        rows.append(base + rank_ref[k:k + 1, :])
    rows.append(jnp.zeros((8 - TOP_K, tk), jnp.int32))
    slot_ref[...] = jnp.concatenate(rows, axis=0)


def _route_slots(idx):
    t = idx.shape[1]
    tk = ROUTE_TILE
    tri = jnp.asarray(np.triu(np.ones((tk, tk), np.float32)), BF16)
    tok = lambda i: (0, i)
    rank, cnt = pl.pallas_call(
        _rank_kernel, grid=(t // tk,),
        in_specs=[pl.BlockSpec((8, tk), tok), pl.BlockSpec((tk, tk), lambda i: (0, 0))],
        out_specs=[pl.BlockSpec((8, tk), tok), pl.BlockSpec((N_EXPERTS, LANES), lambda i: (0, 0))],
        out_shape=[jax.ShapeDtypeStruct((8, t), jnp.int32), jax.ShapeDtypeStruct((N_EXPERTS, LANES), jnp.int32)],
        scratch_shapes=[pltpu.VMEM((N_EXPERTS, LANES), F32)],
        compiler_params=_cparams(("arbitrary",)),
        name="route_rank",
    )(idx, tri)
    counts = cnt[:, 0]
    padded = (counts + MOE_BM - 1) // MOE_BM * MOE_BM
    pad_end = jnp.cumsum(padded)
    pad_start = pad_end - padded
    nblk = (t * TOP_K) // MOE_BM + N_EXPERTS
    blk_row0 = jnp.arange(nblk, dtype=jnp.int32) * MOE_BM
    blk_expert = jnp.minimum(jnp.sum((pad_end[None, :] <= blk_row0[:, None]).astype(jnp.int32), axis=1),
                             N_EXPERTS - 1)
    nused = (pad_end[-1] // MOE_BM).astype(jnp.int32).reshape(1)
    eids = jnp.arange(N_EXPERTS, dtype=jnp.int32)
    later = (eids[None, :] > eids[:, None]) & (padded[None, :] > 0)
    next_expert = jnp.min(jnp.where(later, eids[None, :], N_EXPERTS), axis=1)
    next_expert = jnp.where(next_expert == N_EXPERTS, -1, next_expert).astype(jnp.int32)
    blk_next = jnp.sum(jnp.where(blk_expert[:, None] == eids[None, :], next_expert[None, :], 0), axis=1)
    start_b = jnp.broadcast_to(pad_start.astype(jnp.int32)[:, None], (N_EXPERTS, LANES))
    slot = pl.pallas_call(
        _slot_kernel, grid=(t // tk,),
        in_specs=[pl.BlockSpec((8, tk), tok), pl.BlockSpec((8, tk), tok),
                  pl.BlockSpec((N_EXPERTS, LANES), lambda i: (0, 0))],
        out_specs=pl.BlockSpec((8, tk), tok),
        out_shape=jax.ShapeDtypeStruct((8, t), jnp.int32),
        compiler_params=_cparams(("arbitrary",)),
        name="route_slot",
    )(idx, rank, start_b)
    return slot, blk_expert, nused, blk_next


SC_CORES = 2
SC_SUBCORES = 16
SC_ROW_BUFFER_BYTES = 256 * 1024
SC_MAX_INDEX_LIST = 128


def _sc_chunks(t, row_bytes):
    chunk = min(SC_MAX_INDEX_LIST, SC_ROW_BUFFER_BYTES // row_bytes)
    nchunk = t // chunk
    per = nchunk // (SC_CORES * SC_SUBCORES)
    assert per * SC_CORES * SC_SUBCORES * chunk == t
    return chunk, nchunk, per


def _sc_slot_lists(slot, nchunk, chunk):
    return slot[:TOP_K].reshape(TOP_K, nchunk, chunk).transpose(1, 0, 2)


def _sc_dispatch(h2, slot, nslots):
    t, w = h2.shape
    chunk, nchunk, per = _sc_chunks(t, w * h2.dtype.itemsize)
    mesh = plsc.VectorSubcoreMesh(core_axis_name="c", subcore_axis_name="s")

    @functools.partial(
        pl.kernel, mesh=mesh, out_type=jax.ShapeDtypeStruct((nslots, w), h2.dtype),
        scratch_types=[pltpu.VMEM((TOP_K, chunk), jnp.int32), pltpu.VMEM((chunk, w), h2.dtype)],
        name="moe_dispatch_sc")
    def body(h_hbm, slot_hbm, xs_hbm, idx_v, rows_v):
        wid = lax.axis_index("s") * SC_CORES + lax.axis_index("c")

        @pl.loop(0, per)
        def _(j):
            c = wid * per + j
            pltpu.sync_copy(slot_hbm.at[c], idx_v)
            pltpu.sync_copy(h_hbm.at[pl.ds(c * chunk, chunk)], rows_v)
            for k in range(TOP_K):
                pltpu.sync_copy(rows_v, xs_hbm.at[idx_v.at[k]])

    return body(h2, _sc_slot_lists(slot, nchunk, chunk))


def _sc_gather(ys, slot, t):
    w = ys.shape[1]
    nbuf = 2
    chunk, nchunk, per = _sc_chunks(t, nbuf * w * ys.dtype.itemsize)
    mesh = plsc.VectorSubcoreMesh(core_axis_name="c", subcore_axis_name="s")

    @functools.partial(
        pl.kernel, mesh=mesh, out_type=jax.ShapeDtypeStruct((TOP_K, t, w), ys.dtype),
        scratch_types=[pltpu.VMEM((TOP_K, chunk), jnp.int32), pltpu.VMEM((nbuf, chunk, w), ys.dtype),
                       pltpu.SemaphoreType.DMA((nbuf,))],
        name="moe_gather_sc")
    def body(ys_hbm, slot_hbm, out_hbm, idx_v, rows_v, wsem):
        wid = lax.axis_index("s") * SC_CORES + lax.axis_index("c")

        @pl.loop(0, per)
        def _(j):
            c = wid * per + j
            pltpu.sync_copy(slot_hbm.at[c], idx_v)
            writes = []
            for k in range(TOP_K):
                b = k % nbuf
                if k >= nbuf:
                    writes[k - nbuf].wait()
                pltpu.sync_copy(ys_hbm.at[idx_v.at[k]], rows_v.at[b])
                writes.append(pltpu.async_copy(rows_v.at[b], out_hbm.at[k, pl.ds(c * chunk, chunk)], wsem.at[b]))
            for wr in writes[-nbuf:]:
                wr.wait()

    return body(ys, _sc_slot_lists(slot, nchunk, chunk))


def _ffn_kernel(be_ref, nu_ref, nxt_ref, xs_ref, wgu_hbm, bgu_ref, wd_hbm, bd_ref, y_ref,
                wgu_f, wd_f, wgu_bf, wd_bf, wsem):
    i = pl.program_id(0)
    prev = be_ref[jnp.maximum(i - 1, 0)]

    def weight_copies(e):
        return (pltpu.make_async_copy(wgu_hbm.at[e], wgu_f, wsem.at[0]),
                pltpu.make_async_copy(wd_hbm.at[e], wd_f, wsem.at[1]))

    @pl.when(i == 0)
    def _():
        for cp in weight_copies(be_ref[0]):
            cp.start()

    @pl.when(i < nu_ref[0])
    def _():
        @pl.when((i == 0) | (be_ref[i] != prev))
        def _():
            for cp in weight_copies(be_ref[i]):
                cp.wait()
            wgu_bf[...] = wgu_f[...].astype(BF16)
            wd_bf[...] = wd_f[...].astype(BF16)

            @pl.when(nxt_ref[i] >= 0)
            def _():
                for cp in weight_copies(nxt_ref[i]):
                    cp.start()

        x = _unpack_bf16_rows(xs_ref[...])
        gu = jnp.dot(x, wgu_bf[...], preferred_element_type=F32) + bgu_ref[...]
        gate = jnp.minimum(gu[:, :D_FF], SWIGLU_LIMIT)
        up = jnp.clip(gu[:, D_FF:], -SWIGLU_LIMIT, SWIGLU_LIMIT)
        act = (up + 1.0) * gate * jax.nn.sigmoid(SWIGLU_ALPHA * gate)
        y = jnp.dot(act.astype(BF16), wd_bf[...], preferred_element_type=F32) + bd_ref[...]
        y_ref[...] = _pack_bf16_rows(y)

    @pl.when(i >= nu_ref[0])
    def _():
        y_ref[...] = jnp.zeros_like(y_ref)


def _expert_ffn(xs, blk_expert, nused, blk_next, wgu, bgu, wd, bd):
    nslots = xs.shape[0]
    nblk = nslots // MOE_BM
    blk = lambda i, be, nu, nx: (jnp.minimum(i, nu[0] - 1), 0)
    exp3 = lambda i, be, nu, nx: (be[jnp.minimum(i, nu[0] - 1)], 0, 0)
    gs = pltpu.PrefetchScalarGridSpec(
        num_scalar_prefetch=3, grid=(nblk,),
        in_specs=[pl.BlockSpec((MOE_BM, PACK_W), blk),
                  pl.BlockSpec(memory_space=pl.ANY),
                  pl.BlockSpec((None, 1, 2 * D_FF), exp3),
                  pl.BlockSpec(memory_space=pl.ANY),
                  pl.BlockSpec((None, 1, D_MODEL), exp3)],
        out_specs=pl.BlockSpec((MOE_BM, PACK_W), lambda i, be, nu, nx: (i, 0)),
        scratch_shapes=[pltpu.VMEM((D_MODEL, 2 * D_FF), F32), pltpu.VMEM((D_FF, D_MODEL), F32),
                        pltpu.VMEM((D_MODEL, 2 * D_FF), BF16), pltpu.VMEM((D_FF, D_MODEL), BF16),
                        pltpu.SemaphoreType.DMA((2,))])
    return pl.pallas_call(
        _ffn_kernel, grid_spec=gs,
        out_shape=jax.ShapeDtypeStruct((nslots, PACK_W), jnp.int32),
        compiler_params=_cparams(("arbitrary",)),
        name="moe_ffn",
    )(blk_expert, nused, blk_next, xs, wgu, bgu.reshape(N_EXPERTS, 1, 2 * D_FF), wd,
      bd.reshape(N_EXPERTS, 1, D_MODEL))


def _combine_kernel(sid_ref, x_ref, mod_ref, gate_ref, yk_ref, *o_refs, tm, out_starts):
    del sid_ref
    gpad = jnp.concatenate([gate_ref[...], jnp.zeros((LANES - 8, tm), F32)], axis=0)
    gcol = gpad.T
    acc = gcol[:, 0:1] * _unpack_rows_f32(yk_ref[0])
    for k in range(1, TOP_K):
        acc = acc + gcol[:, k:k + 1] * _unpack_rows_f32(yk_ref[k])
    out = x_ref[...] + mod_ref[5:6, :] * acc
    i = pl.program_id(0)
    ends = out_starts[1:] + (None,)
    for o_ref, s0, s1 in zip(o_refs, out_starts, ends):
        mine = (i >= s0) if s1 is None else ((i >= s0) & (i < s1))

        @pl.when(mine)
        def _():
            o_ref[...] = out


def _combine(x1, mod, tile_sid, gates, yk, out_rows):
    t = x1.shape[0]
    tm = MOVE_TILE
    out_shape = [jax.ShapeDtypeStruct((n, D_MODEL), F32) for n in out_rows]
    out_starts, ntiles = _part_tiles(out_shape, tm)
    assert ntiles * tm == t
    gs = pltpu.PrefetchScalarGridSpec(
        num_scalar_prefetch=1, grid=(t // tm,),
        in_specs=[pl.BlockSpec((tm, D_MODEL), lambda i, sid: (i, 0)),
                  pl.BlockSpec((None, 6, D_MODEL), lambda i, sid: (sid[i], 0, 0)),
                  pl.BlockSpec((8, tm), lambda i, sid: (0, i)),
                  pl.BlockSpec((TOP_K, tm, PACK_W), lambda i, sid: (0, i, 0))],
        out_specs=_part_specs(out_shape, out_starts, tm, D_MODEL))
    return pl.pallas_call(
        functools.partial(_combine_kernel, tm=tm, out_starts=out_starts), grid_spec=gs,
        out_shape=out_shape,
        compiler_params=_cparams(("arbitrary",)),
        name="moe_combine",
    )(tile_sid, x1, mod, gates, yk)


def _moe(x1, h2, idx, gates, mod, tile_sid_move, wgu, bgu, wd, bd, out_rows):
    t = x1.shape[0]
    nslots = t * TOP_K + N_EXPERTS * MOE_BM
    slot, blk_expert, nused, blk_next = _route_slots(idx)
    xs = _sc_dispatch(h2, slot, nslots)
    ys = _expert_ffn(xs, blk_expert, nused, blk_next, wgu, bgu, wd, bd)
    return _combine(x1, mod, tile_sid_move, gates, _sc_gather(ys, slot, t), out_rows)


def _tile_table(seq_lens, tile, fn):
    vals = []
    start = 0
    for sid, n in enumerate(seq_lens):
        assert n % tile == 0
        for j in range(n // tile):
            vals.append(fn(sid, start, n, j))
        start += n
    return jnp.asarray(np.asarray(vals, np.int32))


def _head_block_diag():
    bd = np.kron(np.eye(2 * LANES // HEAD_DIM, dtype=np.float32),
                 np.full((HEAD_DIM, HEAD_DIM), 1.0 / HEAD_DIM, np.float32))
    return jnp.asarray(bd, BF16)


_A_HEAD_ORDER = [kv * A_GROUP + g for g in range(A_GROUP) for kv in range(A_KV_HEADS)]


def _trunk(x_parts, mods, mod_row0, seq_lens, p):
    t = sum(a.shape[0] for a in x_parts)
    sid_tok = _tile_table(seq_lens, TOKEN_TILE, lambda sid, s, n, j: mod_row0 + sid)
    sid_move = _tile_table(seq_lens, MOVE_TILE, lambda sid, s, n, j: mod_row0 + sid)
    bd = _head_block_diag()
    mod0, mod1 = mods

    w_in = p["l0_w_in"]
    a_cols = np.concatenate([np.arange(h * HEAD_DIM, (h + 1) * HEAD_DIM) for h in _A_HEAD_ORDER])
    bq, bk, bv = A_IN, A_IN + B_GROUPS * B_GW, A_IN + 2 * B_GROUPS * B_GW
    col_blocks = [w_in[:, a_cols], w_in[:, A_QW:A_IN]]
    for g in range(B_GROUPS):
        col_blocks += [w_in[:, bq + g * B_GW:bq + (g + 1) * B_GW],
                       w_in[:, bk + g * B_GW:bk + (g + 1) * B_GW],
                       w_in[:, bv + g * B_GW:bv + (g + 1) * B_GW]]
    w0 = jnp.concatenate(col_blocks, axis=1).astype(BF16)
    ones = lambda n: jnp.ones((n,), F32)
    gain_blocks = [jnp.tile(p["l0_q_norm_a"], A_Q_HEADS) * QK_SCALE, jnp.tile(p["l0_k_norm_a"], A_KV_HEADS),
                   ones(A_KVW)]
    for g in range(B_GROUPS):
        gain_blocks += [jnp.tile(p["l0_q_norm_b"], B_HEADS) * QK_SCALE, jnp.tile(p["l0_k_norm_b"], B_HEADS),
                        ones(B_GW)]
    gain0 = jnp.concatenate(gain_blocks).reshape(1, -1)
    slabs0 = [(0, A_QW, True, 1), (A_QW, A_KVW, True, 1), (A_QW + A_KVW, A_KVW, False, 1)]
    c0 = A_IN
    for g, (_, dil) in enumerate(B_PAIRS):
        slabs0 += [(c0, B_GW, True, dil), (c0 + B_GW, B_GW, True, dil), (c0 + 2 * B_GW, B_GW, False, dil)]
        c0 += 3 * B_GW
    proj = _in_proj(x_parts, mod0, sid_tok, p["l0_norm1"], w0, gain0, bd, tuple(slabs0))
    qa, ka, va = proj[0:3]

    slopes_a = _alibi_slopes(A_Q_HEADS)
    heads_a = tuple(tuple((slopes_a[kv * A_GROUP + g], kv * A_GROUP + g) for kv in range(A_KV_HEADS))
                    for g in range(A_GROUP))
    (oa,) = _band_attn(qa, ka, va, seq_lens, tq=ATTN_TQ_A, dil=1, qw=A_QW, kvw=A_KVW, half=A_HALF, heads=heads_a,
                       kv_group=(0,) * A_GROUP, sink=p["l0_sink_a"].astype(F32), out_dtype=BF16, want_lse=False)

    slopes_b = _alibi_slopes(B_GROUPS * B_HEADS)
    attn0 = [oa]
    for g, (window, dil) in enumerate(B_PAIRS):
        qg, kg, vg = proj[3 + 3 * g:6 + 3 * g]
        heads_b = tuple(tuple((slopes_b[g * B_HEADS + 2 * pr + s] * dil, 0) for s in range(2))
                        for pr in range(B_HEADS // 2))
        tq_b = ATTN_TQ_B * BAND_RESIDUES_PER_STEP // min(dil, BAND_RESIDUES_PER_STEP)
        og, lg = _band_attn(qg, kg, vg, [n // dil for n in seq_lens], tq=tq_b, dil=dil, qw=B_GW, kvw=B_GW, half=window // (2 * dil),
                            heads=heads_b, kv_group=tuple(range(B_HEADS // 2)), sink=None, out_dtype=F32,
                            want_lse=True)
        attn0 += [og, lg]

    w_out = p["l0_w_out"]
    wout0 = jnp.concatenate([w_out[a_cols], w_out[A_QW:]], axis=0).astype(BF16)
    br0 = jnp.broadcast_to(p["l0_b_router"].astype(F32)[:, None], (N_EXPERTS, LANES))
    x1, h2, idx, gates = _out_proj(0, x_parts, mod0, sid_tok, p["l0_norm2"], attn0, wout0, p["l0_w_router"].T,
                                   br0)
    x = _moe(x1, h2, idx, gates, mod0, sid_move, p["l0_w_gate_up"], p["l0_b_gate_up"], p["l0_w_down"],
             p["l0_b_down"], [t])

    w1 = p["l1_w_in"].astype(BF16)
    gain1 = jnp.concatenate([jnp.tile(p["l1_q_norm_c"], C_HEADS) * QK_SCALE, jnp.tile(p["l1_k_norm_c"], C_HEADS),
                             ones(C_W)]).reshape(1, -1)
    slabs1 = ((0, C_W, True, 1), (C_W, C_W, True, 1), (2 * C_W, C_W, False, 1))
    qc, kc, vc = _in_proj(x, mod1, sid_tok, p["l1_norm1"], w1, gain1, bd, slabs1)
    na_tile = NA_TROWS * GRID_W
    tile_r0 = _tile_table(seq_lens, na_tile, lambda sid, s, n, j: j * NA_TROWS)
    tile_nr = _tile_table(seq_lens, na_tile, lambda sid, s, n, j: n // GRID_W)
    oc = _na_attn(qc, kc, vc, _na_bias_table(p["l1_rpb_c"]), tile_r0, tile_nr)
    br1 = jnp.broadcast_to(p["l1_b_router"].astype(F32)[:, None], (N_EXPERTS, LANES))
    x1, h2, idx, gates = _out_proj(1, x, mod1, sid_tok, p["l1_norm2"], [oc], p["l1_w_out"].astype(BF16),
                                   p["l1_w_router"].T, br1)
    return _moe(x1, h2, idx, gates, mod1, sid_move, p["l1_w_gate_up"], p["l1_b_gate_up"], p["l1_w_down"],
                p["l1_b_down"], [a.shape[0] for a in x_parts])


def kernel(x_prompt, x_sample, c_prompt, c_sample, l0_ada_w, l0_ada_b, l0_norm1, l0_w_in, l0_q_norm_a, l0_k_norm_a, l0_sink_a, l0_q_norm_b, l0_k_norm_b, l0_w_out, l0_norm2, l0_w_router, l0_b_router, l0_w_gate_up, l0_b_gate_up, l0_w_down, l0_b_down, l1_ada_w, l1_ada_b, l1_norm1, l1_w_in, l1_q_norm_c, l1_k_norm_c, l1_rpb_c, l1_w_out, l1_norm2, l1_w_router, l1_b_router, l1_w_gate_up, l1_b_gate_up, l1_w_down, l1_b_down):
    p = dict(l0_ada_w=l0_ada_w, l0_ada_b=l0_ada_b, l0_norm1=l0_norm1, l0_w_in=l0_w_in, l0_q_norm_a=l0_q_norm_a,
             l0_k_norm_a=l0_k_norm_a, l0_sink_a=l0_sink_a, l0_q_norm_b=l0_q_norm_b, l0_k_norm_b=l0_k_norm_b,
             l0_w_out=l0_w_out, l0_norm2=l0_norm2, l0_w_router=l0_w_router, l0_b_router=l0_b_router,
             l0_w_gate_up=l0_w_gate_up, l0_b_gate_up=l0_b_gate_up, l0_w_down=l0_w_down, l0_b_down=l0_b_down,
             l1_ada_w=l1_ada_w, l1_ada_b=l1_ada_b, l1_norm1=l1_norm1, l1_w_in=l1_w_in, l1_q_norm_c=l1_q_norm_c,
             l1_k_norm_c=l1_k_norm_c, l1_rpb_c=l1_rpb_c, l1_w_out=l1_w_out, l1_norm2=l1_norm2,
             l1_w_router=l1_w_router, l1_b_router=l1_b_router, l1_w_gate_up=l1_w_gate_up,
             l1_b_gate_up=l1_b_gate_up, l1_w_down=l1_w_down, l1_b_down=l1_b_down)
    nb_p, len_p, d = x_prompt.shape
    nb_s, len_s, _ = x_sample.shape
    nseq = nb_p + nb_s
    c_pad = jnp.concatenate([c_prompt, c_sample, jnp.zeros((-nseq % 8, d), F32)], axis=0)
    mods = tuple(_ada_mod(c_pad, p[l + "_ada_w"], p[l + "_ada_b"]).reshape(c_pad.shape[0], 6, D_MODEL)
                 for l in ("l0", "l1"))
    groups = sorted(((x_prompt, 0, 0), (x_sample, nb_p, 1)), key=lambda g: -g[0].shape[0] * g[0].shape[1])
    outs = [None, None]
    for xg, row0, slot in groups:
        nb, ln, _ = xg.shape
        (y,) = _trunk([xg.reshape(nb * ln, d)], mods, row0, (ln,) * nb, p)
        outs[slot] = y.reshape(nb, ln, d)
    return tuple(outs)
```
